```python
import math
import jax, jax.numpy as jnp
from jax import lax
import numpy as np

D_MODEL = 1024
BATCH = 8
SEQ = 8192
DEPTH = 2

MEM_LEN = 256
MIX_W = D_MODEL
GROUP_W = MIX_W // 4
N_FOX_HEADS = 4
N_SB_HEADS = 4
N_MLA_HEADS = 4
N_MEM_HEADS = 4
HEAD_DIM = GROUP_W // 4
MLA_Q_RANK = 256
MLA_KV_RANK = 128
MLA_NOPE = HEAD_DIM
MLA_ROPE = 32
MLA_V = GROUP_W // N_MLA_HEADS
ROPE_THETA = 10000.0
BLOCK_Q = 128
LN_EPS = 1e-5
RMS_EPS = 1e-6
FOX_FORGET_BIAS_INIT = 3.0
DEEPNORM_ALPHA = (2 * DEPTH) ** 0.25
DEEPNORM_BETA = (8 * DEPTH) ** -0.25
SPLIT_SIZES = (GROUP_W, GROUP_W, GROUP_W, N_FOX_HEADS,
               GROUP_W, GROUP_W, GROUP_W,
               MLA_Q_RANK, MLA_KV_RANK, MLA_ROPE,
               GROUP_W,
               MIX_W)
IN_COLS = sum(SPLIT_SIZES)

kernel_name = 'hybrid_fox_stickbreak_mla_memory_deepnorm'


def _layer_norm(x, g, b):
    xf = x.astype(jnp.float32)
    mu = jnp.mean(xf, axis=-1, keepdims=True)
    var = jnp.mean(jnp.square(xf - mu), axis=-1, keepdims=True)
    y = (xf - mu) * lax.rsqrt(var + LN_EPS) * g.astype(jnp.float32) + b.astype(jnp.float32)
    return y.astype(x.dtype)


def _rms_norm(x, g):
    xf = x.astype(jnp.float32)
    y = xf * lax.rsqrt(jnp.mean(jnp.square(xf), axis=-1, keepdims=True) + RMS_EPS)
    return (y * g.astype(jnp.float32)).astype(x.dtype)


def _heads(t, n):
    b, s, _ = t.shape
    return t.reshape(b, s, n, -1).transpose(0, 2, 1, 3)


def _merge(t):
    b, h, s, d = t.shape
    return t.transpose(0, 2, 1, 3).reshape(b, s, h * d)


def _rope(x, positions):
    half = x.shape[-1] // 2
    inv_freq = ROPE_THETA ** (-jnp.arange(half, dtype=jnp.float32) / half)
    ang = positions.astype(jnp.float32)[:, None] * inv_freq[None, :]
    ang = ang.reshape((ang.shape[0],) + (1,) * (x.ndim - 3) + (half,))
    cos, sin = jnp.cos(ang), jnp.sin(ang)
    xf = x.astype(jnp.float32)
    x1, x2 = xf[..., :half], xf[..., half:]
    return jnp.concatenate([x1 * cos - x2 * sin, x1 * sin + x2 * cos], axis=-1).astype(x.dtype)


def _sweep_query_blocks(block_fn, per_query):
    b, h, s = per_query[0].shape[:3]
    nb = s // BLOCK_Q
    blocks = tuple(jnp.moveaxis(a.reshape((b, h, nb, BLOCK_Q) + a.shape[3:]), 2, 0) for a in per_query)
    out = lax.map(lambda args: block_fn(args[0], *args[1:]), (jnp.arange(nb),) + blocks)
    out = jnp.moveaxis(out, 0, 2)
    return out.reshape(b, h, s, out.shape[-1])


def _causal_softmax_attention(q, k, v, scale, log_forget_cum=None):
    key_pos = jnp.arange(k.shape[2])

    def block(i, qb, *fq):
        q_pos = i * BLOCK_Q + jnp.arange(BLOCK_Q)
        logits = jnp.einsum('bhqd,bhkd->bhqk', qb, k).astype(jnp.float32) * scale
        if log_forget_cum is not None:
            logits = logits + fq[0][..., :, None] - log_forget_cum[:, :, None, :]
        mask = key_pos[None, :] <= q_pos[:, None]
        probs = jax.nn.softmax(jnp.where(mask, logits, -jnp.inf), axis=-1)
        return jnp.einsum('bhqk,bhkd->bhqd', probs.astype(v.dtype), v)

    per_query = (q,) if log_forget_cum is None else (q, log_forget_cum)
    return _sweep_query_blocks(block, per_query)


def _stick_breaking_attention(q, k, v, scale):
    key_pos = jnp.arange(k.shape[2])

    def block(i, qb):
        q_pos = i * BLOCK_Q + jnp.arange(BLOCK_Q)
        valid = key_pos[None, :] < q_pos[:, None]
        z = jnp.einsum('bhqd,bhkd->bhqk', qb, k).astype(jnp.float32) * scale
        log_keep = jnp.where(valid, jax.nn.log_sigmoid(-z), 0.0)
        log_tail = lax.cumsum(log_keep, axis=3, reverse=True) - log_keep
        w = jnp.where(valid, jnp.exp(jax.nn.log_sigmoid(z) + log_tail), 0.0)
        return jnp.einsum('bhqk,bhkd->bhqd', w.astype(v.dtype), v)

    return _sweep_query_blocks(block, (q,))


def _fwd_setup_inputs(seed: int = 0) -> dict:
    key = jax.random.key(seed)
    ks = jax.random.split(key, 18)
    f32 = jnp.float32
    nrm = lambda k, shape: jax.random.normal(k, shape, f32)
    return {
        'x': nrm(ks[0], (BATCH, SEQ, D_MODEL)),
        'mem': nrm(ks[1], (BATCH, MEM_LEN, D_MODEL)),
        'ln_in_g': 1.0 + 0.02 * nrm(ks[2], (D_MODEL,)),
        'ln_in_b': 0.02 * nrm(ks[3], (D_MODEL,)),
        'mem_ln_g': 1.0 + 0.02 * nrm(ks[4], (D_MODEL,)),
        'mem_ln_b': 0.02 * nrm(ks[5], (D_MODEL,)),
        'w_in': nrm(ks[6], (DEPTH, D_MODEL, IN_COLS)) * D_MODEL ** -0.5,
        'b_forget': FOX_FORGET_BIAS_INIT + 0.1 * nrm(ks[7], (DEPTH, N_FOX_HEADS)),
        'mla_q_norm_g': 1.0 + 0.02 * nrm(ks[8], (DEPTH, MLA_Q_RANK)),
        'w_mla_q_up': nrm(ks[9], (DEPTH, MLA_Q_RANK, N_MLA_HEADS * (MLA_NOPE + MLA_ROPE))) * MLA_Q_RANK ** -0.5,
        'mla_kv_norm_g': 1.0 + 0.02 * nrm(ks[10], (DEPTH, MLA_KV_RANK)),
        'w_mla_kv_up': nrm(ks[11], (DEPTH, MLA_KV_RANK, N_MLA_HEADS * (MLA_NOPE + MLA_V))) * MLA_KV_RANK ** -0.5,
        'w_mem_kv': nrm(ks[12], (DEPTH, D_MODEL, 2 * GROUP_W)) * D_MODEL ** -0.5,
        'w_out': nrm(ks[13], (DEPTH, MIX_W, D_MODEL)) * (MIX_W ** -0.5 * DEEPNORM_BETA),
        'ln_g': 1.0 + 0.02 * nrm(ks[14], (DEPTH, D_MODEL)),
        'ln_b': 0.02 * nrm(ks[15], (DEPTH, D_MODEL)),
    }


def _fwd_reference(x, mem, ln_in_g, ln_in_b, mem_ln_g, mem_ln_b, w_in, b_forget,
              mla_q_norm_g, w_mla_q_up, mla_kv_norm_g, w_mla_kv_up, w_mem_kv,
              w_out, ln_g, ln_b):
    b, s, _ = x.shape
    positions = jnp.arange(s)
    offsets = [int(o) for o in np.cumsum(SPLIT_SIZES)[:-1]]
    head_scale = HEAD_DIM ** -0.5
    mla_scale = (MLA_NOPE + MLA_ROPE) ** -0.5

    h_res = _layer_norm(x, ln_in_g, ln_in_b)
    mem_n = _layer_norm(mem, mem_ln_g, mem_ln_b)

    for l in range(DEPTH):
        proj = jnp.einsum('bsd,dc->bsc', h_res, w_in[l])
        (fq, fk, fv, f_logit, sq, sk, sv, c_q, c_kv, k_rot, mq, gate) = jnp.split(proj, offsets, axis=-1)

        log_f = jax.nn.log_sigmoid((f_logit + b_forget[l]).astype(jnp.float32))
        f_cum = jnp.cumsum(log_f, axis=1).transpose(0, 2, 1)
        out_fox = _causal_softmax_attention(_heads(fq, N_FOX_HEADS), _heads(fk, N_FOX_HEADS),
                                            _heads(fv, N_FOX_HEADS), head_scale, f_cum)

        out_sb = _stick_breaking_attention(_heads(sq, N_SB_HEADS), _heads(sk, N_SB_HEADS),
                                           _heads(sv, N_SB_HEADS), head_scale)

        q_mla = jnp.einsum('bsr,rc->bsc', _rms_norm(c_q, mla_q_norm_g[l]), w_mla_q_up[l])
        q_mla = q_mla.reshape(b, s, N_MLA_HEADS, MLA_NOPE + MLA_ROPE)
        q_full = jnp.concatenate([q_mla[..., :MLA_NOPE], _rope(q_mla[..., MLA_NOPE:], positions)], axis=-1)
        kv_mla = jnp.einsum('bsr,rc->bsc', _rms_norm(c_kv, mla_kv_norm_g[l]), w_mla_kv_up[l])
        kv_mla = kv_mla.reshape(b, s, N_MLA_HEADS, MLA_NOPE + MLA_V)
        k_rope = jnp.broadcast_to(_rope(k_rot, positions)[:, :, None, :], (b, s, N_MLA_HEADS, MLA_ROPE))
        k_full = jnp.concatenate([kv_mla[..., :MLA_NOPE], k_rope], axis=-1)
        v_mla = kv_mla[..., MLA_NOPE:]
        out_mla = _causal_softmax_attention(q_full.transpose(0, 2, 1, 3), k_full.transpose(0, 2, 1, 3),
                                            v_mla.transpose(0, 2, 1, 3), mla_scale)

        mkv = jnp.einsum('bmd,dc->bmc', mem_n, w_mem_kv[l])
        mk, mv = _heads(mkv[..., :GROUP_W], N_MEM_HEADS), _heads(mkv[..., GROUP_W:], N_MEM_HEADS)
        mem_logits = jnp.einsum('bhsd,bhmd->bhsm', _heads(mq, N_MEM_HEADS), mk).astype(jnp.float32) * head_scale
        mem_p = jax.nn.softmax(mem_logits, axis=-1)
        out_mem = jnp.einsum('bhsm,bhmd->bhsd', mem_p.astype(mv.dtype), mv)

        mixed = jnp.concatenate([_merge(out_fox), _merge(out_sb), _merge(out_mla), _merge(out_mem)], axis=-1)
        y = jnp.einsum('bsc,cd->bsd', mixed * jax.nn.silu(gate), w_out[l])

        h_res = _layer_norm(DEEPNORM_ALPHA * h_res + y, ln_g[l], ln_b[l])

    return h_res


import jax as _jax
import jax.numpy as _jnp

TWIN_FORMAT = 'train_step'
FWD_PARAMS = ['x', 'mem', 'ln_in_g', 'ln_in_b', 'mem_ln_g', 'mem_ln_b', 'w_in', 'b_forget', 'mla_q_norm_g', 'w_mla_q_up', 'mla_kv_norm_g', 'w_mla_kv_up', 'w_mem_kv', 'w_out', 'ln_g', 'ln_b']
TWIN_WEIGHTS = ['ln_in_g', 'ln_in_b', 'mem_ln_g', 'mem_ln_b', 'w_in', 'b_forget', 'mla_q_norm_g', 'w_mla_q_up', 'mla_kv_norm_g', 'w_mla_kv_up', 'w_mem_kv', 'w_out', 'ln_g', 'ln_b']
TWIN_DIFF_INPUT = 'x'
TWIN_INPUTS = ['x', 'mem', 'ln_in_g', 'ln_in_b', 'mem_ln_g', 'mem_ln_b', 'w_in', 'b_forget', 'mla_q_norm_g', 'w_mla_q_up', 'mla_kv_norm_g', 'w_mla_kv_up', 'w_mem_kv', 'w_out', 'ln_g', 'ln_b', 'loss_target', 'm_ln_in_g', 'm_ln_in_b', 'm_mem_ln_g', 'm_mem_ln_b', 'm_w_in', 'm_b_forget', 'm_mla_q_norm_g', 'm_w_mla_q_up', 'm_mla_kv_norm_g', 'm_w_mla_kv_up', 'm_w_mem_kv', 'm_w_out', 'm_ln_g', 'm_ln_b', 'v_ln_in_g', 'v_ln_in_b', 'v_mem_ln_g', 'v_mem_ln_b', 'v_w_in', 'v_b_forget', 'v_mla_q_norm_g', 'v_w_mla_q_up', 'v_mla_kv_norm_g', 'v_w_mla_kv_up', 'v_w_mem_kv', 'v_w_out', 'v_ln_g', 'v_ln_b']
TWIN_OUTPUTS = ['loss', 'grad_x', 'grad_ln_in_g', 'grad_ln_in_b', 'grad_mem_ln_g', 'grad_mem_ln_b', 'grad_w_in', 'grad_b_forget', 'grad_mla_q_norm_g', 'grad_w_mla_q_up', 'grad_mla_kv_norm_g', 'grad_w_mla_kv_up', 'grad_w_mem_kv', 'grad_w_out', 'grad_ln_g', 'grad_ln_b', 'delta_ln_in_g', 'delta_ln_in_b', 'delta_mem_ln_g', 'delta_mem_ln_b', 'delta_w_in', 'delta_b_forget', 'delta_mla_q_norm_g', 'delta_w_mla_q_up', 'delta_mla_kv_norm_g', 'delta_w_mla_kv_up', 'delta_w_mem_kv', 'delta_w_out', 'delta_ln_g', 'delta_ln_b', 'new_m_ln_in_g', 'new_m_ln_in_b', 'new_m_mem_ln_g', 'new_m_mem_ln_b', 'new_m_w_in', 'new_m_b_forget', 'new_m_mla_q_norm_g', 'new_m_w_mla_q_up', 'new_m_mla_kv_norm_g', 'new_m_w_mla_kv_up', 'new_m_w_mem_kv', 'new_m_w_out', 'new_m_ln_g', 'new_m_ln_b', 'new_v_ln_in_g', 'new_v_ln_in_b', 'new_v_mem_ln_g', 'new_v_mem_ln_b', 'new_v_w_in', 'new_v_b_forget', 'new_v_mla_q_norm_g', 'new_v_w_mla_q_up', 'new_v_mla_kv_norm_g', 'new_v_w_mla_kv_up', 'new_v_w_mem_kv', 'new_v_w_out', 'new_v_ln_g', 'new_v_ln_b']
TWIN_LEAF_KINDS = {'loss': 'loss', 'grad_x': 'grad_x', 'grad_ln_in_g': 'grad_w', 'grad_ln_in_b': 'grad_w', 'grad_mem_ln_g': 'grad_w', 'grad_mem_ln_b': 'grad_w', 'grad_w_in': 'grad_w', 'grad_b_forget': 'grad_w', 'grad_mla_q_norm_g': 'grad_w', 'grad_w_mla_q_up': 'grad_w', 'grad_mla_kv_norm_g': 'grad_w', 'grad_w_mla_kv_up': 'grad_w', 'grad_w_mem_kv': 'grad_w', 'grad_w_out': 'grad_w', 'grad_ln_g': 'grad_w', 'grad_ln_b': 'grad_w', 'delta_ln_in_g': 'delta_w', 'delta_ln_in_b': 'delta_w', 'delta_mem_ln_g': 'delta_w', 'delta_mem_ln_b': 'delta_w', 'delta_w_in': 'delta_w', 'delta_b_forget': 'delta_w', 'delta_mla_q_norm_g': 'delta_w', 'delta_w_mla_q_up': 'delta_w', 'delta_mla_kv_norm_g': 'delta_w', 'delta_w_mla_kv_up': 'delta_w', 'delta_w_mem_kv': 'delta_w', 'delta_w_out': 'delta_w', 'delta_ln_g': 'delta_w', 'delta_ln_b': 'delta_w', 'new_m_ln_in_g': 'new_m', 'new_m_ln_in_b': 'new_m', 'new_m_mem_ln_g': 'new_m', 'new_m_mem_ln_b': 'new_m', 'new_m_w_in': 'new_m', 'new_m_b_forget': 'new_m', 'new_m_mla_q_norm_g': 'new_m', 'new_m_w_mla_q_up': 'new_m', 'new_m_mla_kv_norm_g': 'new_m', 'new_m_w_mla_kv_up': 'new_m', 'new_m_w_mem_kv': 'new_m', 'new_m_w_out': 'new_m', 'new_m_ln_g': 'new_m', 'new_m_ln_b': 'new_m', 'new_v_ln_in_g': 'new_v', 'new_v_ln_in_b': 'new_v', 'new_v_mem_ln_g': 'new_v', 'new_v_mem_ln_b': 'new_v', 'new_v_w_in': 'new_v', 'new_v_b_forget': 'new_v', 'new_v_mla_q_norm_g': 'new_v', 'new_v_w_mla_q_up': 'new_v', 'new_v_mla_kv_norm_g': 'new_v', 'new_v_w_mla_kv_up': 'new_v', 'new_v_w_mem_kv': 'new_v', 'new_v_w_out': 'new_v', 'new_v_ln_g': 'new_v', 'new_v_ln_b': 'new_v'}


def _forward(args):
    return _fwd_reference(*[args[k] for k in FWD_PARAMS])


def _output_shape():
    def fwd():
        inp = _fwd_setup_inputs(0)
        return _fwd_reference(*[inp[k] for k in FWD_PARAMS])
    out = _jax.eval_shape(fwd)
    return out.shape, out.dtype

N_MICROBATCH = 1
ADAM_LR = 0.001
ADAM_B1 = 0.9
ADAM_B2 = 0.999
ADAM_EPS = 1e-08
ADAM_WD = 0.01
ADAM_STEP = 10
PER_EXAMPLE_BATCH_AXIS = {'x': 0, 'mem': 0, 'loss_target': 0}
SHARED_INPUTS = []
_WEIGHT_DTYPES = {'ln_in_g': _jnp.float32, 'ln_in_b': _jnp.float32, 'mem_ln_g': _jnp.float32, 'mem_ln_b': _jnp.float32, 'w_in': _jnp.float32, 'b_forget': _jnp.float32, 'mla_q_norm_g': _jnp.float32, 'w_mla_q_up': _jnp.float32, 'mla_kv_norm_g': _jnp.float32, 'w_mla_kv_up': _jnp.float32, 'w_mem_kv': _jnp.float32, 'w_out': _jnp.float32, 'ln_g': _jnp.float32, 'ln_b': _jnp.float32}
MOMENT_SCALE = {'ln_in_g': 2.355015e+00, 'ln_in_b': 9.909560e-01, 'mem_ln_g': 6.545368e-03, 'mem_ln_b': 5.318467e-02, 'w_in': 1.860584e-02, 'b_forget': 7.013646e-02, 'mla_q_norm_g': 1.048534e-02, 'w_mla_q_up': 8.087838e-03, 'mla_kv_norm_g': 2.280350e-02, 'w_mla_kv_up': 1.023816e-02, 'w_mem_kv': 5.740315e-03, 'w_out': 4.000417e-02, 'ln_g': 4.528938e+01, 'ln_b': 1.471242e+00}


def _to_microbatches(a, axis):
    t = _jnp.moveaxis(a, axis, 0)
    t = t.reshape((N_MICROBATCH, t.shape[0] // N_MICROBATCH) + t.shape[1:])
    return _jnp.moveaxis(t, 1, axis + 1)


def setup_inputs(seed: int = 0) -> dict:
    inp = _fwd_setup_inputs(seed)
    key = _jax.random.fold_in(_jax.random.key(seed), 7919)
    shape, _ = _output_shape()
    out = dict(inp)
    out["loss_target"] = _jax.random.normal(_jax.random.fold_in(key, 0), shape, _jnp.float32)
    for i, name in enumerate(TWIN_WEIGHTS):
        w = inp[name].astype(_jnp.float32)
        if MOMENT_SCALE is None:
            s = _jnp.sqrt(_jnp.mean(_jnp.square(w)) + 1e-30)
        else:
            s = MOMENT_SCALE[name]
        km, kv = _jax.random.split(_jax.random.fold_in(key, i + 1))
        out[name] = w
        out["m_" + name] = s * _jax.random.normal(km, w.shape, _jnp.float32)
        out["v_" + name] = (s * s) * _jax.random.uniform(kv, w.shape, _jnp.float32, 0.5, 1.5)
    if N_MICROBATCH > 1:
        for name, axis in PER_EXAMPLE_BATCH_AXIS.items():
            out[name] = _to_microbatches(out[name], axis)
    return {'x': out['x'], 'mem': out['mem'], 'ln_in_g': out['ln_in_g'], 'ln_in_b': out['ln_in_b'], 'mem_ln_g': out['mem_ln_g'], 'mem_ln_b': out['mem_ln_b'], 'w_in': out['w_in'], 'b_forget': out['b_forget'], 'mla_q_norm_g': out['mla_q_norm_g'], 'w_mla_q_up': out['w_mla_q_up'], 'mla_kv_norm_g': out['mla_kv_norm_g'], 'w_mla_kv_up': out['w_mla_kv_up'], 'w_mem_kv': out['w_mem_kv'], 'w_out': out['w_out'], 'ln_g': out['ln_g'], 'ln_b': out['ln_b'], 'loss_target': out['loss_target'], 'm_ln_in_g': out['m_ln_in_g'], 'm_ln_in_b': out['m_ln_in_b'], 'm_mem_ln_g': out['m_mem_ln_g'], 'm_mem_ln_b': out['m_mem_ln_b'], 'm_w_in': out['m_w_in'], 'm_b_forget': out['m_b_forget'], 'm_mla_q_norm_g': out['m_mla_q_norm_g'], 'm_w_mla_q_up': out['m_w_mla_q_up'], 'm_mla_kv_norm_g': out['m_mla_kv_norm_g'], 'm_w_mla_kv_up': out['m_w_mla_kv_up'], 'm_w_mem_kv': out['m_w_mem_kv'], 'm_w_out': out['m_w_out'], 'm_ln_g': out['m_ln_g'], 'm_ln_b': out['m_ln_b'], 'v_ln_in_g': out['v_ln_in_g'], 'v_ln_in_b': out['v_ln_in_b'], 'v_mem_ln_g': out['v_mem_ln_g'], 'v_mem_ln_b': out['v_mem_ln_b'], 'v_w_in': out['v_w_in'], 'v_b_forget': out['v_b_forget'], 'v_mla_q_norm_g': out['v_mla_q_norm_g'], 'v_w_mla_q_up': out['v_w_mla_q_up'], 'v_mla_kv_norm_g': out['v_mla_kv_norm_g'], 'v_w_mla_kv_up': out['v_w_mla_kv_up'], 'v_w_mem_kv': out['v_w_mem_kv'], 'v_w_out': out['v_w_out'], 'v_ln_g': out['v_ln_g'], 'v_ln_b': out['v_ln_b']}


def _loss(weights, diff, rest, loss_target):
    with _jax.named_scope("forward"):
        args = {**rest, TWIN_DIFF_INPUT: diff, **{k: w.astype(_WEIGHT_DTYPES[k]) for k, w in weights.items()}}
        y = _forward(args)
    with _jax.named_scope("loss_head"):
        err = _jnp.square(y.astype(_jnp.float32) - loss_target)
        return 0.5 * _jnp.sum(_jnp.mean(err, axis=-1)) if err.ndim else 0.5 * err


def _adamw(w, g, m, v):
    m = ADAM_B1 * m + (1.0 - ADAM_B1) * g
    v = ADAM_B2 * v + (1.0 - ADAM_B2) * _jnp.square(g)
    m_hat = m / (1.0 - ADAM_B1 ** ADAM_STEP)
    v_hat = v / (1.0 - ADAM_B2 ** ADAM_STEP)
    delta = -ADAM_LR * (m_hat / (_jnp.sqrt(v_hat) + ADAM_EPS) + ADAM_WD * w)
    return delta, m, v


def reference(x, mem, ln_in_g, ln_in_b, mem_ln_g, mem_ln_b, w_in, b_forget, mla_q_norm_g, w_mla_q_up, mla_kv_norm_g, w_mla_kv_up, w_mem_kv, w_out, ln_g, ln_b, loss_target, m_ln_in_g, m_ln_in_b, m_mem_ln_g, m_mem_ln_b, m_w_in, m_b_forget, m_mla_q_norm_g, m_w_mla_q_up, m_mla_kv_norm_g, m_w_mla_kv_up, m_w_mem_kv, m_w_out, m_ln_g, m_ln_b, v_ln_in_g, v_ln_in_b, v_mem_ln_g, v_mem_ln_b, v_w_in, v_b_forget, v_mla_q_norm_g, v_w_mla_q_up, v_mla_kv_norm_g, v_w_mla_kv_up, v_w_mem_kv, v_w_out, v_ln_g, v_ln_b):
    given = dict(x=x, mem=mem, ln_in_g=ln_in_g, ln_in_b=ln_in_b, mem_ln_g=mem_ln_g, mem_ln_b=mem_ln_b, w_in=w_in, b_forget=b_forget, mla_q_norm_g=mla_q_norm_g, w_mla_q_up=w_mla_q_up, mla_kv_norm_g=mla_kv_norm_g, w_mla_kv_up=w_mla_kv_up, w_mem_kv=w_mem_kv, w_out=w_out, ln_g=ln_g, ln_b=ln_b, loss_target=loss_target, m_ln_in_g=m_ln_in_g, m_ln_in_b=m_ln_in_b, m_mem_ln_g=m_mem_ln_g, m_mem_ln_b=m_mem_ln_b, m_w_in=m_w_in, m_b_forget=m_b_forget, m_mla_q_norm_g=m_mla_q_norm_g, m_w_mla_q_up=m_w_mla_q_up, m_mla_kv_norm_g=m_mla_kv_norm_g, m_w_mla_kv_up=m_w_mla_kv_up, m_w_mem_kv=m_w_mem_kv, m_w_out=m_w_out, m_ln_g=m_ln_g, m_ln_b=m_ln_b, v_ln_in_g=v_ln_in_g, v_ln_in_b=v_ln_in_b, v_mem_ln_g=v_mem_ln_g, v_mem_ln_b=v_mem_ln_b, v_w_in=v_w_in, v_b_forget=v_b_forget, v_mla_q_norm_g=v_mla_q_norm_g, v_w_mla_q_up=v_w_mla_q_up, v_mla_kv_norm_g=v_mla_kv_norm_g, v_w_mla_kv_up=v_w_mla_kv_up, v_w_mem_kv=v_w_mem_kv, v_w_out=v_w_out, v_ln_g=v_ln_g, v_ln_b=v_ln_b)
    weights = {n: given[n] for n in TWIN_WEIGHTS}
    shared = {n: given[n] for n in SHARED_INPUTS}
    per_example = {n: given[n] for n in ['x', 'mem']}
    grad_fn = _jax.value_and_grad(_loss, argnums=(0, 1))

    def one_microbatch(ex, loss_target):
        ex = dict(ex)
        diff = ex.pop(TWIN_DIFF_INPUT)
        return grad_fn(weights, diff, {**shared, **ex}, loss_target)

    if N_MICROBATCH == 1:
        loss, (grad_w, grad_x) = one_microbatch(per_example, given["loss_target"])
    else:
        def body(carry, xs):
            loss_sum, grad_sum = carry
            l_k, (gw_k, gx_k) = one_microbatch(xs[0], xs[1])
            with _jax.named_scope("update"):
                return (loss_sum + l_k, _jax.tree.map(_jnp.add, grad_sum, gw_k)), gx_k

        init = (_jnp.zeros((), _jnp.float32), _jax.tree.map(_jnp.zeros_like, weights))
        (loss, grad_w), grad_x = _jax.lax.scan(body, init, (per_example, given["loss_target"]))
    with _jax.named_scope("update"):
        delta_w, new_m, new_v = {}, {}, {}
        for n in TWIN_WEIGHTS:
            delta_w[n], new_m[n], new_v[n] = _adamw(weights[n], grad_w[n], given["m_" + n], given["v_" + n])
    return (loss, grad_x, *[grad_w[n] for n in TWIN_WEIGHTS], *[delta_w[n] for n in TWIN_WEIGHTS],
            *[new_m[n] for n in TWIN_WEIGHTS], *[new_v[n] for n in TWIN_WEIGHTS])
```

```python
import functools
import math

import jax
import jax.numpy as jnp
from jax import lax
from jax.experimental import pallas as pl
from jax.experimental.pallas import tpu as pltpu

F32 = jnp.float32
BF16 = jnp.bfloat16

D_MODEL = 1024
DEPTH = 2
GROUP_W = 256
N_HEADS = 4
HEAD_DIM = 64
MLA_Q_RANK = 256
MLA_KV_RANK = 128
MLA_NOPE = 64
MLA_ROPE = 32
MLA_V = 64
ROPE_THETA = 10000.0
LN_EPS = 1e-5
RMS_EPS = 1e-6
DEEPNORM_ALPHA = (2 * DEPTH) ** 0.25
SPLIT_SIZES = (256, 256, 256, 4, 256, 256, 256, 256, 128, 32, 256, 1024)
IN_COLS = sum(SPLIT_SIZES)
_ORIG_OFF = [sum(SPLIT_SIZES[:i]) for i in range(len(SPLIT_SIZES))]
_PERM = (("fq", 0), ("fk", 1), ("fv", 2), ("sq", 4), ("sk", 5), ("sv", 6), ("c_q", 7), ("c_kv", 8),
         ("mq", 10), ("gate", 11), ("k_rot", 9), ("f_logit", 3))
LANE = 128
PROJ_COLS = ((IN_COLS + LANE - 1) // LANE) * LANE

ADAM_LR = 0.001
ADAM_B1 = 0.9
ADAM_B2 = 0.999
ADAM_EPS = 1e-08
ADAM_WD = 0.01
ADAM_STEP = 10

N_DEV = 8
MESH_AXES = ("x", "y", "c")
VMEM_LIMIT = 48 * 1024 * 1024
ATTN_BLOCK = 256
NEG_BIG = -1e30

_NT = (((1,), (1,)), ((), ()))
_TN = (((0,), (0,)), ((), ()))
_NN = (((1,), (0,)), ((), ()))


def _cp(sem):
    return pltpu.CompilerParams(dimension_semantics=sem, vmem_limit_bytes=VMEM_LIMIT)


def _dot(a, b, dims=_NN):
    return lax.dot_general(a, b, dims, preferred_element_type=F32)


def _pick(n, cands):
    for c in cands:
        if c <= n and n % c == 0:
            return c
    return n


def _matmul(a, b, mode, name, tm=None, tn=None, tk=None):
    if mode == "nn":
        (M, K), (K2, N) = a.shape, b.shape
    elif mode == "nt":
        (M, K), (N, K2) = a.shape, b.shape
    else:
        (K, M), (K2, N) = a.shape, b.shape
    assert K == K2, (a.shape, b.shape, mode)
    tm = tm or _pick(M, (1024, 512, 256, 128, 64, 32, 16, 8))
    tn = tn or (N if N <= 1024 else _pick(N, (1024, 896, 768, 640, 512, 384, 256, 128)))
    tk = tk or (K if K <= 1024 else _pick(K, (512, 256, 128)))
    assert M % tm == 0 and N % tn == 0 and K % tk == 0, (M, N, K, tm, tn, tk)
    nk = K // tk
    dims = {"nn": _NN, "nt": _NT, "tn": _TN}[mode]

    def body(a_ref, b_ref, o_ref, acc_ref):
        part = _dot(a_ref[...].astype(BF16), b_ref[...].astype(BF16), dims)
        if nk == 1:
            o_ref[...] = part
        else:
            k = pl.program_id(2)

            @pl.when(k == 0)
            def _():
                acc_ref[...] = part

            @pl.when(k > 0)
            def _():
                acc_ref[...] += part

            @pl.when(k == nk - 1)
            def _():
                o_ref[...] = acc_ref[...]

    if mode == "nn":
        a_spec = pl.BlockSpec((tm, tk), lambda j, i, k: (i, k))
        b_spec = pl.BlockSpec((tk, tn), lambda j, i, k: (k, j))
    elif mode == "nt":
        a_spec = pl.BlockSpec((tm, tk), lambda j, i, k: (i, k))
        b_spec = pl.BlockSpec((tn, tk), lambda j, i, k: (j, k))
    else:
        a_spec = pl.BlockSpec((tk, tm), lambda j, i, k: (k, i))
        b_spec = pl.BlockSpec((tk, tn), lambda j, i, k: (k, j))
    acc_shape = (tm, tn) if nk > 1 else (8, 128)
    return pl.pallas_call(
        body, name=name, grid=(N // tn, M // tm, nk),
        in_specs=[a_spec, b_spec],
        out_specs=pl.BlockSpec((tm, tn), lambda j, i, k: (i, j)),
        out_shape=jax.ShapeDtypeStruct((M, N), F32),
        scratch_shapes=[pltpu.VMEM(acc_shape, F32)],
        compiler_params=_cp(("parallel", "parallel", "arbitrary")),
    )(a, b)


def _make_mm(name):
    @jax.custom_vjp
    def mm(a, w):
        return _matmul(a, w, "nn", name + "_fwd")

    def fwd(a, w):
        return mm(a, w), (a, w)

    def bwd(res, dy):
        a, w = res
        da = _matmul(dy, w, "nt", name + "_dx")
        dw = _matmul(a, dy, "tn", name + "_dw")
        return da, dw

    mm.defvjp(fwd, bwd)
    return mm


def _row_tile(rows):
    return _pick(rows, (512, 256, 128, 64, 32, 16, 8))


def _ln_stats(u):
    mu = jnp.mean(u, axis=-1, keepdims=True)
    d = u - mu
    var = jnp.mean(d * d, axis=-1, keepdims=True)
    return d, lax.rsqrt(var + LN_EPS)


def _ln_fwd_call(x, res, g, b, name):
    rows, dm = x.shape
    tr = _row_tile(rows)
    has_res = res is not None

    def body(*refs):
        if has_res:
            x_ref, r_ref, g_ref, b_ref, o_ref = refs
            u = DEEPNORM_ALPHA * r_ref[...] + x_ref[...]
        else:
            x_ref, g_ref, b_ref, o_ref = refs
            u = x_ref[...]
        d, rstd = _ln_stats(u)
        o_ref[...] = d * rstd * g_ref[...] + b_ref[...]

    row = pl.BlockSpec((tr, dm), lambda i: (i, 0))
    vec = pl.BlockSpec((1, dm), lambda i: (0, 0))
    args = (x, res) if has_res else (x,)
    return pl.pallas_call(
        body, name=name, grid=(rows // tr,),
        in_specs=[row] * len(args) + [vec, vec], out_specs=row,
        out_shape=jax.ShapeDtypeStruct((rows, dm), F32),
        compiler_params=_cp(("parallel",)),
    )(*args, g.reshape(1, dm), b.reshape(1, dm))


def _ln_bwd_call(dy, x, res, g, name):
    rows, dm = x.shape
    tr = _row_tile(rows)
    has_res = res is not None

    def body(*refs):
        if has_res:
            dy_ref, x_ref, r_ref, g_ref, dx_ref, dr_ref, dg_ref, db_ref = refs
            u = DEEPNORM_ALPHA * r_ref[...] + x_ref[...]
        else:
            dy_ref, x_ref, g_ref, dx_ref, dg_ref, db_ref = refs
            u = x_ref[...]
        i = pl.program_id(0)
        d, rstd = _ln_stats(u)
        xhat = d * rstd
        dyv = dy_ref[...]
        dxh = dyv * g_ref[...]
        m1 = jnp.mean(dxh, axis=-1, keepdims=True)
        m2 = jnp.mean(dxh * xhat, axis=-1, keepdims=True)
        du = rstd * (dxh - m1 - xhat * m2)
        dx_ref[...] = du
        if has_res:
            dr_ref[...] = DEEPNORM_ALPHA * du
        pg = jnp.sum(dyv * xhat, axis=0, keepdims=True)
        pb = jnp.sum(dyv, axis=0, keepdims=True)

        @pl.when(i == 0)
        def _():
            dg_ref[...] = pg
            db_ref[...] = pb

        @pl.when(i > 0)
        def _():
            dg_ref[...] += pg
            db_ref[...] += pb

    row = pl.BlockSpec((tr, dm), lambda i: (i, 0))
    vec = pl.BlockSpec((1, dm), lambda i: (0, 0))
    big = jax.ShapeDtypeStruct((rows, dm), F32)
    small = jax.ShapeDtypeStruct((1, dm), F32)
    args = (dy, x, res) if has_res else (dy, x)
    n_big = 2 if has_res else 1
    outs = pl.pallas_call(
        body, name=name, grid=(rows // tr,),
        in_specs=[row] * len(args) + [vec],
        out_specs=[row] * n_big + [vec, vec],
        out_shape=[big] * n_big + [small, small],
        compiler_params=_cp(("arbitrary",)),
    )(*args, g.reshape(1, dm))
    return outs


def _make_ln(name, has_res):
    if has_res:
        @jax.custom_vjp
        def ln(x, res, g, b):
            return _ln_fwd_call(x, res, g, b, name + "_fwd")

        def fwd(x, res, g, b):
            return ln(x, res, g, b), (x, res, g)

        def bwd(saved, dy):
            x, res, g = saved
            dx, dr, dg, db = _ln_bwd_call(dy, x, res, g, name + "_bwd")
            return dx, dr, dg.reshape(-1), db.reshape(-1)
    else:
        @jax.custom_vjp
        def ln(x, g, b):
            return _ln_fwd_call(x, None, g, b, name + "_fwd")

        def fwd(x, g, b):
            return ln(x, g, b), (x, g)

        def bwd(saved, dy):
            x, g = saved
            dx, dg, db = _ln_bwd_call(dy, x, None, g, name + "_bwd")
            return dx, dg.reshape(-1), db.reshape(-1)

    ln.defvjp(fwd, bwd)
    return ln


def _rms_fwd_call(x, g, name):
    rows, dm = x.shape
    tr = _row_tile(rows)

    def body(x_ref, g_ref, o_ref):
        xv = x_ref[...]
        rstd = lax.rsqrt(jnp.mean(xv * xv, axis=-1, keepdims=True) + RMS_EPS)
        o_ref[...] = xv * rstd * g_ref[...]

    row = pl.BlockSpec((tr, dm), lambda i: (i, 0))
    vec = pl.BlockSpec((1, dm), lambda i: (0, 0))
    return pl.pallas_call(
        body, name=name, grid=(rows // tr,), in_specs=[row, vec], out_specs=row,
        out_shape=jax.ShapeDtypeStruct((rows, dm), F32), compiler_params=_cp(("parallel",)),
    )(x, g.reshape(1, dm))


def _rms_bwd_call(dy, x, g, name):
    rows, dm = x.shape
    tr = _row_tile(rows)

    def body(dy_ref, x_ref, g_ref, dx_ref, dg_ref):
        i = pl.program_id(0)
        xv = x_ref[...]
        dyv = dy_ref[...]
        rstd = lax.rsqrt(jnp.mean(xv * xv, axis=-1, keepdims=True) + RMS_EPS)
        xhat = xv * rstd
        dxh = dyv * g_ref[...]
        m2 = jnp.mean(dxh * xhat, axis=-1, keepdims=True)
        dx_ref[...] = rstd * (dxh - xhat * m2)
        pg = jnp.sum(dyv * xhat, axis=0, keepdims=True)

        @pl.when(i == 0)
        def _():
            dg_ref[...] = pg

        @pl.when(i > 0)
        def _():
            dg_ref[...] += pg

    row = pl.BlockSpec((tr, dm), lambda i: (i, 0))
    vec = pl.BlockSpec((1, dm), lambda i: (0, 0))
    return pl.pallas_call(
        body, name=name, grid=(rows // tr,), in_specs=[row, row, vec], out_specs=[row, vec],
        out_shape=[jax.ShapeDtypeStruct((rows, dm), F32), jax.ShapeDtypeStruct((1, dm), F32)],
        compiler_params=_cp(("arbitrary",)),
    )(dy, x, g.reshape(1, dm))


def _make_rms(name):
    @jax.custom_vjp
    def rms(x, g):
        return _rms_fwd_call(x, g, name + "_fwd")

    def fwd(x, g):
        return rms(x, g), (x, g)

    def bwd(saved, dy):
        x, g = saved
        dx, dg = _rms_bwd_call(dy, x, g, name + "_bwd")
        return dx, dg.reshape(-1)

    rms.defvjp(fwd, bwd)
    return rms


def _sigmoid(x):
    return 1.0 / (1.0 + jnp.exp(-x))


def _gate_fwd_call(mixed, gate, name):
    rows, dm = mixed.shape
    tr = _row_tile(rows)

    def body(m_ref, g_ref, o_ref):
        gv = g_ref[...]
        o_ref[...] = m_ref[...] * (gv * _sigmoid(gv))

    row = pl.BlockSpec((tr, dm), lambda i: (i, 0))
    return pl.pallas_call(
        body, name=name, grid=(rows // tr,), in_specs=[row, row], out_specs=row,
        out_shape=jax.ShapeDtypeStruct((rows, dm), F32), compiler_params=_cp(("parallel",)),
    )(mixed, gate)


def _gate_bwd_call(dy, mixed, gate, name):
    rows, dm = mixed.shape
    tr = _row_tile(rows)

    def body(dy_ref, m_ref, g_ref, dm_ref, dg_ref):
        gv = g_ref[...]
        dyv = dy_ref[...]
        sg = _sigmoid(gv)
        dm_ref[...] = dyv * (gv * sg)
        dg_ref[...] = dyv * m_ref[...] * (sg * (1.0 + gv * (1.0 - sg)))

    row = pl.BlockSpec((tr, dm), lambda i: (i, 0))
    out = jax.ShapeDtypeStruct((rows, dm), F32)
    return pl.pallas_call(
        body, name=name, grid=(rows // tr,), in_specs=[row, row, row], out_specs=[row, row],
        out_shape=[out, out], compiler_params=_cp(("parallel",)),
    )(dy, mixed, gate)


def _make_gate(name):
    @jax.custom_vjp
    def gate_mul(mixed, gate):
        return _gate_fwd_call(mixed, gate, name + "_fwd")

    def fwd(mixed, gate):
        return gate_mul(mixed, gate), (mixed, gate)

    def bwd(saved, dy):
        mixed, gate = saved
        dmix, dgate = _gate_bwd_call(dy, mixed, gate, name + "_bwd")
        return dmix, dgate

    gate_mul.defvjp(fwd, bwd)
    return gate_mul


def _loss_call(y, t, name):
    rows, dm = y.shape
    tr = _row_tile(rows)

    def body(y_ref, t_ref, l_ref, d_ref):
        i = pl.program_id(0)
        e = y_ref[...] - t_ref[...]
        d_ref[...] = e * (1.0 / dm)
        part = 0.5 * jnp.sum(jnp.mean(e * e, axis=-1, keepdims=True), axis=0, keepdims=True)

        @pl.when(i == 0)
        def _():
            l_ref[...] = part

        @pl.when(i > 0)
        def _():
            l_ref[...] += part

    row = pl.BlockSpec((tr, dm), lambda i: (i, 0))
    one = pl.BlockSpec((1, 1), lambda i: (0, 0))
    return pl.pallas_call(
        body, name=name, grid=(rows // tr,), in_specs=[row, row], out_specs=[one, row],
        out_shape=[jax.ShapeDtypeStruct((1, 1), F32), jax.ShapeDtypeStruct((rows, dm), F32)],
        compiler_params=_cp(("arbitrary",)),
    )(y, t)


@jax.custom_vjp
def _loss_op(y, t):
    return _loss_call(y, t, "loss_head")[0][0, 0]


def _loss_fwd(y, t):
    l, d = _loss_call(y, t, "loss_head")
    return l[0, 0], d


def _loss_bwd(d, ct):
    return ct * d, jnp.zeros_like(d)


_loss_op.defvjp(_loss_fwd, _loss_bwd)


def _causal_mask(i, j, bq, bk, strict):
    row = i * bq + lax.broadcasted_iota(jnp.int32, (bq, bk), 0)
    col = j * bk + lax.broadcasted_iota(jnp.int32, (bq, bk), 1)
    return (col < row) if strict else (col <= row)


def _attn_blocks(S, Sk):
    bq = min(ATTN_BLOCK, S)
    bk = min(ATTN_BLOCK, Sk)
    assert S % bq == 0 and Sk % bk == 0
    return bq, bk


def _softmax_attn_fwd(q, k, v, fq, fk, scale, causal, name):
    H, S, dk = q.shape
    Sk, dv = k.shape[1], v.shape[2]
    bq, bk = _attn_blocks(S, Sk)
    nq, nkb = S // bq, Sk // bk
    use_f = fq is not None
    if causal:
        assert S == Sk and bq == bk

    def body(*refs):
        if use_f:
            q_ref, k_ref, v_ref, fq_ref, fk_ref, o_ref, lse_ref = refs
        else:
            q_ref, k_ref, v_ref, o_ref, lse_ref = refs
        i = pl.program_id(1)
        qb = q_ref[...]

        def blk(j, carry, masked):
            m, l, acc = carry
            off = pl.multiple_of(j * bk, bk)
            kb = k_ref[pl.ds(off, bk), :]
            vb = v_ref[pl.ds(off, bk), :]
            s = _dot(qb, kb, _NT) * scale
            if use_f:
                s = s + fq_ref[...] - fk_ref[j]
            if masked:
                s = jnp.where(_causal_mask(i, j, bq, bk, False), s, NEG_BIG)
            m_new = jnp.maximum(m, jnp.max(s, axis=1, keepdims=True))
            p = jnp.exp(s - m_new)
            a = jnp.exp(m - m_new)
            l = a * l + jnp.sum(p, axis=1, keepdims=True)
            acc = a * acc + _dot(p.astype(BF16), vb)
            return m_new, l, acc

        carry = (jnp.full((bq, 1), NEG_BIG, F32), jnp.zeros((bq, 1), F32), jnp.zeros((bq, dv), F32))
        if causal:
            carry = lax.fori_loop(0, i, lambda j, c: blk(j, c, False), carry)
            carry = blk(i, carry, True)
        else:
            carry = lax.fori_loop(0, nkb, lambda j, c: blk(j, c, False), carry)
        m, l, acc = carry
        o_ref[...] = acc / l
        lse_ref[...] = m + jnp.log(l)

    in_specs = [pl.BlockSpec((None, bq, dk), lambda h, i: (h, i, 0)),
                pl.BlockSpec((None, Sk, dk), lambda h, i: (h, 0, 0)),
                pl.BlockSpec((None, Sk, dv), lambda h, i: (h, 0, 0))]
    args = [q, k, v]
    if use_f:
        in_specs += [pl.BlockSpec((None, bq, 1), lambda h, i: (h, i, 0)),
                     pl.BlockSpec((None, nkb, 1, bk), lambda h, i: (h, 0, 0, 0))]
        args += [fq, fk]
    return pl.pallas_call(
        body, name=name, grid=(H, nq), in_specs=in_specs,
        out_specs=[pl.BlockSpec((None, bq, dv), lambda h, i: (h, i, 0)),
                   pl.BlockSpec((None, bq, 1), lambda h, i: (h, i, 0))],
        out_shape=[jax.ShapeDtypeStruct((H, S, dv), F32), jax.ShapeDtypeStruct((H, S, 1), F32)],
        compiler_params=_cp(("parallel", "arbitrary")),
    )(*args)


def _softmax_attn_bwd(q, k, v, fq, fk, o, lse, do, scale, causal, name):
    H, S, dk = q.shape
    Sk, dv = k.shape[1], v.shape[2]
    bq, bk = _attn_blocks(S, Sk)
    nq, nkb = S // bq, Sk // bk
    use_f = fq is not None

    def body(*refs):
        if use_f:
            (q_ref, k_ref, v_ref, fq_ref, fk_ref, o_ref, lse_ref, do_ref,
             dq_ref, dk_ref, dv_ref, dfq_ref, dfk_ref) = refs
        else:
            q_ref, k_ref, v_ref, o_ref, lse_ref, do_ref, dq_ref, dk_ref, dv_ref = refs
        i = pl.program_id(1)

        @pl.when(i == 0)
        def _():
            dk_ref[...] = jnp.zeros_like(dk_ref)
            dv_ref[...] = jnp.zeros_like(dv_ref)
            if use_f:
                dfk_ref[...] = jnp.zeros_like(dfk_ref)

        qb = q_ref[...]
        dof = do_ref[...]
        dob = dof.astype(BF16)
        delta = jnp.sum(dof * o_ref[...], axis=1, keepdims=True)
        lseb = lse_ref[...]

        def blk(j, carry, masked):
            dq, dfq = carry
            off = pl.multiple_of(j * bk, bk)
            kb = k_ref[pl.ds(off, bk), :]
            vb = v_ref[pl.ds(off, bk), :]
            s = _dot(qb, kb, _NT) * scale
            if use_f:
                s = s + fq_ref[...] - fk_ref[j]
            if masked:
                s = jnp.where(_causal_mask(i, j, bq, bk, False), s, NEG_BIG)
            p = jnp.exp(s - lseb)
            dp = _dot(dob, vb, _NT)
            ds = p * (dp - delta)
            dss = (ds * scale).astype(BF16)
            dv_ref[pl.ds(off, bk), :] += _dot(p.astype(BF16), dob, _TN)
            dk_ref[pl.ds(off, bk), :] += _dot(dss, qb, _TN)
            if use_f:
                dfk_ref[j] += -jnp.sum(ds, axis=0, keepdims=True)
                dfq = dfq + jnp.sum(ds, axis=1, keepdims=True)
            return dq + _dot(dss, kb), dfq

        carry = (jnp.zeros((bq, dk), F32), jnp.zeros((bq, 1), F32))
        if causal:
            carry = lax.fori_loop(0, i, lambda j, c: blk(j, c, False), carry)
            carry = blk(i, carry, True)
        else:
            carry = lax.fori_loop(0, nkb, lambda j, c: blk(j, c, False), carry)
        dq_ref[...] = carry[0].astype(BF16)
        if use_f:
            dfq_ref[...] = carry[1]

    qspec = lambda d: pl.BlockSpec((None, bq, d), lambda h, i: (h, i, 0))
    kspec = lambda d: pl.BlockSpec((None, Sk, d), lambda h, i: (h, 0, 0))
    fkspec = pl.BlockSpec((None, nkb, 1, bk), lambda h, i: (h, 0, 0, 0))
    in_specs = [qspec(dk), kspec(dk), kspec(dv)]
    args = [q, k, v]
    if use_f:
        in_specs += [qspec(1), fkspec]
        args += [fq, fk]
    in_specs += [qspec(dv), qspec(1), qspec(dv)]
    args += [o, lse, do]
    out_specs = [qspec(dk), kspec(dk), kspec(dv)]
    out_shape = [jax.ShapeDtypeStruct((H, S, dk), BF16), jax.ShapeDtypeStruct((H, Sk, dk), F32),
                 jax.ShapeDtypeStruct((H, Sk, dv), F32)]
    if use_f:
        out_specs += [qspec(1), fkspec]
        out_shape += [jax.ShapeDtypeStruct((H, S, 1), F32), jax.ShapeDtypeStruct((H, nkb, 1, bk), F32)]
    return pl.pallas_call(
        body, name=name, grid=(H, nq), in_specs=in_specs, out_specs=out_specs, out_shape=out_shape,
        compiler_params=_cp(("parallel", "arbitrary")),
    )(*args)


def _make_softmax_attn(name, scale, causal, use_f):
    if use_f:
        @jax.custom_vjp
        def attn(q, k, v, fq, fk):
            return _softmax_attn_fwd(q, k, v, fq, fk, scale, causal, name + "_fwd")[0]

        def fwd(q, k, v, fq, fk):
            o, lse = _softmax_attn_fwd(q, k, v, fq, fk, scale, causal, name + "_fwd")
            return o, (q, k, v, fq, fk, o, lse)

        def bwd(saved, do):
            q, k, v, fq, fk, o, lse = saved
            dq, dk, dv, dfq, dfk = _softmax_attn_bwd(q, k, v, fq, fk, o, lse, do, scale, causal, name + "_bwd")
            return dq, dk.astype(BF16), dv.astype(BF16), dfq, dfk
    else:
        @jax.custom_vjp
        def attn(q, k, v):
            return _softmax_attn_fwd(q, k, v, None, None, scale, causal, name + "_fwd")[0]

        def fwd(q, k, v):
            o, lse = _softmax_attn_fwd(q, k, v, None, None, scale, causal, name + "_fwd")
            return o, (q, k, v, o, lse)

        def bwd(saved, do):
            q, k, v, o, lse = saved
            dq, dk, dv = _softmax_attn_bwd(q, k, v, None, None, o, lse, do, scale, causal, name + "_bwd")
            return dq, dk.astype(BF16), dv.astype(BF16)

    attn.defvjp(fwd, bwd)
    return attn


def _tri(n, fn):
    r = lax.broadcasted_iota(jnp.int32, (n, n), 0)
    c = lax.broadcasted_iota(jnp.int32, (n, n), 1)
    return jnp.where(fn(r, c), 1.0, 0.0).astype(BF16)


def _dot_hilo(x, u):
    hi = x.astype(BF16)
    lo = (x - hi.astype(F32)).astype(BF16)
    return _dot(hi, u) + _dot(lo, u)


def _sb_logs(z):
    sp = jnp.log1p(jnp.exp(-jnp.abs(z)))
    ls = jnp.minimum(z, 0.0) - sp
    return ls, ls - z


def _sb_attn_fwd(q, k, v, scale, name):
    H, S, dk = q.shape
    dv = v.shape[2]
    bq, bk = _attn_blocks(S, S)
    assert bq == bk
    nq = S // bq

    def body(q_ref, k_ref, v_ref, o_ref, lt_ref):
        i = pl.program_id(1)
        qb = q_ref[...]
        upper = _tri(bk, lambda r, c: r > c)

        def blk(jj, carry, masked):
            rsum, acc = carry
            j = i - jj
            off = pl.multiple_of(j * bk, bk)
            kb = k_ref[pl.ds(off, bk), :]
            vb = v_ref[pl.ds(off, bk), :]
            z = _dot(qb, kb, _NT) * scale
            ls, lk = _sb_logs(z)
            if masked:
                valid = _causal_mask(i, j, bq, bk, True)
                lk = jnp.where(valid, lk, 0.0)
            tail = _dot_hilo(lk, upper) + rsum
            w = jnp.exp(ls + tail)
            if masked:
                w = jnp.where(valid, w, 0.0)
            acc = acc + _dot(w.astype(BF16), vb)
            return rsum + jnp.sum(lk, axis=1, keepdims=True), acc

        carry = (jnp.zeros((bq, 1), F32), jnp.zeros((bq, dv), F32))
        carry = blk(0, carry, True)
        carry = lax.fori_loop(1, i + 1, lambda jj, c: blk(jj, c, False), carry)
        rsum, acc = carry
        o_ref[...] = acc
        lt_ref[...] = rsum

    qspec = lambda d: pl.BlockSpec((None, bq, d), lambda h, i: (h, i, 0))
    kspec = lambda d: pl.BlockSpec((None, S, d), lambda h, i: (h, 0, 0))
    return pl.pallas_call(
        body, name=name, grid=(H, nq), in_specs=[qspec(dk), kspec(dk), kspec(dv)],
        out_specs=[qspec(dv), qspec(1)],
        out_shape=[jax.ShapeDtypeStruct((H, S, dv), F32), jax.ShapeDtypeStruct((H, S, 1), F32)],
        compiler_params=_cp(("parallel", "arbitrary")),
    )(q, k, v)


def _sb_attn_bwd(q, k, v, lt, do, scale, name):
    H, S, dk = q.shape
    dv = v.shape[2]
    bq, bk = _attn_blocks(S, S)
    nq = S // bq

    def body(q_ref, k_ref, v_ref, lt_ref, do_ref, dq_ref, dk_ref, dv_ref):
        i = pl.program_id(1)

        @pl.when(i == 0)
        def _():
            dk_ref[...] = jnp.zeros_like(dk_ref)
            dv_ref[...] = jnp.zeros_like(dv_ref)

        qb = q_ref[...]
        dob = do_ref[...].astype(BF16)
        ltot = lt_ref[...]
        incl = _tri(bk, lambda r, c: r <= c)
        excl = _tri(bk, lambda r, c: r < c)

        def blk(j, carry, masked):
            dq, rp, gp = carry
            off = pl.multiple_of(j * bk, bk)
            kb = k_ref[pl.ds(off, bk), :]
            vb = v_ref[pl.ds(off, bk), :]
            z = _dot(qb, kb, _NT) * scale
            ls, lk = _sb_logs(z)
            if masked:
                valid = _causal_mask(i, j, bq, bk, True)
                lk = jnp.where(valid, lk, 0.0)
            tail = ltot - (rp + _dot_hilo(lk, incl))
            w = jnp.exp(ls + tail)
            if masked:
                w = jnp.where(valid, w, 0.0)
            g = _dot(dob, vb, _NT) * w
            c = gp + _dot_hilo(g, excl)
            sig = jnp.exp(ls)
            dz = g * (1.0 - sig) - c * sig
            if masked:
                dz = jnp.where(valid, dz, 0.0)
            dzs = (dz * scale).astype(BF16)
            dv_ref[pl.ds(off, bk), :] += _dot(w.astype(BF16), dob, _TN)
            dk_ref[pl.ds(off, bk), :] += _dot(dzs, qb, _TN)
            return (dq + _dot(dzs, kb), rp + jnp.sum(lk, axis=1, keepdims=True),
                    gp + jnp.sum(g, axis=1, keepdims=True))

        carry = (jnp.zeros((bq, dk), F32), jnp.zeros((bq, 1), F32), jnp.zeros((bq, 1), F32))
        carry = lax.fori_loop(0, i, lambda j, c: blk(j, c, False), carry)
        carry = blk(i, carry, True)
        dq_ref[...] = carry[0].astype(BF16)

    qspec = lambda d: pl.BlockSpec((None, bq, d), lambda h, i: (h, i, 0))
    kspec = lambda d: pl.BlockSpec((None, S, d), lambda h, i: (h, 0, 0))
    return pl.pallas_call(
        body, name=name, grid=(H, nq),
        in_specs=[qspec(dk), kspec(dk), kspec(dv), qspec(1), qspec(dv)],
        out_specs=[qspec(dk), kspec(dk), kspec(dv)],
        out_shape=[jax.ShapeDtypeStruct((H, S, dk), BF16), jax.ShapeDtypeStruct((H, S, dk), F32),
                   jax.ShapeDtypeStruct((H, S, dv), F32)],
        compiler_params=_cp(("parallel", "arbitrary")),
    )(q, k, v, lt, do)


def _make_sb_attn(name, scale):
    @jax.custom_vjp
    def attn(q, k, v):
        return _sb_attn_fwd(q, k, v, scale, name + "_fwd")[0]

    def fwd(q, k, v):
        o, lt = _sb_attn_fwd(q, k, v, scale, name + "_fwd")
        return o, (q, k, v, lt)

    def bwd(saved, do):
        q, k, v, lt = saved
        dq, dk, dv = _sb_attn_bwd(q, k, v, lt, do, scale, name + "_bwd")
        return dq, dk.astype(BF16), dv.astype(BF16)

    attn.defvjp(fwd, bwd)
    return attn


def _to_heads(t, n):
    s = t.shape[0]
    return t.reshape(s, n, -1).transpose(1, 0, 2)


def _merge_heads(t):
    n, s, d = t.shape
    return t.transpose(1, 0, 2).reshape(s, n * d)


def _rope(x, positions):
    half = x.shape[-1] // 2
    inv_freq = ROPE_THETA ** (-jnp.arange(half, dtype=F32) / half)
    ang = positions.astype(F32)[:, None] * inv_freq[None, :]
    ang = ang.reshape((ang.shape[0],) + (1,) * (x.ndim - 2) + (half,))
    cos, sin = jnp.cos(ang), jnp.sin(ang)
    x1, x2 = x[..., :half], x[..., half:]
    return jnp.concatenate([x1 * cos - x2 * sin, x1 * sin + x2 * cos], axis=-1)


def _permute_w_in(w):
    parts = [w[:, _ORIG_OFF[idx]:_ORIG_OFF[idx] + SPLIT_SIZES[idx]] for _, idx in _PERM]
    pad = jnp.zeros((w.shape[0], PROJ_COLS - IN_COLS), w.dtype)
    return jnp.concatenate(parts + [pad], axis=1)


def _split_proj(proj):
    out, off = {}, 0
    for name, idx in _PERM:
        out[name] = proj[:, off:off + SPLIT_SIZES[idx]]
        off += SPLIT_SIZES[idx]
    return out


def _trunk_loss(wts, x2d, mem2d, target2d):
    s = x2d.shape[0]
    bq, bk = _attn_blocks(s, s)
    positions = jnp.arange(s)
    head_scale = HEAD_DIM ** -0.5
    mla_scale = (MLA_NOPE + MLA_ROPE) ** -0.5

    h = _make_ln("ln_in", False)(x2d, wts["ln_in_g"], wts["ln_in_b"])
    mem_n = _make_ln("ln_mem", False)(mem2d, wts["mem_ln_g"], wts["mem_ln_b"])

    for l in range(DEPTH):
        tag = f"l{l}_"
        proj = _make_mm(tag + "proj")(h, _permute_w_in(wts["w_in"][l]))
        p = _split_proj(proj)
        heads_bf = lambda t: _to_heads(t, N_HEADS).astype(BF16)

        log_f = jax.nn.log_sigmoid(p["f_logit"] + wts["b_forget"][l])
        f_cum = jnp.cumsum(log_f, axis=0).T
        out_fox = _make_softmax_attn(tag + "fox", head_scale, True, True)(
            heads_bf(p["fq"]), heads_bf(p["fk"]), heads_bf(p["fv"]),
            f_cum[:, :, None], f_cum.reshape(N_HEADS, s // bk, 1, bk))

        out_sb = _make_sb_attn(tag + "sb", head_scale)(heads_bf(p["sq"]), heads_bf(p["sk"]), heads_bf(p["sv"]))

        cqn = _make_rms(tag + "rms_q")(p["c_q"], wts["mla_q_norm_g"][l])
        q_mla = _make_mm(tag + "q_up")(cqn, wts["w_mla_q_up"][l]).reshape(s, N_HEADS, MLA_NOPE + MLA_ROPE)
        ckvn = _make_rms(tag + "rms_kv")(p["c_kv"], wts["mla_kv_norm_g"][l])
        kv_mla = _make_mm(tag + "kv_up")(ckvn, wts["w_mla_kv_up"][l]).reshape(s, N_HEADS, MLA_NOPE + MLA_V)
        zpad = jnp.zeros((s, N_HEADS, LANE - MLA_NOPE - MLA_ROPE), F32)
        q_full = jnp.concatenate([q_mla[..., :MLA_NOPE], _rope(q_mla[..., MLA_NOPE:], positions), zpad], axis=-1)
        k_rope = jnp.broadcast_to(_rope(p["k_rot"], positions)[:, None, :], (s, N_HEADS, MLA_ROPE))
        k_full = jnp.concatenate([kv_mla[..., :MLA_NOPE], k_rope, zpad], axis=-1)
        v_mla = kv_mla[..., MLA_NOPE:]
        tr = lambda t: t.transpose(1, 0, 2).astype(BF16)
        out_mla = _make_softmax_attn(tag + "mla", mla_scale, True, False)(tr(q_full), tr(k_full), tr(v_mla))

        mkv = _make_mm(tag + "mem_kv")(mem_n, wts["w_mem_kv"][l])
        out_mem = _make_softmax_attn(tag + "mem", head_scale, False, False)(
            heads_bf(p["mq"]), heads_bf(mkv[:, :GROUP_W]), heads_bf(mkv[:, GROUP_W:]))

        mixed = jnp.concatenate([_merge_heads(out_fox), _merge_heads(out_sb), _merge_heads(out_mla),
                                 _merge_heads(out_mem)], axis=-1)
        gated = _make_gate(tag + "gate")(mixed, p["gate"])
        y = _make_mm(tag + "out")(gated, wts["w_out"][l])
        h = _make_ln(tag + "ln", True)(y, h, wts["ln_g"][l], wts["ln_b"][l])

    return _loss_op(h, target2d)


def _mesh_pos():
    x, y, c = (lax.axis_index(a) for a in MESH_AXES)
    return x, y, c, 4 * x + 2 * y + c


def _peer(x, y, c, mask):
    return (x ^ ((mask >> 2) & 1), y ^ ((mask >> 1) & 1), c ^ (mask & 1))


_ANY = pl.BlockSpec(memory_space=pl.ANY)


def _all_gather(row_shards, stack_shards):
    n_row, n_all = len(row_shards), len(row_shards) + len(stack_shards)
    shards = list(row_shards) + list(stack_shards)

    def body(*refs):
        ins, outs = refs[:n_all], refs[n_all:2 * n_all]
        send_sems, recv_sems, local_sems = refs[2 * n_all:]
        x, y, c, me = _mesh_pos()

        def window(t, slot):
            if t < n_row:
                rows = shards[t].shape[1]
                return outs[t].at[:, pl.ds(slot * rows, rows), :]
            return outs[t].at[slot]

        local = [pltpu.make_async_copy(ins[t], window(t, me), local_sems.at[t]) for t in range(n_all)]
        for cp in local:
            cp.start()
        sends = []
        for mask in range(1, N_DEV):
            for t in range(n_all):
                cp = pltpu.make_async_remote_copy(
                    src_ref=ins[t], dst_ref=window(t, me), send_sem=send_sems.at[t, mask - 1],
                    recv_sem=recv_sems.at[t, mask - 1], device_id=_peer(x, y, c, mask),
                    device_id_type=pl.DeviceIdType.MESH)
                cp.start()
                sends.append(cp)
        for mask in range(1, N_DEV):
            for t in range(n_all):
                pltpu.make_async_remote_copy(
                    src_ref=ins[t], dst_ref=window(t, me ^ mask), send_sem=send_sems.at[t, mask - 1],
                    recv_sem=recv_sems.at[t, mask - 1], device_id=_peer(x, y, c, mask),
                    device_id_type=pl.DeviceIdType.MESH).wait_recv()
        for cp in sends:
            cp.wait_send()
        for cp in local:
            cp.wait()

    out_shape = [jax.ShapeDtypeStruct((a.shape[0], N_DEV * a.shape[1], a.shape[2]), a.dtype) for a in row_shards]
    out_shape += [jax.ShapeDtypeStruct((N_DEV,) + a.shape, a.dtype) for a in stack_shards]
    return pl.pallas_call(
        body, name="all_gather_weights", in_specs=[_ANY] * n_all, out_specs=[_ANY] * n_all, out_shape=out_shape,
        scratch_shapes=[pltpu.SemaphoreType.DMA((n_all, N_DEV - 1)), pltpu.SemaphoreType.DMA((n_all, N_DEV - 1)),
                        pltpu.SemaphoreType.DMA((n_all,))],
    )(*shards)


def _reduce_scatter(row_full, stack_full, bcast):
    n_row, n_stack = len(row_full), len(stack_full)
    n_all = n_row + n_stack + len(bcast)
    fulls = list(row_full) + list(stack_full) + list(bcast)

    def body(*refs):
        ins, outs = refs[:n_all], refs[n_all:2 * n_all]
        send_sems, recv_sems, local_sems = refs[2 * n_all:]
        x, y, c, me = _mesh_pos()

        def part(t, slot):
            if t < n_row:
                rows = fulls[t].shape[1] // N_DEV
                return ins[t].at[:, pl.ds(slot * rows, rows), :]
            if t < n_row + n_stack:
                return ins[t].at[slot]
            return ins[t]

        local = [pltpu.make_async_copy(part(t, me), outs[t].at[me], local_sems.at[t]) for t in range(n_all)]
        for cp in local:
            cp.start()
        sends = []
        for mask in range(1, N_DEV):
            for t in range(n_all):
                cp = pltpu.make_async_remote_copy(
                    src_ref=part(t, me ^ mask), dst_ref=outs[t].at[me], send_sem=send_sems.at[t, mask - 1],
                    recv_sem=recv_sems.at[t, mask - 1], device_id=_peer(x, y, c, mask),
                    device_id_type=pl.DeviceIdType.MESH)
                cp.start()
                sends.append(cp)
        for mask in range(1, N_DEV):
            for t in range(n_all):
                pltpu.make_async_remote_copy(
                    src_ref=part(t, me), dst_ref=outs[t].at[me ^ mask], send_sem=send_sems.at[t, mask - 1],
                    recv_sem=recv_sems.at[t, mask - 1], device_id=_peer(x, y, c, mask),
                    device_id_type=pl.DeviceIdType.MESH).wait_recv()
        for cp in sends:
            cp.wait_send()
        for cp in local:
            cp.wait()

    out_shape = [jax.ShapeDtypeStruct((N_DEV, a.shape[0], a.shape[1] // N_DEV, a.shape[2]), a.dtype) for a in row_full]
    out_shape += [jax.ShapeDtypeStruct(a.shape, a.dtype) for a in stack_full]
    out_shape += [jax.ShapeDtypeStruct((N_DEV,) + a.shape, a.dtype) for a in bcast]
    return pl.pallas_call(
        body, name="reduce_scatter_grads", in_specs=[_ANY] * n_all, out_specs=[_ANY] * n_all, out_shape=out_shape,
        scratch_shapes=[pltpu.SemaphoreType.DMA((n_all, N_DEV - 1)), pltpu.SemaphoreType.DMA((n_all, N_DEV - 1)),
                        pltpu.SemaphoreType.DMA((n_all,))],
    )(*fulls)


def _adamw(slots, w, m, v, name):
    shape = w.shape
    cols = shape[-1]
    rows = math.prod(shape[:-1])
    tr = _pick(rows, (64, 32, 16, 8))
    c1 = 1.0 - ADAM_B1 ** ADAM_STEP
    c2 = 1.0 - ADAM_B2 ** ADAM_STEP

    def body(s_ref, w_ref, m_ref, v_ref, g_ref, d_ref, nm_ref, nv_ref):
        g = s_ref[0]
        for k in range(1, N_DEV):
            g = g + s_ref[k]
        nm = ADAM_B1 * m_ref[...] + (1.0 - ADAM_B1) * g
        nv = ADAM_B2 * v_ref[...] + (1.0 - ADAM_B2) * (g * g)
        g_ref[...] = g
        nm_ref[...] = nm
        nv_ref[...] = nv
        d_ref[...] = -ADAM_LR * ((nm / c1) / (jnp.sqrt(nv / c2) + ADAM_EPS) + ADAM_WD * w_ref[...])

    row = pl.BlockSpec((tr, cols), lambda i: (i, 0))
    out = jax.ShapeDtypeStruct((rows, cols), F32)
    outs = pl.pallas_call(
        body, name=name, grid=(rows // tr,),
        in_specs=[pl.BlockSpec((N_DEV, tr, cols), lambda i: (0, i, 0)), row, row, row],
        out_specs=[row] * 4, out_shape=[out] * 4, compiler_params=_cp(("parallel",)),
    )(slots.reshape(N_DEV, rows, cols), w.reshape(rows, cols), m.reshape(rows, cols), v.reshape(rows, cols))
    return [o.reshape(shape) for o in outs]


_SMALL = ("ln_in_g", "ln_in_b", "mem_ln_g", "mem_ln_b", "b_forget", "mla_q_norm_g", "mla_kv_norm_g", "ln_g", "ln_b")
_ORDER = ("ln_in_g", "ln_in_b", "mem_ln_g", "mem_ln_b", "w_in", "b_forget", "mla_q_norm_g", "w_mla_q_up",
          "mla_kv_norm_g", "w_mla_kv_up", "w_mem_kv", "w_out", "ln_g", "ln_b")


def _pack_small(d):
    flat = jnp.concatenate([d[n].reshape(-1) for n in _SMALL])
    n = flat.shape[0]
    padded = ((n + 8 * LANE - 1) // (8 * LANE)) * (8 * LANE)
    return jnp.pad(flat, (0, padded - n)).reshape(-1, LANE)


def _unpack_small(packed, like):
    flat, out, off = packed.reshape(-1), {}, 0
    for n in _SMALL:
        size = math.prod(like[n].shape)
        out[n] = flat[off:off + size].reshape(like[n].shape)
        off += size
    return out


def _unstack_cols(g):
    n, l, r, c = g.shape
    return g.transpose(1, 2, 0, 3).reshape(l, r, n * c)


def _stack_cols(g):
    l, r, nc = g.shape
    return g.reshape(l, r, N_DEV, nc // N_DEV).transpose(2, 0, 1, 3)


def kernel(x, mem, ln_in_g, ln_in_b, mem_ln_g, mem_ln_b, w_in, b_forget, mla_q_norm_g, w_mla_q_up, mla_kv_norm_g, w_mla_kv_up, w_mem_kv, w_out, ln_g, ln_b, loss_target, m_ln_in_g, m_ln_in_b, m_mem_ln_g, m_mem_ln_b, m_w_in, m_b_forget, m_mla_q_norm_g, m_w_mla_q_up, m_mla_kv_norm_g, m_w_mla_kv_up, m_w_mem_kv, m_w_out, m_ln_g, m_ln_b, v_ln_in_g, v_ln_in_b, v_mem_ln_g, v_mem_ln_b, v_w_in, v_b_forget, v_mla_q_norm_g, v_w_mla_q_up, v_mla_kv_norm_g, v_w_mla_kv_up, v_w_mem_kv, v_w_out, v_ln_g, v_ln_b):
    w_shard = dict(ln_in_g=ln_in_g, ln_in_b=ln_in_b, mem_ln_g=mem_ln_g, mem_ln_b=mem_ln_b, w_in=w_in,
                   b_forget=b_forget, mla_q_norm_g=mla_q_norm_g, w_mla_q_up=w_mla_q_up,
                   mla_kv_norm_g=mla_kv_norm_g, w_mla_kv_up=w_mla_kv_up, w_mem_kv=w_mem_kv, w_out=w_out,
                   ln_g=ln_g, ln_b=ln_b)
    m_shard = dict(ln_in_g=m_ln_in_g, ln_in_b=m_ln_in_b, mem_ln_g=m_mem_ln_g, mem_ln_b=m_mem_ln_b, w_in=m_w_in,
                   b_forget=m_b_forget, mla_q_norm_g=m_mla_q_norm_g, w_mla_q_up=m_w_mla_q_up,
                   mla_kv_norm_g=m_mla_kv_norm_g, w_mla_kv_up=m_w_mla_kv_up, w_mem_kv=m_w_mem_kv, w_out=m_w_out,
                   ln_g=m_ln_g, ln_b=m_ln_b)
    v_shard = dict(ln_in_g=v_ln_in_g, ln_in_b=v_ln_in_b, mem_ln_g=v_mem_ln_g, mem_ln_b=v_mem_ln_b, w_in=v_w_in,
                   b_forget=v_b_forget, mla_q_norm_g=v_mla_q_norm_g, w_mla_q_up=v_w_mla_q_up,
                   mla_kv_norm_g=v_mla_kv_norm_g, w_mla_kv_up=v_w_mla_kv_up, w_mem_kv=v_w_mem_kv, w_out=v_w_out,
                   ln_g=v_ln_g, ln_b=v_ln_b)

    g_in, g_mem, g_out, g_qup, g_kvup = _all_gather([w_in, w_mem_kv, w_out], [w_mla_q_up, w_mla_kv_up])
    full = dict(w_shard)
    full.update(w_in=g_in, w_mem_kv=g_mem, w_out=g_out, w_mla_q_up=_unstack_cols(g_qup),
                w_mla_kv_up=_unstack_cols(g_kvup))

    loss_local, (grad_w, grad_x) = jax.value_and_grad(_trunk_loss, argnums=(0, 1))(
        full, x[0], mem[0], loss_target[0])

    s_in, s_mem, s_out, s_qup, s_kvup, s_small = _reduce_scatter(
        [grad_w["w_in"], grad_w["w_mem_kv"], grad_w["w_out"]],
        [_stack_cols(grad_w["w_mla_q_up"]), _stack_cols(grad_w["w_mla_kv_up"])],
        [_pack_small(grad_w)])

    res = {}
    for name, slots in (("w_in", s_in), ("w_mem_kv", s_mem), ("w_out", s_out), ("w_mla_q_up", s_qup),
                        ("w_mla_kv_up", s_kvup)):
        res[name] = _adamw(slots, w_shard[name], m_shard[name], v_shard[name], "adamw_" + name)
    small = _adamw(s_small, _pack_small(w_shard), _pack_small(m_shard), _pack_small(v_shard), "adamw_small")
    small = [_unpack_small(a, w_shard) for a in small]
    for name in _SMALL:
        res[name] = [a[name] for a in small]

    loss = lax.psum(loss_local, MESH_AXES)
    outs = [loss, grad_x[None]]
    for k in range(4):
        outs += [res[name][k] for name in _ORDER]
    return tuple(outs)
```

```python
import functools
import math

import jax
import jax.numpy as jnp
from jax import lax
from jax.experimental import pallas as pl
from jax.experimental.pallas import tpu as pltpu

F32 = jnp.float32
BF16 = jnp.bfloat16

D_MODEL = 1024
DEPTH = 2
GROUP_W = 256
N_HEADS = 4
HEAD_DIM = 64
MLA_Q_RANK = 256
MLA_KV_RANK = 128
MLA_NOPE = 64
MLA_ROPE = 32
MLA_V = 64
ROPE_THETA = 10000.0
LN_EPS = 1e-5
RMS_EPS = 1e-6
DEEPNORM_ALPHA = (2 * DEPTH) ** 0.25
SPLIT_SIZES = (256, 256, 256, 4, 256, 256, 256, 256, 128, 32, 256, 1024)
IN_COLS = sum(SPLIT_SIZES)
_ORIG_OFF = [sum(SPLIT_SIZES[:i]) for i in range(len(SPLIT_SIZES))]
_PERM = (("fq", 0), ("fk", 1), ("fv", 2), ("sq", 4), ("sk", 5), ("sv", 6), ("c_q", 7), ("c_kv", 8),
         ("mq", 10), ("gate", 11), ("k_rot", 9), ("f_logit", 3))
LANE = 128
PROJ_COLS = ((IN_COLS + LANE - 1) // LANE) * LANE

ADAM_LR = 0.001
ADAM_B1 = 0.9
ADAM_B2 = 0.999
ADAM_EPS = 1e-08
ADAM_WD = 0.01
ADAM_STEP = 10

N_DEV = 8
MESH_AXES = ("x", "y", "c")
VMEM_LIMIT = 48 * 1024 * 1024
ATTN_VMEM_LIMIT = 56 * 1024 * 1024
ATTN_BQ = 512
ATTN_BK = 512
CUMSUM_CHUNK = 256
NEG_BIG = -1e30
LOG2E = math.log2(math.e)

_NT = (((1,), (1,)), ((), ()))
_TN = (((0,), (0,)), ((), ()))
_NN = (((1,), (0,)), ((), ()))


def _cp(sem, vmem=VMEM_LIMIT):
    return pltpu.CompilerParams(dimension_semantics=sem, vmem_limit_bytes=vmem)


def _dot(a, b, dims=_NN):
    return lax.dot_general(a, b, dims, preferred_element_type=F32)


def _pick(n, cands):
    for c in cands:
        if c <= n and n % c == 0:
            return c
    return n


def _matmul(a, b, mode, name, tm=None, tn=None, tk=None):
    if mode == "nn":
        (M, K), (K2, N) = a.shape, b.shape
    elif mode == "nt":
        (M, K), (N, K2) = a.shape, b.shape
    else:
        (K, M), (K2, N) = a.shape, b.shape
    assert K == K2, (a.shape, b.shape, mode)
    tm = tm or _pick(M, (1024, 512, 256, 128, 64, 32, 16, 8))
    tn = tn or (N if N <= 1024 else _pick(N, (1024, 896, 768, 640, 512, 384, 256, 128)))
    tk = tk or (K if K <= 1024 else _pick(K, (512, 256, 128)))
    assert M % tm == 0 and N % tn == 0 and K % tk == 0, (M, N, K, tm, tn, tk)
    nk = K // tk
    dims = {"nn": _NN, "nt": _NT, "tn": _TN}[mode]

    def body(a_ref, b_ref, o_ref, acc_ref):
        part = _dot(a_ref[...].astype(BF16), b_ref[...].astype(BF16), dims)
        if nk == 1:
            o_ref[...] = part
        else:
            k = pl.program_id(2)

            @pl.when(k == 0)
            def _():
                acc_ref[...] = part

            @pl.when(k > 0)
            def _():
                acc_ref[...] += part

            @pl.when(k == nk - 1)
            def _():
                o_ref[...] = acc_ref[...]

    if mode == "nn":
        a_spec = pl.BlockSpec((tm, tk), lambda j, i, k: (i, k))
        b_spec = pl.BlockSpec((tk, tn), lambda j, i, k: (k, j))
    elif mode == "nt":
        a_spec = pl.BlockSpec((tm, tk), lambda j, i, k: (i, k))
        b_spec = pl.BlockSpec((tn, tk), lambda j, i, k: (j, k))
    else:
        a_spec = pl.BlockSpec((tk, tm), lambda j, i, k: (k, i))
        b_spec = pl.BlockSpec((tk, tn), lambda j, i, k: (k, j))
    acc_shape = (tm, tn) if nk > 1 else (8, 128)
    return pl.pallas_call(
        body, name=name, grid=(N // tn, M // tm, nk),
        in_specs=[a_spec, b_spec],
        out_specs=pl.BlockSpec((tm, tn), lambda j, i, k: (i, j)),
        out_shape=jax.ShapeDtypeStruct((M, N), F32),
        scratch_shapes=[pltpu.VMEM(acc_shape, F32)],
        compiler_params=_cp(("parallel", "parallel", "arbitrary")),
    )(a, b)


def _make_mm(name):
    @jax.custom_vjp
    def mm(a, w):
        return _matmul(a, w, "nn", name + "_fwd")

    def fwd(a, w):
        return mm(a, w), (a, w)

    def bwd(res, dy):
        a, w = res
        da = _matmul(dy, w, "nt", name + "_dx")
        dw = _matmul(a, dy, "tn", name + "_dw")
        return da, dw

    mm.defvjp(fwd, bwd)
    return mm


def _row_tile(rows):
    return _pick(rows, (512, 256, 128, 64, 32, 16, 8))


def _ln_stats(u):
    mu = jnp.mean(u, axis=-1, keepdims=True)
    d = u - mu
    var = jnp.mean(d * d, axis=-1, keepdims=True)
    return d, lax.rsqrt(var + LN_EPS)


def _ln_fwd_call(x, res, g, b, name):
    rows, dm = x.shape
    tr = _row_tile(rows)
    has_res = res is not None

    def body(*refs):
        if has_res:
            x_ref, r_ref, g_ref, b_ref, o_ref = refs
            u = DEEPNORM_ALPHA * r_ref[...] + x_ref[...]
        else:
            x_ref, g_ref, b_ref, o_ref = refs
            u = x_ref[...]
        d, rstd = _ln_stats(u)
        o_ref[...] = d * rstd * g_ref[...] + b_ref[...]

    row = pl.BlockSpec((tr, dm), lambda i: (i, 0))
    vec = pl.BlockSpec((1, dm), lambda i: (0, 0))
    args = (x, res) if has_res else (x,)
    return pl.pallas_call(
        body, name=name, grid=(rows // tr,),
        in_specs=[row] * len(args) + [vec, vec], out_specs=row,
        out_shape=jax.ShapeDtypeStruct((rows, dm), F32),
        compiler_params=_cp(("parallel",)),
    )(*args, g.reshape(1, dm), b.reshape(1, dm))


def _ln_bwd_call(dy, x, res, g, name):
    rows, dm = x.shape
    tr = _row_tile(rows)
    has_res = res is not None

    def body(*refs):
        if has_res:
            dy_ref, x_ref, r_ref, g_ref, dx_ref, dr_ref, dg_ref, db_ref = refs
            u = DEEPNORM_ALPHA * r_ref[...] + x_ref[...]
        else:
            dy_ref, x_ref, g_ref, dx_ref, dg_ref, db_ref = refs
            u = x_ref[...]
        i = pl.program_id(0)
        d, rstd = _ln_stats(u)
        xhat = d * rstd
        dyv = dy_ref[...]
        dxh = dyv * g_ref[...]
        m1 = jnp.mean(dxh, axis=-1, keepdims=True)
        m2 = jnp.mean(dxh * xhat, axis=-1, keepdims=True)
        du = rstd * (dxh - m1 - xhat * m2)
        dx_ref[...] = du
        if has_res:
            dr_ref[...] = DEEPNORM_ALPHA * du
        pg = jnp.sum(dyv * xhat, axis=0, keepdims=True)
        pb = jnp.sum(dyv, axis=0, keepdims=True)

        @pl.when(i == 0)
        def _():
            dg_ref[...] = pg
            db_ref[...] = pb

        @pl.when(i > 0)
        def _():
            dg_ref[...] += pg
            db_ref[...] += pb

    row = pl.BlockSpec((tr, dm), lambda i: (i, 0))
    vec = pl.BlockSpec((1, dm), lambda i: (0, 0))
    big = jax.ShapeDtypeStruct((rows, dm), F32)
    small = jax.ShapeDtypeStruct((1, dm), F32)
    args = (dy, x, res) if has_res else (dy, x)
    n_big = 2 if has_res else 1
    outs = pl.pallas_call(
        body, name=name, grid=(rows // tr,),
        in_specs=[row] * len(args) + [vec],
        out_specs=[row] * n_big + [vec, vec],
        out_shape=[big] * n_big + [small, small],
        compiler_params=_cp(("arbitrary",)),
    )(*args, g.reshape(1, dm))
    return outs


def _make_ln(name, has_res):
    if has_res:
        @jax.custom_vjp
        def ln(x, res, g, b):
            return _ln_fwd_call(x, res, g, b, name + "_fwd")

        def fwd(x, res, g, b):
            return ln(x, res, g, b), (x, res, g)

        def bwd(saved, dy):
            x, res, g = saved
            dx, dr, dg, db = _ln_bwd_call(dy, x, res, g, name + "_bwd")
            return dx, dr, dg.reshape(-1), db.reshape(-1)
    else:
        @jax.custom_vjp
        def ln(x, g, b):
            return _ln_fwd_call(x, None, g, b, name + "_fwd")

        def fwd(x, g, b):
            return ln(x, g, b), (x, g)

        def bwd(saved, dy):
            x, g = saved
            dx, dg, db = _ln_bwd_call(dy, x, None, g, name + "_bwd")
            return dx, dg.reshape(-1), db.reshape(-1)

    ln.defvjp(fwd, bwd)
    return ln


def _rms_fwd_call(x, g, name):
    rows, dm = x.shape
    tr = _row_tile(rows)

    def body(x_ref, g_ref, o_ref):
        xv = x_ref[...]
        rstd = lax.rsqrt(jnp.mean(xv * xv, axis=-1, keepdims=True) + RMS_EPS)
        o_ref[...] = xv * rstd * g_ref[...]

    row = pl.BlockSpec((tr, dm), lambda i: (i, 0))
    vec = pl.BlockSpec((1, dm), lambda i: (0, 0))
    return pl.pallas_call(
        body, name=name, grid=(rows // tr,), in_specs=[row, vec], out_specs=row,
        out_shape=jax.ShapeDtypeStruct((rows, dm), F32), compiler_params=_cp(("parallel",)),
    )(x, g.reshape(1, dm))


def _rms_bwd_call(dy, x, g, name):
    rows, dm = x.shape
    tr = _row_tile(rows)

    def body(dy_ref, x_ref, g_ref, dx_ref, dg_ref):
        i = pl.program_id(0)
        xv = x_ref[...]
        dyv = dy_ref[...]
        rstd = lax.rsqrt(jnp.mean(xv * xv, axis=-1, keepdims=True) + RMS_EPS)
        xhat = xv * rstd
        dxh = dyv * g_ref[...]
        m2 = jnp.mean(dxh * xhat, axis=-1, keepdims=True)
        dx_ref[...] = rstd * (dxh - xhat * m2)
        pg = jnp.sum(dyv * xhat, axis=0, keepdims=True)

        @pl.when(i == 0)
        def _():
            dg_ref[...] = pg

        @pl.when(i > 0)
        def _():
            dg_ref[...] += pg

    row = pl.BlockSpec((tr, dm), lambda i: (i, 0))
    vec = pl.BlockSpec((1, dm), lambda i: (0, 0))
    return pl.pallas_call(
        body, name=name, grid=(rows // tr,), in_specs=[row, row, vec], out_specs=[row, vec],
        out_shape=[jax.ShapeDtypeStruct((rows, dm), F32), jax.ShapeDtypeStruct((1, dm), F32)],
        compiler_params=_cp(("arbitrary",)),
    )(dy, x, g.reshape(1, dm))


def _make_rms(name):
    @jax.custom_vjp
    def rms(x, g):
        return _rms_fwd_call(x, g, name + "_fwd")

    def fwd(x, g):
        return rms(x, g), (x, g)

    def bwd(saved, dy):
        x, g = saved
        dx, dg = _rms_bwd_call(dy, x, g, name + "_bwd")
        return dx, dg.reshape(-1)

    rms.defvjp(fwd, bwd)
    return rms


def _sigmoid(x):
    return 1.0 / (1.0 + jnp.exp(-x))


def _gate_fwd_call(mixed, gate, name):
    rows, dm = mixed.shape
    tr = _row_tile(rows)

    def body(m_ref, g_ref, o_ref):
        gv = g_ref[...]
        o_ref[...] = m_ref[...] * (gv * _sigmoid(gv))

    row = pl.BlockSpec((tr, dm), lambda i: (i, 0))
    return pl.pallas_call(
        body, name=name, grid=(rows // tr,), in_specs=[row, row], out_specs=row,
        out_shape=jax.ShapeDtypeStruct((rows, dm), F32), compiler_params=_cp(("parallel",)),
    )(mixed, gate)


def _gate_bwd_call(dy, mixed, gate, name):
    rows, dm = mixed.shape
    tr = _row_tile(rows)

    def body(dy_ref, m_ref, g_ref, dm_ref, dg_ref):
        gv = g_ref[...]
        dyv = dy_ref[...]
        sg = _sigmoid(gv)
        dm_ref[...] = dyv * (gv * sg)
        dg_ref[...] = dyv * m_ref[...] * (sg * (1.0 + gv * (1.0 - sg)))

    row = pl.BlockSpec((tr, dm), lambda i: (i, 0))
    out = jax.ShapeDtypeStruct((rows, dm), F32)
    return pl.pallas_call(
        body, name=name, grid=(rows // tr,), in_specs=[row, row, row], out_specs=[row, row],
        out_shape=[out, out], compiler_params=_cp(("parallel",)),
    )(dy, mixed, gate)


def _make_gate(name):
    @jax.custom_vjp
    def gate_mul(mixed, gate):
        return _gate_fwd_call(mixed, gate, name + "_fwd")

    def fwd(mixed, gate):
        return gate_mul(mixed, gate), (mixed, gate)

    def bwd(saved, dy):
        mixed, gate = saved
        dmix, dgate = _gate_bwd_call(dy, mixed, gate, name + "_bwd")
        return dmix, dgate

    gate_mul.defvjp(fwd, bwd)
    return gate_mul


def _loss_call(y, t, name):
    rows, dm = y.shape
    tr = _row_tile(rows)

    def body(y_ref, t_ref, l_ref, d_ref):
        i = pl.program_id(0)
        e = y_ref[...] - t_ref[...]
        d_ref[...] = e * (1.0 / dm)
        part = 0.5 * jnp.sum(jnp.mean(e * e, axis=-1, keepdims=True), axis=0, keepdims=True)

        @pl.when(i == 0)
        def _():
            l_ref[...] = part

        @pl.when(i > 0)
        def _():
            l_ref[...] += part

    row = pl.BlockSpec((tr, dm), lambda i: (i, 0))
    one = pl.BlockSpec((1, 1), lambda i: (0, 0))
    return pl.pallas_call(
        body, name=name, grid=(rows // tr,), in_specs=[row, row], out_specs=[one, row],
        out_shape=[jax.ShapeDtypeStruct((1, 1), F32), jax.ShapeDtypeStruct((rows, dm), F32)],
        compiler_params=_cp(("arbitrary",)),
    )(y, t)


@jax.custom_vjp
def _loss_op(y, t):
    return _loss_call(y, t, "loss_head")[0][0, 0]


def _loss_fwd(y, t):
    l, d = _loss_call(y, t, "loss_head")
    return l[0, 0], d


def _loss_bwd(d, ct):
    return ct * d, jnp.zeros_like(d)


_loss_op.defvjp(_loss_fwd, _loss_bwd)


def _attn_blocks(S, Sk):
    bq, bk = min(ATTN_BQ, S), min(ATTN_BK, Sk)
    assert S % bq == 0 and Sk % bk == 0
    return bq, bk


def _valid_t(i, j, bq, bk, strict):
    key = j * bk + lax.broadcasted_iota(jnp.int32, (bk, bq), 0)
    qry = i * bq + lax.broadcasted_iota(jnp.int32, (bk, bq), 1)
    return (key < qry) if strict else (key <= qry)


def _sm_fwd_t(qT, k, vT, r, cmul, causal, name):
    H, DK, S = qT.shape
    Sk, dv = k.shape[1], vT.shape[1]
    bq, bk = _attn_blocks(S, Sk)
    nq, nkb = S // bq, Sk // bk
    use_r = r is not None
    if causal:
        assert S == Sk and bq == bk

    def body(*refs):
        if use_r:
            qT_ref, k_ref, vT_ref, r_ref, oT_ref, lse_ref = refs
        else:
            qT_ref, k_ref, vT_ref, oT_ref, lse_ref = refs
        i = pl.program_id(1)
        qTb = qT_ref[...]

        def blk(j, carry, masked):
            m, l, acc = carry
            off = pl.multiple_of(j * bk, bk)
            sT = _dot(k_ref[pl.ds(off, bk), :], qTb) * cmul
            if masked:
                sT = jnp.where(_valid_t(i, j, bq, bk, False), sT, NEG_BIG)
            cm = jnp.max(sT, axis=0, keepdims=True)
            if use_r:
                cm = cm + r_ref[...]
            m_new = jnp.maximum(m, cm)
            shift = (m_new - r_ref[...]) if use_r else m_new
            p = jnp.exp2(sT - shift)
            a = jnp.exp2(m - m_new)
            l = a * l + jnp.sum(p, axis=0, keepdims=True)
            acc = a * acc + _dot(vT_ref[:, pl.ds(off, bk)], p.astype(BF16))
            return m_new, l, acc

        carry = (jnp.full((1, bq), NEG_BIG, F32), jnp.zeros((1, bq), F32), jnp.zeros((dv, bq), F32))
        if causal:
            carry = lax.fori_loop(0, i, lambda j, c: blk(j, c, False), carry)
            carry = blk(i, carry, True)
        else:
            carry = lax.fori_loop(0, nkb, lambda j, c: blk(j, c, False), carry)
        m, l, acc = carry
        oT_ref[...] = acc / l
        lse_ref[...] = m + jnp.log2(l)

    qcol = lambda d: pl.BlockSpec((None, d, bq), lambda h, i: (h, 0, i))
    in_specs = [qcol(DK), pl.BlockSpec((None, Sk, DK), lambda h, i: (h, 0, 0)),
                pl.BlockSpec((None, dv, Sk), lambda h, i: (h, 0, 0))]
    args = [qT, k, vT]
    if use_r:
        in_specs.append(qcol(1))
        args.append(r)
    return pl.pallas_call(
        body, name=name, grid=(H, nq), in_specs=in_specs, out_specs=[qcol(dv), qcol(1)],
        out_shape=[jax.ShapeDtypeStruct((H, dv, S), F32), jax.ShapeDtypeStruct((H, 1, S), F32)],
        compiler_params=_cp(("parallel", "arbitrary"), ATTN_VMEM_LIMIT),
    )(*args)


def _sm_bwd_t(qT, qn, k, kT, v, oT, lse2, doT, do, r, cmul, gscale, causal, name, fcol=None):
    H, DK, S = qT.shape
    Sk, dv = k.shape[1], v.shape[2]
    bq, bk = _attn_blocks(S, Sk)
    nq, nkb = S // bq, Sk // bk
    use_r = r is not None

    def body(*refs):
        if use_r:
            (qT_ref, qn_ref, k_ref, kT_ref, v_ref, oT_ref, lse_ref, doT_ref, do_ref, r_ref,
             dqT_ref, dk_ref, dv_ref, dr_ref) = refs
        else:
            (qT_ref, qn_ref, k_ref, kT_ref, v_ref, oT_ref, lse_ref, doT_ref, do_ref,
             dqT_ref, dk_ref, dv_ref) = refs
        i = pl.program_id(1)

        @pl.when(i == 0)
        def _():
            dk_ref[...] = jnp.zeros_like(dk_ref)
            dv_ref[...] = jnp.zeros_like(dv_ref)

        qTb = qT_ref[...]
        qnb = qn_ref[...]
        dob = do_ref[...]
        doTf = doT_ref[...]
        doTb = doTf.astype(BF16)
        delta = jnp.sum(doTf * oT_ref[...], axis=0, keepdims=True)
        shift = (lse_ref[...] - r_ref[...]) if use_r else lse_ref[...]

        def blk(j, carry, masked):
            dq, dr = carry
            off = pl.multiple_of(j * bk, bk)
            kb = k_ref[pl.ds(off, bk), :]
            sT = _dot(kb, qTb) * cmul
            if masked:
                sT = jnp.where(_valid_t(i, j, bq, bk, False), sT, NEG_BIG)
            p = jnp.exp2(sT - shift)
            dp = _dot(v_ref[pl.ds(off, bk), :], doTb)
            ds = p * (dp - delta)
            dsb = (ds * gscale).astype(BF16) if gscale != 1.0 else ds.astype(BF16)
            dv_ref[pl.ds(off, bk), :] += _dot(p.astype(BF16), dob)
            dkb = _dot(dsb, qnb)
            if use_r:
                dr = dr + jnp.sum(ds, axis=0, keepdims=True)
                lane = lax.broadcasted_iota(jnp.int32, dkb.shape, 1)
                dkb = jnp.where(lane == fcol, jnp.sum(ds, axis=1, keepdims=True), dkb)
            dk_ref[pl.ds(off, bk), :] += dkb
            return dq + _dot(kT_ref[:, pl.ds(off, bk)], dsb), dr

        carry = (jnp.zeros((DK, bq), F32), jnp.zeros((1, bq), F32))
        if causal:
            carry = lax.fori_loop(0, i, lambda j, c: blk(j, c, False), carry)
            carry = blk(i, carry, True)
        else:
            carry = lax.fori_loop(0, nkb, lambda j, c: blk(j, c, False), carry)
        dqT_ref[...] = carry[0]
        if use_r:
            dr_ref[...] = carry[1]

    qcol = lambda d: pl.BlockSpec((None, d, bq), lambda h, i: (h, 0, i))
    qrow = lambda d: pl.BlockSpec((None, bq, d), lambda h, i: (h, i, 0))
    krow = lambda d: pl.BlockSpec((None, Sk, d), lambda h, i: (h, 0, 0))
    in_specs = [qcol(DK), qrow(DK), krow(DK), pl.BlockSpec((None, DK, Sk), lambda h, i: (h, 0, 0)), krow(dv),
                qcol(dv), qcol(1), qcol(dv), qrow(dv)]
    args = [qT, qn, k, kT, v, oT, lse2, doT, do]
    out_specs = [qcol(DK), krow(DK), krow(dv)]
    out_shape = [jax.ShapeDtypeStruct((H, DK, S), F32), jax.ShapeDtypeStruct((H, Sk, DK), F32),
                 jax.ShapeDtypeStruct((H, Sk, dv), F32)]
    if use_r:
        in_specs.append(qcol(1))
        args.append(r)
        out_specs.append(qcol(1))
        out_shape.append(jax.ShapeDtypeStruct((H, 1, S), F32))
    return pl.pallas_call(
        body, name=name, grid=(H, nq), in_specs=in_specs, out_specs=out_specs, out_shape=out_shape,
        compiler_params=_cp(("parallel", "arbitrary"), ATTN_VMEM_LIMIT),
    )(*args)


def _tri(n, fn):
    r = lax.broadcasted_iota(jnp.int32, (n, n), 0)
    c = lax.broadcasted_iota(jnp.int32, (n, n), 1)
    return jnp.where(fn(r, c), 1.0, 0.0).astype(BF16)


def _key_cumsum(x, tri, suffix):
    bk = x.shape[0]
    c = min(CUMSUM_CHUNK, bk)
    n = bk // c
    hi = x.astype(BF16)
    lo = (x - hi.astype(F32)).astype(BF16)
    tot = [jnp.sum(x[a * c:(a + 1) * c], axis=0, keepdims=True) for a in range(n)]
    outs = []
    for a in range(n):
        part = _dot(tri, hi[a * c:(a + 1) * c]) + _dot(tri, lo[a * c:(a + 1) * c])
        others = tot[a + 1:] if suffix else tot[:a]
        for t in others:
            part = part + t
        outs.append(part)
    total = tot[0]
    for t in tot[1:]:
        total = total + t
    return (outs[0] if n == 1 else jnp.concatenate(outs, axis=0)), total


def _sb_logs(z):
    sp = jnp.log1p(jnp.exp(-jnp.abs(z)))
    ls = jnp.minimum(z, 0.0) - sp
    return ls, ls - z


def _sb_fwd_t(qT, k, vT, name):
    H, DK, S = qT.shape
    dv = vT.shape[1]
    bq, bk = _attn_blocks(S, S)
    assert bq == bk
    nq = S // bq
    c = min(CUMSUM_CHUNK, bk)

    def body(qT_ref, k_ref, vT_ref, oT_ref, lt_ref):
        i = pl.program_id(1)
        qTb = qT_ref[...]
        after = _tri(c, lambda s, j: j > s)

        def blk(jj, carry, masked):
            rsum, acc = carry
            j = i - jj
            off = pl.multiple_of(j * bk, bk)
            z = _dot(k_ref[pl.ds(off, bk), :], qTb)
            ls, lk = _sb_logs(z)
            if masked:
                valid = _valid_t(i, j, bq, bk, True)
                lk = jnp.where(valid, lk, 0.0)
            tin, tot = _key_cumsum(lk, after, True)
            w = jnp.exp(ls + tin + rsum)
            if masked:
                w = jnp.where(valid, w, 0.0)
            return rsum + tot, acc + _dot(vT_ref[:, pl.ds(off, bk)], w.astype(BF16))

        carry = (jnp.zeros((1, bq), F32), jnp.zeros((dv, bq), F32))
        carry = blk(0, carry, True)
        carry = lax.fori_loop(1, i + 1, lambda jj, cr: blk(jj, cr, False), carry)
        lt_ref[...] = carry[0]
        oT_ref[...] = carry[1]

    qcol = lambda d: pl.BlockSpec((None, d, bq), lambda h, i: (h, 0, i))
    return pl.pallas_call(
        body, name=name, grid=(H, nq),
        in_specs=[qcol(DK), pl.BlockSpec((None, S, DK), lambda h, i: (h, 0, 0)),
                  pl.BlockSpec((None, dv, S), lambda h, i: (h, 0, 0))],
        out_specs=[qcol(dv), qcol(1)],
        out_shape=[jax.ShapeDtypeStruct((H, dv, S), F32), jax.ShapeDtypeStruct((H, 1, S), F32)],
        compiler_params=_cp(("parallel", "arbitrary"), ATTN_VMEM_LIMIT),
    )(qT, k, vT)


def _sb_bwd_t(qT, qn, k, kT, v, lt, doT, do, name):
    H, DK, S = qT.shape
    dv = v.shape[2]
    bq, bk = _attn_blocks(S, S)
    nq = S // bq
    c = min(CUMSUM_CHUNK, bk)

    def body(qT_ref, qn_ref, k_ref, kT_ref, v_ref, lt_ref, doT_ref, do_ref, dqT_ref, dk_ref, dv_ref):
        i = pl.program_id(1)

        @pl.when(i == 0)
        def _():
            dk_ref[...] = jnp.zeros_like(dk_ref)
            dv_ref[...] = jnp.zeros_like(dv_ref)

        qTb = qT_ref[...]
        qnb = qn_ref[...]
        dob = do_ref[...]
        doTb = doT_ref[...].astype(BF16)
        ltot = lt_ref[...]
        upto = _tri(c, lambda s, j: j <= s)
        before = _tri(c, lambda s, j: j < s)

        def blk(j, carry, masked):
            dq, rp, gp = carry
            off = pl.multiple_of(j * bk, bk)
            kb = k_ref[pl.ds(off, bk), :]
            z = _dot(kb, qTb)
            ls, lk = _sb_logs(z)
            if masked:
                valid = _valid_t(i, j, bq, bk, True)
                lk = jnp.where(valid, lk, 0.0)
            pin, ltb = _key_cumsum(lk, upto, False)
            w = jnp.exp(ls + (ltot - rp) - pin)
            if masked:
                w = jnp.where(valid, w, 0.0)
            g = _dot(v_ref[pl.ds(off, bk), :], doTb) * w
            cin, gtb = _key_cumsum(g, before, False)
            sig = jnp.exp(ls)
            dz = g * (1.0 - sig) - (cin + gp) * sig
            if masked:
                dz = jnp.where(valid, dz, 0.0)
            dzb = dz.astype(BF16)
            dv_ref[pl.ds(off, bk), :] += _dot(w.astype(BF16), dob)
            dk_ref[pl.ds(off, bk), :] += _dot(dzb, qnb)
            return dq + _dot(kT_ref[:, pl.ds(off, bk)], dzb), rp + ltb, gp + gtb

        carry = (jnp.zeros((DK, bq), F32), jnp.zeros((1, bq), F32), jnp.zeros((1, bq), F32))
        carry = lax.fori_loop(0, i, lambda j, cr: blk(j, cr, False), carry)
        carry = blk(i, carry, True)
        dqT_ref[...] = carry[0]

    qcol = lambda d: pl.BlockSpec((None, d, bq), lambda h, i: (h, 0, i))
    qrow = lambda d: pl.BlockSpec((None, bq, d), lambda h, i: (h, i, 0))
    krow = lambda d: pl.BlockSpec((None, S, d), lambda h, i: (h, 0, 0))
    return pl.pallas_call(
        body, name=name, grid=(H, nq),
        in_specs=[qcol(DK), qrow(DK), krow(DK), pl.BlockSpec((None, DK, S), lambda h, i: (h, 0, 0)), krow(dv),
                  qcol(1), qcol(dv), qrow(dv)],
        out_specs=[qcol(DK), krow(DK), krow(dv)],
        out_shape=[jax.ShapeDtypeStruct((H, DK, S), F32), jax.ShapeDtypeStruct((H, S, DK), F32),
                   jax.ShapeDtypeStruct((H, S, dv), F32)],
        compiler_params=_cp(("parallel", "arbitrary"), ATTN_VMEM_LIMIT),
    )(qT, qn, k, kT, v, lt, doT, do)


def _round_bf16(x):
    return lax.reduce_precision(x, exponent_bits=8, mantissa_bits=7)


def _split3(x):
    hi = _round_bf16(x)
    mid = _round_bf16(x - hi)
    lo = _round_bf16(x - hi - mid)
    return hi.astype(BF16), mid.astype(BF16), lo.astype(BF16)


def _pow2(x):
    m, _ = math.frexp(x)
    return m == 0.5


def _pad_last(x, n):
    return jnp.pad(x, [(0, 0)] * (x.ndim - 1) + [(0, n - x.shape[-1])])


def _layouts(q, k, scale, f_cum):
    d = q.shape[-1]
    pre = _pow2(scale)
    qh = jnp.transpose(q * scale if pre else q, (1, 0, 2)).astype(BF16)
    kh = jnp.transpose(k, (1, 0, 2)).astype(BF16)
    S, Sk, H = q.shape[0], k.shape[0], q.shape[1]
    if f_cum is not None:
        assert pre and d + 3 <= LANE
        ones = jnp.ones((H, S, 1), BF16)
        q_s = jnp.concatenate([qh, -ones, -ones, -ones], axis=-1)
        kh = jnp.concatenate([kh] + [t[:, :, None] for t in _split3(f_cum)], axis=-1)
    else:
        q_s = qh
    q_n = qh
    q_s, q_n, kh = _pad_last(q_s, LANE), _pad_last(q_n, LANE), _pad_last(kh, LANE)
    return q_n, jnp.transpose(q_s, (0, 2, 1)), kh, jnp.transpose(kh, (0, 2, 1)), pre


def _make_softmax_attn(name, scale, causal, use_f, d):
    pre = _pow2(scale)
    cmul = LOG2E if pre else scale * LOG2E
    gscale = 1.0 if pre else scale

    def run_fwd(q, k, v, f_cum):
        qn, qT, kn, kT, _ = _layouts(q, k, scale, f_cum)
        vn = jnp.transpose(v, (1, 0, 2)).astype(BF16)
        vT = jnp.transpose(vn, (0, 2, 1))
        r = (f_cum * LOG2E)[:, None, :] if use_f else None
        oT, lse2 = _sm_fwd_t(qT, kn, vT, r, cmul, causal, name + "_fwd")
        return jnp.transpose(oT, (2, 0, 1)), (qn, qT, kn, kT, vn, oT, lse2, r)

    def run_bwd(saved, dout):
        qn, qT, kn, kT, vn, oT, lse2, r = saved
        doT = jnp.transpose(dout, (1, 2, 0))
        do = jnp.transpose(dout, (1, 0, 2)).astype(BF16)
        outs = _sm_bwd_t(qT, qn, kn, kT, vn, oT, lse2, doT, do, r, cmul, gscale, causal, name + "_bwd", d)
        dqT, dk, dv = outs[:3]
        dq = jnp.transpose(dqT[:, :d, :], (2, 0, 1))
        if pre:
            dq = dq * scale
        dkk = jnp.transpose(dk[:, :, :d], (1, 0, 2))
        dvv = jnp.transpose(dv, (1, 0, 2))
        if use_f:
            return dq, dkk, dvv, outs[3][:, 0, :] - dk[:, :, d]
        return dq, dkk, dvv

    if use_f:
        @jax.custom_vjp
        def attn(q, k, v, f_cum):
            return run_fwd(q, k, v, f_cum)[0]

        attn.defvjp(run_fwd, run_bwd)
    else:
        @jax.custom_vjp
        def attn(q, k, v):
            return run_fwd(q, k, v, None)[0]

        attn.defvjp(lambda q, k, v: run_fwd(q, k, v, None), run_bwd)
    return attn


def _make_sb_attn(name, scale, d):
    assert _pow2(scale)

    def run_fwd(q, k, v):
        qn, qT, kn, kT, _ = _layouts(q, k, scale, None)
        vn = jnp.transpose(v, (1, 0, 2)).astype(BF16)
        vT = jnp.transpose(vn, (0, 2, 1))
        oT, lt = _sb_fwd_t(qT, kn, vT, name + "_fwd")
        return jnp.transpose(oT, (2, 0, 1)), (qn, qT, kn, kT, vn, lt)

    def run_bwd(saved, dout):
        qn, qT, kn, kT, vn, lt = saved
        doT = jnp.transpose(dout, (1, 2, 0))
        do = jnp.transpose(dout, (1, 0, 2)).astype(BF16)
        dqT, dk, dv = _sb_bwd_t(qT, qn, kn, kT, vn, lt, doT, do, name + "_bwd")
        dq = jnp.transpose(dqT[:, :d, :], (2, 0, 1)) * scale
        return dq, jnp.transpose(dk[:, :, :d], (1, 0, 2)), jnp.transpose(dv, (1, 0, 2))

    @jax.custom_vjp
    def attn(q, k, v):
        return run_fwd(q, k, v)[0]

    attn.defvjp(run_fwd, run_bwd)
    return attn


def _rope(x, positions):
    half = x.shape[-1] // 2
    inv_freq = ROPE_THETA ** (-jnp.arange(half, dtype=F32) / half)
    ang = positions.astype(F32)[:, None] * inv_freq[None, :]
    ang = ang.reshape((ang.shape[0],) + (1,) * (x.ndim - 2) + (half,))
    cos, sin = jnp.cos(ang), jnp.sin(ang)
    x1, x2 = x[..., :half], x[..., half:]
    return jnp.concatenate([x1 * cos - x2 * sin, x1 * sin + x2 * cos], axis=-1)


def _permute_w_in(w):
    parts = [w[:, _ORIG_OFF[idx]:_ORIG_OFF[idx] + SPLIT_SIZES[idx]] for _, idx in _PERM]
    pad = jnp.zeros((w.shape[0], PROJ_COLS - IN_COLS), w.dtype)
    return jnp.concatenate(parts + [pad], axis=1)


def _split_proj(proj):
    out, off = {}, 0
    for name, idx in _PERM:
        out[name] = proj[:, off:off + SPLIT_SIZES[idx]]
        off += SPLIT_SIZES[idx]
    return out


def _trunk_loss(wts, x2d, mem2d, target2d):
    s = x2d.shape[0]
    positions = jnp.arange(s)
    head_scale = HEAD_DIM ** -0.5
    mla_scale = (MLA_NOPE + MLA_ROPE) ** -0.5
    heads = lambda t: t.reshape(t.shape[0], N_HEADS, -1)

    h = _make_ln("ln_in", False)(x2d, wts["ln_in_g"], wts["ln_in_b"])
    mem_n = _make_ln("ln_mem", False)(mem2d, wts["mem_ln_g"], wts["mem_ln_b"])

    for l in range(DEPTH):
        tag = f"l{l}_"
        proj = _make_mm(tag + "proj")(h, _permute_w_in(wts["w_in"][l]))
        p = _split_proj(proj)

        log_f = jax.nn.log_sigmoid(p["f_logit"] + wts["b_forget"][l])
        f_cum = jnp.cumsum(log_f, axis=0).T
        out_fox = _make_softmax_attn(tag + "fox", head_scale, True, True, HEAD_DIM)(
            heads(p["fq"]), heads(p["fk"]), heads(p["fv"]), f_cum)

        out_sb = _make_sb_attn(tag + "sb", head_scale, HEAD_DIM)(heads(p["sq"]), heads(p["sk"]), heads(p["sv"]))

        cqn = _make_rms(tag + "rms_q")(p["c_q"], wts["mla_q_norm_g"][l])
        q_mla = _make_mm(tag + "q_up")(cqn, wts["w_mla_q_up"][l]).reshape(s, N_HEADS, MLA_NOPE + MLA_ROPE)
        ckvn = _make_rms(tag + "rms_kv")(p["c_kv"], wts["mla_kv_norm_g"][l])
        kv_mla = _make_mm(tag + "kv_up")(ckvn, wts["w_mla_kv_up"][l]).reshape(s, N_HEADS, MLA_NOPE + MLA_V)
        q_full = jnp.concatenate([q_mla[..., :MLA_NOPE], _rope(q_mla[..., MLA_NOPE:], positions)], axis=-1)
        k_rope = jnp.broadcast_to(_rope(p["k_rot"], positions)[:, None, :], (s, N_HEADS, MLA_ROPE))
        k_full = jnp.concatenate([kv_mla[..., :MLA_NOPE], k_rope], axis=-1)
        out_mla = _make_softmax_attn(tag + "mla", mla_scale, True, False, MLA_NOPE + MLA_ROPE)(
            q_full, k_full, kv_mla[..., MLA_NOPE:])

        mkv = _make_mm(tag + "mem_kv")(mem_n, wts["w_mem_kv"][l])
        out_mem = _make_softmax_attn(tag + "mem", head_scale, False, False, HEAD_DIM)(
            heads(p["mq"]), heads(mkv[:, :GROUP_W]), heads(mkv[:, GROUP_W:]))

        mixed = jnp.concatenate([o.reshape(s, GROUP_W) for o in (out_fox, out_sb, out_mla, out_mem)], axis=-1)
        gated = _make_gate(tag + "gate")(mixed, p["gate"])
        y = _make_mm(tag + "out")(gated, wts["w_out"][l])
        h = _make_ln(tag + "ln", True)(y, h, wts["ln_g"][l], wts["ln_b"][l])

    return _loss_op(h, target2d)


def _mesh_pos():
    x, y, c = (lax.axis_index(a) for a in MESH_AXES)
    return x, y, c, 4 * x + 2 * y + c


def _peer(x, y, c, mask):
    return (x ^ ((mask >> 2) & 1), y ^ ((mask >> 1) & 1), c ^ (mask & 1))


_ANY = pl.BlockSpec(memory_space=pl.ANY)


def _all_gather(row_shards, stack_shards):
    n_row, n_all = len(row_shards), len(row_shards) + len(stack_shards)
    shards = list(row_shards) + list(stack_shards)

    def body(*refs):
        ins, outs = refs[:n_all], refs[n_all:2 * n_all]
        send_sems, recv_sems, local_sems = refs[2 * n_all:]
        x, y, c, me = _mesh_pos()

        def window(t, slot):
            if t < n_row:
                rows = shards[t].shape[1]
                return outs[t].at[:, pl.ds(slot * rows, rows), :]
            return outs[t].at[slot]

        local = [pltpu.make_async_copy(ins[t], window(t, me), local_sems.at[t]) for t in range(n_all)]
        for cp in local:
            cp.start()
        sends = []
        for mask in range(1, N_DEV):
            for t in range(n_all):
                cp = pltpu.make_async_remote_copy(
                    src_ref=ins[t], dst_ref=window(t, me), send_sem=send_sems.at[t, mask - 1],
                    recv_sem=recv_sems.at[t, mask - 1], device_id=_peer(x, y, c, mask),
                    device_id_type=pl.DeviceIdType.MESH)
                cp.start()
                sends.append(cp)
        for mask in range(1, N_DEV):
            for t in range(n_all):
                pltpu.make_async_remote_copy(
                    src_ref=ins[t], dst_ref=window(t, me ^ mask), send_sem=send_sems.at[t, mask - 1],
                    recv_sem=recv_sems.at[t, mask - 1], device_id=_peer(x, y, c, mask),
                    device_id_type=pl.DeviceIdType.MESH).wait_recv()
        for cp in sends:
            cp.wait_send()
        for cp in local:
            cp.wait()

    out_shape = [jax.ShapeDtypeStruct((a.shape[0], N_DEV * a.shape[1], a.shape[2]), a.dtype) for a in row_shards]
    out_shape += [jax.ShapeDtypeStruct((N_DEV,) + a.shape, a.dtype) for a in stack_shards]
    return pl.pallas_call(
        body, name="all_gather_weights", in_specs=[_ANY] * n_all, out_specs=[_ANY] * n_all, out_shape=out_shape,
        scratch_shapes=[pltpu.SemaphoreType.DMA((n_all, N_DEV - 1)), pltpu.SemaphoreType.DMA((n_all, N_DEV - 1)),
                        pltpu.SemaphoreType.DMA((n_all,))],
    )(*shards)


def _reduce_scatter(row_full, stack_full, bcast):
    n_row, n_stack = len(row_full), len(stack_full)
    n_all = n_row + n_stack + len(bcast)
    fulls = list(row_full) + list(stack_full) + list(bcast)

    def body(*refs):
        ins, outs = refs[:n_all], refs[n_all:2 * n_all]
        send_sems, recv_sems, local_sems = refs[2 * n_all:]
        x, y, c, me = _mesh_pos()

        def part(t, slot):
            if t < n_row:
                rows = fulls[t].shape[1] // N_DEV
                return ins[t].at[:, pl.ds(slot * rows, rows), :]
            if t < n_row + n_stack:
                return ins[t].at[slot]
            return ins[t]

        local = [pltpu.make_async_copy(part(t, me), outs[t].at[me], local_sems.at[t]) for t in range(n_all)]
        for cp in local:
            cp.start()
        sends = []
        for mask in range(1, N_DEV):
            for t in range(n_all):
                cp = pltpu.make_async_remote_copy(
                    src_ref=part(t, me ^ mask), dst_ref=outs[t].at[me], send_sem=send_sems.at[t, mask - 1],
                    recv_sem=recv_sems.at[t, mask - 1], device_id=_peer(x, y, c, mask),
                    device_id_type=pl.DeviceIdType.MESH)
                cp.start()
                sends.append(cp)
        for mask in range(1, N_DEV):
            for t in range(n_all):
                pltpu.make_async_remote_copy(
                    src_ref=part(t, me), dst_ref=outs[t].at[me ^ mask], send_sem=send_sems.at[t, mask - 1],
                    recv_sem=recv_sems.at[t, mask - 1], device_id=_peer(x, y, c, mask),
                    device_id_type=pl.DeviceIdType.MESH).wait_recv()
        for cp in sends:
            cp.wait_send()
        for cp in local:
            cp.wait()

    out_shape = [jax.ShapeDtypeStruct((N_DEV, a.shape[0], a.shape[1] // N_DEV, a.shape[2]), a.dtype) for a in row_full]
    out_shape += [jax.ShapeDtypeStruct(a.shape, a.dtype) for a in stack_full]
    out_shape += [jax.ShapeDtypeStruct((N_DEV,) + a.shape, a.dtype) for a in bcast]
    return pl.pallas_call(
        body, name="reduce_scatter_grads", in_specs=[_ANY] * n_all, out_specs=[_ANY] * n_all, out_shape=out_shape,
        scratch_shapes=[pltpu.SemaphoreType.DMA((n_all, N_DEV - 1)), pltpu.SemaphoreType.DMA((n_all, N_DEV - 1)),
                        pltpu.SemaphoreType.DMA((n_all,))],
    )(*fulls)


def _adamw(slots, w, m, v, name):
    shape = w.shape
    cols = shape[-1]
    rows = math.prod(shape[:-1])
    tr = _pick(rows, (64, 32, 16, 8))
    c1 = 1.0 - ADAM_B1 ** ADAM_STEP
    c2 = 1.0 - ADAM_B2 ** ADAM_STEP

    def body(s_ref, w_ref, m_ref, v_ref, g_ref, d_ref, nm_ref, nv_ref):
        g = s_ref[0]
        for k in range(1, N_DEV):
            g = g + s_ref[k]
        nm = ADAM_B1 * m_ref[...] + (1.0 - ADAM_B1) * g
        nv = ADAM_B2 * v_ref[...] + (1.0 - ADAM_B2) * (g * g)
        g_ref[...] = g
        nm_ref[...] = nm
        nv_ref[...] = nv
        d_ref[...] = -ADAM_LR * ((nm / c1) / (jnp.sqrt(nv / c2) + ADAM_EPS) + ADAM_WD * w_ref[...])

    row = pl.BlockSpec((tr, cols), lambda i: (i, 0))
    out = jax.ShapeDtypeStruct((rows, cols), F32)
    outs = pl.pallas_call(
        body, name=name, grid=(rows // tr,),
        in_specs=[pl.BlockSpec((N_DEV, tr, cols), lambda i: (0, i, 0)), row, row, row],
        out_specs=[row] * 4, out_shape=[out] * 4, compiler_params=_cp(("parallel",)),
    )(slots.reshape(N_DEV, rows, cols), w.reshape(rows, cols), m.reshape(rows, cols), v.reshape(rows, cols))
    return [o.reshape(shape) for o in outs]


_SMALL = ("ln_in_g", "ln_in_b", "mem_ln_g", "mem_ln_b", "b_forget", "mla_q_norm_g", "mla_kv_norm_g", "ln_g", "ln_b")
_ORDER = ("ln_in_g", "ln_in_b", "mem_ln_g", "mem_ln_b", "w_in", "b_forget", "mla_q_norm_g", "w_mla_q_up",
          "mla_kv_norm_g", "w_mla_kv_up", "w_mem_kv", "w_out", "ln_g", "ln_b")


def _pack_small(d):
    flat = jnp.concatenate([d[n].reshape(-1) for n in _SMALL])
    n = flat.shape[0]
    padded = ((n + 8 * LANE - 1) // (8 * LANE)) * (8 * LANE)
    return jnp.pad(flat, (0, padded - n)).reshape(-1, LANE)


def _unpack_small(packed, like):
    flat, out, off = packed.reshape(-1), {}, 0
    for n in _SMALL:
        size = math.prod(like[n].shape)
        out[n] = flat[off:off + size].reshape(like[n].shape)
        off += size
    return out


def _unstack_cols(g):
    n, l, r, c = g.shape
    return g.transpose(1, 2, 0, 3).reshape(l, r, n * c)


def _stack_cols(g):
    l, r, nc = g.shape
    return g.reshape(l, r, N_DEV, nc // N_DEV).transpose(2, 0, 1, 3)


def kernel(x, mem, ln_in_g, ln_in_b, mem_ln_g, mem_ln_b, w_in, b_forget, mla_q_norm_g, w_mla_q_up, mla_kv_norm_g, w_mla_kv_up, w_mem_kv, w_out, ln_g, ln_b, loss_target, m_ln_in_g, m_ln_in_b, m_mem_ln_g, m_mem_ln_b, m_w_in, m_b_forget, m_mla_q_norm_g, m_w_mla_q_up, m_mla_kv_norm_g, m_w_mla_kv_up, m_w_mem_kv, m_w_out, m_ln_g, m_ln_b, v_ln_in_g, v_ln_in_b, v_mem_ln_g, v_mem_ln_b, v_w_in, v_b_forget, v_mla_q_norm_g, v_w_mla_q_up, v_mla_kv_norm_g, v_w_mla_kv_up, v_w_mem_kv, v_w_out, v_ln_g, v_ln_b):
    w_shard = dict(ln_in_g=ln_in_g, ln_in_b=ln_in_b, mem_ln_g=mem_ln_g, mem_ln_b=mem_ln_b, w_in=w_in,
                   b_forget=b_forget, mla_q_norm_g=mla_q_norm_g, w_mla_q_up=w_mla_q_up,
                   mla_kv_norm_g=mla_kv_norm_g, w_mla_kv_up=w_mla_kv_up, w_mem_kv=w_mem_kv, w_out=w_out,
                   ln_g=ln_g, ln_b=ln_b)
    m_shard = dict(ln_in_g=m_ln_in_g, ln_in_b=m_ln_in_b, mem_ln_g=m_mem_ln_g, mem_ln_b=m_mem_ln_b, w_in=m_w_in,
                   b_forget=m_b_forget, mla_q_norm_g=m_mla_q_norm_g, w_mla_q_up=m_w_mla_q_up,
                   mla_kv_norm_g=m_mla_kv_norm_g, w_mla_kv_up=m_w_mla_kv_up, w_mem_kv=m_w_mem_kv, w_out=m_w_out,
                   ln_g=m_ln_g, ln_b=m_ln_b)
    v_shard = dict(ln_in_g=v_ln_in_g, ln_in_b=v_ln_in_b, mem_ln_g=v_mem_ln_g, mem_ln_b=v_mem_ln_b, w_in=v_w_in,
                   b_forget=v_b_forget, mla_q_norm_g=v_mla_q_norm_g, w_mla_q_up=v_w_mla_q_up,
                   mla_kv_norm_g=v_mla_kv_norm_g, w_mla_kv_up=v_w_mla_kv_up, w_mem_kv=v_w_mem_kv, w_out=v_w_out,
                   ln_g=v_ln_g, ln_b=v_ln_b)

    g_in, g_mem, g_out, g_qup, g_kvup = _all_gather([w_in, w_mem_kv, w_out], [w_mla_q_up, w_mla_kv_up])
    full = dict(w_shard)
    full.update(w_in=g_in, w_mem_kv=g_mem, w_out=g_out, w_mla_q_up=_unstack_cols(g_qup),
                w_mla_kv_up=_unstack_cols(g_kvup))

    loss_local, (grad_w, grad_x) = jax.value_and_grad(_trunk_loss, argnums=(0, 1))(
        full, x[0], mem[0], loss_target[0])

    s_in, s_mem, s_out, s_qup, s_kvup, s_small = _reduce_scatter(
        [grad_w["w_in"], grad_w["w_mem_kv"], grad_w["w_out"]],
        [_stack_cols(grad_w["w_mla_q_up"]), _stack_cols(grad_w["w_mla_kv_up"])],
        [_pack_small(grad_w)])

    res = {}
    for name, slots in (("w_in", s_in), ("w_mem_kv", s_mem), ("w_out", s_out), ("w_mla_q_up", s_qup),
                        ("w_mla_kv_up", s_kvup)):
        res[name] = _adamw(slots, w_shard[name], m_shard[name], v_shard[name], "adamw_" + name)
    small = _adamw(s_small, _pack_small(w_shard), _pack_small(m_shard), _pack_small(v_shard), "adamw_small")
    small = [_unpack_small(a, w_shard) for a in small]
    for name in _SMALL:
        res[name] = [a[name] for a in small]

    loss = lax.psum(loss_local, MESH_AXES)
    outs = [loss, grad_x[None]]
    for k in range(4):
        outs += [res[name][k] for name in _ORDER]
    return tuple(outs)
```

```python
import functools
import math

import jax
import jax.numpy as jnp
from jax import lax
from jax.experimental import pallas as pl
from jax.experimental.pallas import tpu as pltpu

F32 = jnp.float32
BF16 = jnp.bfloat16

D_MODEL = 1024
DEPTH = 2
GROUP_W = 256
N_HEADS = 4
HEAD_DIM = 64
MLA_Q_RANK = 256
MLA_KV_RANK = 128
MLA_NOPE = 64
MLA_ROPE = 32
MLA_V = 64
ROPE_THETA = 10000.0
LN_EPS = 1e-5
RMS_EPS = 1e-6
DEEPNORM_ALPHA = (2 * DEPTH) ** 0.25
SPLIT_SIZES = (256, 256, 256, 4, 256, 256, 256, 256, 128, 32, 256, 1024)
IN_COLS = sum(SPLIT_SIZES)
_ORIG_OFF = [sum(SPLIT_SIZES[:i]) for i in range(len(SPLIT_SIZES))]
_PERM = (("fq", 0), ("fk", 1), ("fv", 2), ("sq", 4), ("sk", 5), ("sv", 6), ("c_q", 7), ("c_kv", 8),
         ("mq", 10), ("gate", 11), ("k_rot", 9), ("f_logit", 3))
LANE = 128
PROJ_COLS = ((IN_COLS + LANE - 1) // LANE) * LANE

ADAM_LR = 0.001
ADAM_B1 = 0.9
ADAM_B2 = 0.999
ADAM_EPS = 1e-08
ADAM_WD = 0.01
ADAM_STEP = 10

N_DEV = 8
MESH_AXES = ("x", "y", "c")
VMEM_LIMIT = 48 * 1024 * 1024
ATTN_VMEM_LIMIT = 56 * 1024 * 1024
ATTN_BQ = 512
ATTN_BK = 512
CUMSUM_CHUNK = 256
NEG_BIG = -1e30

_NT = (((1,), (1,)), ((), ()))
_TN = (((0,), (0,)), ((), ()))
_NN = (((1,), (0,)), ((), ()))


def _cp(sem, vmem=VMEM_LIMIT):
    return pltpu.CompilerParams(dimension_semantics=sem, vmem_limit_bytes=vmem)


def _dot(a, b, dims=_NN):
    return lax.dot_general(a, b, dims, preferred_element_type=F32)


def _pick(n, cands):
    for c in cands:
        if c <= n and n % c == 0:
            return c
    return n


def _matmul(a, b, mode, name, tm=None, tn=None, tk=None):
    if mode == "nn":
        (M, K), (K2, N) = a.shape, b.shape
    elif mode == "nt":
        (M, K), (N, K2) = a.shape, b.shape
    else:
        (K, M), (K2, N) = a.shape, b.shape
    assert K == K2, (a.shape, b.shape, mode)
    tm = tm or _pick(M, (1024, 512, 256, 128, 64, 32, 16, 8))
    tn = tn or (N if N <= 1024 else _pick(N, (1024, 896, 768, 640, 512, 384, 256, 128)))
    tk = tk or (K if K <= 1024 else _pick(K, (512, 256, 128)))
    assert M % tm == 0 and N % tn == 0 and K % tk == 0, (M, N, K, tm, tn, tk)
    nk = K // tk
    dims = {"nn": _NN, "nt": _NT, "tn": _TN}[mode]

    def body(a_ref, b_ref, o_ref, acc_ref):
        part = _dot(a_ref[...].astype(BF16), b_ref[...].astype(BF16), dims)
        if nk == 1:
            o_ref[...] = part
        else:
            k = pl.program_id(2)

            @pl.when(k == 0)
            def _():
                acc_ref[...] = part

            @pl.when(k > 0)
            def _():
                acc_ref[...] += part

            @pl.when(k == nk - 1)
            def _():
                o_ref[...] = acc_ref[...]

    if mode == "nn":
        a_spec = pl.BlockSpec((tm, tk), lambda j, i, k: (i, k))
        b_spec = pl.BlockSpec((tk, tn), lambda j, i, k: (k, j))
    elif mode == "nt":
        a_spec = pl.BlockSpec((tm, tk), lambda j, i, k: (i, k))
        b_spec = pl.BlockSpec((tn, tk), lambda j, i, k: (j, k))
    else:
        a_spec = pl.BlockSpec((tk, tm), lambda j, i, k: (k, i))
        b_spec = pl.BlockSpec((tk, tn), lambda j, i, k: (k, j))
    acc_shape = (tm, tn) if nk > 1 else (8, 128)
    return pl.pallas_call(
        body, name=name, grid=(N // tn, M // tm, nk),
        in_specs=[a_spec, b_spec],
        out_specs=pl.BlockSpec((tm, tn), lambda j, i, k: (i, j)),
        out_shape=jax.ShapeDtypeStruct((M, N), F32),
        scratch_shapes=[pltpu.VMEM(acc_shape, F32)],
        compiler_params=_cp(("parallel", "parallel", "arbitrary")),
    )(a, b)


def _make_mm(name):
    @jax.custom_vjp
    def mm(a, w):
        return _matmul(a, w, "nn", name + "_fwd")

    def fwd(a, w):
        return mm(a, w), (a, w)

    def bwd(res, dy):
        a, w = res
        da = _matmul(dy, w, "nt", name + "_dx")
        dw = _matmul(a, dy, "tn", name + "_dw")
        return da, dw

    mm.defvjp(fwd, bwd)
    return mm


def _row_tile(rows):
    return _pick(rows, (512, 256, 128, 64, 32, 16, 8))


def _ln_stats(u):
    mu = jnp.mean(u, axis=-1, keepdims=True)
    d = u - mu
    var = jnp.mean(d * d, axis=-1, keepdims=True)
    return d, lax.rsqrt(var + LN_EPS)


def _ln_fwd_call(x, res, g, b, name):
    rows, dm = x.shape
    tr = _row_tile(rows)
    has_res = res is not None

    def body(*refs):
        if has_res:
            x_ref, r_ref, g_ref, b_ref, o_ref = refs
            u = DEEPNORM_ALPHA * r_ref[...] + x_ref[...]
        else:
            x_ref, g_ref, b_ref, o_ref = refs
            u = x_ref[...]
        d, rstd = _ln_stats(u)
        o_ref[...] = d * rstd * g_ref[...] + b_ref[...]

    row = pl.BlockSpec((tr, dm), lambda i: (i, 0))
    vec = pl.BlockSpec((1, dm), lambda i: (0, 0))
    args = (x, res) if has_res else (x,)
    return pl.pallas_call(
        body, name=name, grid=(rows // tr,),
        in_specs=[row] * len(args) + [vec, vec], out_specs=row,
        out_shape=jax.ShapeDtypeStruct((rows, dm), F32),
        compiler_params=_cp(("parallel",)),
    )(*args, g.reshape(1, dm), b.reshape(1, dm))


def _ln_bwd_call(dy, x, res, g, name):
    rows, dm = x.shape
    tr = _row_tile(rows)
    has_res = res is not None

    def body(*refs):
        if has_res:
            dy_ref, x_ref, r_ref, g_ref, dx_ref, dr_ref, dg_ref, db_ref = refs
            u = DEEPNORM_ALPHA * r_ref[...] + x_ref[...]
        else:
            dy_ref, x_ref, g_ref, dx_ref, dg_ref, db_ref = refs
            u = x_ref[...]
        i = pl.program_id(0)
        d, rstd = _ln_stats(u)
        xhat = d * rstd
        dyv = dy_ref[...]
        dxh = dyv * g_ref[...]
        m1 = jnp.mean(dxh, axis=-1, keepdims=True)
        m2 = jnp.mean(dxh * xhat, axis=-1, keepdims=True)
        du = rstd * (dxh - m1 - xhat * m2)
        dx_ref[...] = du
        if has_res:
            dr_ref[...] = DEEPNORM_ALPHA * du
        pg = jnp.sum(dyv * xhat, axis=0, keepdims=True)
        pb = jnp.sum(dyv, axis=0, keepdims=True)

        @pl.when(i == 0)
        def _():
            dg_ref[...] = pg
            db_ref[...] = pb

        @pl.when(i > 0)
        def _():
            dg_ref[...] += pg
            db_ref[...] += pb

    row = pl.BlockSpec((tr, dm), lambda i: (i, 0))
    vec = pl.BlockSpec((1, dm), lambda i: (0, 0))
    big = jax.ShapeDtypeStruct((rows, dm), F32)
    small = jax.ShapeDtypeStruct((1, dm), F32)
    args = (dy, x, res) if has_res else (dy, x)
    n_big = 2 if has_res else 1
    outs = pl.pallas_call(
        body, name=name, grid=(rows // tr,),
        in_specs=[row] * len(args) + [vec],
        out_specs=[row] * n_big + [vec, vec],
        out_shape=[big] * n_big + [small, small],
        compiler_params=_cp(("arbitrary",)),
    )(*args, g.reshape(1, dm))
    return outs


def _make_ln(name, has_res):
    if has_res:
        @jax.custom_vjp
        def ln(x, res, g, b):
            return _ln_fwd_call(x, res, g, b, name + "_fwd")

        def fwd(x, res, g, b):
            return ln(x, res, g, b), (x, res, g)

        def bwd(saved, dy):
            x, res, g = saved
            dx, dr, dg, db = _ln_bwd_call(dy, x, res, g, name + "_bwd")
            return dx, dr, dg.reshape(-1), db.reshape(-1)
    else:
        @jax.custom_vjp
        def ln(x, g, b):
            return _ln_fwd_call(x, None, g, b, name + "_fwd")

        def fwd(x, g, b):
            return ln(x, g, b), (x, g)

        def bwd(saved, dy):
            x, g = saved
            dx, dg, db = _ln_bwd_call(dy, x, None, g, name + "_bwd")
            return dx, dg.reshape(-1), db.reshape(-1)

    ln.defvjp(fwd, bwd)
    return ln


def _rms_fwd_call(x, g, name):
    rows, dm = x.shape
    tr = _row_tile(rows)

    def body(x_ref, g_ref, o_ref):
        xv = x_ref[...]
        rstd = lax.rsqrt(jnp.mean(xv * xv, axis=-1, keepdims=True) + RMS_EPS)
        o_ref[...] = xv * rstd * g_ref[...]

    row = pl.BlockSpec((tr, dm), lambda i: (i, 0))
    vec = pl.BlockSpec((1, dm), lambda i: (0, 0))
    return pl.pallas_call(
        body, name=name, grid=(rows // tr,), in_specs=[row, vec], out_specs=row,
        out_shape=jax.ShapeDtypeStruct((rows, dm), F32), compiler_params=_cp(("parallel",)),
    )(x, g.reshape(1, dm))


def _rms_bwd_call(dy, x, g, name):
    rows, dm = x.shape
    tr = _row_tile(rows)

    def body(dy_ref, x_ref, g_ref, dx_ref, dg_ref):
        i = pl.program_id(0)
        xv = x_ref[...]
        dyv = dy_ref[...]
        rstd = lax.rsqrt(jnp.mean(xv * xv, axis=-1, keepdims=True) + RMS_EPS)
        xhat = xv * rstd
        dxh = dyv * g_ref[...]
        m2 = jnp.mean(dxh * xhat, axis=-1, keepdims=True)
        dx_ref[...] = rstd * (dxh - xhat * m2)
        pg = jnp.sum(dyv * xhat, axis=0, keepdims=True)

        @pl.when(i == 0)
        def _():
            dg_ref[...] = pg

        @pl.when(i > 0)
        def _():
            dg_ref[...] += pg

    row = pl.BlockSpec((tr, dm), lambda i: (i, 0))
    vec = pl.BlockSpec((1, dm), lambda i: (0, 0))
    return pl.pallas_call(
        body, name=name, grid=(rows // tr,), in_specs=[row, row, vec], out_specs=[row, vec],
        out_shape=[jax.ShapeDtypeStruct((rows, dm), F32), jax.ShapeDtypeStruct((1, dm), F32)],
        compiler_params=_cp(("arbitrary",)),
    )(dy, x, g.reshape(1, dm))


def _make_rms(name):
    @jax.custom_vjp
    def rms(x, g):
        return _rms_fwd_call(x, g, name + "_fwd")

    def fwd(x, g):
        return rms(x, g), (x, g)

    def bwd(saved, dy):
        x, g = saved
        dx, dg = _rms_bwd_call(dy, x, g, name + "_bwd")
        return dx, dg.reshape(-1)

    rms.defvjp(fwd, bwd)
    return rms


def _sigmoid(x):
    return 1.0 / (1.0 + jnp.exp(-x))


def _gate_fwd_call(mixed, gate, name):
    rows, dm = mixed.shape
    tr = _row_tile(rows)

    def body(m_ref, g_ref, o_ref):
        gv = g_ref[...]
        o_ref[...] = m_ref[...] * (gv * _sigmoid(gv))

    row = pl.BlockSpec((tr, dm), lambda i: (i, 0))
    return pl.pallas_call(
        body, name=name, grid=(rows // tr,), in_specs=[row, row], out_specs=row,
        out_shape=jax.ShapeDtypeStruct((rows, dm), F32), compiler_params=_cp(("parallel",)),
    )(mixed, gate)


def _gate_bwd_call(dy, mixed, gate, name):
    rows, dm = mixed.shape
    tr = _row_tile(rows)

    def body(dy_ref, m_ref, g_ref, dm_ref, dg_ref):
        gv = g_ref[...]
        dyv = dy_ref[...]
        sg = _sigmoid(gv)
        dm_ref[...] = dyv * (gv * sg)
        dg_ref[...] = dyv * m_ref[...] * (sg * (1.0 + gv * (1.0 - sg)))

    row = pl.BlockSpec((tr, dm), lambda i: (i, 0))
    out = jax.ShapeDtypeStruct((rows, dm), F32)
    return pl.pallas_call(
        body, name=name, grid=(rows // tr,), in_specs=[row, row, row], out_specs=[row, row],
        out_shape=[out, out], compiler_params=_cp(("parallel",)),
    )(dy, mixed, gate)


def _make_gate(name):
    @jax.custom_vjp
    def gate_mul(mixed, gate):
        return _gate_fwd_call(mixed, gate, name + "_fwd")

    def fwd(mixed, gate):
        return gate_mul(mixed, gate), (mixed, gate)

    def bwd(saved, dy):
        mixed, gate = saved
        dmix, dgate = _gate_bwd_call(dy, mixed, gate, name + "_bwd")
        return dmix, dgate

    gate_mul.defvjp(fwd, bwd)
    return gate_mul


def _loss_call(y, t, name):
    rows, dm = y.shape
    tr = _row_tile(rows)

    def body(y_ref, t_ref, l_ref, d_ref):
        i = pl.program_id(0)
        e = y_ref[...] - t_ref[...]
        d_ref[...] = e * (1.0 / dm)
        part = 0.5 * jnp.sum(jnp.mean(e * e, axis=-1, keepdims=True), axis=0, keepdims=True)

        @pl.when(i == 0)
        def _():
            l_ref[...] = part

        @pl.when(i > 0)
        def _():
            l_ref[...] += part

    row = pl.BlockSpec((tr, dm), lambda i: (i, 0))
    one = pl.BlockSpec((1, 1), lambda i: (0, 0))
    return pl.pallas_call(
        body, name=name, grid=(rows // tr,), in_specs=[row, row], out_specs=[one, row],
        out_shape=[jax.ShapeDtypeStruct((1, 1), F32), jax.ShapeDtypeStruct((rows, dm), F32)],
        compiler_params=_cp(("arbitrary",)),
    )(y, t)


@jax.custom_vjp
def _loss_op(y, t):
    return _loss_call(y, t, "loss_head")[0][0, 0]


def _loss_fwd(y, t):
    l, d = _loss_call(y, t, "loss_head")
    return l[0, 0], d


def _loss_bwd(d, ct):
    return ct * d, jnp.zeros_like(d)


_loss_op.defvjp(_loss_fwd, _loss_bwd)


def _attn_blocks(S, Sk):
    bq, bk = min(ATTN_BQ, S), min(ATTN_BK, Sk)
    assert S % bq == 0 and Sk % bk == 0
    return bq, bk


def _valid_t(i, j, bq, bk, strict):
    key = j * bk + lax.broadcasted_iota(jnp.int32, (bk, bq), 0)
    qry = i * bq + lax.broadcasted_iota(jnp.int32, (bk, bq), 1)
    return (key < qry) if strict else (key <= qry)


def _sm_fwd_t(qT, k, vT, r, cmul, causal, name):
    H, DK, S = qT.shape
    Sk, dv = k.shape[1], vT.shape[1]
    bq, bk = _attn_blocks(S, Sk)
    nq, nkb = S // bq, Sk // bk
    use_r = r is not None
    if causal:
        assert S == Sk and bq == bk

    def body(*refs):
        if use_r:
            qT_ref, k_ref, vT_ref, r_ref, oT_ref, lse_ref = refs
        else:
            qT_ref, k_ref, vT_ref, oT_ref, lse_ref = refs
        i = pl.program_id(1)
        qTb = qT_ref[...]

        def scores(j):
            off = pl.multiple_of(j * bk, bk)
            return _dot(k_ref[pl.ds(off, bk), :], qTb)

        def blk(j, carry, raw, masked):
            m, l, acc = carry
            off = pl.multiple_of(j * bk, bk)
            sT = raw if cmul == 1.0 else raw * cmul
            if masked:
                sT = jnp.where(_valid_t(i, j, bq, bk, False), sT, NEG_BIG)
            cm = jnp.max(sT, axis=0, keepdims=True)
            if use_r:
                cm = cm + r_ref[...]
            m_new = jnp.maximum(m, cm)
            shift = (m_new - r_ref[...]) if use_r else m_new
            p = jnp.exp(sT - shift)
            a = jnp.exp(m - m_new)
            l = a * l + jnp.sum(p, axis=0, keepdims=True)
            acc = a * acc + _dot(vT_ref[:, pl.ds(off, bk)], p.astype(BF16))
            return m_new, l, acc

        def step(j, c):
            nxt = scores(j + 1)
            return blk(j, c[:3], c[3], False) + (nxt,)

        last = i if causal else nkb - 1
        carry = (jnp.full((1, bq), NEG_BIG, F32), jnp.zeros((1, bq), F32), jnp.zeros((dv, bq), F32), scores(0))
        carry = lax.fori_loop(0, last, step, carry)
        m, l, acc = blk(last, carry[:3], carry[3], causal)
        oT_ref[...] = acc / l
        lse_ref[...] = m + jnp.log(l)

    qcol = lambda d: pl.BlockSpec((None, d, bq), lambda h, i: (h, 0, i))
    in_specs = [qcol(DK), pl.BlockSpec((None, Sk, DK), lambda h, i: (h, 0, 0)),
                pl.BlockSpec((None, dv, Sk), lambda h, i: (h, 0, 0))]
    args = [qT, k, vT]
    if use_r:
        in_specs.append(qcol(1))
        args.append(r)
    return pl.pallas_call(
        body, name=name, grid=(H, nq), in_specs=in_specs, out_specs=[qcol(dv), qcol(1)],
        out_shape=[jax.ShapeDtypeStruct((H, dv, S), F32), jax.ShapeDtypeStruct((H, 1, S), F32)],
        compiler_params=_cp(("parallel", "arbitrary"), ATTN_VMEM_LIMIT),
    )(*args)


def _sm_bwd_t(qT, qn, k, kT, v, oT, lse, doT, do, r, cmul, gscale, causal, name, fcol=None):
    H, DK, S = qT.shape
    Sk, dv = k.shape[1], v.shape[2]
    bq, bk = _attn_blocks(S, Sk)
    nq, nkb = S // bq, Sk // bk
    use_r = r is not None

    def body(*refs):
        if use_r:
            (qT_ref, qn_ref, k_ref, kT_ref, v_ref, oT_ref, lse_ref, doT_ref, do_ref, r_ref,
             dqT_ref, dk_ref, dv_ref, dr_ref) = refs
        else:
            (qT_ref, qn_ref, k_ref, kT_ref, v_ref, oT_ref, lse_ref, doT_ref, do_ref,
             dqT_ref, dk_ref, dv_ref) = refs
        i = pl.program_id(1)

        @pl.when(i == 0)
        def _():
            dk_ref[...] = jnp.zeros_like(dk_ref)
            dv_ref[...] = jnp.zeros_like(dv_ref)

        qTb = qT_ref[...]
        qnb = qn_ref[...]
        dob = do_ref[...]
        doTf = doT_ref[...]
        doTb = doTf.astype(BF16)
        delta = jnp.sum(doTf * oT_ref[...], axis=0, keepdims=True)
        shift = (lse_ref[...] - r_ref[...]) if use_r else lse_ref[...]

        def blk(j, carry, masked):
            dq, dr = carry
            off = pl.multiple_of(j * bk, bk)
            kb = k_ref[pl.ds(off, bk), :]
            sT = _dot(kb, qTb)
            if cmul != 1.0:
                sT = sT * cmul
            if masked:
                sT = jnp.where(_valid_t(i, j, bq, bk, False), sT, NEG_BIG)
            p = jnp.exp(sT - shift)
            dp = _dot(v_ref[pl.ds(off, bk), :], doTb)
            ds = p * (dp - delta)
            dsb = (ds * gscale).astype(BF16) if gscale != 1.0 else ds.astype(BF16)
            dv_ref[pl.ds(off, bk), :] += _dot(p.astype(BF16), dob)
            dkb = _dot(dsb, qnb)
            if use_r:
                dr = dr + jnp.sum(ds, axis=0, keepdims=True)
                lane = lax.broadcasted_iota(jnp.int32, dkb.shape, 1)
                dkb = jnp.where(lane == fcol, jnp.sum(ds, axis=1, keepdims=True), dkb)
            dk_ref[pl.ds(off, bk), :] += dkb
            return dq + _dot(kT_ref[:, pl.ds(off, bk)], dsb), dr

        carry = (jnp.zeros((DK, bq), F32), jnp.zeros((1, bq), F32))
        if causal:
            carry = lax.fori_loop(0, i, lambda j, c: blk(j, c, False), carry)
            carry = blk(i, carry, True)
        else:
            carry = lax.fori_loop(0, nkb, lambda j, c: blk(j, c, False), carry)
        dqT_ref[...] = carry[0]
        if use_r:
            dr_ref[...] = carry[1]

    qcol = lambda d: pl.BlockSpec((None, d, bq), lambda h, i: (h, 0, i))
    qrow = lambda d: pl.BlockSpec((None, bq, d), lambda h, i: (h, i, 0))
    krow = lambda d: pl.BlockSpec((None, Sk, d), lambda h, i: (h, 0, 0))
    in_specs = [qcol(DK), qrow(DK), krow(DK), pl.BlockSpec((None, DK, Sk), lambda h, i: (h, 0, 0)), krow(dv),
                qcol(dv), qcol(1), qcol(dv), qrow(dv)]
    args = [qT, qn, k, kT, v, oT, lse, doT, do]
    out_specs = [qcol(DK), krow(DK), krow(dv)]
    out_shape = [jax.ShapeDtypeStruct((H, DK, S), F32), jax.ShapeDtypeStruct((H, Sk, DK), F32),
                 jax.ShapeDtypeStruct((H, Sk, dv), F32)]
    if use_r:
        in_specs.append(qcol(1))
        args.append(r)
        out_specs.append(qcol(1))
        out_shape.append(jax.ShapeDtypeStruct((H, 1, S), F32))
    return pl.pallas_call(
        body, name=name, grid=(H, nq), in_specs=in_specs, out_specs=out_specs, out_shape=out_shape,
        compiler_params=_cp(("parallel", "arbitrary"), ATTN_VMEM_LIMIT),
    )(*args)


def _tri(n, fn):
    r = lax.broadcasted_iota(jnp.int32, (n, n), 0)
    c = lax.broadcasted_iota(jnp.int32, (n, n), 1)
    return jnp.where(fn(r, c), 1.0, 0.0).astype(BF16)


def _key_cumsum(x, tri2, suffix, base):
    bk = x.shape[0]
    c = min(CUMSUM_CHUNK, bk)
    n = bk // c
    hi32 = lax.bitcast_convert_type(lax.bitcast_convert_type(x, jnp.int32) & jnp.int32(-65536), F32)
    hi = hi32.astype(BF16)
    lo = (x - hi32).astype(BF16)
    tot = [jnp.sum(x[a * c:(a + 1) * c], axis=0, keepdims=True) for a in range(n)]
    outs = []
    for a in range(n):
        row = base
        for t in (tot[a + 1:] if suffix else tot[:a]):
            row = row + t
        stacked = jnp.concatenate([hi[a * c:(a + 1) * c], lo[a * c:(a + 1) * c]], axis=0)
        outs.append(_dot(tri2, stacked) + row)
    total = tot[0]
    for t in tot[1:]:
        total = total + t
    return (outs[0] if n == 1 else jnp.concatenate(outs, axis=0)), total


def _tri2(n, fn):
    t = _tri(n, fn)
    return jnp.concatenate([t, t], axis=1)


def _sb_logs(z):
    neg_abs = lax.bitcast_convert_type(lax.bitcast_convert_type(z, jnp.int32) | jnp.int32(-2 ** 31), F32)
    ls = jnp.minimum(z, 0.0) - jnp.log(1.0 + jnp.exp(neg_abs))
    return ls, ls - z


def _sb_fwd_t(qT, k, vT, name):
    H, DK, S = qT.shape
    dv = vT.shape[1]
    bq, bk = _attn_blocks(S, S)
    assert bq == bk
    nq = S // bq
    c = min(CUMSUM_CHUNK, bk)

    def body(qT_ref, k_ref, vT_ref, oT_ref, lt_ref):
        i = pl.program_id(1)
        qTb = qT_ref[...]
        after = _tri2(c, lambda s, j: j > s)

        def blk(jj, carry, masked):
            rsum, acc = carry
            j = i - jj
            off = pl.multiple_of(j * bk, bk)
            z = _dot(k_ref[pl.ds(off, bk), :], qTb)
            ls, lk = _sb_logs(z)
            if masked:
                valid = _valid_t(i, j, bq, bk, True)
                lk = jnp.where(valid, lk, 0.0)
            tail, tot = _key_cumsum(lk, after, True, rsum)
            w = jnp.exp(ls + tail)
            if masked:
                w = jnp.where(valid, w, 0.0)
            return rsum + tot, acc + _dot(vT_ref[:, pl.ds(off, bk)], w.astype(BF16))

        carry = (jnp.zeros((1, bq), F32), jnp.zeros((dv, bq), F32))
        carry = blk(0, carry, True)
        carry = lax.fori_loop(1, i + 1, lambda jj, cr: blk(jj, cr, False), carry)
        lt_ref[...] = carry[0]
        oT_ref[...] = carry[1]

    qcol = lambda d: pl.BlockSpec((None, d, bq), lambda h, i: (h, 0, i))
    return pl.pallas_call(
        body, name=name, grid=(H, nq),
        in_specs=[qcol(DK), pl.BlockSpec((None, S, DK), lambda h, i: (h, 0, 0)),
                  pl.BlockSpec((None, dv, S), lambda h, i: (h, 0, 0))],
        out_specs=[qcol(dv), qcol(1)],
        out_shape=[jax.ShapeDtypeStruct((H, dv, S), F32), jax.ShapeDtypeStruct((H, 1, S), F32)],
        compiler_params=_cp(("parallel", "arbitrary"), ATTN_VMEM_LIMIT),
    )(qT, k, vT)


def _sb_bwd_t(qT, qn, k, kT, v, lt, doT, do, name):
    H, DK, S = qT.shape
    dv = v.shape[2]
    bq, bk = _attn_blocks(S, S)
    nq = S // bq
    c = min(CUMSUM_CHUNK, bk)

    def body(qT_ref, qn_ref, k_ref, kT_ref, v_ref, lt_ref, doT_ref, do_ref, dqT_ref, dk_ref, dv_ref):
        i = pl.program_id(1)

        @pl.when(i == 0)
        def _():
            dk_ref[...] = jnp.zeros_like(dk_ref)
            dv_ref[...] = jnp.zeros_like(dv_ref)

        qTb = qT_ref[...]
        qnb = qn_ref[...]
        dob = do_ref[...]
        doTb = doT_ref[...].astype(BF16)
        ltot = lt_ref[...]
        upto = _tri2(c, lambda s, j: j <= s)
        before = _tri2(c, lambda s, j: j < s)

        def blk(j, carry, masked):
            dq, rp, gp = carry
            off = pl.multiple_of(j * bk, bk)
            kb = k_ref[pl.ds(off, bk), :]
            z = _dot(kb, qTb)
            ls, lk = _sb_logs(z)
            if masked:
                valid = _valid_t(i, j, bq, bk, True)
                lk = jnp.where(valid, lk, 0.0)
            pin, ltb = _key_cumsum(lk, upto, False, rp - ltot)
            w = jnp.exp(ls - pin)
            if masked:
                w = jnp.where(valid, w, 0.0)
            g = _dot(v_ref[pl.ds(off, bk), :], doTb) * w
            cin, gtb = _key_cumsum(g, before, False, gp)
            sig = jnp.exp(ls)
            dz = g * (1.0 - sig) - cin * sig
            if masked:
                dz = jnp.where(valid, dz, 0.0)
            dzb = dz.astype(BF16)
            dv_ref[pl.ds(off, bk), :] += _dot(w.astype(BF16), dob)
            dk_ref[pl.ds(off, bk), :] += _dot(dzb, qnb)
            return dq + _dot(kT_ref[:, pl.ds(off, bk)], dzb), rp + ltb, gp + gtb

        carry = (jnp.zeros((DK, bq), F32), jnp.zeros((1, bq), F32), jnp.zeros((1, bq), F32))
        carry = lax.fori_loop(0, i, lambda j, cr: blk(j, cr, False), carry)
        carry = blk(i, carry, True)
        dqT_ref[...] = carry[0]

    qcol = lambda d: pl.BlockSpec((None, d, bq), lambda h, i: (h, 0, i))
    qrow = lambda d: pl.BlockSpec((None, bq, d), lambda h, i: (h, i, 0))
    krow = lambda d: pl.BlockSpec((None, S, d), lambda h, i: (h, 0, 0))
    return pl.pallas_call(
        body, name=name, grid=(H, nq),
        in_specs=[qcol(DK), qrow(DK), krow(DK), pl.BlockSpec((None, DK, S), lambda h, i: (h, 0, 0)), krow(dv),
                  qcol(1), qcol(dv), qrow(dv)],
        out_specs=[qcol(DK), krow(DK), krow(dv)],
        out_shape=[jax.ShapeDtypeStruct((H, DK, S), F32), jax.ShapeDtypeStruct((H, S, DK), F32),
                   jax.ShapeDtypeStruct((H, S, dv), F32)],
        compiler_params=_cp(("parallel", "arbitrary"), ATTN_VMEM_LIMIT),
    )(qT, qn, k, kT, v, lt, doT, do)


def _round_bf16(x):
    return lax.reduce_precision(x, exponent_bits=8, mantissa_bits=7)


def _split3(x):
    hi = _round_bf16(x)
    mid = _round_bf16(x - hi)
    lo = _round_bf16(x - hi - mid)
    return hi.astype(BF16), mid.astype(BF16), lo.astype(BF16)


def _pow2(x):
    m, _ = math.frexp(x)
    return m == 0.5


def _pad_last(x, n):
    return jnp.pad(x, [(0, 0)] * (x.ndim - 1) + [(0, n - x.shape[-1])])


def _layouts(q, k, scale, f_cum):
    d = q.shape[-1]
    pre = _pow2(scale)
    qh = jnp.transpose(q * scale if pre else q, (1, 0, 2)).astype(BF16)
    kh = jnp.transpose(k, (1, 0, 2)).astype(BF16)
    S, Sk, H = q.shape[0], k.shape[0], q.shape[1]
    if f_cum is not None:
        assert pre and d + 3 <= LANE
        ones = jnp.ones((H, S, 1), BF16)
        q_s = jnp.concatenate([qh, -ones, -ones, -ones], axis=-1)
        kh = jnp.concatenate([kh] + [t[:, :, None] for t in _split3(f_cum)], axis=-1)
    else:
        q_s = qh
    q_n = qh
    q_s, q_n, kh = _pad_last(q_s, LANE), _pad_last(q_n, LANE), _pad_last(kh, LANE)
    return q_n, jnp.transpose(q_s, (0, 2, 1)), kh, jnp.transpose(kh, (0, 2, 1)), pre


def _make_softmax_attn(name, scale, causal, use_f, d):
    pre = _pow2(scale)
    cmul = 1.0 if pre else scale
    gscale = 1.0 if pre else scale

    def run_fwd(q, k, v, f_cum):
        qn, qT, kn, kT, _ = _layouts(q, k, scale, f_cum)
        vn = jnp.transpose(v, (1, 0, 2)).astype(BF16)
        vT = jnp.transpose(vn, (0, 2, 1))
        r = f_cum[:, None, :] if use_f else None
        oT, lse = _sm_fwd_t(qT, kn, vT, r, cmul, causal, name + "_fwd")
        return jnp.transpose(oT, (2, 0, 1)), (qn, qT, kn, kT, vn, oT, lse, r)

    def run_bwd(saved, dout):
        qn, qT, kn, kT, vn, oT, lse, r = saved
        doT = jnp.transpose(dout, (1, 2, 0))
        do = jnp.transpose(dout, (1, 0, 2)).astype(BF16)
        outs = _sm_bwd_t(qT, qn, kn, kT, vn, oT, lse, doT, do, r, cmul, gscale, causal, name + "_bwd", d)
        dqT, dk, dv = outs[:3]
        dq = jnp.transpose(dqT[:, :d, :], (2, 0, 1))
        if pre:
            dq = dq * scale
        dkk = jnp.transpose(dk[:, :, :d], (1, 0, 2))
        dvv = jnp.transpose(dv, (1, 0, 2))
        if use_f:
            return dq, dkk, dvv, outs[3][:, 0, :] - dk[:, :, d]
        return dq, dkk, dvv

    if use_f:
        @jax.custom_vjp
        def attn(q, k, v, f_cum):
            return run_fwd(q, k, v, f_cum)[0]

        attn.defvjp(run_fwd, run_bwd)
    else:
        @jax.custom_vjp
        def attn(q, k, v):
            return run_fwd(q, k, v, None)[0]

        attn.defvjp(lambda q, k, v: run_fwd(q, k, v, None), run_bwd)
    return attn


def _make_sb_attn(name, scale, d):
    assert _pow2(scale)

    def run_fwd(q, k, v):
        qn, qT, kn, kT, _ = _layouts(q, k, scale, None)
        vn = jnp.transpose(v, (1, 0, 2)).astype(BF16)
        vT = jnp.transpose(vn, (0, 2, 1))
        oT, lt = _sb_fwd_t(qT, kn, vT, name + "_fwd")
        return jnp.transpose(oT, (2, 0, 1)), (qn, qT, kn, kT, vn, lt)

    def run_bwd(saved, dout):
        qn, qT, kn, kT, vn, lt = saved
        doT = jnp.transpose(dout, (1, 2, 0))
        do = jnp.transpose(dout, (1, 0, 2)).astype(BF16)
        dqT, dk, dv = _sb_bwd_t(qT, qn, kn, kT, vn, lt, doT, do, name + "_bwd")
        dq = jnp.transpose(dqT[:, :d, :], (2, 0, 1)) * scale
        return dq, jnp.transpose(dk[:, :, :d], (1, 0, 2)), jnp.transpose(dv, (1, 0, 2))

    @jax.custom_vjp
    def attn(q, k, v):
        return run_fwd(q, k, v)[0]

    attn.defvjp(run_fwd, run_bwd)
    return attn


def _rope(x, positions):
    half = x.shape[-1] // 2
    inv_freq = ROPE_THETA ** (-jnp.arange(half, dtype=F32) / half)
    ang = positions.astype(F32)[:, None] * inv_freq[None, :]
    ang = ang.reshape((ang.shape[0],) + (1,) * (x.ndim - 2) + (half,))
    cos, sin = jnp.cos(ang), jnp.sin(ang)
    x1, x2 = x[..., :half], x[..., half:]
    return jnp.concatenate([x1 * cos - x2 * sin, x1 * sin + x2 * cos], axis=-1)


def _permute_w_in(w):
    parts = [w[:, _ORIG_OFF[idx]:_ORIG_OFF[idx] + SPLIT_SIZES[idx]] for _, idx in _PERM]
    pad = jnp.zeros((w.shape[0], PROJ_COLS - IN_COLS), w.dtype)
    return jnp.concatenate(parts + [pad], axis=1)


@jax.custom_vjp
def _split_cols(proj):
    out, off = [], 0
    for _, idx in _PERM:
        out.append(proj[:, off:off + SPLIT_SIZES[idx]])
        off += SPLIT_SIZES[idx]
    return tuple(out)


def _split_cols_bwd(rows, cts):
    return (jnp.concatenate(list(cts) + [jnp.zeros((rows.shape[0], PROJ_COLS - IN_COLS), F32)], axis=1),)


_split_cols.defvjp(lambda proj: (_split_cols(proj), proj[:, :1]), _split_cols_bwd)


def _split_proj(proj):
    return {name: part for (name, _), part in zip(_PERM, _split_cols(proj))}


def _trunk_loss(wts, x2d, mem2d, target2d):
    s = x2d.shape[0]
    positions = jnp.arange(s)
    head_scale = HEAD_DIM ** -0.5
    mla_scale = (MLA_NOPE + MLA_ROPE) ** -0.5
    heads = lambda t: t.reshape(t.shape[0], N_HEADS, -1)

    h = _make_ln("ln_in", False)(x2d, wts["ln_in_g"], wts["ln_in_b"])
    mem_n = _make_ln("ln_mem", False)(mem2d, wts["mem_ln_g"], wts["mem_ln_b"])

    for l in range(DEPTH):
        tag = f"l{l}_"
        proj = _make_mm(tag + "proj")(h, _permute_w_in(wts["w_in"][l]))
        p = _split_proj(proj)

        log_f = jax.nn.log_sigmoid(p["f_logit"] + wts["b_forget"][l])
        f_cum = jnp.cumsum(log_f, axis=0).T
        out_fox = _make_softmax_attn(tag + "fox", head_scale, True, True, HEAD_DIM)(
            heads(p["fq"]), heads(p["fk"]), heads(p["fv"]), f_cum)

        out_sb = _make_sb_attn(tag + "sb", head_scale, HEAD_DIM)(heads(p["sq"]), heads(p["sk"]), heads(p["sv"]))

        cqn = _make_rms(tag + "rms_q")(p["c_q"], wts["mla_q_norm_g"][l])
        q_mla = _make_mm(tag + "q_up")(cqn, wts["w_mla_q_up"][l]).reshape(s, N_HEADS, MLA_NOPE + MLA_ROPE)
        ckvn = _make_rms(tag + "rms_kv")(p["c_kv"], wts["mla_kv_norm_g"][l])
        kv_mla = _make_mm(tag + "kv_up")(ckvn, wts["w_mla_kv_up"][l]).reshape(s, N_HEADS, MLA_NOPE + MLA_V)
        q_full = jnp.concatenate([q_mla[..., :MLA_NOPE], _rope(q_mla[..., MLA_NOPE:], positions)], axis=-1)
        k_rope = jnp.broadcast_to(_rope(p["k_rot"], positions)[:, None, :], (s, N_HEADS, MLA_ROPE))
        k_full = jnp.concatenate([kv_mla[..., :MLA_NOPE], k_rope], axis=-1)
        out_mla = _make_softmax_attn(tag + "mla", mla_scale, True, False, MLA_NOPE + MLA_ROPE)(
            q_full, k_full, kv_mla[..., MLA_NOPE:])

        mkv = _make_mm(tag + "mem_kv")(mem_n, wts["w_mem_kv"][l])
        out_mem = _make_softmax_attn(tag + "mem", head_scale, False, False, HEAD_DIM)(
            heads(p["mq"]), heads(mkv[:, :GROUP_W]), heads(mkv[:, GROUP_W:]))

        mixed = jnp.concatenate([o.reshape(s, GROUP_W) for o in (out_fox, out_sb, out_mla, out_mem)], axis=-1)
        gated = _make_gate(tag + "gate")(mixed, p["gate"])
        y = _make_mm(tag + "out")(gated, wts["w_out"][l])
        h = _make_ln(tag + "ln", True)(y, h, wts["ln_g"][l], wts["ln_b"][l])

    return _loss_op(h, target2d)


def _mesh_pos():
    x, y, c = (lax.axis_index(a) for a in MESH_AXES)
    return x, y, c, 4 * x + 2 * y + c


def _peer(x, y, c, mask):
    return (x ^ ((mask >> 2) & 1), y ^ ((mask >> 1) & 1), c ^ (mask & 1))


_ANY = pl.BlockSpec(memory_space=pl.ANY)


def _all_gather(row_shards, stack_shards):
    n_row, n_all = len(row_shards), len(row_shards) + len(stack_shards)
    shards = list(row_shards) + list(stack_shards)

    def body(*refs):
        ins, outs = refs[:n_all], refs[n_all:2 * n_all]
        send_sems, recv_sems, local_sems = refs[2 * n_all:]
        x, y, c, me = _mesh_pos()

        def window(t, slot):
            if t < n_row:
                rows = shards[t].shape[1]
                return outs[t].at[:, pl.ds(slot * rows, rows), :]
            return outs[t].at[slot]

        local = [pltpu.make_async_copy(ins[t], window(t, me), local_sems.at[t]) for t in range(n_all)]
        for cp in local:
            cp.start()
        sends = []
        for mask in range(1, N_DEV):
            for t in range(n_all):
                cp = pltpu.make_async_remote_copy(
                    src_ref=ins[t], dst_ref=window(t, me), send_sem=send_sems.at[t, mask - 1],
                    recv_sem=recv_sems.at[t, mask - 1], device_id=_peer(x, y, c, mask),
                    device_id_type=pl.DeviceIdType.MESH)
                cp.start()
                sends.append(cp)
        for mask in range(1, N_DEV):
            for t in range(n_all):
                pltpu.make_async_remote_copy(
                    src_ref=ins[t], dst_ref=window(t, me ^ mask), send_sem=send_sems.at[t, mask - 1],
                    recv_sem=recv_sems.at[t, mask - 1], device_id=_peer(x, y, c, mask),
                    device_id_type=pl.DeviceIdType.MESH).wait_recv()
        for cp in sends:
            cp.wait_send()
        for cp in local:
            cp.wait()

    out_shape = [jax.ShapeDtypeStruct((a.shape[0], N_DEV * a.shape[1], a.shape[2]), a.dtype) for a in row_shards]
    out_shape += [jax.ShapeDtypeStruct((N_DEV,) + a.shape, a.dtype) for a in stack_shards]
    return pl.pallas_call(
        body, name="all_gather_weights", in_specs=[_ANY] * n_all, out_specs=[_ANY] * n_all, out_shape=out_shape,
        scratch_shapes=[pltpu.SemaphoreType.DMA((n_all, N_DEV - 1)), pltpu.SemaphoreType.DMA((n_all, N_DEV - 1)),
                        pltpu.SemaphoreType.DMA((n_all,))],
    )(*shards)


def _reduce_scatter(row_full, stack_full, bcast):
    n_row, n_stack = len(row_full), len(stack_full)
    n_all = n_row + n_stack + len(bcast)
    fulls = list(row_full) + list(stack_full) + list(bcast)

    def body(*refs):
        ins, outs = refs[:n_all], refs[n_all:2 * n_all]
        send_sems, recv_sems, local_sems = refs[2 * n_all:]
        x, y, c, me = _mesh_pos()

        def part(t, slot):
            if t < n_row:
                rows = fulls[t].shape[1] // N_DEV
                return ins[t].at[:, pl.ds(slot * rows, rows), :]
            if t < n_row + n_stack:
                return ins[t].at[slot]
            return ins[t]

        local = [pltpu.make_async_copy(part(t, me), outs[t].at[me], local_sems.at[t]) for t in range(n_all)]
        for cp in local:
            cp.start()
        sends = []
        for mask in range(1, N_DEV):
            for t in range(n_all):
                cp = pltpu.make_async_remote_copy(
                    src_ref=part(t, me ^ mask), dst_ref=outs[t].at[me], send_sem=send_sems.at[t, mask - 1],
                    recv_sem=recv_sems.at[t, mask - 1], device_id=_peer(x, y, c, mask),
                    device_id_type=pl.DeviceIdType.MESH)
                cp.start()
                sends.append(cp)
        for mask in range(1, N_DEV):
            for t in range(n_all):
                pltpu.make_async_remote_copy(
                    src_ref=part(t, me), dst_ref=outs[t].at[me ^ mask], send_sem=send_sems.at[t, mask - 1],
                    recv_sem=recv_sems.at[t, mask - 1], device_id=_peer(x, y, c, mask),
                    device_id_type=pl.DeviceIdType.MESH).wait_recv()
        for cp in sends:
            cp.wait_send()
        for cp in local:
            cp.wait()

    out_shape = [jax.ShapeDtypeStruct((N_DEV, a.shape[0], a.shape[1] // N_DEV, a.shape[2]), a.dtype) for a in row_full]
    out_shape += [jax.ShapeDtypeStruct(a.shape, a.dtype) for a in stack_full]
    out_shape += [jax.ShapeDtypeStruct((N_DEV,) + a.shape, a.dtype) for a in bcast]
    return pl.pallas_call(
        body, name="reduce_scatter_grads", in_specs=[_ANY] * n_all, out_specs=[_ANY] * n_all, out_shape=out_shape,
        scratch_shapes=[pltpu.SemaphoreType.DMA((n_all, N_DEV - 1)), pltpu.SemaphoreType.DMA((n_all, N_DEV - 1)),
                        pltpu.SemaphoreType.DMA((n_all,))],
    )(*fulls)


def _adamw(slots, w, m, v, name):
    shape = w.shape
    cols = shape[-1]
    rows = math.prod(shape[:-1])
    tr = _pick(rows, (64, 32, 16, 8))
    c1 = 1.0 - ADAM_B1 ** ADAM_STEP
    c2 = 1.0 - ADAM_B2 ** ADAM_STEP

    def body(s_ref, w_ref, m_ref, v_ref, g_ref, d_ref, nm_ref, nv_ref):
        g = s_ref[0]
        for k in range(1, N_DEV):
            g = g + s_ref[k]
        nm = ADAM_B1 * m_ref[...] + (1.0 - ADAM_B1) * g
        nv = ADAM_B2 * v_ref[...] + (1.0 - ADAM_B2) * (g * g)
        g_ref[...] = g
        nm_ref[...] = nm
        nv_ref[...] = nv
        d_ref[...] = -ADAM_LR * ((nm / c1) / (jnp.sqrt(nv / c2) + ADAM_EPS) + ADAM_WD * w_ref[...])

    row = pl.BlockSpec((tr, cols), lambda i: (i, 0))
    out = jax.ShapeDtypeStruct((rows, cols), F32)
    outs = pl.pallas_call(
        body, name=name, grid=(rows // tr,),
        in_specs=[pl.BlockSpec((N_DEV, tr, cols), lambda i: (0, i, 0)), row, row, row],
        out_specs=[row] * 4, out_shape=[out] * 4, compiler_params=_cp(("parallel",)),
    )(slots.reshape(N_DEV, rows, cols), w.reshape(rows, cols), m.reshape(rows, cols), v.reshape(rows, cols))
    return [o.reshape(shape) for o in outs]


_SMALL = ("ln_in_g", "ln_in_b", "mem_ln_g", "mem_ln_b", "b_forget", "mla_q_norm_g", "mla_kv_norm_g", "ln_g", "ln_b")
_ORDER = ("ln_in_g", "ln_in_b", "mem_ln_g", "mem_ln_b", "w_in", "b_forget", "mla_q_norm_g", "w_mla_q_up",
          "mla_kv_norm_g", "w_mla_kv_up", "w_mem_kv", "w_out", "ln_g", "ln_b")


def _pack_small(d):
    flat = jnp.concatenate([d[n].reshape(-1) for n in _SMALL])
    n = flat.shape[0]
    padded = ((n + 8 * LANE - 1) // (8 * LANE)) * (8 * LANE)
    return jnp.pad(flat, (0, padded - n)).reshape(-1, LANE)


def _unpack_small(packed, like):
    flat, out, off = packed.reshape(-1), {}, 0
    for n in _SMALL:
        size = math.prod(like[n].shape)
        out[n] = flat[off:off + size].reshape(like[n].shape)
        off += size
    return out


def _unstack_cols(g):
    n, l, r, c = g.shape
    return g.transpose(1, 2, 0, 3).reshape(l, r, n * c)


def _stack_cols(g):
    l, r, nc = g.shape
    return g.reshape(l, r, N_DEV, nc // N_DEV).transpose(2, 0, 1, 3)


def kernel(x, mem, ln_in_g, ln_in_b, mem_ln_g, mem_ln_b, w_in, b_forget, mla_q_norm_g, w_mla_q_up, mla_kv_norm_g, w_mla_kv_up, w_mem_kv, w_out, ln_g, ln_b, loss_target, m_ln_in_g, m_ln_in_b, m_mem_ln_g, m_mem_ln_b, m_w_in, m_b_forget, m_mla_q_norm_g, m_w_mla_q_up, m_mla_kv_norm_g, m_w_mla_kv_up, m_w_mem_kv, m_w_out, m_ln_g, m_ln_b, v_ln_in_g, v_ln_in_b, v_mem_ln_g, v_mem_ln_b, v_w_in, v_b_forget, v_mla_q_norm_g, v_w_mla_q_up, v_mla_kv_norm_g, v_w_mla_kv_up, v_w_mem_kv, v_w_out, v_ln_g, v_ln_b):
    w_shard = dict(ln_in_g=ln_in_g, ln_in_b=ln_in_b, mem_ln_g=mem_ln_g, mem_ln_b=mem_ln_b, w_in=w_in,
                   b_forget=b_forget, mla_q_norm_g=mla_q_norm_g, w_mla_q_up=w_mla_q_up,
                   mla_kv_norm_g=mla_kv_norm_g, w_mla_kv_up=w_mla_kv_up, w_mem_kv=w_mem_kv, w_out=w_out,
                   ln_g=ln_g, ln_b=ln_b)
    m_shard = dict(ln_in_g=m_ln_in_g, ln_in_b=m_ln_in_b, mem_ln_g=m_mem_ln_g, mem_ln_b=m_mem_ln_b, w_in=m_w_in,
                   b_forget=m_b_forget, mla_q_norm_g=m_mla_q_norm_g, w_mla_q_up=m_w_mla_q_up,
                   mla_kv_norm_g=m_mla_kv_norm_g, w_mla_kv_up=m_w_mla_kv_up, w_mem_kv=m_w_mem_kv, w_out=m_w_out,
                   ln_g=m_ln_g, ln_b=m_ln_b)
    v_shard = dict(ln_in_g=v_ln_in_g, ln_in_b=v_ln_in_b, mem_ln_g=v_mem_ln_g, mem_ln_b=v_mem_ln_b, w_in=v_w_in,
                   b_forget=v_b_forget, mla_q_norm_g=v_mla_q_norm_g, w_mla_q_up=v_w_mla_q_up,
                   mla_kv_norm_g=v_mla_kv_norm_g, w_mla_kv_up=v_w_mla_kv_up, w_mem_kv=v_w_mem_kv, w_out=v_w_out,
                   ln_g=v_ln_g, ln_b=v_ln_b)

    g_in, g_mem, g_out, g_qup, g_kvup = _all_gather([w_in, w_mem_kv, w_out], [w_mla_q_up, w_mla_kv_up])
    full = dict(w_shard)
    full.update(w_in=g_in, w_mem_kv=g_mem, w_out=g_out, w_mla_q_up=_unstack_cols(g_qup),
                w_mla_kv_up=_unstack_cols(g_kvup))

    loss_local, (grad_w, grad_x) = jax.value_and_grad(_trunk_loss, argnums=(0, 1))(
        full, x[0], mem[0], loss_target[0])

    s_in, s_mem, s_out, s_qup, s_kvup, s_small = _reduce_scatter(
        [grad_w["w_in"], grad_w["w_mem_kv"], grad_w["w_out"]],
        [_stack_cols(grad_w["w_mla_q_up"]), _stack_cols(grad_w["w_mla_kv_up"])],
        [_pack_small(grad_w)])

    res = {}
    for name, slots in (("w_in", s_in), ("w_mem_kv", s_mem), ("w_out", s_out), ("w_mla_q_up", s_qup),
                        ("w_mla_kv_up", s_kvup)):
        res[name] = _adamw(slots, w_shard[name], m_shard[name], v_shard[name], "adamw_" + name)
    small = _adamw(s_small, _pack_small(w_shard), _pack_small(m_shard), _pack_small(v_shard), "adamw_small")
    small = [_unpack_small(a, w_shard) for a in small]
    for name in _SMALL:
        res[name] = [a[name] for a in small]

    loss = lax.psum(loss_local, MESH_AXES)
    outs = [loss, grad_x[None]]
    for k in range(4):
        outs += [res[name][k] for name in _ORDER]
    return tuple(outs)
```

```python
import functools
import math

import jax
import jax.numpy as jnp
from jax import lax
from jax.experimental import pallas as pl
from jax.experimental.pallas import tpu as pltpu

F32 = jnp.float32
BF16 = jnp.bfloat16

D_MODEL = 1024
DEPTH = 2
GROUP_W = 256
N_HEADS = 4
HEAD_DIM = 64
MLA_Q_RANK = 256
MLA_KV_RANK = 128
MLA_NOPE = 64
MLA_ROPE = 32
MLA_V = 64
ROPE_THETA = 10000.0
LN_EPS = 1e-5
RMS_EPS = 1e-6
DEEPNORM_ALPHA = (2 * DEPTH) ** 0.25
SPLIT_SIZES = (256, 256, 256, 4, 256, 256, 256, 256, 128, 32, 256, 1024)
IN_COLS = sum(SPLIT_SIZES)
_ORIG_OFF = [sum(SPLIT_SIZES[:i]) for i in range(len(SPLIT_SIZES))]
_PERM = (("fq", 0), ("fk", 1), ("fv", 2), ("sq", 4), ("sk", 5), ("sv", 6), ("c_q", 7), ("c_kv", 8),
         ("mq", 10), ("gate", 11), ("k_rot", 9), ("f_logit", 3))
LANE = 128
PROJ_COLS = ((IN_COLS + LANE - 1) // LANE) * LANE

ADAM_LR = 0.001
ADAM_B1 = 0.9
ADAM_B2 = 0.999
ADAM_EPS = 1e-08
ADAM_WD = 0.01
ADAM_STEP = 10

N_DEV = 8
MESH_AXES = ("x", "y", "c")
VMEM_LIMIT = 48 * 1024 * 1024
ATTN_VMEM_LIMIT = 56 * 1024 * 1024
ATTN_BQ = 512
ATTN_BK = 512
CUMSUM_CHUNK = 256
NEG_BIG = -1e30
LOG2E = math.log2(math.e)
MM_TM, MM_TN, MM_TK, MM_TK_NT = 1024, 1664, 1024, 3328

_NT = (((1,), (1,)), ((), ()))
_NN = (((1,), (0,)), ((), ()))


def _cp(sem, vmem=VMEM_LIMIT):
    return pltpu.CompilerParams(dimension_semantics=sem, vmem_limit_bytes=vmem)


def _dot(a, b, dims=_NN):
    return lax.dot_general(a, b, dims, preferred_element_type=F32)


def _pick(n, cands):
    for c in cands:
        if c <= n and n % c == 0:
            return c
    return n


def _tile(n, cap):
    if n <= cap:
        return n
    best = None
    for d in range(LANE, cap + 1, LANE):
        if n % d == 0:
            best = d
    assert best is not None, (n, cap)
    return best


def _matmul(a, b, mode, name):
    if mode == "nn":
        (M, K), (K2, N) = a.shape, b.shape
    else:
        (M, K), (N, K2) = a.shape, b.shape
    assert K == K2 and a.dtype == BF16 and b.dtype == BF16, (a.shape, b.shape, mode)
    tm, tn = _tile(M, MM_TM), _tile(N, MM_TN)
    tk = _tile(K, MM_TK if mode == "nn" else MM_TK_NT)
    nk = K // tk
    dims = _NN if mode == "nn" else _NT

    def body(a_ref, b_ref, o_ref, acc_ref):
        part = _dot(a_ref[...], b_ref[...], dims)
        if nk == 1:
            o_ref[...] = part
        else:
            k = pl.program_id(2)

            @pl.when(k == 0)
            def _():
                acc_ref[...] = part

            @pl.when(k > 0)
            def _():
                acc_ref[...] += part

            @pl.when(k == nk - 1)
            def _():
                o_ref[...] = acc_ref[...]

    a_spec = pl.BlockSpec((tm, tk), lambda j, i, k: (i, k))
    if mode == "nn":
        b_spec = pl.BlockSpec((tk, tn), lambda j, i, k: (k, j))
    else:
        b_spec = pl.BlockSpec((tn, tk), lambda j, i, k: (j, k))
    acc_shape = (tm, tn) if nk > 1 else (8, LANE)
    return pl.pallas_call(
        body, name=name, grid=(N // tn, M // tm, nk),
        in_specs=[a_spec, b_spec],
        out_specs=pl.BlockSpec((tm, tn), lambda j, i, k: (i, j)),
        out_shape=jax.ShapeDtypeStruct((M, N), F32),
        scratch_shapes=[pltpu.VMEM(acc_shape, F32)],
        compiler_params=_cp(("parallel", "parallel", "arbitrary")),
    )(a, b)


def _make_mm(name):
    @jax.custom_vjp
    def mm(a, w):
        return _matmul(a.astype(BF16), w.astype(BF16), "nn", name + "_fwd")

    def fwd(a, w):
        a16, w16 = a.astype(BF16), w.astype(BF16)
        return _matmul(a16, w16, "nn", name + "_fwd"), (a16, w16)

    def bwd(res, dy):
        a16, w16 = res
        dy16 = dy.astype(BF16)
        da = _matmul(dy16, w16, "nt", name + "_dx")
        dw = _matmul(a16.T, dy16, "nn", name + "_dw")
        return da, dw

    mm.defvjp(fwd, bwd)
    return mm


def _row_tile(rows):
    return _pick(rows, (512, 256, 128, 64, 32, 16, 8))


def _ln_stats(u):
    mu = jnp.mean(u, axis=-1, keepdims=True)
    d = u - mu
    var = jnp.mean(d * d, axis=-1, keepdims=True)
    return d, lax.rsqrt(var + LN_EPS)


def _ln_fwd_call(x, res, g, b, name):
    rows, dm = x.shape
    tr = _row_tile(rows)
    has_res = res is not None

    def body(*refs):
        if has_res:
            x_ref, r_ref, g_ref, b_ref, o_ref = refs
            u = DEEPNORM_ALPHA * r_ref[...] + x_ref[...]
        else:
            x_ref, g_ref, b_ref, o_ref = refs
            u = x_ref[...]
        d, rstd = _ln_stats(u)
        o_ref[...] = d * rstd * g_ref[...] + b_ref[...]

    row = pl.BlockSpec((tr, dm), lambda i: (i, 0))
    vec = pl.BlockSpec((1, dm), lambda i: (0, 0))
    args = (x, res) if has_res else (x,)
    return pl.pallas_call(
        body, name=name, grid=(rows // tr,),
        in_specs=[row] * len(args) + [vec, vec], out_specs=row,
        out_shape=jax.ShapeDtypeStruct((rows, dm), F32),
        compiler_params=_cp(("parallel",)),
    )(*args, g.reshape(1, dm), b.reshape(1, dm))


def _ln_bwd_call(dy, x, res, g, name):
    rows, dm = x.shape
    tr = _row_tile(rows)
    has_res = res is not None

    def body(*refs):
        if has_res:
            dy_ref, x_ref, r_ref, g_ref, dx_ref, dr_ref, dg_ref, db_ref = refs
            u = DEEPNORM_ALPHA * r_ref[...] + x_ref[...]
        else:
            dy_ref, x_ref, g_ref, dx_ref, dg_ref, db_ref = refs
            u = x_ref[...]
        i = pl.program_id(0)
        d, rstd = _ln_stats(u)
        xhat = d * rstd
        dyv = dy_ref[...]
        dxh = dyv * g_ref[...]
        m1 = jnp.mean(dxh, axis=-1, keepdims=True)
        m2 = jnp.mean(dxh * xhat, axis=-1, keepdims=True)
        du = rstd * (dxh - m1 - xhat * m2)
        dx_ref[...] = du
        if has_res:
            dr_ref[...] = DEEPNORM_ALPHA * du
        pg = jnp.sum(dyv * xhat, axis=0, keepdims=True)
        pb = jnp.sum(dyv, axis=0, keepdims=True)

        @pl.when(i == 0)
        def _():
            dg_ref[...] = pg
            db_ref[...] = pb

        @pl.when(i > 0)
        def _():
            dg_ref[...] += pg
            db_ref[...] += pb

    row = pl.BlockSpec((tr, dm), lambda i: (i, 0))
    vec = pl.BlockSpec((1, dm), lambda i: (0, 0))
    big = jax.ShapeDtypeStruct((rows, dm), F32)
    small = jax.ShapeDtypeStruct((1, dm), F32)
    args = (dy, x, res) if has_res else (dy, x)
    n_big = 2 if has_res else 1
    outs = pl.pallas_call(
        body, name=name, grid=(rows // tr,),
        in_specs=[row] * len(args) + [vec],
        out_specs=[row] * n_big + [vec, vec],
        out_shape=[big] * n_big + [small, small],
        compiler_params=_cp(("arbitrary",)),
    )(*args, g.reshape(1, dm))
    return outs


def _make_ln(name, has_res):
    if has_res:
        @jax.custom_vjp
        def ln(x, res, g, b):
            return _ln_fwd_call(x, res, g, b, name + "_fwd")

        def fwd(x, res, g, b):
            return ln(x, res, g, b), (x, res, g)

        def bwd(saved, dy):
            x, res, g = saved
            dx, dr, dg, db = _ln_bwd_call(dy, x, res, g, name + "_bwd")
            return dx, dr, dg.reshape(-1), db.reshape(-1)
    else:
        @jax.custom_vjp
        def ln(x, g, b):
            return _ln_fwd_call(x, None, g, b, name + "_fwd")

        def fwd(x, g, b):
            return ln(x, g, b), (x, g)

        def bwd(saved, dy):
            x, g = saved
            dx, dg, db = _ln_bwd_call(dy, x, None, g, name + "_bwd")
            return dx, dg.reshape(-1), db.reshape(-1)

    ln.defvjp(fwd, bwd)
    return ln


def _rms_fwd_call(x, g, name):
    rows, dm = x.shape
    tr = _row_tile(rows)

    def body(x_ref, g_ref, o_ref):
        xv = x_ref[...]
        rstd = lax.rsqrt(jnp.mean(xv * xv, axis=-1, keepdims=True) + RMS_EPS)
        o_ref[...] = xv * rstd * g_ref[...]

    row = pl.BlockSpec((tr, dm), lambda i: (i, 0))
    vec = pl.BlockSpec((1, dm), lambda i: (0, 0))
    return pl.pallas_call(
        body, name=name, grid=(rows // tr,), in_specs=[row, vec], out_specs=row,
        out_shape=jax.ShapeDtypeStruct((rows, dm), F32), compiler_params=_cp(("parallel",)),
    )(x, g.reshape(1, dm))


def _rms_bwd_call(dy, x, g, name):
    rows, dm = x.shape
    tr = _row_tile(rows)

    def body(dy_ref, x_ref, g_ref, dx_ref, dg_ref):
        i = pl.program_id(0)
        xv = x_ref[...]
        dyv = dy_ref[...]
        rstd = lax.rsqrt(jnp.mean(xv * xv, axis=-1, keepdims=True) + RMS_EPS)
        xhat = xv * rstd
        dxh = dyv * g_ref[...]
        m2 = jnp.mean(dxh * xhat, axis=-1, keepdims=True)
        dx_ref[...] = rstd * (dxh - xhat * m2)
        pg = jnp.sum(dyv * xhat, axis=0, keepdims=True)

        @pl.when(i == 0)
        def _():
            dg_ref[...] = pg

        @pl.when(i > 0)
        def _():
            dg_ref[...] += pg

    row = pl.BlockSpec((tr, dm), lambda i: (i, 0))
    vec = pl.BlockSpec((1, dm), lambda i: (0, 0))
    return pl.pallas_call(
        body, name=name, grid=(rows // tr,), in_specs=[row, row, vec], out_specs=[row, vec],
        out_shape=[jax.ShapeDtypeStruct((rows, dm), F32), jax.ShapeDtypeStruct((1, dm), F32)],
        compiler_params=_cp(("arbitrary",)),
    )(dy, x, g.reshape(1, dm))


def _make_rms(name):
    @jax.custom_vjp
    def rms(x, g):
        return _rms_fwd_call(x, g, name + "_fwd")

    def fwd(x, g):
        return rms(x, g), (x, g)

    def bwd(saved, dy):
        x, g = saved
        dx, dg = _rms_bwd_call(dy, x, g, name + "_bwd")
        return dx, dg.reshape(-1)

    rms.defvjp(fwd, bwd)
    return rms


def _sigmoid(x):
    return 1.0 / (1.0 + jnp.exp(-x))


def _gate_fwd_call(mixed, gate, name):
    rows, dm = mixed.shape
    tr = _row_tile(rows)

    def body(m_ref, g_ref, o_ref):
        gv = g_ref[...]
        o_ref[...] = m_ref[...] * (gv * _sigmoid(gv))

    row = pl.BlockSpec((tr, dm), lambda i: (i, 0))
    return pl.pallas_call(
        body, name=name, grid=(rows // tr,), in_specs=[row, row], out_specs=row,
        out_shape=jax.ShapeDtypeStruct((rows, dm), F32), compiler_params=_cp(("parallel",)),
    )(mixed, gate)


def _gate_bwd_call(dy, mixed, gate, name):
    rows, dm = mixed.shape
    tr = _row_tile(rows)

    def body(dy_ref, m_ref, g_ref, dm_ref, dg_ref):
        gv = g_ref[...]
        dyv = dy_ref[...]
        sg = _sigmoid(gv)
        dm_ref[...] = dyv * (gv * sg)
        dg_ref[...] = dyv * m_ref[...] * (sg * (1.0 + gv * (1.0 - sg)))

    row = pl.BlockSpec((tr, dm), lambda i: (i, 0))
    out = jax.ShapeDtypeStruct((rows, dm), F32)
    return pl.pallas_call(
        body, name=name, grid=(rows // tr,), in_specs=[row, row, row], out_specs=[row, row],
        out_shape=[out, out], compiler_params=_cp(("parallel",)),
    )(dy, mixed, gate)


def _make_gate(name):
    @jax.custom_vjp
    def gate_mul(mixed, gate):
        return _gate_fwd_call(mixed, gate, name + "_fwd")

    def fwd(mixed, gate):
        return gate_mul(mixed, gate), (mixed, gate)

    def bwd(saved, dy):
        mixed, gate = saved
        dmix, dgate = _gate_bwd_call(dy, mixed, gate, name + "_bwd")
        return dmix, dgate

    gate_mul.defvjp(fwd, bwd)
    return gate_mul


def _loss_call(y, t, name):
    rows, dm = y.shape
    tr = _row_tile(rows)

    def body(y_ref, t_ref, l_ref, d_ref):
        i = pl.program_id(0)
        e = y_ref[...] - t_ref[...]
        d_ref[...] = e * (1.0 / dm)
        part = 0.5 * jnp.sum(jnp.mean(e * e, axis=-1, keepdims=True), axis=0, keepdims=True)

        @pl.when(i == 0)
        def _():
            l_ref[...] = part

        @pl.when(i > 0)
        def _():
            l_ref[...] += part

    row = pl.BlockSpec((tr, dm), lambda i: (i, 0))
    one = pl.BlockSpec((1, 1), lambda i: (0, 0))
    return pl.pallas_call(
        body, name=name, grid=(rows // tr,), in_specs=[row, row], out_specs=[one, row],
        out_shape=[jax.ShapeDtypeStruct((1, 1), F32), jax.ShapeDtypeStruct((rows, dm), F32)],
        compiler_params=_cp(("arbitrary",)),
    )(y, t)


@jax.custom_vjp
def _loss_op(y, t):
    return _loss_call(y, t, "loss_head")[0][0, 0]


def _loss_fwd(y, t):
    l, d = _loss_call(y, t, "loss_head")
    return l[0, 0], d


def _loss_bwd(d, ct):
    return ct * d, jnp.zeros_like(d)


_loss_op.defvjp(_loss_fwd, _loss_bwd)


def _attn_blocks(S, Sk):
    bq, bk = min(ATTN_BQ, S), min(ATTN_BK, Sk)
    assert S % bq == 0 and Sk % bk == 0
    return bq, bk


def _valid_t(i, j, bq, bk, strict):
    key = j * bk + lax.broadcasted_iota(jnp.int32, (bk, bq), 0)
    qry = i * bq + lax.broadcasted_iota(jnp.int32, (bk, bq), 1)
    return (key < qry) if strict else (key <= qry)


def _sm_fwd_t(qT, k, vT, r, cmul, causal, name):
    H, DK, S = qT.shape
    Sk, dv = k.shape[1], vT.shape[1]
    bq, bk = _attn_blocks(S, Sk)
    nq, nkb = S // bq, Sk // bk
    use_r = r is not None
    if causal:
        assert S == Sk and bq == bk

    def body(*refs):
        if use_r:
            qT_ref, k_ref, vT_ref, r_ref, oT_ref, lse_ref = refs
        else:
            qT_ref, k_ref, vT_ref, oT_ref, lse_ref = refs
        i = pl.program_id(1)
        qTb = qT_ref[...]

        def scores(j):
            off = pl.multiple_of(j * bk, bk)
            return _dot(k_ref[pl.ds(off, bk), :], qTb)

        def blk(j, carry, raw, masked):
            m, l, acc = carry
            off = pl.multiple_of(j * bk, bk)
            sT = raw if cmul == 1.0 else raw * cmul
            if masked:
                sT = jnp.where(_valid_t(i, j, bq, bk, False), sT, NEG_BIG)
            cm = jnp.max(sT, axis=0, keepdims=True)
            if use_r:
                cm = cm + r_ref[...]
            m_new = jnp.maximum(m, cm)
            shift = (m_new - r_ref[...]) if use_r else m_new
            p = jnp.exp2(sT - shift)
            a = jnp.exp2(m - m_new)
            l = a * l + jnp.sum(p, axis=0, keepdims=True)
            acc = a * acc + _dot(vT_ref[:, pl.ds(off, bk)], p.astype(BF16))
            return m_new, l, acc

        def step(j, c):
            nxt = scores(j + 1)
            return blk(j, c[:3], c[3], False) + (nxt,)

        last = i if causal else nkb - 1
        carry = (jnp.full((1, bq), NEG_BIG, F32), jnp.zeros((1, bq), F32), jnp.zeros((dv, bq), F32), scores(0))
        carry = lax.fori_loop(0, last, step, carry)
        m, l, acc = blk(last, carry[:3], carry[3], causal)
        oT_ref[...] = acc / l
        lse_ref[...] = m + jnp.log2(l)

    qcol = lambda d: pl.BlockSpec((None, d, bq), lambda h, i: (h, 0, i))
    in_specs = [qcol(DK), pl.BlockSpec((None, Sk, DK), lambda h, i: (h, 0, 0)),
                pl.BlockSpec((None, dv, Sk), lambda h, i: (h, 0, 0))]
    args = [qT, k, vT]
    if use_r:
        in_specs.append(qcol(1))
        args.append(r)
    return pl.pallas_call(
        body, name=name, grid=(H, nq), in_specs=in_specs, out_specs=[qcol(dv), qcol(1)],
        out_shape=[jax.ShapeDtypeStruct((H, dv, S), F32), jax.ShapeDtypeStruct((H, 1, S), F32)],
        compiler_params=_cp(("parallel", "arbitrary"), ATTN_VMEM_LIMIT),
    )(*args)


def _sm_bwd_t(qT, qn, k, kT, v, oT, lse, doT, do, r, cmul, gscale, causal, name, fcol=None):
    H, DK, S = qT.shape
    Sk, dv = k.shape[1], v.shape[2]
    bq, bk = _attn_blocks(S, Sk)
    nq, nkb = S // bq, Sk // bk
    use_r = r is not None

    def body(*refs):
        if use_r:
            (qT_ref, qn_ref, k_ref, kT_ref, v_ref, oT_ref, lse_ref, doT_ref, do_ref, r_ref,
             dqT_ref, dk_ref, dv_ref, dr_ref) = refs
        else:
            (qT_ref, qn_ref, k_ref, kT_ref, v_ref, oT_ref, lse_ref, doT_ref, do_ref,
             dqT_ref, dk_ref, dv_ref) = refs
        i = pl.program_id(1)

        @pl.when(i == 0)
        def _():
            dk_ref[...] = jnp.zeros_like(dk_ref)
            dv_ref[...] = jnp.zeros_like(dv_ref)

        qTb = qT_ref[...]
        qnb = qn_ref[...]
        dob = do_ref[...]
        doTf = doT_ref[...]
        doTb = doTf.astype(BF16)
        delta = jnp.sum(doTf * oT_ref[...], axis=0, keepdims=True)
        shift = (lse_ref[...] - r_ref[...]) if use_r else lse_ref[...]

        def blk(j, carry, masked):
            dq, dr = carry
            off = pl.multiple_of(j * bk, bk)
            kb = k_ref[pl.ds(off, bk), :]
            sT = _dot(kb, qTb)
            if cmul != 1.0:
                sT = sT * cmul
            if masked:
                sT = jnp.where(_valid_t(i, j, bq, bk, False), sT, NEG_BIG)
            p = jnp.exp2(sT - shift)
            dp = _dot(v_ref[pl.ds(off, bk), :], doTb)
            ds = p * (dp - delta)
            dsb = (ds * gscale).astype(BF16) if gscale != 1.0 else ds.astype(BF16)
            dv_ref[pl.ds(off, bk), :] += _dot(p.astype(BF16), dob)
            dkb = _dot(dsb, qnb)
            if use_r:
                dr = dr + jnp.sum(ds, axis=0, keepdims=True)
                lane = lax.broadcasted_iota(jnp.int32, dkb.shape, 1)
                dkb = jnp.where(lane == fcol, jnp.sum(ds, axis=1, keepdims=True), dkb)
            dk_ref[pl.ds(off, bk), :] += dkb
            return dq + _dot(kT_ref[:, pl.ds(off, bk)], dsb), dr

        carry = (jnp.zeros((DK, bq), F32), jnp.zeros((1, bq), F32))
        if causal:
            carry = lax.fori_loop(0, i, lambda j, c: blk(j, c, False), carry)
            carry = blk(i, carry, True)
        else:
            carry = lax.fori_loop(0, nkb, lambda j, c: blk(j, c, False), carry)
        dqT_ref[...] = carry[0]
        if use_r:
            dr_ref[...] = carry[1]

    qcol = lambda d: pl.BlockSpec((None, d, bq), lambda h, i: (h, 0, i))
    qrow = lambda d: pl.BlockSpec((None, bq, d), lambda h, i: (h, i, 0))
    krow = lambda d: pl.BlockSpec((None, Sk, d), lambda h, i: (h, 0, 0))
    in_specs = [qcol(DK), qrow(DK), krow(DK), pl.BlockSpec((None, DK, Sk), lambda h, i: (h, 0, 0)), krow(dv),
                qcol(dv), qcol(1), qcol(dv), qrow(dv)]
    args = [qT, qn, k, kT, v, oT, lse, doT, do]
    out_specs = [qcol(DK), krow(DK), krow(dv)]
    out_shape = [jax.ShapeDtypeStruct((H, DK, S), F32), jax.ShapeDtypeStruct((H, Sk, DK), F32),
                 jax.ShapeDtypeStruct((H, Sk, dv), F32)]
    if use_r:
        in_specs.append(qcol(1))
        args.append(r)
        out_specs.append(qcol(1))
        out_shape.append(jax.ShapeDtypeStruct((H, 1, S), F32))
    return pl.pallas_call(
        body, name=name, grid=(H, nq), in_specs=in_specs, out_specs=out_specs, out_shape=out_shape,
        compiler_params=_cp(("parallel", "arbitrary"), ATTN_VMEM_LIMIT),
    )(*args)


def _tri(n, fn):
    r = lax.broadcasted_iota(jnp.int32, (n, n), 0)
    c = lax.broadcasted_iota(jnp.int32, (n, n), 1)
    return jnp.where(fn(r, c), 1.0, 0.0).astype(BF16)


def _key_cumsum(x, tri2, suffix, base):
    bk = x.shape[0]
    c = min(CUMSUM_CHUNK, bk)
    n = bk // c
    hi32 = lax.bitcast_convert_type(lax.bitcast_convert_type(x, jnp.int32) & jnp.int32(-65536), F32)
    hi = hi32.astype(BF16)
    lo = (x - hi32).astype(BF16)
    tot = [jnp.sum(x[a * c:(a + 1) * c], axis=0, keepdims=True) for a in range(n)]
    outs = []
    for a in range(n):
        row = base
        for t in (tot[a + 1:] if suffix else tot[:a]):
            row = row + t
        stacked = jnp.concatenate([hi[a * c:(a + 1) * c], lo[a * c:(a + 1) * c]], axis=0)
        outs.append(_dot(tri2, stacked) + row)
    total = tot[0]
    for t in tot[1:]:
        total = total + t
    return (outs[0] if n == 1 else jnp.concatenate(outs, axis=0)), total


def _tri2(n, fn):
    t = _tri(n, fn)
    return jnp.concatenate([t, t], axis=1)


def _sb_logs(z):
    neg_abs = lax.bitcast_convert_type(lax.bitcast_convert_type(z, jnp.int32) | jnp.int32(-2 ** 31), F32)
    ls = jnp.minimum(z, 0.0) - jnp.log(1.0 + jnp.exp(neg_abs))
    return ls, ls - z


def _sb_fwd_t(qT, k, vT, name):
    H, DK, S = qT.shape
    dv = vT.shape[1]
    bq, bk = _attn_blocks(S, S)
    assert bq == bk
    nq = S // bq
    c = min(CUMSUM_CHUNK, bk)

    def body(qT_ref, k_ref, vT_ref, oT_ref, lt_ref):
        i = pl.program_id(1)
        qTb = qT_ref[...]
        after = _tri2(c, lambda s, j: j > s)

        def blk(jj, carry, masked):
            rsum, acc = carry
            j = i - jj
            off = pl.multiple_of(j * bk, bk)
            z = _dot(k_ref[pl.ds(off, bk), :], qTb)
            ls, lk = _sb_logs(z)
            if masked:
                valid = _valid_t(i, j, bq, bk, True)
                lk = jnp.where(valid, lk, 0.0)
            tail, tot = _key_cumsum(lk, after, True, rsum)
            w = jnp.exp(ls + tail)
            if masked:
                w = jnp.where(valid, w, 0.0)
            return rsum + tot, acc + _dot(vT_ref[:, pl.ds(off, bk)], w.astype(BF16))

        carry = (jnp.zeros((1, bq), F32), jnp.zeros((dv, bq), F32))
        carry = blk(0, carry, True)
        carry = lax.fori_loop(1, i + 1, lambda jj, cr: blk(jj, cr, False), carry)
        lt_ref[...] = carry[0]
        oT_ref[...] = carry[1]

    qcol = lambda d: pl.BlockSpec((None, d, bq), lambda h, i: (h, 0, i))
    return pl.pallas_call(
        body, name=name, grid=(H, nq),
        in_specs=[qcol(DK), pl.BlockSpec((None, S, DK), lambda h, i: (h, 0, 0)),
                  pl.BlockSpec((None, dv, S), lambda h, i: (h, 0, 0))],
        out_specs=[qcol(dv), qcol(1)],
        out_shape=[jax.ShapeDtypeStruct((H, dv, S), F32), jax.ShapeDtypeStruct((H, 1, S), F32)],
        compiler_params=_cp(("parallel", "arbitrary"), ATTN_VMEM_LIMIT),
    )(qT, k, vT)


def _sb_bwd_t(qT, qn, k, kT, v, lt, doT, do, name):
    H, DK, S = qT.shape
    dv = v.shape[2]
    bq, bk = _attn_blocks(S, S)
    nq = S // bq
    c = min(CUMSUM_CHUNK, bk)

    def body(qT_ref, qn_ref, k_ref, kT_ref, v_ref, lt_ref, doT_ref, do_ref, dqT_ref, dk_ref, dv_ref):
        i = pl.program_id(1)

        @pl.when(i == 0)
        def _():
            dk_ref[...] = jnp.zeros_like(dk_ref)
            dv_ref[...] = jnp.zeros_like(dv_ref)

        qTb = qT_ref[...]
        qnb = qn_ref[...]
        dob = do_ref[...]
        doTb = doT_ref[...].astype(BF16)
        ltot = lt_ref[...]
        upto = _tri2(c, lambda s, j: j <= s)
        before = _tri2(c, lambda s, j: j < s)

        def blk(j, carry, masked):
            dq, rp, gp = carry
            off = pl.multiple_of(j * bk, bk)
            kb = k_ref[pl.ds(off, bk), :]
            z = _dot(kb, qTb)
            ls, lk = _sb_logs(z)
            if masked:
                valid = _valid_t(i, j, bq, bk, True)
                lk = jnp.where(valid, lk, 0.0)
            pin, ltb = _key_cumsum(lk, upto, False, rp - ltot)
            w = jnp.exp(ls - pin)
            if masked:
                w = jnp.where(valid, w, 0.0)
            g = _dot(v_ref[pl.ds(off, bk), :], doTb) * w
            cin, gtb = _key_cumsum(g, before, False, gp)
            sig = jnp.exp(ls)
            dz = g * (1.0 - sig) - cin * sig
            if masked:
                dz = jnp.where(valid, dz, 0.0)
            dzb = dz.astype(BF16)
            dv_ref[pl.ds(off, bk), :] += _dot(w.astype(BF16), dob)
            dk_ref[pl.ds(off, bk), :] += _dot(dzb, qnb)
            return dq + _dot(kT_ref[:, pl.ds(off, bk)], dzb), rp + ltb, gp + gtb

        carry = (jnp.zeros((DK, bq), F32), jnp.zeros((1, bq), F32), jnp.zeros((1, bq), F32))
        carry = lax.fori_loop(0, i, lambda j, cr: blk(j, cr, False), carry)
        carry = blk(i, carry, True)
        dqT_ref[...] = carry[0]

    qcol = lambda d: pl.BlockSpec((None, d, bq), lambda h, i: (h, 0, i))
    qrow = lambda d: pl.BlockSpec((None, bq, d), lambda h, i: (h, i, 0))
    krow = lambda d: pl.BlockSpec((None, S, d), lambda h, i: (h, 0, 0))
    return pl.pallas_call(
        body, name=name, grid=(H, nq),
        in_specs=[qcol(DK), qrow(DK), krow(DK), pl.BlockSpec((None, DK, S), lambda h, i: (h, 0, 0)), krow(dv),
                  qcol(1), qcol(dv), qrow(dv)],
        out_specs=[qcol(DK), krow(DK), krow(dv)],
        out_shape=[jax.ShapeDtypeStruct((H, DK, S), F32), jax.ShapeDtypeStruct((H, S, DK), F32),
                   jax.ShapeDtypeStruct((H, S, dv), F32)],
        compiler_params=_cp(("parallel", "arbitrary"), ATTN_VMEM_LIMIT),
    )(qT, qn, k, kT, v, lt, doT, do)


def _round_bf16(x):
    return lax.reduce_precision(x, exponent_bits=8, mantissa_bits=7)


def _split3(x):
    hi = _round_bf16(x)
    mid = _round_bf16(x - hi)
    lo = _round_bf16(x - hi - mid)
    return hi.astype(BF16), mid.astype(BF16), lo.astype(BF16)


def _pow2(x):
    m, _ = math.frexp(x)
    return m == 0.5


def _pad_last(x, n):
    return jnp.pad(x, [(0, 0)] * (x.ndim - 1) + [(0, n - x.shape[-1])])


def _layouts(q, k, scale, f_cum):
    d = q.shape[-1]
    pre = _pow2(scale)
    qh = jnp.transpose(q * scale if pre else q, (1, 0, 2)).astype(BF16)
    kh = jnp.transpose(k, (1, 0, 2)).astype(BF16)
    S, Sk, H = q.shape[0], k.shape[0], q.shape[1]
    if f_cum is not None:
        assert pre and d + 3 <= LANE
        ones = jnp.ones((H, S, 1), BF16)
        q_s = jnp.concatenate([qh, -ones, -ones, -ones], axis=-1)
        kh = jnp.concatenate([kh] + [t[:, :, None] for t in _split3(f_cum)], axis=-1)
    else:
        q_s = qh
    q_n = qh
    q_s, q_n, kh = _pad_last(q_s, LANE), _pad_last(q_n, LANE), _pad_last(kh, LANE)
    return q_n, jnp.transpose(q_s, (0, 2, 1)), kh, jnp.transpose(kh, (0, 2, 1)), pre


def _make_softmax_attn(name, scale, causal, use_f, d):
    pre = _pow2(scale)
    cmul = LOG2E if pre else scale * LOG2E
    gscale = 1.0 if pre else scale

    def run_fwd(q, k, v, f_cum):
        qn, qT, kn, kT, _ = _layouts(q, k, scale, f_cum)
        vn = jnp.transpose(v, (1, 0, 2)).astype(BF16)
        vT = jnp.transpose(vn, (0, 2, 1))
        r = (f_cum * LOG2E)[:, None, :] if use_f else None
        oT, lse = _sm_fwd_t(qT, kn, vT, r, cmul, causal, name + "_fwd")
        return jnp.transpose(oT, (2, 0, 1)), (qn, qT, kn, kT, vn, oT, lse, r)

    def run_bwd(saved, dout):
        qn, qT, kn, kT, vn, oT, lse, r = saved
        doT = jnp.transpose(dout, (1, 2, 0))
        do = jnp.transpose(dout, (1, 0, 2)).astype(BF16)
        outs = _sm_bwd_t(qT, qn, kn, kT, vn, oT, lse, doT, do, r, cmul, gscale, causal, name + "_bwd", d)
        dqT, dk, dv = outs[:3]
        dq = jnp.transpose(dqT[:, :d, :], (2, 0, 1))
        if pre:
            dq = dq * scale
        dkk = jnp.transpose(dk[:, :, :d], (1, 0, 2))
        dvv = jnp.transpose(dv, (1, 0, 2))
        if use_f:
            return dq, dkk, dvv, outs[3][:, 0, :] - dk[:, :, d]
        return dq, dkk, dvv

    if use_f:
        @jax.custom_vjp
        def attn(q, k, v, f_cum):
            return run_fwd(q, k, v, f_cum)[0]

        attn.defvjp(run_fwd, run_bwd)
    else:
        @jax.custom_vjp
        def attn(q, k, v):
            return run_fwd(q, k, v, None)[0]

        attn.defvjp(lambda q, k, v: run_fwd(q, k, v, None), run_bwd)
    return attn


def _make_sb_attn(name, scale, d):
    assert _pow2(scale)

    def run_fwd(q, k, v):
        qn, qT, kn, kT, _ = _layouts(q, k, scale, None)
        vn = jnp.transpose(v, (1, 0, 2)).astype(BF16)
        vT = jnp.transpose(vn, (0, 2, 1))
        oT, lt = _sb_fwd_t(qT, kn, vT, name + "_fwd")
        return jnp.transpose(oT, (2, 0, 1)), (qn, qT, kn, kT, vn, lt)

    def run_bwd(saved, dout):
        qn, qT, kn, kT, vn, lt = saved
        doT = jnp.transpose(dout, (1, 2, 0))
        do = jnp.transpose(dout, (1, 0, 2)).astype(BF16)
        dqT, dk, dv = _sb_bwd_t(qT, qn, kn, kT, vn, lt, doT, do, name + "_bwd")
        dq = jnp.transpose(dqT[:, :d, :], (2, 0, 1)) * scale
        return dq, jnp.transpose(dk[:, :, :d], (1, 0, 2)), jnp.transpose(dv, (1, 0, 2))

    @jax.custom_vjp
    def attn(q, k, v):
        return run_fwd(q, k, v)[0]

    attn.defvjp(run_fwd, run_bwd)
    return attn


def _rope(x, positions):
    half = x.shape[-1] // 2
    inv_freq = ROPE_THETA ** (-jnp.arange(half, dtype=F32) / half)
    ang = positions.astype(F32)[:, None] * inv_freq[None, :]
    ang = ang.reshape((ang.shape[0],) + (1,) * (x.ndim - 2) + (half,))
    cos, sin = jnp.cos(ang), jnp.sin(ang)
    x1, x2 = x[..., :half], x[..., half:]
    return jnp.concatenate([x1 * cos - x2 * sin, x1 * sin + x2 * cos], axis=-1)


def _permute_w_in(w):
    parts = [w[:, _ORIG_OFF[idx]:_ORIG_OFF[idx] + SPLIT_SIZES[idx]] for _, idx in _PERM]
    pad = jnp.zeros((w.shape[0], PROJ_COLS - IN_COLS), w.dtype)
    return jnp.concatenate(parts + [pad], axis=1)


@jax.custom_vjp
def _split_cols(proj):
    out, off = [], 0
    for _, idx in _PERM:
        out.append(proj[:, off:off + SPLIT_SIZES[idx]])
        off += SPLIT_SIZES[idx]
    return tuple(out)


def _split_cols_bwd(rows, cts):
    return (jnp.concatenate(list(cts) + [jnp.zeros((rows.shape[0], PROJ_COLS - IN_COLS), F32)], axis=1),)


_split_cols.defvjp(lambda proj: (_split_cols(proj), proj[:, :1]), _split_cols_bwd)


def _split_proj(proj):
    return {name: part for (name, _), part in zip(_PERM, _split_cols(proj))}


def _trunk_loss(wts, x2d, mem2d, target2d):
    s = x2d.shape[0]
    positions = jnp.arange(s)
    head_scale = HEAD_DIM ** -0.5
    mla_scale = (MLA_NOPE + MLA_ROPE) ** -0.5
    heads = lambda t: t.reshape(t.shape[0], N_HEADS, -1)

    h = _make_ln("ln_in", False)(x2d, wts["ln_in_g"], wts["ln_in_b"])
    mem_n = _make_ln("ln_mem", False)(mem2d, wts["mem_ln_g"], wts["mem_ln_b"])

    for l in range(DEPTH):
        tag = f"l{l}_"
        proj = _make_mm(tag + "proj")(h, _permute_w_in(wts["w_in"][l]))
        p = _split_proj(proj)

        log_f = jax.nn.log_sigmoid(p["f_logit"] + wts["b_forget"][l])
        f_cum = jnp.cumsum(log_f, axis=0).T
        out_fox = _make_softmax_attn(tag + "fox", head_scale, True, True, HEAD_DIM)(
            heads(p["fq"]), heads(p["fk"]), heads(p["fv"]), f_cum)

        out_sb = _make_sb_attn(tag + "sb", head_scale, HEAD_DIM)(heads(p["sq"]), heads(p["sk"]), heads(p["sv"]))

        cqn = _make_rms(tag + "rms_q")(p["c_q"], wts["mla_q_norm_g"][l])
        q_mla = _make_mm(tag + "q_up")(cqn, wts["w_mla_q_up"][l]).reshape(s, N_HEADS, MLA_NOPE + MLA_ROPE)
        ckvn = _make_rms(tag + "rms_kv")(p["c_kv"], wts["mla_kv_norm_g"][l])
        kv_mla = _make_mm(tag + "kv_up")(ckvn, wts["w_mla_kv_up"][l]).reshape(s, N_HEADS, MLA_NOPE + MLA_V)
        q_full = jnp.concatenate([q_mla[..., :MLA_NOPE], _rope(q_mla[..., MLA_NOPE:], positions)], axis=-1)
        k_rope = jnp.broadcast_to(_rope(p["k_rot"], positions)[:, None, :], (s, N_HEADS, MLA_ROPE))
        k_full = jnp.concatenate([kv_mla[..., :MLA_NOPE], k_rope], axis=-1)
        out_mla = _make_softmax_attn(tag + "mla", mla_scale, True, False, MLA_NOPE + MLA_ROPE)(
            q_full, k_full, kv_mla[..., MLA_NOPE:])

        mkv = _make_mm(tag + "mem_kv")(mem_n, wts["w_mem_kv"][l])
        out_mem = _make_softmax_attn(tag + "mem", head_scale, False, False, HEAD_DIM)(
            heads(p["mq"]), heads(mkv[:, :GROUP_W]), heads(mkv[:, GROUP_W:]))

        mixed = jnp.concatenate([o.reshape(s, GROUP_W) for o in (out_fox, out_sb, out_mla, out_mem)], axis=-1)
        gated = _make_gate(tag + "gate")(mixed, p["gate"])
        y = _make_mm(tag + "out")(gated, wts["w_out"][l])
        h = _make_ln(tag + "ln", True)(y, h, wts["ln_g"][l], wts["ln_b"][l])

    return _loss_op(h, target2d)


def _mesh_pos():
    x, y, c = (lax.axis_index(a) for a in MESH_AXES)
    return x, y, c, 4 * x + 2 * y + c


def _peer(x, y, c, mask):
    return (x ^ ((mask >> 2) & 1), y ^ ((mask >> 1) & 1), c ^ (mask & 1))


_ANY = pl.BlockSpec(memory_space=pl.ANY)


def _all_gather(row_shards, stack_shards):
    n_row, n_all = len(row_shards), len(row_shards) + len(stack_shards)
    shards = list(row_shards) + list(stack_shards)

    def body(*refs):
        ins, outs = refs[:n_all], refs[n_all:2 * n_all]
        send_sems, recv_sems, local_sems = refs[2 * n_all:]
        x, y, c, me = _mesh_pos()

        def window(t, slot):
            if t < n_row:
                rows = shards[t].shape[1]
                return outs[t].at[:, pl.ds(slot * rows, rows), :]
            return outs[t].at[slot]

        local = [pltpu.make_async_copy(ins[t], window(t, me), local_sems.at[t]) for t in range(n_all)]
        for cp in local:
            cp.start()
        sends = []
        for mask in range(1, N_DEV):
            for t in range(n_all):
                cp = pltpu.make_async_remote_copy(
                    src_ref=ins[t], dst_ref=window(t, me), send_sem=send_sems.at[t, mask - 1],
                    recv_sem=recv_sems.at[t, mask - 1], device_id=_peer(x, y, c, mask),
                    device_id_type=pl.DeviceIdType.MESH)
                cp.start()
                sends.append(cp)
        for mask in range(1, N_DEV):
            for t in range(n_all):
                pltpu.make_async_remote_copy(
                    src_ref=ins[t], dst_ref=window(t, me ^ mask), send_sem=send_sems.at[t, mask - 1],
                    recv_sem=recv_sems.at[t, mask - 1], device_id=_peer(x, y, c, mask),
                    device_id_type=pl.DeviceIdType.MESH).wait_recv()
        for cp in sends:
            cp.wait_send()
        for cp in local:
            cp.wait()

    out_shape = [jax.ShapeDtypeStruct((a.shape[0], N_DEV * a.shape[1], a.shape[2]), a.dtype) for a in row_shards]
    out_shape += [jax.ShapeDtypeStruct((N_DEV,) + a.shape, a.dtype) for a in stack_shards]
    return pl.pallas_call(
        body, name="all_gather_weights", in_specs=[_ANY] * n_all, out_specs=[_ANY] * n_all, out_shape=out_shape,
        scratch_shapes=[pltpu.SemaphoreType.DMA((n_all, N_DEV - 1)), pltpu.SemaphoreType.DMA((n_all, N_DEV - 1)),
                        pltpu.SemaphoreType.DMA((n_all,))],
    )(*shards)


def _reduce_scatter(row_full, stack_full, bcast):
    n_row, n_stack = len(row_full), len(stack_full)
    n_all = n_row + n_stack + len(bcast)
    fulls = list(row_full) + list(stack_full) + list(bcast)

    def body(*refs):
        ins, outs = refs[:n_all], refs[n_all:2 * n_all]
        send_sems, recv_sems, local_sems = refs[2 * n_all:]
        x, y, c, me = _mesh_pos()

        def part(t, slot):
            if t < n_row:
                rows = fulls[t].shape[1] // N_DEV
                return ins[t].at[:, pl.ds(slot * rows, rows), :]
            if t < n_row + n_stack:
                return ins[t].at[slot]
            return ins[t]

        local = [pltpu.make_async_copy(part(t, me), outs[t].at[me], local_sems.at[t]) for t in range(n_all)]
        for cp in local:
            cp.start()
        sends = []
        for mask in range(1, N_DEV):
            for t in range(n_all):
                cp = pltpu.make_async_remote_copy(
                    src_ref=part(t, me ^ mask), dst_ref=outs[t].at[me], send_sem=send_sems.at[t, mask - 1],
                    recv_sem=recv_sems.at[t, mask - 1], device_id=_peer(x, y, c, mask),
                    device_id_type=pl.DeviceIdType.MESH)
                cp.start()
                sends.append(cp)
        for mask in range(1, N_DEV):
            for t in range(n_all):
                pltpu.make_async_remote_copy(
                    src_ref=part(t, me), dst_ref=outs[t].at[me ^ mask], send_sem=send_sems.at[t, mask - 1],
                    recv_sem=recv_sems.at[t, mask - 1], device_id=_peer(x, y, c, mask),
                    device_id_type=pl.DeviceIdType.MESH).wait_recv()
        for cp in sends:
            cp.wait_send()
        for cp in local:
            cp.wait()

    out_shape = [jax.ShapeDtypeStruct((N_DEV, a.shape[0], a.shape[1] // N_DEV, a.shape[2]), a.dtype) for a in row_full]
    out_shape += [jax.ShapeDtypeStruct(a.shape, a.dtype) for a in stack_full]
    out_shape += [jax.ShapeDtypeStruct((N_DEV,) + a.shape, a.dtype) for a in bcast]
    return pl.pallas_call(
        body, name="reduce_scatter_grads", in_specs=[_ANY] * n_all, out_specs=[_ANY] * n_all, out_shape=out_shape,
        scratch_shapes=[pltpu.SemaphoreType.DMA((n_all, N_DEV - 1)), pltpu.SemaphoreType.DMA((n_all, N_DEV - 1)),
                        pltpu.SemaphoreType.DMA((n_all,))],
    )(*fulls)


def _adamw(slots, w, m, v, name):
    shape = w.shape
    cols = shape[-1]
    rows = math.prod(shape[:-1])
    tr = _pick(rows, (64, 32, 16, 8))
    c1 = 1.0 - ADAM_B1 ** ADAM_STEP
    c2 = 1.0 - ADAM_B2 ** ADAM_STEP

    def body(s_ref, w_ref, m_ref, v_ref, g_ref, d_ref, nm_ref, nv_ref):
        g = s_ref[0]
        for k in range(1, N_DEV):
            g = g + s_ref[k]
        nm = ADAM_B1 * m_ref[...] + (1.0 - ADAM_B1) * g
        nv = ADAM_B2 * v_ref[...] + (1.0 - ADAM_B2) * (g * g)
        g_ref[...] = g
        nm_ref[...] = nm
        nv_ref[...] = nv
        d_ref[...] = -ADAM_LR * ((nm / c1) / (jnp.sqrt(nv / c2) + ADAM_EPS) + ADAM_WD * w_ref[...])

    row = pl.BlockSpec((tr, cols), lambda i: (i, 0))
    out = jax.ShapeDtypeStruct((rows, cols), F32)
    outs = pl.pallas_call(
        body, name=name, grid=(rows // tr,),
        in_specs=[pl.BlockSpec((N_DEV, tr, cols), lambda i: (0, i, 0)), row, row, row],
        out_specs=[row] * 4, out_shape=[out] * 4, compiler_params=_cp(("parallel",)),
    )(slots.reshape(N_DEV, rows, cols), w.reshape(rows, cols), m.reshape(rows, cols), v.reshape(rows, cols))
    return [o.reshape(shape) for o in outs]


_SMALL = ("ln_in_g", "ln_in_b", "mem_ln_g", "mem_ln_b", "b_forget", "mla_q_norm_g", "mla_kv_norm_g", "ln_g", "ln_b")
_ORDER = ("ln_in_g", "ln_in_b", "mem_ln_g", "mem_ln_b", "w_in", "b_forget", "mla_q_norm_g", "w_mla_q_up",
          "mla_kv_norm_g", "w_mla_kv_up", "w_mem_kv", "w_out", "ln_g", "ln_b")


def _pack_small(d):
    flat = jnp.concatenate([d[n].reshape(-1) for n in _SMALL])
    n = flat.shape[0]
    padded = ((n + 8 * LANE - 1) // (8 * LANE)) * (8 * LANE)
    return jnp.pad(flat, (0, padded - n)).reshape(-1, LANE)


def _unpack_small(packed, like):
    flat, out, off = packed.reshape(-1), {}, 0
    for n in _SMALL:
        size = math.prod(like[n].shape)
        out[n] = flat[off:off + size].reshape(like[n].shape)
        off += size
    return out


def _unstack_cols(g):
    n, l, r, c = g.shape
    return g.transpose(1, 2, 0, 3).reshape(l, r, n * c)


def _stack_cols(g):
    l, r, nc = g.shape
    return g.reshape(l, r, N_DEV, nc // N_DEV).transpose(2, 0, 1, 3)


def kernel(x, mem, ln_in_g, ln_in_b, mem_ln_g, mem_ln_b, w_in, b_forget, mla_q_norm_g, w_mla_q_up, mla_kv_norm_g, w_mla_kv_up, w_mem_kv, w_out, ln_g, ln_b, loss_target, m_ln_in_g, m_ln_in_b, m_mem_ln_g, m_mem_ln_b, m_w_in, m_b_forget, m_mla_q_norm_g, m_w_mla_q_up, m_mla_kv_norm_g, m_w_mla_kv_up, m_w_mem_kv, m_w_out, m_ln_g, m_ln_b, v_ln_in_g, v_ln_in_b, v_mem_ln_g, v_mem_ln_b, v_w_in, v_b_forget, v_mla_q_norm_g, v_w_mla_q_up, v_mla_kv_norm_g, v_w_mla_kv_up, v_w_mem_kv, v_w_out, v_ln_g, v_ln_b):
    w_shard = dict(ln_in_g=ln_in_g, ln_in_b=ln_in_b, mem_ln_g=mem_ln_g, mem_ln_b=mem_ln_b, w_in=w_in,
                   b_forget=b_forget, mla_q_norm_g=mla_q_norm_g, w_mla_q_up=w_mla_q_up,
                   mla_kv_norm_g=mla_kv_norm_g, w_mla_kv_up=w_mla_kv_up, w_mem_kv=w_mem_kv, w_out=w_out,
                   ln_g=ln_g, ln_b=ln_b)
    m_shard = dict(ln_in_g=m_ln_in_g, ln_in_b=m_ln_in_b, mem_ln_g=m_mem_ln_g, mem_ln_b=m_mem_ln_b, w_in=m_w_in,
                   b_forget=m_b_forget, mla_q_norm_g=m_mla_q_norm_g, w_mla_q_up=m_w_mla_q_up,
                   mla_kv_norm_g=m_mla_kv_norm_g, w_mla_kv_up=m_w_mla_kv_up, w_mem_kv=m_w_mem_kv, w_out=m_w_out,
                   ln_g=m_ln_g, ln_b=m_ln_b)
    v_shard = dict(ln_in_g=v_ln_in_g, ln_in_b=v_ln_in_b, mem_ln_g=v_mem_ln_g, mem_ln_b=v_mem_ln_b, w_in=v_w_in,
                   b_forget=v_b_forget, mla_q_norm_g=v_mla_q_norm_g, w_mla_q_up=v_w_mla_q_up,
                   mla_kv_norm_g=v_mla_kv_norm_g, w_mla_kv_up=v_w_mla_kv_up, w_mem_kv=v_w_mem_kv, w_out=v_w_out,
                   ln_g=v_ln_g, ln_b=v_ln_b)

    g_in, g_mem, g_out, g_qup, g_kvup = _all_gather([w_in, w_mem_kv, w_out], [w_mla_q_up, w_mla_kv_up])
    full = dict(w_shard)
    full.update(w_in=g_in, w_mem_kv=g_mem, w_out=g_out, w_mla_q_up=_unstack_cols(g_qup),
                w_mla_kv_up=_unstack_cols(g_kvup))

    loss_local, (grad_w, grad_x) = jax.value_and_grad(_trunk_loss, argnums=(0, 1))(
        full, x[0], mem[0], loss_target[0])

    s_in, s_mem, s_out, s_qup, s_kvup, s_small = _reduce_scatter(
        [grad_w["w_in"], grad_w["w_mem_kv"], grad_w["w_out"]],
        [_stack_cols(grad_w["w_mla_q_up"]), _stack_cols(grad_w["w_mla_kv_up"])],
        [_pack_small(grad_w)])

    res = {}
    for name, slots in (("w_in", s_in), ("w_mem_kv", s_mem), ("w_out", s_out), ("w_mla_q_up", s_qup),
                        ("w_mla_kv_up", s_kvup)):
        res[name] = _adamw(slots, w_shard[name], m_shard[name], v_shard[name], "adamw_" + name)
    small = _adamw(s_small, _pack_small(w_shard), _pack_small(m_shard), _pack_small(v_shard), "adamw_small")
    small = [_unpack_small(a, w_shard) for a in small]
    for name in _SMALL:
        res[name] = [a[name] for a in small]

    loss = lax.psum(loss_local, MESH_AXES)
    outs = [loss, grad_x[None]]
    for k in range(4):
        outs += [res[name][k] for name in _ORDER]
    return tuple(outs)
```

```python
import functools
import math

import jax
import jax.numpy as jnp
from jax import lax
from jax.experimental import pallas as pl
from jax.experimental.pallas import tpu as pltpu

F32 = jnp.float32
BF16 = jnp.bfloat16

D_MODEL = 1024
DEPTH = 2
GROUP_W = 256
N_HEADS = 4
HEAD_DIM = 64
MLA_Q_RANK = 256
MLA_KV_RANK = 128
MLA_NOPE = 64
MLA_ROPE = 32
MLA_V = 64
ROPE_THETA = 10000.0
LN_EPS = 1e-5
RMS_EPS = 1e-6
DEEPNORM_ALPHA = (2 * DEPTH) ** 0.25
SPLIT_SIZES = (256, 256, 256, 4, 256, 256, 256, 256, 128, 32, 256, 1024)
IN_COLS = sum(SPLIT_SIZES)
_ORIG_OFF = [sum(SPLIT_SIZES[:i]) for i in range(len(SPLIT_SIZES))]
_PERM = (("fq", 0), ("fk", 1), ("fv", 2), ("sq", 4), ("sk", 5), ("sv", 6), ("c_q", 7), ("c_kv", 8),
         ("mq", 10), ("gate", 11), ("k_rot", 9), ("f_logit", 3))
LANE = 128
PROJ_COLS = ((IN_COLS + LANE - 1) // LANE) * LANE

ADAM_LR = 0.001
ADAM_B1 = 0.9
ADAM_B2 = 0.999
ADAM_EPS = 1e-08
ADAM_WD = 0.01
ADAM_STEP = 10

N_DEV = 8
MESH_AXES = ("x", "y", "c")
VMEM_LIMIT = 48 * 1024 * 1024
ATTN_VMEM_LIMIT = 56 * 1024 * 1024
ATTN_BQ = 512
ATTN_BK = 512
CUMSUM_CHUNK = 256
NEG_BIG = -1e30
LOG2E = math.log2(math.e)
MM_TM, MM_TN, MM_TK, MM_TK_NT = 1024, 1664, 1024, 3328

_NT = (((1,), (1,)), ((), ()))
_NN = (((1,), (0,)), ((), ()))


def _cp(sem, vmem=VMEM_LIMIT):
    return pltpu.CompilerParams(dimension_semantics=sem, vmem_limit_bytes=vmem)


def _dot(a, b, dims=_NN):
    return lax.dot_general(a, b, dims, preferred_element_type=F32)


def _pick(n, cands):
    for c in cands:
        if c <= n and n % c == 0:
            return c
    return n


def _tile(n, cap):
    if n <= cap:
        return n
    best = None
    for d in range(LANE, cap + 1, LANE):
        if n % d == 0:
            best = d
    assert best is not None, (n, cap)
    return best


def _matmul(a, b, mode, name):
    if mode == "nn":
        (M, K), (K2, N) = a.shape, b.shape
    else:
        (M, K), (N, K2) = a.shape, b.shape
    assert K == K2 and a.dtype == BF16 and b.dtype == BF16, (a.shape, b.shape, mode)
    tm, tn = _tile(M, MM_TM), _tile(N, MM_TN)
    tk = _tile(K, MM_TK if mode == "nn" else MM_TK_NT)
    nk = K // tk
    dims = _NN if mode == "nn" else _NT

    def body(a_ref, b_ref, o_ref, acc_ref):
        part = _dot(a_ref[...], b_ref[...], dims)
        if nk == 1:
            o_ref[...] = part
        else:
            k = pl.program_id(2)

            @pl.when(k == 0)
            def _():
                acc_ref[...] = part

            @pl.when(k > 0)
            def _():
                acc_ref[...] += part

            @pl.when(k == nk - 1)
            def _():
                o_ref[...] = acc_ref[...]

    a_spec = pl.BlockSpec((tm, tk), lambda j, i, k: (i, k))
    if mode == "nn":
        b_spec = pl.BlockSpec((tk, tn), lambda j, i, k: (k, j))
    else:
        b_spec = pl.BlockSpec((tn, tk), lambda j, i, k: (j, k))
    acc_shape = (tm, tn) if nk > 1 else (8, LANE)
    return pl.pallas_call(
        body, name=name, grid=(N // tn, M // tm, nk),
        in_specs=[a_spec, b_spec],
        out_specs=pl.BlockSpec((tm, tn), lambda j, i, k: (i, j)),
        out_shape=jax.ShapeDtypeStruct((M, N), F32),
        scratch_shapes=[pltpu.VMEM(acc_shape, F32)],
        compiler_params=_cp(("parallel", "parallel", "arbitrary")),
    )(a, b)


def _make_mm(name):
    @jax.custom_vjp
    def mm(a, w):
        return _matmul(a.astype(BF16), w.astype(BF16), "nn", name + "_fwd")

    def fwd(a, w):
        a16, w16 = a.astype(BF16), w.astype(BF16)
        return _matmul(a16, w16, "nn", name + "_fwd"), (a16, w16)

    def bwd(res, dy):
        a16, w16 = res
        dy16 = dy.astype(BF16)
        da = _matmul(dy16, w16, "nt", name + "_dx")
        dw = _matmul(a16.T, dy16, "nn", name + "_dw")
        return da, dw

    mm.defvjp(fwd, bwd)
    return mm


def _row_tile(rows):
    return _pick(rows, (512, 256, 128, 64, 32, 16, 8))


def _ln_stats(u):
    mu = jnp.mean(u, axis=-1, keepdims=True)
    d = u - mu
    var = jnp.mean(d * d, axis=-1, keepdims=True)
    return d, lax.rsqrt(var + LN_EPS)


def _ln_fwd_call(x, res, g, b, name):
    rows, dm = x.shape
    tr = _row_tile(rows)
    has_res = res is not None

    def body(*refs):
        if has_res:
            x_ref, r_ref, g_ref, b_ref, o_ref = refs
            u = DEEPNORM_ALPHA * r_ref[...] + x_ref[...]
        else:
            x_ref, g_ref, b_ref, o_ref = refs
            u = x_ref[...]
        d, rstd = _ln_stats(u)
        o_ref[...] = d * rstd * g_ref[...] + b_ref[...]

    row = pl.BlockSpec((tr, dm), lambda i: (i, 0))
    vec = pl.BlockSpec((1, dm), lambda i: (0, 0))
    args = (x, res) if has_res else (x,)
    return pl.pallas_call(
        body, name=name, grid=(rows // tr,),
        in_specs=[row] * len(args) + [vec, vec], out_specs=row,
        out_shape=jax.ShapeDtypeStruct((rows, dm), F32),
        compiler_params=_cp(("parallel",)),
    )(*args, g.reshape(1, dm), b.reshape(1, dm))


def _ln_bwd_call(dy, x, res, g, name):
    rows, dm = x.shape
    tr = _row_tile(rows)
    has_res = res is not None

    def body(*refs):
        if has_res:
            dy_ref, x_ref, r_ref, g_ref, dx_ref, dr_ref, dg_ref, db_ref = refs
            u = DEEPNORM_ALPHA * r_ref[...] + x_ref[...]
        else:
            dy_ref, x_ref, g_ref, dx_ref, dg_ref, db_ref = refs
            u = x_ref[...]
        i = pl.program_id(0)
        d, rstd = _ln_stats(u)
        xhat = d * rstd
        dyv = dy_ref[...]
        dxh = dyv * g_ref[...]
        m1 = jnp.mean(dxh, axis=-1, keepdims=True)
        m2 = jnp.mean(dxh * xhat, axis=-1, keepdims=True)
        du = rstd * (dxh - m1 - xhat * m2)
        dx_ref[...] = du
        if has_res:
            dr_ref[...] = DEEPNORM_ALPHA * du
        pg = jnp.sum(dyv * xhat, axis=0, keepdims=True)
        pb = jnp.sum(dyv, axis=0, keepdims=True)

        @pl.when(i == 0)
        def _():
            dg_ref[...] = pg
            db_ref[...] = pb

        @pl.when(i > 0)
        def _():
            dg_ref[...] += pg
            db_ref[...] += pb

    row = pl.BlockSpec((tr, dm), lambda i: (i, 0))
    vec = pl.BlockSpec((1, dm), lambda i: (0, 0))
    big = jax.ShapeDtypeStruct((rows, dm), F32)
    small = jax.ShapeDtypeStruct((1, dm), F32)
    args = (dy, x, res) if has_res else (dy, x)
    n_big = 2 if has_res else 1
    outs = pl.pallas_call(
        body, name=name, grid=(rows // tr,),
        in_specs=[row] * len(args) + [vec],
        out_specs=[row] * n_big + [vec, vec],
        out_shape=[big] * n_big + [small, small],
        compiler_params=_cp(("arbitrary",)),
    )(*args, g.reshape(1, dm))
    return outs


def _make_ln(name, has_res):
    if has_res:
        @jax.custom_vjp
        def ln(x, res, g, b):
            return _ln_fwd_call(x, res, g, b, name + "_fwd")

        def fwd(x, res, g, b):
            return ln(x, res, g, b), (x, res, g)

        def bwd(saved, dy):
            x, res, g = saved
            dx, dr, dg, db = _ln_bwd_call(dy, x, res, g, name + "_bwd")
            return dx, dr, dg.reshape(-1), db.reshape(-1)
    else:
        @jax.custom_vjp
        def ln(x, g, b):
            return _ln_fwd_call(x, None, g, b, name + "_fwd")

        def fwd(x, g, b):
            return ln(x, g, b), (x, g)

        def bwd(saved, dy):
            x, g = saved
            dx, dg, db = _ln_bwd_call(dy, x, None, g, name + "_bwd")
            return dx, dg.reshape(-1), db.reshape(-1)

    ln.defvjp(fwd, bwd)
    return ln


def _rms_fwd_call(x, g, name):
    rows, dm = x.shape
    tr = _row_tile(rows)

    def body(x_ref, g_ref, o_ref):
        xv = x_ref[...]
        rstd = lax.rsqrt(jnp.mean(xv * xv, axis=-1, keepdims=True) + RMS_EPS)
        o_ref[...] = xv * rstd * g_ref[...]

    row = pl.BlockSpec((tr, dm), lambda i: (i, 0))
    vec = pl.BlockSpec((1, dm), lambda i: (0, 0))
    return pl.pallas_call(
        body, name=name, grid=(rows // tr,), in_specs=[row, vec], out_specs=row,
        out_shape=jax.ShapeDtypeStruct((rows, dm), F32), compiler_params=_cp(("parallel",)),
    )(x, g.reshape(1, dm))


def _rms_bwd_call(dy, x, g, name):
    rows, dm = x.shape
    tr = _row_tile(rows)

    def body(dy_ref, x_ref, g_ref, dx_ref, dg_ref):
        i = pl.program_id(0)
        xv = x_ref[...]
        dyv = dy_ref[...]
        rstd = lax.rsqrt(jnp.mean(xv * xv, axis=-1, keepdims=True) + RMS_EPS)
        xhat = xv * rstd
        dxh = dyv * g_ref[...]
        m2 = jnp.mean(dxh * xhat, axis=-1, keepdims=True)
        dx_ref[...] = rstd * (dxh - xhat * m2)
        pg = jnp.sum(dyv * xhat, axis=0, keepdims=True)

        @pl.when(i == 0)
        def _():
            dg_ref[...] = pg

        @pl.when(i > 0)
        def _():
            dg_ref[...] += pg

    row = pl.BlockSpec((tr, dm), lambda i: (i, 0))
    vec = pl.BlockSpec((1, dm), lambda i: (0, 0))
    return pl.pallas_call(
        body, name=name, grid=(rows // tr,), in_specs=[row, row, vec], out_specs=[row, vec],
        out_shape=[jax.ShapeDtypeStruct((rows, dm), F32), jax.ShapeDtypeStruct((1, dm), F32)],
        compiler_params=_cp(("arbitrary",)),
    )(dy, x, g.reshape(1, dm))


def _make_rms(name):
    @jax.custom_vjp
    def rms(x, g):
        return _rms_fwd_call(x, g, name + "_fwd")

    def fwd(x, g):
        return rms(x, g), (x, g)

    def bwd(saved, dy):
        x, g = saved
        dx, dg = _rms_bwd_call(dy, x, g, name + "_bwd")
        return dx, dg.reshape(-1)

    rms.defvjp(fwd, bwd)
    return rms


def _sigmoid(x):
    return 1.0 / (1.0 + jnp.exp(-x))


def _gate_fwd_call(mixed, gate, name):
    rows, dm = mixed.shape
    tr = _row_tile(rows)

    def body(m_ref, g_ref, o_ref):
        gv = g_ref[...]
        o_ref[...] = m_ref[...] * (gv * _sigmoid(gv))

    row = pl.BlockSpec((tr, dm), lambda i: (i, 0))
    return pl.pallas_call(
        body, name=name, grid=(rows // tr,), in_specs=[row, row], out_specs=row,
        out_shape=jax.ShapeDtypeStruct((rows, dm), F32), compiler_params=_cp(("parallel",)),
    )(mixed, gate)


def _gate_bwd_call(dy, mixed, gate, name):
    rows, dm = mixed.shape
    tr = _row_tile(rows)

    def body(dy_ref, m_ref, g_ref, dm_ref, dg_ref):
        gv = g_ref[...]
        dyv = dy_ref[...]
        sg = _sigmoid(gv)
        dm_ref[...] = dyv * (gv * sg)
        dg_ref[...] = dyv * m_ref[...] * (sg * (1.0 + gv * (1.0 - sg)))

    row = pl.BlockSpec((tr, dm), lambda i: (i, 0))
    out = jax.ShapeDtypeStruct((rows, dm), F32)
    return pl.pallas_call(
        body, name=name, grid=(rows // tr,), in_specs=[row, row, row], out_specs=[row, row],
        out_shape=[out, out], compiler_params=_cp(("parallel",)),
    )(dy, mixed, gate)


def _make_gate(name):
    @jax.custom_vjp
    def gate_mul(mixed, gate):
        return _gate_fwd_call(mixed, gate, name + "_fwd")

    def fwd(mixed, gate):
        return gate_mul(mixed, gate), (mixed, gate)

    def bwd(saved, dy):
        mixed, gate = saved
        dmix, dgate = _gate_bwd_call(dy, mixed, gate, name + "_bwd")
        return dmix, dgate

    gate_mul.defvjp(fwd, bwd)
    return gate_mul


def _loss_call(y, t, name):
    rows, dm = y.shape
    tr = _row_tile(rows)

    def body(y_ref, t_ref, l_ref, d_ref):
        i = pl.program_id(0)
        e = y_ref[...] - t_ref[...]
        d_ref[...] = e * (1.0 / dm)
        part = 0.5 * jnp.sum(jnp.mean(e * e, axis=-1, keepdims=True), axis=0, keepdims=True)

        @pl.when(i == 0)
        def _():
            l_ref[...] = part

        @pl.when(i > 0)
        def _():
            l_ref[...] += part

    row = pl.BlockSpec((tr, dm), lambda i: (i, 0))
    one = pl.BlockSpec((1, 1), lambda i: (0, 0))
    return pl.pallas_call(
        body, name=name, grid=(rows // tr,), in_specs=[row, row], out_specs=[one, row],
        out_shape=[jax.ShapeDtypeStruct((1, 1), F32), jax.ShapeDtypeStruct((rows, dm), F32)],
        compiler_params=_cp(("arbitrary",)),
    )(y, t)


@jax.custom_vjp
def _loss_op(y, t):
    return _loss_call(y, t, "loss_head")[0][0, 0]


def _loss_fwd(y, t):
    l, d = _loss_call(y, t, "loss_head")
    return l[0, 0], d


def _loss_bwd(d, ct):
    return ct * d, jnp.zeros_like(d)


_loss_op.defvjp(_loss_fwd, _loss_bwd)


def _attn_blocks(S, Sk):
    bq, bk = min(ATTN_BQ, S), min(ATTN_BK, Sk)
    assert S % bq == 0 and Sk % bk == 0
    return bq, bk


def _valid_t(i, j, bq, bk, strict):
    key = j * bk + lax.broadcasted_iota(jnp.int32, (bk, bq), 0)
    qry = i * bq + lax.broadcasted_iota(jnp.int32, (bk, bq), 1)
    return (key < qry) if strict else (key <= qry)


def _sm_fwd_t(qT, k, vT, cmul, causal, name):
    H, DK, S = qT.shape
    Sk, dv = k.shape[1], vT.shape[1]
    bq, bk = _attn_blocks(S, Sk)
    nq, nkb = S // bq, Sk // bk
    if causal:
        assert S == Sk and bq == bk

    def body(qT_ref, k_ref, vT_ref, oT_ref, lse_ref):
        i = pl.program_id(1)
        qTb = qT_ref[...]

        def scores(j):
            off = pl.multiple_of(j * bk, bk)
            return _dot(k_ref[pl.ds(off, bk), :], qTb)

        def blk(j, carry, raw, masked):
            m, l, acc = carry
            off = pl.multiple_of(j * bk, bk)
            sT = raw * cmul
            if masked:
                sT = jnp.where(_valid_t(i, j, bq, bk, False), sT, NEG_BIG)
            m_new = jnp.maximum(m, jnp.max(sT, axis=0, keepdims=True))
            p = jnp.exp2(sT - m_new)
            a = jnp.exp2(m - m_new)
            l = a * l + jnp.sum(p, axis=0, keepdims=True)
            acc = a * acc + _dot(vT_ref[:, pl.ds(off, bk)], p.astype(BF16))
            return m_new, l, acc

        def step(j, c):
            nxt = scores(j + 1)
            return blk(j, c[:3], c[3], False) + (nxt,)

        last = i if causal else nkb - 1
        carry = (jnp.full((1, bq), NEG_BIG, F32), jnp.zeros((1, bq), F32), jnp.zeros((dv, bq), F32), scores(0))
        carry = lax.fori_loop(0, last, step, carry)
        m, l, acc = blk(last, carry[:3], carry[3], causal)
        oT_ref[...] = acc / l
        lse_ref[...] = m + jnp.log2(l)

    qcol = lambda d: pl.BlockSpec((None, d, bq), lambda h, i: (h, 0, i))
    return pl.pallas_call(
        body, name=name, grid=(H, nq),
        in_specs=[qcol(DK), pl.BlockSpec((None, Sk, DK), lambda h, i: (h, 0, 0)),
                  pl.BlockSpec((None, dv, Sk), lambda h, i: (h, 0, 0))],
        out_specs=[qcol(dv), qcol(1)],
        out_shape=[jax.ShapeDtypeStruct((H, dv, S), F32), jax.ShapeDtypeStruct((H, 1, S), F32)],
        compiler_params=_cp(("parallel", "arbitrary"), ATTN_VMEM_LIMIT),
    )(qT, k, vT)


def _sm_bwd_t(qT, qn, k, kT, v, oT, lse, doT, do, cmul, gscale, causal, name):
    H, DK, S = qT.shape
    Sk, dv = k.shape[1], v.shape[2]
    bq, bk = _attn_blocks(S, Sk)
    nq, nkb = S // bq, Sk // bk

    def body(qT_ref, qn_ref, k_ref, kT_ref, v_ref, oT_ref, lse_ref, doT_ref, do_ref, dqT_ref, dk_ref, dv_ref):
        i = pl.program_id(1)

        @pl.when(i == 0)
        def _():
            dk_ref[...] = jnp.zeros_like(dk_ref)
            dv_ref[...] = jnp.zeros_like(dv_ref)

        qTb = qT_ref[...]
        qnb = qn_ref[...]
        dob = do_ref[...]
        doTf = doT_ref[...]
        doTb = doTf.astype(BF16)
        delta = jnp.sum(doTf * oT_ref[...], axis=0, keepdims=True)
        lse = lse_ref[...]

        def blk(j, dq, masked):
            off = pl.multiple_of(j * bk, bk)
            sT = _dot(k_ref[pl.ds(off, bk), :], qTb) * cmul
            if masked:
                sT = jnp.where(_valid_t(i, j, bq, bk, False), sT, NEG_BIG)
            p = jnp.exp2(sT - lse)
            dp = _dot(v_ref[pl.ds(off, bk), :], doTb)
            ds = p * (dp - delta)
            dsb = (ds * gscale).astype(BF16) if gscale != 1.0 else ds.astype(BF16)
            dv_ref[pl.ds(off, bk), :] += _dot(p.astype(BF16), dob)
            dk_ref[pl.ds(off, bk), :] += _dot(dsb, qnb)
            return dq + _dot(kT_ref[:, pl.ds(off, bk)], dsb)

        dq = jnp.zeros((DK, bq), F32)
        if causal:
            dq = lax.fori_loop(0, i, lambda j, c: blk(j, c, False), dq)
            dq = blk(i, dq, True)
        else:
            dq = lax.fori_loop(0, nkb, lambda j, c: blk(j, c, False), dq)
        dqT_ref[...] = dq

    qcol = lambda d: pl.BlockSpec((None, d, bq), lambda h, i: (h, 0, i))
    qrow = lambda d: pl.BlockSpec((None, bq, d), lambda h, i: (h, i, 0))
    krow = lambda d: pl.BlockSpec((None, Sk, d), lambda h, i: (h, 0, 0))
    return pl.pallas_call(
        body, name=name, grid=(H, nq),
        in_specs=[qcol(DK), qrow(DK), krow(DK), pl.BlockSpec((None, DK, Sk), lambda h, i: (h, 0, 0)), krow(dv),
                  qcol(dv), qcol(1), qcol(dv), qrow(dv)],
        out_specs=[qcol(DK), krow(DK), krow(dv)],
        out_shape=[jax.ShapeDtypeStruct((H, DK, S), F32), jax.ShapeDtypeStruct((H, Sk, DK), F32),
                   jax.ShapeDtypeStruct((H, Sk, dv), F32)],
        compiler_params=_cp(("parallel", "arbitrary"), ATTN_VMEM_LIMIT),
    )(qT, qn, k, kT, v, oT, lse, doT, do)


def _tri(n, fn):
    r = lax.broadcasted_iota(jnp.int32, (n, n), 0)
    c = lax.broadcasted_iota(jnp.int32, (n, n), 1)
    return jnp.where(fn(r, c), 1.0, 0.0).astype(BF16)


def _key_cumsum(x, tri2, suffix, base):
    bk = x.shape[0]
    c = min(CUMSUM_CHUNK, bk)
    n = bk // c
    hi32 = lax.bitcast_convert_type(lax.bitcast_convert_type(x, jnp.int32) & jnp.int32(-65536), F32)
    hi = hi32.astype(BF16)
    lo = (x - hi32).astype(BF16)
    tot = [jnp.sum(x[a * c:(a + 1) * c], axis=0, keepdims=True) for a in range(n)]
    outs = []
    for a in range(n):
        row = base
        for t in (tot[a + 1:] if suffix else tot[:a]):
            row = row + t
        stacked = jnp.concatenate([hi[a * c:(a + 1) * c], lo[a * c:(a + 1) * c]], axis=0)
        outs.append(_dot(tri2, stacked) + row)
    total = tot[0]
    for t in tot[1:]:
        total = total + t
    return (outs[0] if n == 1 else jnp.concatenate(outs, axis=0)), total


def _tri2(n, fn):
    t = _tri(n, fn)
    return jnp.concatenate([t, t], axis=1)


def _sb_logs(z):
    neg_abs = lax.bitcast_convert_type(lax.bitcast_convert_type(z, jnp.int32) | jnp.int32(-2 ** 31), F32)
    ls = jnp.minimum(z, 0.0) - jnp.log(1.0 + jnp.exp(neg_abs))
    return ls, ls - z


PAIR = LANE // HEAD_DIM


def _head_lanes(shape, w, axis):
    idx = lax.broadcasted_iota(jnp.int32, shape, axis)
    return (idx >= HEAD_DIM * w) & (idx < HEAD_DIM * (w + 1))


def _bias_rows(w, bq):
    row = lax.broadcasted_iota(jnp.int32, (LANE, bq), 0)
    return jnp.where((row >= 3 * w) & (row < 3 * w + 3), -1.0, 0.0).astype(BF16)


def _merge_pair(parts):
    return jnp.where(_head_lanes(parts[0].shape, 0, 0), parts[0], parts[1]).T


def _smp_fwd(q2, k2, vT2, bias, r, causal, name):
    S, C = q2.shape
    Sk = k2.shape[0]
    bq, bk = _attn_blocks(S, Sk)
    nq, nkb, P = S // bq, Sk // bk, C // LANE
    use_f = bias is not None
    if causal:
        assert S == Sk and bq == bk

    def body(*refs):
        if use_f:
            q_ref, k_ref, vT_ref, b_ref, r_ref, o_ref, lse_ref = refs
        else:
            q_ref, k_ref, vT_ref, o_ref, lse_ref = refs
        i = pl.program_id(1)
        qp = q_ref[...]
        outs = []
        for w in range(PAIR):
            qT = jnp.where(_head_lanes(qp.shape, w, 1), qp, jnp.zeros_like(qp)).T
            if use_f:
                qT = jnp.concatenate([qT, _bias_rows(w, bq)], axis=0)

            def blk(j, carry, masked, qT=qT, w=w):
                m, l, acc = carry
                off = pl.multiple_of(j * bk, bk)
                kb = k_ref[pl.ds(off, bk), :]
                if use_f:
                    kb = jnp.concatenate([kb, b_ref[pl.ds(off, bk), :]], axis=1)
                sT = _dot(kb, qT) * LOG2E
                if masked:
                    sT = jnp.where(_valid_t(i, j, bq, bk, False), sT, NEG_BIG)
                cm = jnp.max(sT, axis=0, keepdims=True)
                if use_f:
                    cm = cm + r_ref[w]
                m_new = jnp.maximum(m, cm)
                shift = (m_new - r_ref[w]) if use_f else m_new
                p = jnp.exp2(sT - shift)
                a = jnp.exp2(m - m_new)
                l = a * l + jnp.sum(p, axis=0, keepdims=True)
                acc = a * acc + _dot(vT_ref[:, pl.ds(off, bk)], p.astype(BF16))
                return m_new, l, acc

            carry = (jnp.full((1, bq), NEG_BIG, F32), jnp.zeros((1, bq), F32), jnp.zeros((LANE, bq), F32))
            if causal:
                carry = lax.fori_loop(0, i, lambda j, c: blk(j, c, False), carry)
                carry = blk(i, carry, True)
            else:
                carry = lax.fori_loop(0, nkb, lambda j, c: blk(j, c, False), carry)
            m, l, acc = carry
            outs.append(acc / l)
            lse_ref[w] = m + jnp.log2(l)
        o_ref[...] = _merge_pair(outs)

    qblk = pl.BlockSpec((bq, LANE), lambda p, i: (i, p))
    kres = pl.BlockSpec((Sk, LANE), lambda p, i: (0, p))
    stat = pl.BlockSpec((PAIR, 1, bq), lambda p, i: (p, 0, i))
    in_specs = [qblk, kres, pl.BlockSpec((LANE, Sk), lambda p, i: (p, 0))]
    args = [q2, k2, vT2]
    if use_f:
        in_specs += [kres, stat]
        args += [bias, r]
    return pl.pallas_call(
        body, name=name, grid=(P, nq), in_specs=in_specs, out_specs=[qblk, stat],
        out_shape=[jax.ShapeDtypeStruct((S, C), F32), jax.ShapeDtypeStruct((PAIR * P, 1, S), F32)],
        compiler_params=_cp(("parallel", "arbitrary"), ATTN_VMEM_LIMIT),
    )(*args)


def _smp_bwd(q2, k2, kT2, v2, o2, lse, do2, bias, r, scale, causal, name):
    S, C = q2.shape
    Sk = k2.shape[0]
    bq, bk = _attn_blocks(S, Sk)
    nq, nkb, P = S // bq, Sk // bk, C // LANE
    use_f = bias is not None

    def body(*refs):
        if use_f:
            (q_ref, k_ref, kT_ref, v_ref, o_ref, lse_ref, do_ref, b_ref, r_ref,
             dq_ref, dk_ref, dv_ref, dr_ref, db_ref) = refs
        else:
            q_ref, k_ref, kT_ref, v_ref, o_ref, lse_ref, do_ref, dq_ref, dk_ref, dv_ref = refs
        i = pl.program_id(1)

        @pl.when(i == 0)
        def _():
            dk_ref[...] = jnp.zeros_like(dk_ref)
            dv_ref[...] = jnp.zeros_like(dv_ref)
            if use_f:
                db_ref[...] = jnp.zeros_like(db_ref)

        qp = q_ref[...]
        dof = do_ref[...]
        prod = dof * o_ref[...]
        dqs = []
        for w in range(PAIR):
            mine = _head_lanes(qp.shape, w, 1)
            qz = jnp.where(mine, qp, jnp.zeros_like(qp))
            qT = qz.T
            if use_f:
                qT = jnp.concatenate([qT, _bias_rows(w, bq)], axis=0)
            doz = jnp.where(mine, dof, 0.0).astype(BF16)
            doT = doz.T
            delta = jnp.sum(jnp.where(mine, prod, 0.0).T, axis=0, keepdims=True)
            shift = (lse_ref[w] - r_ref[w]) if use_f else lse_ref[w]

            def blk(j, carry, masked, qz=qz, qT=qT, doz=doz, doT=doT, delta=delta, shift=shift, w=w):
                dq, dr = carry
                off = pl.multiple_of(j * bk, bk)
                kb = k_ref[pl.ds(off, bk), :]
                if use_f:
                    kb = jnp.concatenate([kb, b_ref[pl.ds(off, bk), :]], axis=1)
                sT = _dot(kb, qT) * LOG2E
                if masked:
                    sT = jnp.where(_valid_t(i, j, bq, bk, False), sT, NEG_BIG)
                p = jnp.exp2(sT - shift)
                dp = _dot(v_ref[pl.ds(off, bk), :], doT)
                ds = p * (dp - delta)
                dsb = ds.astype(BF16)
                dv_ref[pl.ds(off, bk), :] += _dot(p.astype(BF16), doz)
                dk_ref[pl.ds(off, bk), :] += _dot(dsb, qz)
                if use_f:
                    dr = dr + jnp.sum(ds, axis=0, keepdims=True)
                    lane = lax.broadcasted_iota(jnp.int32, (bk, LANE), 1)
                    db_ref[pl.ds(off, bk), :] += jnp.where(lane == 3 * w, jnp.sum(ds, axis=1, keepdims=True), 0.0)
                return dq + _dot(kT_ref[:, pl.ds(off, bk)], dsb), dr

            carry = (jnp.zeros((LANE, bq), F32), jnp.zeros((1, bq), F32))
            if causal:
                carry = lax.fori_loop(0, i, lambda j, c: blk(j, c, False), carry)
                carry = blk(i, carry, True)
            else:
                carry = lax.fori_loop(0, nkb, lambda j, c: blk(j, c, False), carry)
            dqs.append(carry[0])
            if use_f:
                dr_ref[w] = carry[1]
        dq_ref[...] = _merge_pair(dqs) * scale

    qblk = pl.BlockSpec((bq, LANE), lambda p, i: (i, p))
    kres = pl.BlockSpec((Sk, LANE), lambda p, i: (0, p))
    stat = pl.BlockSpec((PAIR, 1, bq), lambda p, i: (p, 0, i))
    in_specs = [qblk, kres, pl.BlockSpec((LANE, Sk), lambda p, i: (p, 0)), kres, qblk, stat, qblk]
    args = [q2, k2, kT2, v2, o2, lse, do2]
    out_specs = [qblk, kres, kres]
    out_shape = [jax.ShapeDtypeStruct((S, C), F32), jax.ShapeDtypeStruct((Sk, C), F32),
                 jax.ShapeDtypeStruct((Sk, C), F32)]
    if use_f:
        in_specs += [kres, stat]
        args += [bias, r]
        out_specs += [stat, kres]
        out_shape += [jax.ShapeDtypeStruct((PAIR * P, 1, S), F32), jax.ShapeDtypeStruct((Sk, C), F32)]
    return pl.pallas_call(
        body, name=name, grid=(P, nq), in_specs=in_specs, out_specs=out_specs, out_shape=out_shape,
        compiler_params=_cp(("parallel", "arbitrary"), ATTN_VMEM_LIMIT),
    )(*args)


def _sbp_fwd(q2, k2, vT2, name):
    S, C = q2.shape
    bq, bk = _attn_blocks(S, S)
    assert bq == bk
    nq, P = S // bq, C // LANE
    c = min(CUMSUM_CHUNK, bk)

    def body(q_ref, k_ref, vT_ref, o_ref, lt_ref):
        i = pl.program_id(1)
        qp = q_ref[...]
        after = _tri2(c, lambda s, j: j > s)
        outs = []
        for w in range(PAIR):
            qT = jnp.where(_head_lanes(qp.shape, w, 1), qp, jnp.zeros_like(qp)).T

            def blk(jj, carry, masked, qT=qT):
                rsum, acc = carry
                j = i - jj
                off = pl.multiple_of(j * bk, bk)
                z = _dot(k_ref[pl.ds(off, bk), :], qT)
                ls, lk = _sb_logs(z)
                if masked:
                    valid = _valid_t(i, j, bq, bk, True)
                    lk = jnp.where(valid, lk, 0.0)
                tail, tot = _key_cumsum(lk, after, True, rsum)
                wgt = jnp.exp(ls + tail)
                if masked:
                    wgt = jnp.where(valid, wgt, 0.0)
                return rsum + tot, acc + _dot(vT_ref[:, pl.ds(off, bk)], wgt.astype(BF16))

            carry = (jnp.zeros((1, bq), F32), jnp.zeros((LANE, bq), F32))
            carry = blk(0, carry, True)
            carry = lax.fori_loop(1, i + 1, lambda jj, cr: blk(jj, cr, False), carry)
            lt_ref[w] = carry[0]
            outs.append(carry[1])
        o_ref[...] = _merge_pair(outs)

    qblk = pl.BlockSpec((bq, LANE), lambda p, i: (i, p))
    stat = pl.BlockSpec((PAIR, 1, bq), lambda p, i: (p, 0, i))
    return pl.pallas_call(
        body, name=name, grid=(P, nq),
        in_specs=[qblk, pl.BlockSpec((S, LANE), lambda p, i: (0, p)), pl.BlockSpec((LANE, S), lambda p, i: (p, 0))],
        out_specs=[qblk, stat],
        out_shape=[jax.ShapeDtypeStruct((S, C), F32), jax.ShapeDtypeStruct((PAIR * P, 1, S), F32)],
        compiler_params=_cp(("parallel", "arbitrary"), ATTN_VMEM_LIMIT),
    )(q2, k2, vT2)


def _sbp_bwd(q2, k2, kT2, v2, lt, do2, scale, name):
    S, C = q2.shape
    bq, bk = _attn_blocks(S, S)
    nq, P = S // bq, C // LANE
    c = min(CUMSUM_CHUNK, bk)

    def body(q_ref, k_ref, kT_ref, v_ref, lt_ref, do_ref, dq_ref, dk_ref, dv_ref):
        i = pl.program_id(1)

        @pl.when(i == 0)
        def _():
            dk_ref[...] = jnp.zeros_like(dk_ref)
            dv_ref[...] = jnp.zeros_like(dv_ref)

        qp = q_ref[...]
        dof = do_ref[...]
        upto = _tri2(c, lambda s, j: j <= s)
        before = _tri2(c, lambda s, j: j < s)
        dqs = []
        for w in range(PAIR):
            mine = _head_lanes(qp.shape, w, 1)
            qz = jnp.where(mine, qp, jnp.zeros_like(qp))
            qT = qz.T
            doz = jnp.where(mine, dof, 0.0).astype(BF16)
            doT = doz.T
            ltot = lt_ref[w]

            def blk(j, carry, masked, qz=qz, qT=qT, doz=doz, doT=doT, ltot=ltot):
                dq, rp, gp = carry
                off = pl.multiple_of(j * bk, bk)
                z = _dot(k_ref[pl.ds(off, bk), :], qT)
                ls, lk = _sb_logs(z)
                if masked:
                    valid = _valid_t(i, j, bq, bk, True)
                    lk = jnp.where(valid, lk, 0.0)
                pin, ltb = _key_cumsum(lk, upto, False, rp - ltot)
                wgt = jnp.exp(ls - pin)
                if masked:
                    wgt = jnp.where(valid, wgt, 0.0)
                g = _dot(v_ref[pl.ds(off, bk), :], doT) * wgt
                cin, gtb = _key_cumsum(g, before, False, gp)
                sig = jnp.exp(ls)
                dz = g * (1.0 - sig) - cin * sig
                if masked:
                    dz = jnp.where(valid, dz, 0.0)
                dzb = dz.astype(BF16)
                dv_ref[pl.ds(off, bk), :] += _dot(wgt.astype(BF16), doz)
                dk_ref[pl.ds(off, bk), :] += _dot(dzb, qz)
                return dq + _dot(kT_ref[:, pl.ds(off, bk)], dzb), rp + ltb, gp + gtb

            carry = (jnp.zeros((LANE, bq), F32), jnp.zeros((1, bq), F32), jnp.zeros((1, bq), F32))
            carry = lax.fori_loop(0, i, lambda j, cr: blk(j, cr, False), carry)
            carry = blk(i, carry, True)
            dqs.append(carry[0])
        dq_ref[...] = _merge_pair(dqs) * scale

    qblk = pl.BlockSpec((bq, LANE), lambda p, i: (i, p))
    kres = pl.BlockSpec((S, LANE), lambda p, i: (0, p))
    stat = pl.BlockSpec((PAIR, 1, bq), lambda p, i: (p, 0, i))
    return pl.pallas_call(
        body, name=name, grid=(P, nq),
        in_specs=[qblk, kres, pl.BlockSpec((LANE, S), lambda p, i: (p, 0)), kres, stat, qblk],
        out_specs=[qblk, kres, kres],
        out_shape=[jax.ShapeDtypeStruct((S, C), F32)] * 3,
        compiler_params=_cp(("parallel", "arbitrary"), ATTN_VMEM_LIMIT),
    )(q2, k2, kT2, v2, lt, do2)


def _bias_cols(f_cum):
    H, Sk = f_cum.shape
    terms = jnp.stack(_split3(f_cum), axis=-1)
    packed = terms.reshape(H // PAIR, PAIR, Sk, 3).transpose(2, 0, 1, 3).reshape(Sk, H // PAIR, PAIR * 3)
    return jnp.pad(packed, ((0, 0), (0, 0), (0, LANE - PAIR * 3))).reshape(Sk, -1)


def _make_packed_softmax(name, scale, causal, use_f):
    assert _pow2(scale)

    def run_fwd(q, k, v, f_cum):
        q16, k16, v16 = (q * scale).astype(BF16), k.astype(BF16), v.astype(BF16)
        bias = _bias_cols(f_cum) if use_f else None
        r = (f_cum * LOG2E)[:, None, :] if use_f else None
        o, lse = _smp_fwd(q16, k16, v16.T, bias, r, causal, name + "_fwd")
        return o, (q16, k16, v16, o, lse, bias, r)

    def run_bwd(saved, do):
        q16, k16, v16, o, lse, bias, r = saved
        outs = _smp_bwd(q16, k16, k16.T, v16, o, lse, do, bias, r, scale, causal, name + "_bwd")
        if use_f:
            dq, dk, dv, dr, db = outs
            H = dr.shape[0]
            dkey = db.reshape(db.shape[0], H // PAIR, LANE)[:, :, 0:3 * PAIR:3].reshape(db.shape[0], H).T
            return dq, dk, dv, dr[:, 0, :] - dkey
        return tuple(outs)

    if use_f:
        @jax.custom_vjp
        def attn(q, k, v, f_cum):
            return run_fwd(q, k, v, f_cum)[0]

        attn.defvjp(run_fwd, run_bwd)
    else:
        @jax.custom_vjp
        def attn(q, k, v):
            return run_fwd(q, k, v, None)[0]

        attn.defvjp(lambda q, k, v: run_fwd(q, k, v, None), run_bwd)
    return attn


def _make_packed_sb(name, scale):
    assert _pow2(scale)

    def run_fwd(q, k, v):
        q16, k16, v16 = (q * scale).astype(BF16), k.astype(BF16), v.astype(BF16)
        o, lt = _sbp_fwd(q16, k16, v16.T, name + "_fwd")
        return o, (q16, k16, v16, lt)

    def run_bwd(saved, do):
        q16, k16, v16, lt = saved
        return tuple(_sbp_bwd(q16, k16, k16.T, v16, lt, do, scale, name + "_bwd"))

    @jax.custom_vjp
    def attn(q, k, v):
        return run_fwd(q, k, v)[0]

    attn.defvjp(run_fwd, run_bwd)
    return attn


def _round_bf16(x):
    return lax.reduce_precision(x, exponent_bits=8, mantissa_bits=7)


def _split3(x):
    hi = _round_bf16(x)
    mid = _round_bf16(x - hi)
    lo = _round_bf16(x - hi - mid)
    return hi.astype(BF16), mid.astype(BF16), lo.astype(BF16)


def _pow2(x):
    m, _ = math.frexp(x)
    return m == 0.5


def _pad_last(x, n):
    return jnp.pad(x, [(0, 0)] * (x.ndim - 1) + [(0, n - x.shape[-1])])


def _layouts(q, k, scale):
    qh = _pad_last(jnp.transpose(q * scale if _pow2(scale) else q, (1, 0, 2)).astype(BF16), LANE)
    kh = _pad_last(jnp.transpose(k, (1, 0, 2)).astype(BF16), LANE)
    return qh, jnp.transpose(qh, (0, 2, 1)), kh, jnp.transpose(kh, (0, 2, 1))


def _make_softmax_attn(name, scale, causal, d):
    pre = _pow2(scale)
    cmul = LOG2E if pre else scale * LOG2E
    gscale = 1.0 if pre else scale

    def run_fwd(q, k, v):
        qn, qT, kn, kT = _layouts(q, k, scale)
        vn = jnp.transpose(v, (1, 0, 2)).astype(BF16)
        oT, lse = _sm_fwd_t(qT, kn, jnp.transpose(vn, (0, 2, 1)), cmul, causal, name + "_fwd")
        return jnp.transpose(oT, (2, 0, 1)), (qn, qT, kn, kT, vn, oT, lse)

    def run_bwd(saved, dout):
        qn, qT, kn, kT, vn, oT, lse = saved
        doT = jnp.transpose(dout, (1, 2, 0))
        do = jnp.transpose(dout, (1, 0, 2)).astype(BF16)
        dqT, dk, dv = _sm_bwd_t(qT, qn, kn, kT, vn, oT, lse, doT, do, cmul, gscale, causal, name + "_bwd")
        dq = jnp.transpose(dqT[:, :d, :], (2, 0, 1))
        if pre:
            dq = dq * scale
        return dq, jnp.transpose(dk[:, :, :d], (1, 0, 2)), jnp.transpose(dv, (1, 0, 2))

    @jax.custom_vjp
    def attn(q, k, v):
        return run_fwd(q, k, v)[0]

    attn.defvjp(run_fwd, run_bwd)
    return attn


def _rope(x, positions):
    half = x.shape[-1] // 2
    inv_freq = ROPE_THETA ** (-jnp.arange(half, dtype=F32) / half)
    ang = positions.astype(F32)[:, None] * inv_freq[None, :]
    ang = ang.reshape((ang.shape[0],) + (1,) * (x.ndim - 2) + (half,))
    cos, sin = jnp.cos(ang), jnp.sin(ang)
    x1, x2 = x[..., :half], x[..., half:]
    return jnp.concatenate([x1 * cos - x2 * sin, x1 * sin + x2 * cos], axis=-1)


def _permute_w_in(w):
    parts = [w[:, _ORIG_OFF[idx]:_ORIG_OFF[idx] + SPLIT_SIZES[idx]] for _, idx in _PERM]
    pad = jnp.zeros((w.shape[0], PROJ_COLS - IN_COLS), w.dtype)
    return jnp.concatenate(parts + [pad], axis=1)


@jax.custom_vjp
def _split_cols(proj):
    out, off = [], 0
    for _, idx in _PERM:
        out.append(proj[:, off:off + SPLIT_SIZES[idx]])
        off += SPLIT_SIZES[idx]
    return tuple(out)


def _split_cols_bwd(rows, cts):
    return (jnp.concatenate(list(cts) + [jnp.zeros((rows.shape[0], PROJ_COLS - IN_COLS), F32)], axis=1),)


_split_cols.defvjp(lambda proj: (_split_cols(proj), proj[:, :1]), _split_cols_bwd)


def _split_proj(proj):
    return {name: part for (name, _), part in zip(_PERM, _split_cols(proj))}


def _trunk_loss(wts, x2d, mem2d, target2d):
    s = x2d.shape[0]
    positions = jnp.arange(s)
    head_scale = HEAD_DIM ** -0.5
    mla_scale = (MLA_NOPE + MLA_ROPE) ** -0.5

    h = _make_ln("ln_in", False)(x2d, wts["ln_in_g"], wts["ln_in_b"])
    mem_n = _make_ln("ln_mem", False)(mem2d, wts["mem_ln_g"], wts["mem_ln_b"])

    for l in range(DEPTH):
        tag = f"l{l}_"
        proj = _make_mm(tag + "proj")(h, _permute_w_in(wts["w_in"][l]))
        p = _split_proj(proj)

        log_f = jax.nn.log_sigmoid(p["f_logit"] + wts["b_forget"][l])
        f_cum = jnp.cumsum(log_f, axis=0).T
        out_fox = _make_packed_softmax(tag + "fox", head_scale, True, True)(p["fq"], p["fk"], p["fv"], f_cum)

        out_sb = _make_packed_sb(tag + "sb", head_scale)(p["sq"], p["sk"], p["sv"])

        cqn = _make_rms(tag + "rms_q")(p["c_q"], wts["mla_q_norm_g"][l])
        q_mla = _make_mm(tag + "q_up")(cqn, wts["w_mla_q_up"][l]).reshape(s, N_HEADS, MLA_NOPE + MLA_ROPE)
        ckvn = _make_rms(tag + "rms_kv")(p["c_kv"], wts["mla_kv_norm_g"][l])
        kv_mla = _make_mm(tag + "kv_up")(ckvn, wts["w_mla_kv_up"][l]).reshape(s, N_HEADS, MLA_NOPE + MLA_V)
        q_full = jnp.concatenate([q_mla[..., :MLA_NOPE], _rope(q_mla[..., MLA_NOPE:], positions)], axis=-1)
        k_rope = jnp.broadcast_to(_rope(p["k_rot"], positions)[:, None, :], (s, N_HEADS, MLA_ROPE))
        k_full = jnp.concatenate([kv_mla[..., :MLA_NOPE], k_rope], axis=-1)
        out_mla = _make_softmax_attn(tag + "mla", mla_scale, True, MLA_NOPE + MLA_ROPE)(
            q_full, k_full, kv_mla[..., MLA_NOPE:]).reshape(s, GROUP_W)

        mkv = _make_mm(tag + "mem_kv")(mem_n, wts["w_mem_kv"][l])
        out_mem = _make_packed_softmax(tag + "mem", head_scale, False, False)(
            p["mq"], mkv[:, :GROUP_W], mkv[:, GROUP_W:])

        mixed = jnp.concatenate([out_fox, out_sb, out_mla, out_mem], axis=-1)
        gated = _make_gate(tag + "gate")(mixed, p["gate"])
        y = _make_mm(tag + "out")(gated, wts["w_out"][l])
        h = _make_ln(tag + "ln", True)(y, h, wts["ln_g"][l], wts["ln_b"][l])

    return _loss_op(h, target2d)


def _mesh_pos():
    x, y, c = (lax.axis_index(a) for a in MESH_AXES)
    return x, y, c, 4 * x + 2 * y + c


def _peer(x, y, c, mask):
    return (x ^ ((mask >> 2) & 1), y ^ ((mask >> 1) & 1), c ^ (mask & 1))


_ANY = pl.BlockSpec(memory_space=pl.ANY)


def _all_gather(row_shards, stack_shards):
    n_row, n_all = len(row_shards), len(row_shards) + len(stack_shards)
    shards = list(row_shards) + list(stack_shards)

    def body(*refs):
        ins, outs = refs[:n_all], refs[n_all:2 * n_all]
        send_sems, recv_sems, local_sems = refs[2 * n_all:]
        x, y, c, me = _mesh_pos()

        def window(t, slot):
            if t < n_row:
                rows = shards[t].shape[1]
                return outs[t].at[:, pl.ds(slot * rows, rows), :]
            return outs[t].at[slot]

        local = [pltpu.make_async_copy(ins[t], window(t, me), local_sems.at[t]) for t in range(n_all)]
        for cp in local:
            cp.start()
        sends = []
        for mask in range(1, N_DEV):
            for t in range(n_all):
                cp = pltpu.make_async_remote_copy(
                    src_ref=ins[t], dst_ref=window(t, me), send_sem=send_sems.at[t, mask - 1],
                    recv_sem=recv_sems.at[t, mask - 1], device_id=_peer(x, y, c, mask),
                    device_id_type=pl.DeviceIdType.MESH)
                cp.start()
                sends.append(cp)
        for mask in range(1, N_DEV):
            for t in range(n_all):
                pltpu.make_async_remote_copy(
                    src_ref=ins[t], dst_ref=window(t, me ^ mask), send_sem=send_sems.at[t, mask - 1],
                    recv_sem=recv_sems.at[t, mask - 1], device_id=_peer(x, y, c, mask),
                    device_id_type=pl.DeviceIdType.MESH).wait_recv()
        for cp in sends:
            cp.wait_send()
        for cp in local:
            cp.wait()

    out_shape = [jax.ShapeDtypeStruct((a.shape[0], N_DEV * a.shape[1], a.shape[2]), a.dtype) for a in row_shards]
    out_shape += [jax.ShapeDtypeStruct((N_DEV,) + a.shape, a.dtype) for a in stack_shards]
    return pl.pallas_call(
        body, name="all_gather_weights", in_specs=[_ANY] * n_all, out_specs=[_ANY] * n_all, out_shape=out_shape,
        scratch_shapes=[pltpu.SemaphoreType.DMA((n_all, N_DEV - 1)), pltpu.SemaphoreType.DMA((n_all, N_DEV - 1)),
                        pltpu.SemaphoreType.DMA((n_all,))],
    )(*shards)


def _reduce_scatter(row_full, stack_full, bcast):
    n_row, n_stack = len(row_full), len(stack_full)
    n_all = n_row + n_stack + len(bcast)
    fulls = list(row_full) + list(stack_full) + list(bcast)

    def body(*refs):
        ins, outs = refs[:n_all], refs[n_all:2 * n_all]
        send_sems, recv_sems, local_sems = refs[2 * n_all:]
        x, y, c, me = _mesh_pos()

        def part(t, slot):
            if t < n_row:
                rows = fulls[t].shape[1] // N_DEV
                return ins[t].at[:, pl.ds(slot * rows, rows), :]
            if t < n_row + n_stack:
                return ins[t].at[slot]
            return ins[t]

        local = [pltpu.make_async_copy(part(t, me), outs[t].at[me], local_sems.at[t]) for t in range(n_all)]
        for cp in local:
            cp.start()
        sends = []
        for mask in range(1, N_DEV):
            for t in range(n_all):
                cp = pltpu.make_async_remote_copy(
                    src_ref=part(t, me ^ mask), dst_ref=outs[t].at[me], send_sem=send_sems.at[t, mask - 1],
                    recv_sem=recv_sems.at[t, mask - 1], device_id=_peer(x, y, c, mask),
                    device_id_type=pl.DeviceIdType.MESH)
                cp.start()
                sends.append(cp)
        for mask in range(1, N_DEV):
            for t in range(n_all):
                pltpu.make_async_remote_copy(
                    src_ref=part(t, me), dst_ref=outs[t].at[me ^ mask], send_sem=send_sems.at[t, mask - 1],
                    recv_sem=recv_sems.at[t, mask - 1], device_id=_peer(x, y, c, mask),
                    device_id_type=pl.DeviceIdType.MESH).wait_recv()
        for cp in sends:
            cp.wait_send()
        for cp in local:
            cp.wait()

    out_shape = [jax.ShapeDtypeStruct((N_DEV, a.shape[0], a.shape[1] // N_DEV, a.shape[2]), a.dtype) for a in row_full]
    out_shape += [jax.ShapeDtypeStruct(a.shape, a.dtype) for a in stack_full]
    out_shape += [jax.ShapeDtypeStruct((N_DEV,) + a.shape, a.dtype) for a in bcast]
    return pl.pallas_call(
        body, name="reduce_scatter_grads", in_specs=[_ANY] * n_all, out_specs=[_ANY] * n_all, out_shape=out_shape,
        scratch_shapes=[pltpu.SemaphoreType.DMA((n_all, N_DEV - 1)), pltpu.SemaphoreType.DMA((n_all, N_DEV - 1)),
                        pltpu.SemaphoreType.DMA((n_all,))],
    )(*fulls)


def _adamw(slots, w, m, v, name):
    shape = w.shape
    cols = shape[-1]
    rows = math.prod(shape[:-1])
    tr = _pick(rows, (64, 32, 16, 8))
    c1 = 1.0 - ADAM_B1 ** ADAM_STEP
    c2 = 1.0 - ADAM_B2 ** ADAM_STEP

    def body(s_ref, w_ref, m_ref, v_ref, g_ref, d_ref, nm_ref, nv_ref):
        g = s_ref[0]
        for k in range(1, N_DEV):
            g = g + s_ref[k]
        nm = ADAM_B1 * m_ref[...] + (1.0 - ADAM_B1) * g
        nv = ADAM_B2 * v_ref[...] + (1.0 - ADAM_B2) * (g * g)
        g_ref[...] = g
        nm_ref[...] = nm
        nv_ref[...] = nv
        d_ref[...] = -ADAM_LR * ((nm / c1) / (jnp.sqrt(nv / c2) + ADAM_EPS) + ADAM_WD * w_ref[...])

    row = pl.BlockSpec((tr, cols), lambda i: (i, 0))
    out = jax.ShapeDtypeStruct((rows, cols), F32)
    outs = pl.pallas_call(
        body, name=name, grid=(rows // tr,),
        in_specs=[pl.BlockSpec((N_DEV, tr, cols), lambda i: (0, i, 0)), row, row, row],
        out_specs=[row] * 4, out_shape=[out] * 4, compiler_params=_cp(("parallel",)),
    )(slots.reshape(N_DEV, rows, cols), w.reshape(rows, cols), m.reshape(rows, cols), v.reshape(rows, cols))
    return [o.reshape(shape) for o in outs]


_SMALL = ("ln_in_g", "ln_in_b", "mem_ln_g", "mem_ln_b", "b_forget", "mla_q_norm_g", "mla_kv_norm_g", "ln_g", "ln_b")
_ORDER = ("ln_in_g", "ln_in_b", "mem_ln_g", "mem_ln_b", "w_in", "b_forget", "mla_q_norm_g", "w_mla_q_up",
          "mla_kv_norm_g", "w_mla_kv_up", "w_mem_kv", "w_out", "ln_g", "ln_b")


def _pack_small(d):
    flat = jnp.concatenate([d[n].reshape(-1) for n in _SMALL])
    n = flat.shape[0]
    padded = ((n + 8 * LANE - 1) // (8 * LANE)) * (8 * LANE)
    return jnp.pad(flat, (0, padded - n)).reshape(-1, LANE)


def _unpack_small(packed, like):
    flat, out, off = packed.reshape(-1), {}, 0
    for n in _SMALL:
        size = math.prod(like[n].shape)
        out[n] = flat[off:off + size].reshape(like[n].shape)
        off += size
    return out


def _unstack_cols(g):
    n, l, r, c = g.shape
    return g.transpose(1, 2, 0, 3).reshape(l, r, n * c)


def _stack_cols(g):
    l, r, nc = g.shape
    return g.reshape(l, r, N_DEV, nc // N_DEV).transpose(2, 0, 1, 3)


def kernel(x, mem, ln_in_g, ln_in_b, mem_ln_g, mem_ln_b, w_in, b_forget, mla_q_norm_g, w_mla_q_up, mla_kv_norm_g, w_mla_kv_up, w_mem_kv, w_out, ln_g, ln_b, loss_target, m_ln_in_g, m_ln_in_b, m_mem_ln_g, m_mem_ln_b, m_w_in, m_b_forget, m_mla_q_norm_g, m_w_mla_q_up, m_mla_kv_norm_g, m_w_mla_kv_up, m_w_mem_kv, m_w_out, m_ln_g, m_ln_b, v_ln_in_g, v_ln_in_b, v_mem_ln_g, v_mem_ln_b, v_w_in, v_b_forget, v_mla_q_norm_g, v_w_mla_q_up, v_mla_kv_norm_g, v_w_mla_kv_up, v_w_mem_kv, v_w_out, v_ln_g, v_ln_b):
    w_shard = dict(ln_in_g=ln_in_g, ln_in_b=ln_in_b, mem_ln_g=mem_ln_g, mem_ln_b=mem_ln_b, w_in=w_in,
                   b_forget=b_forget, mla_q_norm_g=mla_q_norm_g, w_mla_q_up=w_mla_q_up,
                   mla_kv_norm_g=mla_kv_norm_g, w_mla_kv_up=w_mla_kv_up, w_mem_kv=w_mem_kv, w_out=w_out,
                   ln_g=ln_g, ln_b=ln_b)
    m_shard = dict(ln_in_g=m_ln_in_g, ln_in_b=m_ln_in_b, mem_ln_g=m_mem_ln_g, mem_ln_b=m_mem_ln_b, w_in=m_w_in,
                   b_forget=m_b_forget, mla_q_norm_g=m_mla_q_norm_g, w_mla_q_up=m_w_mla_q_up,
                   mla_kv_norm_g=m_mla_kv_norm_g, w_mla_kv_up=m_w_mla_kv_up, w_mem_kv=m_w_mem_kv, w_out=m_w_out,
                   ln_g=m_ln_g, ln_b=m_ln_b)
    v_shard = dict(ln_in_g=v_ln_in_g, ln_in_b=v_ln_in_b, mem_ln_g=v_mem_ln_g, mem_ln_b=v_mem_ln_b, w_in=v_w_in,
                   b_forget=v_b_forget, mla_q_norm_g=v_mla_q_norm_g, w_mla_q_up=v_w_mla_q_up,
                   mla_kv_norm_g=v_mla_kv_norm_g, w_mla_kv_up=v_w_mla_kv_up, w_mem_kv=v_w_mem_kv, w_out=v_w_out,
                   ln_g=v_ln_g, ln_b=v_ln_b)

    g_in, g_mem, g_out, g_qup, g_kvup = _all_gather([w_in, w_mem_kv, w_out], [w_mla_q_up, w_mla_kv_up])
    full = dict(w_shard)
    full.update(w_in=g_in, w_mem_kv=g_mem, w_out=g_out, w_mla_q_up=_unstack_cols(g_qup),
                w_mla_kv_up=_unstack_cols(g_kvup))

    loss_local, (grad_w, grad_x) = jax.value_and_grad(_trunk_loss, argnums=(0, 1))(
        full, x[0], mem[0], loss_target[0])

    s_in, s_mem, s_out, s_qup, s_kvup, s_small = _reduce_scatter(
        [grad_w["w_in"], grad_w["w_mem_kv"], grad_w["w_out"]],
        [_stack_cols(grad_w["w_mla_q_up"]), _stack_cols(grad_w["w_mla_kv_up"])],
        [_pack_small(grad_w)])

    res = {}
    for name, slots in (("w_in", s_in), ("w_mem_kv", s_mem), ("w_out", s_out), ("w_mla_q_up", s_qup),
                        ("w_mla_kv_up", s_kvup)):
        res[name] = _adamw(slots, w_shard[name], m_shard[name], v_shard[name], "adamw_" + name)
    small = _adamw(s_small, _pack_small(w_shard), _pack_small(m_shard), _pack_small(v_shard), "adamw_small")
    small = [_unpack_small(a, w_shard) for a in small]
    for name in _SMALL:
        res[name] = [a[name] for a in small]

    loss = lax.psum(loss_local, MESH_AXES)
    outs = [loss, grad_x[None]]
    for k in range(4):
        outs += [res[name][k] for name in _ORDER]
    return tuple(outs)
```

```python
import functools
import math

import jax
import jax.numpy as jnp
from jax import lax
from jax.experimental import pallas as pl
from jax.experimental.pallas import tpu as pltpu

F32 = jnp.float32
BF16 = jnp.bfloat16

D_MODEL = 1024
DEPTH = 2
GROUP_W = 256
N_HEADS = 4
HEAD_DIM = 64
MLA_Q_RANK = 256
MLA_KV_RANK = 128
MLA_NOPE = 64
MLA_ROPE = 32
MLA_V = 64
ROPE_THETA = 10000.0
LN_EPS = 1e-5
RMS_EPS = 1e-6
DEEPNORM_ALPHA = (2 * DEPTH) ** 0.25
SPLIT_SIZES = (256, 256, 256, 4, 256, 256, 256, 256, 128, 32, 256, 1024)
IN_COLS = sum(SPLIT_SIZES)
_ORIG_OFF = [sum(SPLIT_SIZES[:i]) for i in range(len(SPLIT_SIZES))]
_PERM = (("fq", 0), ("fk", 1), ("fv", 2), ("sq", 4), ("sk", 5), ("sv", 6), ("c_q", 7), ("c_kv", 8),
         ("mq", 10), ("gate", 11), ("k_rot", 9), ("f_logit", 3))
LANE = 128
PROJ_COLS = ((IN_COLS + LANE - 1) // LANE) * LANE

ADAM_LR = 0.001
ADAM_B1 = 0.9
ADAM_B2 = 0.999
ADAM_EPS = 1e-08
ADAM_WD = 0.01
ADAM_STEP = 10

N_DEV = 8
MESH_AXES = ("x", "y", "c")
VMEM_LIMIT = 48 * 1024 * 1024
ATTN_VMEM_LIMIT = 56 * 1024 * 1024
ATTN_BQ = 512
ATTN_BK = 512
CUMSUM_CHUNK = 256
NEG_BIG = -1e30
LOG2E = math.log2(math.e)
MM_TM, MM_TN, MM_TK, MM_TK_NT = 1024, 1664, 1024, 3328

_NT = (((1,), (1,)), ((), ()))
_NN = (((1,), (0,)), ((), ()))


def _cp(sem, vmem=VMEM_LIMIT):
    return pltpu.CompilerParams(dimension_semantics=sem, vmem_limit_bytes=vmem)


def _dot(a, b, dims=_NN):
    return lax.dot_general(a, b, dims, preferred_element_type=F32)


def _pick(n, cands):
    for c in cands:
        if c <= n and n % c == 0:
            return c
    return n


def _tile(n, cap):
    if n <= cap:
        return n
    best = None
    for d in range(LANE, cap + 1, LANE):
        if n % d == 0:
            best = d
    assert best is not None, (n, cap)
    return best


def _matmul(a, b, mode, name):
    if mode == "nn":
        (M, K), (K2, N) = a.shape, b.shape
    else:
        (M, K), (N, K2) = a.shape, b.shape
    assert K == K2 and a.dtype == BF16 and b.dtype == BF16, (a.shape, b.shape, mode)
    tm, tn = _tile(M, MM_TM), _tile(N, MM_TN)
    tk = _tile(K, MM_TK if mode == "nn" else MM_TK_NT)
    nk = K // tk
    dims = _NN if mode == "nn" else _NT

    def body(a_ref, b_ref, o_ref, acc_ref):
        part = _dot(a_ref[...], b_ref[...], dims)
        if nk == 1:
            o_ref[...] = part
        else:
            k = pl.program_id(2)

            @pl.when(k == 0)
            def _():
                acc_ref[...] = part

            @pl.when(k > 0)
            def _():
                acc_ref[...] += part

            @pl.when(k == nk - 1)
            def _():
                o_ref[...] = acc_ref[...]

    a_spec = pl.BlockSpec((tm, tk), lambda j, i, k: (i, k))
    if mode == "nn":
        b_spec = pl.BlockSpec((tk, tn), lambda j, i, k: (k, j))
    else:
        b_spec = pl.BlockSpec((tn, tk), lambda j, i, k: (j, k))
    acc_shape = (tm, tn) if nk > 1 else (8, LANE)
    return pl.pallas_call(
        body, name=name, grid=(N // tn, M // tm, nk),
        in_specs=[a_spec, b_spec],
        out_specs=pl.BlockSpec((tm, tn), lambda j, i, k: (i, j)),
        out_shape=jax.ShapeDtypeStruct((M, N), F32),
        scratch_shapes=[pltpu.VMEM(acc_shape, F32)],
        compiler_params=_cp(("parallel", "parallel", "arbitrary")),
    )(a, b)


def _make_mm(name):
    @jax.custom_vjp
    def mm(a, w):
        return _matmul(a.astype(BF16), w.astype(BF16), "nn", name + "_fwd")

    def fwd(a, w):
        a16, w16 = a.astype(BF16), w.astype(BF16)
        return _matmul(a16, w16, "nn", name + "_fwd"), (a16, w16)

    def bwd(res, dy):
        a16, w16 = res
        dy16 = dy.astype(BF16)
        da = _matmul(dy16, w16, "nt", name + "_dx")
        dw = _matmul(a16.T, dy16, "nn", name + "_dw")
        return da, dw

    mm.defvjp(fwd, bwd)
    return mm


def _row_tile(rows):
    return _pick(rows, (512, 256, 128, 64, 32, 16, 8))


def _ln_stats(u):
    mu = jnp.mean(u, axis=-1, keepdims=True)
    d = u - mu
    var = jnp.mean(d * d, axis=-1, keepdims=True)
    return d, lax.rsqrt(var + LN_EPS)


def _ln_fwd_call(x, res, g, b, name):
    rows, dm = x.shape
    tr = _row_tile(rows)
    has_res = res is not None

    def body(*refs):
        if has_res:
            x_ref, r_ref, g_ref, b_ref, o_ref = refs
            u = DEEPNORM_ALPHA * r_ref[...] + x_ref[...]
        else:
            x_ref, g_ref, b_ref, o_ref = refs
            u = x_ref[...]
        d, rstd = _ln_stats(u)
        o_ref[...] = d * rstd * g_ref[...] + b_ref[...]

    row = pl.BlockSpec((tr, dm), lambda i: (i, 0))
    vec = pl.BlockSpec((1, dm), lambda i: (0, 0))
    args = (x, res) if has_res else (x,)
    return pl.pallas_call(
        body, name=name, grid=(rows // tr,),
        in_specs=[row] * len(args) + [vec, vec], out_specs=row,
        out_shape=jax.ShapeDtypeStruct((rows, dm), F32),
        compiler_params=_cp(("parallel",)),
    )(*args, g.reshape(1, dm), b.reshape(1, dm))


def _ln_bwd_call(dy, x, res, g, name):
    rows, dm = x.shape
    tr = _row_tile(rows)
    has_res = res is not None

    def body(*refs):
        if has_res:
            dy_ref, x_ref, r_ref, g_ref, dx_ref, dr_ref, dg_ref, db_ref = refs
            u = DEEPNORM_ALPHA * r_ref[...] + x_ref[...]
        else:
            dy_ref, x_ref, g_ref, dx_ref, dg_ref, db_ref = refs
            u = x_ref[...]
        i = pl.program_id(0)
        d, rstd = _ln_stats(u)
        xhat = d * rstd
        dyv = dy_ref[...]
        dxh = dyv * g_ref[...]
        m1 = jnp.mean(dxh, axis=-1, keepdims=True)
        m2 = jnp.mean(dxh * xhat, axis=-1, keepdims=True)
        du = rstd * (dxh - m1 - xhat * m2)
        dx_ref[...] = du
        if has_res:
            dr_ref[...] = DEEPNORM_ALPHA * du
        pg = jnp.sum(dyv * xhat, axis=0, keepdims=True)
        pb = jnp.sum(dyv, axis=0, keepdims=True)

        @pl.when(i == 0)
        def _():
            dg_ref[...] = pg
            db_ref[...] = pb

        @pl.when(i > 0)
        def _():
            dg_ref[...] += pg
            db_ref[...] += pb

    row = pl.BlockSpec((tr, dm), lambda i: (i, 0))
    vec = pl.BlockSpec((1, dm), lambda i: (0, 0))
    big = jax.ShapeDtypeStruct((rows, dm), F32)
    small = jax.ShapeDtypeStruct((1, dm), F32)
    args = (dy, x, res) if has_res else (dy, x)
    n_big = 2 if has_res else 1
    outs = pl.pallas_call(
        body, name=name, grid=(rows // tr,),
        in_specs=[row] * len(args) + [vec],
        out_specs=[row] * n_big + [vec, vec],
        out_shape=[big] * n_big + [small, small],
        compiler_params=_cp(("arbitrary",)),
    )(*args, g.reshape(1, dm))
    return outs


def _make_ln(name, has_res):
    if has_res:
        @jax.custom_vjp
        def ln(x, res, g, b):
            return _ln_fwd_call(x, res, g, b, name + "_fwd")

        def fwd(x, res, g, b):
            return ln(x, res, g, b), (x, res, g)

        def bwd(saved, dy):
            x, res, g = saved
            dx, dr, dg, db = _ln_bwd_call(dy, x, res, g, name + "_bwd")
            return dx, dr, dg.reshape(-1), db.reshape(-1)
    else:
        @jax.custom_vjp
        def ln(x, g, b):
            return _ln_fwd_call(x, None, g, b, name + "_fwd")

        def fwd(x, g, b):
            return ln(x, g, b), (x, g)

        def bwd(saved, dy):
            x, g = saved
            dx, dg, db = _ln_bwd_call(dy, x, None, g, name + "_bwd")
            return dx, dg.reshape(-1), db.reshape(-1)

    ln.defvjp(fwd, bwd)
    return ln


def _rms_fwd_call(x, g, name):
    rows, dm = x.shape
    tr = _row_tile(rows)

    def body(x_ref, g_ref, o_ref):
        xv = x_ref[...]
        rstd = lax.rsqrt(jnp.mean(xv * xv, axis=-1, keepdims=True) + RMS_EPS)
        o_ref[...] = xv * rstd * g_ref[...]

    row = pl.BlockSpec((tr, dm), lambda i: (i, 0))
    vec = pl.BlockSpec((1, dm), lambda i: (0, 0))
    return pl.pallas_call(
        body, name=name, grid=(rows // tr,), in_specs=[row, vec], out_specs=row,
        out_shape=jax.ShapeDtypeStruct((rows, dm), F32), compiler_params=_cp(("parallel",)),
    )(x, g.reshape(1, dm))


def _rms_bwd_call(dy, x, g, name):
    rows, dm = x.shape
    tr = _row_tile(rows)

    def body(dy_ref, x_ref, g_ref, dx_ref, dg_ref):
        i = pl.program_id(0)
        xv = x_ref[...]
        dyv = dy_ref[...]
        rstd = lax.rsqrt(jnp.mean(xv * xv, axis=-1, keepdims=True) + RMS_EPS)
        xhat = xv * rstd
        dxh = dyv * g_ref[...]
        m2 = jnp.mean(dxh * xhat, axis=-1, keepdims=True)
        dx_ref[...] = rstd * (dxh - xhat * m2)
        pg = jnp.sum(dyv * xhat, axis=0, keepdims=True)

        @pl.when(i == 0)
        def _():
            dg_ref[...] = pg

        @pl.when(i > 0)
        def _():
            dg_ref[...] += pg

    row = pl.BlockSpec((tr, dm), lambda i: (i, 0))
    vec = pl.BlockSpec((1, dm), lambda i: (0, 0))
    return pl.pallas_call(
        body, name=name, grid=(rows // tr,), in_specs=[row, row, vec], out_specs=[row, vec],
        out_shape=[jax.ShapeDtypeStruct((rows, dm), F32), jax.ShapeDtypeStruct((1, dm), F32)],
        compiler_params=_cp(("arbitrary",)),
    )(dy, x, g.reshape(1, dm))


def _make_rms(name):
    @jax.custom_vjp
    def rms(x, g):
        return _rms_fwd_call(x, g, name + "_fwd")

    def fwd(x, g):
        return rms(x, g), (x, g)

    def bwd(saved, dy):
        x, g = saved
        dx, dg = _rms_bwd_call(dy, x, g, name + "_bwd")
        return dx, dg.reshape(-1)

    rms.defvjp(fwd, bwd)
    return rms


def _sigmoid(x):
    return 1.0 / (1.0 + jnp.exp(-x))


def _gate_fwd_call(mixed, gate, name):
    rows, dm = mixed.shape
    tr = _row_tile(rows)

    def body(m_ref, g_ref, o_ref):
        gv = g_ref[...]
        o_ref[...] = m_ref[...] * (gv * _sigmoid(gv))

    row = pl.BlockSpec((tr, dm), lambda i: (i, 0))
    return pl.pallas_call(
        body, name=name, grid=(rows // tr,), in_specs=[row, row], out_specs=row,
        out_shape=jax.ShapeDtypeStruct((rows, dm), F32), compiler_params=_cp(("parallel",)),
    )(mixed, gate)


def _gate_bwd_call(dy, mixed, gate, name):
    rows, dm = mixed.shape
    tr = _row_tile(rows)

    def body(dy_ref, m_ref, g_ref, dm_ref, dg_ref):
        gv = g_ref[...]
        dyv = dy_ref[...]
        sg = _sigmoid(gv)
        dm_ref[...] = dyv * (gv * sg)
        dg_ref[...] = dyv * m_ref[...] * (sg * (1.0 + gv * (1.0 - sg)))

    row = pl.BlockSpec((tr, dm), lambda i: (i, 0))
    out = jax.ShapeDtypeStruct((rows, dm), F32)
    return pl.pallas_call(
        body, name=name, grid=(rows // tr,), in_specs=[row, row, row], out_specs=[row, row],
        out_shape=[out, out], compiler_params=_cp(("parallel",)),
    )(dy, mixed, gate)


def _make_gate(name):
    @jax.custom_vjp
    def gate_mul(mixed, gate):
        return _gate_fwd_call(mixed, gate, name + "_fwd")

    def fwd(mixed, gate):
        return gate_mul(mixed, gate), (mixed, gate)

    def bwd(saved, dy):
        mixed, gate = saved
        dmix, dgate = _gate_bwd_call(dy, mixed, gate, name + "_bwd")
        return dmix, dgate

    gate_mul.defvjp(fwd, bwd)
    return gate_mul


def _loss_call(y, t, name):
    rows, dm = y.shape
    tr = _row_tile(rows)

    def body(y_ref, t_ref, l_ref, d_ref):
        i = pl.program_id(0)
        e = y_ref[...] - t_ref[...]
        d_ref[...] = e * (1.0 / dm)
        part = 0.5 * jnp.sum(jnp.mean(e * e, axis=-1, keepdims=True), axis=0, keepdims=True)

        @pl.when(i == 0)
        def _():
            l_ref[...] = part

        @pl.when(i > 0)
        def _():
            l_ref[...] += part

    row = pl.BlockSpec((tr, dm), lambda i: (i, 0))
    one = pl.BlockSpec((1, 1), lambda i: (0, 0))
    return pl.pallas_call(
        body, name=name, grid=(rows // tr,), in_specs=[row, row], out_specs=[one, row],
        out_shape=[jax.ShapeDtypeStruct((1, 1), F32), jax.ShapeDtypeStruct((rows, dm), F32)],
        compiler_params=_cp(("arbitrary",)),
    )(y, t)


@jax.custom_vjp
def _loss_op(y, t):
    return _loss_call(y, t, "loss_head")[0][0, 0]


def _loss_fwd(y, t):
    l, d = _loss_call(y, t, "loss_head")
    return l[0, 0], d


def _loss_bwd(d, ct):
    return ct * d, jnp.zeros_like(d)


_loss_op.defvjp(_loss_fwd, _loss_bwd)


def _attn_blocks(S, Sk):
    bq, bk = min(ATTN_BQ, S), min(ATTN_BK, Sk)
    assert S % bq == 0 and Sk % bk == 0
    return bq, bk


def _valid_t(i, j, bq, bk, strict):
    key = j * bk + lax.broadcasted_iota(jnp.int32, (bk, bq), 0)
    qry = i * bq + lax.broadcasted_iota(jnp.int32, (bk, bq), 1)
    return (key < qry) if strict else (key <= qry)


def _sm_fwd_t(qT, k, vT, cmul, causal, name):
    H, DK, S = qT.shape
    Sk, dv = k.shape[1], vT.shape[1]
    bq, bk = _attn_blocks(S, Sk)
    nq, nkb = S // bq, Sk // bk
    if causal:
        assert S == Sk and bq == bk

    def body(qT_ref, k_ref, vT_ref, oT_ref, lse_ref):
        i = pl.program_id(1)
        qTb = qT_ref[...]

        def scores(j):
            off = pl.multiple_of(j * bk, bk)
            return _dot(k_ref[pl.ds(off, bk), :], qTb)

        def blk(j, carry, raw, masked):
            m, l, acc = carry
            off = pl.multiple_of(j * bk, bk)
            sT = raw * cmul
            if masked:
                sT = jnp.where(_valid_t(i, j, bq, bk, False), sT, NEG_BIG)
            m_new = jnp.maximum(m, jnp.max(sT, axis=0, keepdims=True))
            p = jnp.exp2(sT - m_new)
            a = jnp.exp2(m - m_new)
            l = a * l + jnp.sum(p, axis=0, keepdims=True)
            acc = a * acc + _dot(vT_ref[:, pl.ds(off, bk)], p.astype(BF16))
            return m_new, l, acc

        def step(j, c):
            nxt = scores(j + 1)
            return blk(j, c[:3], c[3], False) + (nxt,)

        last = i if causal else nkb - 1
        carry = (jnp.full((1, bq), NEG_BIG, F32), jnp.zeros((1, bq), F32), jnp.zeros((dv, bq), F32), scores(0))
        carry = lax.fori_loop(0, last, step, carry)
        m, l, acc = blk(last, carry[:3], carry[3], causal)
        oT_ref[...] = acc / l
        lse_ref[...] = m + jnp.log2(l)

    qcol = lambda d: pl.BlockSpec((None, d, bq), lambda h, i: (h, 0, i))
    return pl.pallas_call(
        body, name=name, grid=(H, nq),
        in_specs=[qcol(DK), pl.BlockSpec((None, Sk, DK), lambda h, i: (h, 0, 0)),
                  pl.BlockSpec((None, dv, Sk), lambda h, i: (h, 0, 0))],
        out_specs=[qcol(dv), qcol(1)],
        out_shape=[jax.ShapeDtypeStruct((H, dv, S), F32), jax.ShapeDtypeStruct((H, 1, S), F32)],
        compiler_params=_cp(("parallel", "arbitrary"), ATTN_VMEM_LIMIT),
    )(qT, k, vT)


def _sm_bwd_t(qT, qn, k, kT, v, oT, lse, doT, do, cmul, gscale, causal, name):
    H, DK, S = qT.shape
    Sk, dv = k.shape[1], v.shape[2]
    bq, bk = _attn_blocks(S, Sk)
    nq, nkb = S // bq, Sk // bk

    def body(qT_ref, qn_ref, k_ref, kT_ref, v_ref, oT_ref, lse_ref, doT_ref, do_ref, dqT_ref, dk_ref, dv_ref):
        i = pl.program_id(1)

        @pl.when(i == 0)
        def _():
            dk_ref[...] = jnp.zeros_like(dk_ref)
            dv_ref[...] = jnp.zeros_like(dv_ref)

        qTb = qT_ref[...]
        qnb = qn_ref[...]
        dob = do_ref[...]
        doTf = doT_ref[...]
        doTb = doTf.astype(BF16)
        delta = jnp.sum(doTf * oT_ref[...], axis=0, keepdims=True)
        lse = lse_ref[...]

        def blk(j, dq, masked):
            off = pl.multiple_of(j * bk, bk)
            sT = _dot(k_ref[pl.ds(off, bk), :], qTb) * cmul
            if masked:
                sT = jnp.where(_valid_t(i, j, bq, bk, False), sT, NEG_BIG)
            p = jnp.exp2(sT - lse)
            dp = _dot(v_ref[pl.ds(off, bk), :], doTb)
            ds = p * (dp - delta)
            dsb = (ds * gscale).astype(BF16) if gscale != 1.0 else ds.astype(BF16)
            dv_ref[pl.ds(off, bk), :] += _dot(p.astype(BF16), dob)
            dk_ref[pl.ds(off, bk), :] += _dot(dsb, qnb)
            return dq + _dot(kT_ref[:, pl.ds(off, bk)], dsb)

        dq = jnp.zeros((DK, bq), F32)
        if causal:
            dq = lax.fori_loop(0, i, lambda j, c: blk(j, c, False), dq)
            dq = blk(i, dq, True)
        else:
            dq = lax.fori_loop(0, nkb, lambda j, c: blk(j, c, False), dq)
        dqT_ref[...] = dq

    qcol = lambda d: pl.BlockSpec((None, d, bq), lambda h, i: (h, 0, i))
    qrow = lambda d: pl.BlockSpec((None, bq, d), lambda h, i: (h, i, 0))
    krow = lambda d: pl.BlockSpec((None, Sk, d), lambda h, i: (h, 0, 0))
    return pl.pallas_call(
        body, name=name, grid=(H, nq),
        in_specs=[qcol(DK), qrow(DK), krow(DK), pl.BlockSpec((None, DK, Sk), lambda h, i: (h, 0, 0)), krow(dv),
                  qcol(dv), qcol(1), qcol(dv), qrow(dv)],
        out_specs=[qcol(DK), krow(DK), krow(dv)],
        out_shape=[jax.ShapeDtypeStruct((H, DK, S), F32), jax.ShapeDtypeStruct((H, Sk, DK), F32),
                   jax.ShapeDtypeStruct((H, Sk, dv), F32)],
        compiler_params=_cp(("parallel", "arbitrary"), ATTN_VMEM_LIMIT),
    )(qT, qn, k, kT, v, oT, lse, doT, do)


def _tri(n, fn):
    r = lax.broadcasted_iota(jnp.int32, (n, n), 0)
    c = lax.broadcasted_iota(jnp.int32, (n, n), 1)
    return jnp.where(fn(r, c), 1.0, 0.0).astype(BF16)


def _key_cumsum(x, tri2, suffix, base):
    bk = x.shape[0]
    c = min(CUMSUM_CHUNK, bk)
    n = bk // c
    hi32 = lax.bitcast_convert_type(lax.bitcast_convert_type(x, jnp.int32) & jnp.int32(-65536), F32)
    hi = hi32.astype(BF16)
    lo = (x - hi32).astype(BF16)
    tot = [jnp.sum(x[a * c:(a + 1) * c], axis=0, keepdims=True) for a in range(n)]
    outs = []
    for a in range(n):
        row = base
        for t in (tot[a + 1:] if suffix else tot[:a]):
            row = row + t
        stacked = jnp.concatenate([hi[a * c:(a + 1) * c], lo[a * c:(a + 1) * c]], axis=0)
        outs.append(_dot(tri2, stacked) + row)
    total = tot[0]
    for t in tot[1:]:
        total = total + t
    return (outs[0] if n == 1 else jnp.concatenate(outs, axis=0)), total


def _tri2(n, fn):
    t = _tri(n, fn)
    return jnp.concatenate([t, t], axis=1)


def _sb_logs(z):
    neg_abs = lax.bitcast_convert_type(lax.bitcast_convert_type(z, jnp.int32) | jnp.int32(-2 ** 31), F32)
    ls = jnp.minimum(z, 0.0) - jnp.log(1.0 + jnp.exp(neg_abs))
    return ls, ls - z


PAIR = LANE // HEAD_DIM


def _head_lanes(shape, w, axis):
    idx = lax.broadcasted_iota(jnp.int32, shape, axis)
    return (idx >= HEAD_DIM * w) & (idx < HEAD_DIM * (w + 1))


def _bias_rows(w, bq):
    row = lax.broadcasted_iota(jnp.int32, (LANE, bq), 0)
    return jnp.where((row >= 3 * w) & (row < 3 * w + 3), -1.0, 0.0).astype(BF16)


def _merge_pair(parts):
    return jnp.where(_head_lanes(parts[0].shape, 0, 0), parts[0], parts[1]).T


def _smp_fwd(q2, k2, vT2, bias, r, causal, name):
    S, C = q2.shape
    Sk = k2.shape[0]
    bq, bk = _attn_blocks(S, Sk)
    nq, nkb, P = S // bq, Sk // bk, C // LANE
    use_f = bias is not None
    if causal:
        assert S == Sk and bq == bk

    def body(*refs):
        if use_f:
            q_ref, k_ref, vT_ref, b_ref, r_ref, o_ref, lse_ref = refs
        else:
            q_ref, k_ref, vT_ref, o_ref, lse_ref = refs
        i = pl.program_id(1)
        qp = q_ref[...]
        outs = []
        for w in range(PAIR):
            qT = jnp.where(_head_lanes(qp.shape, w, 1), qp, jnp.zeros_like(qp)).T
            if use_f:
                qT = jnp.concatenate([qT, _bias_rows(w, bq)], axis=0)

            def blk(j, carry, masked, qT=qT, w=w):
                m, l, acc = carry
                off = pl.multiple_of(j * bk, bk)
                kb = k_ref[pl.ds(off, bk), :]
                if use_f:
                    kb = jnp.concatenate([kb, b_ref[pl.ds(off, bk), :]], axis=1)
                sT = _dot(kb, qT) * LOG2E
                if masked:
                    sT = jnp.where(_valid_t(i, j, bq, bk, False), sT, NEG_BIG)
                cm = jnp.max(sT, axis=0, keepdims=True)
                if use_f:
                    cm = cm + r_ref[w]
                m_new = jnp.maximum(m, cm)
                shift = (m_new - r_ref[w]) if use_f else m_new
                p = jnp.exp2(sT - shift)
                a = jnp.exp2(m - m_new)
                l = a * l + jnp.sum(p, axis=0, keepdims=True)
                acc = a * acc + _dot(vT_ref[:, pl.ds(off, bk)], p.astype(BF16))
                return m_new, l, acc

            carry = (jnp.full((1, bq), NEG_BIG, F32), jnp.zeros((1, bq), F32), jnp.zeros((LANE, bq), F32))
            if causal:
                carry = lax.fori_loop(0, i, lambda j, c: blk(j, c, False), carry)
                carry = blk(i, carry, True)
            else:
                carry = lax.fori_loop(0, nkb, lambda j, c: blk(j, c, False), carry)
            m, l, acc = carry
            outs.append(acc / l)
            lse_ref[w] = m + jnp.log2(l)
        o_ref[...] = _merge_pair(outs)

    qblk = pl.BlockSpec((bq, LANE), lambda p, i: (i, p))
    kres = pl.BlockSpec((Sk, LANE), lambda p, i: (0, p))
    stat = pl.BlockSpec((PAIR, 1, bq), lambda p, i: (p, 0, i))
    in_specs = [qblk, kres, pl.BlockSpec((LANE, Sk), lambda p, i: (p, 0))]
    args = [q2, k2, vT2]
    if use_f:
        in_specs += [kres, stat]
        args += [bias, r]
    return pl.pallas_call(
        body, name=name, grid=(P, nq), in_specs=in_specs, out_specs=[qblk, stat],
        out_shape=[jax.ShapeDtypeStruct((S, C), F32), jax.ShapeDtypeStruct((PAIR * P, 1, S), F32)],
        compiler_params=_cp(("parallel", "arbitrary"), ATTN_VMEM_LIMIT),
    )(*args)


def _smp_bwd(q2, k2, kT2, v2, o2, lse, do2, bias, r, scale, causal, name):
    S, C = q2.shape
    Sk = k2.shape[0]
    bq, bk = _attn_blocks(S, Sk)
    nq, nkb, P = S // bq, Sk // bk, C // LANE
    use_f = bias is not None

    def body(*refs):
        if use_f:
            (q_ref, k_ref, kT_ref, v_ref, o_ref, lse_ref, do_ref, b_ref, r_ref,
             dq_ref, dk_ref, dv_ref, dr_ref, dkey_ref, db_ref) = refs
        else:
            q_ref, k_ref, kT_ref, v_ref, o_ref, lse_ref, do_ref, dq_ref, dk_ref, dv_ref = refs
        i = pl.program_id(1)

        @pl.when(i == 0)
        def _():
            dk_ref[...] = jnp.zeros_like(dk_ref)
            dv_ref[...] = jnp.zeros_like(dv_ref)
            if use_f:
                db_ref[...] = jnp.zeros_like(db_ref)

        qp = q_ref[...]
        dof = do_ref[...]
        prod = dof * o_ref[...]
        dqs = []
        for w in range(PAIR):
            mine = _head_lanes(qp.shape, w, 1)
            qz = jnp.where(mine, qp, jnp.zeros_like(qp))
            qT = qz.T
            if use_f:
                qT = jnp.concatenate([qT, _bias_rows(w, bq)], axis=0)
            doz = jnp.where(mine, dof, 0.0).astype(BF16)
            doT = doz.T
            delta = jnp.sum(jnp.where(mine, prod, 0.0).T, axis=0, keepdims=True)
            shift = (lse_ref[w] - r_ref[w]) if use_f else lse_ref[w]

            def blk(j, carry, masked, qz=qz, qT=qT, doz=doz, doT=doT, delta=delta, shift=shift, w=w):
                dq, dr = carry
                off = pl.multiple_of(j * bk, bk)
                kb = k_ref[pl.ds(off, bk), :]
                if use_f:
                    kb = jnp.concatenate([kb, b_ref[pl.ds(off, bk), :]], axis=1)
                sT = _dot(kb, qT) * LOG2E
                if masked:
                    sT = jnp.where(_valid_t(i, j, bq, bk, False), sT, NEG_BIG)
                p = jnp.exp2(sT - shift)
                dp = _dot(v_ref[pl.ds(off, bk), :], doT)
                ds = p * (dp - delta)
                dsb = ds.astype(BF16)
                dv_ref[pl.ds(off, bk), :] += _dot(p.astype(BF16), doz)
                dk_ref[pl.ds(off, bk), :] += _dot(dsb, qz)
                if use_f:
                    dr = dr + jnp.sum(ds, axis=0, keepdims=True)
                    lane = lax.broadcasted_iota(jnp.int32, (bk, LANE), 1)
                    db_ref[pl.ds(off, bk), :] += jnp.where(lane == w, jnp.sum(ds, axis=1, keepdims=True), 0.0)
                return dq + _dot(kT_ref[:, pl.ds(off, bk)], dsb), dr

            carry = (jnp.zeros((LANE, bq), F32), jnp.zeros((1, bq), F32))
            if causal:
                carry = lax.fori_loop(0, i, lambda j, c: blk(j, c, False), carry)
                carry = blk(i, carry, True)
            else:
                carry = lax.fori_loop(0, nkb, lambda j, c: blk(j, c, False), carry)
            dqs.append(carry[0])
            if use_f:
                dr_ref[w] = carry[1]
        dq_ref[...] = _merge_pair(dqs) * scale

        if use_f:
            @pl.when(i == nq - 1)
            def _():
                def chunk(cidx, carry):
                    off = pl.multiple_of(cidx * LANE, LANE)
                    t = db_ref[pl.ds(off, LANE), :].T
                    for w in range(PAIR):
                        dkey_ref[w, :, pl.ds(off, LANE)] = t[w:w + 1, :]
                    return carry

                lax.fori_loop(0, Sk // LANE, chunk, 0)

    qblk = pl.BlockSpec((bq, LANE), lambda p, i: (i, p))
    kres = pl.BlockSpec((Sk, LANE), lambda p, i: (0, p))
    stat = pl.BlockSpec((PAIR, 1, bq), lambda p, i: (p, 0, i))
    in_specs = [qblk, kres, pl.BlockSpec((LANE, Sk), lambda p, i: (p, 0)), kres, qblk, stat, qblk]
    args = [q2, k2, kT2, v2, o2, lse, do2]
    out_specs = [qblk, kres, kres]
    out_shape = [jax.ShapeDtypeStruct((S, C), F32), jax.ShapeDtypeStruct((Sk, C), F32),
                 jax.ShapeDtypeStruct((Sk, C), F32)]
    scratch = []
    if use_f:
        in_specs += [kres, stat]
        args += [bias, r]
        out_specs += [stat, pl.BlockSpec((PAIR, 1, Sk), lambda p, i: (p, 0, 0))]
        out_shape += [jax.ShapeDtypeStruct((PAIR * P, 1, S), F32), jax.ShapeDtypeStruct((PAIR * P, 1, Sk), F32)]
        scratch = [pltpu.VMEM((Sk, LANE), F32)]
    return pl.pallas_call(
        body, name=name, grid=(P, nq), in_specs=in_specs, out_specs=out_specs, out_shape=out_shape,
        scratch_shapes=scratch, compiler_params=_cp(("parallel", "arbitrary"), ATTN_VMEM_LIMIT),
    )(*args)


def _sbp_fwd(q2, k2, vT2, name):
    S, C = q2.shape
    bq, bk = _attn_blocks(S, S)
    assert bq == bk
    nq, P = S // bq, C // LANE
    c = min(CUMSUM_CHUNK, bk)

    def body(q_ref, k_ref, vT_ref, o_ref, lt_ref):
        i = pl.program_id(1)
        qp = q_ref[...]
        after = _tri2(c, lambda s, j: j > s)
        outs = []
        for w in range(PAIR):
            qT = jnp.where(_head_lanes(qp.shape, w, 1), qp, jnp.zeros_like(qp)).T

            def blk(jj, carry, masked, qT=qT):
                rsum, acc = carry
                j = i - jj
                off = pl.multiple_of(j * bk, bk)
                z = _dot(k_ref[pl.ds(off, bk), :], qT)
                ls, lk = _sb_logs(z)
                if masked:
                    valid = _valid_t(i, j, bq, bk, True)
                    lk = jnp.where(valid, lk, 0.0)
                tail, tot = _key_cumsum(lk, after, True, rsum)
                wgt = jnp.exp(ls + tail)
                if masked:
                    wgt = jnp.where(valid, wgt, 0.0)
                return rsum + tot, acc + _dot(vT_ref[:, pl.ds(off, bk)], wgt.astype(BF16))

            carry = (jnp.zeros((1, bq), F32), jnp.zeros((LANE, bq), F32))
            carry = blk(0, carry, True)
            carry = lax.fori_loop(1, i + 1, lambda jj, cr: blk(jj, cr, False), carry)
            lt_ref[w] = carry[0]
            outs.append(carry[1])
        o_ref[...] = _merge_pair(outs)

    qblk = pl.BlockSpec((bq, LANE), lambda p, i: (i, p))
    stat = pl.BlockSpec((PAIR, 1, bq), lambda p, i: (p, 0, i))
    return pl.pallas_call(
        body, name=name, grid=(P, nq),
        in_specs=[qblk, pl.BlockSpec((S, LANE), lambda p, i: (0, p)), pl.BlockSpec((LANE, S), lambda p, i: (p, 0))],
        out_specs=[qblk, stat],
        out_shape=[jax.ShapeDtypeStruct((S, C), F32), jax.ShapeDtypeStruct((PAIR * P, 1, S), F32)],
        compiler_params=_cp(("parallel", "arbitrary"), ATTN_VMEM_LIMIT),
    )(q2, k2, vT2)


def _sbp_bwd(q2, k2, kT2, v2, lt, do2, scale, name):
    S, C = q2.shape
    bq, bk = _attn_blocks(S, S)
    nq, P = S // bq, C // LANE
    c = min(CUMSUM_CHUNK, bk)

    def body(q_ref, k_ref, kT_ref, v_ref, lt_ref, do_ref, dq_ref, dk_ref, dv_ref):
        i = pl.program_id(1)

        @pl.when(i == 0)
        def _():
            dk_ref[...] = jnp.zeros_like(dk_ref)
            dv_ref[...] = jnp.zeros_like(dv_ref)

        qp = q_ref[...]
        dof = do_ref[...]
        upto = _tri2(c, lambda s, j: j <= s)
        before = _tri2(c, lambda s, j: j < s)
        dqs = []
        for w in range(PAIR):
            mine = _head_lanes(qp.shape, w, 1)
            qz = jnp.where(mine, qp, jnp.zeros_like(qp))
            qT = qz.T
            doz = jnp.where(mine, dof, 0.0).astype(BF16)
            doT = doz.T
            ltot = lt_ref[w]

            def blk(j, carry, masked, qz=qz, qT=qT, doz=doz, doT=doT, ltot=ltot):
                dq, rp, gp = carry
                off = pl.multiple_of(j * bk, bk)
                z = _dot(k_ref[pl.ds(off, bk), :], qT)
                ls, lk = _sb_logs(z)
                if masked:
                    valid = _valid_t(i, j, bq, bk, True)
                    lk = jnp.where(valid, lk, 0.0)
                pin, ltb = _key_cumsum(lk, upto, False, rp - ltot)
                wgt = jnp.exp(ls - pin)
                if masked:
                    wgt = jnp.where(valid, wgt, 0.0)
                g = _dot(v_ref[pl.ds(off, bk), :], doT) * wgt
                cin, gtb = _key_cumsum(g, before, False, gp)
                sig = jnp.exp(ls)
                dz = g * (1.0 - sig) - cin * sig
                if masked:
                    dz = jnp.where(valid, dz, 0.0)
                dzb = dz.astype(BF16)
                dv_ref[pl.ds(off, bk), :] += _dot(wgt.astype(BF16), doz)
                dk_ref[pl.ds(off, bk), :] += _dot(dzb, qz)
                return dq + _dot(kT_ref[:, pl.ds(off, bk)], dzb), rp + ltb, gp + gtb

            carry = (jnp.zeros((LANE, bq), F32), jnp.zeros((1, bq), F32), jnp.zeros((1, bq), F32))
            carry = lax.fori_loop(0, i, lambda j, cr: blk(j, cr, False), carry)
            carry = blk(i, carry, True)
            dqs.append(carry[0])
        dq_ref[...] = _merge_pair(dqs) * scale

    qblk = pl.BlockSpec((bq, LANE), lambda p, i: (i, p))
    kres = pl.BlockSpec((S, LANE), lambda p, i: (0, p))
    stat = pl.BlockSpec((PAIR, 1, bq), lambda p, i: (p, 0, i))
    return pl.pallas_call(
        body, name=name, grid=(P, nq),
        in_specs=[qblk, kres, pl.BlockSpec((LANE, S), lambda p, i: (p, 0)), kres, stat, qblk],
        out_specs=[qblk, kres, kres],
        out_shape=[jax.ShapeDtypeStruct((S, C), F32)] * 3,
        compiler_params=_cp(("parallel", "arbitrary"), ATTN_VMEM_LIMIT),
    )(q2, k2, kT2, v2, lt, do2)


def _bias_cols(f_cum):
    H, Sk = f_cum.shape
    terms = jnp.stack(_split3(f_cum), axis=-1)
    packed = terms.reshape(H // PAIR, PAIR, Sk, 3).transpose(2, 0, 1, 3).reshape(Sk, H // PAIR, PAIR * 3)
    return jnp.pad(packed, ((0, 0), (0, 0), (0, LANE - PAIR * 3))).reshape(Sk, -1)


def _make_packed_softmax(name, scale, causal, use_f):
    assert _pow2(scale)

    def run_fwd(q, k, v, f_cum):
        q16, k16, v16 = (q * scale).astype(BF16), k.astype(BF16), v.astype(BF16)
        bias = _bias_cols(f_cum) if use_f else None
        r = (f_cum * LOG2E)[:, None, :] if use_f else None
        o, lse = _smp_fwd(q16, k16, v16.T, bias, r, causal, name + "_fwd")
        return o, (q16, k16, v16, o, lse, bias, r)

    def run_bwd(saved, do):
        q16, k16, v16, o, lse, bias, r = saved
        outs = _smp_bwd(q16, k16, k16.T, v16, o, lse, do, bias, r, scale, causal, name + "_bwd")
        if use_f:
            dq, dk, dv, dr, dkey = outs
            return dq, dk, dv, dr[:, 0, :] - dkey[:, 0, :]
        return tuple(outs)

    if use_f:
        @jax.custom_vjp
        def attn(q, k, v, f_cum):
            return run_fwd(q, k, v, f_cum)[0]

        attn.defvjp(run_fwd, run_bwd)
    else:
        @jax.custom_vjp
        def attn(q, k, v):
            return run_fwd(q, k, v, None)[0]

        attn.defvjp(lambda q, k, v: run_fwd(q, k, v, None), run_bwd)
    return attn


def _make_packed_sb(name, scale):
    assert _pow2(scale)

    def run_fwd(q, k, v):
        q16, k16, v16 = (q * scale).astype(BF16), k.astype(BF16), v.astype(BF16)
        o, lt = _sbp_fwd(q16, k16, v16.T, name + "_fwd")
        return o, (q16, k16, v16, lt)

    def run_bwd(saved, do):
        q16, k16, v16, lt = saved
        return tuple(_sbp_bwd(q16, k16, k16.T, v16, lt, do, scale, name + "_bwd"))

    @jax.custom_vjp
    def attn(q, k, v):
        return run_fwd(q, k, v)[0]

    attn.defvjp(run_fwd, run_bwd)
    return attn


def _round_bf16(x):
    return lax.reduce_precision(x, exponent_bits=8, mantissa_bits=7)


def _split3(x):
    hi = _round_bf16(x)
    mid = _round_bf16(x - hi)
    lo = _round_bf16(x - hi - mid)
    return hi.astype(BF16), mid.astype(BF16), lo.astype(BF16)


def _pow2(x):
    m, _ = math.frexp(x)
    return m == 0.5


def _pad_last(x, n):
    return jnp.pad(x, [(0, 0)] * (x.ndim - 1) + [(0, n - x.shape[-1])])


def _layouts(q, k, scale):
    qh = _pad_last(jnp.transpose(q * scale if _pow2(scale) else q, (1, 0, 2)).astype(BF16), LANE)
    kh = _pad_last(jnp.transpose(k, (1, 0, 2)).astype(BF16), LANE)
    return qh, jnp.transpose(qh, (0, 2, 1)), kh, jnp.transpose(kh, (0, 2, 1))


def _make_softmax_attn(name, scale, causal, d):
    pre = _pow2(scale)
    cmul = LOG2E if pre else scale * LOG2E
    gscale = 1.0 if pre else scale

    def run_fwd(q, k, v):
        qn, qT, kn, kT = _layouts(q, k, scale)
        vn = jnp.transpose(v, (1, 0, 2)).astype(BF16)
        oT, lse = _sm_fwd_t(qT, kn, jnp.transpose(vn, (0, 2, 1)), cmul, causal, name + "_fwd")
        return jnp.transpose(oT, (2, 0, 1)), (qn, qT, kn, kT, vn, oT, lse)

    def run_bwd(saved, dout):
        qn, qT, kn, kT, vn, oT, lse = saved
        doT = jnp.transpose(dout, (1, 2, 0))
        do = jnp.transpose(dout, (1, 0, 2)).astype(BF16)
        dqT, dk, dv = _sm_bwd_t(qT, qn, kn, kT, vn, oT, lse, doT, do, cmul, gscale, causal, name + "_bwd")
        dq = jnp.transpose(dqT[:, :d, :], (2, 0, 1))
        if pre:
            dq = dq * scale
        return dq, jnp.transpose(dk[:, :, :d], (1, 0, 2)), jnp.transpose(dv, (1, 0, 2))

    @jax.custom_vjp
    def attn(q, k, v):
        return run_fwd(q, k, v)[0]

    attn.defvjp(run_fwd, run_bwd)
    return attn


def _rope(x, positions):
    half = x.shape[-1] // 2
    inv_freq = ROPE_THETA ** (-jnp.arange(half, dtype=F32) / half)
    ang = positions.astype(F32)[:, None] * inv_freq[None, :]
    ang = ang.reshape((ang.shape[0],) + (1,) * (x.ndim - 2) + (half,))
    cos, sin = jnp.cos(ang), jnp.sin(ang)
    x1, x2 = x[..., :half], x[..., half:]
    return jnp.concatenate([x1 * cos - x2 * sin, x1 * sin + x2 * cos], axis=-1)


def _permute_w_in(w):
    parts = [w[:, _ORIG_OFF[idx]:_ORIG_OFF[idx] + SPLIT_SIZES[idx]] for _, idx in _PERM]
    pad = jnp.zeros((w.shape[0], PROJ_COLS - IN_COLS), w.dtype)
    return jnp.concatenate(parts + [pad], axis=1)


def _make_proj(name):
    def split(proj):
        out, off = [], 0
        for _, idx in _PERM:
            out.append(proj[:, off:off + SPLIT_SIZES[idx]])
            off += SPLIT_SIZES[idx]
        return tuple(out)

    def run_fwd(a, w):
        a16, w16 = a.astype(BF16), w.astype(BF16)
        return split(_matmul(a16, w16, "nn", name + "_fwd")), (a16, w16)

    def run_bwd(res, cts):
        a16, w16 = res
        pad = jnp.zeros((a16.shape[0], PROJ_COLS - IN_COLS), BF16)
        dy16 = jnp.concatenate([c.astype(BF16) for c in cts] + [pad], axis=1)
        return _matmul(dy16, w16, "nt", name + "_dx"), _matmul(a16.T, dy16, "nn", name + "_dw")

    @jax.custom_vjp
    def proj(a, w):
        return run_fwd(a, w)[0]

    proj.defvjp(run_fwd, run_bwd)
    return lambda a, w: {n: part for (n, _), part in zip(_PERM, proj(a, w))}


def _trunk_loss(wts, x2d, mem2d, target2d):
    s = x2d.shape[0]
    positions = jnp.arange(s)
    head_scale = HEAD_DIM ** -0.5
    mla_scale = (MLA_NOPE + MLA_ROPE) ** -0.5

    h = _make_ln("ln_in", False)(x2d, wts["ln_in_g"], wts["ln_in_b"])
    mem_n = _make_ln("ln_mem", False)(mem2d, wts["mem_ln_g"], wts["mem_ln_b"])

    for l in range(DEPTH):
        tag = f"l{l}_"
        p = _make_proj(tag + "proj")(h, _permute_w_in(wts["w_in"][l]))

        log_f = jax.nn.log_sigmoid(p["f_logit"] + wts["b_forget"][l])
        f_cum = jnp.cumsum(log_f, axis=0).T
        out_fox = _make_packed_softmax(tag + "fox", head_scale, True, True)(p["fq"], p["fk"], p["fv"], f_cum)

        out_sb = _make_packed_sb(tag + "sb", head_scale)(p["sq"], p["sk"], p["sv"])

        cqn = _make_rms(tag + "rms_q")(p["c_q"], wts["mla_q_norm_g"][l])
        q_mla = _make_mm(tag + "q_up")(cqn, wts["w_mla_q_up"][l]).reshape(s, N_HEADS, MLA_NOPE + MLA_ROPE)
        ckvn = _make_rms(tag + "rms_kv")(p["c_kv"], wts["mla_kv_norm_g"][l])
        kv_mla = _make_mm(tag + "kv_up")(ckvn, wts["w_mla_kv_up"][l]).reshape(s, N_HEADS, MLA_NOPE + MLA_V)
        q_full = jnp.concatenate([q_mla[..., :MLA_NOPE], _rope(q_mla[..., MLA_NOPE:], positions)], axis=-1)
        k_rope = jnp.broadcast_to(_rope(p["k_rot"], positions)[:, None, :], (s, N_HEADS, MLA_ROPE))
        k_full = jnp.concatenate([kv_mla[..., :MLA_NOPE], k_rope], axis=-1)
        out_mla = _make_softmax_attn(tag + "mla", mla_scale, True, MLA_NOPE + MLA_ROPE)(
            q_full, k_full, kv_mla[..., MLA_NOPE:]).reshape(s, GROUP_W)

        mkv = _make_mm(tag + "mem_kv")(mem_n, wts["w_mem_kv"][l])
        out_mem = _make_packed_softmax(tag + "mem", head_scale, False, False)(
            p["mq"], mkv[:, :GROUP_W], mkv[:, GROUP_W:])

        mixed = jnp.concatenate([out_fox, out_sb, out_mla, out_mem], axis=-1)
        gated = _make_gate(tag + "gate")(mixed, p["gate"])
        y = _make_mm(tag + "out")(gated, wts["w_out"][l])
        h = _make_ln(tag + "ln", True)(y, h, wts["ln_g"][l], wts["ln_b"][l])

    return _loss_op(h, target2d)


def _mesh_pos():
    x, y, c = (lax.axis_index(a) for a in MESH_AXES)
    return x, y, c, 4 * x + 2 * y + c


def _peer(x, y, c, mask):
    return (x ^ ((mask >> 2) & 1), y ^ ((mask >> 1) & 1), c ^ (mask & 1))


_ANY = pl.BlockSpec(memory_space=pl.ANY)


def _all_gather(row_shards, stack_shards):
    n_row, n_all = len(row_shards), len(row_shards) + len(stack_shards)
    shards = list(row_shards) + list(stack_shards)

    def body(*refs):
        ins, outs = refs[:n_all], refs[n_all:2 * n_all]
        send_sems, recv_sems, local_sems = refs[2 * n_all:]
        x, y, c, me = _mesh_pos()

        def window(t, slot):
            if t < n_row:
                rows = shards[t].shape[1]
                return outs[t].at[:, pl.ds(slot * rows, rows), :]
            return outs[t].at[slot]

        local = [pltpu.make_async_copy(ins[t], window(t, me), local_sems.at[t]) for t in range(n_all)]
        for cp in local:
            cp.start()
        sends = []
        for mask in range(1, N_DEV):
            for t in range(n_all):
                cp = pltpu.make_async_remote_copy(
                    src_ref=ins[t], dst_ref=window(t, me), send_sem=send_sems.at[t, mask - 1],
                    recv_sem=recv_sems.at[t, mask - 1], device_id=_peer(x, y, c, mask),
                    device_id_type=pl.DeviceIdType.MESH)
                cp.start()
                sends.append(cp)
        for mask in range(1, N_DEV):
            for t in range(n_all):
                pltpu.make_async_remote_copy(
                    src_ref=ins[t], dst_ref=window(t, me ^ mask), send_sem=send_sems.at[t, mask - 1],
                    recv_sem=recv_sems.at[t, mask - 1], device_id=_peer(x, y, c, mask),
                    device_id_type=pl.DeviceIdType.MESH).wait_recv()
        for cp in sends:
            cp.wait_send()
        for cp in local:
            cp.wait()

    out_shape = [jax.ShapeDtypeStruct((a.shape[0], N_DEV * a.shape[1], a.shape[2]), a.dtype) for a in row_shards]
    out_shape += [jax.ShapeDtypeStruct((N_DEV,) + a.shape, a.dtype) for a in stack_shards]
    return pl.pallas_call(
        body, name="all_gather_weights", in_specs=[_ANY] * n_all, out_specs=[_ANY] * n_all, out_shape=out_shape,
        scratch_shapes=[pltpu.SemaphoreType.DMA((n_all, N_DEV - 1)), pltpu.SemaphoreType.DMA((n_all, N_DEV - 1)),
                        pltpu.SemaphoreType.DMA((n_all,))],
    )(*shards)


def _reduce_scatter(row_full, stack_full, bcast):
    n_row, n_stack = len(row_full), len(stack_full)
    n_all = n_row + n_stack + len(bcast)
    fulls = list(row_full) + list(stack_full) + list(bcast)

    def body(*refs):
        ins, outs = refs[:n_all], refs[n_all:2 * n_all]
        send_sems, recv_sems, local_sems = refs[2 * n_all:]
        x, y, c, me = _mesh_pos()

        def part(t, slot):
            if t < n_row:
                rows = fulls[t].shape[1] // N_DEV
                return ins[t].at[:, pl.ds(slot * rows, rows), :]
            if t < n_row + n_stack:
                return ins[t].at[slot]
            return ins[t]

        local = [pltpu.make_async_copy(part(t, me), outs[t].at[me], local_sems.at[t]) for t in range(n_all)]
        for cp in local:
            cp.start()
        sends = []
        for mask in range(1, N_DEV):
            for t in range(n_all):
                cp = pltpu.make_async_remote_copy(
                    src_ref=part(t, me ^ mask), dst_ref=outs[t].at[me], send_sem=send_sems.at[t, mask - 1],
                    recv_sem=recv_sems.at[t, mask - 1], device_id=_peer(x, y, c, mask),
                    device_id_type=pl.DeviceIdType.MESH)
                cp.start()
                sends.append(cp)
        for mask in range(1, N_DEV):
            for t in range(n_all):
                pltpu.make_async_remote_copy(
                    src_ref=part(t, me), dst_ref=outs[t].at[me ^ mask], send_sem=send_sems.at[t, mask - 1],
                    recv_sem=recv_sems.at[t, mask - 1], device_id=_peer(x, y, c, mask),
                    device_id_type=pl.DeviceIdType.MESH).wait_recv()
        for cp in sends:
            cp.wait_send()
        for cp in local:
            cp.wait()

    out_shape = [jax.ShapeDtypeStruct((N_DEV, a.shape[0], a.shape[1] // N_DEV, a.shape[2]), a.dtype) for a in row_full]
    out_shape += [jax.ShapeDtypeStruct(a.shape, a.dtype) for a in stack_full]
    out_shape += [jax.ShapeDtypeStruct((N_DEV,) + a.shape, a.dtype) for a in bcast]
    return pl.pallas_call(
        body, name="reduce_scatter_grads", in_specs=[_ANY] * n_all, out_specs=[_ANY] * n_all, out_shape=out_shape,
        scratch_shapes=[pltpu.SemaphoreType.DMA((n_all, N_DEV - 1)), pltpu.SemaphoreType.DMA((n_all, N_DEV - 1)),
                        pltpu.SemaphoreType.DMA((n_all,))],
    )(*fulls)


def _adamw(slots, w, m, v, name):
    shape = w.shape
    cols = shape[-1]
    rows = math.prod(shape[:-1])
    tr = _pick(rows, (64, 32, 16, 8))
    c1 = 1.0 - ADAM_B1 ** ADAM_STEP
    c2 = 1.0 - ADAM_B2 ** ADAM_STEP

    def body(s_ref, w_ref, m_ref, v_ref, g_ref, d_ref, nm_ref, nv_ref):
        g = s_ref[0]
        for k in range(1, N_DEV):
            g = g + s_ref[k]
        nm = ADAM_B1 * m_ref[...] + (1.0 - ADAM_B1) * g
        nv = ADAM_B2 * v_ref[...] + (1.0 - ADAM_B2) * (g * g)
        g_ref[...] = g
        nm_ref[...] = nm
        nv_ref[...] = nv
        d_ref[...] = -ADAM_LR * ((nm / c1) / (jnp.sqrt(nv / c2) + ADAM_EPS) + ADAM_WD * w_ref[...])

    row = pl.BlockSpec((tr, cols), lambda i: (i, 0))
    out = jax.ShapeDtypeStruct((rows, cols), F32)
    outs = pl.pallas_call(
        body, name=name, grid=(rows // tr,),
        in_specs=[pl.BlockSpec((N_DEV, tr, cols), lambda i: (0, i, 0)), row, row, row],
        out_specs=[row] * 4, out_shape=[out] * 4, compiler_params=_cp(("parallel",)),
    )(slots.reshape(N_DEV, rows, cols), w.reshape(rows, cols), m.reshape(rows, cols), v.reshape(rows, cols))
    return [o.reshape(shape) for o in outs]


_SMALL = ("ln_in_g", "ln_in_b", "mem_ln_g", "mem_ln_b", "b_forget", "mla_q_norm_g", "mla_kv_norm_g", "ln_g", "ln_b")
_ORDER = ("ln_in_g", "ln_in_b", "mem_ln_g", "mem_ln_b", "w_in", "b_forget", "mla_q_norm_g", "w_mla_q_up",
          "mla_kv_norm_g", "w_mla_kv_up", "w_mem_kv", "w_out", "ln_g", "ln_b")


def _pack_small(d):
    flat = jnp.concatenate([d[n].reshape(-1) for n in _SMALL])
    n = flat.shape[0]
    padded = ((n + 8 * LANE - 1) // (8 * LANE)) * (8 * LANE)
    return jnp.pad(flat, (0, padded - n)).reshape(-1, LANE)


def _unpack_small(packed, like):
    flat, out, off = packed.reshape(-1), {}, 0
    for n in _SMALL:
        size = math.prod(like[n].shape)
        out[n] = flat[off:off + size].reshape(like[n].shape)
        off += size
    return out


def _unstack_cols(g):
    n, l, r, c = g.shape
    return g.transpose(1, 2, 0, 3).reshape(l, r, n * c)


def _stack_cols(g):
    l, r, nc = g.shape
    return g.reshape(l, r, N_DEV, nc // N_DEV).transpose(2, 0, 1, 3)


def kernel(x, mem, ln_in_g, ln_in_b, mem_ln_g, mem_ln_b, w_in, b_forget, mla_q_norm_g, w_mla_q_up, mla_kv_norm_g, w_mla_kv_up, w_mem_kv, w_out, ln_g, ln_b, loss_target, m_ln_in_g, m_ln_in_b, m_mem_ln_g, m_mem_ln_b, m_w_in, m_b_forget, m_mla_q_norm_g, m_w_mla_q_up, m_mla_kv_norm_g, m_w_mla_kv_up, m_w_mem_kv, m_w_out, m_ln_g, m_ln_b, v_ln_in_g, v_ln_in_b, v_mem_ln_g, v_mem_ln_b, v_w_in, v_b_forget, v_mla_q_norm_g, v_w_mla_q_up, v_mla_kv_norm_g, v_w_mla_kv_up, v_w_mem_kv, v_w_out, v_ln_g, v_ln_b):
    w_shard = dict(ln_in_g=ln_in_g, ln_in_b=ln_in_b, mem_ln_g=mem_ln_g, mem_ln_b=mem_ln_b, w_in=w_in,
                   b_forget=b_forget, mla_q_norm_g=mla_q_norm_g, w_mla_q_up=w_mla_q_up,
                   mla_kv_norm_g=mla_kv_norm_g, w_mla_kv_up=w_mla_kv_up, w_mem_kv=w_mem_kv, w_out=w_out,
                   ln_g=ln_g, ln_b=ln_b)
    m_shard = dict(ln_in_g=m_ln_in_g, ln_in_b=m_ln_in_b, mem_ln_g=m_mem_ln_g, mem_ln_b=m_mem_ln_b, w_in=m_w_in,
                   b_forget=m_b_forget, mla_q_norm_g=m_mla_q_norm_g, w_mla_q_up=m_w_mla_q_up,
                   mla_kv_norm_g=m_mla_kv_norm_g, w_mla_kv_up=m_w_mla_kv_up, w_mem_kv=m_w_mem_kv, w_out=m_w_out,
                   ln_g=m_ln_g, ln_b=m_ln_b)
    v_shard = dict(ln_in_g=v_ln_in_g, ln_in_b=v_ln_in_b, mem_ln_g=v_mem_ln_g, mem_ln_b=v_mem_ln_b, w_in=v_w_in,
                   b_forget=v_b_forget, mla_q_norm_g=v_mla_q_norm_g, w_mla_q_up=v_w_mla_q_up,
                   mla_kv_norm_g=v_mla_kv_norm_g, w_mla_kv_up=v_w_mla_kv_up, w_mem_kv=v_w_mem_kv, w_out=v_w_out,
                   ln_g=v_ln_g, ln_b=v_ln_b)

    to16 = lambda ws: [a.astype(BF16) for a in ws]
    gathered = _all_gather(to16([w_in, w_mem_kv, w_out]), to16([w_mla_q_up, w_mla_kv_up]))
    g_in, g_mem, g_out, g_qup, g_kvup = [a.astype(F32) for a in gathered]
    full = dict(w_shard)
    full.update(w_in=g_in, w_mem_kv=g_mem, w_out=g_out, w_mla_q_up=_unstack_cols(g_qup),
                w_mla_kv_up=_unstack_cols(g_kvup))

    loss_local, (grad_w, grad_x) = jax.value_and_grad(_trunk_loss, argnums=(0, 1))(
        full, x[0], mem[0], loss_target[0])

    s_in, s_mem, s_out, s_qup, s_kvup, s_small = _reduce_scatter(
        [grad_w["w_in"], grad_w["w_mem_kv"], grad_w["w_out"]],
        [_stack_cols(grad_w["w_mla_q_up"]), _stack_cols(grad_w["w_mla_kv_up"])],
        [_pack_small(grad_w)])

    res = {}
    for name, slots in (("w_in", s_in), ("w_mem_kv", s_mem), ("w_out", s_out), ("w_mla_q_up", s_qup),
                        ("w_mla_kv_up", s_kvup)):
        res[name] = _adamw(slots, w_shard[name], m_shard[name], v_shard[name], "adamw_" + name)
    small = _adamw(s_small, _pack_small(w_shard), _pack_small(m_shard), _pack_small(v_shard), "adamw_small")
    small = [_unpack_small(a, w_shard) for a in small]
    for name in _SMALL:
        res[name] = [a[name] for a in small]

    loss = lax.psum(loss_local, MESH_AXES)
    outs = [loss, grad_x[None]]
    for k in range(4):
        outs += [res[name][k] for name in _ORDER]
    return tuple(outs)
```

```python
import functools
import math

import jax
import jax.numpy as jnp
from jax import lax
from jax.experimental import pallas as pl
from jax.experimental.pallas import tpu as pltpu

F32 = jnp.float32
BF16 = jnp.bfloat16

D_MODEL = 1024
DEPTH = 2
GROUP_W = 256
N_HEADS = 4
HEAD_DIM = 64
MLA_Q_RANK = 256
MLA_KV_RANK = 128
MLA_NOPE = 64
MLA_ROPE = 32
MLA_V = 64
ROPE_THETA = 10000.0
LN_EPS = 1e-5
RMS_EPS = 1e-6
DEEPNORM_ALPHA = (2 * DEPTH) ** 0.25
SPLIT_SIZES = (256, 256, 256, 4, 256, 256, 256, 256, 128, 32, 256, 1024)
IN_COLS = sum(SPLIT_SIZES)
_ORIG_OFF = [sum(SPLIT_SIZES[:i]) for i in range(len(SPLIT_SIZES))]
_PERM = (("fq", 0), ("fk", 1), ("fv", 2), ("sq", 4), ("sk", 5), ("sv", 6), ("c_q", 7), ("c_kv", 8),
         ("mq", 10), ("gate", 11), ("k_rot", 9), ("f_logit", 3))
LANE = 128
PROJ_COLS = ((IN_COLS + LANE - 1) // LANE) * LANE

ADAM_LR = 0.001
ADAM_B1 = 0.9
ADAM_B2 = 0.999
ADAM_EPS = 1e-08
ADAM_WD = 0.01
ADAM_STEP = 10

N_DEV = 8
MESH_AXES = ("x", "y", "c")
VMEM_LIMIT = 48 * 1024 * 1024
ATTN_VMEM_LIMIT = 56 * 1024 * 1024
ATTN_BQ = 512
ATTN_BK = 512
CUMSUM_CHUNK = 256
NEG_BIG = -1e30
LOG2E = math.log2(math.e)
MM_TM, MM_TN, MM_TK, MM_TK_NT = 1024, 1664, 1024, 3328

_NT = (((1,), (1,)), ((), ()))
_NN = (((1,), (0,)), ((), ()))


def _cp(sem, vmem=VMEM_LIMIT):
    return pltpu.CompilerParams(dimension_semantics=sem, vmem_limit_bytes=vmem)


def _dot(a, b, dims=_NN):
    return lax.dot_general(a, b, dims, preferred_element_type=F32)


def _pick(n, cands):
    for c in cands:
        if c <= n and n % c == 0:
            return c
    return n


def _tile(n, cap):
    if n <= cap:
        return n
    best = None
    for d in range(LANE, cap + 1, LANE):
        if n % d == 0:
            best = d
    assert best is not None, (n, cap)
    return best


def _matmul(a, b, mode, name):
    if mode == "nn":
        (M, K), (K2, N) = a.shape, b.shape
    else:
        (M, K), (N, K2) = a.shape, b.shape
    assert K == K2 and a.dtype == BF16 and b.dtype == BF16, (a.shape, b.shape, mode)
    tm, tn = _tile(M, MM_TM), _tile(N, MM_TN)
    tk = _tile(K, MM_TK if mode == "nn" else MM_TK_NT)
    nk = K // tk
    dims = _NN if mode == "nn" else _NT

    def body(a_ref, b_ref, o_ref, acc_ref):
        part = _dot(a_ref[...], b_ref[...], dims)
        if nk == 1:
            o_ref[...] = part
        else:
            k = pl.program_id(2)

            @pl.when(k == 0)
            def _():
                acc_ref[...] = part

            @pl.when(k > 0)
            def _():
                acc_ref[...] += part

            @pl.when(k == nk - 1)
            def _():
                o_ref[...] = acc_ref[...]

    a_spec = pl.BlockSpec((tm, tk), lambda j, i, k: (i, k))
    if mode == "nn":
        b_spec = pl.BlockSpec((tk, tn), lambda j, i, k: (k, j))
    else:
        b_spec = pl.BlockSpec((tn, tk), lambda j, i, k: (j, k))
    acc_shape = (tm, tn) if nk > 1 else (8, LANE)
    return pl.pallas_call(
        body, name=name, grid=(N // tn, M // tm, nk),
        in_specs=[a_spec, b_spec],
        out_specs=pl.BlockSpec((tm, tn), lambda j, i, k: (i, j)),
        out_shape=jax.ShapeDtypeStruct((M, N), F32),
        scratch_shapes=[pltpu.VMEM(acc_shape, F32)],
        compiler_params=_cp(("parallel", "parallel", "arbitrary")),
    )(a, b)


def _make_mm(name):
    @jax.custom_vjp
    def mm(a, w):
        return _matmul(a.astype(BF16), w.astype(BF16), "nn", name + "_fwd")

    def fwd(a, w):
        a16, w16 = a.astype(BF16), w.astype(BF16)
        return _matmul(a16, w16, "nn", name + "_fwd"), (a16, w16)

    def bwd(res, dy):
        a16, w16 = res
        dy16 = dy.astype(BF16)
        da = _matmul(dy16, w16, "nt", name + "_dx")
        dw = _matmul(a16.T, dy16, "nn", name + "_dw")
        return da, dw

    mm.defvjp(fwd, bwd)
    return mm


def _row_tile(rows):
    return _pick(rows, (512, 256, 128, 64, 32, 16, 8))


def _ln_stats(u):
    mu = jnp.mean(u, axis=-1, keepdims=True)
    d = u - mu
    var = jnp.mean(d * d, axis=-1, keepdims=True)
    return d, lax.rsqrt(var + LN_EPS)


def _ln_fwd_call(x, res, g, b, name):
    rows, dm = x.shape
    tr = _row_tile(rows)
    has_res = res is not None

    def body(*refs):
        if has_res:
            x_ref, r_ref, g_ref, b_ref, o_ref = refs
            u = DEEPNORM_ALPHA * r_ref[...] + x_ref[...]
        else:
            x_ref, g_ref, b_ref, o_ref = refs
            u = x_ref[...]
        d, rstd = _ln_stats(u)
        o_ref[...] = d * rstd * g_ref[...] + b_ref[...]

    row = pl.BlockSpec((tr, dm), lambda i: (i, 0))
    vec = pl.BlockSpec((1, dm), lambda i: (0, 0))
    args = (x, res) if has_res else (x,)
    return pl.pallas_call(
        body, name=name, grid=(rows // tr,),
        in_specs=[row] * len(args) + [vec, vec], out_specs=row,
        out_shape=jax.ShapeDtypeStruct((rows, dm), F32),
        compiler_params=_cp(("parallel",)),
    )(*args, g.reshape(1, dm), b.reshape(1, dm))


def _ln_bwd_call(dy, x, res, g, name):
    rows, dm = x.shape
    tr = _row_tile(rows)
    has_res = res is not None

    def body(*refs):
        if has_res:
            dy_ref, x_ref, r_ref, g_ref, dx_ref, dr_ref, dg_ref, db_ref = refs
            u = DEEPNORM_ALPHA * r_ref[...] + x_ref[...]
        else:
            dy_ref, x_ref, g_ref, dx_ref, dg_ref, db_ref = refs
            u = x_ref[...]
        i = pl.program_id(0)
        d, rstd = _ln_stats(u)
        xhat = d * rstd
        dyv = dy_ref[...]
        dxh = dyv * g_ref[...]
        m1 = jnp.mean(dxh, axis=-1, keepdims=True)
        m2 = jnp.mean(dxh * xhat, axis=-1, keepdims=True)
        du = rstd * (dxh - m1 - xhat * m2)
        dx_ref[...] = du
        if has_res:
            dr_ref[...] = DEEPNORM_ALPHA * du
        pg = jnp.sum(dyv * xhat, axis=0, keepdims=True)
        pb = jnp.sum(dyv, axis=0, keepdims=True)

        @pl.when(i == 0)
        def _():
            dg_ref[...] = pg
            db_ref[...] = pb

        @pl.when(i > 0)
        def _():
            dg_ref[...] += pg
            db_ref[...] += pb

    row = pl.BlockSpec((tr, dm), lambda i: (i, 0))
    vec = pl.BlockSpec((1, dm), lambda i: (0, 0))
    big = jax.ShapeDtypeStruct((rows, dm), F32)
    small = jax.ShapeDtypeStruct((1, dm), F32)
    args = (dy, x, res) if has_res else (dy, x)
    n_big = 2 if has_res else 1
    outs = pl.pallas_call(
        body, name=name, grid=(rows // tr,),
        in_specs=[row] * len(args) + [vec],
        out_specs=[row] * n_big + [vec, vec],
        out_shape=[big] * n_big + [small, small],
        compiler_params=_cp(("arbitrary",)),
    )(*args, g.reshape(1, dm))
    return outs


def _make_ln(name, has_res):
    if has_res:
        @jax.custom_vjp
        def ln(x, res, g, b):
            return _ln_fwd_call(x, res, g, b, name + "_fwd")

        def fwd(x, res, g, b):
            return ln(x, res, g, b), (x, res, g)

        def bwd(saved, dy):
            x, res, g = saved
            dx, dr, dg, db = _ln_bwd_call(dy, x, res, g, name + "_bwd")
            return dx, dr, dg.reshape(-1), db.reshape(-1)
    else:
        @jax.custom_vjp
        def ln(x, g, b):
            return _ln_fwd_call(x, None, g, b, name + "_fwd")

        def fwd(x, g, b):
            return ln(x, g, b), (x, g)

        def bwd(saved, dy):
            x, g = saved
            dx, dg, db = _ln_bwd_call(dy, x, None, g, name + "_bwd")
            return dx, dg.reshape(-1), db.reshape(-1)

    ln.defvjp(fwd, bwd)
    return ln


def _rms_fwd_call(x, g, name):
    rows, dm = x.shape
    tr = _row_tile(rows)

    def body(x_ref, g_ref, o_ref):
        xv = x_ref[...]
        rstd = lax.rsqrt(jnp.mean(xv * xv, axis=-1, keepdims=True) + RMS_EPS)
        o_ref[...] = xv * rstd * g_ref[...]

    row = pl.BlockSpec((tr, dm), lambda i: (i, 0))
    vec = pl.BlockSpec((1, dm), lambda i: (0, 0))
    return pl.pallas_call(
        body, name=name, grid=(rows // tr,), in_specs=[row, vec], out_specs=row,
        out_shape=jax.ShapeDtypeStruct((rows, dm), F32), compiler_params=_cp(("parallel",)),
    )(x, g.reshape(1, dm))


def _rms_bwd_call(dy, x, g, name):
    rows, dm = x.shape
    tr = _row_tile(rows)

    def body(dy_ref, x_ref, g_ref, dx_ref, dg_ref):
        i = pl.program_id(0)
        xv = x_ref[...]
        dyv = dy_ref[...]
        rstd = lax.rsqrt(jnp.mean(xv * xv, axis=-1, keepdims=True) + RMS_EPS)
        xhat = xv * rstd
        dxh = dyv * g_ref[...]
        m2 = jnp.mean(dxh * xhat, axis=-1, keepdims=True)
        dx_ref[...] = rstd * (dxh - xhat * m2)
        pg = jnp.sum(dyv * xhat, axis=0, keepdims=True)

        @pl.when(i == 0)
        def _():
            dg_ref[...] = pg

        @pl.when(i > 0)
        def _():
            dg_ref[...] += pg

    row = pl.BlockSpec((tr, dm), lambda i: (i, 0))
    vec = pl.BlockSpec((1, dm), lambda i: (0, 0))
    return pl.pallas_call(
        body, name=name, grid=(rows // tr,), in_specs=[row, row, vec], out_specs=[row, vec],
        out_shape=[jax.ShapeDtypeStruct((rows, dm), F32), jax.ShapeDtypeStruct((1, dm), F32)],
        compiler_params=_cp(("arbitrary",)),
    )(dy, x, g.reshape(1, dm))


def _make_rms(name):
    @jax.custom_vjp
    def rms(x, g):
        return _rms_fwd_call(x, g, name + "_fwd")

    def fwd(x, g):
        return rms(x, g), (x, g)

    def bwd(saved, dy):
        x, g = saved
        dx, dg = _rms_bwd_call(dy, x, g, name + "_bwd")
        return dx, dg.reshape(-1)

    rms.defvjp(fwd, bwd)
    return rms


def _sigmoid(x):
    return 1.0 / (1.0 + jnp.exp(-x))


def _gate_fwd_call(mixed, gate, name):
    rows, dm = mixed.shape
    tr = _row_tile(rows)

    def body(m_ref, g_ref, o_ref):
        gv = g_ref[...]
        o_ref[...] = m_ref[...] * (gv * _sigmoid(gv))

    row = pl.BlockSpec((tr, dm), lambda i: (i, 0))
    return pl.pallas_call(
        body, name=name, grid=(rows // tr,), in_specs=[row, row], out_specs=row,
        out_shape=jax.ShapeDtypeStruct((rows, dm), F32), compiler_params=_cp(("parallel",)),
    )(mixed, gate)


def _gate_bwd_call(dy, mixed, gate, name):
    rows, dm = mixed.shape
    tr = _row_tile(rows)

    def body(dy_ref, m_ref, g_ref, dm_ref, dg_ref):
        gv = g_ref[...]
        dyv = dy_ref[...]
        sg = _sigmoid(gv)
        dm_ref[...] = dyv * (gv * sg)
        dg_ref[...] = dyv * m_ref[...] * (sg * (1.0 + gv * (1.0 - sg)))

    row = pl.BlockSpec((tr, dm), lambda i: (i, 0))
    out = jax.ShapeDtypeStruct((rows, dm), F32)
    return pl.pallas_call(
        body, name=name, grid=(rows // tr,), in_specs=[row, row, row], out_specs=[row, row],
        out_shape=[out, out], compiler_params=_cp(("parallel",)),
    )(dy, mixed, gate)


def _make_gate(name):
    @jax.custom_vjp
    def gate_mul(mixed, gate):
        return _gate_fwd_call(mixed, gate, name + "_fwd")

    def fwd(mixed, gate):
        return gate_mul(mixed, gate), (mixed, gate)

    def bwd(saved, dy):
        mixed, gate = saved
        dmix, dgate = _gate_bwd_call(dy, mixed, gate, name + "_bwd")
        return dmix, dgate

    gate_mul.defvjp(fwd, bwd)
    return gate_mul


def _loss_call(y, t, name):
    rows, dm = y.shape
    tr = _row_tile(rows)

    def body(y_ref, t_ref, l_ref, d_ref):
        i = pl.program_id(0)
        e = y_ref[...] - t_ref[...]
        d_ref[...] = e * (1.0 / dm)
        part = 0.5 * jnp.sum(jnp.mean(e * e, axis=-1, keepdims=True), axis=0, keepdims=True)

        @pl.when(i == 0)
        def _():
            l_ref[...] = part

        @pl.when(i > 0)
        def _():
            l_ref[...] += part

    row = pl.BlockSpec((tr, dm), lambda i: (i, 0))
    one = pl.BlockSpec((1, 1), lambda i: (0, 0))
    return pl.pallas_call(
        body, name=name, grid=(rows // tr,), in_specs=[row, row], out_specs=[one, row],
        out_shape=[jax.ShapeDtypeStruct((1, 1), F32), jax.ShapeDtypeStruct((rows, dm), F32)],
        compiler_params=_cp(("arbitrary",)),
    )(y, t)


@jax.custom_vjp
def _loss_op(y, t):
    return _loss_call(y, t, "loss_head")[0][0, 0]


def _loss_fwd(y, t):
    l, d = _loss_call(y, t, "loss_head")
    return l[0, 0], d


def _loss_bwd(d, ct):
    return ct * d, jnp.zeros_like(d)


_loss_op.defvjp(_loss_fwd, _loss_bwd)


def _attn_blocks(S, Sk):
    bq, bk = min(ATTN_BQ, S), min(ATTN_BK, Sk)
    assert S % bq == 0 and Sk % bk == 0
    return bq, bk


def _valid_t(i, j, bq, bk, strict):
    key = j * bk + lax.broadcasted_iota(jnp.int32, (bk, bq), 0)
    qry = i * bq + lax.broadcasted_iota(jnp.int32, (bk, bq), 1)
    return (key < qry) if strict else (key <= qry)


def _sm_fwd_t(qT, k, vT, cmul, causal, name):
    H, DK, S = qT.shape
    Sk, dv = k.shape[1], vT.shape[1]
    bq, bk = _attn_blocks(S, Sk)
    nq, nkb = S // bq, Sk // bk
    if causal:
        assert S == Sk and bq == bk

    def body(qT_ref, k_ref, vT_ref, oT_ref, lse_ref):
        i = pl.program_id(1)
        qTb = qT_ref[...]

        def scores(j):
            off = pl.multiple_of(j * bk, bk)
            return _dot(k_ref[pl.ds(off, bk), :], qTb)

        def blk(j, carry, raw, masked):
            m, l, acc = carry
            off = pl.multiple_of(j * bk, bk)
            sT = raw * cmul
            if masked:
                sT = jnp.where(_valid_t(i, j, bq, bk, False), sT, NEG_BIG)
            m_new = jnp.maximum(m, jnp.max(sT, axis=0, keepdims=True))
            p = jnp.exp2(sT - m_new)
            a = jnp.exp2(m - m_new)
            l = a * l + jnp.sum(p, axis=0, keepdims=True)
            acc = a * acc + _dot(vT_ref[:, pl.ds(off, bk)], p.astype(BF16))
            return m_new, l, acc

        def step(j, c):
            nxt = scores(j + 1)
            return blk(j, c[:3], c[3], False) + (nxt,)

        last = i if causal else nkb - 1
        carry = (jnp.full((1, bq), NEG_BIG, F32), jnp.zeros((1, bq), F32), jnp.zeros((dv, bq), F32), scores(0))
        carry = lax.fori_loop(0, last, step, carry)
        m, l, acc = blk(last, carry[:3], carry[3], causal)
        oT_ref[...] = acc / l
        lse_ref[...] = m + jnp.log2(l)

    qcol = lambda d: pl.BlockSpec((None, d, bq), lambda h, i: (h, 0, i))
    return pl.pallas_call(
        body, name=name, grid=(H, nq),
        in_specs=[qcol(DK), pl.BlockSpec((None, Sk, DK), lambda h, i: (h, 0, 0)),
                  pl.BlockSpec((None, dv, Sk), lambda h, i: (h, 0, 0))],
        out_specs=[qcol(dv), qcol(1)],
        out_shape=[jax.ShapeDtypeStruct((H, dv, S), F32), jax.ShapeDtypeStruct((H, 1, S), F32)],
        compiler_params=_cp(("parallel", "arbitrary"), ATTN_VMEM_LIMIT),
    )(qT, k, vT)


def _sm_bwd_t(qT, qn, k, kT, v, oT, lse, doT, do, cmul, gscale, causal, name):
    H, DK, S = qT.shape
    Sk, dv = k.shape[1], v.shape[2]
    bq, bk = _attn_blocks(S, Sk)
    nq, nkb = S // bq, Sk // bk

    def body(qT_ref, qn_ref, k_ref, kT_ref, v_ref, oT_ref, lse_ref, doT_ref, do_ref, dqT_ref, dk_ref, dv_ref):
        i = pl.program_id(1)

        @pl.when(i == 0)
        def _():
            dk_ref[...] = jnp.zeros_like(dk_ref)
            dv_ref[...] = jnp.zeros_like(dv_ref)

        qTb = qT_ref[...]
        qnb = qn_ref[...]
        dob = do_ref[...]
        doTf = doT_ref[...]
        doTb = doTf.astype(BF16)
        delta = jnp.sum(doTf * oT_ref[...], axis=0, keepdims=True)
        lse = lse_ref[...]

        def blk(j, dq, masked):
            off = pl.multiple_of(j * bk, bk)
            sT = _dot(k_ref[pl.ds(off, bk), :], qTb) * cmul
            if masked:
                sT = jnp.where(_valid_t(i, j, bq, bk, False), sT, NEG_BIG)
            p = jnp.exp2(sT - lse)
            dp = _dot(v_ref[pl.ds(off, bk), :], doTb)
            ds = p * (dp - delta)
            dsb = (ds * gscale).astype(BF16) if gscale != 1.0 else ds.astype(BF16)
            dv_ref[pl.ds(off, bk), :] += _dot(p.astype(BF16), dob)
            dk_ref[pl.ds(off, bk), :] += _dot(dsb, qnb)
            return dq + _dot(kT_ref[:, pl.ds(off, bk)], dsb)

        dq = jnp.zeros((DK, bq), F32)
        if causal:
            dq = lax.fori_loop(0, i, lambda j, c: blk(j, c, False), dq)
            dq = blk(i, dq, True)
        else:
            dq = lax.fori_loop(0, nkb, lambda j, c: blk(j, c, False), dq)
        dqT_ref[...] = dq

    qcol = lambda d: pl.BlockSpec((None, d, bq), lambda h, i: (h, 0, i))
    qrow = lambda d: pl.BlockSpec((None, bq, d), lambda h, i: (h, i, 0))
    krow = lambda d: pl.BlockSpec((None, Sk, d), lambda h, i: (h, 0, 0))
    return pl.pallas_call(
        body, name=name, grid=(H, nq),
        in_specs=[qcol(DK), qrow(DK), krow(DK), pl.BlockSpec((None, DK, Sk), lambda h, i: (h, 0, 0)), krow(dv),
                  qcol(dv), qcol(1), qcol(dv), qrow(dv)],
        out_specs=[qcol(DK), krow(DK), krow(dv)],
        out_shape=[jax.ShapeDtypeStruct((H, DK, S), F32), jax.ShapeDtypeStruct((H, Sk, DK), F32),
                   jax.ShapeDtypeStruct((H, Sk, dv), F32)],
        compiler_params=_cp(("parallel", "arbitrary"), ATTN_VMEM_LIMIT),
    )(qT, qn, k, kT, v, oT, lse, doT, do)


def _tri(n, fn):
    r = lax.broadcasted_iota(jnp.int32, (n, n), 0)
    c = lax.broadcasted_iota(jnp.int32, (n, n), 1)
    return jnp.where(fn(r, c), 1.0, 0.0).astype(BF16)


def _key_cumsum(x, tri2, suffix, base):
    bk = x.shape[0]
    c = min(CUMSUM_CHUNK, bk)
    n = bk // c
    hi32 = lax.bitcast_convert_type(lax.bitcast_convert_type(x, jnp.int32) & jnp.int32(-65536), F32)
    hi = hi32.astype(BF16)
    lo = (x - hi32).astype(BF16)
    tot = [jnp.sum(x[a * c:(a + 1) * c], axis=0, keepdims=True) for a in range(n)]
    outs = []
    for a in range(n):
        row = base
        for t in (tot[a + 1:] if suffix else tot[:a]):
            row = row + t
        stacked = jnp.concatenate([hi[a * c:(a + 1) * c], lo[a * c:(a + 1) * c]], axis=0)
        outs.append(_dot(tri2, stacked) + row)
    total = tot[0]
    for t in tot[1:]:
        total = total + t
    return (outs[0] if n == 1 else jnp.concatenate(outs, axis=0)), total


def _tri2(n, fn):
    t = _tri(n, fn)
    return jnp.concatenate([t, t], axis=1)


def _sb_logs(z):
    neg_abs = lax.bitcast_convert_type(lax.bitcast_convert_type(z, jnp.int32) | jnp.int32(-2 ** 31), F32)
    ls = jnp.minimum(z, 0.0) - jnp.log(1.0 + jnp.exp(neg_abs))
    return ls, ls - z


PAIR = LANE // HEAD_DIM


def _head_lanes(shape, w, axis):
    idx = lax.broadcasted_iota(jnp.int32, shape, axis)
    return (idx >= HEAD_DIM * w) & (idx < HEAD_DIM * (w + 1))


def _bias_rows(w, bq):
    row = lax.broadcasted_iota(jnp.int32, (LANE, bq), 0)
    return jnp.where((row >= 3 * w) & (row < 3 * w + 3), -1.0, 0.0).astype(BF16)


def _merge_pair(parts):
    return jnp.where(_head_lanes(parts[0].shape, 0, 0), parts[0], parts[1]).T


def _smp_fwd(q2, k2, vT2, bias, r, causal, name):
    S, C = q2.shape
    Sk = k2.shape[0]
    bq, bk = _attn_blocks(S, Sk)
    nq, nkb, P = S // bq, Sk // bk, C // LANE
    use_f = bias is not None
    if causal:
        assert S == Sk and bq == bk

    def body(*refs):
        if use_f:
            q_ref, k_ref, vT_ref, b_ref, r_ref, o_ref, lse_ref = refs
        else:
            q_ref, k_ref, vT_ref, o_ref, lse_ref = refs
        i = pl.program_id(1)
        qp = q_ref[...]
        outs = []
        for w in range(PAIR):
            qT = jnp.where(_head_lanes(qp.shape, w, 1), qp, jnp.zeros_like(qp)).T
            if use_f:
                qT = jnp.concatenate([qT, _bias_rows(w, bq)], axis=0)

            def blk(j, carry, masked, qT=qT, w=w):
                m, l, acc = carry
                off = pl.multiple_of(j * bk, bk)
                kb = k_ref[pl.ds(off, bk), :]
                if use_f:
                    kb = jnp.concatenate([kb, b_ref[pl.ds(off, bk), :]], axis=1)
                sT = _dot(kb, qT) * LOG2E
                if masked:
                    sT = jnp.where(_valid_t(i, j, bq, bk, False), sT, NEG_BIG)
                cm = jnp.max(sT, axis=0, keepdims=True)
                if use_f:
                    cm = cm + r_ref[w]
                m_new = jnp.maximum(m, cm)
                shift = (m_new - r_ref[w]) if use_f else m_new
                p = jnp.exp2(sT - shift)
                a = jnp.exp2(m - m_new)
                l = a * l + jnp.sum(p, axis=0, keepdims=True)
                acc = a * acc + _dot(vT_ref[:, pl.ds(off, bk)], p.astype(BF16))
                return m_new, l, acc

            carry = (jnp.full((1, bq), NEG_BIG, F32), jnp.zeros((1, bq), F32), jnp.zeros((LANE, bq), F32))
            if causal:
                carry = lax.fori_loop(0, i, lambda j, c: blk(j, c, False), carry)
                carry = blk(i, carry, True)
            else:
                carry = lax.fori_loop(0, nkb, lambda j, c: blk(j, c, False), carry)
            m, l, acc = carry
            outs.append(acc / l)
            lse_ref[w] = m + jnp.log2(l)
        o_ref[...] = _merge_pair(outs)

    qblk = pl.BlockSpec((bq, LANE), lambda p, i: (i, p))
    kres = pl.BlockSpec((Sk, LANE), lambda p, i: (0, p))
    stat = pl.BlockSpec((PAIR, 1, bq), lambda p, i: (p, 0, i))
    in_specs = [qblk, kres, pl.BlockSpec((LANE, Sk), lambda p, i: (p, 0))]
    args = [q2, k2, vT2]
    if use_f:
        in_specs += [kres, stat]
        args += [bias, r]
    return pl.pallas_call(
        body, name=name, grid=(P, nq), in_specs=in_specs, out_specs=[qblk, stat],
        out_shape=[jax.ShapeDtypeStruct((S, C), F32), jax.ShapeDtypeStruct((PAIR * P, 1, S), F32)],
        compiler_params=_cp(("parallel", "arbitrary"), ATTN_VMEM_LIMIT),
    )(*args)


def _smp_bwd(q2, k2, kT2, v2, o2, lse, do2, bias, r, scale, causal, name):
    S, C = q2.shape
    Sk = k2.shape[0]
    bq, bk = _attn_blocks(S, Sk)
    nq, nkb, P = S // bq, Sk // bk, C // LANE
    use_f = bias is not None

    def body(*refs):
        if use_f:
            (q_ref, k_ref, kT_ref, v_ref, o_ref, lse_ref, do_ref, b_ref, r_ref,
             dq_ref, dk_ref, dv_ref, dr_ref, dkey_ref, db_ref) = refs
        else:
            q_ref, k_ref, kT_ref, v_ref, o_ref, lse_ref, do_ref, dq_ref, dk_ref, dv_ref = refs
        i = pl.program_id(1)

        @pl.when(i == 0)
        def _():
            dk_ref[...] = jnp.zeros_like(dk_ref)
            dv_ref[...] = jnp.zeros_like(dv_ref)
            if use_f:
                db_ref[...] = jnp.zeros_like(db_ref)

        qp = q_ref[...]
        dof = do_ref[...]
        prod = dof * o_ref[...]
        dqs = []
        for w in range(PAIR):
            mine = _head_lanes(qp.shape, w, 1)
            qz = jnp.where(mine, qp, jnp.zeros_like(qp))
            qT = qz.T
            if use_f:
                qT = jnp.concatenate([qT, _bias_rows(w, bq)], axis=0)
            doz = jnp.where(mine, dof, 0.0).astype(BF16)
            doT = doz.T
            delta = jnp.sum(jnp.where(mine, prod, 0.0).T, axis=0, keepdims=True)
            shift = (lse_ref[w] - r_ref[w]) if use_f else lse_ref[w]

            def blk(j, carry, masked, qz=qz, qT=qT, doz=doz, doT=doT, delta=delta, shift=shift, w=w):
                dq, dr = carry
                off = pl.multiple_of(j * bk, bk)
                kb = k_ref[pl.ds(off, bk), :]
                if use_f:
                    kb = jnp.concatenate([kb, b_ref[pl.ds(off, bk), :]], axis=1)
                sT = _dot(kb, qT) * LOG2E
                if masked:
                    sT = jnp.where(_valid_t(i, j, bq, bk, False), sT, NEG_BIG)
                p = jnp.exp2(sT - shift)
                dp = _dot(v_ref[pl.ds(off, bk), :], doT)
                ds = p * (dp - delta)
                dsb = ds.astype(BF16)
                dv_ref[pl.ds(off, bk), :] += _dot(p.astype(BF16), doz)
                dk_ref[pl.ds(off, bk), :] += _dot(dsb, qz)
                if use_f:
                    dr = dr + jnp.sum(ds, axis=0, keepdims=True)
                    lane = lax.broadcasted_iota(jnp.int32, (bk, LANE), 1)
                    db_ref[pl.ds(off, bk), :] += jnp.where(lane == w, jnp.sum(ds, axis=1, keepdims=True), 0.0)
                return dq + _dot(kT_ref[:, pl.ds(off, bk)], dsb), dr

            carry = (jnp.zeros((LANE, bq), F32), jnp.zeros((1, bq), F32))
            if causal:
                carry = lax.fori_loop(0, i, lambda j, c: blk(j, c, False), carry)
                carry = blk(i, carry, True)
            else:
                carry = lax.fori_loop(0, nkb, lambda j, c: blk(j, c, False), carry)
            dqs.append(carry[0])
            if use_f:
                dr_ref[w] = carry[1]
        dq_ref[...] = _merge_pair(dqs) * scale

        if use_f:
            @pl.when(i == nq - 1)
            def _():
                def chunk(cidx, carry):
                    off = pl.multiple_of(cidx * LANE, LANE)
                    t = db_ref[pl.ds(off, LANE), :].T
                    for w in range(PAIR):
                        dkey_ref[w, :, pl.ds(off, LANE)] = t[w:w + 1, :]
                    return carry

                lax.fori_loop(0, Sk // LANE, chunk, 0)

    qblk = pl.BlockSpec((bq, LANE), lambda p, i: (i, p))
    kres = pl.BlockSpec((Sk, LANE), lambda p, i: (0, p))
    stat = pl.BlockSpec((PAIR, 1, bq), lambda p, i: (p, 0, i))
    in_specs = [qblk, kres, pl.BlockSpec((LANE, Sk), lambda p, i: (p, 0)), kres, qblk, stat, qblk]
    args = [q2, k2, kT2, v2, o2, lse, do2]
    out_specs = [qblk, kres, kres]
    out_shape = [jax.ShapeDtypeStruct((S, C), F32), jax.ShapeDtypeStruct((Sk, C), F32),
                 jax.ShapeDtypeStruct((Sk, C), F32)]
    scratch = []
    if use_f:
        in_specs += [kres, stat]
        args += [bias, r]
        out_specs += [stat, pl.BlockSpec((PAIR, 1, Sk), lambda p, i: (p, 0, 0))]
        out_shape += [jax.ShapeDtypeStruct((PAIR * P, 1, S), F32), jax.ShapeDtypeStruct((PAIR * P, 1, Sk), F32)]
        scratch = [pltpu.VMEM((Sk, LANE), F32)]
    return pl.pallas_call(
        body, name=name, grid=(P, nq), in_specs=in_specs, out_specs=out_specs, out_shape=out_shape,
        scratch_shapes=scratch, compiler_params=_cp(("parallel", "arbitrary"), ATTN_VMEM_LIMIT),
    )(*args)


def _sbp_fwd(q2, k2, vT2, name):
    S, C = q2.shape
    bq, bk = _attn_blocks(S, S)
    assert bq == bk
    nq, P = S // bq, C // LANE
    c = min(CUMSUM_CHUNK, bk)

    def body(q_ref, k_ref, vT_ref, o_ref, lt_ref):
        i = pl.program_id(1)
        qp = q_ref[...]
        after = _tri2(c, lambda s, j: j > s)
        outs = []
        for w in range(PAIR):
            qT = jnp.where(_head_lanes(qp.shape, w, 1), qp, jnp.zeros_like(qp)).T

            def blk(jj, carry, masked, qT=qT):
                rsum, acc = carry
                j = i - jj
                off = pl.multiple_of(j * bk, bk)
                z = _dot(k_ref[pl.ds(off, bk), :], qT)
                ls, lk = _sb_logs(z)
                if masked:
                    valid = _valid_t(i, j, bq, bk, True)
                    lk = jnp.where(valid, lk, 0.0)
                tail, tot = _key_cumsum(lk, after, True, rsum)
                wgt = jnp.exp(ls + tail)
                if masked:
                    wgt = jnp.where(valid, wgt, 0.0)
                return rsum + tot, acc + _dot(vT_ref[:, pl.ds(off, bk)], wgt.astype(BF16))

            carry = (jnp.zeros((1, bq), F32), jnp.zeros((LANE, bq), F32))
            carry = blk(0, carry, True)
            carry = lax.fori_loop(1, i + 1, lambda jj, cr: blk(jj, cr, False), carry)
            lt_ref[w] = carry[0]
            outs.append(carry[1])
        o_ref[...] = _merge_pair(outs)

    qblk = pl.BlockSpec((bq, LANE), lambda p, i: (i, p))
    stat = pl.BlockSpec((PAIR, 1, bq), lambda p, i: (p, 0, i))
    return pl.pallas_call(
        body, name=name, grid=(P, nq),
        in_specs=[qblk, pl.BlockSpec((S, LANE), lambda p, i: (0, p)), pl.BlockSpec((LANE, S), lambda p, i: (p, 0))],
        out_specs=[qblk, stat],
        out_shape=[jax.ShapeDtypeStruct((S, C), F32), jax.ShapeDtypeStruct((PAIR * P, 1, S), F32)],
        compiler_params=_cp(("parallel", "arbitrary"), ATTN_VMEM_LIMIT),
    )(q2, k2, vT2)


def _sbp_bwd(q2, k2, kT2, v2, lt, do2, scale, name):
    S, C = q2.shape
    bq, bk = _attn_blocks(S, S)
    nq, P = S // bq, C // LANE
    c = min(CUMSUM_CHUNK, bk)

    def body(q_ref, k_ref, kT_ref, v_ref, lt_ref, do_ref, dq_ref, dk_ref, dv_ref):
        i = pl.program_id(1)

        @pl.when(i == 0)
        def _():
            dk_ref[...] = jnp.zeros_like(dk_ref)
            dv_ref[...] = jnp.zeros_like(dv_ref)

        qp = q_ref[...]
        dof = do_ref[...]
        upto = _tri2(c, lambda s, j: j <= s)
        before = _tri2(c, lambda s, j: j < s)
        dqs = []
        for w in range(PAIR):
            mine = _head_lanes(qp.shape, w, 1)
            qz = jnp.where(mine, qp, jnp.zeros_like(qp))
            qT = qz.T
            doz = jnp.where(mine, dof, 0.0).astype(BF16)
            doT = doz.T
            ltot = lt_ref[w]

            def blk(j, carry, masked, qz=qz, qT=qT, doz=doz, doT=doT, ltot=ltot):
                dq, rp, gp = carry
                off = pl.multiple_of(j * bk, bk)
                z = _dot(k_ref[pl.ds(off, bk), :], qT)
                ls, lk = _sb_logs(z)
                if masked:
                    valid = _valid_t(i, j, bq, bk, True)
                    lk = jnp.where(valid, lk, 0.0)
                pin, ltb = _key_cumsum(lk, upto, False, rp - ltot)
                wgt = jnp.exp(ls - pin)
                if masked:
                    wgt = jnp.where(valid, wgt, 0.0)
                g = _dot(v_ref[pl.ds(off, bk), :], doT) * wgt
                cin, gtb = _key_cumsum(g, before, False, gp)
                sig = jnp.exp(ls)
                dz = g * (1.0 - sig) - cin * sig
                if masked:
                    dz = jnp.where(valid, dz, 0.0)
                dzb = dz.astype(BF16)
                dv_ref[pl.ds(off, bk), :] += _dot(wgt.astype(BF16), doz)
                dk_ref[pl.ds(off, bk), :] += _dot(dzb, qz)
                return dq + _dot(kT_ref[:, pl.ds(off, bk)], dzb), rp + ltb, gp + gtb

            carry = (jnp.zeros((LANE, bq), F32), jnp.zeros((1, bq), F32), jnp.zeros((1, bq), F32))
            carry = lax.fori_loop(0, i, lambda j, cr: blk(j, cr, False), carry)
            carry = blk(i, carry, True)
            dqs.append(carry[0])
        dq_ref[...] = _merge_pair(dqs) * scale

    qblk = pl.BlockSpec((bq, LANE), lambda p, i: (i, p))
    kres = pl.BlockSpec((S, LANE), lambda p, i: (0, p))
    stat = pl.BlockSpec((PAIR, 1, bq), lambda p, i: (p, 0, i))
    return pl.pallas_call(
        body, name=name, grid=(P, nq),
        in_specs=[qblk, kres, pl.BlockSpec((LANE, S), lambda p, i: (p, 0)), kres, stat, qblk],
        out_specs=[qblk, kres, kres],
        out_shape=[jax.ShapeDtypeStruct((S, C), F32)] * 3,
        compiler_params=_cp(("parallel", "arbitrary"), ATTN_VMEM_LIMIT),
    )(q2, k2, kT2, v2, lt, do2)


def _bias_cols(f_cum):
    H, Sk = f_cum.shape
    terms = jnp.stack(_split3(f_cum), axis=-1)
    packed = terms.reshape(H // PAIR, PAIR, Sk, 3).transpose(2, 0, 1, 3).reshape(Sk, H // PAIR, PAIR * 3)
    return jnp.pad(packed, ((0, 0), (0, 0), (0, LANE - PAIR * 3))).reshape(Sk, -1)


def _make_packed_softmax(name, scale, causal, use_f):
    assert _pow2(scale)

    def run_fwd(q, k, v, f_cum):
        q16, k16, v16 = (q * scale).astype(BF16), k.astype(BF16), v.astype(BF16)
        bias = _bias_cols(f_cum) if use_f else None
        r = (f_cum * LOG2E)[:, None, :] if use_f else None
        o, lse = _smp_fwd(q16, k16, v16.T, bias, r, causal, name + "_fwd")
        return o, (q16, k16, v16, o, lse, bias, r)

    def run_bwd(saved, do):
        q16, k16, v16, o, lse, bias, r = saved
        outs = _smp_bwd(q16, k16, k16.T, v16, o, lse, do, bias, r, scale, causal, name + "_bwd")
        if use_f:
            dq, dk, dv, dr, dkey = outs
            return dq, dk, dv, dr[:, 0, :] - dkey[:, 0, :]
        return tuple(outs)

    if use_f:
        @jax.custom_vjp
        def attn(q, k, v, f_cum):
            return run_fwd(q, k, v, f_cum)[0]

        attn.defvjp(run_fwd, run_bwd)
    else:
        @jax.custom_vjp
        def attn(q, k, v):
            return run_fwd(q, k, v, None)[0]

        attn.defvjp(lambda q, k, v: run_fwd(q, k, v, None), run_bwd)
    return attn


def _make_packed_sb(name, scale):
    assert _pow2(scale)

    def run_fwd(q, k, v):
        q16, k16, v16 = (q * scale).astype(BF16), k.astype(BF16), v.astype(BF16)
        o, lt = _sbp_fwd(q16, k16, v16.T, name + "_fwd")
        return o, (q16, k16, v16, lt)

    def run_bwd(saved, do):
        q16, k16, v16, lt = saved
        return tuple(_sbp_bwd(q16, k16, k16.T, v16, lt, do, scale, name + "_bwd"))

    @jax.custom_vjp
    def attn(q, k, v):
        return run_fwd(q, k, v)[0]

    attn.defvjp(run_fwd, run_bwd)
    return attn


def _round_bf16(x):
    return lax.reduce_precision(x, exponent_bits=8, mantissa_bits=7)


def _split3(x):
    hi = _round_bf16(x)
    mid = _round_bf16(x - hi)
    lo = _round_bf16(x - hi - mid)
    return hi.astype(BF16), mid.astype(BF16), lo.astype(BF16)


def _pow2(x):
    m, _ = math.frexp(x)
    return m == 0.5


def _pad_last(x, n):
    return jnp.pad(x, [(0, 0)] * (x.ndim - 1) + [(0, n - x.shape[-1])])


def _layouts(q, k, scale):
    qh = _pad_last(jnp.transpose(q * scale if _pow2(scale) else q, (1, 0, 2)).astype(BF16), LANE)
    kh = _pad_last(jnp.transpose(k, (1, 0, 2)).astype(BF16), LANE)
    return qh, jnp.transpose(qh, (0, 2, 1)), kh, jnp.transpose(kh, (0, 2, 1))


def _make_softmax_attn(name, scale, causal, d):
    pre = _pow2(scale)
    cmul = LOG2E if pre else scale * LOG2E
    gscale = 1.0 if pre else scale

    def run_fwd(q, k, v):
        qn, qT, kn, kT = _layouts(q, k, scale)
        vn = jnp.transpose(v, (1, 0, 2)).astype(BF16)
        oT, lse = _sm_fwd_t(qT, kn, jnp.transpose(vn, (0, 2, 1)), cmul, causal, name + "_fwd")
        return jnp.transpose(oT, (2, 0, 1)), (qn, qT, kn, kT, vn, oT, lse)

    def run_bwd(saved, dout):
        qn, qT, kn, kT, vn, oT, lse = saved
        doT = jnp.transpose(dout, (1, 2, 0))
        do = jnp.transpose(dout, (1, 0, 2)).astype(BF16)
        dqT, dk, dv = _sm_bwd_t(qT, qn, kn, kT, vn, oT, lse, doT, do, cmul, gscale, causal, name + "_bwd")
        dq = jnp.transpose(dqT[:, :d, :], (2, 0, 1))
        if pre:
            dq = dq * scale
        return dq, jnp.transpose(dk[:, :, :d], (1, 0, 2)), jnp.transpose(dv, (1, 0, 2))

    @jax.custom_vjp
    def attn(q, k, v):
        return run_fwd(q, k, v)[0]

    attn.defvjp(run_fwd, run_bwd)
    return attn


def _rope(x, positions):
    half = x.shape[-1] // 2
    inv_freq = ROPE_THETA ** (-jnp.arange(half, dtype=F32) / half)
    ang = positions.astype(F32)[:, None] * inv_freq[None, :]
    ang = ang.reshape((ang.shape[0],) + (1,) * (x.ndim - 2) + (half,))
    cos, sin = jnp.cos(ang), jnp.sin(ang)
    x1, x2 = x[..., :half], x[..., half:]
    return jnp.concatenate([x1 * cos - x2 * sin, x1 * sin + x2 * cos], axis=-1)


def _permute_w_in(w):
    parts = [w[:, _ORIG_OFF[idx]:_ORIG_OFF[idx] + SPLIT_SIZES[idx]] for _, idx in _PERM]
    pad = jnp.zeros((w.shape[0], PROJ_COLS - IN_COLS), w.dtype)
    return jnp.concatenate(parts + [pad], axis=1)


def _make_proj(name):
    def split(proj):
        out, off = [], 0
        for _, idx in _PERM:
            out.append(proj[:, off:off + SPLIT_SIZES[idx]])
            off += SPLIT_SIZES[idx]
        return tuple(out)

    def run_fwd(a, w):
        a16, w16 = a.astype(BF16), w.astype(BF16)
        return split(_matmul(a16, w16, "nn", name + "_fwd")), (a16, w16)

    def run_bwd(res, cts):
        a16, w16 = res
        pad = jnp.zeros((a16.shape[0], PROJ_COLS - IN_COLS), BF16)
        dy16 = jnp.concatenate([c.astype(BF16) for c in cts] + [pad], axis=1)
        return _matmul(dy16, w16, "nt", name + "_dx"), _matmul(a16.T, dy16, "nn", name + "_dw")

    @jax.custom_vjp
    def proj(a, w):
        return run_fwd(a, w)[0]

    proj.defvjp(run_fwd, run_bwd)
    return lambda a, w: {n: part for (n, _), part in zip(_PERM, proj(a, w))}


def _trunk_loss(wts, x2d, mem2d, target2d):
    s = x2d.shape[0]
    positions = jnp.arange(s)
    head_scale = HEAD_DIM ** -0.5
    mla_scale = (MLA_NOPE + MLA_ROPE) ** -0.5

    h = _make_ln("ln_in", False)(x2d, wts["ln_in_g"], wts["ln_in_b"])
    mem_n = _make_ln("ln_mem", False)(mem2d, wts["mem_ln_g"], wts["mem_ln_b"])

    for l in range(DEPTH):
        tag = f"l{l}_"
        p = _make_proj(tag + "proj")(h, _permute_w_in(wts["w_in"][l]))

        log_f = jax.nn.log_sigmoid(p["f_logit"] + wts["b_forget"][l])
        f_cum = jnp.cumsum(log_f, axis=0).T
        out_fox = _make_packed_softmax(tag + "fox", head_scale, True, True)(p["fq"], p["fk"], p["fv"], f_cum)

        out_sb = _make_packed_sb(tag + "sb", head_scale)(p["sq"], p["sk"], p["sv"])

        cqn = _make_rms(tag + "rms_q")(p["c_q"], wts["mla_q_norm_g"][l])
        q_mla = _make_mm(tag + "q_up")(cqn, wts["w_mla_q_up"][l]).reshape(s, N_HEADS, MLA_NOPE + MLA_ROPE)
        ckvn = _make_rms(tag + "rms_kv")(p["c_kv"], wts["mla_kv_norm_g"][l])
        kv_mla = _make_mm(tag + "kv_up")(ckvn, wts["w_mla_kv_up"][l]).reshape(s, N_HEADS, MLA_NOPE + MLA_V)
        q_full = jnp.concatenate([q_mla[..., :MLA_NOPE], _rope(q_mla[..., MLA_NOPE:], positions)], axis=-1)
        k_rope = jnp.broadcast_to(_rope(p["k_rot"], positions)[:, None, :], (s, N_HEADS, MLA_ROPE))
        k_full = jnp.concatenate([kv_mla[..., :MLA_NOPE], k_rope], axis=-1)
        out_mla = _make_softmax_attn(tag + "mla", mla_scale, True, MLA_NOPE + MLA_ROPE)(
            q_full, k_full, kv_mla[..., MLA_NOPE:]).reshape(s, GROUP_W)

        mkv = _make_mm(tag + "mem_kv")(mem_n, wts["w_mem_kv"][l])
        out_mem = _make_packed_softmax(tag + "mem", head_scale, False, False)(
            p["mq"], mkv[:, :GROUP_W], mkv[:, GROUP_W:])

        mixed = jnp.concatenate([out_fox, out_sb, out_mla, out_mem], axis=-1)
        gated = _make_gate(tag + "gate")(mixed, p["gate"])
        y = _make_mm(tag + "out")(gated, wts["w_out"][l])
        h = _make_ln(tag + "ln", True)(y, h, wts["ln_g"][l], wts["ln_b"][l])

    return _loss_op(h, target2d)


def _mesh_pos():
    x, y, c = (lax.axis_index(a) for a in MESH_AXES)
    return x, y, c, 4 * x + 2 * y + c


def _peer(x, y, c, mask):
    return (x ^ ((mask >> 2) & 1), y ^ ((mask >> 1) & 1), c ^ (mask & 1))


_ANY = pl.BlockSpec(memory_space=pl.ANY)


def _all_gather(row_shards, stack_shards):
    n_row, n_all = len(row_shards), len(row_shards) + len(stack_shards)
    shards = list(row_shards) + list(stack_shards)

    def body(*refs):
        ins, outs = refs[:n_all], refs[n_all:2 * n_all]
        send_sems, recv_sems, local_sems = refs[2 * n_all:]
        x, y, c, me = _mesh_pos()

        def window(t, slot):
            if t < n_row:
                rows = shards[t].shape[1]
                return outs[t].at[:, pl.ds(slot * rows, rows), :]
            return outs[t].at[slot]

        local = [pltpu.make_async_copy(ins[t], window(t, me), local_sems.at[t]) for t in range(n_all)]
        for cp in local:
            cp.start()
        sends = []
        for mask in range(1, N_DEV):
            for t in range(n_all):
                cp = pltpu.make_async_remote_copy(
                    src_ref=ins[t], dst_ref=window(t, me), send_sem=send_sems.at[t, mask - 1],
                    recv_sem=recv_sems.at[t, mask - 1], device_id=_peer(x, y, c, mask),
                    device_id_type=pl.DeviceIdType.MESH)
                cp.start()
                sends.append(cp)
        for mask in range(1, N_DEV):
            for t in range(n_all):
                pltpu.make_async_remote_copy(
                    src_ref=ins[t], dst_ref=window(t, me ^ mask), send_sem=send_sems.at[t, mask - 1],
                    recv_sem=recv_sems.at[t, mask - 1], device_id=_peer(x, y, c, mask),
                    device_id_type=pl.DeviceIdType.MESH).wait_recv()
        for cp in sends:
            cp.wait_send()
        for cp in local:
            cp.wait()

    out_shape = [jax.ShapeDtypeStruct((a.shape[0], N_DEV * a.shape[1], a.shape[2]), a.dtype) for a in row_shards]
    out_shape += [jax.ShapeDtypeStruct((N_DEV,) + a.shape, a.dtype) for a in stack_shards]
    return pl.pallas_call(
        body, name="all_gather_weights", in_specs=[_ANY] * n_all, out_specs=[_ANY] * n_all, out_shape=out_shape,
        scratch_shapes=[pltpu.SemaphoreType.DMA((n_all, N_DEV - 1)), pltpu.SemaphoreType.DMA((n_all, N_DEV - 1)),
                        pltpu.SemaphoreType.DMA((n_all,))],
    )(*shards)


def _reduce_scatter(row_full, stack_full, bcast):
    n_row, n_stack = len(row_full), len(stack_full)
    n_all = n_row + n_stack + len(bcast)
    fulls = list(row_full) + list(stack_full) + list(bcast)

    def body(*refs):
        ins, outs = refs[:n_all], refs[n_all:2 * n_all]
        send_sems, recv_sems, local_sems = refs[2 * n_all:]
        x, y, c, me = _mesh_pos()

        def part(t, slot):
            if t < n_row:
                rows = fulls[t].shape[1] // N_DEV
                return ins[t].at[:, pl.ds(slot * rows, rows), :]
            if t < n_row + n_stack:
                return ins[t].at[slot]
            return ins[t]

        local = [pltpu.make_async_copy(part(t, me), outs[t].at[me], local_sems.at[t]) for t in range(n_all)]
        for cp in local:
            cp.start()
        sends = []
        for mask in range(1, N_DEV):
            for t in range(n_all):
                cp = pltpu.make_async_remote_copy(
                    src_ref=part(t, me ^ mask), dst_ref=outs[t].at[me], send_sem=send_sems.at[t, mask - 1],
                    recv_sem=recv_sems.at[t, mask - 1], device_id=_peer(x, y, c, mask),
                    device_id_type=pl.DeviceIdType.MESH)
                cp.start()
                sends.append(cp)
        for mask in range(1, N_DEV):
            for t in range(n_all):
                pltpu.make_async_remote_copy(
                    src_ref=part(t, me), dst_ref=outs[t].at[me ^ mask], send_sem=send_sems.at[t, mask - 1],
                    recv_sem=recv_sems.at[t, mask - 1], device_id=_peer(x, y, c, mask),
                    device_id_type=pl.DeviceIdType.MESH).wait_recv()
        for cp in sends:
            cp.wait_send()
        for cp in local:
            cp.wait()

    out_shape = [jax.ShapeDtypeStruct((N_DEV, a.shape[0], a.shape[1] // N_DEV, a.shape[2]), a.dtype) for a in row_full]
    out_shape += [jax.ShapeDtypeStruct(a.shape, a.dtype) for a in stack_full]
    out_shape += [jax.ShapeDtypeStruct((N_DEV,) + a.shape, a.dtype) for a in bcast]
    return pl.pallas_call(
        body, name="reduce_scatter_grads", in_specs=[_ANY] * n_all, out_specs=[_ANY] * n_all, out_shape=out_shape,
        scratch_shapes=[pltpu.SemaphoreType.DMA((n_all, N_DEV - 1)), pltpu.SemaphoreType.DMA((n_all, N_DEV - 1)),
                        pltpu.SemaphoreType.DMA((n_all,))],
    )(*fulls)


def _adamw(slots, w, m, v, name):
    shape = w.shape
    cols = shape[-1]
    rows = math.prod(shape[:-1])
    tr = _pick(rows, (64, 32, 16, 8))
    c1 = 1.0 - ADAM_B1 ** ADAM_STEP
    c2 = 1.0 - ADAM_B2 ** ADAM_STEP

    def body(s_ref, w_ref, m_ref, v_ref, g_ref, d_ref, nm_ref, nv_ref):
        g = s_ref[0].astype(F32)
        for k in range(1, N_DEV):
            g = g + s_ref[k].astype(F32)
        nm = ADAM_B1 * m_ref[...] + (1.0 - ADAM_B1) * g
        nv = ADAM_B2 * v_ref[...] + (1.0 - ADAM_B2) * (g * g)
        g_ref[...] = g
        nm_ref[...] = nm
        nv_ref[...] = nv
        d_ref[...] = -ADAM_LR * ((nm / c1) / (jnp.sqrt(nv / c2) + ADAM_EPS) + ADAM_WD * w_ref[...])

    row = pl.BlockSpec((tr, cols), lambda i: (i, 0))
    out = jax.ShapeDtypeStruct((rows, cols), F32)
    outs = pl.pallas_call(
        body, name=name, grid=(rows // tr,),
        in_specs=[pl.BlockSpec((N_DEV, tr, cols), lambda i: (0, i, 0)), row, row, row],
        out_specs=[row] * 4, out_shape=[out] * 4, compiler_params=_cp(("parallel",)),
    )(slots.reshape(N_DEV, rows, cols), w.reshape(rows, cols), m.reshape(rows, cols), v.reshape(rows, cols))
    return [o.reshape(shape) for o in outs]


_SMALL = ("ln_in_g", "ln_in_b", "mem_ln_g", "mem_ln_b", "b_forget", "mla_q_norm_g", "mla_kv_norm_g", "ln_g", "ln_b")
_ORDER = ("ln_in_g", "ln_in_b", "mem_ln_g", "mem_ln_b", "w_in", "b_forget", "mla_q_norm_g", "w_mla_q_up",
          "mla_kv_norm_g", "w_mla_kv_up", "w_mem_kv", "w_out", "ln_g", "ln_b")


def _pack_small(d):
    flat = jnp.concatenate([d[n].reshape(-1) for n in _SMALL])
    n = flat.shape[0]
    padded = ((n + 8 * LANE - 1) // (8 * LANE)) * (8 * LANE)
    return jnp.pad(flat, (0, padded - n)).reshape(-1, LANE)


def _unpack_small(packed, like):
    flat, out, off = packed.reshape(-1), {}, 0
    for n in _SMALL:
        size = math.prod(like[n].shape)
        out[n] = flat[off:off + size].reshape(like[n].shape)
        off += size
    return out


def _unstack_cols(g):
    n, l, r, c = g.shape
    return g.transpose(1, 2, 0, 3).reshape(l, r, n * c)


def _stack_cols(g):
    l, r, nc = g.shape
    return g.reshape(l, r, N_DEV, nc // N_DEV).transpose(2, 0, 1, 3)


def kernel(x, mem, ln_in_g, ln_in_b, mem_ln_g, mem_ln_b, w_in, b_forget, mla_q_norm_g, w_mla_q_up, mla_kv_norm_g, w_mla_kv_up, w_mem_kv, w_out, ln_g, ln_b, loss_target, m_ln_in_g, m_ln_in_b, m_mem_ln_g, m_mem_ln_b, m_w_in, m_b_forget, m_mla_q_norm_g, m_w_mla_q_up, m_mla_kv_norm_g, m_w_mla_kv_up, m_w_mem_kv, m_w_out, m_ln_g, m_ln_b, v_ln_in_g, v_ln_in_b, v_mem_ln_g, v_mem_ln_b, v_w_in, v_b_forget, v_mla_q_norm_g, v_w_mla_q_up, v_mla_kv_norm_g, v_w_mla_kv_up, v_w_mem_kv, v_w_out, v_ln_g, v_ln_b):
    w_shard = dict(ln_in_g=ln_in_g, ln_in_b=ln_in_b, mem_ln_g=mem_ln_g, mem_ln_b=mem_ln_b, w_in=w_in,
                   b_forget=b_forget, mla_q_norm_g=mla_q_norm_g, w_mla_q_up=w_mla_q_up,
                   mla_kv_norm_g=mla_kv_norm_g, w_mla_kv_up=w_mla_kv_up, w_mem_kv=w_mem_kv, w_out=w_out,
                   ln_g=ln_g, ln_b=ln_b)
    m_shard = dict(ln_in_g=m_ln_in_g, ln_in_b=m_ln_in_b, mem_ln_g=m_mem_ln_g, mem_ln_b=m_mem_ln_b, w_in=m_w_in,
                   b_forget=m_b_forget, mla_q_norm_g=m_mla_q_norm_g, w_mla_q_up=m_w_mla_q_up,
                   mla_kv_norm_g=m_mla_kv_norm_g, w_mla_kv_up=m_w_mla_kv_up, w_mem_kv=m_w_mem_kv, w_out=m_w_out,
                   ln_g=m_ln_g, ln_b=m_ln_b)
    v_shard = dict(ln_in_g=v_ln_in_g, ln_in_b=v_ln_in_b, mem_ln_g=v_mem_ln_g, mem_ln_b=v_mem_ln_b, w_in=v_w_in,
                   b_forget=v_b_forget, mla_q_norm_g=v_mla_q_norm_g, w_mla_q_up=v_w_mla_q_up,
                   mla_kv_norm_g=v_mla_kv_norm_g, w_mla_kv_up=v_w_mla_kv_up, w_mem_kv=v_w_mem_kv, w_out=v_w_out,
                   ln_g=v_ln_g, ln_b=v_ln_b)

    to16 = lambda ws: [a.astype(BF16) for a in ws]
    gathered = _all_gather(to16([w_in, w_mem_kv, w_out]), to16([w_mla_q_up, w_mla_kv_up]))
    g_in, g_mem, g_out, g_qup, g_kvup = [a.astype(F32) for a in gathered]
    full = dict(w_shard)
    full.update(w_in=g_in, w_mem_kv=g_mem, w_out=g_out, w_mla_q_up=_unstack_cols(g_qup),
                w_mla_kv_up=_unstack_cols(g_kvup))

    loss_local, (grad_w, grad_x) = jax.value_and_grad(_trunk_loss, argnums=(0, 1))(
        full, x[0], mem[0], loss_target[0])

    s_in, s_mem, s_out, s_qup, s_kvup, s_small = _reduce_scatter(
        to16([grad_w["w_in"], grad_w["w_mem_kv"], grad_w["w_out"]]),
        to16([_stack_cols(grad_w["w_mla_q_up"]), _stack_cols(grad_w["w_mla_kv_up"])]),
        [_pack_small(grad_w)])

    res = {}
    for name, slots in (("w_in", s_in), ("w_mem_kv", s_mem), ("w_out", s_out), ("w_mla_q_up", s_qup),
                        ("w_mla_kv_up", s_kvup)):
        res[name] = _adamw(slots, w_shard[name], m_shard[name], v_shard[name], "adamw_" + name)
    small = _adamw(s_small, _pack_small(w_shard), _pack_small(m_shard), _pack_small(v_shard), "adamw_small")
    small = [_unpack_small(a, w_shard) for a in small]
    for name in _SMALL:
        res[name] = [a[name] for a in small]

    loss = lax.psum(loss_local, MESH_AXES)
    outs = [loss, grad_x[None]]
    for k in range(4):
        outs += [res[name][k] for name in _ORDER]
    return tuple(outs)
```

```python
import functools
import math

import jax
import jax.numpy as jnp
from jax import lax
from jax.experimental import pallas as pl
from jax.experimental.pallas import tpu as pltpu

F32 = jnp.float32
BF16 = jnp.bfloat16

D_MODEL = 1024
DEPTH = 2
GROUP_W = 256
N_HEADS = 4
HEAD_DIM = 64
MLA_Q_RANK = 256
MLA_KV_RANK = 128
MLA_NOPE = 64
MLA_ROPE = 32
MLA_V = 64
ROPE_THETA = 10000.0
LN_EPS = 1e-5
RMS_EPS = 1e-6
DEEPNORM_ALPHA = (2 * DEPTH) ** 0.25
SPLIT_SIZES = (256, 256, 256, 4, 256, 256, 256, 256, 128, 32, 256, 1024)
IN_COLS = sum(SPLIT_SIZES)
_ORIG_OFF = [sum(SPLIT_SIZES[:i]) for i in range(len(SPLIT_SIZES))]
_PERM = (("fq", 0), ("fk", 1), ("fv", 2), ("sq", 4), ("sk", 5), ("sv", 6), ("c_q", 7), ("c_kv", 8),
         ("mq", 10), ("gate", 11), ("k_rot", 9), ("f_logit", 3))
LANE = 128
PROJ_COLS = ((IN_COLS + LANE - 1) // LANE) * LANE

ADAM_LR = 0.001
ADAM_B1 = 0.9
ADAM_B2 = 0.999
ADAM_EPS = 1e-08
ADAM_WD = 0.01
ADAM_STEP = 10

N_DEV = 8
MESH_AXES = ("x", "y", "c")
VMEM_LIMIT = 48 * 1024 * 1024
ATTN_VMEM_LIMIT = 56 * 1024 * 1024
ATTN_BQ = 512
ATTN_BK = 512
CUMSUM_CHUNK = 256
NEG_BIG = -1e30
LOG2E = math.log2(math.e)
MM_TM, MM_TN, MM_TK, MM_TK_NT = 1024, 1664, 1024, 3328

_NT = (((1,), (1,)), ((), ()))
_NN = (((1,), (0,)), ((), ()))


def _cp(sem, vmem=VMEM_LIMIT):
    return pltpu.CompilerParams(dimension_semantics=sem, vmem_limit_bytes=vmem)


def _dot(a, b, dims=_NN):
    return lax.dot_general(a, b, dims, preferred_element_type=F32)


def _pick(n, cands):
    for c in cands:
        if c <= n and n % c == 0:
            return c
    return n


def _tile(n, cap):
    if n <= cap:
        return n
    best = None
    for d in range(LANE, cap + 1, LANE):
        if n % d == 0:
            best = d
    assert best is not None, (n, cap)
    return best


def _matmul(a, b, mode, name):
    if mode == "nn":
        (M, K), (K2, N) = a.shape, b.shape
    else:
        (M, K), (N, K2) = a.shape, b.shape
    assert K == K2 and a.dtype == BF16 and b.dtype == BF16, (a.shape, b.shape, mode)
    tm, tn = _tile(M, MM_TM), _tile(N, MM_TN)
    tk = _tile(K, MM_TK if mode == "nn" else MM_TK_NT)
    nk = K // tk
    dims = _NN if mode == "nn" else _NT

    def body(a_ref, b_ref, o_ref, acc_ref):
        part = _dot(a_ref[...], b_ref[...], dims)
        if nk == 1:
            o_ref[...] = part
        else:
            k = pl.program_id(2)

            @pl.when(k == 0)
            def _():
                acc_ref[...] = part

            @pl.when(k > 0)
            def _():
                acc_ref[...] += part

            @pl.when(k == nk - 1)
            def _():
                o_ref[...] = acc_ref[...]

    a_spec = pl.BlockSpec((tm, tk), lambda j, i, k: (i, k))
    if mode == "nn":
        b_spec = pl.BlockSpec((tk, tn), lambda j, i, k: (k, j))
    else:
        b_spec = pl.BlockSpec((tn, tk), lambda j, i, k: (j, k))
    acc_shape = (tm, tn) if nk > 1 else (8, LANE)
    return pl.pallas_call(
        body, name=name, grid=(N // tn, M // tm, nk),
        in_specs=[a_spec, b_spec],
        out_specs=pl.BlockSpec((tm, tn), lambda j, i, k: (i, j)),
        out_shape=jax.ShapeDtypeStruct((M, N), F32),
        scratch_shapes=[pltpu.VMEM(acc_shape, F32)],
        compiler_params=_cp(("parallel", "parallel", "arbitrary")),
    )(a, b)


def _make_mm(name):
    @jax.custom_vjp
    def mm(a, w):
        return _matmul(a.astype(BF16), w.astype(BF16), "nn", name + "_fwd")

    def fwd(a, w):
        a16, w16 = a.astype(BF16), w.astype(BF16)
        return _matmul(a16, w16, "nn", name + "_fwd"), (a16, w16)

    def bwd(res, dy):
        a16, w16 = res
        dy16 = dy.astype(BF16)
        da = _matmul(dy16, w16, "nt", name + "_dx")
        dw = _matmul(a16.T, dy16, "nn", name + "_dw")
        return da, dw

    mm.defvjp(fwd, bwd)
    return mm


def _row_tile(rows):
    return _pick(rows, (512, 256, 128, 64, 32, 16, 8))


def _ln_stats(u):
    mu = jnp.mean(u, axis=-1, keepdims=True)
    d = u - mu
    var = jnp.mean(d * d, axis=-1, keepdims=True)
    return d, lax.rsqrt(var + LN_EPS)


def _ln_fwd_call(x, res, g, b, name):
    rows, dm = x.shape
    tr = _row_tile(rows)
    has_res = res is not None

    def body(*refs):
        if has_res:
            x_ref, r_ref, g_ref, b_ref, o_ref = refs
            u = DEEPNORM_ALPHA * r_ref[...] + x_ref[...]
        else:
            x_ref, g_ref, b_ref, o_ref = refs
            u = x_ref[...]
        d, rstd = _ln_stats(u)
        o_ref[...] = d * rstd * g_ref[...] + b_ref[...]

    row = pl.BlockSpec((tr, dm), lambda i: (i, 0))
    vec = pl.BlockSpec((1, dm), lambda i: (0, 0))
    args = (x, res) if has_res else (x,)
    return pl.pallas_call(
        body, name=name, grid=(rows // tr,),
        in_specs=[row] * len(args) + [vec, vec], out_specs=row,
        out_shape=jax.ShapeDtypeStruct((rows, dm), F32),
        compiler_params=_cp(("parallel",)),
    )(*args, g.reshape(1, dm), b.reshape(1, dm))


def _ln_bwd_call(dy, x, res, g, name):
    rows, dm = x.shape
    tr = _row_tile(rows)
    has_res = res is not None

    def body(*refs):
        if has_res:
            dy_ref, x_ref, r_ref, g_ref, dx_ref, dr_ref, dg_ref, db_ref = refs
            u = DEEPNORM_ALPHA * r_ref[...] + x_ref[...]
        else:
            dy_ref, x_ref, g_ref, dx_ref, dg_ref, db_ref = refs
            u = x_ref[...]
        i = pl.program_id(0)
        d, rstd = _ln_stats(u)
        xhat = d * rstd
        dyv = dy_ref[...]
        dxh = dyv * g_ref[...]
        m1 = jnp.mean(dxh, axis=-1, keepdims=True)
        m2 = jnp.mean(dxh * xhat, axis=-1, keepdims=True)
        du = rstd * (dxh - m1 - xhat * m2)
        dx_ref[...] = du
        if has_res:
            dr_ref[...] = DEEPNORM_ALPHA * du
        pg = jnp.sum(dyv * xhat, axis=0, keepdims=True)
        pb = jnp.sum(dyv, axis=0, keepdims=True)

        @pl.when(i == 0)
        def _():
            dg_ref[...] = pg
            db_ref[...] = pb

        @pl.when(i > 0)
        def _():
            dg_ref[...] += pg
            db_ref[...] += pb

    row = pl.BlockSpec((tr, dm), lambda i: (i, 0))
    vec = pl.BlockSpec((1, dm), lambda i: (0, 0))
    big = jax.ShapeDtypeStruct((rows, dm), F32)
    small = jax.ShapeDtypeStruct((1, dm), F32)
    args = (dy, x, res) if has_res else (dy, x)
    n_big = 2 if has_res else 1
    outs = pl.pallas_call(
        body, name=name, grid=(rows // tr,),
        in_specs=[row] * len(args) + [vec],
        out_specs=[row] * n_big + [vec, vec],
        out_shape=[big] * n_big + [small, small],
        compiler_params=_cp(("arbitrary",)),
    )(*args, g.reshape(1, dm))
    return outs


def _make_ln(name, has_res):
    if has_res:
        @jax.custom_vjp
        def ln(x, res, g, b):
            return _ln_fwd_call(x, res, g, b, name + "_fwd")

        def fwd(x, res, g, b):
            return ln(x, res, g, b), (x, res, g)

        def bwd(saved, dy):
            x, res, g = saved
            dx, dr, dg, db = _ln_bwd_call(dy, x, res, g, name + "_bwd")
            return dx, dr, dg.reshape(-1), db.reshape(-1)
    else:
        @jax.custom_vjp
        def ln(x, g, b):
            return _ln_fwd_call(x, None, g, b, name + "_fwd")

        def fwd(x, g, b):
            return ln(x, g, b), (x, g)

        def bwd(saved, dy):
            x, g = saved
            dx, dg, db = _ln_bwd_call(dy, x, None, g, name + "_bwd")
            return dx, dg.reshape(-1), db.reshape(-1)

    ln.defvjp(fwd, bwd)
    return ln


def _rms_fwd_call(x, g, name):
    rows, dm = x.shape
    tr = _row_tile(rows)

    def body(x_ref, g_ref, o_ref):
        xv = x_ref[...]
        rstd = lax.rsqrt(jnp.mean(xv * xv, axis=-1, keepdims=True) + RMS_EPS)
        o_ref[...] = xv * rstd * g_ref[...]

    row = pl.BlockSpec((tr, dm), lambda i: (i, 0))
    vec = pl.BlockSpec((1, dm), lambda i: (0, 0))
    return pl.pallas_call(
        body, name=name, grid=(rows // tr,), in_specs=[row, vec], out_specs=row,
        out_shape=jax.ShapeDtypeStruct((rows, dm), F32), compiler_params=_cp(("parallel",)),
    )(x, g.reshape(1, dm))


def _rms_bwd_call(dy, x, g, name):
    rows, dm = x.shape
    tr = _row_tile(rows)

    def body(dy_ref, x_ref, g_ref, dx_ref, dg_ref):
        i = pl.program_id(0)
        xv = x_ref[...]
        dyv = dy_ref[...]
        rstd = lax.rsqrt(jnp.mean(xv * xv, axis=-1, keepdims=True) + RMS_EPS)
        xhat = xv * rstd
        dxh = dyv * g_ref[...]
        m2 = jnp.mean(dxh * xhat, axis=-1, keepdims=True)
        dx_ref[...] = rstd * (dxh - xhat * m2)
        pg = jnp.sum(dyv * xhat, axis=0, keepdims=True)

        @pl.when(i == 0)
        def _():
            dg_ref[...] = pg

        @pl.when(i > 0)
        def _():
            dg_ref[...] += pg

    row = pl.BlockSpec((tr, dm), lambda i: (i, 0))
    vec = pl.BlockSpec((1, dm), lambda i: (0, 0))
    return pl.pallas_call(
        body, name=name, grid=(rows // tr,), in_specs=[row, row, vec], out_specs=[row, vec],
        out_shape=[jax.ShapeDtypeStruct((rows, dm), F32), jax.ShapeDtypeStruct((1, dm), F32)],
        compiler_params=_cp(("arbitrary",)),
    )(dy, x, g.reshape(1, dm))


def _make_rms(name):
    @jax.custom_vjp
    def rms(x, g):
        return _rms_fwd_call(x, g, name + "_fwd")

    def fwd(x, g):
        return rms(x, g), (x, g)

    def bwd(saved, dy):
        x, g = saved
        dx, dg = _rms_bwd_call(dy, x, g, name + "_bwd")
        return dx, dg.reshape(-1)

    rms.defvjp(fwd, bwd)
    return rms


def _sigmoid(x):
    return 1.0 / (1.0 + jnp.exp(-x))


def _gate_fwd_call(mixed, gate, name):
    rows, dm = mixed.shape
    tr = _row_tile(rows)

    def body(m_ref, g_ref, o_ref):
        gv = g_ref[...]
        o_ref[...] = m_ref[...] * (gv * _sigmoid(gv))

    row = pl.BlockSpec((tr, dm), lambda i: (i, 0))
    return pl.pallas_call(
        body, name=name, grid=(rows // tr,), in_specs=[row, row], out_specs=row,
        out_shape=jax.ShapeDtypeStruct((rows, dm), F32), compiler_params=_cp(("parallel",)),
    )(mixed, gate)


def _gate_bwd_call(dy, mixed, gate, name):
    rows, dm = mixed.shape
    tr = _row_tile(rows)

    def body(dy_ref, m_ref, g_ref, dm_ref, dg_ref):
        gv = g_ref[...]
        dyv = dy_ref[...]
        sg = _sigmoid(gv)
        dm_ref[...] = dyv * (gv * sg)
        dg_ref[...] = dyv * m_ref[...] * (sg * (1.0 + gv * (1.0 - sg)))

    row = pl.BlockSpec((tr, dm), lambda i: (i, 0))
    out = jax.ShapeDtypeStruct((rows, dm), F32)
    return pl.pallas_call(
        body, name=name, grid=(rows // tr,), in_specs=[row, row, row], out_specs=[row, row],
        out_shape=[out, out], compiler_params=_cp(("parallel",)),
    )(dy, mixed, gate)


def _make_gate(name):
    @jax.custom_vjp
    def gate_mul(mixed, gate):
        return _gate_fwd_call(mixed, gate, name + "_fwd")

    def fwd(mixed, gate):
        return gate_mul(mixed, gate), (mixed, gate)

    def bwd(saved, dy):
        mixed, gate = saved
        dmix, dgate = _gate_bwd_call(dy, mixed, gate, name + "_bwd")
        return dmix, dgate

    gate_mul.defvjp(fwd, bwd)
    return gate_mul


def _loss_call(y, t, name):
    rows, dm = y.shape
    tr = _row_tile(rows)

    def body(y_ref, t_ref, l_ref, d_ref):
        i = pl.program_id(0)
        e = y_ref[...] - t_ref[...]
        d_ref[...] = e * (1.0 / dm)
        part = 0.5 * jnp.sum(jnp.mean(e * e, axis=-1, keepdims=True), axis=0, keepdims=True)

        @pl.when(i == 0)
        def _():
            l_ref[...] = part

        @pl.when(i > 0)
        def _():
            l_ref[...] += part

    row = pl.BlockSpec((tr, dm), lambda i: (i, 0))
    one = pl.BlockSpec((1, 1), lambda i: (0, 0))
    return pl.pallas_call(
        body, name=name, grid=(rows // tr,), in_specs=[row, row], out_specs=[one, row],
        out_shape=[jax.ShapeDtypeStruct((1, 1), F32), jax.ShapeDtypeStruct((rows, dm), F32)],
        compiler_params=_cp(("arbitrary",)),
    )(y, t)


@jax.custom_vjp
def _loss_op(y, t):
    return _loss_call(y, t, "loss_head")[0][0, 0]


def _loss_fwd(y, t):
    l, d = _loss_call(y, t, "loss_head")
    return l[0, 0], d


def _loss_bwd(d, ct):
    return ct * d, jnp.zeros_like(d)


_loss_op.defvjp(_loss_fwd, _loss_bwd)


def _attn_blocks(S, Sk):
    bq, bk = min(ATTN_BQ, S), min(ATTN_BK, Sk)
    assert S % bq == 0 and Sk % bk == 0
    return bq, bk


def _valid_t(i, j, bq, bk, strict):
    key = j * bk + lax.broadcasted_iota(jnp.int32, (bk, bq), 0)
    qry = i * bq + lax.broadcasted_iota(jnp.int32, (bk, bq), 1)
    return (key < qry) if strict else (key <= qry)


def _sm_fwd_t(qT, k, vT, cmul, causal, name):
    H, DK, S = qT.shape
    Sk, dv = k.shape[1], vT.shape[1]
    bq, bk = _attn_blocks(S, Sk)
    nq, nkb = S // bq, Sk // bk
    assert H % PAIR == 0
    heads = range(PAIR)
    if causal:
        assert S == Sk and bq == bk

    def body(qT_ref, k_ref, vT_ref, oT_ref, lse_ref):
        i = pl.program_id(1)
        qTs = [qT_ref[w] for w in heads]

        def blk(j, carry, masked):
            off = pl.multiple_of(j * bk, bk)
            sT = [_dot(k_ref[w, pl.ds(off, bk), :], qTs[w]) * cmul for w in heads]
            if masked:
                valid = _valid_t(i, j, bq, bk, False)
                sT = [jnp.where(valid, s, NEG_BIG) for s in sT]
            m_new = [jnp.maximum(carry[w][0], jnp.max(sT[w], axis=0, keepdims=True)) for w in heads]
            p = [jnp.exp2(sT[w] - m_new[w]) for w in heads]
            a = [jnp.exp2(carry[w][0] - m_new[w]) for w in heads]
            l = [a[w] * carry[w][1] + jnp.sum(p[w], axis=0, keepdims=True) for w in heads]
            acc = [a[w] * carry[w][2] + _dot(vT_ref[w, :, pl.ds(off, bk)], p[w].astype(BF16)) for w in heads]
            return tuple((m_new[w], l[w], acc[w]) for w in heads)

        carry = tuple((jnp.full((1, bq), NEG_BIG, F32), jnp.zeros((1, bq), F32), jnp.zeros((dv, bq), F32))
                      for _ in heads)
        if causal:
            carry = lax.fori_loop(0, i, lambda j, c: blk(j, c, False), carry)
            carry = blk(i, carry, True)
        else:
            carry = lax.fori_loop(0, nkb, lambda j, c: blk(j, c, False), carry)
        for w in heads:
            oT_ref[w] = carry[w][2] / carry[w][1]
            lse_ref[w] = carry[w][0] + jnp.log2(carry[w][1])

    qcol = lambda d: pl.BlockSpec((PAIR, d, bq), lambda h, i: (h, 0, i))
    return pl.pallas_call(
        body, name=name, grid=(H // PAIR, nq),
        in_specs=[qcol(DK), pl.BlockSpec((PAIR, Sk, DK), lambda h, i: (h, 0, 0)),
                  pl.BlockSpec((PAIR, dv, Sk), lambda h, i: (h, 0, 0))],
        out_specs=[qcol(dv), qcol(1)],
        out_shape=[jax.ShapeDtypeStruct((H, dv, S), F32), jax.ShapeDtypeStruct((H, 1, S), F32)],
        compiler_params=_cp(("parallel", "arbitrary"), ATTN_VMEM_LIMIT),
    )(qT, k, vT)


def _sm_bwd_t(qT, qn, k, kT, v, oT, lse, doT, do, cmul, gscale, causal, name):
    H, DK, S = qT.shape
    Sk, dv = k.shape[1], v.shape[2]
    bq, bk = _attn_blocks(S, Sk)
    nq, nkb = S // bq, Sk // bk

    def body(qT_ref, qn_ref, k_ref, kT_ref, v_ref, oT_ref, lse_ref, doT_ref, do_ref, dqT_ref, dk_ref, dv_ref):
        i = pl.program_id(1)

        @pl.when(i == 0)
        def _():
            dk_ref[...] = jnp.zeros_like(dk_ref)
            dv_ref[...] = jnp.zeros_like(dv_ref)

        qTb = qT_ref[...]
        qnb = qn_ref[...]
        dob = do_ref[...]
        doTf = doT_ref[...]
        doTb = doTf.astype(BF16)
        delta = jnp.sum(doTf * oT_ref[...], axis=0, keepdims=True)
        lse = lse_ref[...]

        def blk(j, dq, masked):
            off = pl.multiple_of(j * bk, bk)
            sT = _dot(k_ref[pl.ds(off, bk), :], qTb) * cmul
            if masked:
                sT = jnp.where(_valid_t(i, j, bq, bk, False), sT, NEG_BIG)
            p = jnp.exp2(sT - lse)
            dp = _dot(v_ref[pl.ds(off, bk), :], doTb)
            ds = p * (dp - delta)
            dsb = (ds * gscale).astype(BF16) if gscale != 1.0 else ds.astype(BF16)
            dv_ref[pl.ds(off, bk), :] += _dot(p.astype(BF16), dob)
            dk_ref[pl.ds(off, bk), :] += _dot(dsb, qnb)
            return dq + _dot(kT_ref[:, pl.ds(off, bk)], dsb)

        dq = jnp.zeros((DK, bq), F32)
        if causal:
            dq = lax.fori_loop(0, i, lambda j, c: blk(j, c, False), dq)
            dq = blk(i, dq, True)
        else:
            dq = lax.fori_loop(0, nkb, lambda j, c: blk(j, c, False), dq)
        dqT_ref[...] = dq

    qcol = lambda d: pl.BlockSpec((None, d, bq), lambda h, i: (h, 0, i))
    qrow = lambda d: pl.BlockSpec((None, bq, d), lambda h, i: (h, i, 0))
    krow = lambda d: pl.BlockSpec((None, Sk, d), lambda h, i: (h, 0, 0))
    return pl.pallas_call(
        body, name=name, grid=(H, nq),
        in_specs=[qcol(DK), qrow(DK), krow(DK), pl.BlockSpec((None, DK, Sk), lambda h, i: (h, 0, 0)), krow(dv),
                  qcol(dv), qcol(1), qcol(dv), qrow(dv)],
        out_specs=[qcol(DK), krow(DK), krow(dv)],
        out_shape=[jax.ShapeDtypeStruct((H, DK, S), F32), jax.ShapeDtypeStruct((H, Sk, DK), F32),
                   jax.ShapeDtypeStruct((H, Sk, dv), F32)],
        compiler_params=_cp(("parallel", "arbitrary"), ATTN_VMEM_LIMIT),
    )(qT, qn, k, kT, v, oT, lse, doT, do)


def _tri(n, fn):
    r = lax.broadcasted_iota(jnp.int32, (n, n), 0)
    c = lax.broadcasted_iota(jnp.int32, (n, n), 1)
    return jnp.where(fn(r, c), 1.0, 0.0).astype(BF16)


def _key_cumsum(x, tri2, suffix, base):
    bk = x.shape[0]
    c = min(CUMSUM_CHUNK, bk)
    n = bk // c
    hi32 = lax.bitcast_convert_type(lax.bitcast_convert_type(x, jnp.int32) & jnp.int32(-65536), F32)
    hi = hi32.astype(BF16)
    lo = (x - hi32).astype(BF16)
    tot = [jnp.sum(x[a * c:(a + 1) * c], axis=0, keepdims=True) for a in range(n)]
    outs = []
    for a in range(n):
        row = base
        for t in (tot[a + 1:] if suffix else tot[:a]):
            row = row + t
        stacked = jnp.concatenate([hi[a * c:(a + 1) * c], lo[a * c:(a + 1) * c]], axis=0)
        outs.append(_dot(tri2, stacked) + row)
    total = tot[0]
    for t in tot[1:]:
        total = total + t
    return (outs[0] if n == 1 else jnp.concatenate(outs, axis=0)), total


def _tri2(n, fn):
    t = _tri(n, fn)
    return jnp.concatenate([t, t], axis=1)


def _sb_logs(z):
    neg_abs = lax.bitcast_convert_type(lax.bitcast_convert_type(z, jnp.int32) | jnp.int32(-2 ** 31), F32)
    ls = jnp.minimum(z, 0.0) - jnp.log(1.0 + jnp.exp(neg_abs))
    return ls, ls - z


PAIR = LANE // HEAD_DIM


def _head_lanes(shape, w, axis):
    idx = lax.broadcasted_iota(jnp.int32, shape, axis)
    return (idx >= HEAD_DIM * w) & (idx < HEAD_DIM * (w + 1))


def _bias_rows(w, bq):
    row = lax.broadcasted_iota(jnp.int32, (LANE, bq), 0)
    return jnp.where((row >= 3 * w) & (row < 3 * w + 3), -1.0, 0.0).astype(BF16)


def _merge_pair(parts):
    return jnp.where(_head_lanes(parts[0].shape, 0, 0), parts[0], parts[1]).T


def _smp_fwd(q2, k2, vT2, bias, r, causal, name):
    S, C = q2.shape
    Sk = k2.shape[0]
    bq, bk = _attn_blocks(S, Sk)
    nq, nkb, P = S // bq, Sk // bk, C // LANE
    use_f = bias is not None
    if causal:
        assert S == Sk and bq == bk

    def body(*refs):
        if use_f:
            q_ref, k_ref, vT_ref, b_ref, r_ref, o_ref, lse_ref = refs
        else:
            q_ref, k_ref, vT_ref, o_ref, lse_ref = refs
        i = pl.program_id(1)
        qp = q_ref[...]
        heads = range(PAIR)
        qTs = [jnp.where(_head_lanes(qp.shape, w, 1), qp, jnp.zeros_like(qp)).T for w in heads]
        if use_f:
            qTs = [jnp.concatenate([qTs[w], _bias_rows(w, bq)], axis=0) for w in heads]

        def blk(j, carry, masked):
            off = pl.multiple_of(j * bk, bk)
            kb = k_ref[pl.ds(off, bk), :]
            if use_f:
                kb = jnp.concatenate([kb, b_ref[pl.ds(off, bk), :]], axis=1)
            vTb = vT_ref[:, pl.ds(off, bk)]
            sT = [_dot(kb, qTs[w]) * LOG2E for w in heads]
            if masked:
                valid = _valid_t(i, j, bq, bk, False)
                sT = [jnp.where(valid, s, NEG_BIG) for s in sT]
            cm = [jnp.max(s, axis=0, keepdims=True) for s in sT]
            if use_f:
                cm = [cm[w] + r_ref[w] for w in heads]
            m_new = [jnp.maximum(carry[w][0], cm[w]) for w in heads]
            shift = [(m_new[w] - r_ref[w]) if use_f else m_new[w] for w in heads]
            p = [jnp.exp2(sT[w] - shift[w]) for w in heads]
            a = [jnp.exp2(carry[w][0] - m_new[w]) for w in heads]
            l = [a[w] * carry[w][1] + jnp.sum(p[w], axis=0, keepdims=True) for w in heads]
            acc = [a[w] * carry[w][2] + _dot(vTb, p[w].astype(BF16)) for w in heads]
            return tuple((m_new[w], l[w], acc[w]) for w in heads)

        carry = tuple((jnp.full((1, bq), NEG_BIG, F32), jnp.zeros((1, bq), F32), jnp.zeros((LANE, bq), F32))
                      for _ in heads)
        if causal:
            carry = lax.fori_loop(0, i, lambda j, c: blk(j, c, False), carry)
            carry = blk(i, carry, True)
        else:
            carry = lax.fori_loop(0, nkb, lambda j, c: blk(j, c, False), carry)
        for w in heads:
            lse_ref[w] = carry[w][0] + jnp.log2(carry[w][1])
        o_ref[...] = _merge_pair([carry[w][2] / carry[w][1] for w in heads])

    qblk = pl.BlockSpec((bq, LANE), lambda p, i: (i, p))
    kres = pl.BlockSpec((Sk, LANE), lambda p, i: (0, p))
    stat = pl.BlockSpec((PAIR, 1, bq), lambda p, i: (p, 0, i))
    in_specs = [qblk, kres, pl.BlockSpec((LANE, Sk), lambda p, i: (p, 0))]
    args = [q2, k2, vT2]
    if use_f:
        in_specs += [kres, stat]
        args += [bias, r]
    return pl.pallas_call(
        body, name=name, grid=(P, nq), in_specs=in_specs, out_specs=[qblk, stat],
        out_shape=[jax.ShapeDtypeStruct((S, C), F32), jax.ShapeDtypeStruct((PAIR * P, 1, S), F32)],
        compiler_params=_cp(("parallel", "arbitrary"), ATTN_VMEM_LIMIT),
    )(*args)


def _smp_bwd(q2, k2, kT2, v2, o2, lse, do2, bias, r, scale, causal, name):
    S, C = q2.shape
    Sk = k2.shape[0]
    bq, bk = _attn_blocks(S, Sk)
    nq, nkb, P = S // bq, Sk // bk, C // LANE
    use_f = bias is not None

    def body(*refs):
        if use_f:
            (q_ref, k_ref, kT_ref, v_ref, o_ref, lse_ref, do_ref, b_ref, r_ref,
             dq_ref, dk_ref, dv_ref, dr_ref, dkey_ref, db_ref) = refs
        else:
            q_ref, k_ref, kT_ref, v_ref, o_ref, lse_ref, do_ref, dq_ref, dk_ref, dv_ref = refs
        i = pl.program_id(1)

        @pl.when(i == 0)
        def _():
            dk_ref[...] = jnp.zeros_like(dk_ref)
            dv_ref[...] = jnp.zeros_like(dv_ref)
            if use_f:
                db_ref[...] = jnp.zeros_like(db_ref)

        qp = q_ref[...]
        dof = do_ref[...]
        prod = dof * o_ref[...]
        heads = range(PAIR)
        mine = [_head_lanes(qp.shape, w, 1) for w in heads]
        qz = [jnp.where(mine[w], qp, jnp.zeros_like(qp)) for w in heads]
        qTs = [qz[w].T for w in heads]
        if use_f:
            qTs = [jnp.concatenate([qTs[w], _bias_rows(w, bq)], axis=0) for w in heads]
        doz = [jnp.where(mine[w], dof, 0.0).astype(BF16) for w in heads]
        doT = [doz[w].T for w in heads]
        delta = [jnp.sum(jnp.where(mine[w], prod, 0.0).T, axis=0, keepdims=True) for w in heads]
        shift = [(lse_ref[w] - r_ref[w]) if use_f else lse_ref[w] for w in heads]

        def blk(j, carry, masked):
            off = pl.multiple_of(j * bk, bk)
            kb = k_ref[pl.ds(off, bk), :]
            if use_f:
                kb = jnp.concatenate([kb, b_ref[pl.ds(off, bk), :]], axis=1)
            vb = v_ref[pl.ds(off, bk), :]
            kTb = kT_ref[:, pl.ds(off, bk)]
            sT = [_dot(kb, qTs[w]) * LOG2E for w in heads]
            if masked:
                valid = _valid_t(i, j, bq, bk, False)
                sT = [jnp.where(valid, s, NEG_BIG) for s in sT]
            p = [jnp.exp2(sT[w] - shift[w]) for w in heads]
            dp = [_dot(vb, doT[w]) for w in heads]
            ds = [p[w] * (dp[w] - delta[w]) for w in heads]
            dsb = [d.astype(BF16) for d in ds]
            dvs = [_dot(p[w].astype(BF16), doz[w]) for w in heads]
            dks = [_dot(dsb[w], qz[w]) for w in heads]
            dv_ref[pl.ds(off, bk), :] += dvs[0] + dvs[1]
            dk_ref[pl.ds(off, bk), :] += dks[0] + dks[1]
            dr = [carry[w][1] for w in heads]
            if use_f:
                dr = [dr[w] + jnp.sum(ds[w], axis=0, keepdims=True) for w in heads]
                lane = lax.broadcasted_iota(jnp.int32, (bk, LANE), 1)
                cols = [jnp.where(lane == w, jnp.sum(ds[w], axis=1, keepdims=True), 0.0) for w in heads]
                db_ref[pl.ds(off, bk), :] += cols[0] + cols[1]
            dq = [carry[w][0] + _dot(kTb, dsb[w]) for w in heads]
            return tuple((dq[w], dr[w]) for w in heads)

        carry = tuple((jnp.zeros((LANE, bq), F32), jnp.zeros((1, bq), F32)) for _ in heads)
        if causal:
            carry = lax.fori_loop(0, i, lambda j, c: blk(j, c, False), carry)
            carry = blk(i, carry, True)
        else:
            carry = lax.fori_loop(0, nkb, lambda j, c: blk(j, c, False), carry)
        if use_f:
            for w in heads:
                dr_ref[w] = carry[w][1]
        dq_ref[...] = _merge_pair([carry[w][0] for w in heads]) * scale

        if use_f:
            @pl.when(i == nq - 1)
            def _():
                def chunk(cidx, carry):
                    off = pl.multiple_of(cidx * LANE, LANE)
                    t = db_ref[pl.ds(off, LANE), :].T
                    for w in range(PAIR):
                        dkey_ref[w, :, pl.ds(off, LANE)] = t[w:w + 1, :]
                    return carry

                lax.fori_loop(0, Sk // LANE, chunk, 0)

    qblk = pl.BlockSpec((bq, LANE), lambda p, i: (i, p))
    kres = pl.BlockSpec((Sk, LANE), lambda p, i: (0, p))
    stat = pl.BlockSpec((PAIR, 1, bq), lambda p, i: (p, 0, i))
    in_specs = [qblk, kres, pl.BlockSpec((LANE, Sk), lambda p, i: (p, 0)), kres, qblk, stat, qblk]
    args = [q2, k2, kT2, v2, o2, lse, do2]
    out_specs = [qblk, kres, kres]
    out_shape = [jax.ShapeDtypeStruct((S, C), F32), jax.ShapeDtypeStruct((Sk, C), F32),
                 jax.ShapeDtypeStruct((Sk, C), F32)]
    scratch = []
    if use_f:
        in_specs += [kres, stat]
        args += [bias, r]
        out_specs += [stat, pl.BlockSpec((PAIR, 1, Sk), lambda p, i: (p, 0, 0))]
        out_shape += [jax.ShapeDtypeStruct((PAIR * P, 1, S), F32), jax.ShapeDtypeStruct((PAIR * P, 1, Sk), F32)]
        scratch = [pltpu.VMEM((Sk, LANE), F32)]
    return pl.pallas_call(
        body, name=name, grid=(P, nq), in_specs=in_specs, out_specs=out_specs, out_shape=out_shape,
        scratch_shapes=scratch, compiler_params=_cp(("parallel", "arbitrary"), ATTN_VMEM_LIMIT),
    )(*args)


def _sbp_fwd(q2, k2, vT2, name):
    S, C = q2.shape
    bq, bk = _attn_blocks(S, S)
    assert bq == bk
    nq, P = S // bq, C // LANE
    c = min(CUMSUM_CHUNK, bk)

    def body(q_ref, k_ref, vT_ref, o_ref, lt_ref):
        i = pl.program_id(1)
        qp = q_ref[...]
        after = _tri2(c, lambda s, j: j > s)
        heads = range(PAIR)
        qTs = [jnp.where(_head_lanes(qp.shape, w, 1), qp, jnp.zeros_like(qp)).T for w in heads]

        def blk(jj, carry, masked):
            j = i - jj
            off = pl.multiple_of(j * bk, bk)
            kb = k_ref[pl.ds(off, bk), :]
            vTb = vT_ref[:, pl.ds(off, bk)]
            logs = [_sb_logs(_dot(kb, qTs[w])) for w in heads]
            ls, lk = [t[0] for t in logs], [t[1] for t in logs]
            if masked:
                valid = _valid_t(i, j, bq, bk, True)
                lk = [jnp.where(valid, t, 0.0) for t in lk]
            cs = [_key_cumsum(lk[w], after, True, carry[w][0]) for w in heads]
            wgt = [jnp.exp(ls[w] + cs[w][0]) for w in heads]
            if masked:
                wgt = [jnp.where(valid, t, 0.0) for t in wgt]
            acc = [carry[w][1] + _dot(vTb, wgt[w].astype(BF16)) for w in heads]
            return tuple((carry[w][0] + cs[w][1], acc[w]) for w in heads)

        carry = tuple((jnp.zeros((1, bq), F32), jnp.zeros((LANE, bq), F32)) for _ in heads)
        carry = blk(0, carry, True)
        carry = lax.fori_loop(1, i + 1, lambda jj, cr: blk(jj, cr, False), carry)
        for w in heads:
            lt_ref[w] = carry[w][0]
        o_ref[...] = _merge_pair([carry[w][1] for w in heads])

    qblk = pl.BlockSpec((bq, LANE), lambda p, i: (i, p))
    stat = pl.BlockSpec((PAIR, 1, bq), lambda p, i: (p, 0, i))
    return pl.pallas_call(
        body, name=name, grid=(P, nq),
        in_specs=[qblk, pl.BlockSpec((S, LANE), lambda p, i: (0, p)), pl.BlockSpec((LANE, S), lambda p, i: (p, 0))],
        out_specs=[qblk, stat],
        out_shape=[jax.ShapeDtypeStruct((S, C), F32), jax.ShapeDtypeStruct((PAIR * P, 1, S), F32)],
        compiler_params=_cp(("parallel", "arbitrary"), ATTN_VMEM_LIMIT),
    )(q2, k2, vT2)


def _sbp_bwd(q2, k2, kT2, v2, lt, do2, scale, name):
    S, C = q2.shape
    bq, bk = _attn_blocks(S, S)
    nq, P = S // bq, C // LANE
    c = min(CUMSUM_CHUNK, bk)

    def body(q_ref, k_ref, kT_ref, v_ref, lt_ref, do_ref, dq_ref, dk_ref, dv_ref):
        i = pl.program_id(1)

        @pl.when(i == 0)
        def _():
            dk_ref[...] = jnp.zeros_like(dk_ref)
            dv_ref[...] = jnp.zeros_like(dv_ref)

        qp = q_ref[...]
        dof = do_ref[...]
        upto = _tri2(c, lambda s, j: j <= s)
        before = _tri2(c, lambda s, j: j < s)
        heads = range(PAIR)
        mine = [_head_lanes(qp.shape, w, 1) for w in heads]
        qz = [jnp.where(mine[w], qp, jnp.zeros_like(qp)) for w in heads]
        qTs = [qz[w].T for w in heads]
        doz = [jnp.where(mine[w], dof, 0.0).astype(BF16) for w in heads]
        doT = [doz[w].T for w in heads]
        ltot = [lt_ref[w] for w in heads]

        def blk(j, carry, masked):
            off = pl.multiple_of(j * bk, bk)
            kb = k_ref[pl.ds(off, bk), :]
            vb = v_ref[pl.ds(off, bk), :]
            kTb = kT_ref[:, pl.ds(off, bk)]
            logs = [_sb_logs(_dot(kb, qTs[w])) for w in heads]
            ls, lk = [t[0] for t in logs], [t[1] for t in logs]
            if masked:
                valid = _valid_t(i, j, bq, bk, True)
                lk = [jnp.where(valid, t, 0.0) for t in lk]
            pin = [_key_cumsum(lk[w], upto, False, carry[w][1] - ltot[w]) for w in heads]
            wgt = [jnp.exp(ls[w] - pin[w][0]) for w in heads]
            if masked:
                wgt = [jnp.where(valid, t, 0.0) for t in wgt]
            g = [_dot(vb, doT[w]) * wgt[w] for w in heads]
            cin = [_key_cumsum(g[w], before, False, carry[w][2]) for w in heads]
            sig = [jnp.exp(t) for t in ls]
            dz = [g[w] * (1.0 - sig[w]) - cin[w][0] * sig[w] for w in heads]
            if masked:
                dz = [jnp.where(valid, t, 0.0) for t in dz]
            dzb = [t.astype(BF16) for t in dz]
            dvs = [_dot(wgt[w].astype(BF16), doz[w]) for w in heads]
            dks = [_dot(dzb[w], qz[w]) for w in heads]
            dv_ref[pl.ds(off, bk), :] += dvs[0] + dvs[1]
            dk_ref[pl.ds(off, bk), :] += dks[0] + dks[1]
            return tuple((carry[w][0] + _dot(kTb, dzb[w]), carry[w][1] + pin[w][1], carry[w][2] + cin[w][1])
                         for w in heads)

        carry = tuple((jnp.zeros((LANE, bq), F32), jnp.zeros((1, bq), F32), jnp.zeros((1, bq), F32)) for _ in heads)
        carry = lax.fori_loop(0, i, lambda j, cr: blk(j, cr, False), carry)
        carry = blk(i, carry, True)
        dq_ref[...] = _merge_pair([carry[w][0] for w in heads]) * scale

    qblk = pl.BlockSpec((bq, LANE), lambda p, i: (i, p))
    kres = pl.BlockSpec((S, LANE), lambda p, i: (0, p))
    stat = pl.BlockSpec((PAIR, 1, bq), lambda p, i: (p, 0, i))
    return pl.pallas_call(
        body, name=name, grid=(P, nq),
        in_specs=[qblk, kres, pl.BlockSpec((LANE, S), lambda p, i: (p, 0)), kres, stat, qblk],
        out_specs=[qblk, kres, kres],
        out_shape=[jax.ShapeDtypeStruct((S, C), F32)] * 3,
        compiler_params=_cp(("parallel", "arbitrary"), ATTN_VMEM_LIMIT),
    )(q2, k2, kT2, v2, lt, do2)


def _bias_cols(f_cum):
    H, Sk = f_cum.shape
    terms = jnp.stack(_split3(f_cum), axis=-1)
    packed = terms.reshape(H // PAIR, PAIR, Sk, 3).transpose(2, 0, 1, 3).reshape(Sk, H // PAIR, PAIR * 3)
    return jnp.pad(packed, ((0, 0), (0, 0), (0, LANE - PAIR * 3))).reshape(Sk, -1)


def _make_packed_softmax(name, scale, causal, use_f):
    assert _pow2(scale)

    def run_fwd(q, k, v, f_cum):
        q16, k16, v16 = (q * scale).astype(BF16), k.astype(BF16), v.astype(BF16)
        bias = _bias_cols(f_cum) if use_f else None
        r = (f_cum * LOG2E)[:, None, :] if use_f else None
        o, lse = _smp_fwd(q16, k16, v16.T, bias, r, causal, name + "_fwd")
        return o, (q16, k16, v16, o, lse, bias, r)

    def run_bwd(saved, do):
        q16, k16, v16, o, lse, bias, r = saved
        outs = _smp_bwd(q16, k16, k16.T, v16, o, lse, do, bias, r, scale, causal, name + "_bwd")
        if use_f:
            dq, dk, dv, dr, dkey = outs
            return dq, dk, dv, dr[:, 0, :] - dkey[:, 0, :]
        return tuple(outs)

    if use_f:
        @jax.custom_vjp
        def attn(q, k, v, f_cum):
            return run_fwd(q, k, v, f_cum)[0]

        attn.defvjp(run_fwd, run_bwd)
    else:
        @jax.custom_vjp
        def attn(q, k, v):
            return run_fwd(q, k, v, None)[0]

        attn.defvjp(lambda q, k, v: run_fwd(q, k, v, None), run_bwd)
    return attn


def _make_packed_sb(name, scale):
    assert _pow2(scale)

    def run_fwd(q, k, v):
        q16, k16, v16 = (q * scale).astype(BF16), k.astype(BF16), v.astype(BF16)
        o, lt = _sbp_fwd(q16, k16, v16.T, name + "_fwd")
        return o, (q16, k16, v16, lt)

    def run_bwd(saved, do):
        q16, k16, v16, lt = saved
        return tuple(_sbp_bwd(q16, k16, k16.T, v16, lt, do, scale, name + "_bwd"))

    @jax.custom_vjp
    def attn(q, k, v):
        return run_fwd(q, k, v)[0]

    attn.defvjp(run_fwd, run_bwd)
    return attn


def _round_bf16(x):
    return lax.reduce_precision(x, exponent_bits=8, mantissa_bits=7)


def _split3(x):
    hi = _round_bf16(x)
    mid = _round_bf16(x - hi)
    lo = _round_bf16(x - hi - mid)
    return hi.astype(BF16), mid.astype(BF16), lo.astype(BF16)


def _pow2(x):
    m, _ = math.frexp(x)
    return m == 0.5


def _pad_last(x, n):
    return jnp.pad(x, [(0, 0)] * (x.ndim - 1) + [(0, n - x.shape[-1])])


def _layouts(q, k, scale):
    qh = _pad_last(jnp.transpose(q * scale if _pow2(scale) else q, (1, 0, 2)).astype(BF16), LANE)
    kh = _pad_last(jnp.transpose(k, (1, 0, 2)).astype(BF16), LANE)
    return qh, jnp.transpose(qh, (0, 2, 1)), kh, jnp.transpose(kh, (0, 2, 1))


def _make_softmax_attn(name, scale, causal, d):
    pre = _pow2(scale)
    cmul = LOG2E if pre else scale * LOG2E
    gscale = 1.0 if pre else scale

    def run_fwd(q, k, v):
        qn, qT, kn, kT = _layouts(q, k, scale)
        vn = jnp.transpose(v, (1, 0, 2)).astype(BF16)
        oT, lse = _sm_fwd_t(qT, kn, jnp.transpose(vn, (0, 2, 1)), cmul, causal, name + "_fwd")
        return jnp.transpose(oT, (2, 0, 1)), (qn, qT, kn, kT, vn, oT, lse)

    def run_bwd(saved, dout):
        qn, qT, kn, kT, vn, oT, lse = saved
        doT = jnp.transpose(dout, (1, 2, 0))
        do = jnp.transpose(dout, (1, 0, 2)).astype(BF16)
        dqT, dk, dv = _sm_bwd_t(qT, qn, kn, kT, vn, oT, lse, doT, do, cmul, gscale, causal, name + "_bwd")
        dq = jnp.transpose(dqT[:, :d, :], (2, 0, 1))
        if pre:
            dq = dq * scale
        return dq, jnp.transpose(dk[:, :, :d], (1, 0, 2)), jnp.transpose(dv, (1, 0, 2))

    @jax.custom_vjp
    def attn(q, k, v):
        return run_fwd(q, k, v)[0]

    attn.defvjp(run_fwd, run_bwd)
    return attn


def _rope(x, positions):
    half = x.shape[-1] // 2
    inv_freq = ROPE_THETA ** (-jnp.arange(half, dtype=F32) / half)
    ang = positions.astype(F32)[:, None] * inv_freq[None, :]
    ang = ang.reshape((ang.shape[0],) + (1,) * (x.ndim - 2) + (half,))
    cos, sin = jnp.cos(ang), jnp.sin(ang)
    x1, x2 = x[..., :half], x[..., half:]
    return jnp.concatenate([x1 * cos - x2 * sin, x1 * sin + x2 * cos], axis=-1)


def _permute_w_in(w):
    parts = [w[:, _ORIG_OFF[idx]:_ORIG_OFF[idx] + SPLIT_SIZES[idx]] for _, idx in _PERM]
    pad = jnp.zeros((w.shape[0], PROJ_COLS - IN_COLS), w.dtype)
    return jnp.concatenate(parts + [pad], axis=1)


def _make_proj(name):
    def split(proj):
        out, off = [], 0
        for _, idx in _PERM:
            out.append(proj[:, off:off + SPLIT_SIZES[idx]])
            off += SPLIT_SIZES[idx]
        return tuple(out)

    def run_fwd(a, w):
        a16, w16 = a.astype(BF16), w.astype(BF16)
        return split(_matmul(a16, w16, "nn", name + "_fwd")), (a16, w16)

    def run_bwd(res, cts):
        a16, w16 = res
        pad = jnp.zeros((a16.shape[0], PROJ_COLS - IN_COLS), BF16)
        dy16 = jnp.concatenate([c.astype(BF16) for c in cts] + [pad], axis=1)
        return _matmul(dy16, w16, "nt", name + "_dx"), _matmul(a16.T, dy16, "nn", name + "_dw")

    @jax.custom_vjp
    def proj(a, w):
        return run_fwd(a, w)[0]

    proj.defvjp(run_fwd, run_bwd)
    return lambda a, w: {n: part for (n, _), part in zip(_PERM, proj(a, w))}


def _trunk_loss(wts, x2d, mem2d, target2d):
    s = x2d.shape[0]
    positions = jnp.arange(s)
    head_scale = HEAD_DIM ** -0.5
    mla_scale = (MLA_NOPE + MLA_ROPE) ** -0.5

    h = _make_ln("ln_in", False)(x2d, wts["ln_in_g"], wts["ln_in_b"])
    mem_n = _make_ln("ln_mem", False)(mem2d, wts["mem_ln_g"], wts["mem_ln_b"])

    for l in range(DEPTH):
        tag = f"l{l}_"
        p = _make_proj(tag + "proj")(h, _permute_w_in(wts["w_in"][l]))

        log_f = jax.nn.log_sigmoid(p["f_logit"] + wts["b_forget"][l])
        f_cum = jnp.cumsum(log_f, axis=0).T
        out_fox = _make_packed_softmax(tag + "fox", head_scale, True, True)(p["fq"], p["fk"], p["fv"], f_cum)

        out_sb = _make_packed_sb(tag + "sb", head_scale)(p["sq"], p["sk"], p["sv"])

        cqn = _make_rms(tag + "rms_q")(p["c_q"], wts["mla_q_norm_g"][l])
        q_mla = _make_mm(tag + "q_up")(cqn, wts["w_mla_q_up"][l]).reshape(s, N_HEADS, MLA_NOPE + MLA_ROPE)
        ckvn = _make_rms(tag + "rms_kv")(p["c_kv"], wts["mla_kv_norm_g"][l])
        kv_mla = _make_mm(tag + "kv_up")(ckvn, wts["w_mla_kv_up"][l]).reshape(s, N_HEADS, MLA_NOPE + MLA_V)
        q_full = jnp.concatenate([q_mla[..., :MLA_NOPE], _rope(q_mla[..., MLA_NOPE:], positions)], axis=-1)
        k_rope = jnp.broadcast_to(_rope(p["k_rot"], positions)[:, None, :], (s, N_HEADS, MLA_ROPE))
        k_full = jnp.concatenate([kv_mla[..., :MLA_NOPE], k_rope], axis=-1)
        out_mla = _make_softmax_attn(tag + "mla", mla_scale, True, MLA_NOPE + MLA_ROPE)(
            q_full, k_full, kv_mla[..., MLA_NOPE:]).reshape(s, GROUP_W)

        mkv = _make_mm(tag + "mem_kv")(mem_n, wts["w_mem_kv"][l])
        out_mem = _make_packed_softmax(tag + "mem", head_scale, False, False)(
            p["mq"], mkv[:, :GROUP_W], mkv[:, GROUP_W:])

        mixed = jnp.concatenate([out_fox, out_sb, out_mla, out_mem], axis=-1)
        gated = _make_gate(tag + "gate")(mixed, p["gate"])
        y = _make_mm(tag + "out")(gated, wts["w_out"][l])
        h = _make_ln(tag + "ln", True)(y, h, wts["ln_g"][l], wts["ln_b"][l])

    return _loss_op(h, target2d)


def _mesh_pos():
    x, y, c = (lax.axis_index(a) for a in MESH_AXES)
    return x, y, c, 4 * x + 2 * y + c


def _peer(x, y, c, mask):
    return (x ^ ((mask >> 2) & 1), y ^ ((mask >> 1) & 1), c ^ (mask & 1))


_ANY = pl.BlockSpec(memory_space=pl.ANY)


def _all_gather(row_shards, stack_shards):
    n_row, n_all = len(row_shards), len(row_shards) + len(stack_shards)
    shards = list(row_shards) + list(stack_shards)

    def body(*refs):
        ins, outs = refs[:n_all], refs[n_all:2 * n_all]
        send_sems, recv_sems, local_sems = refs[2 * n_all:]
        x, y, c, me = _mesh_pos()

        def window(t, slot):
            if t < n_row:
                rows = shards[t].shape[1]
                return outs[t].at[:, pl.ds(slot * rows, rows), :]
            return outs[t].at[slot]

        local = [pltpu.make_async_copy(ins[t], window(t, me), local_sems.at[t]) for t in range(n_all)]
        for cp in local:
            cp.start()
        sends = []
        for mask in range(1, N_DEV):
            for t in range(n_all):
                cp = pltpu.make_async_remote_copy(
                    src_ref=ins[t], dst_ref=window(t, me), send_sem=send_sems.at[t, mask - 1],
                    recv_sem=recv_sems.at[t, mask - 1], device_id=_peer(x, y, c, mask),
                    device_id_type=pl.DeviceIdType.MESH)
                cp.start()
                sends.append(cp)
        for mask in range(1, N_DEV):
            for t in range(n_all):
                pltpu.make_async_remote_copy(
                    src_ref=ins[t], dst_ref=window(t, me ^ mask), send_sem=send_sems.at[t, mask - 1],
                    recv_sem=recv_sems.at[t, mask - 1], device_id=_peer(x, y, c, mask),
                    device_id_type=pl.DeviceIdType.MESH).wait_recv()
        for cp in sends:
            cp.wait_send()
        for cp in local:
            cp.wait()

    out_shape = [jax.ShapeDtypeStruct((a.shape[0], N_DEV * a.shape[1], a.shape[2]), a.dtype) for a in row_shards]
    out_shape += [jax.ShapeDtypeStruct((N_DEV,) + a.shape, a.dtype) for a in stack_shards]
    return pl.pallas_call(
        body, name="all_gather_weights", in_specs=[_ANY] * n_all, out_specs=[_ANY] * n_all, out_shape=out_shape,
        scratch_shapes=[pltpu.SemaphoreType.DMA((n_all, N_DEV - 1)), pltpu.SemaphoreType.DMA((n_all, N_DEV - 1)),
                        pltpu.SemaphoreType.DMA((n_all,))],
    )(*shards)


def _reduce_scatter(row_full, stack_full, bcast):
    n_row, n_stack = len(row_full), len(stack_full)
    n_all = n_row + n_stack + len(bcast)
    fulls = list(row_full) + list(stack_full) + list(bcast)

    def body(*refs):
        ins, outs = refs[:n_all], refs[n_all:2 * n_all]
        send_sems, recv_sems, local_sems = refs[2 * n_all:]
        x, y, c, me = _mesh_pos()

        def part(t, slot):
            if t < n_row:
                rows = fulls[t].shape[1] // N_DEV
                return ins[t].at[:, pl.ds(slot * rows, rows), :]
            if t < n_row + n_stack:
                return ins[t].at[slot]
            return ins[t]

        local = [pltpu.make_async_copy(part(t, me), outs[t].at[me], local_sems.at[t]) for t in range(n_all)]
        for cp in local:
            cp.start()
        sends = []
        for mask in range(1, N_DEV):
            for t in range(n_all):
                cp = pltpu.make_async_remote_copy(
                    src_ref=part(t, me ^ mask), dst_ref=outs[t].at[me], send_sem=send_sems.at[t, mask - 1],
                    recv_sem=recv_sems.at[t, mask - 1], device_id=_peer(x, y, c, mask),
                    device_id_type=pl.DeviceIdType.MESH)
                cp.start()
                sends.append(cp)
        for mask in range(1, N_DEV):
            for t in range(n_all):
                pltpu.make_async_remote_copy(
                    src_ref=part(t, me), dst_ref=outs[t].at[me ^ mask], send_sem=send_sems.at[t, mask - 1],
                    recv_sem=recv_sems.at[t, mask - 1], device_id=_peer(x, y, c, mask),
                    device_id_type=pl.DeviceIdType.MESH).wait_recv()
        for cp in sends:
            cp.wait_send()
        for cp in local:
            cp.wait()

    out_shape = [jax.ShapeDtypeStruct((N_DEV, a.shape[0], a.shape[1] // N_DEV, a.shape[2]), a.dtype) for a in row_full]
    out_shape += [jax.ShapeDtypeStruct(a.shape, a.dtype) for a in stack_full]
    out_shape += [jax.ShapeDtypeStruct((N_DEV,) + a.shape, a.dtype) for a in bcast]
    return pl.pallas_call(
        body, name="reduce_scatter_grads", in_specs=[_ANY] * n_all, out_specs=[_ANY] * n_all, out_shape=out_shape,
        scratch_shapes=[pltpu.SemaphoreType.DMA((n_all, N_DEV - 1)), pltpu.SemaphoreType.DMA((n_all, N_DEV - 1)),
                        pltpu.SemaphoreType.DMA((n_all,))],
    )(*fulls)


def _adamw(slots, w, m, v, name):
    shape = w.shape
    cols = shape[-1]
    rows = math.prod(shape[:-1])
    tr = _pick(rows, (64, 32, 16, 8))
    c1 = 1.0 - ADAM_B1 ** ADAM_STEP
    c2 = 1.0 - ADAM_B2 ** ADAM_STEP

    def body(s_ref, w_ref, m_ref, v_ref, g_ref, d_ref, nm_ref, nv_ref):
        g = s_ref[0].astype(F32)
        for k in range(1, N_DEV):
            g = g + s_ref[k].astype(F32)
        nm = ADAM_B1 * m_ref[...] + (1.0 - ADAM_B1) * g
        nv = ADAM_B2 * v_ref[...] + (1.0 - ADAM_B2) * (g * g)
        g_ref[...] = g
        nm_ref[...] = nm
        nv_ref[...] = nv
        d_ref[...] = -ADAM_LR * ((nm / c1) / (jnp.sqrt(nv / c2) + ADAM_EPS) + ADAM_WD * w_ref[...])

    row = pl.BlockSpec((tr, cols), lambda i: (i, 0))
    out = jax.ShapeDtypeStruct((rows, cols), F32)
    outs = pl.pallas_call(
        body, name=name, grid=(rows // tr,),
        in_specs=[pl.BlockSpec((N_DEV, tr, cols), lambda i: (0, i, 0)), row, row, row],
        out_specs=[row] * 4, out_shape=[out] * 4, compiler_params=_cp(("parallel",)),
    )(slots.reshape(N_DEV, rows, cols), w.reshape(rows, cols), m.reshape(rows, cols), v.reshape(rows, cols))
    return [o.reshape(shape) for o in outs]


_SMALL = ("ln_in_g", "ln_in_b", "mem_ln_g", "mem_ln_b", "b_forget", "mla_q_norm_g", "mla_kv_norm_g", "ln_g", "ln_b")
_ORDER = ("ln_in_g", "ln_in_b", "mem_ln_g", "mem_ln_b", "w_in", "b_forget", "mla_q_norm_g", "w_mla_q_up",
          "mla_kv_norm_g", "w_mla_kv_up", "w_mem_kv", "w_out", "ln_g", "ln_b")


def _pack_small(d):
    flat = jnp.concatenate([d[n].reshape(-1) for n in _SMALL])
    n = flat.shape[0]
    padded = ((n + 8 * LANE - 1) // (8 * LANE)) * (8 * LANE)
    return jnp.pad(flat, (0, padded - n)).reshape(-1, LANE)


def _unpack_small(packed, like):
    flat, out, off = packed.reshape(-1), {}, 0
    for n in _SMALL:
        size = math.prod(like[n].shape)
        out[n] = flat[off:off + size].reshape(like[n].shape)
        off += size
    return out


def _unstack_cols(g):
    n, l, r, c = g.shape
    return g.transpose(1, 2, 0, 3).reshape(l, r, n * c)


def _stack_cols(g):
    l, r, nc = g.shape
    return g.reshape(l, r, N_DEV, nc // N_DEV).transpose(2, 0, 1, 3)


def kernel(x, mem, ln_in_g, ln_in_b, mem_ln_g, mem_ln_b, w_in, b_forget, mla_q_norm_g, w_mla_q_up, mla_kv_norm_g, w_mla_kv_up, w_mem_kv, w_out, ln_g, ln_b, loss_target, m_ln_in_g, m_ln_in_b, m_mem_ln_g, m_mem_ln_b, m_w_in, m_b_forget, m_mla_q_norm_g, m_w_mla_q_up, m_mla_kv_norm_g, m_w_mla_kv_up, m_w_mem_kv, m_w_out, m_ln_g, m_ln_b, v_ln_in_g, v_ln_in_b, v_mem_ln_g, v_mem_ln_b, v_w_in, v_b_forget, v_mla_q_norm_g, v_w_mla_q_up, v_mla_kv_norm_g, v_w_mla_kv_up, v_w_mem_kv, v_w_out, v_ln_g, v_ln_b):
    w_shard = dict(ln_in_g=ln_in_g, ln_in_b=ln_in_b, mem_ln_g=mem_ln_g, mem_ln_b=mem_ln_b, w_in=w_in,
                   b_forget=b_forget, mla_q_norm_g=mla_q_norm_g, w_mla_q_up=w_mla_q_up,
                   mla_kv_norm_g=mla_kv_norm_g, w_mla_kv_up=w_mla_kv_up, w_mem_kv=w_mem_kv, w_out=w_out,
                   ln_g=ln_g, ln_b=ln_b)
    m_shard = dict(ln_in_g=m_ln_in_g, ln_in_b=m_ln_in_b, mem_ln_g=m_mem_ln_g, mem_ln_b=m_mem_ln_b, w_in=m_w_in,
                   b_forget=m_b_forget, mla_q_norm_g=m_mla_q_norm_g, w_mla_q_up=m_w_mla_q_up,
                   mla_kv_norm_g=m_mla_kv_norm_g, w_mla_kv_up=m_w_mla_kv_up, w_mem_kv=m_w_mem_kv, w_out=m_w_out,
                   ln_g=m_ln_g, ln_b=m_ln_b)
    v_shard = dict(ln_in_g=v_ln_in_g, ln_in_b=v_ln_in_b, mem_ln_g=v_mem_ln_g, mem_ln_b=v_mem_ln_b, w_in=v_w_in,
                   b_forget=v_b_forget, mla_q_norm_g=v_mla_q_norm_g, w_mla_q_up=v_w_mla_q_up,
                   mla_kv_norm_g=v_mla_kv_norm_g, w_mla_kv_up=v_w_mla_kv_up, w_mem_kv=v_w_mem_kv, w_out=v_w_out,
                   ln_g=v_ln_g, ln_b=v_ln_b)

    to16 = lambda ws: [a.astype(BF16) for a in ws]
    gathered = _all_gather(to16([w_in, w_mem_kv, w_out]), to16([w_mla_q_up, w_mla_kv_up]))
    g_in, g_mem, g_out, g_qup, g_kvup = [a.astype(F32) for a in gathered]
    full = dict(w_shard)
    full.update(w_in=g_in, w_mem_kv=g_mem, w_out=g_out, w_mla_q_up=_unstack_cols(g_qup),
                w_mla_kv_up=_unstack_cols(g_kvup))

    loss_local, (grad_w, grad_x) = jax.value_and_grad(_trunk_loss, argnums=(0, 1))(
        full, x[0], mem[0], loss_target[0])

    s_in, s_mem, s_out, s_qup, s_kvup, s_small = _reduce_scatter(
        to16([grad_w["w_in"], grad_w["w_mem_kv"], grad_w["w_out"]]),
        to16([_stack_cols(grad_w["w_mla_q_up"]), _stack_cols(grad_w["w_mla_kv_up"])]),
        [_pack_small(grad_w)])

    res = {}
    for name, slots in (("w_in", s_in), ("w_mem_kv", s_mem), ("w_out", s_out), ("w_mla_q_up", s_qup),
                        ("w_mla_kv_up", s_kvup)):
        res[name] = _adamw(slots, w_shard[name], m_shard[name], v_shard[name], "adamw_" + name)
    small = _adamw(s_small, _pack_small(w_shard), _pack_small(m_shard), _pack_small(v_shard), "adamw_small")
    small = [_unpack_small(a, w_shard) for a in small]
    for name in _SMALL:
        res[name] = [a[name] for a in small]

    loss = lax.psum(loss_local, MESH_AXES)
    outs = [loss, grad_x[None]]
    for k in range(4):
        outs += [res[name][k] for name in _ORDER]
    return tuple(outs)
```

```python
import functools
import math

import jax
import jax.numpy as jnp
from jax import lax
from jax.experimental import pallas as pl
from jax.experimental.pallas import tpu as pltpu

F32 = jnp.float32
BF16 = jnp.bfloat16

D_MODEL = 1024
DEPTH = 2
GROUP_W = 256
N_HEADS = 4
HEAD_DIM = 64
MLA_Q_RANK = 256
MLA_KV_RANK = 128
MLA_NOPE = 64
MLA_ROPE = 32
MLA_V = 64
ROPE_THETA = 10000.0
LN_EPS = 1e-5
RMS_EPS = 1e-6
DEEPNORM_ALPHA = (2 * DEPTH) ** 0.25
SPLIT_SIZES = (256, 256, 256, 4, 256, 256, 256, 256, 128, 32, 256, 1024)
IN_COLS = sum(SPLIT_SIZES)
_ORIG_OFF = [sum(SPLIT_SIZES[:i]) for i in range(len(SPLIT_SIZES))]
_PERM = (("fq", 0), ("fk", 1), ("fv", 2), ("sq", 4), ("sk", 5), ("sv", 6), ("c_q", 7), ("c_kv", 8),
         ("mq", 10), ("gate", 11), ("k_rot", 9), ("f_logit", 3))
LANE = 128
PROJ_COLS = ((IN_COLS + LANE - 1) // LANE) * LANE

ADAM_LR = 0.001
ADAM_B1 = 0.9
ADAM_B2 = 0.999
ADAM_EPS = 1e-08
ADAM_WD = 0.01
ADAM_STEP = 10

N_DEV = 8
MESH_AXES = ("x", "y", "c")
VMEM_LIMIT = 48 * 1024 * 1024
ATTN_VMEM_LIMIT = 56 * 1024 * 1024
ATTN_BQ = 512
ATTN_BK = 512
CUMSUM_CHUNK = 256
NEG_BIG = -1e30
LOG2E = math.log2(math.e)
MM_TM, MM_TN, MM_TK, MM_TK_NT = 1024, 1664, 1024, 3328

_NT = (((1,), (1,)), ((), ()))
_NN = (((1,), (0,)), ((), ()))


def _cp(sem, vmem=VMEM_LIMIT):
    return pltpu.CompilerParams(dimension_semantics=sem, vmem_limit_bytes=vmem)


def _dot(a, b, dims=_NN):
    return lax.dot_general(a, b, dims, preferred_element_type=F32)


def _pick(n, cands):
    for c in cands:
        if c <= n and n % c == 0:
            return c
    return n


def _tile(n, cap):
    if n <= cap:
        return n
    best = None
    for d in range(LANE, cap + 1, LANE):
        if n % d == 0:
            best = d
    assert best is not None, (n, cap)
    return best


def _matmul(a, b, mode, name, also16=False):
    if mode == "nn":
        (M, K), (K2, N) = a.shape, b.shape
    else:
        (M, K), (N, K2) = a.shape, b.shape
    assert K == K2 and a.dtype == BF16 and b.dtype == BF16, (a.shape, b.shape, mode)
    tm, tn = _tile(M, MM_TM), _tile(N, MM_TN)
    tk = _tile(K, MM_TK if mode == "nn" else MM_TK_NT)
    nk = K // tk
    dims = _NN if mode == "nn" else _NT

    def body(a_ref, b_ref, *rest):
        o_ref, acc_ref = rest[0], rest[-1]
        part = _dot(a_ref[...], b_ref[...], dims)
        if nk == 1:
            o_ref[...] = part
            if also16:
                rest[1][...] = part.astype(BF16)
        else:
            assert not also16
            k = pl.program_id(2)

            @pl.when(k == 0)
            def _():
                acc_ref[...] = part

            @pl.when(k > 0)
            def _():
                acc_ref[...] += part

            @pl.when(k == nk - 1)
            def _():
                o_ref[...] = acc_ref[...]

    a_spec = pl.BlockSpec((tm, tk), lambda j, i, k: (i, k))
    if mode == "nn":
        b_spec = pl.BlockSpec((tk, tn), lambda j, i, k: (k, j))
    else:
        b_spec = pl.BlockSpec((tn, tk), lambda j, i, k: (j, k))
    acc_shape = (tm, tn) if nk > 1 else (8, LANE)
    o_spec = pl.BlockSpec((tm, tn), lambda j, i, k: (i, j))
    outs = pl.pallas_call(
        body, name=name, grid=(N // tn, M // tm, nk),
        in_specs=[a_spec, b_spec],
        out_specs=[o_spec, o_spec] if also16 else o_spec,
        out_shape=[jax.ShapeDtypeStruct((M, N), F32), jax.ShapeDtypeStruct((M, N), BF16)] if also16
        else jax.ShapeDtypeStruct((M, N), F32),
        scratch_shapes=[pltpu.VMEM(acc_shape, F32)],
        compiler_params=_cp(("parallel", "parallel", "arbitrary")),
    )(a, b)
    return outs


def _make_mm(name):
    @jax.custom_vjp
    def mm(a, w):
        return _matmul(a.astype(BF16), w.astype(BF16), "nn", name + "_fwd")

    def fwd(a, w):
        a16, w16 = a.astype(BF16), w.astype(BF16)
        return _matmul(a16, w16, "nn", name + "_fwd"), (a16, w16)

    def bwd(res, dy):
        a16, w16 = res
        dy16 = dy.astype(BF16)
        da = _matmul(dy16, w16, "nt", name + "_dx")
        dw = _matmul(a16.T, dy16, "nn", name + "_dw")
        return da, dw

    mm.defvjp(fwd, bwd)
    return mm


def _row_tile(rows):
    return _pick(rows, (512, 256, 128, 64, 32, 16, 8))


def _ln_stats(u):
    mu = jnp.mean(u, axis=-1, keepdims=True)
    d = u - mu
    var = jnp.mean(d * d, axis=-1, keepdims=True)
    return d, lax.rsqrt(var + LN_EPS)


def _ln_fwd_call(x, res, g, b, name, also16=False):
    rows, dm = x.shape
    tr = _row_tile(rows)
    has_res = res is not None
    n_in = 2 if has_res else 1

    def body(*refs):
        if has_res:
            u = DEEPNORM_ALPHA * refs[1][...] + refs[0][...]
        else:
            u = refs[0][...]
        g_ref, b_ref = refs[n_in], refs[n_in + 1]
        d, rstd = _ln_stats(u)
        y = d * rstd * g_ref[...] + b_ref[...]
        refs[n_in + 2][...] = y
        if also16:
            y16 = y.astype(BF16)
            refs[n_in + 3][...] = y16
            refs[n_in + 4][...] = y16.T

    row = pl.BlockSpec((tr, dm), lambda i: (i, 0))
    vec = pl.BlockSpec((1, dm), lambda i: (0, 0))
    args = (x, res) if has_res else (x,)
    out_specs, out_shape = [row], [jax.ShapeDtypeStruct((rows, dm), F32)]
    if also16:
        out_specs += [row, pl.BlockSpec((dm, tr), lambda i: (0, i))]
        out_shape += [jax.ShapeDtypeStruct((rows, dm), BF16), jax.ShapeDtypeStruct((dm, rows), BF16)]
    outs = pl.pallas_call(
        body, name=name, grid=(rows // tr,),
        in_specs=[row] * n_in + [vec, vec], out_specs=out_specs, out_shape=out_shape,
        compiler_params=_cp(("parallel",)),
    )(*args, g.reshape(1, dm), b.reshape(1, dm))
    return outs if also16 else outs[0]


def _ln_bwd_call(dy, x, res, g, name, dy2=None):
    rows, dm = x.shape
    tr = _row_tile(rows)
    has_res = res is not None
    two = dy2 is not None

    def body(*refs):
        dy_ref, refs = refs[0], refs[1:]
        if two:
            dy2_ref, refs = refs[0], refs[1:]
        if has_res:
            x_ref, r_ref, g_ref, dx_ref, dr_ref, dg_ref, db_ref = refs
            u = DEEPNORM_ALPHA * r_ref[...] + x_ref[...]
        else:
            x_ref, g_ref, dx_ref, dg_ref, db_ref = refs
            u = x_ref[...]
        i = pl.program_id(0)
        d, rstd = _ln_stats(u)
        xhat = d * rstd
        dyv = dy_ref[...] + dy2_ref[...] if two else dy_ref[...]
        dxh = dyv * g_ref[...]
        m1 = jnp.mean(dxh, axis=-1, keepdims=True)
        m2 = jnp.mean(dxh * xhat, axis=-1, keepdims=True)
        du = rstd * (dxh - m1 - xhat * m2)
        dx_ref[...] = du
        if has_res:
            dr_ref[...] = DEEPNORM_ALPHA * du
        pg = jnp.sum(dyv * xhat, axis=0, keepdims=True)
        pb = jnp.sum(dyv, axis=0, keepdims=True)

        @pl.when(i == 0)
        def _():
            dg_ref[...] = pg
            db_ref[...] = pb

        @pl.when(i > 0)
        def _():
            dg_ref[...] += pg
            db_ref[...] += pb

    row = pl.BlockSpec((tr, dm), lambda i: (i, 0))
    vec = pl.BlockSpec((1, dm), lambda i: (0, 0))
    big = jax.ShapeDtypeStruct((rows, dm), F32)
    small = jax.ShapeDtypeStruct((1, dm), F32)
    args = ((dy, dy2) if two else (dy,)) + ((x, res) if has_res else (x,))
    n_big = 2 if has_res else 1
    outs = pl.pallas_call(
        body, name=name, grid=(rows // tr,),
        in_specs=[row] * len(args) + [vec],
        out_specs=[row] * n_big + [vec, vec],
        out_shape=[big] * n_big + [small, small],
        compiler_params=_cp(("arbitrary",)),
    )(*args, g.reshape(1, dm))
    return outs


def _make_ln(name, has_res):
    if has_res:
        @jax.custom_vjp
        def ln(x, res, g, b):
            return _ln_fwd_call(x, res, g, b, name + "_fwd")

        def fwd(x, res, g, b):
            return ln(x, res, g, b), (x, res, g)

        def bwd(saved, dy):
            x, res, g = saved
            dx, dr, dg, db = _ln_bwd_call(dy, x, res, g, name + "_bwd")
            return dx, dr, dg.reshape(-1), db.reshape(-1)
    else:
        @jax.custom_vjp
        def ln(x, g, b):
            return _ln_fwd_call(x, None, g, b, name + "_fwd")

        def fwd(x, g, b):
            return ln(x, g, b), (x, g)

        def bwd(saved, dy):
            x, g = saved
            dx, dg, db = _ln_bwd_call(dy, x, None, g, name + "_bwd")
            return dx, dg.reshape(-1), db.reshape(-1)

    ln.defvjp(fwd, bwd)
    return ln


def _rms_fwd_call(x, g, name):
    rows, dm = x.shape
    tr = _row_tile(rows)

    def body(x_ref, g_ref, o_ref):
        xv = x_ref[...]
        rstd = lax.rsqrt(jnp.mean(xv * xv, axis=-1, keepdims=True) + RMS_EPS)
        o_ref[...] = xv * rstd * g_ref[...]

    row = pl.BlockSpec((tr, dm), lambda i: (i, 0))
    vec = pl.BlockSpec((1, dm), lambda i: (0, 0))
    return pl.pallas_call(
        body, name=name, grid=(rows // tr,), in_specs=[row, vec], out_specs=row,
        out_shape=jax.ShapeDtypeStruct((rows, dm), F32), compiler_params=_cp(("parallel",)),
    )(x, g.reshape(1, dm))


def _rms_bwd_call(dy, x, g, name):
    rows, dm = x.shape
    tr = _row_tile(rows)

    def body(dy_ref, x_ref, g_ref, dx_ref, dg_ref):
        i = pl.program_id(0)
        xv = x_ref[...]
        dyv = dy_ref[...]
        rstd = lax.rsqrt(jnp.mean(xv * xv, axis=-1, keepdims=True) + RMS_EPS)
        xhat = xv * rstd
        dxh = dyv * g_ref[...]
        m2 = jnp.mean(dxh * xhat, axis=-1, keepdims=True)
        dx_ref[...] = rstd * (dxh - xhat * m2)
        pg = jnp.sum(dyv * xhat, axis=0, keepdims=True)

        @pl.when(i == 0)
        def _():
            dg_ref[...] = pg

        @pl.when(i > 0)
        def _():
            dg_ref[...] += pg

    row = pl.BlockSpec((tr, dm), lambda i: (i, 0))
    vec = pl.BlockSpec((1, dm), lambda i: (0, 0))
    return pl.pallas_call(
        body, name=name, grid=(rows // tr,), in_specs=[row, row, vec], out_specs=[row, vec],
        out_shape=[jax.ShapeDtypeStruct((rows, dm), F32), jax.ShapeDtypeStruct((1, dm), F32)],
        compiler_params=_cp(("arbitrary",)),
    )(dy, x, g.reshape(1, dm))


def _make_rms(name):
    @jax.custom_vjp
    def rms(x, g):
        return _rms_fwd_call(x, g, name + "_fwd")

    def fwd(x, g):
        return rms(x, g), (x, g)

    def bwd(saved, dy):
        x, g = saved
        dx, dg = _rms_bwd_call(dy, x, g, name + "_bwd")
        return dx, dg.reshape(-1)

    rms.defvjp(fwd, bwd)
    return rms


def _sigmoid(x):
    return 1.0 / (1.0 + jnp.exp(-x))


def _gate_fwd_call(mixed, gate, name):
    rows, dm = mixed.shape
    tr = _row_tile(rows)

    def body(m_ref, g_ref, o_ref, oT_ref):
        gv = g_ref[...]
        y16 = (m_ref[...] * (gv * _sigmoid(gv))).astype(BF16)
        o_ref[...] = y16
        oT_ref[...] = y16.T

    row = pl.BlockSpec((tr, dm), lambda i: (i, 0))
    return pl.pallas_call(
        body, name=name, grid=(rows // tr,), in_specs=[row, row],
        out_specs=[row, pl.BlockSpec((dm, tr), lambda i: (0, i))],
        out_shape=[jax.ShapeDtypeStruct((rows, dm), BF16), jax.ShapeDtypeStruct((dm, rows), BF16)],
        compiler_params=_cp(("parallel",)),
    )(mixed, gate)


def _gate_bwd_call(dy, mixed, gate, name):
    rows, dm = mixed.shape
    tr = _row_tile(rows)

    def body(dy_ref, m_ref, g_ref, dm_ref, dg_ref):
        gv = g_ref[...]
        dyv = dy_ref[...]
        sg = _sigmoid(gv)
        dm_ref[...] = dyv * (gv * sg)
        dg_ref[...] = dyv * m_ref[...] * (sg * (1.0 + gv * (1.0 - sg)))

    row = pl.BlockSpec((tr, dm), lambda i: (i, 0))
    out = jax.ShapeDtypeStruct((rows, dm), F32)
    return pl.pallas_call(
        body, name=name, grid=(rows // tr,), in_specs=[row, row, row], out_specs=[row, row],
        out_shape=[out, out], compiler_params=_cp(("parallel",)),
    )(dy, mixed, gate)


def _make_gate_out(name):
    def run_fwd(mixed, gate, w):
        g16, gT16 = _gate_fwd_call(mixed, gate, name + "_gate_fwd")
        w16 = w.astype(BF16)
        return _matmul(g16, w16, "nn", name + "_fwd"), (mixed, gate, gT16, w16)

    def run_bwd(saved, dy):
        mixed, gate, gT16, w16 = saved
        dy16 = dy.astype(BF16)
        dgated = _matmul(dy16, w16, "nt", name + "_dx")
        dmix, dgate = _gate_bwd_call(dgated, mixed, gate, name + "_gate_bwd")
        return dmix, dgate, _matmul(gT16, dy16, "nn", name + "_dw")

    @jax.custom_vjp
    def gate_out(mixed, gate, w):
        return run_fwd(mixed, gate, w)[0]

    gate_out.defvjp(run_fwd, run_bwd)
    return gate_out


def _loss_call(y, t, name):
    rows, dm = y.shape
    tr = _row_tile(rows)

    def body(y_ref, t_ref, l_ref, d_ref):
        i = pl.program_id(0)
        e = y_ref[...] - t_ref[...]
        d_ref[...] = e * (1.0 / dm)
        part = 0.5 * jnp.sum(jnp.mean(e * e, axis=-1, keepdims=True), axis=0, keepdims=True)

        @pl.when(i == 0)
        def _():
            l_ref[...] = part

        @pl.when(i > 0)
        def _():
            l_ref[...] += part

    row = pl.BlockSpec((tr, dm), lambda i: (i, 0))
    one = pl.BlockSpec((1, 1), lambda i: (0, 0))
    return pl.pallas_call(
        body, name=name, grid=(rows // tr,), in_specs=[row, row], out_specs=[one, row],
        out_shape=[jax.ShapeDtypeStruct((1, 1), F32), jax.ShapeDtypeStruct((rows, dm), F32)],
        compiler_params=_cp(("arbitrary",)),
    )(y, t)


@jax.custom_vjp
def _loss_op(y, t):
    return _loss_call(y, t, "loss_head")[0][0, 0]


def _loss_fwd(y, t):
    l, d = _loss_call(y, t, "loss_head")
    return l[0, 0], d


def _loss_bwd(d, ct):
    return ct * d, jnp.zeros_like(d)


_loss_op.defvjp(_loss_fwd, _loss_bwd)


def _attn_blocks(S, Sk):
    bq, bk = min(ATTN_BQ, S), min(ATTN_BK, Sk)
    assert S % bq == 0 and Sk % bk == 0
    return bq, bk


def _valid_t(i, j, bq, bk, strict):
    key = j * bk + lax.broadcasted_iota(jnp.int32, (bk, bq), 0)
    qry = i * bq + lax.broadcasted_iota(jnp.int32, (bk, bq), 1)
    return (key < qry) if strict else (key <= qry)


def _sm_fwd_t(qT, k, vT, cmul, causal, name):
    H, DK, S = qT.shape
    Sk, dv = k.shape[1], vT.shape[1]
    bq, bk = _attn_blocks(S, Sk)
    nq, nkb = S // bq, Sk // bk
    assert H % PAIR == 0
    heads = range(PAIR)
    if causal:
        assert S == Sk and bq == bk

    def body(qT_ref, k_ref, vT_ref, oT_ref, lse_ref):
        i = pl.program_id(1)
        qTs = [qT_ref[w] for w in heads]

        def blk(j, carry, masked):
            off = pl.multiple_of(j * bk, bk)
            sT = [_dot(k_ref[w, pl.ds(off, bk), :], qTs[w]) * cmul for w in heads]
            if masked:
                valid = _valid_t(i, j, bq, bk, False)
                sT = [jnp.where(valid, s, NEG_BIG) for s in sT]
            m_new = [jnp.maximum(carry[w][0], jnp.max(sT[w], axis=0, keepdims=True)) for w in heads]
            p = [jnp.exp2(sT[w] - m_new[w]) for w in heads]
            a = [jnp.exp2(carry[w][0] - m_new[w]) for w in heads]
            l = [a[w] * carry[w][1] + jnp.sum(p[w], axis=0, keepdims=True) for w in heads]
            acc = [a[w] * carry[w][2] + _dot(vT_ref[w, :, pl.ds(off, bk)], p[w].astype(BF16)) for w in heads]
            return tuple((m_new[w], l[w], acc[w]) for w in heads)

        carry = tuple((jnp.full((1, bq), NEG_BIG, F32), jnp.zeros((1, bq), F32), jnp.zeros((dv, bq), F32))
                      for _ in heads)
        if causal:
            carry = lax.fori_loop(0, i, lambda j, c: blk(j, c, False), carry)
            carry = blk(i, carry, True)
        else:
            carry = lax.fori_loop(0, nkb, lambda j, c: blk(j, c, False), carry)
        for w in heads:
            oT_ref[w] = carry[w][2] / carry[w][1]
            lse_ref[w] = carry[w][0] + jnp.log2(carry[w][1])

    qcol = lambda d: pl.BlockSpec((PAIR, d, bq), lambda h, i: (h, 0, i))
    return pl.pallas_call(
        body, name=name, grid=(H // PAIR, nq),
        in_specs=[qcol(DK), pl.BlockSpec((PAIR, Sk, DK), lambda h, i: (h, 0, 0)),
                  pl.BlockSpec((PAIR, dv, Sk), lambda h, i: (h, 0, 0))],
        out_specs=[qcol(dv), qcol(1)],
        out_shape=[jax.ShapeDtypeStruct((H, dv, S), F32), jax.ShapeDtypeStruct((H, 1, S), F32)],
        compiler_params=_cp(("parallel", "arbitrary"), ATTN_VMEM_LIMIT),
    )(qT, k, vT)


def _sm_bwd_t(qT, qn, k, kT, v, oT, lse, doT, do, cmul, gscale, causal, name):
    H, DK, S = qT.shape
    Sk, dv = k.shape[1], v.shape[2]
    bq, bk = _attn_blocks(S, Sk)
    nq, nkb = S // bq, Sk // bk

    def body(qT_ref, qn_ref, k_ref, kT_ref, v_ref, oT_ref, lse_ref, doT_ref, do_ref, dqT_ref, dk_ref, dv_ref):
        i = pl.program_id(1)

        @pl.when(i == 0)
        def _():
            dk_ref[...] = jnp.zeros_like(dk_ref)
            dv_ref[...] = jnp.zeros_like(dv_ref)

        qTb = qT_ref[...]
        qnb = qn_ref[...]
        dob = do_ref[...]
        doTf = doT_ref[...]
        doTb = doTf.astype(BF16)
        delta = jnp.sum(doTf * oT_ref[...], axis=0, keepdims=True)
        lse = lse_ref[...]

        def blk(j, dq, masked):
            off = pl.multiple_of(j * bk, bk)
            sT = _dot(k_ref[pl.ds(off, bk), :], qTb) * cmul
            if masked:
                sT = jnp.where(_valid_t(i, j, bq, bk, False), sT, NEG_BIG)
            p = jnp.exp2(sT - lse)
            dp = _dot(v_ref[pl.ds(off, bk), :], doTb)
            ds = p * (dp - delta)
            dsb = (ds * gscale).astype(BF16) if gscale != 1.0 else ds.astype(BF16)
            dv_ref[pl.ds(off, bk), :] += _dot(p.astype(BF16), dob)
            dk_ref[pl.ds(off, bk), :] += _dot(dsb, qnb)
            return dq + _dot(kT_ref[:, pl.ds(off, bk)], dsb)

        dq = jnp.zeros((DK, bq), F32)
        if causal:
            dq = lax.fori_loop(0, i, lambda j, c: blk(j, c, False), dq)
            dq = blk(i, dq, True)
        else:
            dq = lax.fori_loop(0, nkb, lambda j, c: blk(j, c, False), dq)
        dqT_ref[...] = dq

    qcol = lambda d: pl.BlockSpec((None, d, bq), lambda h, i: (h, 0, i))
    qrow = lambda d: pl.BlockSpec((None, bq, d), lambda h, i: (h, i, 0))
    krow = lambda d: pl.BlockSpec((None, Sk, d), lambda h, i: (h, 0, 0))
    return pl.pallas_call(
        body, name=name, grid=(H, nq),
        in_specs=[qcol(DK), qrow(DK), krow(DK), pl.BlockSpec((None, DK, Sk), lambda h, i: (h, 0, 0)), krow(dv),
                  qcol(dv), qcol(1), qcol(dv), qrow(dv)],
        out_specs=[qcol(DK), krow(DK), krow(dv)],
        out_shape=[jax.ShapeDtypeStruct((H, DK, S), F32), jax.ShapeDtypeStruct((H, Sk, DK), F32),
                   jax.ShapeDtypeStruct((H, Sk, dv), F32)],
        compiler_params=_cp(("parallel", "arbitrary"), ATTN_VMEM_LIMIT),
    )(qT, qn, k, kT, v, oT, lse, doT, do)


def _tri(n, fn):
    r = lax.broadcasted_iota(jnp.int32, (n, n), 0)
    c = lax.broadcasted_iota(jnp.int32, (n, n), 1)
    return jnp.where(fn(r, c), 1.0, 0.0).astype(BF16)


def _key_cumsum(x, tri2, suffix, base):
    bk = x.shape[0]
    c = min(CUMSUM_CHUNK, bk)
    n = bk // c
    hi32 = lax.bitcast_convert_type(lax.bitcast_convert_type(x, jnp.int32) & jnp.int32(-65536), F32)
    hi = hi32.astype(BF16)
    lo = (x - hi32).astype(BF16)
    tot = [jnp.sum(x[a * c:(a + 1) * c], axis=0, keepdims=True) for a in range(n)]
    outs = []
    for a in range(n):
        row = base
        for t in (tot[a + 1:] if suffix else tot[:a]):
            row = row + t
        stacked = jnp.concatenate([hi[a * c:(a + 1) * c], lo[a * c:(a + 1) * c]], axis=0)
        outs.append(_dot(tri2, stacked) + row)
    total = tot[0]
    for t in tot[1:]:
        total = total + t
    return (outs[0] if n == 1 else jnp.concatenate(outs, axis=0)), total


def _tri2(n, fn):
    t = _tri(n, fn)
    return jnp.concatenate([t, t], axis=1)


def _sb_logs(z):
    neg_abs = lax.bitcast_convert_type(lax.bitcast_convert_type(z, jnp.int32) | jnp.int32(-2 ** 31), F32)
    ls = jnp.minimum(z, 0.0) - jnp.log(1.0 + jnp.exp(neg_abs))
    return ls, ls - z


PAIR = LANE // HEAD_DIM


def _head_lanes(shape, w, axis):
    idx = lax.broadcasted_iota(jnp.int32, shape, axis)
    return (idx >= HEAD_DIM * w) & (idx < HEAD_DIM * (w + 1))


def _bias_rows(w, bq):
    row = lax.broadcasted_iota(jnp.int32, (LANE, bq), 0)
    return jnp.where((row >= 3 * w) & (row < 3 * w + 3), -1.0, 0.0).astype(BF16)


def _merge_pair(parts):
    return jnp.where(_head_lanes(parts[0].shape, 0, 0), parts[0], parts[1]).T


def _smp_fwd(q2, k2, vT2, bias, r, causal, name):
    S, C = q2.shape
    Sk = k2.shape[0]
    bq, bk = _attn_blocks(S, Sk)
    nq, nkb, P = S // bq, Sk // bk, C // LANE
    use_f = bias is not None
    if causal:
        assert S == Sk and bq == bk

    def body(*refs):
        if use_f:
            q_ref, k_ref, vT_ref, b_ref, r_ref, o_ref, lse_ref = refs
        else:
            q_ref, k_ref, vT_ref, o_ref, lse_ref = refs
        i = pl.program_id(1)
        qp = q_ref[...]
        heads = range(PAIR)
        qTs = [jnp.where(_head_lanes(qp.shape, w, 1), qp, jnp.zeros_like(qp)).T for w in heads]
        if use_f:
            qTs = [jnp.concatenate([qTs[w], _bias_rows(w, bq)], axis=0) for w in heads]

        def blk(j, carry, masked):
            off = pl.multiple_of(j * bk, bk)
            kb = k_ref[pl.ds(off, bk), :]
            if use_f:
                kb = jnp.concatenate([kb, b_ref[pl.ds(off, bk), :]], axis=1)
            vTb = vT_ref[:, pl.ds(off, bk)]
            sT = [_dot(kb, qTs[w]) * LOG2E for w in heads]
            if masked:
                valid = _valid_t(i, j, bq, bk, False)
                sT = [jnp.where(valid, s, NEG_BIG) for s in sT]
            cm = [jnp.max(s, axis=0, keepdims=True) for s in sT]
            if use_f:
                cm = [cm[w] + r_ref[w] for w in heads]
            m_new = [jnp.maximum(carry[w][0], cm[w]) for w in heads]
            shift = [(m_new[w] - r_ref[w]) if use_f else m_new[w] for w in heads]
            p = [jnp.exp2(sT[w] - shift[w]) for w in heads]
            a = [jnp.exp2(carry[w][0] - m_new[w]) for w in heads]
            l = [a[w] * carry[w][1] + jnp.sum(p[w], axis=0, keepdims=True) for w in heads]
            acc = [a[w] * carry[w][2] + _dot(vTb, p[w].astype(BF16)) for w in heads]
            return tuple((m_new[w], l[w], acc[w]) for w in heads)

        carry = tuple((jnp.full((1, bq), NEG_BIG, F32), jnp.zeros((1, bq), F32), jnp.zeros((LANE, bq), F32))
                      for _ in heads)
        if causal:
            carry = lax.fori_loop(0, i, lambda j, c: blk(j, c, False), carry)
            carry = blk(i, carry, True)
        else:
            carry = lax.fori_loop(0, nkb, lambda j, c: blk(j, c, False), carry)
        for w in heads:
            lse_ref[w] = carry[w][0] + jnp.log2(carry[w][1])
        o_ref[...] = _merge_pair([carry[w][2] / carry[w][1] for w in heads])

    qblk = pl.BlockSpec((bq, LANE), lambda p, i: (i, p))
    kres = pl.BlockSpec((Sk, LANE), lambda p, i: (0, p))
    stat = pl.BlockSpec((PAIR, 1, bq), lambda p, i: (p, 0, i))
    in_specs = [qblk, kres, pl.BlockSpec((LANE, Sk), lambda p, i: (p, 0))]
    args = [q2, k2, vT2]
    if use_f:
        in_specs += [kres, stat]
        args += [bias, r]
    return pl.pallas_call(
        body, name=name, grid=(P, nq), in_specs=in_specs, out_specs=[qblk, stat],
        out_shape=[jax.ShapeDtypeStruct((S, C), F32), jax.ShapeDtypeStruct((PAIR * P, 1, S), F32)],
        compiler_params=_cp(("parallel", "arbitrary"), ATTN_VMEM_LIMIT),
    )(*args)


def _smp_bwd(q2, k2, kT2, v2, o2, lse, do2, bias, r, scale, causal, name):
    S, C = q2.shape
    Sk = k2.shape[0]
    bq, bk = _attn_blocks(S, Sk)
    nq, nkb, P = S // bq, Sk // bk, C // LANE
    use_f = bias is not None

    def body(*refs):
        if use_f:
            (q_ref, k_ref, kT_ref, v_ref, o_ref, lse_ref, do_ref, b_ref, r_ref,
             dq_ref, dk_ref, dv_ref, dr_ref, dkey_ref, db_ref) = refs
        else:
            q_ref, k_ref, kT_ref, v_ref, o_ref, lse_ref, do_ref, dq_ref, dk_ref, dv_ref = refs
        i = pl.program_id(1)

        @pl.when(i == 0)
        def _():
            dk_ref[...] = jnp.zeros_like(dk_ref)
            dv_ref[...] = jnp.zeros_like(dv_ref)
            if use_f:
                db_ref[...] = jnp.zeros_like(db_ref)

        qp = q_ref[...]
        dof = do_ref[...]
        prod = dof * o_ref[...]
        heads = range(PAIR)
        mine = [_head_lanes(qp.shape, w, 1) for w in heads]
        qz = [jnp.where(mine[w], qp, jnp.zeros_like(qp)) for w in heads]
        qTs = [qz[w].T for w in heads]
        if use_f:
            qTs = [jnp.concatenate([qTs[w], _bias_rows(w, bq)], axis=0) for w in heads]
        doz = [jnp.where(mine[w], dof, 0.0).astype(BF16) for w in heads]
        doT = [doz[w].T for w in heads]
        delta = [jnp.sum(jnp.where(mine[w], prod, 0.0).T, axis=0, keepdims=True) for w in heads]
        shift = [(lse_ref[w] - r_ref[w]) if use_f else lse_ref[w] for w in heads]

        def blk(j, carry, masked):
            off = pl.multiple_of(j * bk, bk)
            kb = k_ref[pl.ds(off, bk), :]
            if use_f:
                kb = jnp.concatenate([kb, b_ref[pl.ds(off, bk), :]], axis=1)
            vb = v_ref[pl.ds(off, bk), :]
            kTb = kT_ref[:, pl.ds(off, bk)]
            sT = [_dot(kb, qTs[w]) * LOG2E for w in heads]
            if masked:
                valid = _valid_t(i, j, bq, bk, False)
                sT = [jnp.where(valid, s, NEG_BIG) for s in sT]
            p = [jnp.exp2(sT[w] - shift[w]) for w in heads]
            dp = [_dot(vb, doT[w]) for w in heads]
            ds = [p[w] * (dp[w] - delta[w]) for w in heads]
            dsb = [d.astype(BF16) for d in ds]
            dvs = [_dot(p[w].astype(BF16), doz[w]) for w in heads]
            dks = [_dot(dsb[w], qz[w]) for w in heads]
            dv_ref[pl.ds(off, bk), :] += dvs[0] + dvs[1]
            dk_ref[pl.ds(off, bk), :] += dks[0] + dks[1]
            dr = [carry[w][1] for w in heads]
            if use_f:
                dr = [dr[w] + jnp.sum(ds[w], axis=0, keepdims=True) for w in heads]
                lane = lax.broadcasted_iota(jnp.int32, (bk, LANE), 1)
                cols = [jnp.where(lane == w, jnp.sum(ds[w], axis=1, keepdims=True), 0.0) for w in heads]
                db_ref[pl.ds(off, bk), :] += cols[0] + cols[1]
            dq = [carry[w][0] + _dot(kTb, dsb[w]) for w in heads]
            return tuple((dq[w], dr[w]) for w in heads)

        carry = tuple((jnp.zeros((LANE, bq), F32), jnp.zeros((1, bq), F32)) for _ in heads)
        if causal:
            carry = lax.fori_loop(0, i, lambda j, c: blk(j, c, False), carry)
            carry = blk(i, carry, True)
        else:
            carry = lax.fori_loop(0, nkb, lambda j, c: blk(j, c, False), carry)
        if use_f:
            for w in heads:
                dr_ref[w] = carry[w][1]
        dq_ref[...] = _merge_pair([carry[w][0] for w in heads]) * scale

        if use_f:
            @pl.when(i == nq - 1)
            def _():
                def chunk(cidx, carry):
                    off = pl.multiple_of(cidx * LANE, LANE)
                    t = db_ref[pl.ds(off, LANE), :].T
                    for w in range(PAIR):
                        dkey_ref[w, :, pl.ds(off, LANE)] = t[w:w + 1, :]
                    return carry

                lax.fori_loop(0, Sk // LANE, chunk, 0)

    qblk = pl.BlockSpec((bq, LANE), lambda p, i: (i, p))
    kres = pl.BlockSpec((Sk, LANE), lambda p, i: (0, p))
    stat = pl.BlockSpec((PAIR, 1, bq), lambda p, i: (p, 0, i))
    in_specs = [qblk, kres, pl.BlockSpec((LANE, Sk), lambda p, i: (p, 0)), kres, qblk, stat, qblk]
    args = [q2, k2, kT2, v2, o2, lse, do2]
    out_specs = [qblk, kres, kres]
    out_shape = [jax.ShapeDtypeStruct((S, C), F32), jax.ShapeDtypeStruct((Sk, C), F32),
                 jax.ShapeDtypeStruct((Sk, C), F32)]
    scratch = []
    if use_f:
        in_specs += [kres, stat]
        args += [bias, r]
        out_specs += [stat, pl.BlockSpec((PAIR, 1, Sk), lambda p, i: (p, 0, 0))]
        out_shape += [jax.ShapeDtypeStruct((PAIR * P, 1, S), F32), jax.ShapeDtypeStruct((PAIR * P, 1, Sk), F32)]
        scratch = [pltpu.VMEM((Sk, LANE), F32)]
    return pl.pallas_call(
        body, name=name, grid=(P, nq), in_specs=in_specs, out_specs=out_specs, out_shape=out_shape,
        scratch_shapes=scratch, compiler_params=_cp(("parallel", "arbitrary"), ATTN_VMEM_LIMIT),
    )(*args)


def _sbp_fwd(q2, k2, vT2, name):
    S, C = q2.shape
    bq, bk = _attn_blocks(S, S)
    assert bq == bk
    nq, P = S // bq, C // LANE
    c = min(CUMSUM_CHUNK, bk)

    def body(q_ref, k_ref, vT_ref, o_ref, lt_ref):
        i = pl.program_id(1)
        qp = q_ref[...]
        after = _tri2(c, lambda s, j: j > s)
        heads = range(PAIR)
        qTs = [jnp.where(_head_lanes(qp.shape, w, 1), qp, jnp.zeros_like(qp)).T for w in heads]

        def blk(jj, carry, masked):
            j = i - jj
            off = pl.multiple_of(j * bk, bk)
            kb = k_ref[pl.ds(off, bk), :]
            vTb = vT_ref[:, pl.ds(off, bk)]
            logs = [_sb_logs(_dot(kb, qTs[w])) for w in heads]
            ls, lk = [t[0] for t in logs], [t[1] for t in logs]
            if masked:
                valid = _valid_t(i, j, bq, bk, True)
                lk = [jnp.where(valid, t, 0.0) for t in lk]
            cs = [_key_cumsum(lk[w], after, True, carry[w][0]) for w in heads]
            wgt = [jnp.exp(ls[w] + cs[w][0]) for w in heads]
            if masked:
                wgt = [jnp.where(valid, t, 0.0) for t in wgt]
            acc = [carry[w][1] + _dot(vTb, wgt[w].astype(BF16)) for w in heads]
            return tuple((carry[w][0] + cs[w][1], acc[w]) for w in heads)

        carry = tuple((jnp.zeros((1, bq), F32), jnp.zeros((LANE, bq), F32)) for _ in heads)
        carry = blk(0, carry, True)
        carry = lax.fori_loop(1, i + 1, lambda jj, cr: blk(jj, cr, False), carry)
        for w in heads:
            lt_ref[w] = carry[w][0]
        o_ref[...] = _merge_pair([carry[w][1] for w in heads])

    qblk = pl.BlockSpec((bq, LANE), lambda p, i: (i, p))
    stat = pl.BlockSpec((PAIR, 1, bq), lambda p, i: (p, 0, i))
    return pl.pallas_call(
        body, name=name, grid=(P, nq),
        in_specs=[qblk, pl.BlockSpec((S, LANE), lambda p, i: (0, p)), pl.BlockSpec((LANE, S), lambda p, i: (p, 0))],
        out_specs=[qblk, stat],
        out_shape=[jax.ShapeDtypeStruct((S, C), F32), jax.ShapeDtypeStruct((PAIR * P, 1, S), F32)],
        compiler_params=_cp(("parallel", "arbitrary"), ATTN_VMEM_LIMIT),
    )(q2, k2, vT2)


def _sbp_bwd(q2, k2, kT2, v2, lt, do2, scale, name):
    S, C = q2.shape
    bq, bk = _attn_blocks(S, S)
    nq, P = S // bq, C // LANE
    c = min(CUMSUM_CHUNK, bk)

    def body(q_ref, k_ref, kT_ref, v_ref, lt_ref, do_ref, dq_ref, dk_ref, dv_ref):
        i = pl.program_id(1)

        @pl.when(i == 0)
        def _():
            dk_ref[...] = jnp.zeros_like(dk_ref)
            dv_ref[...] = jnp.zeros_like(dv_ref)

        qp = q_ref[...]
        dof = do_ref[...]
        upto = _tri2(c, lambda s, j: j <= s)
        before = _tri2(c, lambda s, j: j < s)
        heads = range(PAIR)
        mine = [_head_lanes(qp.shape, w, 1) for w in heads]
        qz = [jnp.where(mine[w], qp, jnp.zeros_like(qp)) for w in heads]
        qTs = [qz[w].T for w in heads]
        doz = [jnp.where(mine[w], dof, 0.0).astype(BF16) for w in heads]
        doT = [doz[w].T for w in heads]
        ltot = [lt_ref[w] for w in heads]

        def blk(j, carry, masked):
            off = pl.multiple_of(j * bk, bk)
            kb = k_ref[pl.ds(off, bk), :]
            vb = v_ref[pl.ds(off, bk), :]
            kTb = kT_ref[:, pl.ds(off, bk)]
            logs = [_sb_logs(_dot(kb, qTs[w])) for w in heads]
            ls, lk = [t[0] for t in logs], [t[1] for t in logs]
            if masked:
                valid = _valid_t(i, j, bq, bk, True)
                lk = [jnp.where(valid, t, 0.0) for t in lk]
            pin = [_key_cumsum(lk[w], upto, False, carry[w][1] - ltot[w]) for w in heads]
            wgt = [jnp.exp(ls[w] - pin[w][0]) for w in heads]
            if masked:
                wgt = [jnp.where(valid, t, 0.0) for t in wgt]
            g = [_dot(vb, doT[w]) * wgt[w] for w in heads]
            cin = [_key_cumsum(g[w], before, False, carry[w][2]) for w in heads]
            sig = [jnp.exp(t) for t in ls]
            dz = [g[w] * (1.0 - sig[w]) - cin[w][0] * sig[w] for w in heads]
            if masked:
                dz = [jnp.where(valid, t, 0.0) for t in dz]
            dzb = [t.astype(BF16) for t in dz]
            dvs = [_dot(wgt[w].astype(BF16), doz[w]) for w in heads]
            dks = [_dot(dzb[w], qz[w]) for w in heads]
            dv_ref[pl.ds(off, bk), :] += dvs[0] + dvs[1]
            dk_ref[pl.ds(off, bk), :] += dks[0] + dks[1]
            return tuple((carry[w][0] + _dot(kTb, dzb[w]), carry[w][1] + pin[w][1], carry[w][2] + cin[w][1])
                         for w in heads)

        carry = tuple((jnp.zeros((LANE, bq), F32), jnp.zeros((1, bq), F32), jnp.zeros((1, bq), F32)) for _ in heads)
        carry = lax.fori_loop(0, i, lambda j, cr: blk(j, cr, False), carry)
        carry = blk(i, carry, True)
        dq_ref[...] = _merge_pair([carry[w][0] for w in heads]) * scale

    qblk = pl.BlockSpec((bq, LANE), lambda p, i: (i, p))
    kres = pl.BlockSpec((S, LANE), lambda p, i: (0, p))
    stat = pl.BlockSpec((PAIR, 1, bq), lambda p, i: (p, 0, i))
    return pl.pallas_call(
        body, name=name, grid=(P, nq),
        in_specs=[qblk, kres, pl.BlockSpec((LANE, S), lambda p, i: (p, 0)), kres, stat, qblk],
        out_specs=[qblk, kres, kres],
        out_shape=[jax.ShapeDtypeStruct((S, C), F32)] * 3,
        compiler_params=_cp(("parallel", "arbitrary"), ATTN_VMEM_LIMIT),
    )(q2, k2, kT2, v2, lt, do2)


def _bias_cols(f_cum):
    H, Sk = f_cum.shape
    terms = jnp.stack(_split3(f_cum), axis=-1)
    packed = terms.reshape(H // PAIR, PAIR, Sk, 3).transpose(2, 0, 1, 3).reshape(Sk, H // PAIR, PAIR * 3)
    return jnp.pad(packed, ((0, 0), (0, 0), (0, LANE - PAIR * 3))).reshape(Sk, -1)


def _make_packed_softmax(name, scale, causal, use_f):
    assert _pow2(scale)

    def run_fwd(q16, k16, v16, f_cum):
        q16 = q16 * scale
        bias = _bias_cols(f_cum) if use_f else None
        r = (f_cum * LOG2E)[:, None, :] if use_f else None
        o, lse = _smp_fwd(q16, k16, v16.T, bias, r, causal, name + "_fwd")
        return o, (q16, k16, v16, o, lse, bias, r)

    def run_bwd(saved, do):
        q16, k16, v16, o, lse, bias, r = saved
        outs = _smp_bwd(q16, k16, k16.T, v16, o, lse, do, bias, r, scale, causal, name + "_bwd")
        dq, dk, dv = (t.astype(BF16) for t in outs[:3])
        if use_f:
            return dq, dk, dv, outs[3][:, 0, :] - outs[4][:, 0, :]
        return dq, dk, dv

    if use_f:
        @jax.custom_vjp
        def attn(q, k, v, f_cum):
            return run_fwd(q, k, v, f_cum)[0]

        attn.defvjp(run_fwd, run_bwd)
    else:
        @jax.custom_vjp
        def attn(q, k, v):
            return run_fwd(q, k, v, None)[0]

        attn.defvjp(lambda q, k, v: run_fwd(q, k, v, None), run_bwd)
    return attn


def _make_packed_sb(name, scale):
    assert _pow2(scale)

    def run_fwd(q16, k16, v16):
        q16 = q16 * scale
        o, lt = _sbp_fwd(q16, k16, v16.T, name + "_fwd")
        return o, (q16, k16, v16, lt)

    def run_bwd(saved, do):
        q16, k16, v16, lt = saved
        return tuple(t.astype(BF16) for t in _sbp_bwd(q16, k16, k16.T, v16, lt, do, scale, name + "_bwd"))

    @jax.custom_vjp
    def attn(q, k, v):
        return run_fwd(q, k, v)[0]

    attn.defvjp(run_fwd, run_bwd)
    return attn


def _round_bf16(x):
    return lax.reduce_precision(x, exponent_bits=8, mantissa_bits=7)


def _split3(x):
    hi = _round_bf16(x)
    mid = _round_bf16(x - hi)
    lo = _round_bf16(x - hi - mid)
    return hi.astype(BF16), mid.astype(BF16), lo.astype(BF16)


def _pow2(x):
    m, _ = math.frexp(x)
    return m == 0.5


def _pad_last(x, n):
    return jnp.pad(x, [(0, 0)] * (x.ndim - 1) + [(0, n - x.shape[-1])])


def _layouts(q, k, scale):
    qh = _pad_last(jnp.transpose(q * scale if _pow2(scale) else q, (1, 0, 2)).astype(BF16), LANE)
    kh = _pad_last(jnp.transpose(k, (1, 0, 2)).astype(BF16), LANE)
    return qh, jnp.transpose(qh, (0, 2, 1)), kh, jnp.transpose(kh, (0, 2, 1))


def _make_softmax_attn(name, scale, causal, d):
    pre = _pow2(scale)
    cmul = LOG2E if pre else scale * LOG2E
    gscale = 1.0 if pre else scale

    def run_fwd(q, k, v):
        qn, qT, kn, kT = _layouts(q, k, scale)
        vn = jnp.transpose(v, (1, 0, 2)).astype(BF16)
        oT, lse = _sm_fwd_t(qT, kn, jnp.transpose(vn, (0, 2, 1)), cmul, causal, name + "_fwd")
        return jnp.transpose(oT, (2, 0, 1)), (qn, qT, kn, kT, vn, oT, lse)

    def run_bwd(saved, dout):
        qn, qT, kn, kT, vn, oT, lse = saved
        doT = jnp.transpose(dout, (1, 2, 0))
        do = jnp.transpose(dout, (1, 0, 2)).astype(BF16)
        dqT, dk, dv = _sm_bwd_t(qT, qn, kn, kT, vn, oT, lse, doT, do, cmul, gscale, causal, name + "_bwd")
        dq = jnp.transpose(dqT[:, :d, :], (2, 0, 1))
        if pre:
            dq = dq * scale
        return dq, jnp.transpose(dk[:, :, :d], (1, 0, 2)), jnp.transpose(dv, (1, 0, 2))

    @jax.custom_vjp
    def attn(q, k, v):
        return run_fwd(q, k, v)[0]

    attn.defvjp(run_fwd, run_bwd)
    return attn


def _rope(x, positions):
    half = x.shape[-1] // 2
    inv_freq = ROPE_THETA ** (-jnp.arange(half, dtype=F32) / half)
    ang = positions.astype(F32)[:, None] * inv_freq[None, :]
    ang = ang.reshape((ang.shape[0],) + (1,) * (x.ndim - 2) + (half,))
    cos, sin = jnp.cos(ang), jnp.sin(ang)
    x1, x2 = x[..., :half], x[..., half:]
    return jnp.concatenate([x1 * cos - x2 * sin, x1 * sin + x2 * cos], axis=-1)


def _permute_w_in(w):
    parts = [w[:, _ORIG_OFF[idx]:_ORIG_OFF[idx] + SPLIT_SIZES[idx]] for _, idx in _PERM]
    pad = jnp.zeros((w.shape[0], PROJ_COLS - IN_COLS), w.dtype)
    return jnp.concatenate(parts + [pad], axis=1)


_BF16_PIECES = ("fq", "fk", "fv", "sq", "sk", "sv", "mq")


def _make_ln_proj(name, has_res):
    def split(proj32, proj16):
        out, off = [], 0
        for n, idx in _PERM:
            src = proj16 if n in _BF16_PIECES else proj32
            out.append(src[:, off:off + SPLIT_SIZES[idx]])
            off += SPLIT_SIZES[idx]
        return tuple(out)

    def run_fwd(x, res, g, b, w):
        h, h16, hT16 = _ln_fwd_call(x, res, g, b, name + "_ln_fwd", also16=True)
        w16 = w.astype(BF16)
        proj32, proj16 = _matmul(h16, w16, "nn", name + "_fwd", also16=True)
        return (h, split(proj32, proj16)), (x, res, g, hT16, w16)

    def run_bwd(saved, cts):
        x, res, g, hT16, w16 = saved
        dh, dpieces = cts
        pad = jnp.zeros((x.shape[0], PROJ_COLS - IN_COLS), BF16)
        dy16 = jnp.concatenate([c.astype(BF16) for c in dpieces] + [pad], axis=1)
        da = _matmul(dy16, w16, "nt", name + "_dx")
        dw = _matmul(hT16, dy16, "nn", name + "_dw")
        outs = _ln_bwd_call(dh, x, res, g, name + "_ln_bwd", dy2=da)
        if has_res:
            dx, dr, dg, db = outs
            return dx, dr, dg.reshape(-1), db.reshape(-1), dw
        dx, dg, db = outs
        return dx, dg.reshape(-1), db.reshape(-1), dw

    if has_res:
        @jax.custom_vjp
        def op(x, res, g, b, w):
            return run_fwd(x, res, g, b, w)[0]

        op.defvjp(run_fwd, run_bwd)
    else:
        @jax.custom_vjp
        def op(x, g, b, w):
            return run_fwd(x, None, g, b, w)[0]

        op.defvjp(lambda x, g, b, w: run_fwd(x, None, g, b, w), run_bwd)

    def call(*args):
        h, pieces = op(*args)
        return h, {n: part for (n, _), part in zip(_PERM, pieces)}

    return call


def _trunk_loss(wts, x2d, mem2d, target2d):
    s = x2d.shape[0]
    positions = jnp.arange(s)
    head_scale = HEAD_DIM ** -0.5
    mla_scale = (MLA_NOPE + MLA_ROPE) ** -0.5

    mem_n = _make_ln("ln_mem", False)(mem2d, wts["mem_ln_g"], wts["mem_ln_b"])
    h, y = None, x2d
    for l in range(DEPTH):
        tag = f"l{l}_"
        w_p = _permute_w_in(wts["w_in"][l])
        if l == 0:
            h, p = _make_ln_proj(tag + "proj", False)(y, wts["ln_in_g"], wts["ln_in_b"], w_p)
        else:
            h, p = _make_ln_proj(tag + "proj", True)(y, h, wts["ln_g"][l - 1], wts["ln_b"][l - 1], w_p)

        log_f = jax.nn.log_sigmoid(p["f_logit"] + wts["b_forget"][l])
        f_cum = jnp.cumsum(log_f, axis=0).T
        out_fox = _make_packed_softmax(tag + "fox", head_scale, True, True)(p["fq"], p["fk"], p["fv"], f_cum)

        out_sb = _make_packed_sb(tag + "sb", head_scale)(p["sq"], p["sk"], p["sv"])

        cqn = _make_rms(tag + "rms_q")(p["c_q"], wts["mla_q_norm_g"][l])
        q_mla = _make_mm(tag + "q_up")(cqn, wts["w_mla_q_up"][l]).reshape(s, N_HEADS, MLA_NOPE + MLA_ROPE)
        ckvn = _make_rms(tag + "rms_kv")(p["c_kv"], wts["mla_kv_norm_g"][l])
        kv_mla = _make_mm(tag + "kv_up")(ckvn, wts["w_mla_kv_up"][l]).reshape(s, N_HEADS, MLA_NOPE + MLA_V)
        q_full = jnp.concatenate([q_mla[..., :MLA_NOPE], _rope(q_mla[..., MLA_NOPE:], positions)], axis=-1)
        k_rope = jnp.broadcast_to(_rope(p["k_rot"], positions)[:, None, :], (s, N_HEADS, MLA_ROPE))
        k_full = jnp.concatenate([kv_mla[..., :MLA_NOPE], k_rope], axis=-1)
        out_mla = _make_softmax_attn(tag + "mla", mla_scale, True, MLA_NOPE + MLA_ROPE)(
            q_full, k_full, kv_mla[..., MLA_NOPE:]).reshape(s, GROUP_W)

        mkv = _make_mm(tag + "mem_kv")(mem_n, wts["w_mem_kv"][l])
        out_mem = _make_packed_softmax(tag + "mem", head_scale, False, False)(
            p["mq"], mkv[:, :GROUP_W].astype(BF16), mkv[:, GROUP_W:].astype(BF16))

        mixed = jnp.concatenate([out_fox, out_sb, out_mla, out_mem], axis=-1)
        y = _make_gate_out(tag + "out")(mixed, p["gate"], wts["w_out"][l])

    h = _make_ln(f"l{DEPTH - 1}_ln", True)(y, h, wts["ln_g"][DEPTH - 1], wts["ln_b"][DEPTH - 1])
    return _loss_op(h, target2d)


def _mesh_pos():
    x, y, c = (lax.axis_index(a) for a in MESH_AXES)
    return x, y, c, 4 * x + 2 * y + c


def _peer(x, y, c, mask):
    return (x ^ ((mask >> 2) & 1), y ^ ((mask >> 1) & 1), c ^ (mask & 1))


_ANY = pl.BlockSpec(memory_space=pl.ANY)


def _all_gather(row_shards, stack_shards):
    n_row, n_all = len(row_shards), len(row_shards) + len(stack_shards)
    shards = list(row_shards) + list(stack_shards)

    def body(*refs):
        ins, outs = refs[:n_all], refs[n_all:2 * n_all]
        send_sems, recv_sems, local_sems = refs[2 * n_all:]
        x, y, c, me = _mesh_pos()

        def window(t, slot):
            if t < n_row:
                rows = shards[t].shape[1]
                return outs[t].at[:, pl.ds(slot * rows, rows), :]
            return outs[t].at[slot]

        local = [pltpu.make_async_copy(ins[t], window(t, me), local_sems.at[t]) for t in range(n_all)]
        for cp in local:
            cp.start()
        sends = []
        for mask in range(1, N_DEV):
            for t in range(n_all):
                cp = pltpu.make_async_remote_copy(
                    src_ref=ins[t], dst_ref=window(t, me), send_sem=send_sems.at[t, mask - 1],
                    recv_sem=recv_sems.at[t, mask - 1], device_id=_peer(x, y, c, mask),
                    device_id_type=pl.DeviceIdType.MESH)
                cp.start()
                sends.append(cp)
        for mask in range(1, N_DEV):
            for t in range(n_all):
                pltpu.make_async_remote_copy(
                    src_ref=ins[t], dst_ref=window(t, me ^ mask), send_sem=send_sems.at[t, mask - 1],
                    recv_sem=recv_sems.at[t, mask - 1], device_id=_peer(x, y, c, mask),
                    device_id_type=pl.DeviceIdType.MESH).wait_recv()
        for cp in sends:
            cp.wait_send()
        for cp in local:
            cp.wait()

    out_shape = [jax.ShapeDtypeStruct((a.shape[0], N_DEV * a.shape[1], a.shape[2]), a.dtype) for a in row_shards]
    out_shape += [jax.ShapeDtypeStruct((N_DEV,) + a.shape, a.dtype) for a in stack_shards]
    return pl.pallas_call(
        body, name="all_gather_weights", in_specs=[_ANY] * n_all, out_specs=[_ANY] * n_all, out_shape=out_shape,
        scratch_shapes=[pltpu.SemaphoreType.DMA((n_all, N_DEV - 1)), pltpu.SemaphoreType.DMA((n_all, N_DEV - 1)),
                        pltpu.SemaphoreType.DMA((n_all,))],
    )(*shards)


def _reduce_scatter(row_full, stack_full, bcast):
    n_row, n_stack = len(row_full), len(stack_full)
    n_all = n_row + n_stack + len(bcast)
    fulls = list(row_full) + list(stack_full) + list(bcast)

    def body(*refs):
        ins, outs = refs[:n_all], refs[n_all:2 * n_all]
        send_sems, recv_sems, local_sems = refs[2 * n_all:]
        x, y, c, me = _mesh_pos()

        def part(t, slot):
            if t < n_row:
                rows = fulls[t].shape[1] // N_DEV
                return ins[t].at[:, pl.ds(slot * rows, rows), :]
            if t < n_row + n_stack:
                return ins[t].at[slot]
            return ins[t]

        local = [pltpu.make_async_copy(part(t, me), outs[t].at[me], local_sems.at[t]) for t in range(n_all)]
        for cp in local:
            cp.start()
        sends = []
        for mask in range(1, N_DEV):
            for t in range(n_all):
                cp = pltpu.make_async_remote_copy(
                    src_ref=part(t, me ^ mask), dst_ref=outs[t].at[me], send_sem=send_sems.at[t, mask - 1],
                    recv_sem=recv_sems.at[t, mask - 1], device_id=_peer(x, y, c, mask),
                    device_id_type=pl.DeviceIdType.MESH)
                cp.start()
                sends.append(cp)
        for mask in range(1, N_DEV):
            for t in range(n_all):
                pltpu.make_async_remote_copy(
                    src_ref=part(t, me), dst_ref=outs[t].at[me ^ mask], send_sem=send_sems.at[t, mask - 1],
                    recv_sem=recv_sems.at[t, mask - 1], device_id=_peer(x, y, c, mask),
                    device_id_type=pl.DeviceIdType.MESH).wait_recv()
        for cp in sends:
            cp.wait_send()
        for cp in local:
            cp.wait()

    out_shape = [jax.ShapeDtypeStruct((N_DEV, a.shape[0], a.shape[1] // N_DEV, a.shape[2]), a.dtype) for a in row_full]
    out_shape += [jax.ShapeDtypeStruct(a.shape, a.dtype) for a in stack_full]
    out_shape += [jax.ShapeDtypeStruct((N_DEV,) + a.shape, a.dtype) for a in bcast]
    return pl.pallas_call(
        body, name="reduce_scatter_grads", in_specs=[_ANY] * n_all, out_specs=[_ANY] * n_all, out_shape=out_shape,
        scratch_shapes=[pltpu.SemaphoreType.DMA((n_all, N_DEV - 1)), pltpu.SemaphoreType.DMA((n_all, N_DEV - 1)),
                        pltpu.SemaphoreType.DMA((n_all,))],
    )(*fulls)


def _adamw(slots, w, m, v, name):
    shape = w.shape
    cols = shape[-1]
    rows = math.prod(shape[:-1])
    tr = _pick(rows, (64, 32, 16, 8))
    c1 = 1.0 - ADAM_B1 ** ADAM_STEP
    c2 = 1.0 - ADAM_B2 ** ADAM_STEP

    def body(s_ref, w_ref, m_ref, v_ref, g_ref, d_ref, nm_ref, nv_ref):
        g = s_ref[0].astype(F32)
        for k in range(1, N_DEV):
            g = g + s_ref[k].astype(F32)
        nm = ADAM_B1 * m_ref[...] + (1.0 - ADAM_B1) * g
        nv = ADAM_B2 * v_ref[...] + (1.0 - ADAM_B2) * (g * g)
        g_ref[...] = g
        nm_ref[...] = nm
        nv_ref[...] = nv
        d_ref[...] = -ADAM_LR * ((nm / c1) / (jnp.sqrt(nv / c2) + ADAM_EPS) + ADAM_WD * w_ref[...])

    row = pl.BlockSpec((tr, cols), lambda i: (i, 0))
    out = jax.ShapeDtypeStruct((rows, cols), F32)
    outs = pl.pallas_call(
        body, name=name, grid=(rows // tr,),
        in_specs=[pl.BlockSpec((N_DEV, tr, cols), lambda i: (0, i, 0)), row, row, row],
        out_specs=[row] * 4, out_shape=[out] * 4, compiler_params=_cp(("parallel",)),
    )(slots.reshape(N_DEV, rows, cols), w.reshape(rows, cols), m.reshape(rows, cols), v.reshape(rows, cols))
    return [o.reshape(shape) for o in outs]


_SMALL = ("ln_in_g", "ln_in_b", "mem_ln_g", "mem_ln_b", "b_forget", "mla_q_norm_g", "mla_kv_norm_g", "ln_g", "ln_b")
_ORDER = ("ln_in_g", "ln_in_b", "mem_ln_g", "mem_ln_b", "w_in", "b_forget", "mla_q_norm_g", "w_mla_q_up",
          "mla_kv_norm_g", "w_mla_kv_up", "w_mem_kv", "w_out", "ln_g", "ln_b")


def _pack_small(d):
    flat = jnp.concatenate([d[n].reshape(-1) for n in _SMALL])
    n = flat.shape[0]
    padded = ((n + 8 * LANE - 1) // (8 * LANE)) * (8 * LANE)
    return jnp.pad(flat, (0, padded - n)).reshape(-1, LANE)


def _unpack_small(packed, like):
    flat, out, off = packed.reshape(-1), {}, 0
    for n in _SMALL:
        size = math.prod(like[n].shape)
        out[n] = flat[off:off + size].reshape(like[n].shape)
        off += size
    return out


def _unstack_cols(g):
    n, l, r, c = g.shape
    return g.transpose(1, 2, 0, 3).reshape(l, r, n * c)


def _stack_cols(g):
    l, r, nc = g.shape
    return g.reshape(l, r, N_DEV, nc // N_DEV).transpose(2, 0, 1, 3)


def kernel(x, mem, ln_in_g, ln_in_b, mem_ln_g, mem_ln_b, w_in, b_forget, mla_q_norm_g, w_mla_q_up, mla_kv_norm_g, w_mla_kv_up, w_mem_kv, w_out, ln_g, ln_b, loss_target, m_ln_in_g, m_ln_in_b, m_mem_ln_g, m_mem_ln_b, m_w_in, m_b_forget, m_mla_q_norm_g, m_w_mla_q_up, m_mla_kv_norm_g, m_w_mla_kv_up, m_w_mem_kv, m_w_out, m_ln_g, m_ln_b, v_ln_in_g, v_ln_in_b, v_mem_ln_g, v_mem_ln_b, v_w_in, v_b_forget, v_mla_q_norm_g, v_w_mla_q_up, v_mla_kv_norm_g, v_w_mla_kv_up, v_w_mem_kv, v_w_out, v_ln_g, v_ln_b):
    w_shard = dict(ln_in_g=ln_in_g, ln_in_b=ln_in_b, mem_ln_g=mem_ln_g, mem_ln_b=mem_ln_b, w_in=w_in,
                   b_forget=b_forget, mla_q_norm_g=mla_q_norm_g, w_mla_q_up=w_mla_q_up,
                   mla_kv_norm_g=mla_kv_norm_g, w_mla_kv_up=w_mla_kv_up, w_mem_kv=w_mem_kv, w_out=w_out,
                   ln_g=ln_g, ln_b=ln_b)
    m_shard = dict(ln_in_g=m_ln_in_g, ln_in_b=m_ln_in_b, mem_ln_g=m_mem_ln_g, mem_ln_b=m_mem_ln_b, w_in=m_w_in,
                   b_forget=m_b_forget, mla_q_norm_g=m_mla_q_norm_g, w_mla_q_up=m_w_mla_q_up,
                   mla_kv_norm_g=m_mla_kv_norm_g, w_mla_kv_up=m_w_mla_kv_up, w_mem_kv=m_w_mem_kv, w_out=m_w_out,
                   ln_g=m_ln_g, ln_b=m_ln_b)
    v_shard = dict(ln_in_g=v_ln_in_g, ln_in_b=v_ln_in_b, mem_ln_g=v_mem_ln_g, mem_ln_b=v_mem_ln_b, w_in=v_w_in,
                   b_forget=v_b_forget, mla_q_norm_g=v_mla_q_norm_g, w_mla_q_up=v_w_mla_q_up,
                   mla_kv_norm_g=v_mla_kv_norm_g, w_mla_kv_up=v_w_mla_kv_up, w_mem_kv=v_w_mem_kv, w_out=v_w_out,
                   ln_g=v_ln_g, ln_b=v_ln_b)

    to16 = lambda ws: [a.astype(BF16) for a in ws]
    gathered = _all_gather(to16([w_in, w_mem_kv, w_out]), to16([w_mla_q_up, w_mla_kv_up]))
    g_in, g_mem, g_out, g_qup, g_kvup = [a.astype(F32) for a in gathered]
    full = dict(w_shard)
    full.update(w_in=g_in, w_mem_kv=g_mem, w_out=g_out, w_mla_q_up=_unstack_cols(g_qup),
                w_mla_kv_up=_unstack_cols(g_kvup))

    loss_local, (grad_w, grad_x) = jax.value_and_grad(_trunk_loss, argnums=(0, 1))(
        full, x[0], mem[0], loss_target[0])

    s_in, s_mem, s_out, s_qup, s_kvup, s_small = _reduce_scatter(
        to16([grad_w["w_in"], grad_w["w_mem_kv"], grad_w["w_out"]]),
        to16([_stack_cols(grad_w["w_mla_q_up"]), _stack_cols(grad_w["w_mla_kv_up"])]),
        [_pack_small(grad_w)])

    res = {}
    for name, slots in (("w_in", s_in), ("w_mem_kv", s_mem), ("w_out", s_out), ("w_mla_q_up", s_qup),
                        ("w_mla_kv_up", s_kvup)):
        res[name] = _adamw(slots, w_shard[name], m_shard[name], v_shard[name], "adamw_" + name)
    small = _adamw(s_small, _pack_small(w_shard), _pack_small(m_shard), _pack_small(v_shard), "adamw_small")
    small = [_unpack_small(a, w_shard) for a in small]
    for name in _SMALL:
        res[name] = [a[name] for a in small]

    loss = lax.psum(loss_local, MESH_AXES)
    outs = [loss, grad_x[None]]
    for k in range(4):
        outs += [res[name][k] for name in _ORDER]
    return tuple(outs)
```

```python
import functools
import math

import jax
import jax.numpy as jnp
from jax import lax
from jax.experimental import pallas as pl
from jax.experimental.pallas import tpu as pltpu

F32 = jnp.float32
BF16 = jnp.bfloat16

D_MODEL = 1024
DEPTH = 2
GROUP_W = 256
N_HEADS = 4
HEAD_DIM = 64
MLA_Q_RANK = 256
MLA_KV_RANK = 128
MLA_NOPE = 64
MLA_ROPE = 32
MLA_V = 64
ROPE_THETA = 10000.0
LN_EPS = 1e-5
RMS_EPS = 1e-6
DEEPNORM_ALPHA = (2 * DEPTH) ** 0.25
SPLIT_SIZES = (256, 256, 256, 4, 256, 256, 256, 256, 128, 32, 256, 1024)
IN_COLS = sum(SPLIT_SIZES)
_ORIG_OFF = [sum(SPLIT_SIZES[:i]) for i in range(len(SPLIT_SIZES))]
_PERM = (("fq", 0), ("fk", 1), ("fv", 2), ("sq", 4), ("sk", 5), ("sv", 6), ("c_q", 7), ("c_kv", 8),
         ("mq", 10), ("gate", 11), ("k_rot", 9), ("f_logit", 3))
LANE = 128
PROJ_COLS = ((IN_COLS + LANE - 1) // LANE) * LANE

ADAM_LR = 0.001
ADAM_B1 = 0.9
ADAM_B2 = 0.999
ADAM_EPS = 1e-08
ADAM_WD = 0.01
ADAM_STEP = 10

N_DEV = 8
MESH_AXES = ("x", "y", "c")
VMEM_LIMIT = 48 * 1024 * 1024
ATTN_VMEM_LIMIT = 56 * 1024 * 1024
ATTN_BQ = 512
ATTN_BK = 512
CUMSUM_CHUNK = 256
NEG_BIG = -1e30
LOG2E = math.log2(math.e)
MM_TM, MM_TN, MM_TK, MM_TK_NT = 1024, 1664, 1024, 3328

_NT = (((1,), (1,)), ((), ()))
_NN = (((1,), (0,)), ((), ()))


def _cp(sem, vmem=VMEM_LIMIT):
    return pltpu.CompilerParams(dimension_semantics=sem, vmem_limit_bytes=vmem)


def _dot(a, b, dims=_NN):
    return lax.dot_general(a, b, dims, preferred_element_type=F32)


def _pick(n, cands):
    for c in cands:
        if c <= n and n % c == 0:
            return c
    return n


def _tile(n, cap):
    if n <= cap:
        return n
    best = None
    for d in range(LANE, cap + 1, LANE):
        if n % d == 0:
            best = d
    assert best is not None, (n, cap)
    return best


def _matmul(a, b, mode, name, also16=False):
    if mode == "nn":
        (M, K), (K2, N) = a.shape, b.shape
    else:
        (M, K), (N, K2) = a.shape, b.shape
    assert K == K2 and a.dtype == BF16 and b.dtype == BF16, (a.shape, b.shape, mode)
    tm, tn = _tile(M, MM_TM), _tile(N, MM_TN)
    tk = _tile(K, MM_TK if mode == "nn" else MM_TK_NT)
    nk = K // tk
    dims = _NN if mode == "nn" else _NT

    def body(a_ref, b_ref, *rest):
        o_ref, acc_ref = rest[0], rest[-1]
        part = _dot(a_ref[...], b_ref[...], dims)
        if nk == 1:
            o_ref[...] = part
            if also16:
                rest[1][...] = part.astype(BF16)
        else:
            assert not also16
            k = pl.program_id(2)

            @pl.when(k == 0)
            def _():
                acc_ref[...] = part

            @pl.when(k > 0)
            def _():
                acc_ref[...] += part

            @pl.when(k == nk - 1)
            def _():
                o_ref[...] = acc_ref[...]

    a_spec = pl.BlockSpec((tm, tk), lambda j, i, k: (i, k))
    if mode == "nn":
        b_spec = pl.BlockSpec((tk, tn), lambda j, i, k: (k, j))
    else:
        b_spec = pl.BlockSpec((tn, tk), lambda j, i, k: (j, k))
    acc_shape = (tm, tn) if nk > 1 else (8, LANE)
    o_spec = pl.BlockSpec((tm, tn), lambda j, i, k: (i, j))
    outs = pl.pallas_call(
        body, name=name, grid=(N // tn, M // tm, nk),
        in_specs=[a_spec, b_spec],
        out_specs=[o_spec, o_spec] if also16 else o_spec,
        out_shape=[jax.ShapeDtypeStruct((M, N), F32), jax.ShapeDtypeStruct((M, N), BF16)] if also16
        else jax.ShapeDtypeStruct((M, N), F32),
        scratch_shapes=[pltpu.VMEM(acc_shape, F32)],
        compiler_params=_cp(("parallel", "parallel", "arbitrary")),
    )(a, b)
    return outs


def _make_mm(name):
    @jax.custom_vjp
    def mm(a, w):
        return _matmul(a.astype(BF16), w.astype(BF16), "nn", name + "_fwd")

    def fwd(a, w):
        a16, w16 = a.astype(BF16), w.astype(BF16)
        return _matmul(a16, w16, "nn", name + "_fwd"), (a16, w16)

    def bwd(res, dy):
        a16, w16 = res
        dy16 = dy.astype(BF16)
        da = _matmul(dy16, w16, "nt", name + "_dx")
        dw = _matmul(a16.T, dy16, "nn", name + "_dw")
        return da, dw

    mm.defvjp(fwd, bwd)
    return mm


def _row_tile(rows):
    return _pick(rows, (512, 256, 128, 64, 32, 16, 8))


def _ln_stats(u):
    mu = jnp.mean(u, axis=-1, keepdims=True)
    d = u - mu
    var = jnp.mean(d * d, axis=-1, keepdims=True)
    return d, lax.rsqrt(var + LN_EPS)


def _ln_fwd_call(x, res, g, b, name, also16=False):
    rows, dm = x.shape
    tr = _row_tile(rows)
    has_res = res is not None
    n_in = 2 if has_res else 1

    def body(*refs):
        if has_res:
            u = DEEPNORM_ALPHA * refs[1][...] + refs[0][...]
        else:
            u = refs[0][...]
        g_ref, b_ref = refs[n_in], refs[n_in + 1]
        d, rstd = _ln_stats(u)
        y = d * rstd * g_ref[...] + b_ref[...]
        refs[n_in + 2][...] = y
        if also16:
            y16 = y.astype(BF16)
            refs[n_in + 3][...] = y16
            refs[n_in + 4][...] = y16.T

    row = pl.BlockSpec((tr, dm), lambda i: (i, 0))
    vec = pl.BlockSpec((1, dm), lambda i: (0, 0))
    args = (x, res) if has_res else (x,)
    out_specs, out_shape = [row], [jax.ShapeDtypeStruct((rows, dm), F32)]
    if also16:
        out_specs += [row, pl.BlockSpec((dm, tr), lambda i: (0, i))]
        out_shape += [jax.ShapeDtypeStruct((rows, dm), BF16), jax.ShapeDtypeStruct((dm, rows), BF16)]
    outs = pl.pallas_call(
        body, name=name, grid=(rows // tr,),
        in_specs=[row] * n_in + [vec, vec], out_specs=out_specs, out_shape=out_shape,
        compiler_params=_cp(("parallel",)),
    )(*args, g.reshape(1, dm), b.reshape(1, dm))
    return outs if also16 else outs[0]


def _ln_bwd_call(dy, x, res, g, name, dy2=None):
    rows, dm = x.shape
    tr = _row_tile(rows)
    has_res = res is not None
    two = dy2 is not None

    def body(*refs):
        dy_ref, refs = refs[0], refs[1:]
        if two:
            dy2_ref, refs = refs[0], refs[1:]
        if has_res:
            x_ref, r_ref, g_ref, dx_ref, dr_ref, dg_ref, db_ref = refs
            u = DEEPNORM_ALPHA * r_ref[...] + x_ref[...]
        else:
            x_ref, g_ref, dx_ref, dg_ref, db_ref = refs
            u = x_ref[...]
        i = pl.program_id(0)
        d, rstd = _ln_stats(u)
        xhat = d * rstd
        dyv = dy_ref[...] + dy2_ref[...] if two else dy_ref[...]
        dxh = dyv * g_ref[...]
        m1 = jnp.mean(dxh, axis=-1, keepdims=True)
        m2 = jnp.mean(dxh * xhat, axis=-1, keepdims=True)
        du = rstd * (dxh - m1 - xhat * m2)
        dx_ref[...] = du
        if has_res:
            dr_ref[...] = DEEPNORM_ALPHA * du
        pg = jnp.sum(dyv * xhat, axis=0, keepdims=True)
        pb = jnp.sum(dyv, axis=0, keepdims=True)

        @pl.when(i == 0)
        def _():
            dg_ref[...] = pg
            db_ref[...] = pb

        @pl.when(i > 0)
        def _():
            dg_ref[...] += pg
            db_ref[...] += pb

    row = pl.BlockSpec((tr, dm), lambda i: (i, 0))
    vec = pl.BlockSpec((1, dm), lambda i: (0, 0))
    big = jax.ShapeDtypeStruct((rows, dm), F32)
    small = jax.ShapeDtypeStruct((1, dm), F32)
    args = ((dy, dy2) if two else (dy,)) + ((x, res) if has_res else (x,))
    n_big = 2 if has_res else 1
    outs = pl.pallas_call(
        body, name=name, grid=(rows // tr,),
        in_specs=[row] * len(args) + [vec],
        out_specs=[row] * n_big + [vec, vec],
        out_shape=[big] * n_big + [small, small],
        compiler_params=_cp(("arbitrary",)),
    )(*args, g.reshape(1, dm))
    return outs


def _make_ln(name, has_res):
    if has_res:
        @jax.custom_vjp
        def ln(x, res, g, b):
            return _ln_fwd_call(x, res, g, b, name + "_fwd")

        def fwd(x, res, g, b):
            return ln(x, res, g, b), (x, res, g)

        def bwd(saved, dy):
            x, res, g = saved
            dx, dr, dg, db = _ln_bwd_call(dy, x, res, g, name + "_bwd")
            return dx, dr, dg.reshape(-1), db.reshape(-1)
    else:
        @jax.custom_vjp
        def ln(x, g, b):
            return _ln_fwd_call(x, None, g, b, name + "_fwd")

        def fwd(x, g, b):
            return ln(x, g, b), (x, g)

        def bwd(saved, dy):
            x, g = saved
            dx, dg, db = _ln_bwd_call(dy, x, None, g, name + "_bwd")
            return dx, dg.reshape(-1), db.reshape(-1)

    ln.defvjp(fwd, bwd)
    return ln


def _rms_fwd_call(x, g, name):
    rows, dm = x.shape
    tr = _row_tile(rows)

    def body(x_ref, g_ref, o_ref):
        xv = x_ref[...]
        rstd = lax.rsqrt(jnp.mean(xv * xv, axis=-1, keepdims=True) + RMS_EPS)
        o_ref[...] = xv * rstd * g_ref[...]

    row = pl.BlockSpec((tr, dm), lambda i: (i, 0))
    vec = pl.BlockSpec((1, dm), lambda i: (0, 0))
    return pl.pallas_call(
        body, name=name, grid=(rows // tr,), in_specs=[row, vec], out_specs=row,
        out_shape=jax.ShapeDtypeStruct((rows, dm), F32), compiler_params=_cp(("parallel",)),
    )(x, g.reshape(1, dm))


def _rms_bwd_call(dy, x, g, name):
    rows, dm = x.shape
    tr = _row_tile(rows)

    def body(dy_ref, x_ref, g_ref, dx_ref, dg_ref):
        i = pl.program_id(0)
        xv = x_ref[...]
        dyv = dy_ref[...]
        rstd = lax.rsqrt(jnp.mean(xv * xv, axis=-1, keepdims=True) + RMS_EPS)
        xhat = xv * rstd
        dxh = dyv * g_ref[...]
        m2 = jnp.mean(dxh * xhat, axis=-1, keepdims=True)
        dx_ref[...] = rstd * (dxh - xhat * m2)
        pg = jnp.sum(dyv * xhat, axis=0, keepdims=True)

        @pl.when(i == 0)
        def _():
            dg_ref[...] = pg

        @pl.when(i > 0)
        def _():
            dg_ref[...] += pg

    row = pl.BlockSpec((tr, dm), lambda i: (i, 0))
    vec = pl.BlockSpec((1, dm), lambda i: (0, 0))
    return pl.pallas_call(
        body, name=name, grid=(rows // tr,), in_specs=[row, row, vec], out_specs=[row, vec],
        out_shape=[jax.ShapeDtypeStruct((rows, dm), F32), jax.ShapeDtypeStruct((1, dm), F32)],
        compiler_params=_cp(("arbitrary",)),
    )(dy, x, g.reshape(1, dm))


def _make_rms(name):
    @jax.custom_vjp
    def rms(x, g):
        return _rms_fwd_call(x, g, name + "_fwd")

    def fwd(x, g):
        return rms(x, g), (x, g)

    def bwd(saved, dy):
        x, g = saved
        dx, dg = _rms_bwd_call(dy, x, g, name + "_bwd")
        return dx, dg.reshape(-1)

    rms.defvjp(fwd, bwd)
    return rms


def _sigmoid(x):
    return 1.0 / (1.0 + jnp.exp(-x))


def _gate_fwd_call(mixed, gate, name):
    rows, dm = mixed.shape
    tr = _row_tile(rows)

    def body(m_ref, g_ref, o_ref, oT_ref):
        gv = g_ref[...]
        y16 = (m_ref[...] * (gv * _sigmoid(gv))).astype(BF16)
        o_ref[...] = y16
        oT_ref[...] = y16.T

    row = pl.BlockSpec((tr, dm), lambda i: (i, 0))
    return pl.pallas_call(
        body, name=name, grid=(rows // tr,), in_specs=[row, row],
        out_specs=[row, pl.BlockSpec((dm, tr), lambda i: (0, i))],
        out_shape=[jax.ShapeDtypeStruct((rows, dm), BF16), jax.ShapeDtypeStruct((dm, rows), BF16)],
        compiler_params=_cp(("parallel",)),
    )(mixed, gate)


def _gate_bwd_call(dy, mixed, gate, name):
    rows, dm = mixed.shape
    tr = _row_tile(rows)

    def body(dy_ref, m_ref, g_ref, dm_ref, dg_ref):
        gv = g_ref[...]
        dyv = dy_ref[...]
        sg = _sigmoid(gv)
        dm_ref[...] = dyv * (gv * sg)
        dg_ref[...] = dyv * m_ref[...] * (sg * (1.0 + gv * (1.0 - sg)))

    row = pl.BlockSpec((tr, dm), lambda i: (i, 0))
    out = jax.ShapeDtypeStruct((rows, dm), F32)
    return pl.pallas_call(
        body, name=name, grid=(rows // tr,), in_specs=[row, row, row], out_specs=[row, row],
        out_shape=[out, out], compiler_params=_cp(("parallel",)),
    )(dy, mixed, gate)


def _make_gate_out(name):
    def run_fwd(mixed, gate, w):
        g16, gT16 = _gate_fwd_call(mixed, gate, name + "_gate_fwd")
        w16 = w.astype(BF16)
        return _matmul(g16, w16, "nn", name + "_fwd"), (mixed, gate, gT16, w16)

    def run_bwd(saved, dy):
        mixed, gate, gT16, w16 = saved
        dy16 = dy.astype(BF16)
        dgated = _matmul(dy16, w16, "nt", name + "_dx")
        dmix, dgate = _gate_bwd_call(dgated, mixed, gate, name + "_gate_bwd")
        return dmix, dgate, _matmul(gT16, dy16, "nn", name + "_dw")

    @jax.custom_vjp
    def gate_out(mixed, gate, w):
        return run_fwd(mixed, gate, w)[0]

    gate_out.defvjp(run_fwd, run_bwd)
    return gate_out


def _loss_call(y, t, name):
    rows, dm = y.shape
    tr = _row_tile(rows)

    def body(y_ref, t_ref, l_ref, d_ref):
        i = pl.program_id(0)
        e = y_ref[...] - t_ref[...]
        d_ref[...] = e * (1.0 / dm)
        part = 0.5 * jnp.sum(jnp.mean(e * e, axis=-1, keepdims=True), axis=0, keepdims=True)

        @pl.when(i == 0)
        def _():
            l_ref[...] = part

        @pl.when(i > 0)
        def _():
            l_ref[...] += part

    row = pl.BlockSpec((tr, dm), lambda i: (i, 0))
    one = pl.BlockSpec((1, 1), lambda i: (0, 0))
    return pl.pallas_call(
        body, name=name, grid=(rows // tr,), in_specs=[row, row], out_specs=[one, row],
        out_shape=[jax.ShapeDtypeStruct((1, 1), F32), jax.ShapeDtypeStruct((rows, dm), F32)],
        compiler_params=_cp(("arbitrary",)),
    )(y, t)


@jax.custom_vjp
def _loss_op(y, t):
    return _loss_call(y, t, "loss_head")[0][0, 0]


def _loss_fwd(y, t):
    l, d = _loss_call(y, t, "loss_head")
    return l[0, 0], d


def _loss_bwd(d, ct):
    return ct * d, jnp.zeros_like(d)


_loss_op.defvjp(_loss_fwd, _loss_bwd)


def _attn_blocks(S, Sk):
    bq, bk = min(ATTN_BQ, S), min(ATTN_BK, Sk)
    assert S % bq == 0 and Sk % bk == 0
    return bq, bk


def _valid_t(i, j, bq, bk, strict):
    key = j * bk + lax.broadcasted_iota(jnp.int32, (bk, bq), 0)
    qry = i * bq + lax.broadcasted_iota(jnp.int32, (bk, bq), 1)
    return (key < qry) if strict else (key <= qry)


def _sm_fwd_t(qT, k, vT, cmul, causal, name):
    H, DK, S = qT.shape
    Sk, dv = k.shape[1], vT.shape[1]
    bq, bk = _attn_blocks(S, Sk)
    nq, nkb = S // bq, Sk // bk
    assert H % PAIR == 0
    heads = range(PAIR)
    if causal:
        assert S == Sk and bq == bk

    def body(qT_ref, k_ref, vT_ref, oT_ref, lse_ref):
        i = pl.program_id(1)
        qTs = [qT_ref[w] for w in heads]

        def blk(j, carry, masked):
            off = pl.multiple_of(j * bk, bk)
            sT = [_dot(k_ref[w, pl.ds(off, bk), :], qTs[w]) * cmul for w in heads]
            if masked:
                valid = _valid_t(i, j, bq, bk, False)
                sT = [jnp.where(valid, s, NEG_BIG) for s in sT]
            m_new = [jnp.maximum(carry[w][0], jnp.max(sT[w], axis=0, keepdims=True)) for w in heads]
            p = [jnp.exp2(sT[w] - m_new[w]) for w in heads]
            a = [jnp.exp2(carry[w][0] - m_new[w]) for w in heads]
            l = [a[w] * carry[w][1] + jnp.sum(p[w], axis=0, keepdims=True) for w in heads]
            acc = [a[w] * carry[w][2] + _dot(vT_ref[w, :, pl.ds(off, bk)], p[w].astype(BF16)) for w in heads]
            return tuple((m_new[w], l[w], acc[w]) for w in heads)

        carry = tuple((jnp.full((1, bq), NEG_BIG, F32), jnp.zeros((1, bq), F32), jnp.zeros((dv, bq), F32))
                      for _ in heads)
        if causal:
            carry = lax.fori_loop(0, i, lambda j, c: blk(j, c, False), carry)
            carry = blk(i, carry, True)
        else:
            carry = lax.fori_loop(0, nkb, lambda j, c: blk(j, c, False), carry)
        for w in heads:
            oT_ref[w] = carry[w][2] / carry[w][1]
            lse_ref[w] = carry[w][0] + jnp.log2(carry[w][1])

    qcol = lambda d: pl.BlockSpec((PAIR, d, bq), lambda h, i: (h, 0, i))
    return pl.pallas_call(
        body, name=name, grid=(H // PAIR, nq),
        in_specs=[qcol(DK), pl.BlockSpec((PAIR, Sk, DK), lambda h, i: (h, 0, 0)),
                  pl.BlockSpec((PAIR, dv, Sk), lambda h, i: (h, 0, 0))],
        out_specs=[qcol(dv), qcol(1)],
        out_shape=[jax.ShapeDtypeStruct((H, dv, S), F32), jax.ShapeDtypeStruct((H, 1, S), F32)],
        compiler_params=_cp(("parallel", "arbitrary"), ATTN_VMEM_LIMIT),
    )(qT, k, vT)


def _sm_bwd_t(qT, qn, k, kT, v, oT, lse, doT, do, cmul, gscale, causal, name):
    H, DK, S = qT.shape
    Sk, dv = k.shape[1], v.shape[2]
    bq, bk = _attn_blocks(S, Sk)
    nq, nkb = S // bq, Sk // bk

    def body(qT_ref, qn_ref, k_ref, kT_ref, v_ref, oT_ref, lse_ref, doT_ref, do_ref, dqT_ref, dk_ref, dv_ref):
        i = pl.program_id(1)

        @pl.when(i == 0)
        def _():
            dk_ref[...] = jnp.zeros_like(dk_ref)
            dv_ref[...] = jnp.zeros_like(dv_ref)

        qTb = qT_ref[...]
        qnb = qn_ref[...]
        dob = do_ref[...]
        doTf = doT_ref[...]
        doTb = doTf.astype(BF16)
        delta = jnp.sum(doTf * oT_ref[...], axis=0, keepdims=True)
        lse = lse_ref[...]

        def blk(j, dq, masked):
            off = pl.multiple_of(j * bk, bk)
            sT = _dot(k_ref[pl.ds(off, bk), :], qTb) * cmul
            if masked:
                sT = jnp.where(_valid_t(i, j, bq, bk, False), sT, NEG_BIG)
            p = jnp.exp2(sT - lse)
            dp = _dot(v_ref[pl.ds(off, bk), :], doTb)
            ds = p * (dp - delta)
            dsb = (ds * gscale).astype(BF16) if gscale != 1.0 else ds.astype(BF16)
            dv_ref[pl.ds(off, bk), :] += _dot(p.astype(BF16), dob)
            dk_ref[pl.ds(off, bk), :] += _dot(dsb, qnb)
            return dq + _dot(kT_ref[:, pl.ds(off, bk)], dsb)

        dq = jnp.zeros((DK, bq), F32)
        if causal:
            dq = lax.fori_loop(0, i, lambda j, c: blk(j, c, False), dq)
            dq = blk(i, dq, True)
        else:
            dq = lax.fori_loop(0, nkb, lambda j, c: blk(j, c, False), dq)
        dqT_ref[...] = dq

    qcol = lambda d: pl.BlockSpec((None, d, bq), lambda h, i: (h, 0, i))
    qrow = lambda d: pl.BlockSpec((None, bq, d), lambda h, i: (h, i, 0))
    krow = lambda d: pl.BlockSpec((None, Sk, d), lambda h, i: (h, 0, 0))
    return pl.pallas_call(
        body, name=name, grid=(H, nq),
        in_specs=[qcol(DK), qrow(DK), krow(DK), pl.BlockSpec((None, DK, Sk), lambda h, i: (h, 0, 0)), krow(dv),
                  qcol(dv), qcol(1), qcol(dv), qrow(dv)],
        out_specs=[qcol(DK), krow(DK), krow(dv)],
        out_shape=[jax.ShapeDtypeStruct((H, DK, S), F32), jax.ShapeDtypeStruct((H, Sk, DK), F32),
                   jax.ShapeDtypeStruct((H, Sk, dv), F32)],
        compiler_params=_cp(("parallel", "arbitrary"), ATTN_VMEM_LIMIT),
    )(qT, qn, k, kT, v, oT, lse, doT, do)


def _tri(n, fn):
    r = lax.broadcasted_iota(jnp.int32, (n, n), 0)
    c = lax.broadcasted_iota(jnp.int32, (n, n), 1)
    return jnp.where(fn(r, c), 1.0, 0.0).astype(BF16)


def _key_cumsum(x, tri2, suffix, base):
    bk = x.shape[0]
    c = min(CUMSUM_CHUNK, bk)
    n = bk // c
    hi32 = lax.bitcast_convert_type(lax.bitcast_convert_type(x, jnp.int32) & jnp.int32(-65536), F32)
    hi = hi32.astype(BF16)
    lo = (x - hi32).astype(BF16)
    tot = [jnp.sum(x[a * c:(a + 1) * c], axis=0, keepdims=True) for a in range(n)]
    outs = []
    for a in range(n):
        row = base
        for t in (tot[a + 1:] if suffix else tot[:a]):
            row = row + t
        stacked = jnp.concatenate([hi[a * c:(a + 1) * c], lo[a * c:(a + 1) * c]], axis=0)
        outs.append(_dot(tri2, stacked) + row)
    total = tot[0]
    for t in tot[1:]:
        total = total + t
    return (outs[0] if n == 1 else jnp.concatenate(outs, axis=0)), total


def _tri2(n, fn):
    t = _tri(n, fn)
    return jnp.concatenate([t, t], axis=1)


def _sb_logs(z):
    neg_abs = lax.bitcast_convert_type(lax.bitcast_convert_type(z, jnp.int32) | jnp.int32(-2 ** 31), F32)
    ls = jnp.minimum(z, 0.0) - jnp.log(1.0 + jnp.exp(neg_abs))
    return ls, ls - z


PAIR = LANE // HEAD_DIM


def _head_lanes(shape, w, axis):
    idx = lax.broadcasted_iota(jnp.int32, shape, axis)
    return (idx >= HEAD_DIM * w) & (idx < HEAD_DIM * (w + 1))


def _bias_rows(w, bq):
    row = lax.broadcasted_iota(jnp.int32, (LANE, bq), 0)
    return jnp.where((row >= 3 * w) & (row < 3 * w + 3), -1.0, 0.0).astype(BF16)


def _merge_pair(parts):
    return jnp.where(_head_lanes(parts[0].shape, 0, 0), parts[0], parts[1]).T


def _smp_fwd(q2, k2, v2, bias, r, causal, name):
    S, C = q2.shape
    Sk = k2.shape[0]
    bq, bk = _attn_blocks(S, Sk)
    nq, nkb, P = S // bq, Sk // bk, C // LANE
    use_f = bias is not None
    if causal:
        assert S == Sk and bq == bk

    def body(*refs):
        if use_f:
            q_ref, k_ref, v_ref, b_ref, r_ref, o_ref, lse_ref = refs
        else:
            q_ref, k_ref, v_ref, o_ref, lse_ref = refs
        i = pl.program_id(1)
        qp = q_ref[...]
        heads = range(PAIR)
        qTs = [jnp.where(_head_lanes(qp.shape, w, 1), qp, jnp.zeros_like(qp)).T for w in heads]
        if use_f:
            qTs = [jnp.concatenate([qTs[w], _bias_rows(w, bq)], axis=0) for w in heads]

        def blk(j, carry, masked):
            off = pl.multiple_of(j * bk, bk)
            kb = k_ref[pl.ds(off, bk), :]
            if use_f:
                kb = jnp.concatenate([kb, b_ref[pl.ds(off, bk), :]], axis=1)
            vTb = v_ref[pl.ds(off, bk), :].T
            sT = [_dot(kb, qTs[w]) * LOG2E for w in heads]
            if masked:
                valid = _valid_t(i, j, bq, bk, False)
                sT = [jnp.where(valid, s, NEG_BIG) for s in sT]
            cm = [jnp.max(s, axis=0, keepdims=True) for s in sT]
            if use_f:
                cm = [cm[w] + r_ref[w] for w in heads]
            m_new = [jnp.maximum(carry[w][0], cm[w]) for w in heads]
            shift = [(m_new[w] - r_ref[w]) if use_f else m_new[w] for w in heads]
            p = [jnp.exp2(sT[w] - shift[w]) for w in heads]
            a = [jnp.exp2(carry[w][0] - m_new[w]) for w in heads]
            l = [a[w] * carry[w][1] + jnp.sum(p[w], axis=0, keepdims=True) for w in heads]
            acc = [a[w] * carry[w][2] + _dot(vTb, p[w].astype(BF16)) for w in heads]
            return tuple((m_new[w], l[w], acc[w]) for w in heads)

        carry = tuple((jnp.full((1, bq), NEG_BIG, F32), jnp.zeros((1, bq), F32), jnp.zeros((LANE, bq), F32))
                      for _ in heads)
        if causal:
            carry = lax.fori_loop(0, i, lambda j, c: blk(j, c, False), carry)
            carry = blk(i, carry, True)
        else:
            carry = lax.fori_loop(0, nkb, lambda j, c: blk(j, c, False), carry)
        for w in heads:
            lse_ref[w] = carry[w][0] + jnp.log2(carry[w][1])
        o_ref[...] = _merge_pair([carry[w][2] / carry[w][1] for w in heads])

    qblk = pl.BlockSpec((bq, LANE), lambda p, i: (i, p))
    kres = pl.BlockSpec((Sk, LANE), lambda p, i: (0, p))
    stat = pl.BlockSpec((PAIR, 1, bq), lambda p, i: (p, 0, i))
    in_specs = [qblk, kres, kres]
    args = [q2, k2, v2]
    if use_f:
        in_specs += [kres, stat]
        args += [bias, r]
    return pl.pallas_call(
        body, name=name, grid=(P, nq), in_specs=in_specs, out_specs=[qblk, stat],
        out_shape=[jax.ShapeDtypeStruct((S, C), F32), jax.ShapeDtypeStruct((PAIR * P, 1, S), F32)],
        compiler_params=_cp(("parallel", "arbitrary"), ATTN_VMEM_LIMIT),
    )(*args)


def _smp_bwd(q2, k2, v2, o2, lse, do2, bias, r, scale, causal, name):
    S, C = q2.shape
    Sk = k2.shape[0]
    bq, bk = _attn_blocks(S, Sk)
    nq, nkb, P = S // bq, Sk // bk, C // LANE
    use_f = bias is not None

    def body(*refs):
        if use_f:
            (q_ref, k_ref, v_ref, o_ref, lse_ref, do_ref, b_ref, r_ref,
             dq_ref, dk_ref, dv_ref, dr_ref, dkey_ref, dk_acc, dv_acc, db_ref) = refs
        else:
            q_ref, k_ref, v_ref, o_ref, lse_ref, do_ref, dq_ref, dk_ref, dv_ref, dk_acc, dv_acc = refs
        i = pl.program_id(1)

        @pl.when(i == 0)
        def _():
            dk_acc[...] = jnp.zeros_like(dk_acc)
            dv_acc[...] = jnp.zeros_like(dv_acc)
            if use_f:
                db_ref[...] = jnp.zeros_like(db_ref)

        qp = q_ref[...]
        dof = do_ref[...]
        prod = dof * o_ref[...]
        heads = range(PAIR)
        mine = [_head_lanes(qp.shape, w, 1) for w in heads]
        qz = [jnp.where(mine[w], qp, jnp.zeros_like(qp)) for w in heads]
        qTs = [qz[w].T for w in heads]
        if use_f:
            qTs = [jnp.concatenate([qTs[w], _bias_rows(w, bq)], axis=0) for w in heads]
        doz = [jnp.where(mine[w], dof, 0.0).astype(BF16) for w in heads]
        doT = [doz[w].T for w in heads]
        delta = [jnp.sum(jnp.where(mine[w], prod, 0.0).T, axis=0, keepdims=True) for w in heads]
        shift = [(lse_ref[w] - r_ref[w]) if use_f else lse_ref[w] for w in heads]

        def blk(j, carry, masked):
            off = pl.multiple_of(j * bk, bk)
            kb = k_ref[pl.ds(off, bk), :]
            kTb = kb.T
            if use_f:
                kb = jnp.concatenate([kb, b_ref[pl.ds(off, bk), :]], axis=1)
            vb = v_ref[pl.ds(off, bk), :]
            sT = [_dot(kb, qTs[w]) * LOG2E for w in heads]
            if masked:
                valid = _valid_t(i, j, bq, bk, False)
                sT = [jnp.where(valid, s, NEG_BIG) for s in sT]
            p = [jnp.exp2(sT[w] - shift[w]) for w in heads]
            dp = [_dot(vb, doT[w]) for w in heads]
            ds = [p[w] * (dp[w] - delta[w]) for w in heads]
            dsb = [d.astype(BF16) for d in ds]
            dvs = [_dot(p[w].astype(BF16), doz[w]) for w in heads]
            dks = [_dot(dsb[w], qz[w]) for w in heads]
            dv_acc[pl.ds(off, bk), :] += dvs[0] + dvs[1]
            dk_acc[pl.ds(off, bk), :] += dks[0] + dks[1]
            dr = [carry[w][1] for w in heads]
            if use_f:
                dr = [dr[w] + jnp.sum(ds[w], axis=0, keepdims=True) for w in heads]
                lane = lax.broadcasted_iota(jnp.int32, (bk, LANE), 1)
                cols = [jnp.where(lane == w, jnp.sum(ds[w], axis=1, keepdims=True), 0.0) for w in heads]
                db_ref[pl.ds(off, bk), :] += cols[0] + cols[1]
            dq = [carry[w][0] + _dot(kTb, dsb[w]) for w in heads]
            return tuple((dq[w], dr[w]) for w in heads)

        carry = tuple((jnp.zeros((LANE, bq), F32), jnp.zeros((1, bq), F32)) for _ in heads)
        if causal:
            carry = lax.fori_loop(0, i, lambda j, c: blk(j, c, False), carry)
            carry = blk(i, carry, True)
        else:
            carry = lax.fori_loop(0, nkb, lambda j, c: blk(j, c, False), carry)
        if use_f:
            for w in heads:
                dr_ref[w] = carry[w][1]
        dq_ref[...] = (_merge_pair([carry[w][0] for w in heads]) * scale).astype(BF16)

        @pl.when(i == nq - 1)
        def _():
            dk_ref[...] = dk_acc[...].astype(BF16)
            dv_ref[...] = dv_acc[...].astype(BF16)

        if use_f:
            @pl.when(i == nq - 1)
            def _():
                def chunk(cidx, carry):
                    off = pl.multiple_of(cidx * LANE, LANE)
                    t = db_ref[pl.ds(off, LANE), :].T
                    for w in range(PAIR):
                        dkey_ref[w, :, pl.ds(off, LANE)] = t[w:w + 1, :]
                    return carry

                lax.fori_loop(0, Sk // LANE, chunk, 0)

    qblk = pl.BlockSpec((bq, LANE), lambda p, i: (i, p))
    kres = pl.BlockSpec((Sk, LANE), lambda p, i: (0, p))
    stat = pl.BlockSpec((PAIR, 1, bq), lambda p, i: (p, 0, i))
    in_specs = [qblk, kres, kres, qblk, stat, qblk]
    args = [q2, k2, v2, o2, lse, do2]
    out_specs = [qblk, kres, kres]
    out_shape = [jax.ShapeDtypeStruct((S, C), BF16), jax.ShapeDtypeStruct((Sk, C), BF16),
                 jax.ShapeDtypeStruct((Sk, C), BF16)]
    scratch = [pltpu.VMEM((Sk, LANE), F32), pltpu.VMEM((Sk, LANE), F32)]
    if use_f:
        in_specs += [kres, stat]
        args += [bias, r]
        out_specs += [stat, pl.BlockSpec((PAIR, 1, Sk), lambda p, i: (p, 0, 0))]
        out_shape += [jax.ShapeDtypeStruct((PAIR * P, 1, S), F32), jax.ShapeDtypeStruct((PAIR * P, 1, Sk), F32)]
        scratch.append(pltpu.VMEM((Sk, LANE), F32))
    return pl.pallas_call(
        body, name=name, grid=(P, nq), in_specs=in_specs, out_specs=out_specs, out_shape=out_shape,
        scratch_shapes=scratch, compiler_params=_cp(("parallel", "arbitrary"), ATTN_VMEM_LIMIT),
    )(*args)


def _sbp_fwd(q2, k2, v2, name):
    S, C = q2.shape
    bq, bk = _attn_blocks(S, S)
    assert bq == bk
    nq, P = S // bq, C // LANE
    c = min(CUMSUM_CHUNK, bk)

    def body(q_ref, k_ref, v_ref, o_ref, lt_ref):
        i = pl.program_id(1)
        qp = q_ref[...]
        after = _tri2(c, lambda s, j: j > s)
        heads = range(PAIR)
        qTs = [jnp.where(_head_lanes(qp.shape, w, 1), qp, jnp.zeros_like(qp)).T for w in heads]

        def blk(jj, carry, masked):
            j = i - jj
            off = pl.multiple_of(j * bk, bk)
            kb = k_ref[pl.ds(off, bk), :]
            vTb = v_ref[pl.ds(off, bk), :].T
            logs = [_sb_logs(_dot(kb, qTs[w])) for w in heads]
            ls, lk = [t[0] for t in logs], [t[1] for t in logs]
            if masked:
                valid = _valid_t(i, j, bq, bk, True)
                lk = [jnp.where(valid, t, 0.0) for t in lk]
            cs = [_key_cumsum(lk[w], after, True, carry[w][0]) for w in heads]
            wgt = [jnp.exp(ls[w] + cs[w][0]) for w in heads]
            if masked:
                wgt = [jnp.where(valid, t, 0.0) for t in wgt]
            acc = [carry[w][1] + _dot(vTb, wgt[w].astype(BF16)) for w in heads]
            return tuple((carry[w][0] + cs[w][1], acc[w]) for w in heads)

        carry = tuple((jnp.zeros((1, bq), F32), jnp.zeros((LANE, bq), F32)) for _ in heads)
        carry = blk(0, carry, True)
        carry = lax.fori_loop(1, i + 1, lambda jj, cr: blk(jj, cr, False), carry)
        for w in heads:
            lt_ref[w] = carry[w][0]
        o_ref[...] = _merge_pair([carry[w][1] for w in heads])

    qblk = pl.BlockSpec((bq, LANE), lambda p, i: (i, p))
    stat = pl.BlockSpec((PAIR, 1, bq), lambda p, i: (p, 0, i))
    return pl.pallas_call(
        body, name=name, grid=(P, nq),
        in_specs=[qblk, pl.BlockSpec((S, LANE), lambda p, i: (0, p)), pl.BlockSpec((S, LANE), lambda p, i: (0, p))],
        out_specs=[qblk, stat],
        out_shape=[jax.ShapeDtypeStruct((S, C), F32), jax.ShapeDtypeStruct((PAIR * P, 1, S), F32)],
        compiler_params=_cp(("parallel", "arbitrary"), ATTN_VMEM_LIMIT),
    )(q2, k2, v2)


def _sbp_bwd(q2, k2, v2, lt, do2, scale, name):
    S, C = q2.shape
    bq, bk = _attn_blocks(S, S)
    nq, P = S // bq, C // LANE
    c = min(CUMSUM_CHUNK, bk)

    def body(q_ref, k_ref, v_ref, lt_ref, do_ref, dq_ref, dk_ref, dv_ref, dk_acc, dv_acc):
        i = pl.program_id(1)

        @pl.when(i == 0)
        def _():
            dk_acc[...] = jnp.zeros_like(dk_acc)
            dv_acc[...] = jnp.zeros_like(dv_acc)

        qp = q_ref[...]
        dof = do_ref[...]
        upto = _tri2(c, lambda s, j: j <= s)
        before = _tri2(c, lambda s, j: j < s)
        heads = range(PAIR)
        mine = [_head_lanes(qp.shape, w, 1) for w in heads]
        qz = [jnp.where(mine[w], qp, jnp.zeros_like(qp)) for w in heads]
        qTs = [qz[w].T for w in heads]
        doz = [jnp.where(mine[w], dof, 0.0).astype(BF16) for w in heads]
        doT = [doz[w].T for w in heads]
        ltot = [lt_ref[w] for w in heads]

        def blk(j, carry, masked):
            off = pl.multiple_of(j * bk, bk)
            kb = k_ref[pl.ds(off, bk), :]
            vb = v_ref[pl.ds(off, bk), :]
            kTb = kb.T
            logs = [_sb_logs(_dot(kb, qTs[w])) for w in heads]
            ls, lk = [t[0] for t in logs], [t[1] for t in logs]
            if masked:
                valid = _valid_t(i, j, bq, bk, True)
                lk = [jnp.where(valid, t, 0.0) for t in lk]
            pin = [_key_cumsum(lk[w], upto, False, carry[w][1] - ltot[w]) for w in heads]
            wgt = [jnp.exp(ls[w] - pin[w][0]) for w in heads]
            if masked:
                wgt = [jnp.where(valid, t, 0.0) for t in wgt]
            g = [_dot(vb, doT[w]) * wgt[w] for w in heads]
            cin = [_key_cumsum(g[w], before, False, carry[w][2]) for w in heads]
            sig = [jnp.exp(t) for t in ls]
            dz = [g[w] * (1.0 - sig[w]) - cin[w][0] * sig[w] for w in heads]
            if masked:
                dz = [jnp.where(valid, t, 0.0) for t in dz]
            dzb = [t.astype(BF16) for t in dz]
            dvs = [_dot(wgt[w].astype(BF16), doz[w]) for w in heads]
            dks = [_dot(dzb[w], qz[w]) for w in heads]
            dv_acc[pl.ds(off, bk), :] += dvs[0] + dvs[1]
            dk_acc[pl.ds(off, bk), :] += dks[0] + dks[1]
            return tuple((carry[w][0] + _dot(kTb, dzb[w]), carry[w][1] + pin[w][1], carry[w][2] + cin[w][1])
                         for w in heads)

        carry = tuple((jnp.zeros((LANE, bq), F32), jnp.zeros((1, bq), F32), jnp.zeros((1, bq), F32)) for _ in heads)
        carry = lax.fori_loop(0, i, lambda j, cr: blk(j, cr, False), carry)
        carry = blk(i, carry, True)
        dq_ref[...] = (_merge_pair([carry[w][0] for w in heads]) * scale).astype(BF16)

        @pl.when(i == nq - 1)
        def _():
            dk_ref[...] = dk_acc[...].astype(BF16)
            dv_ref[...] = dv_acc[...].astype(BF16)

    qblk = pl.BlockSpec((bq, LANE), lambda p, i: (i, p))
    kres = pl.BlockSpec((S, LANE), lambda p, i: (0, p))
    stat = pl.BlockSpec((PAIR, 1, bq), lambda p, i: (p, 0, i))
    return pl.pallas_call(
        body, name=name, grid=(P, nq),
        in_specs=[qblk, kres, kres, stat, qblk],
        out_specs=[qblk, kres, kres],
        out_shape=[jax.ShapeDtypeStruct((S, C), BF16)] * 3,
        scratch_shapes=[pltpu.VMEM((S, LANE), F32), pltpu.VMEM((S, LANE), F32)],
        compiler_params=_cp(("parallel", "arbitrary"), ATTN_VMEM_LIMIT),
    )(q2, k2, v2, lt, do2)


def _bias_cols(f_cum):
    H, Sk = f_cum.shape
    terms = jnp.stack(_split3(f_cum), axis=-1)
    packed = terms.reshape(H // PAIR, PAIR, Sk, 3).transpose(2, 0, 1, 3).reshape(Sk, H // PAIR, PAIR * 3)
    return jnp.pad(packed, ((0, 0), (0, 0), (0, LANE - PAIR * 3))).reshape(Sk, -1)


def _make_packed_softmax(name, scale, causal, use_f):
    assert _pow2(scale)

    def run_fwd(q16, k16, v16, f_cum):
        q16 = q16 * scale
        bias = _bias_cols(f_cum) if use_f else None
        r = (f_cum * LOG2E)[:, None, :] if use_f else None
        o, lse = _smp_fwd(q16, k16, v16, bias, r, causal, name + "_fwd")
        return o, (q16, k16, v16, o, lse, bias, r)

    def run_bwd(saved, do):
        q16, k16, v16, o, lse, bias, r = saved
        outs = _smp_bwd(q16, k16, v16, o, lse, do, bias, r, scale, causal, name + "_bwd")
        if use_f:
            return outs[0], outs[1], outs[2], outs[3][:, 0, :] - outs[4][:, 0, :]
        return tuple(outs)

    if use_f:
        @jax.custom_vjp
        def attn(q, k, v, f_cum):
            return run_fwd(q, k, v, f_cum)[0]

        attn.defvjp(run_fwd, run_bwd)
    else:
        @jax.custom_vjp
        def attn(q, k, v):
            return run_fwd(q, k, v, None)[0]

        attn.defvjp(lambda q, k, v: run_fwd(q, k, v, None), run_bwd)
    return attn


def _make_packed_sb(name, scale):
    assert _pow2(scale)

    def run_fwd(q16, k16, v16):
        q16 = q16 * scale
        o, lt = _sbp_fwd(q16, k16, v16, name + "_fwd")
        return o, (q16, k16, v16, lt)

    def run_bwd(saved, do):
        q16, k16, v16, lt = saved
        return tuple(_sbp_bwd(q16, k16, v16, lt, do, scale, name + "_bwd"))

    @jax.custom_vjp
    def attn(q, k, v):
        return run_fwd(q, k, v)[0]

    attn.defvjp(run_fwd, run_bwd)
    return attn


def _round_bf16(x):
    return lax.reduce_precision(x, exponent_bits=8, mantissa_bits=7)


def _split3(x):
    hi = _round_bf16(x)
    mid = _round_bf16(x - hi)
    lo = _round_bf16(x - hi - mid)
    return hi.astype(BF16), mid.astype(BF16), lo.astype(BF16)


def _pow2(x):
    m, _ = math.frexp(x)
    return m == 0.5


def _pad_last(x, n):
    return jnp.pad(x, [(0, 0)] * (x.ndim - 1) + [(0, n - x.shape[-1])])


def _layouts(q, k, scale):
    qh = _pad_last(jnp.transpose(q * scale if _pow2(scale) else q, (1, 0, 2)).astype(BF16), LANE)
    kh = _pad_last(jnp.transpose(k, (1, 0, 2)).astype(BF16), LANE)
    return qh, jnp.transpose(qh, (0, 2, 1)), kh, jnp.transpose(kh, (0, 2, 1))


def _make_softmax_attn(name, scale, causal, d):
    pre = _pow2(scale)
    cmul = LOG2E if pre else scale * LOG2E
    gscale = 1.0 if pre else scale

    def run_fwd(q, k, v):
        qn, qT, kn, kT = _layouts(q, k, scale)
        vn = jnp.transpose(v, (1, 0, 2)).astype(BF16)
        oT, lse = _sm_fwd_t(qT, kn, jnp.transpose(vn, (0, 2, 1)), cmul, causal, name + "_fwd")
        return jnp.transpose(oT, (2, 0, 1)), (qn, qT, kn, kT, vn, oT, lse)

    def run_bwd(saved, dout):
        qn, qT, kn, kT, vn, oT, lse = saved
        doT = jnp.transpose(dout, (1, 2, 0))
        do = jnp.transpose(dout, (1, 0, 2)).astype(BF16)
        dqT, dk, dv = _sm_bwd_t(qT, qn, kn, kT, vn, oT, lse, doT, do, cmul, gscale, causal, name + "_bwd")
        dq = jnp.transpose(dqT[:, :d, :], (2, 0, 1))
        if pre:
            dq = dq * scale
        return dq, jnp.transpose(dk[:, :, :d], (1, 0, 2)), jnp.transpose(dv, (1, 0, 2))

    @jax.custom_vjp
    def attn(q, k, v):
        return run_fwd(q, k, v)[0]

    attn.defvjp(run_fwd, run_bwd)
    return attn


def _rope(x, positions):
    half = x.shape[-1] // 2
    inv_freq = ROPE_THETA ** (-jnp.arange(half, dtype=F32) / half)
    ang = positions.astype(F32)[:, None] * inv_freq[None, :]
    ang = ang.reshape((ang.shape[0],) + (1,) * (x.ndim - 2) + (half,))
    cos, sin = jnp.cos(ang), jnp.sin(ang)
    x1, x2 = x[..., :half], x[..., half:]
    return jnp.concatenate([x1 * cos - x2 * sin, x1 * sin + x2 * cos], axis=-1)


def _permute_w_in(w):
    parts = [w[:, _ORIG_OFF[idx]:_ORIG_OFF[idx] + SPLIT_SIZES[idx]] for _, idx in _PERM]
    pad = jnp.zeros((w.shape[0], PROJ_COLS - IN_COLS), w.dtype)
    return jnp.concatenate(parts + [pad], axis=1)


_BF16_PIECES = ("fq", "fk", "fv", "sq", "sk", "sv", "mq")


def _make_ln_proj(name, has_res):
    def split(proj32, proj16):
        out, off = [], 0
        for n, idx in _PERM:
            src = proj16 if n in _BF16_PIECES else proj32
            out.append(src[:, off:off + SPLIT_SIZES[idx]])
            off += SPLIT_SIZES[idx]
        return tuple(out)

    def run_fwd(x, res, g, b, w):
        h, h16, hT16 = _ln_fwd_call(x, res, g, b, name + "_ln_fwd", also16=True)
        w16 = w.astype(BF16)
        proj32, proj16 = _matmul(h16, w16, "nn", name + "_fwd", also16=True)
        return (h, split(proj32, proj16)), (x, res, g, hT16, w16)

    def run_bwd(saved, cts):
        x, res, g, hT16, w16 = saved
        dh, dpieces = cts
        pad = jnp.zeros((x.shape[0], PROJ_COLS - IN_COLS), BF16)
        dy16 = jnp.concatenate([c.astype(BF16) for c in dpieces] + [pad], axis=1)
        da = _matmul(dy16, w16, "nt", name + "_dx")
        dw = _matmul(hT16, dy16, "nn", name + "_dw")
        outs = _ln_bwd_call(dh, x, res, g, name + "_ln_bwd", dy2=da)
        if has_res:
            dx, dr, dg, db = outs
            return dx, dr, dg.reshape(-1), db.reshape(-1), dw
        dx, dg, db = outs
        return dx, dg.reshape(-1), db.reshape(-1), dw

    if has_res:
        @jax.custom_vjp
        def op(x, res, g, b, w):
            return run_fwd(x, res, g, b, w)[0]

        op.defvjp(run_fwd, run_bwd)
    else:
        @jax.custom_vjp
        def op(x, g, b, w):
            return run_fwd(x, None, g, b, w)[0]

        op.defvjp(lambda x, g, b, w: run_fwd(x, None, g, b, w), run_bwd)

    def call(*args):
        h, pieces = op(*args)
        return h, {n: part for (n, _), part in zip(_PERM, pieces)}

    return call


def _trunk_loss(wts, x2d, mem2d, target2d):
    s = x2d.shape[0]
    positions = jnp.arange(s)
    head_scale = HEAD_DIM ** -0.5
    mla_scale = (MLA_NOPE + MLA_ROPE) ** -0.5

    mem_n = _make_ln("ln_mem", False)(mem2d, wts["mem_ln_g"], wts["mem_ln_b"])
    h, y = None, x2d
    for l in range(DEPTH):
        tag = f"l{l}_"
        w_p = _permute_w_in(wts["w_in"][l])
        if l == 0:
            h, p = _make_ln_proj(tag + "proj", False)(y, wts["ln_in_g"], wts["ln_in_b"], w_p)
        else:
            h, p = _make_ln_proj(tag + "proj", True)(y, h, wts["ln_g"][l - 1], wts["ln_b"][l - 1], w_p)

        log_f = jax.nn.log_sigmoid(p["f_logit"] + wts["b_forget"][l])
        f_cum = jnp.cumsum(log_f, axis=0).T
        out_fox = _make_packed_softmax(tag + "fox", head_scale, True, True)(p["fq"], p["fk"], p["fv"], f_cum)

        out_sb = _make_packed_sb(tag + "sb", head_scale)(p["sq"], p["sk"], p["sv"])

        cqn = _make_rms(tag + "rms_q")(p["c_q"], wts["mla_q_norm_g"][l])
        q_mla = _make_mm(tag + "q_up")(cqn, wts["w_mla_q_up"][l]).reshape(s, N_HEADS, MLA_NOPE + MLA_ROPE)
        ckvn = _make_rms(tag + "rms_kv")(p["c_kv"], wts["mla_kv_norm_g"][l])
        kv_mla = _make_mm(tag + "kv_up")(ckvn, wts["w_mla_kv_up"][l]).reshape(s, N_HEADS, MLA_NOPE + MLA_V)
        q_full = jnp.concatenate([q_mla[..., :MLA_NOPE], _rope(q_mla[..., MLA_NOPE:], positions)], axis=-1)
        k_rope = jnp.broadcast_to(_rope(p["k_rot"], positions)[:, None, :], (s, N_HEADS, MLA_ROPE))
        k_full = jnp.concatenate([kv_mla[..., :MLA_NOPE], k_rope], axis=-1)
        out_mla = _make_softmax_attn(tag + "mla", mla_scale, True, MLA_NOPE + MLA_ROPE)(
            q_full, k_full, kv_mla[..., MLA_NOPE:]).reshape(s, GROUP_W)

        mkv = _make_mm(tag + "mem_kv")(mem_n, wts["w_mem_kv"][l])
        out_mem = _make_packed_softmax(tag + "mem", head_scale, False, False)(
            p["mq"], mkv[:, :GROUP_W].astype(BF16), mkv[:, GROUP_W:].astype(BF16))

        mixed = jnp.concatenate([out_fox, out_sb, out_mla, out_mem], axis=-1)
        y = _make_gate_out(tag + "out")(mixed, p["gate"], wts["w_out"][l])

    h = _make_ln(f"l{DEPTH - 1}_ln", True)(y, h, wts["ln_g"][DEPTH - 1], wts["ln_b"][DEPTH - 1])
    return _loss_op(h, target2d)


def _mesh_pos():
    x, y, c = (lax.axis_index(a) for a in MESH_AXES)
    return x, y, c, 4 * x + 2 * y + c


def _peer(x, y, c, mask):
    return (x ^ ((mask >> 2) & 1), y ^ ((mask >> 1) & 1), c ^ (mask & 1))


_ANY = pl.BlockSpec(memory_space=pl.ANY)


def _all_gather(row_shards, stack_shards):
    n_row, n_all = len(row_shards), len(row_shards) + len(stack_shards)
    shards = list(row_shards) + list(stack_shards)

    def body(*refs):
        ins, outs = refs[:n_all], refs[n_all:2 * n_all]
        send_sems, recv_sems, local_sems = refs[2 * n_all:]
        x, y, c, me = _mesh_pos()

        def window(t, slot):
            if t < n_row:
                rows = shards[t].shape[1]
                return outs[t].at[:, pl.ds(slot * rows, rows), :]
            return outs[t].at[slot]

        local = [pltpu.make_async_copy(ins[t], window(t, me), local_sems.at[t]) for t in range(n_all)]
        for cp in local:
            cp.start()
        sends = []
        for mask in range(1, N_DEV):
            for t in range(n_all):
                cp = pltpu.make_async_remote_copy(
                    src_ref=ins[t], dst_ref=window(t, me), send_sem=send_sems.at[t, mask - 1],
                    recv_sem=recv_sems.at[t, mask - 1], device_id=_peer(x, y, c, mask),
                    device_id_type=pl.DeviceIdType.MESH)
                cp.start()
                sends.append(cp)
        for mask in range(1, N_DEV):
            for t in range(n_all):
                pltpu.make_async_remote_copy(
                    src_ref=ins[t], dst_ref=window(t, me ^ mask), send_sem=send_sems.at[t, mask - 1],
                    recv_sem=recv_sems.at[t, mask - 1], device_id=_peer(x, y, c, mask),
                    device_id_type=pl.DeviceIdType.MESH).wait_recv()
        for cp in sends:
            cp.wait_send()
        for cp in local:
            cp.wait()

    out_shape = [jax.ShapeDtypeStruct((a.shape[0], N_DEV * a.shape[1], a.shape[2]), a.dtype) for a in row_shards]
    out_shape += [jax.ShapeDtypeStruct((N_DEV,) + a.shape, a.dtype) for a in stack_shards]
    return pl.pallas_call(
        body, name="all_gather_weights", in_specs=[_ANY] * n_all, out_specs=[_ANY] * n_all, out_shape=out_shape,
        scratch_shapes=[pltpu.SemaphoreType.DMA((n_all, N_DEV - 1)), pltpu.SemaphoreType.DMA((n_all, N_DEV - 1)),
                        pltpu.SemaphoreType.DMA((n_all,))],
    )(*shards)


def _reduce_scatter(row_full, stack_full, bcast):
    n_row, n_stack = len(row_full), len(stack_full)
    n_all = n_row + n_stack + len(bcast)
    fulls = list(row_full) + list(stack_full) + list(bcast)

    def body(*refs):
        ins, outs = refs[:n_all], refs[n_all:2 * n_all]
        send_sems, recv_sems, local_sems = refs[2 * n_all:]
        x, y, c, me = _mesh_pos()

        def part(t, slot):
            if t < n_row:
                rows = fulls[t].shape[1] // N_DEV
                return ins[t].at[:, pl.ds(slot * rows, rows), :]
            if t < n_row + n_stack:
                return ins[t].at[slot]
            return ins[t]

        local = [pltpu.make_async_copy(part(t, me), outs[t].at[me], local_sems.at[t]) for t in range(n_all)]
        for cp in local:
            cp.start()
        sends = []
        for mask in range(1, N_DEV):
            for t in range(n_all):
                cp = pltpu.make_async_remote_copy(
                    src_ref=part(t, me ^ mask), dst_ref=outs[t].at[me], send_sem=send_sems.at[t, mask - 1],
                    recv_sem=recv_sems.at[t, mask - 1], device_id=_peer(x, y, c, mask),
                    device_id_type=pl.DeviceIdType.MESH)
                cp.start()
                sends.append(cp)
        for mask in range(1, N_DEV):
            for t in range(n_all):
                pltpu.make_async_remote_copy(
                    src_ref=part(t, me), dst_ref=outs[t].at[me ^ mask], send_sem=send_sems.at[t, mask - 1],
                    recv_sem=recv_sems.at[t, mask - 1], device_id=_peer(x, y, c, mask),
                    device_id_type=pl.DeviceIdType.MESH).wait_recv()
        for cp in sends:
            cp.wait_send()
        for cp in local:
            cp.wait()

    out_shape = [jax.ShapeDtypeStruct((N_DEV, a.shape[0], a.shape[1] // N_DEV, a.shape[2]), a.dtype) for a in row_full]
    out_shape += [jax.ShapeDtypeStruct(a.shape, a.dtype) for a in stack_full]
    out_shape += [jax.ShapeDtypeStruct((N_DEV,) + a.shape, a.dtype) for a in bcast]
    return pl.pallas_call(
        body, name="reduce_scatter_grads", in_specs=[_ANY] * n_all, out_specs=[_ANY] * n_all, out_shape=out_shape,
        scratch_shapes=[pltpu.SemaphoreType.DMA((n_all, N_DEV - 1)), pltpu.SemaphoreType.DMA((n_all, N_DEV - 1)),
                        pltpu.SemaphoreType.DMA((n_all,))],
    )(*fulls)


def _adamw(slots, w, m, v, name):
    shape = w.shape
    cols = shape[-1]
    rows = math.prod(shape[:-1])
    tr = _pick(rows, (64, 32, 16, 8))
    c1 = 1.0 - ADAM_B1 ** ADAM_STEP
    c2 = 1.0 - ADAM_B2 ** ADAM_STEP

    def body(s_ref, w_ref, m_ref, v_ref, g_ref, d_ref, nm_ref, nv_ref):
        g = s_ref[0].astype(F32)
        for k in range(1, N_DEV):
            g = g + s_ref[k].astype(F32)
        nm = ADAM_B1 * m_ref[...] + (1.0 - ADAM_B1) * g
        nv = ADAM_B2 * v_ref[...] + (1.0 - ADAM_B2) * (g * g)
        g_ref[...] = g
        nm_ref[...] = nm
        nv_ref[...] = nv
        d_ref[...] = -ADAM_LR * ((nm / c1) / (jnp.sqrt(nv / c2) + ADAM_EPS) + ADAM_WD * w_ref[...])

    row = pl.BlockSpec((tr, cols), lambda i: (i, 0))
    out = jax.ShapeDtypeStruct((rows, cols), F32)
    outs = pl.pallas_call(
        body, name=name, grid=(rows // tr,),
        in_specs=[pl.BlockSpec((N_DEV, tr, cols), lambda i: (0, i, 0)), row, row, row],
        out_specs=[row] * 4, out_shape=[out] * 4, compiler_params=_cp(("parallel",)),
    )(slots.reshape(N_DEV, rows, cols), w.reshape(rows, cols), m.reshape(rows, cols), v.reshape(rows, cols))
    return [o.reshape(shape) for o in outs]


_SMALL = ("ln_in_g", "ln_in_b", "mem_ln_g", "mem_ln_b", "b_forget", "mla_q_norm_g", "mla_kv_norm_g", "ln_g", "ln_b")
_ORDER = ("ln_in_g", "ln_in_b", "mem_ln_g", "mem_ln_b", "w_in", "b_forget", "mla_q_norm_g", "w_mla_q_up",
          "mla_kv_norm_g", "w_mla_kv_up", "w_mem_kv", "w_out", "ln_g", "ln_b")


def _pack_small(d):
    flat = jnp.concatenate([d[n].reshape(-1) for n in _SMALL])
    n = flat.shape[0]
    padded = ((n + 8 * LANE - 1) // (8 * LANE)) * (8 * LANE)
    return jnp.pad(flat, (0, padded - n)).reshape(-1, LANE)


def _unpack_small(packed, like):
    flat, out, off = packed.reshape(-1), {}, 0
    for n in _SMALL:
        size = math.prod(like[n].shape)
        out[n] = flat[off:off + size].reshape(like[n].shape)
        off += size
    return out


def _unstack_cols(g):
    n, l, r, c = g.shape
    return g.transpose(1, 2, 0, 3).reshape(l, r, n * c)


def _stack_cols(g):
    l, r, nc = g.shape
    return g.reshape(l, r, N_DEV, nc // N_DEV).transpose(2, 0, 1, 3)


def kernel(x, mem, ln_in_g, ln_in_b, mem_ln_g, mem_ln_b, w_in, b_forget, mla_q_norm_g, w_mla_q_up, mla_kv_norm_g, w_mla_kv_up, w_mem_kv, w_out, ln_g, ln_b, loss_target, m_ln_in_g, m_ln_in_b, m_mem_ln_g, m_mem_ln_b, m_w_in, m_b_forget, m_mla_q_norm_g, m_w_mla_q_up, m_mla_kv_norm_g, m_w_mla_kv_up, m_w_mem_kv, m_w_out, m_ln_g, m_ln_b, v_ln_in_g, v_ln_in_b, v_mem_ln_g, v_mem_ln_b, v_w_in, v_b_forget, v_mla_q_norm_g, v_w_mla_q_up, v_mla_kv_norm_g, v_w_mla_kv_up, v_w_mem_kv, v_w_out, v_ln_g, v_ln_b):
    w_shard = dict(ln_in_g=ln_in_g, ln_in_b=ln_in_b, mem_ln_g=mem_ln_g, mem_ln_b=mem_ln_b, w_in=w_in,
                   b_forget=b_forget, mla_q_norm_g=mla_q_norm_g, w_mla_q_up=w_mla_q_up,
                   mla_kv_norm_g=mla_kv_norm_g, w_mla_kv_up=w_mla_kv_up, w_mem_kv=w_mem_kv, w_out=w_out,
                   ln_g=ln_g, ln_b=ln_b)
    m_shard = dict(ln_in_g=m_ln_in_g, ln_in_b=m_ln_in_b, mem_ln_g=m_mem_ln_g, mem_ln_b=m_mem_ln_b, w_in=m_w_in,
                   b_forget=m_b_forget, mla_q_norm_g=m_mla_q_norm_g, w_mla_q_up=m_w_mla_q_up,
                   mla_kv_norm_g=m_mla_kv_norm_g, w_mla_kv_up=m_w_mla_kv_up, w_mem_kv=m_w_mem_kv, w_out=m_w_out,
                   ln_g=m_ln_g, ln_b=m_ln_b)
    v_shard = dict(ln_in_g=v_ln_in_g, ln_in_b=v_ln_in_b, mem_ln_g=v_mem_ln_g, mem_ln_b=v_mem_ln_b, w_in=v_w_in,
                   b_forget=v_b_forget, mla_q_norm_g=v_mla_q_norm_g, w_mla_q_up=v_w_mla_q_up,
                   mla_kv_norm_g=v_mla_kv_norm_g, w_mla_kv_up=v_w_mla_kv_up, w_mem_kv=v_w_mem_kv, w_out=v_w_out,
                   ln_g=v_ln_g, ln_b=v_ln_b)

    to16 = lambda ws: [a.astype(BF16) for a in ws]
    gathered = _all_gather(to16([w_in, w_mem_kv, w_out]), to16([w_mla_q_up, w_mla_kv_up]))
    g_in, g_mem, g_out, g_qup, g_kvup = [a.astype(F32) for a in gathered]
    full = dict(w_shard)
    full.update(w_in=g_in, w_mem_kv=g_mem, w_out=g_out, w_mla_q_up=_unstack_cols(g_qup),
                w_mla_kv_up=_unstack_cols(g_kvup))

    loss_local, (grad_w, grad_x) = jax.value_and_grad(_trunk_loss, argnums=(0, 1))(
        full, x[0], mem[0], loss_target[0])

    s_in, s_mem, s_out, s_qup, s_kvup, s_small = _reduce_scatter(
        to16([grad_w["w_in"], grad_w["w_mem_kv"], grad_w["w_out"]]),
        to16([_stack_cols(grad_w["w_mla_q_up"]), _stack_cols(grad_w["w_mla_kv_up"])]),
        [_pack_small(grad_w)])

    res = {}
    for name, slots in (("w_in", s_in), ("w_mem_kv", s_mem), ("w_out", s_out), ("w_mla_q_up", s_qup),
                        ("w_mla_kv_up", s_kvup)):
        res[name] = _adamw(slots, w_shard[name], m_shard[name], v_shard[name], "adamw_" + name)
    small = _adamw(s_small, _pack_small(w_shard), _pack_small(m_shard), _pack_small(v_shard), "adamw_small")
    small = [_unpack_small(a, w_shard) for a in small]
    for name in _SMALL:
        res[name] = [a[name] for a in small]

    loss = lax.psum(loss_local, MESH_AXES)
    outs = [loss, grad_x[None]]
    for k in range(4):
        outs += [res[name][k] for name in _ORDER]
    return tuple(outs)
```

```python
import functools
import math

import jax
import jax.numpy as jnp
from jax import lax
from jax.experimental import pallas as pl
from jax.experimental.pallas import tpu as pltpu

F32 = jnp.float32
BF16 = jnp.bfloat16

D_MODEL = 1024
DEPTH = 2
GROUP_W = 256
N_HEADS = 4
HEAD_DIM = 64
MLA_Q_RANK = 256
MLA_KV_RANK = 128
MLA_NOPE = 64
MLA_ROPE = 32
MLA_V = 64
ROPE_THETA = 10000.0
LN_EPS = 1e-5
RMS_EPS = 1e-6
DEEPNORM_ALPHA = (2 * DEPTH) ** 0.25
SPLIT_SIZES = (256, 256, 256, 4, 256, 256, 256, 256, 128, 32, 256, 1024)
IN_COLS = sum(SPLIT_SIZES)
_ORIG_OFF = [sum(SPLIT_SIZES[:i]) for i in range(len(SPLIT_SIZES))]
_PERM = (("fq", 0), ("fk", 1), ("fv", 2), ("sq", 4), ("sk", 5), ("sv", 6), ("c_q", 7), ("c_kv", 8),
         ("mq", 10), ("gate", 11), ("k_rot", 9), ("f_logit", 3))
LANE = 128
PROJ_COLS = ((IN_COLS + LANE - 1) // LANE) * LANE

ADAM_LR = 0.001
ADAM_B1 = 0.9
ADAM_B2 = 0.999
ADAM_EPS = 1e-08
ADAM_WD = 0.01
ADAM_STEP = 10

N_DEV = 8
MESH_AXES = ("x", "y", "c")
VMEM_LIMIT = 48 * 1024 * 1024
ATTN_VMEM_LIMIT = 56 * 1024 * 1024
ATTN_BQ = 512
ATTN_BK = 512
CUMSUM_CHUNK = 256
NEG_BIG = -1e30
LOG2E = math.log2(math.e)
MM_TM, MM_TN, MM_TK, MM_TK_NT = 1024, 1664, 1024, 3328

_NT = (((1,), (1,)), ((), ()))
_NN = (((1,), (0,)), ((), ()))


def _cp(sem, vmem=VMEM_LIMIT):
    return pltpu.CompilerParams(dimension_semantics=sem, vmem_limit_bytes=vmem)


def _dot(a, b, dims=_NN):
    return lax.dot_general(a, b, dims, preferred_element_type=F32)


def _pick(n, cands):
    for c in cands:
        if c <= n and n % c == 0:
            return c
    return n


def _tile(n, cap):
    if n <= cap:
        return n
    best = None
    for d in range(LANE, cap + 1, LANE):
        if n % d == 0:
            best = d
    assert best is not None, (n, cap)
    return best


def _matmul(a, b, mode, name, also16=False):
    if mode == "nn":
        (M, K), (K2, N) = a.shape, b.shape
    else:
        (M, K), (N, K2) = a.shape, b.shape
    assert K == K2 and a.dtype == BF16 and b.dtype == BF16, (a.shape, b.shape, mode)
    tm, tn = _tile(M, MM_TM), _tile(N, MM_TN)
    tk = _tile(K, MM_TK if mode == "nn" else MM_TK_NT)
    nk = K // tk
    dims = _NN if mode == "nn" else _NT

    def body(a_ref, b_ref, *rest):
        o_ref, acc_ref = rest[0], rest[-1]
        part = _dot(a_ref[...], b_ref[...], dims)
        if nk == 1:
            o_ref[...] = part
            if also16:
                rest[1][...] = part.astype(BF16)
        else:
            assert not also16
            k = pl.program_id(2)

            @pl.when(k == 0)
            def _():
                acc_ref[...] = part

            @pl.when(k > 0)
            def _():
                acc_ref[...] += part

            @pl.when(k == nk - 1)
            def _():
                o_ref[...] = acc_ref[...]

    a_spec = pl.BlockSpec((tm, tk), lambda j, i, k: (i, k))
    if mode == "nn":
        b_spec = pl.BlockSpec((tk, tn), lambda j, i, k: (k, j))
    else:
        b_spec = pl.BlockSpec((tn, tk), lambda j, i, k: (j, k))
    acc_shape = (tm, tn) if nk > 1 else (8, LANE)
    o_spec = pl.BlockSpec((tm, tn), lambda j, i, k: (i, j))
    outs = pl.pallas_call(
        body, name=name, grid=(N // tn, M // tm, nk),
        in_specs=[a_spec, b_spec],
        out_specs=[o_spec, o_spec] if also16 else o_spec,
        out_shape=[jax.ShapeDtypeStruct((M, N), F32), jax.ShapeDtypeStruct((M, N), BF16)] if also16
        else jax.ShapeDtypeStruct((M, N), F32),
        scratch_shapes=[pltpu.VMEM(acc_shape, F32)],
        compiler_params=_cp(("parallel", "parallel", "arbitrary")),
    )(a, b)
    return outs


def _make_mm(name):
    @jax.custom_vjp
    def mm(a, w):
        return _matmul(a.astype(BF16), w.astype(BF16), "nn", name + "_fwd")

    def fwd(a, w):
        a16, w16 = a.astype(BF16), w.astype(BF16)
        return _matmul(a16, w16, "nn", name + "_fwd"), (a16, w16)

    def bwd(res, dy):
        a16, w16 = res
        dy16 = dy.astype(BF16)
        da = _matmul(dy16, w16, "nt", name + "_dx")
        dw = _matmul(a16.T, dy16, "nn", name + "_dw")
        return da, dw

    mm.defvjp(fwd, bwd)
    return mm


def _row_tile(rows):
    return _pick(rows, (512, 256, 128, 64, 32, 16, 8))


def _ln_stats(u):
    mu = jnp.mean(u, axis=-1, keepdims=True)
    d = u - mu
    var = jnp.mean(d * d, axis=-1, keepdims=True)
    return d, lax.rsqrt(var + LN_EPS)


def _ln_fwd_call(x, res, g, b, name, also16=False):
    rows, dm = x.shape
    tr = _row_tile(rows)
    has_res = res is not None
    n_in = 2 if has_res else 1

    def body(*refs):
        if has_res:
            u = DEEPNORM_ALPHA * refs[1][...] + refs[0][...]
        else:
            u = refs[0][...]
        g_ref, b_ref = refs[n_in], refs[n_in + 1]
        d, rstd = _ln_stats(u)
        y = d * rstd * g_ref[...] + b_ref[...]
        refs[n_in + 2][...] = y
        if also16:
            y16 = y.astype(BF16)
            refs[n_in + 3][...] = y16
            refs[n_in + 4][...] = y16.T

    row = pl.BlockSpec((tr, dm), lambda i: (i, 0))
    vec = pl.BlockSpec((1, dm), lambda i: (0, 0))
    args = (x, res) if has_res else (x,)
    out_specs, out_shape = [row], [jax.ShapeDtypeStruct((rows, dm), F32)]
    if also16:
        out_specs += [row, pl.BlockSpec((dm, tr), lambda i: (0, i))]
        out_shape += [jax.ShapeDtypeStruct((rows, dm), BF16), jax.ShapeDtypeStruct((dm, rows), BF16)]
    outs = pl.pallas_call(
        body, name=name, grid=(rows // tr,),
        in_specs=[row] * n_in + [vec, vec], out_specs=out_specs, out_shape=out_shape,
        compiler_params=_cp(("parallel",)),
    )(*args, g.reshape(1, dm), b.reshape(1, dm))
    return outs if also16 else outs[0]


def _ln_bwd_call(dy, x, res, g, name, dy2=None):
    rows, dm = x.shape
    tr = _row_tile(rows)
    has_res = res is not None
    two = dy2 is not None

    def body(*refs):
        dy_ref, refs = refs[0], refs[1:]
        if two:
            dy2_ref, refs = refs[0], refs[1:]
        if has_res:
            x_ref, r_ref, g_ref, dx_ref, dr_ref, dg_ref, db_ref = refs
            u = DEEPNORM_ALPHA * r_ref[...] + x_ref[...]
        else:
            x_ref, g_ref, dx_ref, dg_ref, db_ref = refs
            u = x_ref[...]
        i = pl.program_id(0)
        d, rstd = _ln_stats(u)
        xhat = d * rstd
        dyv = dy_ref[...] + dy2_ref[...] if two else dy_ref[...]
        dxh = dyv * g_ref[...]
        m1 = jnp.mean(dxh, axis=-1, keepdims=True)
        m2 = jnp.mean(dxh * xhat, axis=-1, keepdims=True)
        du = rstd * (dxh - m1 - xhat * m2)
        dx_ref[...] = du
        if has_res:
            dr_ref[...] = DEEPNORM_ALPHA * du
        pg = jnp.sum(dyv * xhat, axis=0, keepdims=True)
        pb = jnp.sum(dyv, axis=0, keepdims=True)

        @pl.when(i == 0)
        def _():
            dg_ref[...] = pg
            db_ref[...] = pb

        @pl.when(i > 0)
        def _():
            dg_ref[...] += pg
            db_ref[...] += pb

    row = pl.BlockSpec((tr, dm), lambda i: (i, 0))
    vec = pl.BlockSpec((1, dm), lambda i: (0, 0))
    big = jax.ShapeDtypeStruct((rows, dm), F32)
    small = jax.ShapeDtypeStruct((1, dm), F32)
    args = ((dy, dy2) if two else (dy,)) + ((x, res) if has_res else (x,))
    n_big = 2 if has_res else 1
    outs = pl.pallas_call(
        body, name=name, grid=(rows // tr,),
        in_specs=[row] * len(args) + [vec],
        out_specs=[row] * n_big + [vec, vec],
        out_shape=[big] * n_big + [small, small],
        compiler_params=_cp(("arbitrary",)),
    )(*args, g.reshape(1, dm))
    return outs


def _make_ln(name, has_res):
    if has_res:
        @jax.custom_vjp
        def ln(x, res, g, b):
            return _ln_fwd_call(x, res, g, b, name + "_fwd")

        def fwd(x, res, g, b):
            return ln(x, res, g, b), (x, res, g)

        def bwd(saved, dy):
            x, res, g = saved
            dx, dr, dg, db = _ln_bwd_call(dy, x, res, g, name + "_bwd")
            return dx, dr, dg.reshape(-1), db.reshape(-1)
    else:
        @jax.custom_vjp
        def ln(x, g, b):
            return _ln_fwd_call(x, None, g, b, name + "_fwd")

        def fwd(x, g, b):
            return ln(x, g, b), (x, g)

        def bwd(saved, dy):
            x, g = saved
            dx, dg, db = _ln_bwd_call(dy, x, None, g, name + "_bwd")
            return dx, dg.reshape(-1), db.reshape(-1)

    ln.defvjp(fwd, bwd)
    return ln


def _rms_fwd_call(x, g, name):
    rows, dm = x.shape
    tr = _row_tile(rows)

    def body(x_ref, g_ref, o_ref):
        xv = x_ref[...]
        rstd = lax.rsqrt(jnp.mean(xv * xv, axis=-1, keepdims=True) + RMS_EPS)
        o_ref[...] = xv * rstd * g_ref[...]

    row = pl.BlockSpec((tr, dm), lambda i: (i, 0))
    vec = pl.BlockSpec((1, dm), lambda i: (0, 0))
    return pl.pallas_call(
        body, name=name, grid=(rows // tr,), in_specs=[row, vec], out_specs=row,
        out_shape=jax.ShapeDtypeStruct((rows, dm), F32), compiler_params=_cp(("parallel",)),
    )(x, g.reshape(1, dm))


def _rms_bwd_call(dy, x, g, name):
    rows, dm = x.shape
    tr = _row_tile(rows)

    def body(dy_ref, x_ref, g_ref, dx_ref, dg_ref):
        i = pl.program_id(0)
        xv = x_ref[...]
        dyv = dy_ref[...]
        rstd = lax.rsqrt(jnp.mean(xv * xv, axis=-1, keepdims=True) + RMS_EPS)
        xhat = xv * rstd
        dxh = dyv * g_ref[...]
        m2 = jnp.mean(dxh * xhat, axis=-1, keepdims=True)
        dx_ref[...] = rstd * (dxh - xhat * m2)
        pg = jnp.sum(dyv * xhat, axis=0, keepdims=True)

        @pl.when(i == 0)
        def _():
            dg_ref[...] = pg

        @pl.when(i > 0)
        def _():
            dg_ref[...] += pg

    row = pl.BlockSpec((tr, dm), lambda i: (i, 0))
    vec = pl.BlockSpec((1, dm), lambda i: (0, 0))
    return pl.pallas_call(
        body, name=name, grid=(rows // tr,), in_specs=[row, row, vec], out_specs=[row, vec],
        out_shape=[jax.ShapeDtypeStruct((rows, dm), F32), jax.ShapeDtypeStruct((1, dm), F32)],
        compiler_params=_cp(("arbitrary",)),
    )(dy, x, g.reshape(1, dm))


def _make_rms(name):
    @jax.custom_vjp
    def rms(x, g):
        return _rms_fwd_call(x, g, name + "_fwd")

    def fwd(x, g):
        return rms(x, g), (x, g)

    def bwd(saved, dy):
        x, g = saved
        dx, dg = _rms_bwd_call(dy, x, g, name + "_bwd")
        return dx, dg.reshape(-1)

    rms.defvjp(fwd, bwd)
    return rms


def _sigmoid(x):
    return 1.0 / (1.0 + jnp.exp(-x))


def _gate_fwd_call(mixed, gate, name):
    rows, dm = mixed.shape
    tr = _row_tile(rows)

    def body(m_ref, g_ref, o_ref, oT_ref):
        gv = g_ref[...]
        y16 = (m_ref[...] * (gv * _sigmoid(gv))).astype(BF16)
        o_ref[...] = y16
        oT_ref[...] = y16.T

    row = pl.BlockSpec((tr, dm), lambda i: (i, 0))
    return pl.pallas_call(
        body, name=name, grid=(rows // tr,), in_specs=[row, row],
        out_specs=[row, pl.BlockSpec((dm, tr), lambda i: (0, i))],
        out_shape=[jax.ShapeDtypeStruct((rows, dm), BF16), jax.ShapeDtypeStruct((dm, rows), BF16)],
        compiler_params=_cp(("parallel",)),
    )(mixed, gate)


def _gate_bwd_call(dy, mixed, gate, name):
    rows, dm = mixed.shape
    tr = _row_tile(rows)

    def body(dy_ref, m_ref, g_ref, dm_ref, dg_ref):
        gv = g_ref[...]
        dyv = dy_ref[...]
        sg = _sigmoid(gv)
        dm_ref[...] = dyv * (gv * sg)
        dg_ref[...] = dyv * m_ref[...] * (sg * (1.0 + gv * (1.0 - sg)))

    row = pl.BlockSpec((tr, dm), lambda i: (i, 0))
    out = jax.ShapeDtypeStruct((rows, dm), F32)
    return pl.pallas_call(
        body, name=name, grid=(rows // tr,), in_specs=[row, row, row], out_specs=[row, row],
        out_shape=[out, out], compiler_params=_cp(("parallel",)),
    )(dy, mixed, gate)


def _make_gate_out(name):
    def run_fwd(mixed, gate, w):
        g16, gT16 = _gate_fwd_call(mixed, gate, name + "_gate_fwd")
        w16 = w.astype(BF16)
        return _matmul(g16, w16, "nn", name + "_fwd"), (mixed, gate, gT16, w16)

    def run_bwd(saved, dy):
        mixed, gate, gT16, w16 = saved
        dy16 = dy.astype(BF16)
        dgated = _matmul(dy16, w16, "nt", name + "_dx")
        dmix, dgate = _gate_bwd_call(dgated, mixed, gate, name + "_gate_bwd")
        return dmix, dgate, _matmul(gT16, dy16, "nn", name + "_dw")

    @jax.custom_vjp
    def gate_out(mixed, gate, w):
        return run_fwd(mixed, gate, w)[0]

    gate_out.defvjp(run_fwd, run_bwd)
    return gate_out


def _loss_call(y, t, name):
    rows, dm = y.shape
    tr = _row_tile(rows)

    def body(y_ref, t_ref, l_ref, d_ref):
        i = pl.program_id(0)
        e = y_ref[...] - t_ref[...]
        d_ref[...] = e * (1.0 / dm)
        part = 0.5 * jnp.sum(jnp.mean(e * e, axis=-1, keepdims=True), axis=0, keepdims=True)

        @pl.when(i == 0)
        def _():
            l_ref[...] = part

        @pl.when(i > 0)
        def _():
            l_ref[...] += part

    row = pl.BlockSpec((tr, dm), lambda i: (i, 0))
    one = pl.BlockSpec((1, 1), lambda i: (0, 0))
    return pl.pallas_call(
        body, name=name, grid=(rows // tr,), in_specs=[row, row], out_specs=[one, row],
        out_shape=[jax.ShapeDtypeStruct((1, 1), F32), jax.ShapeDtypeStruct((rows, dm), F32)],
        compiler_params=_cp(("arbitrary",)),
    )(y, t)


@jax.custom_vjp
def _loss_op(y, t):
    return _loss_call(y, t, "loss_head")[0][0, 0]


def _loss_fwd(y, t):
    l, d = _loss_call(y, t, "loss_head")
    return l[0, 0], d


def _loss_bwd(d, ct):
    return ct * d, jnp.zeros_like(d)


_loss_op.defvjp(_loss_fwd, _loss_bwd)


def _attn_blocks(S, Sk):
    bq, bk = min(ATTN_BQ, S), min(ATTN_BK, Sk)
    assert S % bq == 0 and Sk % bk == 0
    return bq, bk


def _valid_t(i, j, bq, bk, strict):
    key = j * bk + lax.broadcasted_iota(jnp.int32, (bk, bq), 0)
    qry = i * bq + lax.broadcasted_iota(jnp.int32, (bk, bq), 1)
    return (key < qry) if strict else (key <= qry)


def _sm_fwd_t(qT, k, vT, cmul, causal, name):
    H, DK, S = qT.shape
    Sk, dv = k.shape[1], vT.shape[1]
    bq, bk = _attn_blocks(S, Sk)
    nq, nkb = S // bq, Sk // bk
    hb = PAIR * FWD_PAIRS if H % (PAIR * FWD_PAIRS) == 0 else 1
    heads = range(hb)
    if causal:
        assert S == Sk and bq == bk

    def body(qT_ref, k_ref, vT_ref, oT_ref, lse_ref):
        i = pl.program_id(1)
        qTs = [qT_ref[w] for w in heads]

        def blk(j, carry, masked):
            off = pl.multiple_of(j * bk, bk)
            sT = [_dot(k_ref[w, pl.ds(off, bk), :], qTs[w]) * cmul for w in heads]
            if masked:
                valid = _valid_t(i, j, bq, bk, False)
                sT = [jnp.where(valid, s, NEG_BIG) for s in sT]
            m_new = [jnp.maximum(carry[w][0], jnp.max(sT[w], axis=0, keepdims=True)) for w in heads]
            p = [jnp.exp2(sT[w] - m_new[w]) for w in heads]
            a = [jnp.exp2(carry[w][0] - m_new[w]) for w in heads]
            l = [a[w] * carry[w][1] + jnp.sum(p[w], axis=0, keepdims=True) for w in heads]
            acc = [a[w] * carry[w][2] + _dot(vT_ref[w, :, pl.ds(off, bk)], p[w].astype(BF16)) for w in heads]
            return tuple((m_new[w], l[w], acc[w]) for w in heads)

        carry = tuple((jnp.full((1, bq), NEG_BIG, F32), jnp.zeros((1, bq), F32), jnp.zeros((dv, bq), F32))
                      for _ in heads)
        if causal:
            carry = lax.fori_loop(0, i, lambda j, c: blk(j, c, False), carry)
            carry = blk(i, carry, True)
        else:
            carry = lax.fori_loop(0, nkb, lambda j, c: blk(j, c, False), carry)
        for w in heads:
            oT_ref[w] = carry[w][2] / carry[w][1]
            lse_ref[w] = carry[w][0] + jnp.log2(carry[w][1])

    qcol = lambda d: pl.BlockSpec((hb, d, bq), lambda h, i: (h, 0, i))
    return pl.pallas_call(
        body, name=name, grid=(H // hb, nq),
        in_specs=[qcol(DK), pl.BlockSpec((hb, Sk, DK), lambda h, i: (h, 0, 0)),
                  pl.BlockSpec((hb, dv, Sk), lambda h, i: (h, 0, 0))],
        out_specs=[qcol(dv), qcol(1)],
        out_shape=[jax.ShapeDtypeStruct((H, dv, S), F32), jax.ShapeDtypeStruct((H, 1, S), F32)],
        compiler_params=_cp(("parallel", "arbitrary"), ATTN_VMEM_LIMIT),
    )(qT, k, vT)


def _sm_bwd_t(qT, qn, k, kT, v, oT, lse, doT, do, cmul, gscale, causal, name):
    H, DK, S = qT.shape
    Sk, dv = k.shape[1], v.shape[2]
    bq, bk = _attn_blocks(S, Sk)
    nq, nkb = S // bq, Sk // bk

    def body(qT_ref, qn_ref, k_ref, kT_ref, v_ref, oT_ref, lse_ref, doT_ref, do_ref, dqT_ref, dk_ref, dv_ref):
        i = pl.program_id(1)

        @pl.when(i == 0)
        def _():
            dk_ref[...] = jnp.zeros_like(dk_ref)
            dv_ref[...] = jnp.zeros_like(dv_ref)

        qTb = qT_ref[...]
        qnb = qn_ref[...]
        dob = do_ref[...]
        doTf = doT_ref[...]
        doTb = doTf.astype(BF16)
        delta = jnp.sum(doTf * oT_ref[...], axis=0, keepdims=True)
        lse = lse_ref[...]

        def blk(j, dq, masked):
            off = pl.multiple_of(j * bk, bk)
            sT = _dot(k_ref[pl.ds(off, bk), :], qTb) * cmul
            if masked:
                sT = jnp.where(_valid_t(i, j, bq, bk, False), sT, NEG_BIG)
            p = jnp.exp2(sT - lse)
            dp = _dot(v_ref[pl.ds(off, bk), :], doTb)
            ds = p * (dp - delta)
            dsb = (ds * gscale).astype(BF16) if gscale != 1.0 else ds.astype(BF16)
            dv_ref[pl.ds(off, bk), :] += _dot(p.astype(BF16), dob)
            dk_ref[pl.ds(off, bk), :] += _dot(dsb, qnb)
            return dq + _dot(kT_ref[:, pl.ds(off, bk)], dsb)

        dq = jnp.zeros((DK, bq), F32)
        if causal:
            dq = lax.fori_loop(0, i, lambda j, c: blk(j, c, False), dq)
            dq = blk(i, dq, True)
        else:
            dq = lax.fori_loop(0, nkb, lambda j, c: blk(j, c, False), dq)
        dqT_ref[...] = dq

    qcol = lambda d: pl.BlockSpec((None, d, bq), lambda h, i: (h, 0, i))
    qrow = lambda d: pl.BlockSpec((None, bq, d), lambda h, i: (h, i, 0))
    krow = lambda d: pl.BlockSpec((None, Sk, d), lambda h, i: (h, 0, 0))
    return pl.pallas_call(
        body, name=name, grid=(H, nq),
        in_specs=[qcol(DK), qrow(DK), krow(DK), pl.BlockSpec((None, DK, Sk), lambda h, i: (h, 0, 0)), krow(dv),
                  qcol(dv), qcol(1), qcol(dv), qrow(dv)],
        out_specs=[qcol(DK), krow(DK), krow(dv)],
        out_shape=[jax.ShapeDtypeStruct((H, DK, S), F32), jax.ShapeDtypeStruct((H, Sk, DK), F32),
                   jax.ShapeDtypeStruct((H, Sk, dv), F32)],
        compiler_params=_cp(("parallel", "arbitrary"), ATTN_VMEM_LIMIT),
    )(qT, qn, k, kT, v, oT, lse, doT, do)


def _tri(n, fn):
    r = lax.broadcasted_iota(jnp.int32, (n, n), 0)
    c = lax.broadcasted_iota(jnp.int32, (n, n), 1)
    return jnp.where(fn(r, c), 1.0, 0.0).astype(BF16)


def _key_cumsum(x, tri2, suffix, base):
    bk = x.shape[0]
    c = min(CUMSUM_CHUNK, bk)
    n = bk // c
    hi32 = lax.bitcast_convert_type(lax.bitcast_convert_type(x, jnp.int32) & jnp.int32(-65536), F32)
    hi = hi32.astype(BF16)
    lo = (x - hi32).astype(BF16)
    tot = [jnp.sum(x[a * c:(a + 1) * c], axis=0, keepdims=True) for a in range(n)]
    outs = []
    for a in range(n):
        row = base
        for t in (tot[a + 1:] if suffix else tot[:a]):
            row = row + t
        stacked = jnp.concatenate([hi[a * c:(a + 1) * c], lo[a * c:(a + 1) * c]], axis=0)
        outs.append(_dot(tri2, stacked) + row)
    total = tot[0]
    for t in tot[1:]:
        total = total + t
    return (outs[0] if n == 1 else jnp.concatenate(outs, axis=0)), total


def _tri2(n, fn):
    t = _tri(n, fn)
    return jnp.concatenate([t, t], axis=1)


def _sb_logs(z):
    neg_abs = lax.bitcast_convert_type(lax.bitcast_convert_type(z, jnp.int32) | jnp.int32(-2 ** 31), F32)
    ls = jnp.minimum(z, 0.0) - jnp.log(1.0 + jnp.exp(neg_abs))
    return ls, ls - z


PAIR = LANE // HEAD_DIM
FWD_PAIRS = 2


def _head_lanes(shape, w, axis):
    idx = lax.broadcasted_iota(jnp.int32, shape, axis)
    return (idx >= HEAD_DIM * w) & (idx < HEAD_DIM * (w + 1))


def _bias_rows(w, bq):
    row = lax.broadcasted_iota(jnp.int32, (LANE, bq), 0)
    return jnp.where((row >= 3 * w) & (row < 3 * w + 3), -1.0, 0.0).astype(BF16)


def _merge_pair(parts):
    return jnp.where(_head_lanes(parts[0].shape, 0, 0), parts[0], parts[1]).T


def _smp_fwd(q2, k2, v2, bias, r, causal, name):
    S, C = q2.shape
    Sk = k2.shape[0]
    bq, bk = _attn_blocks(S, Sk)
    nq, nkb, P = S // bq, Sk // bk, C // LANE
    gp = FWD_PAIRS if P % FWD_PAIRS == 0 else 1
    use_f = bias is not None
    if causal:
        assert S == Sk and bq == bk

    def body(*refs):
        if use_f:
            q_ref, k_ref, v_ref, b_ref, r_ref, o_ref, lse_ref = refs
        else:
            q_ref, k_ref, v_ref, o_ref, lse_ref = refs
        i = pl.program_id(1)
        heads = range(PAIR * gp)
        lanes = [slice(LANE * (h // PAIR), LANE * (h // PAIR + 1)) for h in heads]
        qps = [q_ref[:, lanes[h]] for h in heads]
        qTs = [jnp.where(_head_lanes(qps[h].shape, h % PAIR, 1), qps[h], jnp.zeros_like(qps[h])).T for h in heads]
        if use_f:
            qTs = [jnp.concatenate([qTs[h], _bias_rows(h % PAIR, bq)], axis=0) for h in heads]

        def blk(j, carry, masked):
            off = pl.multiple_of(j * bk, bk)
            kbs = [k_ref[pl.ds(off, bk), LANE * g:LANE * (g + 1)] for g in range(gp)]
            if use_f:
                kbs = [jnp.concatenate([kbs[g], b_ref[pl.ds(off, bk), LANE * g:LANE * (g + 1)]], axis=1)
                       for g in range(gp)]
            vTbs = [v_ref[pl.ds(off, bk), LANE * g:LANE * (g + 1)].T for g in range(gp)]
            sT = [_dot(kbs[h // PAIR], qTs[h]) * LOG2E for h in heads]
            if masked:
                valid = _valid_t(i, j, bq, bk, False)
                sT = [jnp.where(valid, s, NEG_BIG) for s in sT]
            cm = [jnp.max(s, axis=0, keepdims=True) for s in sT]
            if use_f:
                cm = [cm[h] + r_ref[h] for h in heads]
            m_new = [jnp.maximum(carry[h][0], cm[h]) for h in heads]
            shift = [(m_new[h] - r_ref[h]) if use_f else m_new[h] for h in heads]
            p = [jnp.exp2(sT[h] - shift[h]) for h in heads]
            a = [jnp.exp2(carry[h][0] - m_new[h]) for h in heads]
            l = [a[h] * carry[h][1] + jnp.sum(p[h], axis=0, keepdims=True) for h in heads]
            acc = [a[h] * carry[h][2] + _dot(vTbs[h // PAIR], p[h].astype(BF16)) for h in heads]
            return tuple((m_new[h], l[h], acc[h]) for h in heads)

        carry = tuple((jnp.full((1, bq), NEG_BIG, F32), jnp.zeros((1, bq), F32), jnp.zeros((LANE, bq), F32))
                      for _ in heads)
        if causal:
            carry = lax.fori_loop(0, i, lambda j, c: blk(j, c, False), carry)
            carry = blk(i, carry, True)
        else:
            carry = lax.fori_loop(0, nkb, lambda j, c: blk(j, c, False), carry)
        for h in heads:
            lse_ref[h] = carry[h][0] + jnp.log2(carry[h][1])
        for g in range(gp):
            o_ref[:, LANE * g:LANE * (g + 1)] = _merge_pair(
                [carry[h][2] / carry[h][1] for h in range(PAIR * g, PAIR * (g + 1))])

    qblk = pl.BlockSpec((bq, LANE * gp), lambda p, i: (i, p))
    kres = pl.BlockSpec((Sk, LANE * gp), lambda p, i: (0, p))
    stat = pl.BlockSpec((PAIR * gp, 1, bq), lambda p, i: (p, 0, i))
    in_specs = [qblk, kres, kres]
    args = [q2, k2, v2]
    if use_f:
        in_specs += [kres, stat]
        args += [bias, r]
    return pl.pallas_call(
        body, name=name, grid=(P // gp, nq), in_specs=in_specs, out_specs=[qblk, stat],
        out_shape=[jax.ShapeDtypeStruct((S, C), F32), jax.ShapeDtypeStruct((PAIR * P, 1, S), F32)],
        compiler_params=_cp(("parallel", "arbitrary"), ATTN_VMEM_LIMIT),
    )(*args)


def _smp_bwd(q2, k2, v2, o2, lse, do2, bias, r, scale, causal, name):
    S, C = q2.shape
    Sk = k2.shape[0]
    bq, bk = _attn_blocks(S, Sk)
    nq, nkb, P = S // bq, Sk // bk, C // LANE
    use_f = bias is not None

    def body(*refs):
        if use_f:
            (q_ref, k_ref, v_ref, o_ref, lse_ref, do_ref, b_ref, r_ref,
             dq_ref, dk_ref, dv_ref, dr_ref, dkey_ref, dk_acc, dv_acc, db_ref) = refs
        else:
            q_ref, k_ref, v_ref, o_ref, lse_ref, do_ref, dq_ref, dk_ref, dv_ref, dk_acc, dv_acc = refs
        i = pl.program_id(1)

        @pl.when(i == 0)
        def _():
            dk_acc[...] = jnp.zeros_like(dk_acc)
            dv_acc[...] = jnp.zeros_like(dv_acc)
            if use_f:
                db_ref[...] = jnp.zeros_like(db_ref)

        qp = q_ref[...]
        dof = do_ref[...]
        prod = dof * o_ref[...]
        heads = range(PAIR)
        mine = [_head_lanes(qp.shape, w, 1) for w in heads]
        qz = [jnp.where(mine[w], qp, jnp.zeros_like(qp)) for w in heads]
        qTs = [qz[w].T for w in heads]
        if use_f:
            qTs = [jnp.concatenate([qTs[w], _bias_rows(w, bq)], axis=0) for w in heads]
        doz = [jnp.where(mine[w], dof, 0.0).astype(BF16) for w in heads]
        doT = [doz[w].T for w in heads]
        delta = [jnp.sum(jnp.where(mine[w], prod, 0.0).T, axis=0, keepdims=True) for w in heads]
        shift = [(lse_ref[w] - r_ref[w]) if use_f else lse_ref[w] for w in heads]

        def blk(j, carry, masked):
            off = pl.multiple_of(j * bk, bk)
            kb = k_ref[pl.ds(off, bk), :]
            kTb = kb.T
            if use_f:
                kb = jnp.concatenate([kb, b_ref[pl.ds(off, bk), :]], axis=1)
            vb = v_ref[pl.ds(off, bk), :]
            sT = [_dot(kb, qTs[w]) * LOG2E for w in heads]
            if masked:
                valid = _valid_t(i, j, bq, bk, False)
                sT = [jnp.where(valid, s, NEG_BIG) for s in sT]
            p = [jnp.exp2(sT[w] - shift[w]) for w in heads]
            dp = [_dot(vb, doT[w]) for w in heads]
            ds = [p[w] * (dp[w] - delta[w]) for w in heads]
            dsb = [d.astype(BF16) for d in ds]
            dvs = [_dot(p[w].astype(BF16), doz[w]) for w in heads]
            dks = [_dot(dsb[w], qz[w]) for w in heads]
            dv_acc[pl.ds(off, bk), :] += dvs[0] + dvs[1]
            dk_acc[pl.ds(off, bk), :] += dks[0] + dks[1]
            dr = [carry[w][1] for w in heads]
            if use_f:
                dr = [dr[w] + jnp.sum(ds[w], axis=0, keepdims=True) for w in heads]
                lane = lax.broadcasted_iota(jnp.int32, (bk, LANE), 1)
                cols = [jnp.where(lane == w, jnp.sum(ds[w], axis=1, keepdims=True), 0.0) for w in heads]
                db_ref[pl.ds(off, bk), :] += cols[0] + cols[1]
            dq = [carry[w][0] + _dot(kTb, dsb[w]) for w in heads]
            return tuple((dq[w], dr[w]) for w in heads)

        carry = tuple((jnp.zeros((LANE, bq), F32), jnp.zeros((1, bq), F32)) for _ in heads)
        if causal:
            carry = lax.fori_loop(0, i, lambda j, c: blk(j, c, False), carry)
            carry = blk(i, carry, True)
        else:
            carry = lax.fori_loop(0, nkb, lambda j, c: blk(j, c, False), carry)
        if use_f:
            for w in heads:
                dr_ref[w] = carry[w][1]
        dq_ref[...] = (_merge_pair([carry[w][0] for w in heads]) * scale).astype(BF16)

        @pl.when(i == nq - 1)
        def _():
            dk_ref[...] = dk_acc[...].astype(BF16)
            dv_ref[...] = dv_acc[...].astype(BF16)

        if use_f:
            @pl.when(i == nq - 1)
            def _():
                def chunk(cidx, carry):
                    off = pl.multiple_of(cidx * LANE, LANE)
                    t = db_ref[pl.ds(off, LANE), :].T
                    for w in range(PAIR):
                        dkey_ref[w, :, pl.ds(off, LANE)] = t[w:w + 1, :]
                    return carry

                lax.fori_loop(0, Sk // LANE, chunk, 0)

    qblk = pl.BlockSpec((bq, LANE), lambda p, i: (i, p))
    kres = pl.BlockSpec((Sk, LANE), lambda p, i: (0, p))
    stat = pl.BlockSpec((PAIR, 1, bq), lambda p, i: (p, 0, i))
    in_specs = [qblk, kres, kres, qblk, stat, qblk]
    args = [q2, k2, v2, o2, lse, do2]
    out_specs = [qblk, kres, kres]
    out_shape = [jax.ShapeDtypeStruct((S, C), BF16), jax.ShapeDtypeStruct((Sk, C), BF16),
                 jax.ShapeDtypeStruct((Sk, C), BF16)]
    scratch = [pltpu.VMEM((Sk, LANE), F32), pltpu.VMEM((Sk, LANE), F32)]
    if use_f:
        in_specs += [kres, stat]
        args += [bias, r]
        out_specs += [stat, pl.BlockSpec((PAIR, 1, Sk), lambda p, i: (p, 0, 0))]
        out_shape += [jax.ShapeDtypeStruct((PAIR * P, 1, S), F32), jax.ShapeDtypeStruct((PAIR * P, 1, Sk), F32)]
        scratch.append(pltpu.VMEM((Sk, LANE), F32))
    return pl.pallas_call(
        body, name=name, grid=(P, nq), in_specs=in_specs, out_specs=out_specs, out_shape=out_shape,
        scratch_shapes=scratch, compiler_params=_cp(("parallel", "arbitrary"), ATTN_VMEM_LIMIT),
    )(*args)


def _sbp_fwd(q2, k2, v2, name):
    S, C = q2.shape
    bq, bk = _attn_blocks(S, S)
    assert bq == bk
    nq, P = S // bq, C // LANE
    gp = FWD_PAIRS if P % FWD_PAIRS == 0 else 1
    c = min(CUMSUM_CHUNK, bk)

    def body(q_ref, k_ref, v_ref, o_ref, lt_ref):
        i = pl.program_id(1)
        after = _tri2(c, lambda s, j: j > s)
        heads = range(PAIR * gp)
        qps = [q_ref[:, LANE * (h // PAIR):LANE * (h // PAIR + 1)] for h in heads]
        qTs = [jnp.where(_head_lanes(qps[h].shape, h % PAIR, 1), qps[h], jnp.zeros_like(qps[h])).T for h in heads]

        def blk(jj, carry, masked):
            j = i - jj
            off = pl.multiple_of(j * bk, bk)
            kbs = [k_ref[pl.ds(off, bk), LANE * g:LANE * (g + 1)] for g in range(gp)]
            vTbs = [v_ref[pl.ds(off, bk), LANE * g:LANE * (g + 1)].T for g in range(gp)]
            logs = [_sb_logs(_dot(kbs[h // PAIR], qTs[h])) for h in heads]
            ls, lk = [t[0] for t in logs], [t[1] for t in logs]
            if masked:
                valid = _valid_t(i, j, bq, bk, True)
                lk = [jnp.where(valid, t, 0.0) for t in lk]
            cs = [_key_cumsum(lk[h], after, True, carry[h][0]) for h in heads]
            wgt = [jnp.exp(ls[h] + cs[h][0]) for h in heads]
            if masked:
                wgt = [jnp.where(valid, t, 0.0) for t in wgt]
            acc = [carry[h][1] + _dot(vTbs[h // PAIR], wgt[h].astype(BF16)) for h in heads]
            return tuple((carry[h][0] + cs[h][1], acc[h]) for h in heads)

        carry = tuple((jnp.zeros((1, bq), F32), jnp.zeros((LANE, bq), F32)) for _ in heads)
        carry = blk(0, carry, True)
        carry = lax.fori_loop(1, i + 1, lambda jj, cr: blk(jj, cr, False), carry)
        for h in heads:
            lt_ref[h] = carry[h][0]
        for g in range(gp):
            o_ref[:, LANE * g:LANE * (g + 1)] = _merge_pair([carry[h][1] for h in range(PAIR * g, PAIR * (g + 1))])

    qblk = pl.BlockSpec((bq, LANE * gp), lambda p, i: (i, p))
    kres = pl.BlockSpec((S, LANE * gp), lambda p, i: (0, p))
    stat = pl.BlockSpec((PAIR * gp, 1, bq), lambda p, i: (p, 0, i))
    return pl.pallas_call(
        body, name=name, grid=(P // gp, nq),
        in_specs=[qblk, kres, kres],
        out_specs=[qblk, stat],
        out_shape=[jax.ShapeDtypeStruct((S, C), F32), jax.ShapeDtypeStruct((PAIR * P, 1, S), F32)],
        compiler_params=_cp(("parallel", "arbitrary"), ATTN_VMEM_LIMIT),
    )(q2, k2, v2)


def _sbp_bwd(q2, k2, v2, lt, do2, scale, name):
    S, C = q2.shape
    bq, bk = _attn_blocks(S, S)
    nq, P = S // bq, C // LANE
    c = min(CUMSUM_CHUNK, bk)

    def body(q_ref, k_ref, v_ref, lt_ref, do_ref, dq_ref, dk_ref, dv_ref, dk_acc, dv_acc):
        i = pl.program_id(1)

        @pl.when(i == 0)
        def _():
            dk_acc[...] = jnp.zeros_like(dk_acc)
            dv_acc[...] = jnp.zeros_like(dv_acc)

        qp = q_ref[...]
        dof = do_ref[...]
        upto = _tri2(c, lambda s, j: j <= s)
        before = _tri2(c, lambda s, j: j < s)
        heads = range(PAIR)
        mine = [_head_lanes(qp.shape, w, 1) for w in heads]
        qz = [jnp.where(mine[w], qp, jnp.zeros_like(qp)) for w in heads]
        qTs = [qz[w].T for w in heads]
        doz = [jnp.where(mine[w], dof, 0.0).astype(BF16) for w in heads]
        doT = [doz[w].T for w in heads]
        ltot = [lt_ref[w] for w in heads]

        def blk(j, carry, masked):
            off = pl.multiple_of(j * bk, bk)
            kb = k_ref[pl.ds(off, bk), :]
            vb = v_ref[pl.ds(off, bk), :]
            kTb = kb.T
            logs = [_sb_logs(_dot(kb, qTs[w])) for w in heads]
            ls, lk = [t[0] for t in logs], [t[1] for t in logs]
            if masked:
                valid = _valid_t(i, j, bq, bk, True)
                lk = [jnp.where(valid, t, 0.0) for t in lk]
            pin = [_key_cumsum(lk[w], upto, False, carry[w][1] - ltot[w]) for w in heads]
            wgt = [jnp.exp(ls[w] - pin[w][0]) for w in heads]
            if masked:
                wgt = [jnp.where(valid, t, 0.0) for t in wgt]
            g = [_dot(vb, doT[w]) * wgt[w] for w in heads]
            cin = [_key_cumsum(g[w], before, False, carry[w][2]) for w in heads]
            sig = [jnp.exp(t) for t in ls]
            dz = [g[w] * (1.0 - sig[w]) - cin[w][0] * sig[w] for w in heads]
            if masked:
                dz = [jnp.where(valid, t, 0.0) for t in dz]
            dzb = [t.astype(BF16) for t in dz]
            dvs = [_dot(wgt[w].astype(BF16), doz[w]) for w in heads]
            dks = [_dot(dzb[w], qz[w]) for w in heads]
            dv_acc[pl.ds(off, bk), :] += dvs[0] + dvs[1]
            dk_acc[pl.ds(off, bk), :] += dks[0] + dks[1]
            return tuple((carry[w][0] + _dot(kTb, dzb[w]), carry[w][1] + pin[w][1], carry[w][2] + cin[w][1])
                         for w in heads)

        carry = tuple((jnp.zeros((LANE, bq), F32), jnp.zeros((1, bq), F32), jnp.zeros((1, bq), F32)) for _ in heads)
        carry = lax.fori_loop(0, i, lambda j, cr: blk(j, cr, False), carry)
        carry = blk(i, carry, True)
        dq_ref[...] = (_merge_pair([carry[w][0] for w in heads]) * scale).astype(BF16)

        @pl.when(i == nq - 1)
        def _():
            dk_ref[...] = dk_acc[...].astype(BF16)
            dv_ref[...] = dv_acc[...].astype(BF16)

    qblk = pl.BlockSpec((bq, LANE), lambda p, i: (i, p))
    kres = pl.BlockSpec((S, LANE), lambda p, i: (0, p))
    stat = pl.BlockSpec((PAIR, 1, bq), lambda p, i: (p, 0, i))
    return pl.pallas_call(
        body, name=name, grid=(P, nq),
        in_specs=[qblk, kres, kres, stat, qblk],
        out_specs=[qblk, kres, kres],
        out_shape=[jax.ShapeDtypeStruct((S, C), BF16)] * 3,
        scratch_shapes=[pltpu.VMEM((S, LANE), F32), pltpu.VMEM((S, LANE), F32)],
        compiler_params=_cp(("parallel", "arbitrary"), ATTN_VMEM_LIMIT),
    )(q2, k2, v2, lt, do2)


def _bias_cols(f_cum):
    H, Sk = f_cum.shape
    terms = jnp.stack(_split3(f_cum), axis=-1)
    packed = terms.reshape(H // PAIR, PAIR, Sk, 3).transpose(2, 0, 1, 3).reshape(Sk, H // PAIR, PAIR * 3)
    return jnp.pad(packed, ((0, 0), (0, 0), (0, LANE - PAIR * 3))).reshape(Sk, -1)


def _make_packed_softmax(name, scale, causal, use_f):
    assert _pow2(scale)

    def run_fwd(q16, k16, v16, f_cum):
        q16 = q16 * scale
        bias = _bias_cols(f_cum) if use_f else None
        r = (f_cum * LOG2E)[:, None, :] if use_f else None
        o, lse = _smp_fwd(q16, k16, v16, bias, r, causal, name + "_fwd")
        return o, (q16, k16, v16, o, lse, bias, r)

    def run_bwd(saved, do):
        q16, k16, v16, o, lse, bias, r = saved
        outs = _smp_bwd(q16, k16, v16, o, lse, do, bias, r, scale, causal, name + "_bwd")
        if use_f:
            return outs[0], outs[1], outs[2], outs[3][:, 0, :] - outs[4][:, 0, :]
        return tuple(outs)

    if use_f:
        @jax.custom_vjp
        def attn(q, k, v, f_cum):
            return run_fwd(q, k, v, f_cum)[0]

        attn.defvjp(run_fwd, run_bwd)
    else:
        @jax.custom_vjp
        def attn(q, k, v):
            return run_fwd(q, k, v, None)[0]

        attn.defvjp(lambda q, k, v: run_fwd(q, k, v, None), run_bwd)
    return attn


def _make_packed_sb(name, scale):
    assert _pow2(scale)

    def run_fwd(q16, k16, v16):
        q16 = q16 * scale
        o, lt = _sbp_fwd(q16, k16, v16, name + "_fwd")
        return o, (q16, k16, v16, lt)

    def run_bwd(saved, do):
        q16, k16, v16, lt = saved
        return tuple(_sbp_bwd(q16, k16, v16, lt, do, scale, name + "_bwd"))

    @jax.custom_vjp
    def attn(q, k, v):
        return run_fwd(q, k, v)[0]

    attn.defvjp(run_fwd, run_bwd)
    return attn


def _round_bf16(x):
    return lax.reduce_precision(x, exponent_bits=8, mantissa_bits=7)


def _split3(x):
    hi = _round_bf16(x)
    mid = _round_bf16(x - hi)
    lo = _round_bf16(x - hi - mid)
    return hi.astype(BF16), mid.astype(BF16), lo.astype(BF16)


def _pow2(x):
    m, _ = math.frexp(x)
    return m == 0.5


def _pad_last(x, n):
    return jnp.pad(x, [(0, 0)] * (x.ndim - 1) + [(0, n - x.shape[-1])])


def _layouts(q, k, scale):
    qh = _pad_last(jnp.transpose(q * scale if _pow2(scale) else q, (1, 0, 2)).astype(BF16), LANE)
    kh = _pad_last(jnp.transpose(k, (1, 0, 2)).astype(BF16), LANE)
    return qh, jnp.transpose(qh, (0, 2, 1)), kh, jnp.transpose(kh, (0, 2, 1))


def _make_softmax_attn(name, scale, causal, d):
    pre = _pow2(scale)
    cmul = LOG2E if pre else scale * LOG2E
    gscale = 1.0 if pre else scale

    def run_fwd(q, k, v):
        qn, qT, kn, kT = _layouts(q, k, scale)
        vn = jnp.transpose(v, (1, 0, 2)).astype(BF16)
        oT, lse = _sm_fwd_t(qT, kn, jnp.transpose(vn, (0, 2, 1)), cmul, causal, name + "_fwd")
        return jnp.transpose(oT, (2, 0, 1)), (qn, qT, kn, kT, vn, oT, lse)

    def run_bwd(saved, dout):
        qn, qT, kn, kT, vn, oT, lse = saved
        doT = jnp.transpose(dout, (1, 2, 0))
        do = jnp.transpose(dout, (1, 0, 2)).astype(BF16)
        dqT, dk, dv = _sm_bwd_t(qT, qn, kn, kT, vn, oT, lse, doT, do, cmul, gscale, causal, name + "_bwd")
        dq = jnp.transpose(dqT[:, :d, :], (2, 0, 1))
        if pre:
            dq = dq * scale
        return dq, jnp.transpose(dk[:, :, :d], (1, 0, 2)), jnp.transpose(dv, (1, 0, 2))

    @jax.custom_vjp
    def attn(q, k, v):
        return run_fwd(q, k, v)[0]

    attn.defvjp(run_fwd, run_bwd)
    return attn


def _rope(x, positions):
    half = x.shape[-1] // 2
    inv_freq = ROPE_THETA ** (-jnp.arange(half, dtype=F32) / half)
    ang = positions.astype(F32)[:, None] * inv_freq[None, :]
    ang = ang.reshape((ang.shape[0],) + (1,) * (x.ndim - 2) + (half,))
    cos, sin = jnp.cos(ang), jnp.sin(ang)
    x1, x2 = x[..., :half], x[..., half:]
    return jnp.concatenate([x1 * cos - x2 * sin, x1 * sin + x2 * cos], axis=-1)


def _permute_w_in(w):
    parts = [w[:, _ORIG_OFF[idx]:_ORIG_OFF[idx] + SPLIT_SIZES[idx]] for _, idx in _PERM]
    pad = jnp.zeros((w.shape[0], PROJ_COLS - IN_COLS), w.dtype)
    return jnp.concatenate(parts + [pad], axis=1)


_BF16_PIECES = ("fq", "fk", "fv", "sq", "sk", "sv", "mq")


def _make_ln_proj(name, has_res):
    def split(proj32, proj16):
        out, off = [], 0
        for n, idx in _PERM:
            src = proj16 if n in _BF16_PIECES else proj32
            out.append(src[:, off:off + SPLIT_SIZES[idx]])
            off += SPLIT_SIZES[idx]
        return tuple(out)

    def run_fwd(x, res, g, b, w):
        h, h16, hT16 = _ln_fwd_call(x, res, g, b, name + "_ln_fwd", also16=True)
        w16 = w.astype(BF16)
        proj32, proj16 = _matmul(h16, w16, "nn", name + "_fwd", also16=True)
        return (h, split(proj32, proj16)), (x, res, g, hT16, w16)

    def run_bwd(saved, cts):
        x, res, g, hT16, w16 = saved
        dh, dpieces = cts
        pad = jnp.zeros((x.shape[0], PROJ_COLS - IN_COLS), BF16)
        dy16 = jnp.concatenate([c.astype(BF16) for c in dpieces] + [pad], axis=1)
        da = _matmul(dy16, w16, "nt", name + "_dx")
        dw = _matmul(hT16, dy16, "nn", name + "_dw")
        outs = _ln_bwd_call(dh, x, res, g, name + "_ln_bwd", dy2=da)
        if has_res:
            dx, dr, dg, db = outs
            return dx, dr, dg.reshape(-1), db.reshape(-1), dw
        dx, dg, db = outs
        return dx, dg.reshape(-1), db.reshape(-1), dw

    if has_res:
        @jax.custom_vjp
        def op(x, res, g, b, w):
            return run_fwd(x, res, g, b, w)[0]

        op.defvjp(run_fwd, run_bwd)
    else:
        @jax.custom_vjp
        def op(x, g, b, w):
            return run_fwd(x, None, g, b, w)[0]

        op.defvjp(lambda x, g, b, w: run_fwd(x, None, g, b, w), run_bwd)

    def call(*args):
        h, pieces = op(*args)
        return h, {n: part for (n, _), part in zip(_PERM, pieces)}

    return call


def _trunk_loss(wts, x2d, mem2d, target2d):
    s = x2d.shape[0]
    positions = jnp.arange(s)
    head_scale = HEAD_DIM ** -0.5
    mla_scale = (MLA_NOPE + MLA_ROPE) ** -0.5

    mem_n = _make_ln("ln_mem", False)(mem2d, wts["mem_ln_g"], wts["mem_ln_b"])
    h, y = None, x2d
    for l in range(DEPTH):
        tag = f"l{l}_"
        w_p = _permute_w_in(wts["w_in"][l])
        if l == 0:
            h, p = _make_ln_proj(tag + "proj", False)(y, wts["ln_in_g"], wts["ln_in_b"], w_p)
        else:
            h, p = _make_ln_proj(tag + "proj", True)(y, h, wts["ln_g"][l - 1], wts["ln_b"][l - 1], w_p)

        log_f = jax.nn.log_sigmoid(p["f_logit"] + wts["b_forget"][l])
        f_cum = jnp.cumsum(log_f, axis=0).T
        out_fox = _make_packed_softmax(tag + "fox", head_scale, True, True)(p["fq"], p["fk"], p["fv"], f_cum)

        out_sb = _make_packed_sb(tag + "sb", head_scale)(p["sq"], p["sk"], p["sv"])

        cqn = _make_rms(tag + "rms_q")(p["c_q"], wts["mla_q_norm_g"][l])
        q_mla = _make_mm(tag + "q_up")(cqn, wts["w_mla_q_up"][l]).reshape(s, N_HEADS, MLA_NOPE + MLA_ROPE)
        ckvn = _make_rms(tag + "rms_kv")(p["c_kv"], wts["mla_kv_norm_g"][l])
        kv_mla = _make_mm(tag + "kv_up")(ckvn, wts["w_mla_kv_up"][l]).reshape(s, N_HEADS, MLA_NOPE + MLA_V)
        q_full = jnp.concatenate([q_mla[..., :MLA_NOPE], _rope(q_mla[..., MLA_NOPE:], positions)], axis=-1)
        k_rope = jnp.broadcast_to(_rope(p["k_rot"], positions)[:, None, :], (s, N_HEADS, MLA_ROPE))
        k_full = jnp.concatenate([kv_mla[..., :MLA_NOPE], k_rope], axis=-1)
        out_mla = _make_softmax_attn(tag + "mla", mla_scale, True, MLA_NOPE + MLA_ROPE)(
            q_full, k_full, kv_mla[..., MLA_NOPE:]).reshape(s, GROUP_W)

        mkv = _make_mm(tag + "mem_kv")(mem_n, wts["w_mem_kv"][l])
        out_mem = _make_packed_softmax(tag + "mem", head_scale, False, False)(
            p["mq"], mkv[:, :GROUP_W].astype(BF16), mkv[:, GROUP_W:].astype(BF16))

        mixed = jnp.concatenate([out_fox, out_sb, out_mla, out_mem], axis=-1)
        y = _make_gate_out(tag + "out")(mixed, p["gate"], wts["w_out"][l])

    h = _make_ln(f"l{DEPTH - 1}_ln", True)(y, h, wts["ln_g"][DEPTH - 1], wts["ln_b"][DEPTH - 1])
    return _loss_op(h, target2d)


def _mesh_pos():
    x, y, c = (lax.axis_index(a) for a in MESH_AXES)
    return x, y, c, 4 * x + 2 * y + c


def _peer(x, y, c, mask):
    return (x ^ ((mask >> 2) & 1), y ^ ((mask >> 1) & 1), c ^ (mask & 1))


_ANY = pl.BlockSpec(memory_space=pl.ANY)


def _all_gather(row_shards, stack_shards):
    n_row, n_all = len(row_shards), len(row_shards) + len(stack_shards)
    shards = list(row_shards) + list(stack_shards)

    def body(*refs):
        ins, outs = refs[:n_all], refs[n_all:2 * n_all]
        send_sems, recv_sems, local_sems = refs[2 * n_all:]
        x, y, c, me = _mesh_pos()

        def window(t, slot):
            if t < n_row:
                rows = shards[t].shape[1]
                return outs[t].at[:, pl.ds(slot * rows, rows), :]
            return outs[t].at[slot]

        local = [pltpu.make_async_copy(ins[t], window(t, me), local_sems.at[t]) for t in range(n_all)]
        for cp in local:
            cp.start()
        sends = []
        for mask in range(1, N_DEV):
            for t in range(n_all):
                cp = pltpu.make_async_remote_copy(
                    src_ref=ins[t], dst_ref=window(t, me), send_sem=send_sems.at[t, mask - 1],
                    recv_sem=recv_sems.at[t, mask - 1], device_id=_peer(x, y, c, mask),
                    device_id_type=pl.DeviceIdType.MESH)
                cp.start()
                sends.append(cp)
        for mask in range(1, N_DEV):
            for t in range(n_all):
                pltpu.make_async_remote_copy(
                    src_ref=ins[t], dst_ref=window(t, me ^ mask), send_sem=send_sems.at[t, mask - 1],
                    recv_sem=recv_sems.at[t, mask - 1], device_id=_peer(x, y, c, mask),
                    device_id_type=pl.DeviceIdType.MESH).wait_recv()
        for cp in sends:
            cp.wait_send()
        for cp in local:
            cp.wait()

    out_shape = [jax.ShapeDtypeStruct((a.shape[0], N_DEV * a.shape[1], a.shape[2]), a.dtype) for a in row_shards]
    out_shape += [jax.ShapeDtypeStruct((N_DEV,) + a.shape, a.dtype) for a in stack_shards]
    return pl.pallas_call(
        body, name="all_gather_weights", in_specs=[_ANY] * n_all, out_specs=[_ANY] * n_all, out_shape=out_shape,
        scratch_shapes=[pltpu.SemaphoreType.DMA((n_all, N_DEV - 1)), pltpu.SemaphoreType.DMA((n_all, N_DEV - 1)),
                        pltpu.SemaphoreType.DMA((n_all,))],
    )(*shards)


def _reduce_scatter(row_full, stack_full, bcast):
    n_row, n_stack = len(row_full), len(stack_full)
    n_all = n_row + n_stack + len(bcast)
    fulls = list(row_full) + list(stack_full) + list(bcast)

    def body(*refs):
        ins, outs = refs[:n_all], refs[n_all:2 * n_all]
        send_sems, recv_sems, local_sems = refs[2 * n_all:]
        x, y, c, me = _mesh_pos()

        def part(t, slot):
            if t < n_row:
                rows = fulls[t].shape[1] // N_DEV
                return ins[t].at[:, pl.ds(slot * rows, rows), :]
            if t < n_row + n_stack:
                return ins[t].at[slot]
            return ins[t]

        local = [pltpu.make_async_copy(part(t, me), outs[t].at[me], local_sems.at[t]) for t in range(n_all)]
        for cp in local:
            cp.start()
        sends = []
        for mask in range(1, N_DEV):
            for t in range(n_all):
                cp = pltpu.make_async_remote_copy(
                    src_ref=part(t, me ^ mask), dst_ref=outs[t].at[me], send_sem=send_sems.at[t, mask - 1],
                    recv_sem=recv_sems.at[t, mask - 1], device_id=_peer(x, y, c, mask),
                    device_id_type=pl.DeviceIdType.MESH)
                cp.start()
                sends.append(cp)
        for mask in range(1, N_DEV):
            for t in range(n_all):
                pltpu.make_async_remote_copy(
                    src_ref=part(t, me), dst_ref=outs[t].at[me ^ mask], send_sem=send_sems.at[t, mask - 1],
                    recv_sem=recv_sems.at[t, mask - 1], device_id=_peer(x, y, c, mask),
                    device_id_type=pl.DeviceIdType.MESH).wait_recv()
        for cp in sends:
            cp.wait_send()
        for cp in local:
            cp.wait()

    out_shape = [jax.ShapeDtypeStruct((N_DEV, a.shape[0], a.shape[1] // N_DEV, a.shape[2]), a.dtype) for a in row_full]
    out_shape += [jax.ShapeDtypeStruct(a.shape, a.dtype) for a in stack_full]
    out_shape += [jax.ShapeDtypeStruct((N_DEV,) + a.shape, a.dtype) for a in bcast]
    return pl.pallas_call(
        body, name="reduce_scatter_grads", in_specs=[_ANY] * n_all, out_specs=[_ANY] * n_all, out_shape=out_shape,
        scratch_shapes=[pltpu.SemaphoreType.DMA((n_all, N_DEV - 1)), pltpu.SemaphoreType.DMA((n_all, N_DEV - 1)),
                        pltpu.SemaphoreType.DMA((n_all,))],
    )(*fulls)


def _adamw(slots, w, m, v, name):
    shape = w.shape
    cols = shape[-1]
    rows = math.prod(shape[:-1])
    tr = _pick(rows, (64, 32, 16, 8))
    c1 = 1.0 - ADAM_B1 ** ADAM_STEP
    c2 = 1.0 - ADAM_B2 ** ADAM_STEP

    def body(s_ref, w_ref, m_ref, v_ref, g_ref, d_ref, nm_ref, nv_ref):
        g = s_ref[0].astype(F32)
        for k in range(1, N_DEV):
            g = g + s_ref[k].astype(F32)
        nm = ADAM_B1 * m_ref[...] + (1.0 - ADAM_B1) * g
        nv = ADAM_B2 * v_ref[...] + (1.0 - ADAM_B2) * (g * g)
        g_ref[...] = g
        nm_ref[...] = nm
        nv_ref[...] = nv
        d_ref[...] = -ADAM_LR * ((nm / c1) / (jnp.sqrt(nv / c2) + ADAM_EPS) + ADAM_WD * w_ref[...])

    row = pl.BlockSpec((tr, cols), lambda i: (i, 0))
    out = jax.ShapeDtypeStruct((rows, cols), F32)
    outs = pl.pallas_call(
        body, name=name, grid=(rows // tr,),
        in_specs=[pl.BlockSpec((N_DEV, tr, cols), lambda i: (0, i, 0)), row, row, row],
        out_specs=[row] * 4, out_shape=[out] * 4, compiler_params=_cp(("parallel",)),
    )(slots.reshape(N_DEV, rows, cols), w.reshape(rows, cols), m.reshape(rows, cols), v.reshape(rows, cols))
    return [o.reshape(shape) for o in outs]


_SMALL = ("ln_in_g", "ln_in_b", "mem_ln_g", "mem_ln_b", "b_forget", "mla_q_norm_g", "mla_kv_norm_g", "ln_g", "ln_b")
_ORDER = ("ln_in_g", "ln_in_b", "mem_ln_g", "mem_ln_b", "w_in", "b_forget", "mla_q_norm_g", "w_mla_q_up",
          "mla_kv_norm_g", "w_mla_kv_up", "w_mem_kv", "w_out", "ln_g", "ln_b")


def _pack_small(d):
    flat = jnp.concatenate([d[n].reshape(-1) for n in _SMALL])
    n = flat.shape[0]
    padded = ((n + 8 * LANE - 1) // (8 * LANE)) * (8 * LANE)
    return jnp.pad(flat, (0, padded - n)).reshape(-1, LANE)


def _unpack_small(packed, like):
    flat, out, off = packed.reshape(-1), {}, 0
    for n in _SMALL:
        size = math.prod(like[n].shape)
        out[n] = flat[off:off + size].reshape(like[n].shape)
        off += size
    return out


def _unstack_cols(g):
    n, l, r, c = g.shape
    return g.transpose(1, 2, 0, 3).reshape(l, r, n * c)


def _stack_cols(g):
    l, r, nc = g.shape
    return g.reshape(l, r, N_DEV, nc // N_DEV).transpose(2, 0, 1, 3)


def kernel(x, mem, ln_in_g, ln_in_b, mem_ln_g, mem_ln_b, w_in, b_forget, mla_q_norm_g, w_mla_q_up, mla_kv_norm_g, w_mla_kv_up, w_mem_kv, w_out, ln_g, ln_b, loss_target, m_ln_in_g, m_ln_in_b, m_mem_ln_g, m_mem_ln_b, m_w_in, m_b_forget, m_mla_q_norm_g, m_w_mla_q_up, m_mla_kv_norm_g, m_w_mla_kv_up, m_w_mem_kv, m_w_out, m_ln_g, m_ln_b, v_ln_in_g, v_ln_in_b, v_mem_ln_g, v_mem_ln_b, v_w_in, v_b_forget, v_mla_q_norm_g, v_w_mla_q_up, v_mla_kv_norm_g, v_w_mla_kv_up, v_w_mem_kv, v_w_out, v_ln_g, v_ln_b):
    w_shard = dict(ln_in_g=ln_in_g, ln_in_b=ln_in_b, mem_ln_g=mem_ln_g, mem_ln_b=mem_ln_b, w_in=w_in,
                   b_forget=b_forget, mla_q_norm_g=mla_q_norm_g, w_mla_q_up=w_mla_q_up,
                   mla_kv_norm_g=mla_kv_norm_g, w_mla_kv_up=w_mla_kv_up, w_mem_kv=w_mem_kv, w_out=w_out,
                   ln_g=ln_g, ln_b=ln_b)
    m_shard = dict(ln_in_g=m_ln_in_g, ln_in_b=m_ln_in_b, mem_ln_g=m_mem_ln_g, mem_ln_b=m_mem_ln_b, w_in=m_w_in,
                   b_forget=m_b_forget, mla_q_norm_g=m_mla_q_norm_g, w_mla_q_up=m_w_mla_q_up,
                   mla_kv_norm_g=m_mla_kv_norm_g, w_mla_kv_up=m_w_mla_kv_up, w_mem_kv=m_w_mem_kv, w_out=m_w_out,
                   ln_g=m_ln_g, ln_b=m_ln_b)
    v_shard = dict(ln_in_g=v_ln_in_g, ln_in_b=v_ln_in_b, mem_ln_g=v_mem_ln_g, mem_ln_b=v_mem_ln_b, w_in=v_w_in,
                   b_forget=v_b_forget, mla_q_norm_g=v_mla_q_norm_g, w_mla_q_up=v_w_mla_q_up,
                   mla_kv_norm_g=v_mla_kv_norm_g, w_mla_kv_up=v_w_mla_kv_up, w_mem_kv=v_w_mem_kv, w_out=v_w_out,
                   ln_g=v_ln_g, ln_b=v_ln_b)

    to16 = lambda ws: [a.astype(BF16) for a in ws]
    gathered = _all_gather(to16([w_in, w_mem_kv, w_out]), to16([w_mla_q_up, w_mla_kv_up]))
    g_in, g_mem, g_out, g_qup, g_kvup = [a.astype(F32) for a in gathered]
    full = dict(w_shard)
    full.update(w_in=g_in, w_mem_kv=g_mem, w_out=g_out, w_mla_q_up=_unstack_cols(g_qup),
                w_mla_kv_up=_unstack_cols(g_kvup))

    loss_local, (grad_w, grad_x) = jax.value_and_grad(_trunk_loss, argnums=(0, 1))(
        full, x[0], mem[0], loss_target[0])

    s_in, s_mem, s_out, s_qup, s_kvup, s_small = _reduce_scatter(
        to16([grad_w["w_in"], grad_w["w_mem_kv"], grad_w["w_out"]]),
        to16([_stack_cols(grad_w["w_mla_q_up"]), _stack_cols(grad_w["w_mla_kv_up"])]),
        [_pack_small(grad_w)])

    res = {}
    for name, slots in (("w_in", s_in), ("w_mem_kv", s_mem), ("w_out", s_out), ("w_mla_q_up", s_qup),
                        ("w_mla_kv_up", s_kvup)):
        res[name] = _adamw(slots, w_shard[name], m_shard[name], v_shard[name], "adamw_" + name)
    small = _adamw(s_small, _pack_small(w_shard), _pack_small(m_shard), _pack_small(v_shard), "adamw_small")
    small = [_unpack_small(a, w_shard) for a in small]
    for name in _SMALL:
        res[name] = [a[name] for a in small]

    loss = lax.psum(loss_local, MESH_AXES)
    outs = [loss, grad_x[None]]
    for k in range(4):
        outs += [res[name][k] for name in _ORDER]
    return tuple(outs)
```

```python
import functools
import math

import jax
import jax.numpy as jnp
from jax import lax
from jax.experimental import pallas as pl
from jax.experimental.pallas import tpu as pltpu

F32 = jnp.float32
BF16 = jnp.bfloat16

D_MODEL = 1024
DEPTH = 2
GROUP_W = 256
N_HEADS = 4
HEAD_DIM = 64
MLA_Q_RANK = 256
MLA_KV_RANK = 128
MLA_NOPE = 64
MLA_ROPE = 32
MLA_V = 64
ROPE_THETA = 10000.0
LN_EPS = 1e-5
RMS_EPS = 1e-6
DEEPNORM_ALPHA = (2 * DEPTH) ** 0.25
SPLIT_SIZES = (256, 256, 256, 4, 256, 256, 256, 256, 128, 32, 256, 1024)
IN_COLS = sum(SPLIT_SIZES)
_ORIG_OFF = [sum(SPLIT_SIZES[:i]) for i in range(len(SPLIT_SIZES))]
_PERM = (("fq", 0), ("fk", 1), ("fv", 2), ("sq", 4), ("sk", 5), ("sv", 6), ("c_q", 7), ("c_kv", 8),
         ("mq", 10), ("gate", 11), ("k_rot", 9), ("f_logit", 3))
LANE = 128
PROJ_COLS = ((IN_COLS + LANE - 1) // LANE) * LANE

ADAM_LR = 0.001
ADAM_B1 = 0.9
ADAM_B2 = 0.999
ADAM_EPS = 1e-08
ADAM_WD = 0.01
ADAM_STEP = 10

N_DEV = 8
MESH_AXES = ("x", "y", "c")
VMEM_LIMIT = 48 * 1024 * 1024
ATTN_VMEM_LIMIT = 56 * 1024 * 1024
ATTN_BQ = 512
ATTN_BK = 512
CUMSUM_CHUNK = 256
NEG_BIG = -1e30
LOG2E = math.log2(math.e)
MM_TM, MM_TN, MM_TK, MM_TK_NT = 1024, 1664, 1024, 3328

_NT = (((1,), (1,)), ((), ()))
_NN = (((1,), (0,)), ((), ()))


def _cp(sem, vmem=VMEM_LIMIT):
    return pltpu.CompilerParams(dimension_semantics=sem, vmem_limit_bytes=vmem)


def _dot(a, b, dims=_NN):
    return lax.dot_general(a, b, dims, preferred_element_type=F32)


def _pick(n, cands):
    for c in cands:
        if c <= n and n % c == 0:
            return c
    return n


def _tile(n, cap):
    if n <= cap:
        return n
    best = None
    for d in range(LANE, cap + 1, LANE):
        if n % d == 0:
            best = d
    assert best is not None, (n, cap)
    return best


def _matmul(a, b, mode, name, also16=False):
    if mode == "nn":
        (M, K), (K2, N) = a.shape, b.shape
    else:
        (M, K), (N, K2) = a.shape, b.shape
    assert K == K2 and a.dtype == BF16 and b.dtype == BF16, (a.shape, b.shape, mode)
    tm, tn = _tile(M, MM_TM), _tile(N, MM_TN)
    tk = _tile(K, MM_TK if mode == "nn" else MM_TK_NT)
    nk = K // tk
    dims = _NN if mode == "nn" else _NT

    def body(a_ref, b_ref, *rest):
        o_ref, acc_ref = rest[0], rest[-1]
        part = _dot(a_ref[...], b_ref[...], dims)
        if nk == 1:
            o_ref[...] = part
            if also16:
                rest[1][...] = part.astype(BF16)
        else:
            assert not also16
            k = pl.program_id(2)

            @pl.when(k == 0)
            def _():
                acc_ref[...] = part

            @pl.when(k > 0)
            def _():
                acc_ref[...] += part

            @pl.when(k == nk - 1)
            def _():
                o_ref[...] = acc_ref[...]

    a_spec = pl.BlockSpec((tm, tk), lambda j, i, k: (i, k))
    if mode == "nn":
        b_spec = pl.BlockSpec((tk, tn), lambda j, i, k: (k, j))
    else:
        b_spec = pl.BlockSpec((tn, tk), lambda j, i, k: (j, k))
    acc_shape = (tm, tn) if nk > 1 else (8, LANE)
    o_spec = pl.BlockSpec((tm, tn), lambda j, i, k: (i, j))
    outs = pl.pallas_call(
        body, name=name, grid=(N // tn, M // tm, nk),
        in_specs=[a_spec, b_spec],
        out_specs=[o_spec, o_spec] if also16 else o_spec,
        out_shape=[jax.ShapeDtypeStruct((M, N), F32), jax.ShapeDtypeStruct((M, N), BF16)] if also16
        else jax.ShapeDtypeStruct((M, N), F32),
        scratch_shapes=[pltpu.VMEM(acc_shape, F32)],
        compiler_params=_cp(("parallel", "parallel", "arbitrary")),
    )(a, b)
    return outs


def _make_mm(name):
    @jax.custom_vjp
    def mm(a, w):
        return _matmul(a.astype(BF16), w.astype(BF16), "nn", name + "_fwd")

    def fwd(a, w):
        a16, w16 = a.astype(BF16), w.astype(BF16)
        return _matmul(a16, w16, "nn", name + "_fwd"), (a16, w16)

    def bwd(res, dy):
        a16, w16 = res
        dy16 = dy.astype(BF16)
        da = _matmul(dy16, w16, "nt", name + "_dx")
        dw = _matmul(a16.T, dy16, "nn", name + "_dw")
        return da, dw

    mm.defvjp(fwd, bwd)
    return mm


def _row_tile(rows):
    return _pick(rows, (512, 256, 128, 64, 32, 16, 8))


def _ln_stats(u):
    mu = jnp.mean(u, axis=-1, keepdims=True)
    d = u - mu
    var = jnp.mean(d * d, axis=-1, keepdims=True)
    return d, lax.rsqrt(var + LN_EPS)


def _ln_fwd_call(x, res, g, b, name, also16=False):
    rows, dm = x.shape
    tr = _row_tile(rows)
    has_res = res is not None
    n_in = 2 if has_res else 1

    def body(*refs):
        if has_res:
            u = DEEPNORM_ALPHA * refs[1][...] + refs[0][...]
        else:
            u = refs[0][...]
        g_ref, b_ref = refs[n_in], refs[n_in + 1]
        d, rstd = _ln_stats(u)
        y = d * rstd * g_ref[...] + b_ref[...]
        refs[n_in + 2][...] = y
        if also16:
            y16 = y.astype(BF16)
            refs[n_in + 3][...] = y16
            refs[n_in + 4][...] = y16.T

    row = pl.BlockSpec((tr, dm), lambda i: (i, 0))
    vec = pl.BlockSpec((1, dm), lambda i: (0, 0))
    args = (x, res) if has_res else (x,)
    out_specs, out_shape = [row], [jax.ShapeDtypeStruct((rows, dm), F32)]
    if also16:
        out_specs += [row, pl.BlockSpec((dm, tr), lambda i: (0, i))]
        out_shape += [jax.ShapeDtypeStruct((rows, dm), BF16), jax.ShapeDtypeStruct((dm, rows), BF16)]
    outs = pl.pallas_call(
        body, name=name, grid=(rows // tr,),
        in_specs=[row] * n_in + [vec, vec], out_specs=out_specs, out_shape=out_shape,
        compiler_params=_cp(("parallel",)),
    )(*args, g.reshape(1, dm), b.reshape(1, dm))
    return outs if also16 else outs[0]


def _ln_bwd_call(dy, x, res, g, name, dy2=None):
    rows, dm = x.shape
    tr = _row_tile(rows)
    has_res = res is not None
    two = dy2 is not None

    def body(*refs):
        dy_ref, refs = refs[0], refs[1:]
        if two:
            dy2_ref, refs = refs[0], refs[1:]
        if has_res:
            x_ref, r_ref, g_ref, dx_ref, dr_ref, dg_ref, db_ref = refs
            u = DEEPNORM_ALPHA * r_ref[...] + x_ref[...]
        else:
            x_ref, g_ref, dx_ref, dg_ref, db_ref = refs
            u = x_ref[...]
        i = pl.program_id(0)
        d, rstd = _ln_stats(u)
        xhat = d * rstd
        dyv = dy_ref[...] + dy2_ref[...] if two else dy_ref[...]
        dxh = dyv * g_ref[...]
        m1 = jnp.mean(dxh, axis=-1, keepdims=True)
        m2 = jnp.mean(dxh * xhat, axis=-1, keepdims=True)
        du = rstd * (dxh - m1 - xhat * m2)
        dx_ref[...] = du
        if has_res:
            dr_ref[...] = DEEPNORM_ALPHA * du
        pg = jnp.sum(dyv * xhat, axis=0, keepdims=True)
        pb = jnp.sum(dyv, axis=0, keepdims=True)

        @pl.when(i == 0)
        def _():
            dg_ref[...] = pg
            db_ref[...] = pb

        @pl.when(i > 0)
        def _():
            dg_ref[...] += pg
            db_ref[...] += pb

    row = pl.BlockSpec((tr, dm), lambda i: (i, 0))
    vec = pl.BlockSpec((1, dm), lambda i: (0, 0))
    big = jax.ShapeDtypeStruct((rows, dm), F32)
    small = jax.ShapeDtypeStruct((1, dm), F32)
    args = ((dy, dy2) if two else (dy,)) + ((x, res) if has_res else (x,))
    n_big = 2 if has_res else 1
    outs = pl.pallas_call(
        body, name=name, grid=(rows // tr,),
        in_specs=[row] * len(args) + [vec],
        out_specs=[row] * n_big + [vec, vec],
        out_shape=[big] * n_big + [small, small],
        compiler_params=_cp(("arbitrary",)),
    )(*args, g.reshape(1, dm))
    return outs


def _make_ln(name, has_res):
    if has_res:
        @jax.custom_vjp
        def ln(x, res, g, b):
            return _ln_fwd_call(x, res, g, b, name + "_fwd")

        def fwd(x, res, g, b):
            return ln(x, res, g, b), (x, res, g)

        def bwd(saved, dy):
            x, res, g = saved
            dx, dr, dg, db = _ln_bwd_call(dy, x, res, g, name + "_bwd")
            return dx, dr, dg.reshape(-1), db.reshape(-1)
    else:
        @jax.custom_vjp
        def ln(x, g, b):
            return _ln_fwd_call(x, None, g, b, name + "_fwd")

        def fwd(x, g, b):
            return ln(x, g, b), (x, g)

        def bwd(saved, dy):
            x, g = saved
            dx, dg, db = _ln_bwd_call(dy, x, None, g, name + "_bwd")
            return dx, dg.reshape(-1), db.reshape(-1)

    ln.defvjp(fwd, bwd)
    return ln


def _rms_fwd_call(x, g, name):
    rows, dm = x.shape
    tr = _row_tile(rows)

    def body(x_ref, g_ref, o_ref):
        xv = x_ref[...]
        rstd = lax.rsqrt(jnp.mean(xv * xv, axis=-1, keepdims=True) + RMS_EPS)
        o_ref[...] = xv * rstd * g_ref[...]

    row = pl.BlockSpec((tr, dm), lambda i: (i, 0))
    vec = pl.BlockSpec((1, dm), lambda i: (0, 0))
    return pl.pallas_call(
        body, name=name, grid=(rows // tr,), in_specs=[row, vec], out_specs=row,
        out_shape=jax.ShapeDtypeStruct((rows, dm), F32), compiler_params=_cp(("parallel",)),
    )(x, g.reshape(1, dm))


def _rms_bwd_call(dy, x, g, name):
    rows, dm = x.shape
    tr = _row_tile(rows)

    def body(dy_ref, x_ref, g_ref, dx_ref, dg_ref):
        i = pl.program_id(0)
        xv = x_ref[...]
        dyv = dy_ref[...]
        rstd = lax.rsqrt(jnp.mean(xv * xv, axis=-1, keepdims=True) + RMS_EPS)
        xhat = xv * rstd
        dxh = dyv * g_ref[...]
        m2 = jnp.mean(dxh * xhat, axis=-1, keepdims=True)
        dx_ref[...] = rstd * (dxh - xhat * m2)
        pg = jnp.sum(dyv * xhat, axis=0, keepdims=True)

        @pl.when(i == 0)
        def _():
            dg_ref[...] = pg

        @pl.when(i > 0)
        def _():
            dg_ref[...] += pg

    row = pl.BlockSpec((tr, dm), lambda i: (i, 0))
    vec = pl.BlockSpec((1, dm), lambda i: (0, 0))
    return pl.pallas_call(
        body, name=name, grid=(rows // tr,), in_specs=[row, row, vec], out_specs=[row, vec],
        out_shape=[jax.ShapeDtypeStruct((rows, dm), F32), jax.ShapeDtypeStruct((1, dm), F32)],
        compiler_params=_cp(("arbitrary",)),
    )(dy, x, g.reshape(1, dm))


def _make_rms(name):
    @jax.custom_vjp
    def rms(x, g):
        return _rms_fwd_call(x, g, name + "_fwd")

    def fwd(x, g):
        return rms(x, g), (x, g)

    def bwd(saved, dy):
        x, g = saved
        dx, dg = _rms_bwd_call(dy, x, g, name + "_bwd")
        return dx, dg.reshape(-1)

    rms.defvjp(fwd, bwd)
    return rms


def _sigmoid(x):
    return 1.0 / (1.0 + jnp.exp(-x))


def _gate_fwd_call(mixed, gate, name):
    rows, dm = mixed.shape
    tr = _row_tile(rows)

    def body(m_ref, g_ref, o_ref, oT_ref):
        gv = g_ref[...]
        y16 = (m_ref[...] * (gv * _sigmoid(gv))).astype(BF16)
        o_ref[...] = y16
        oT_ref[...] = y16.T

    row = pl.BlockSpec((tr, dm), lambda i: (i, 0))
    return pl.pallas_call(
        body, name=name, grid=(rows // tr,), in_specs=[row, row],
        out_specs=[row, pl.BlockSpec((dm, tr), lambda i: (0, i))],
        out_shape=[jax.ShapeDtypeStruct((rows, dm), BF16), jax.ShapeDtypeStruct((dm, rows), BF16)],
        compiler_params=_cp(("parallel",)),
    )(mixed, gate)


def _gate_bwd_call(dy, mixed, gate, name):
    rows, dm = mixed.shape
    tr = _row_tile(rows)

    def body(dy_ref, m_ref, g_ref, dm_ref, dg_ref):
        gv = g_ref[...]
        dyv = dy_ref[...]
        sg = _sigmoid(gv)
        dm_ref[...] = dyv * (gv * sg)
        dg_ref[...] = dyv * m_ref[...] * (sg * (1.0 + gv * (1.0 - sg)))

    row = pl.BlockSpec((tr, dm), lambda i: (i, 0))
    out = jax.ShapeDtypeStruct((rows, dm), F32)
    return pl.pallas_call(
        body, name=name, grid=(rows // tr,), in_specs=[row, row, row], out_specs=[row, row],
        out_shape=[out, out], compiler_params=_cp(("parallel",)),
    )(dy, mixed, gate)


def _make_gate_out(name):
    def run_fwd(mixed, gate, w):
        g16, gT16 = _gate_fwd_call(mixed, gate, name + "_gate_fwd")
        w16 = w.astype(BF16)
        return _matmul(g16, w16, "nn", name + "_fwd"), (mixed, gate, gT16, w16)

    def run_bwd(saved, dy):
        mixed, gate, gT16, w16 = saved
        dy16 = dy.astype(BF16)
        dgated = _matmul(dy16, w16, "nt", name + "_dx")
        dmix, dgate = _gate_bwd_call(dgated, mixed, gate, name + "_gate_bwd")
        return dmix, dgate, _matmul(gT16, dy16, "nn", name + "_dw")

    @jax.custom_vjp
    def gate_out(mixed, gate, w):
        return run_fwd(mixed, gate, w)[0]

    gate_out.defvjp(run_fwd, run_bwd)
    return gate_out


def _loss_call(y, t, name):
    rows, dm = y.shape
    tr = _row_tile(rows)

    def body(y_ref, t_ref, l_ref, d_ref):
        i = pl.program_id(0)
        e = y_ref[...] - t_ref[...]
        d_ref[...] = e * (1.0 / dm)
        part = 0.5 * jnp.sum(jnp.mean(e * e, axis=-1, keepdims=True), axis=0, keepdims=True)

        @pl.when(i == 0)
        def _():
            l_ref[...] = part

        @pl.when(i > 0)
        def _():
            l_ref[...] += part

    row = pl.BlockSpec((tr, dm), lambda i: (i, 0))
    one = pl.BlockSpec((1, 1), lambda i: (0, 0))
    return pl.pallas_call(
        body, name=name, grid=(rows // tr,), in_specs=[row, row], out_specs=[one, row],
        out_shape=[jax.ShapeDtypeStruct((1, 1), F32), jax.ShapeDtypeStruct((rows, dm), F32)],
        compiler_params=_cp(("arbitrary",)),
    )(y, t)


@jax.custom_vjp
def _loss_op(y, t):
    return _loss_call(y, t, "loss_head")[0][0, 0]


def _loss_fwd(y, t):
    l, d = _loss_call(y, t, "loss_head")
    return l[0, 0], d


def _loss_bwd(d, ct):
    return ct * d, jnp.zeros_like(d)


_loss_op.defvjp(_loss_fwd, _loss_bwd)


def _attn_blocks(S, Sk):
    bq, bk = min(ATTN_BQ, S), min(ATTN_BK, Sk)
    assert S % bq == 0 and Sk % bk == 0
    return bq, bk


def _valid_t(i, j, bq, bk, strict):
    key = j * bk + lax.broadcasted_iota(jnp.int32, (bk, bq), 0)
    qry = i * bq + lax.broadcasted_iota(jnp.int32, (bk, bq), 1)
    return (key < qry) if strict else (key <= qry)


def _sm_fwd_t(qT, k, vT, cmul, causal, name):
    H, DK, S = qT.shape
    Sk, dv = k.shape[1], vT.shape[1]
    bq, bk = _attn_blocks(S, Sk)
    nq, nkb = S // bq, Sk // bk
    hb = PAIR * FWD_PAIRS if H % (PAIR * FWD_PAIRS) == 0 else 1
    heads = range(hb)
    if causal:
        assert S == Sk and bq == bk

    def body(qT_ref, k_ref, vT_ref, oT_ref, lse_ref):
        i = pl.program_id(1)
        qTs = [qT_ref[w] for w in heads]

        def blk(j, carry, masked):
            off = pl.multiple_of(j * bk, bk)
            sT = [_dot(k_ref[w, pl.ds(off, bk), :], qTs[w]) * cmul for w in heads]
            if masked:
                valid = _valid_t(i, j, bq, bk, False)
                sT = [jnp.where(valid, s, NEG_BIG) for s in sT]
            m_new = [jnp.maximum(carry[w][0], jnp.max(sT[w], axis=0, keepdims=True)) for w in heads]
            p = [jnp.exp2(sT[w] - m_new[w]) for w in heads]
            a = [jnp.exp2(carry[w][0] - m_new[w]) for w in heads]
            l = [a[w] * carry[w][1] + jnp.sum(p[w], axis=0, keepdims=True) for w in heads]
            acc = [a[w] * carry[w][2] + _dot(vT_ref[w, :, pl.ds(off, bk)], p[w].astype(BF16)) for w in heads]
            return tuple((m_new[w], l[w], acc[w]) for w in heads)

        carry = tuple((jnp.full((1, bq), NEG_BIG, F32), jnp.zeros((1, bq), F32), jnp.zeros((dv, bq), F32))
                      for _ in heads)
        if causal:
            carry = lax.fori_loop(0, i, lambda j, c: blk(j, c, False), carry)
            carry = blk(i, carry, True)
        else:
            carry = lax.fori_loop(0, nkb, lambda j, c: blk(j, c, False), carry)
        for w in heads:
            oT_ref[w] = carry[w][2] / carry[w][1]
            lse_ref[w] = carry[w][0] + jnp.log2(carry[w][1])

    qcol = lambda d: pl.BlockSpec((hb, d, bq), lambda h, i: (h, 0, i))
    return pl.pallas_call(
        body, name=name, grid=(H // hb, nq),
        in_specs=[qcol(DK), pl.BlockSpec((hb, Sk, DK), lambda h, i: (h, 0, 0)),
                  pl.BlockSpec((hb, dv, Sk), lambda h, i: (h, 0, 0))],
        out_specs=[qcol(dv), qcol(1)],
        out_shape=[jax.ShapeDtypeStruct((H, dv, S), F32), jax.ShapeDtypeStruct((H, 1, S), F32)],
        compiler_params=_cp(("parallel", "arbitrary"), ATTN_VMEM_LIMIT),
    )(qT, k, vT)


def _sm_bwd_t(qT, qn, k, kT, v, oT, lse, doT, do, cmul, gscale, causal, name):
    H, DK, S = qT.shape
    Sk, dv = k.shape[1], v.shape[2]
    bq, bk = _attn_blocks(S, Sk)
    nq, nkb = S // bq, Sk // bk

    def body(qT_ref, qn_ref, k_ref, kT_ref, v_ref, oT_ref, lse_ref, doT_ref, do_ref, dqT_ref, dk_ref, dv_ref):
        i = pl.program_id(1)

        @pl.when(i == 0)
        def _():
            dk_ref[...] = jnp.zeros_like(dk_ref)
            dv_ref[...] = jnp.zeros_like(dv_ref)

        qTb = qT_ref[...]
        qnb = qn_ref[...]
        dob = do_ref[...]
        doTf = doT_ref[...]
        doTb = doTf.astype(BF16)
        delta = jnp.sum(doTf * oT_ref[...], axis=0, keepdims=True)
        lse = lse_ref[...]

        def blk(j, dq, masked):
            off = pl.multiple_of(j * bk, bk)
            sT = _dot(k_ref[pl.ds(off, bk), :], qTb) * cmul
            if masked:
                sT = jnp.where(_valid_t(i, j, bq, bk, False), sT, NEG_BIG)
            p = jnp.exp2(sT - lse)
            dp = _dot(v_ref[pl.ds(off, bk), :], doTb)
            ds = p * (dp - delta)
            dsb = (ds * gscale).astype(BF16) if gscale != 1.0 else ds.astype(BF16)
            dv_ref[pl.ds(off, bk), :] += _dot(p.astype(BF16), dob)
            dk_ref[pl.ds(off, bk), :] += _dot(dsb, qnb)
            return dq + _dot(kT_ref[:, pl.ds(off, bk)], dsb)

        dq = jnp.zeros((DK, bq), F32)
        if causal:
            dq = lax.fori_loop(0, i, lambda j, c: blk(j, c, False), dq)
            dq = blk(i, dq, True)
        else:
            dq = lax.fori_loop(0, nkb, lambda j, c: blk(j, c, False), dq)
        dqT_ref[...] = dq

    qcol = lambda d: pl.BlockSpec((None, d, bq), lambda h, i: (h, 0, i))
    qrow = lambda d: pl.BlockSpec((None, bq, d), lambda h, i: (h, i, 0))
    krow = lambda d: pl.BlockSpec((None, Sk, d), lambda h, i: (h, 0, 0))
    return pl.pallas_call(
        body, name=name, grid=(H, nq),
        in_specs=[qcol(DK), qrow(DK), krow(DK), pl.BlockSpec((None, DK, Sk), lambda h, i: (h, 0, 0)), krow(dv),
                  qcol(dv), qcol(1), qcol(dv), qrow(dv)],
        out_specs=[qcol(DK), krow(DK), krow(dv)],
        out_shape=[jax.ShapeDtypeStruct((H, DK, S), F32), jax.ShapeDtypeStruct((H, Sk, DK), F32),
                   jax.ShapeDtypeStruct((H, Sk, dv), F32)],
        compiler_params=_cp(("parallel", "arbitrary"), ATTN_VMEM_LIMIT),
    )(qT, qn, k, kT, v, oT, lse, doT, do)


def _tri(n, fn):
    r = lax.broadcasted_iota(jnp.int32, (n, n), 0)
    c = lax.broadcasted_iota(jnp.int32, (n, n), 1)
    return jnp.where(fn(r, c), 1.0, 0.0).astype(BF16)


def _key_cumsum(x, tri2, suffix, base):
    bk = x.shape[0]
    c = min(CUMSUM_CHUNK, bk)
    n = bk // c
    hi32 = lax.bitcast_convert_type(lax.bitcast_convert_type(x, jnp.int32) & jnp.int32(-65536), F32)
    hi = hi32.astype(BF16)
    lo = (x - hi32).astype(BF16)
    tot = [jnp.sum(x[a * c:(a + 1) * c], axis=0, keepdims=True) for a in range(n)]
    outs = []
    for a in range(n):
        row = base
        for t in (tot[a + 1:] if suffix else tot[:a]):
            row = row + t
        stacked = jnp.concatenate([hi[a * c:(a + 1) * c], lo[a * c:(a + 1) * c]], axis=0)
        outs.append(_dot(tri2, stacked) + row)
    total = tot[0]
    for t in tot[1:]:
        total = total + t
    return (outs[0] if n == 1 else jnp.concatenate(outs, axis=0)), total


def _tri2(n, fn):
    t = _tri(n, fn)
    return jnp.concatenate([t, t], axis=1)


def _sb_logs(z):
    neg_abs = lax.bitcast_convert_type(lax.bitcast_convert_type(z, jnp.int32) | jnp.int32(-2 ** 31), F32)
    ls = jnp.minimum(z, 0.0) - jnp.log(1.0 + jnp.exp(neg_abs))
    return ls, ls - z


PAIR = LANE // HEAD_DIM
FWD_PAIRS = 2
SB_DEAD = -110.0


def _head_lanes(shape, w, axis):
    idx = lax.broadcasted_iota(jnp.int32, shape, axis)
    return (idx >= HEAD_DIM * w) & (idx < HEAD_DIM * (w + 1))


def _bias_rows(w, bq):
    row = lax.broadcasted_iota(jnp.int32, (LANE, bq), 0)
    return jnp.where((row >= 3 * w) & (row < 3 * w + 3), -1.0, 0.0).astype(BF16)


def _merge_pair(parts):
    return jnp.where(_head_lanes(parts[0].shape, 0, 0), parts[0], parts[1]).T


def _smp_fwd(q2, k2, v2, bias, r, causal, name):
    S, C = q2.shape
    Sk = k2.shape[0]
    bq, bk = _attn_blocks(S, Sk)
    nq, nkb, P = S // bq, Sk // bk, C // LANE
    gp = FWD_PAIRS if P % FWD_PAIRS == 0 else 1
    use_f = bias is not None
    if causal:
        assert S == Sk and bq == bk

    def body(*refs):
        if use_f:
            q_ref, k_ref, v_ref, b_ref, r_ref, o_ref, lse_ref = refs
        else:
            q_ref, k_ref, v_ref, o_ref, lse_ref = refs
        i = pl.program_id(1)
        heads = range(PAIR * gp)
        lanes = [slice(LANE * (h // PAIR), LANE * (h // PAIR + 1)) for h in heads]
        qps = [q_ref[:, lanes[h]] for h in heads]
        qTs = [jnp.where(_head_lanes(qps[h].shape, h % PAIR, 1), qps[h], jnp.zeros_like(qps[h])).T for h in heads]
        if use_f:
            qTs = [jnp.concatenate([qTs[h], _bias_rows(h % PAIR, bq)], axis=0) for h in heads]

        def blk(j, carry, masked):
            off = pl.multiple_of(j * bk, bk)
            kbs = [k_ref[pl.ds(off, bk), LANE * g:LANE * (g + 1)] for g in range(gp)]
            if use_f:
                kbs = [jnp.concatenate([kbs[g], b_ref[pl.ds(off, bk), LANE * g:LANE * (g + 1)]], axis=1)
                       for g in range(gp)]
            vTbs = [v_ref[pl.ds(off, bk), LANE * g:LANE * (g + 1)].T for g in range(gp)]
            sT = [_dot(kbs[h // PAIR], qTs[h]) * LOG2E for h in heads]
            if masked:
                valid = _valid_t(i, j, bq, bk, False)
                sT = [jnp.where(valid, s, NEG_BIG) for s in sT]
            cm = [jnp.max(s, axis=0, keepdims=True) for s in sT]
            if use_f:
                cm = [cm[h] + r_ref[h] for h in heads]
            m_new = [jnp.maximum(carry[h][0], cm[h]) for h in heads]
            shift = [(m_new[h] - r_ref[h]) if use_f else m_new[h] for h in heads]
            p = [jnp.exp2(sT[h] - shift[h]) for h in heads]
            a = [jnp.exp2(carry[h][0] - m_new[h]) for h in heads]
            l = [a[h] * carry[h][1] + jnp.sum(p[h], axis=0, keepdims=True) for h in heads]
            acc = [a[h] * carry[h][2] + _dot(vTbs[h // PAIR], p[h].astype(BF16)) for h in heads]
            return tuple((m_new[h], l[h], acc[h]) for h in heads)

        carry = tuple((jnp.full((1, bq), NEG_BIG, F32), jnp.zeros((1, bq), F32), jnp.zeros((LANE, bq), F32))
                      for _ in heads)
        if causal:
            carry = lax.fori_loop(0, i, lambda j, c: blk(j, c, False), carry)
            carry = blk(i, carry, True)
        else:
            carry = lax.fori_loop(0, nkb, lambda j, c: blk(j, c, False), carry)
        for h in heads:
            lse_ref[h] = carry[h][0] + jnp.log2(carry[h][1])
        for g in range(gp):
            o_ref[:, LANE * g:LANE * (g + 1)] = _merge_pair(
                [carry[h][2] / carry[h][1] for h in range(PAIR * g, PAIR * (g + 1))])

    qblk = pl.BlockSpec((bq, LANE * gp), lambda p, i: (i, p))
    kres = pl.BlockSpec((Sk, LANE * gp), lambda p, i: (0, p))
    stat = pl.BlockSpec((PAIR * gp, 1, bq), lambda p, i: (p, 0, i))
    in_specs = [qblk, kres, kres]
    args = [q2, k2, v2]
    if use_f:
        in_specs += [kres, stat]
        args += [bias, r]
    return pl.pallas_call(
        body, name=name, grid=(P // gp, nq), in_specs=in_specs, out_specs=[qblk, stat],
        out_shape=[jax.ShapeDtypeStruct((S, C), F32), jax.ShapeDtypeStruct((PAIR * P, 1, S), F32)],
        compiler_params=_cp(("parallel", "arbitrary"), ATTN_VMEM_LIMIT),
    )(*args)


def _smp_bwd(q2, k2, v2, o2, lse, do2, bias, r, scale, causal, name):
    S, C = q2.shape
    Sk = k2.shape[0]
    bq, bk = _attn_blocks(S, Sk)
    nq, nkb, P = S // bq, Sk // bk, C // LANE
    use_f = bias is not None

    def body(*refs):
        if use_f:
            (q_ref, k_ref, v_ref, o_ref, lse_ref, do_ref, b_ref, r_ref,
             dq_ref, dk_ref, dv_ref, dr_ref, dkey_ref, dk_acc, dv_acc, db_ref) = refs
        else:
            q_ref, k_ref, v_ref, o_ref, lse_ref, do_ref, dq_ref, dk_ref, dv_ref, dk_acc, dv_acc = refs
        i = pl.program_id(1)

        @pl.when(i == 0)
        def _():
            dk_acc[...] = jnp.zeros_like(dk_acc)
            dv_acc[...] = jnp.zeros_like(dv_acc)
            if use_f:
                db_ref[...] = jnp.zeros_like(db_ref)

        qp = q_ref[...]
        dof = do_ref[...]
        prod = dof * o_ref[...]
        heads = range(PAIR)
        mine = [_head_lanes(qp.shape, w, 1) for w in heads]
        qz = [jnp.where(mine[w], qp, jnp.zeros_like(qp)) for w in heads]
        qTs = [qz[w].T for w in heads]
        if use_f:
            qTs = [jnp.concatenate([qTs[w], _bias_rows(w, bq)], axis=0) for w in heads]
        doz = [jnp.where(mine[w], dof, 0.0).astype(BF16) for w in heads]
        doT = [doz[w].T for w in heads]
        delta = [jnp.sum(jnp.where(mine[w], prod, 0.0).T, axis=0, keepdims=True) for w in heads]
        shift = [(lse_ref[w] - r_ref[w]) if use_f else lse_ref[w] for w in heads]

        def blk(j, carry, masked):
            off = pl.multiple_of(j * bk, bk)
            kb = k_ref[pl.ds(off, bk), :]
            kTb = kb.T
            if use_f:
                kb = jnp.concatenate([kb, b_ref[pl.ds(off, bk), :]], axis=1)
            vb = v_ref[pl.ds(off, bk), :]
            sT = [_dot(kb, qTs[w]) * LOG2E for w in heads]
            if masked:
                valid = _valid_t(i, j, bq, bk, False)
                sT = [jnp.where(valid, s, NEG_BIG) for s in sT]
            p = [jnp.exp2(sT[w] - shift[w]) for w in heads]
            dp = [_dot(vb, doT[w]) for w in heads]
            ds = [p[w] * (dp[w] - delta[w]) for w in heads]
            dsb = [d.astype(BF16) for d in ds]
            dvs = [_dot(p[w].astype(BF16), doz[w]) for w in heads]
            dks = [_dot(dsb[w], qz[w]) for w in heads]
            dv_acc[pl.ds(off, bk), :] += dvs[0] + dvs[1]
            dk_acc[pl.ds(off, bk), :] += dks[0] + dks[1]
            dr = [carry[w][1] for w in heads]
            if use_f:
                dr = [dr[w] + jnp.sum(ds[w], axis=0, keepdims=True) for w in heads]
                lane = lax.broadcasted_iota(jnp.int32, (bk, LANE), 1)
                cols = [jnp.where(lane == w, jnp.sum(ds[w], axis=1, keepdims=True), 0.0) for w in heads]
                db_ref[pl.ds(off, bk), :] += cols[0] + cols[1]
            dq = [carry[w][0] + _dot(kTb, dsb[w]) for w in heads]
            return tuple((dq[w], dr[w]) for w in heads)

        carry = tuple((jnp.zeros((LANE, bq), F32), jnp.zeros((1, bq), F32)) for _ in heads)
        if causal:
            carry = lax.fori_loop(0, i, lambda j, c: blk(j, c, False), carry)
            carry = blk(i, carry, True)
        else:
            carry = lax.fori_loop(0, nkb, lambda j, c: blk(j, c, False), carry)
        if use_f:
            for w in heads:
                dr_ref[w] = carry[w][1]
        dq_ref[...] = (_merge_pair([carry[w][0] for w in heads]) * scale).astype(BF16)

        @pl.when(i == nq - 1)
        def _():
            dk_ref[...] = dk_acc[...].astype(BF16)
            dv_ref[...] = dv_acc[...].astype(BF16)

        if use_f:
            @pl.when(i == nq - 1)
            def _():
                def chunk(cidx, carry):
                    off = pl.multiple_of(cidx * LANE, LANE)
                    t = db_ref[pl.ds(off, LANE), :].T
                    for w in range(PAIR):
                        dkey_ref[w, :, pl.ds(off, LANE)] = t[w:w + 1, :]
                    return carry

                lax.fori_loop(0, Sk // LANE, chunk, 0)

    qblk = pl.BlockSpec((bq, LANE), lambda p, i: (i, p))
    kres = pl.BlockSpec((Sk, LANE), lambda p, i: (0, p))
    stat = pl.BlockSpec((PAIR, 1, bq), lambda p, i: (p, 0, i))
    in_specs = [qblk, kres, kres, qblk, stat, qblk]
    args = [q2, k2, v2, o2, lse, do2]
    out_specs = [qblk, kres, kres]
    out_shape = [jax.ShapeDtypeStruct((S, C), BF16), jax.ShapeDtypeStruct((Sk, C), BF16),
                 jax.ShapeDtypeStruct((Sk, C), BF16)]
    scratch = [pltpu.VMEM((Sk, LANE), F32), pltpu.VMEM((Sk, LANE), F32)]
    if use_f:
        in_specs += [kres, stat]
        args += [bias, r]
        out_specs += [stat, pl.BlockSpec((PAIR, 1, Sk), lambda p, i: (p, 0, 0))]
        out_shape += [jax.ShapeDtypeStruct((PAIR * P, 1, S), F32), jax.ShapeDtypeStruct((PAIR * P, 1, Sk), F32)]
        scratch.append(pltpu.VMEM((Sk, LANE), F32))
    return pl.pallas_call(
        body, name=name, grid=(P, nq), in_specs=in_specs, out_specs=out_specs, out_shape=out_shape,
        scratch_shapes=scratch, compiler_params=_cp(("parallel", "arbitrary"), ATTN_VMEM_LIMIT),
    )(*args)


def _sbp_fwd(q2, k2, v2, name):
    S, C = q2.shape
    bq, bk = _attn_blocks(S, S)
    assert bq == bk
    nq, P = S // bq, C // LANE
    gp = FWD_PAIRS if P % FWD_PAIRS == 0 else 1
    c = min(CUMSUM_CHUNK, bk)

    def body(q_ref, k_ref, v_ref, o_ref, lt_ref, js_ref):
        i = pl.program_id(1)
        after = _tri2(c, lambda s, j: j > s)
        heads = range(PAIR * gp)
        qps = [q_ref[:, LANE * (h // PAIR):LANE * (h // PAIR + 1)] for h in heads]
        qTs = [jnp.where(_head_lanes(qps[h].shape, h % PAIR, 1), qps[h], jnp.zeros_like(qps[h])).T for h in heads]

        def blk(jj, carry, masked):
            j = i - jj
            off = pl.multiple_of(j * bk, bk)
            kbs = [k_ref[pl.ds(off, bk), LANE * g:LANE * (g + 1)] for g in range(gp)]
            vTbs = [v_ref[pl.ds(off, bk), LANE * g:LANE * (g + 1)].T for g in range(gp)]
            logs = [_sb_logs(_dot(kbs[h // PAIR], qTs[h])) for h in heads]
            ls, lk = [t[0] for t in logs], [t[1] for t in logs]
            if masked:
                valid = _valid_t(i, j, bq, bk, True)
                lk = [jnp.where(valid, t, 0.0) for t in lk]
            cs = [_key_cumsum(lk[h], after, True, carry[h][0]) for h in heads]
            wgt = [jnp.exp(ls[h] + cs[h][0]) for h in heads]
            if masked:
                wgt = [jnp.where(valid, t, 0.0) for t in wgt]
            acc = [carry[h][1] + _dot(vTbs[h // PAIR], wgt[h].astype(BF16)) for h in heads]
            return tuple((carry[h][0] + cs[h][1], acc[h]) for h in heads)

        def step(jj, state):
            carry, first = state
            live = jnp.max(functools.reduce(jnp.maximum, [carry[h][0] for h in heads])) >= SB_DEAD
            carry = lax.cond(live, lambda cr: blk(jj, cr, False), lambda cr: cr, carry)
            return carry, jnp.where(live, i - jj, first)

        carry = tuple((jnp.zeros((1, bq), F32), jnp.zeros((LANE, bq), F32)) for _ in heads)
        carry = blk(0, carry, True)
        carry, first = lax.fori_loop(1, i + 1, step, (carry, i))
        js_ref[0] = jnp.full((1, bq), first, jnp.int32)
        for h in heads:
            lt_ref[h] = carry[h][0]
        for g in range(gp):
            o_ref[:, LANE * g:LANE * (g + 1)] = _merge_pair([carry[h][1] for h in range(PAIR * g, PAIR * (g + 1))])

    qblk = pl.BlockSpec((bq, LANE * gp), lambda p, i: (i, p))
    kres = pl.BlockSpec((S, LANE * gp), lambda p, i: (0, p))
    stat = pl.BlockSpec((PAIR * gp, 1, bq), lambda p, i: (p, 0, i))
    return pl.pallas_call(
        body, name=name, grid=(P // gp, nq),
        in_specs=[qblk, kres, kres],
        out_specs=[qblk, stat, pl.BlockSpec((1, 1, bq), lambda p, i: (p, 0, i))],
        out_shape=[jax.ShapeDtypeStruct((S, C), F32), jax.ShapeDtypeStruct((PAIR * P, 1, S), F32),
                   jax.ShapeDtypeStruct((P // gp, 1, S), jnp.int32)],
        compiler_params=_cp(("parallel", "arbitrary"), ATTN_VMEM_LIMIT),
    )(q2, k2, v2)


def _sbp_bwd(q2, k2, v2, lt, first, do2, scale, name):
    S, C = q2.shape
    bq, bk = _attn_blocks(S, S)
    nq, P = S // bq, C // LANE
    c = min(CUMSUM_CHUNK, bk)

    per_group = P // first.shape[0]

    def body(first_ref, q_ref, k_ref, v_ref, lt_ref, do_ref, dq_ref, dk_ref, dv_ref, dk_acc, dv_acc):
        i = pl.program_id(1)

        @pl.when(i == 0)
        def _():
            dk_acc[...] = jnp.zeros_like(dk_acc)
            dv_acc[...] = jnp.zeros_like(dv_acc)

        qp = q_ref[...]
        dof = do_ref[...]
        upto = _tri2(c, lambda s, j: j <= s)
        before = _tri2(c, lambda s, j: j < s)
        heads = range(PAIR)
        mine = [_head_lanes(qp.shape, w, 1) for w in heads]
        qz = [jnp.where(mine[w], qp, jnp.zeros_like(qp)) for w in heads]
        qTs = [qz[w].T for w in heads]
        doz = [jnp.where(mine[w], dof, 0.0).astype(BF16) for w in heads]
        doT = [doz[w].T for w in heads]
        ltot = [lt_ref[w] for w in heads]

        def blk(j, carry, masked):
            off = pl.multiple_of(j * bk, bk)
            kb = k_ref[pl.ds(off, bk), :]
            vb = v_ref[pl.ds(off, bk), :]
            kTb = kb.T
            logs = [_sb_logs(_dot(kb, qTs[w])) for w in heads]
            ls, lk = [t[0] for t in logs], [t[1] for t in logs]
            if masked:
                valid = _valid_t(i, j, bq, bk, True)
                lk = [jnp.where(valid, t, 0.0) for t in lk]
            pin = [_key_cumsum(lk[w], upto, False, carry[w][1] - ltot[w]) for w in heads]
            wgt = [jnp.exp(ls[w] - pin[w][0]) for w in heads]
            if masked:
                wgt = [jnp.where(valid, t, 0.0) for t in wgt]
            g = [_dot(vb, doT[w]) * wgt[w] for w in heads]
            cin = [_key_cumsum(g[w], before, False, carry[w][2]) for w in heads]
            sig = [jnp.exp(t) for t in ls]
            dz = [g[w] * (1.0 - sig[w]) - cin[w][0] * sig[w] for w in heads]
            if masked:
                dz = [jnp.where(valid, t, 0.0) for t in dz]
            dzb = [t.astype(BF16) for t in dz]
            dvs = [_dot(wgt[w].astype(BF16), doz[w]) for w in heads]
            dks = [_dot(dzb[w], qz[w]) for w in heads]
            dv_acc[pl.ds(off, bk), :] += dvs[0] + dvs[1]
            dk_acc[pl.ds(off, bk), :] += dks[0] + dks[1]
            return tuple((carry[w][0] + _dot(kTb, dzb[w]), carry[w][1] + pin[w][1], carry[w][2] + cin[w][1])
                         for w in heads)

        carry = tuple((jnp.zeros((LANE, bq), F32), jnp.zeros((1, bq), F32), jnp.zeros((1, bq), F32)) for _ in heads)
        start = first_ref[pl.program_id(0) // per_group, i]
        carry = lax.fori_loop(start, i, lambda j, cr: blk(j, cr, False), carry)
        carry = blk(i, carry, True)
        dq_ref[...] = (_merge_pair([carry[w][0] for w in heads]) * scale).astype(BF16)

        @pl.when(i == nq - 1)
        def _():
            dk_ref[...] = dk_acc[...].astype(BF16)
            dv_ref[...] = dv_acc[...].astype(BF16)

    qblk = pl.BlockSpec((bq, LANE), lambda p, i: (i, p))
    kres = pl.BlockSpec((S, LANE), lambda p, i: (0, p))
    stat = pl.BlockSpec((PAIR, 1, bq), lambda p, i: (p, 0, i))
    return pl.pallas_call(
        body, name=name, grid=(P, nq),
        in_specs=[pl.BlockSpec(memory_space=pltpu.SMEM), qblk, kres, kres, stat, qblk],
        out_specs=[qblk, kres, kres],
        out_shape=[jax.ShapeDtypeStruct((S, C), BF16)] * 3,
        scratch_shapes=[pltpu.VMEM((S, LANE), F32), pltpu.VMEM((S, LANE), F32)],
        compiler_params=_cp(("parallel", "arbitrary"), ATTN_VMEM_LIMIT),
    )(first, q2, k2, v2, lt, do2)


def _bias_cols(f_cum):
    H, Sk = f_cum.shape
    terms = jnp.stack(_split3(f_cum), axis=-1)
    packed = terms.reshape(H // PAIR, PAIR, Sk, 3).transpose(2, 0, 1, 3).reshape(Sk, H // PAIR, PAIR * 3)
    return jnp.pad(packed, ((0, 0), (0, 0), (0, LANE - PAIR * 3))).reshape(Sk, -1)


def _make_packed_softmax(name, scale, causal, use_f):
    assert _pow2(scale)

    def run_fwd(q16, k16, v16, f_cum):
        q16 = q16 * scale
        bias = _bias_cols(f_cum) if use_f else None
        r = (f_cum * LOG2E)[:, None, :] if use_f else None
        o, lse = _smp_fwd(q16, k16, v16, bias, r, causal, name + "_fwd")
        return o, (q16, k16, v16, o, lse, bias, r)

    def run_bwd(saved, do):
        q16, k16, v16, o, lse, bias, r = saved
        outs = _smp_bwd(q16, k16, v16, o, lse, do, bias, r, scale, causal, name + "_bwd")
        if use_f:
            return outs[0], outs[1], outs[2], outs[3][:, 0, :] - outs[4][:, 0, :]
        return tuple(outs)

    if use_f:
        @jax.custom_vjp
        def attn(q, k, v, f_cum):
            return run_fwd(q, k, v, f_cum)[0]

        attn.defvjp(run_fwd, run_bwd)
    else:
        @jax.custom_vjp
        def attn(q, k, v):
            return run_fwd(q, k, v, None)[0]

        attn.defvjp(lambda q, k, v: run_fwd(q, k, v, None), run_bwd)
    return attn


def _make_packed_sb(name, scale):
    assert _pow2(scale)

    def run_fwd(q16, k16, v16):
        q16 = q16 * scale
        o, lt, first = _sbp_fwd(q16, k16, v16, name + "_fwd")
        bq, _ = _attn_blocks(q16.shape[0], q16.shape[0])
        return o, (q16, k16, v16, lt, first[:, 0, ::bq])

    def run_bwd(saved, do):
        q16, k16, v16, lt, first = saved
        return tuple(_sbp_bwd(q16, k16, v16, lt, first, do, scale, name + "_bwd"))

    @jax.custom_vjp
    def attn(q, k, v):
        return run_fwd(q, k, v)[0]

    attn.defvjp(run_fwd, run_bwd)
    return attn


def _round_bf16(x):
    return lax.reduce_precision(x, exponent_bits=8, mantissa_bits=7)


def _split3(x):
    hi = _round_bf16(x)
    mid = _round_bf16(x - hi)
    lo = _round_bf16(x - hi - mid)
    return hi.astype(BF16), mid.astype(BF16), lo.astype(BF16)


def _pow2(x):
    m, _ = math.frexp(x)
    return m == 0.5


def _pad_last(x, n):
    return jnp.pad(x, [(0, 0)] * (x.ndim - 1) + [(0, n - x.shape[-1])])


def _layouts(q, k, scale):
    qh = _pad_last(jnp.transpose(q * scale if _pow2(scale) else q, (1, 0, 2)).astype(BF16), LANE)
    kh = _pad_last(jnp.transpose(k, (1, 0, 2)).astype(BF16), LANE)
    return qh, jnp.transpose(qh, (0, 2, 1)), kh, jnp.transpose(kh, (0, 2, 1))


def _make_softmax_attn(name, scale, causal, d):
    pre = _pow2(scale)
    cmul = LOG2E if pre else scale * LOG2E
    gscale = 1.0 if pre else scale

    def run_fwd(q, k, v):
        qn, qT, kn, kT = _layouts(q, k, scale)
        vn = jnp.transpose(v, (1, 0, 2)).astype(BF16)
        oT, lse = _sm_fwd_t(qT, kn, jnp.transpose(vn, (0, 2, 1)), cmul, causal, name + "_fwd")
        return jnp.transpose(oT, (2, 0, 1)), (qn, qT, kn, kT, vn, oT, lse)

    def run_bwd(saved, dout):
        qn, qT, kn, kT, vn, oT, lse = saved
        doT = jnp.transpose(dout, (1, 2, 0))
        do = jnp.transpose(dout, (1, 0, 2)).astype(BF16)
        dqT, dk, dv = _sm_bwd_t(qT, qn, kn, kT, vn, oT, lse, doT, do, cmul, gscale, causal, name + "_bwd")
        dq = jnp.transpose(dqT[:, :d, :], (2, 0, 1))
        if pre:
            dq = dq * scale
        return dq, jnp.transpose(dk[:, :, :d], (1, 0, 2)), jnp.transpose(dv, (1, 0, 2))

    @jax.custom_vjp
    def attn(q, k, v):
        return run_fwd(q, k, v)[0]

    attn.defvjp(run_fwd, run_bwd)
    return attn


def _rope(x, positions):
    half = x.shape[-1] // 2
    inv_freq = ROPE_THETA ** (-jnp.arange(half, dtype=F32) / half)
    ang = positions.astype(F32)[:, None] * inv_freq[None, :]
    ang = ang.reshape((ang.shape[0],) + (1,) * (x.ndim - 2) + (half,))
    cos, sin = jnp.cos(ang), jnp.sin(ang)
    x1, x2 = x[..., :half], x[..., half:]
    return jnp.concatenate([x1 * cos - x2 * sin, x1 * sin + x2 * cos], axis=-1)


def _permute_w_in(w):
    parts = [w[:, _ORIG_OFF[idx]:_ORIG_OFF[idx] + SPLIT_SIZES[idx]] for _, idx in _PERM]
    pad = jnp.zeros((w.shape[0], PROJ_COLS - IN_COLS), w.dtype)
    return jnp.concatenate(parts + [pad], axis=1)


_BF16_PIECES = ("fq", "fk", "fv", "sq", "sk", "sv", "mq")


def _make_ln_proj(name, has_res):
    def split(proj32, proj16):
        out, off = [], 0
        for n, idx in _PERM:
            src = proj16 if n in _BF16_PIECES else proj32
            out.append(src[:, off:off + SPLIT_SIZES[idx]])
            off += SPLIT_SIZES[idx]
        return tuple(out)

    def run_fwd(x, res, g, b, w):
        h, h16, hT16 = _ln_fwd_call(x, res, g, b, name + "_ln_fwd", also16=True)
        w16 = w.astype(BF16)
        proj32, proj16 = _matmul(h16, w16, "nn", name + "_fwd", also16=True)
        return (h, split(proj32, proj16)), (x, res, g, hT16, w16)

    def run_bwd(saved, cts):
        x, res, g, hT16, w16 = saved
        dh, dpieces = cts
        pad = jnp.zeros((x.shape[0], PROJ_COLS - IN_COLS), BF16)
        dy16 = jnp.concatenate([c.astype(BF16) for c in dpieces] + [pad], axis=1)
        da = _matmul(dy16, w16, "nt", name + "_dx")
        dw = _matmul(hT16, dy16, "nn", name + "_dw")
        outs = _ln_bwd_call(dh, x, res, g, name + "_ln_bwd", dy2=da)
        if has_res:
            dx, dr, dg, db = outs
            return dx, dr, dg.reshape(-1), db.reshape(-1), dw
        dx, dg, db = outs
        return dx, dg.reshape(-1), db.reshape(-1), dw

    if has_res:
        @jax.custom_vjp
        def op(x, res, g, b, w):
            return run_fwd(x, res, g, b, w)[0]

        op.defvjp(run_fwd, run_bwd)
    else:
        @jax.custom_vjp
        def op(x, g, b, w):
            return run_fwd(x, None, g, b, w)[0]

        op.defvjp(lambda x, g, b, w: run_fwd(x, None, g, b, w), run_bwd)

    def call(*args):
        h, pieces = op(*args)
        return h, {n: part for (n, _), part in zip(_PERM, pieces)}

    return call


def _trunk_loss(wts, x2d, mem2d, target2d):
    s = x2d.shape[0]
    positions = jnp.arange(s)
    head_scale = HEAD_DIM ** -0.5
    mla_scale = (MLA_NOPE + MLA_ROPE) ** -0.5

    mem_n = _make_ln("ln_mem", False)(mem2d, wts["mem_ln_g"], wts["mem_ln_b"])
    h, y = None, x2d
    for l in range(DEPTH):
        tag = f"l{l}_"
        w_p = _permute_w_in(wts["w_in"][l])
        if l == 0:
            h, p = _make_ln_proj(tag + "proj", False)(y, wts["ln_in_g"], wts["ln_in_b"], w_p)
        else:
            h, p = _make_ln_proj(tag + "proj", True)(y, h, wts["ln_g"][l - 1], wts["ln_b"][l - 1], w_p)

        log_f = jax.nn.log_sigmoid(p["f_logit"] + wts["b_forget"][l])
        f_cum = jnp.cumsum(log_f, axis=0).T
        out_fox = _make_packed_softmax(tag + "fox", head_scale, True, True)(p["fq"], p["fk"], p["fv"], f_cum)

        out_sb = _make_packed_sb(tag + "sb", head_scale)(p["sq"], p["sk"], p["sv"])

        cqn = _make_rms(tag + "rms_q")(p["c_q"], wts["mla_q_norm_g"][l])
        q_mla = _make_mm(tag + "q_up")(cqn, wts["w_mla_q_up"][l]).reshape(s, N_HEADS, MLA_NOPE + MLA_ROPE)
        ckvn = _make_rms(tag + "rms_kv")(p["c_kv"], wts["mla_kv_norm_g"][l])
        kv_mla = _make_mm(tag + "kv_up")(ckvn, wts["w_mla_kv_up"][l]).reshape(s, N_HEADS, MLA_NOPE + MLA_V)
        q_full = jnp.concatenate([q_mla[..., :MLA_NOPE], _rope(q_mla[..., MLA_NOPE:], positions)], axis=-1)
        k_rope = jnp.broadcast_to(_rope(p["k_rot"], positions)[:, None, :], (s, N_HEADS, MLA_ROPE))
        k_full = jnp.concatenate([kv_mla[..., :MLA_NOPE], k_rope], axis=-1)
        out_mla = _make_softmax_attn(tag + "mla", mla_scale, True, MLA_NOPE + MLA_ROPE)(
            q_full, k_full, kv_mla[..., MLA_NOPE:]).reshape(s, GROUP_W)

        mkv = _make_mm(tag + "mem_kv")(mem_n, wts["w_mem_kv"][l])
        out_mem = _make_packed_softmax(tag + "mem", head_scale, False, False)(
            p["mq"], mkv[:, :GROUP_W].astype(BF16), mkv[:, GROUP_W:].astype(BF16))

        mixed = jnp.concatenate([out_fox, out_sb, out_mla, out_mem], axis=-1)
        y = _make_gate_out(tag + "out")(mixed, p["gate"], wts["w_out"][l])

    h = _make_ln(f"l{DEPTH - 1}_ln", True)(y, h, wts["ln_g"][DEPTH - 1], wts["ln_b"][DEPTH - 1])
    return _loss_op(h, target2d)


def _mesh_pos():
    x, y, c = (lax.axis_index(a) for a in MESH_AXES)
    return x, y, c, 4 * x + 2 * y + c


def _peer(x, y, c, mask):
    return (x ^ ((mask >> 2) & 1), y ^ ((mask >> 1) & 1), c ^ (mask & 1))


_ANY = pl.BlockSpec(memory_space=pl.ANY)


def _all_gather(row_shards, stack_shards):
    n_row, n_all = len(row_shards), len(row_shards) + len(stack_shards)
    shards = list(row_shards) + list(stack_shards)

    def body(*refs):
        ins, outs = refs[:n_all], refs[n_all:2 * n_all]
        send_sems, recv_sems, local_sems = refs[2 * n_all:]
        x, y, c, me = _mesh_pos()

        def window(t, slot):
            if t < n_row:
                rows = shards[t].shape[1]
                return outs[t].at[:, pl.ds(slot * rows, rows), :]
            return outs[t].at[slot]

        local = [pltpu.make_async_copy(ins[t], window(t, me), local_sems.at[t]) for t in range(n_all)]
        for cp in local:
            cp.start()
        sends = []
        for mask in range(1, N_DEV):
            for t in range(n_all):
                cp = pltpu.make_async_remote_copy(
                    src_ref=ins[t], dst_ref=window(t, me), send_sem=send_sems.at[t, mask - 1],
                    recv_sem=recv_sems.at[t, mask - 1], device_id=_peer(x, y, c, mask),
                    device_id_type=pl.DeviceIdType.MESH)
                cp.start()
                sends.append(cp)
        for mask in range(1, N_DEV):
            for t in range(n_all):
                pltpu.make_async_remote_copy(
                    src_ref=ins[t], dst_ref=window(t, me ^ mask), send_sem=send_sems.at[t, mask - 1],
                    recv_sem=recv_sems.at[t, mask - 1], device_id=_peer(x, y, c, mask),
                    device_id_type=pl.DeviceIdType.MESH).wait_recv()
        for cp in sends:
            cp.wait_send()
        for cp in local:
            cp.wait()

    out_shape = [jax.ShapeDtypeStruct((a.shape[0], N_DEV * a.shape[1], a.shape[2]), a.dtype) for a in row_shards]
    out_shape += [jax.ShapeDtypeStruct((N_DEV,) + a.shape, a.dtype) for a in stack_shards]
    return pl.pallas_call(
        body, name="all_gather_weights", in_specs=[_ANY] * n_all, out_specs=[_ANY] * n_all, out_shape=out_shape,
        scratch_shapes=[pltpu.SemaphoreType.DMA((n_all, N_DEV - 1)), pltpu.SemaphoreType.DMA((n_all, N_DEV - 1)),
                        pltpu.SemaphoreType.DMA((n_all,))],
    )(*shards)


def _reduce_scatter(row_full, stack_full, bcast):
    n_row, n_stack = len(row_full), len(stack_full)
    n_all = n_row + n_stack + len(bcast)
    fulls = list(row_full) + list(stack_full) + list(bcast)

    def body(*refs):
        ins, outs = refs[:n_all], refs[n_all:2 * n_all]
        send_sems, recv_sems, local_sems = refs[2 * n_all:]
        x, y, c, me = _mesh_pos()

        def part(t, slot):
            if t < n_row:
                rows = fulls[t].shape[1] // N_DEV
                return ins[t].at[:, pl.ds(slot * rows, rows), :]
            if t < n_row + n_stack:
                return ins[t].at[slot]
            return ins[t]

        local = [pltpu.make_async_copy(part(t, me), outs[t].at[me], local_sems.at[t]) for t in range(n_all)]
        for cp in local:
            cp.start()
        sends = []
        for mask in range(1, N_DEV):
            for t in range(n_all):
                cp = pltpu.make_async_remote_copy(
                    src_ref=part(t, me ^ mask), dst_ref=outs[t].at[me], send_sem=send_sems.at[t, mask - 1],
                    recv_sem=recv_sems.at[t, mask - 1], device_id=_peer(x, y, c, mask),
                    device_id_type=pl.DeviceIdType.MESH)
                cp.start()
                sends.append(cp)
        for mask in range(1, N_DEV):
            for t in range(n_all):
                pltpu.make_async_remote_copy(
                    src_ref=part(t, me), dst_ref=outs[t].at[me ^ mask], send_sem=send_sems.at[t, mask - 1],
                    recv_sem=recv_sems.at[t, mask - 1], device_id=_peer(x, y, c, mask),
                    device_id_type=pl.DeviceIdType.MESH).wait_recv()
        for cp in sends:
            cp.wait_send()
        for cp in local:
            cp.wait()

    out_shape = [jax.ShapeDtypeStruct((N_DEV, a.shape[0], a.shape[1] // N_DEV, a.shape[2]), a.dtype) for a in row_full]
    out_shape += [jax.ShapeDtypeStruct(a.shape, a.dtype) for a in stack_full]
    out_shape += [jax.ShapeDtypeStruct((N_DEV,) + a.shape, a.dtype) for a in bcast]
    return pl.pallas_call(
        body, name="reduce_scatter_grads", in_specs=[_ANY] * n_all, out_specs=[_ANY] * n_all, out_shape=out_shape,
        scratch_shapes=[pltpu.SemaphoreType.DMA((n_all, N_DEV - 1)), pltpu.SemaphoreType.DMA((n_all, N_DEV - 1)),
                        pltpu.SemaphoreType.DMA((n_all,))],
    )(*fulls)


def _adamw(slots, w, m, v, name):
    shape = w.shape
    cols = shape[-1]
    rows = math.prod(shape[:-1])
    tr = _pick(rows, (64, 32, 16, 8))
    c1 = 1.0 - ADAM_B1 ** ADAM_STEP
    c2 = 1.0 - ADAM_B2 ** ADAM_STEP

    def body(s_ref, w_ref, m_ref, v_ref, g_ref, d_ref, nm_ref, nv_ref):
        g = s_ref[0].astype(F32)
        for k in range(1, N_DEV):
            g = g + s_ref[k].astype(F32)
        nm = ADAM_B1 * m_ref[...] + (1.0 - ADAM_B1) * g
        nv = ADAM_B2 * v_ref[...] + (1.0 - ADAM_B2) * (g * g)
        g_ref[...] = g
        nm_ref[...] = nm
        nv_ref[...] = nv
        d_ref[...] = -ADAM_LR * ((nm / c1) / (jnp.sqrt(nv / c2) + ADAM_EPS) + ADAM_WD * w_ref[...])

    row = pl.BlockSpec((tr, cols), lambda i: (i, 0))
    out = jax.ShapeDtypeStruct((rows, cols), F32)
    outs = pl.pallas_call(
        body, name=name, grid=(rows // tr,),
        in_specs=[pl.BlockSpec((N_DEV, tr, cols), lambda i: (0, i, 0)), row, row, row],
        out_specs=[row] * 4, out_shape=[out] * 4, compiler_params=_cp(("parallel",)),
    )(slots.reshape(N_DEV, rows, cols), w.reshape(rows, cols), m.reshape(rows, cols), v.reshape(rows, cols))
    return [o.reshape(shape) for o in outs]


_SMALL = ("ln_in_g", "ln_in_b", "mem_ln_g", "mem_ln_b", "b_forget", "mla_q_norm_g", "mla_kv_norm_g", "ln_g", "ln_b")
_ORDER = ("ln_in_g", "ln_in_b", "mem_ln_g", "mem_ln_b", "w_in", "b_forget", "mla_q_norm_g", "w_mla_q_up",
          "mla_kv_norm_g", "w_mla_kv_up", "w_mem_kv", "w_out", "ln_g", "ln_b")


def _pack_small(d):
    flat = jnp.concatenate([d[n].reshape(-1) for n in _SMALL])
    n = flat.shape[0]
    padded = ((n + 8 * LANE - 1) // (8 * LANE)) * (8 * LANE)
    return jnp.pad(flat, (0, padded - n)).reshape(-1, LANE)


def _unpack_small(packed, like):
    flat, out, off = packed.reshape(-1), {}, 0
    for n in _SMALL:
        size = math.prod(like[n].shape)
        out[n] = flat[off:off + size].reshape(like[n].shape)
        off += size
    return out


def _unstack_cols(g):
    n, l, r, c = g.shape
    return g.transpose(1, 2, 0, 3).reshape(l, r, n * c)


def _stack_cols(g):
    l, r, nc = g.shape
    return g.reshape(l, r, N_DEV, nc // N_DEV).transpose(2, 0, 1, 3)


def kernel(x, mem, ln_in_g, ln_in_b, mem_ln_g, mem_ln_b, w_in, b_forget, mla_q_norm_g, w_mla_q_up, mla_kv_norm_g, w_mla_kv_up, w_mem_kv, w_out, ln_g, ln_b, loss_target, m_ln_in_g, m_ln_in_b, m_mem_ln_g, m_mem_ln_b, m_w_in, m_b_forget, m_mla_q_norm_g, m_w_mla_q_up, m_mla_kv_norm_g, m_w_mla_kv_up, m_w_mem_kv, m_w_out, m_ln_g, m_ln_b, v_ln_in_g, v_ln_in_b, v_mem_ln_g, v_mem_ln_b, v_w_in, v_b_forget, v_mla_q_norm_g, v_w_mla_q_up, v_mla_kv_norm_g, v_w_mla_kv_up, v_w_mem_kv, v_w_out, v_ln_g, v_ln_b):
    w_shard = dict(ln_in_g=ln_in_g, ln_in_b=ln_in_b, mem_ln_g=mem_ln_g, mem_ln_b=mem_ln_b, w_in=w_in,
                   b_forget=b_forget, mla_q_norm_g=mla_q_norm_g, w_mla_q_up=w_mla_q_up,
                   mla_kv_norm_g=mla_kv_norm_g, w_mla_kv_up=w_mla_kv_up, w_mem_kv=w_mem_kv, w_out=w_out,
                   ln_g=ln_g, ln_b=ln_b)
    m_shard = dict(ln_in_g=m_ln_in_g, ln_in_b=m_ln_in_b, mem_ln_g=m_mem_ln_g, mem_ln_b=m_mem_ln_b, w_in=m_w_in,
                   b_forget=m_b_forget, mla_q_norm_g=m_mla_q_norm_g, w_mla_q_up=m_w_mla_q_up,
                   mla_kv_norm_g=m_mla_kv_norm_g, w_mla_kv_up=m_w_mla_kv_up, w_mem_kv=m_w_mem_kv, w_out=m_w_out,
                   ln_g=m_ln_g, ln_b=m_ln_b)
    v_shard = dict(ln_in_g=v_ln_in_g, ln_in_b=v_ln_in_b, mem_ln_g=v_mem_ln_g, mem_ln_b=v_mem_ln_b, w_in=v_w_in,
                   b_forget=v_b_forget, mla_q_norm_g=v_mla_q_norm_g, w_mla_q_up=v_w_mla_q_up,
                   mla_kv_norm_g=v_mla_kv_norm_g, w_mla_kv_up=v_w_mla_kv_up, w_mem_kv=v_w_mem_kv, w_out=v_w_out,
                   ln_g=v_ln_g, ln_b=v_ln_b)

    to16 = lambda ws: [a.astype(BF16) for a in ws]
    gathered = _all_gather(to16([w_in, w_mem_kv, w_out]), to16([w_mla_q_up, w_mla_kv_up]))
    g_in, g_mem, g_out, g_qup, g_kvup = [a.astype(F32) for a in gathered]
    full = dict(w_shard)
    full.update(w_in=g_in, w_mem_kv=g_mem, w_out=g_out, w_mla_q_up=_unstack_cols(g_qup),
                w_mla_kv_up=_unstack_cols(g_kvup))

    loss_local, (grad_w, grad_x) = jax.value_and_grad(_trunk_loss, argnums=(0, 1))(
        full, x[0], mem[0], loss_target[0])

    s_in, s_mem, s_out, s_qup, s_kvup, s_small = _reduce_scatter(
        to16([grad_w["w_in"], grad_w["w_mem_kv"], grad_w["w_out"]]),
        to16([_stack_cols(grad_w["w_mla_q_up"]), _stack_cols(grad_w["w_mla_kv_up"])]),
        [_pack_small(grad_w)])

    res = {}
    for name, slots in (("w_in", s_in), ("w_mem_kv", s_mem), ("w_out", s_out), ("w_mla_q_up", s_qup),
                        ("w_mla_kv_up", s_kvup)):
        res[name] = _adamw(slots, w_shard[name], m_shard[name], v_shard[name], "adamw_" + name)
    small = _adamw(s_small, _pack_small(w_shard), _pack_small(m_shard), _pack_small(v_shard), "adamw_small")
    small = [_unpack_small(a, w_shard) for a in small]
    for name in _SMALL:
        res[name] = [a[name] for a in small]

    loss = lax.psum(loss_local, MESH_AXES)
    outs = [loss, grad_x[None]]
    for k in range(4):
        outs += [res[name][k] for name in _ORDER]
    return tuple(outs)
```

```python
import functools
import math

import jax
import jax.numpy as jnp
from jax import lax
from jax.experimental import pallas as pl
from jax.experimental.pallas import tpu as pltpu

F32 = jnp.float32
BF16 = jnp.bfloat16

D_MODEL = 1024
DEPTH = 2
GROUP_W = 256
N_HEADS = 4
HEAD_DIM = 64
MLA_Q_RANK = 256
MLA_KV_RANK = 128
MLA_NOPE = 64
MLA_ROPE = 32
MLA_V = 64
ROPE_THETA = 10000.0
LN_EPS = 1e-5
RMS_EPS = 1e-6
DEEPNORM_ALPHA = (2 * DEPTH) ** 0.25
SPLIT_SIZES = (256, 256, 256, 4, 256, 256, 256, 256, 128, 32, 256, 1024)
IN_COLS = sum(SPLIT_SIZES)
_ORIG_OFF = [sum(SPLIT_SIZES[:i]) for i in range(len(SPLIT_SIZES))]
_PERM = (("fq", 0), ("fk", 1), ("fv", 2), ("sq", 4), ("sk", 5), ("sv", 6), ("c_q", 7), ("c_kv", 8),
         ("mq", 10), ("gate", 11), ("k_rot", 9), ("f_logit", 3))
LANE = 128
PROJ_COLS = ((IN_COLS + LANE - 1) // LANE) * LANE

ADAM_LR = 0.001
ADAM_B1 = 0.9
ADAM_B2 = 0.999
ADAM_EPS = 1e-08
ADAM_WD = 0.01
ADAM_STEP = 10

N_DEV = 8
MESH_AXES = ("x", "y", "c")
VMEM_LIMIT = 48 * 1024 * 1024
ATTN_VMEM_LIMIT = 56 * 1024 * 1024
ATTN_BQ = 512
ATTN_BK = 512
CUMSUM_CHUNK = 256
NEG_BIG = -1e30
LOG2E = math.log2(math.e)
MM_TM, MM_TN, MM_TK, MM_TK_NT = 1024, 1664, 1024, 3328

_NT = (((1,), (1,)), ((), ()))
_NN = (((1,), (0,)), ((), ()))


def _cp(sem, vmem=VMEM_LIMIT):
    return pltpu.CompilerParams(dimension_semantics=sem, vmem_limit_bytes=vmem)


def _dot(a, b, dims=_NN):
    return lax.dot_general(a, b, dims, preferred_element_type=F32)


def _pick(n, cands):
    for c in cands:
        if c <= n and n % c == 0:
            return c
    return n


def _tile(n, cap):
    if n <= cap:
        return n
    best = None
    for d in range(LANE, cap + 1, LANE):
        if n % d == 0:
            best = d
    assert best is not None, (n, cap)
    return best


def _matmul(a, b, mode, name, also16=False):
    if mode == "nn":
        (M, K), (K2, N) = a.shape, b.shape
    else:
        (M, K), (N, K2) = a.shape, b.shape
    assert K == K2 and a.dtype == BF16 and b.dtype == BF16, (a.shape, b.shape, mode)
    tm, tn = _tile(M, MM_TM), _tile(N, MM_TN)
    tk = _tile(K, MM_TK if mode == "nn" else MM_TK_NT)
    nk = K // tk
    dims = _NN if mode == "nn" else _NT

    def body(a_ref, b_ref, *rest):
        o_ref, acc_ref = rest[0], rest[-1]
        part = _dot(a_ref[...], b_ref[...], dims)
        if nk == 1:
            o_ref[...] = part
            if also16:
                rest[1][...] = part.astype(BF16)
        else:
            assert not also16
            k = pl.program_id(2)

            @pl.when(k == 0)
            def _():
                acc_ref[...] = part

            @pl.when(k > 0)
            def _():
                acc_ref[...] += part

            @pl.when(k == nk - 1)
            def _():
                o_ref[...] = acc_ref[...]

    a_spec = pl.BlockSpec((tm, tk), lambda j, i, k: (i, k))
    if mode == "nn":
        b_spec = pl.BlockSpec((tk, tn), lambda j, i, k: (k, j))
    else:
        b_spec = pl.BlockSpec((tn, tk), lambda j, i, k: (j, k))
    acc_shape = (tm, tn) if nk > 1 else (8, LANE)
    o_spec = pl.BlockSpec((tm, tn), lambda j, i, k: (i, j))
    outs = pl.pallas_call(
        body, name=name, grid=(N // tn, M // tm, nk),
        in_specs=[a_spec, b_spec],
        out_specs=[o_spec, o_spec] if also16 else o_spec,
        out_shape=[jax.ShapeDtypeStruct((M, N), F32), jax.ShapeDtypeStruct((M, N), BF16)] if also16
        else jax.ShapeDtypeStruct((M, N), F32),
        scratch_shapes=[pltpu.VMEM(acc_shape, F32)],
        compiler_params=_cp(("parallel", "parallel", "arbitrary")),
    )(a, b)
    return outs


def _make_mm(name):
    @jax.custom_vjp
    def mm(a, w):
        return _matmul(a.astype(BF16), w.astype(BF16), "nn", name + "_fwd")

    def fwd(a, w):
        a16, w16 = a.astype(BF16), w.astype(BF16)
        return _matmul(a16, w16, "nn", name + "_fwd"), (a16, w16)

    def bwd(res, dy):
        a16, w16 = res
        dy16 = dy.astype(BF16)
        da = _matmul(dy16, w16, "nt", name + "_dx")
        dw = _matmul(a16.T, dy16, "nn", name + "_dw")
        return da, dw

    mm.defvjp(fwd, bwd)
    return mm


def _row_tile(rows):
    return _pick(rows, (512, 256, 128, 64, 32, 16, 8))


def _ln_stats(u):
    mu = jnp.mean(u, axis=-1, keepdims=True)
    d = u - mu
    var = jnp.mean(d * d, axis=-1, keepdims=True)
    return d, lax.rsqrt(var + LN_EPS)


def _ln_fwd_call(x, res, g, b, name, also16=False):
    rows, dm = x.shape
    tr = _row_tile(rows)
    has_res = res is not None
    n_in = 2 if has_res else 1

    def body(*refs):
        if has_res:
            u = DEEPNORM_ALPHA * refs[1][...] + refs[0][...]
        else:
            u = refs[0][...]
        g_ref, b_ref = refs[n_in], refs[n_in + 1]
        d, rstd = _ln_stats(u)
        y = d * rstd * g_ref[...] + b_ref[...]
        refs[n_in + 2][...] = y
        if also16:
            y16 = y.astype(BF16)
            refs[n_in + 3][...] = y16
            refs[n_in + 4][...] = y16.T

    row = pl.BlockSpec((tr, dm), lambda i: (i, 0))
    vec = pl.BlockSpec((1, dm), lambda i: (0, 0))
    args = (x, res) if has_res else (x,)
    out_specs, out_shape = [row], [jax.ShapeDtypeStruct((rows, dm), F32)]
    if also16:
        out_specs += [row, pl.BlockSpec((dm, tr), lambda i: (0, i))]
        out_shape += [jax.ShapeDtypeStruct((rows, dm), BF16), jax.ShapeDtypeStruct((dm, rows), BF16)]
    outs = pl.pallas_call(
        body, name=name, grid=(rows // tr,),
        in_specs=[row] * n_in + [vec, vec], out_specs=out_specs, out_shape=out_shape,
        compiler_params=_cp(("parallel",)),
    )(*args, g.reshape(1, dm), b.reshape(1, dm))
    return outs if also16 else outs[0]


def _ln_bwd_call(dy, x, res, g, name, dy2=None):
    rows, dm = x.shape
    tr = _row_tile(rows)
    has_res = res is not None
    two = dy2 is not None

    def body(*refs):
        dy_ref, refs = refs[0], refs[1:]
        if two:
            dy2_ref, refs = refs[0], refs[1:]
        if has_res:
            x_ref, r_ref, g_ref, dx_ref, dr_ref, dg_ref, db_ref = refs
            u = DEEPNORM_ALPHA * r_ref[...] + x_ref[...]
        else:
            x_ref, g_ref, dx_ref, dg_ref, db_ref = refs
            u = x_ref[...]
        i = pl.program_id(0)
        d, rstd = _ln_stats(u)
        xhat = d * rstd
        dyv = dy_ref[...] + dy2_ref[...] if two else dy_ref[...]
        dxh = dyv * g_ref[...]
        m1 = jnp.mean(dxh, axis=-1, keepdims=True)
        m2 = jnp.mean(dxh * xhat, axis=-1, keepdims=True)
        du = rstd * (dxh - m1 - xhat * m2)
        dx_ref[...] = du
        if has_res:
            dr_ref[...] = DEEPNORM_ALPHA * du
        pg = jnp.sum(dyv * xhat, axis=0, keepdims=True)
        pb = jnp.sum(dyv, axis=0, keepdims=True)

        @pl.when(i == 0)
        def _():
            dg_ref[...] = pg
            db_ref[...] = pb

        @pl.when(i > 0)
        def _():
            dg_ref[...] += pg
            db_ref[...] += pb

    row = pl.BlockSpec((tr, dm), lambda i: (i, 0))
    vec = pl.BlockSpec((1, dm), lambda i: (0, 0))
    big = jax.ShapeDtypeStruct((rows, dm), F32)
    small = jax.ShapeDtypeStruct((1, dm), F32)
    args = ((dy, dy2) if two else (dy,)) + ((x, res) if has_res else (x,))
    n_big = 2 if has_res else 1
    outs = pl.pallas_call(
        body, name=name, grid=(rows // tr,),
        in_specs=[row] * len(args) + [vec],
        out_specs=[row] * n_big + [vec, vec],
        out_shape=[big] * n_big + [small, small],
        compiler_params=_cp(("arbitrary",)),
    )(*args, g.reshape(1, dm))
    return outs


def _make_ln(name, has_res):
    if has_res:
        @jax.custom_vjp
        def ln(x, res, g, b):
            return _ln_fwd_call(x, res, g, b, name + "_fwd")

        def fwd(x, res, g, b):
            return ln(x, res, g, b), (x, res, g)

        def bwd(saved, dy):
            x, res, g = saved
            dx, dr, dg, db = _ln_bwd_call(dy, x, res, g, name + "_bwd")
            return dx, dr, dg.reshape(-1), db.reshape(-1)
    else:
        @jax.custom_vjp
        def ln(x, g, b):
            return _ln_fwd_call(x, None, g, b, name + "_fwd")

        def fwd(x, g, b):
            return ln(x, g, b), (x, g)

        def bwd(saved, dy):
            x, g = saved
            dx, dg, db = _ln_bwd_call(dy, x, None, g, name + "_bwd")
            return dx, dg.reshape(-1), db.reshape(-1)

    ln.defvjp(fwd, bwd)
    return ln


def _rms_fwd_call(x, g, name):
    rows, dm = x.shape
    tr = _row_tile(rows)

    def body(x_ref, g_ref, o_ref):
        xv = x_ref[...]
        rstd = lax.rsqrt(jnp.mean(xv * xv, axis=-1, keepdims=True) + RMS_EPS)
        o_ref[...] = xv * rstd * g_ref[...]

    row = pl.BlockSpec((tr, dm), lambda i: (i, 0))
    vec = pl.BlockSpec((1, dm), lambda i: (0, 0))
    return pl.pallas_call(
        body, name=name, grid=(rows // tr,), in_specs=[row, vec], out_specs=row,
        out_shape=jax.ShapeDtypeStruct((rows, dm), F32), compiler_params=_cp(("parallel",)),
    )(x, g.reshape(1, dm))


def _rms_bwd_call(dy, x, g, name):
    rows, dm = x.shape
    tr = _row_tile(rows)

    def body(dy_ref, x_ref, g_ref, dx_ref, dg_ref):
        i = pl.program_id(0)
        xv = x_ref[...]
        dyv = dy_ref[...]
        rstd = lax.rsqrt(jnp.mean(xv * xv, axis=-1, keepdims=True) + RMS_EPS)
        xhat = xv * rstd
        dxh = dyv * g_ref[...]
        m2 = jnp.mean(dxh * xhat, axis=-1, keepdims=True)
        dx_ref[...] = rstd * (dxh - xhat * m2)
        pg = jnp.sum(dyv * xhat, axis=0, keepdims=True)

        @pl.when(i == 0)
        def _():
            dg_ref[...] = pg

        @pl.when(i > 0)
        def _():
            dg_ref[...] += pg

    row = pl.BlockSpec((tr, dm), lambda i: (i, 0))
    vec = pl.BlockSpec((1, dm), lambda i: (0, 0))
    return pl.pallas_call(
        body, name=name, grid=(rows // tr,), in_specs=[row, row, vec], out_specs=[row, vec],
        out_shape=[jax.ShapeDtypeStruct((rows, dm), F32), jax.ShapeDtypeStruct((1, dm), F32)],
        compiler_params=_cp(("arbitrary",)),
    )(dy, x, g.reshape(1, dm))


def _make_rms(name):
    @jax.custom_vjp
    def rms(x, g):
        return _rms_fwd_call(x, g, name + "_fwd")

    def fwd(x, g):
        return rms(x, g), (x, g)

    def bwd(saved, dy):
        x, g = saved
        dx, dg = _rms_bwd_call(dy, x, g, name + "_bwd")
        return dx, dg.reshape(-1)

    rms.defvjp(fwd, bwd)
    return rms


def _sigmoid(x):
    return 1.0 / (1.0 + jnp.exp(-x))


def _gate_fwd_call(mixed, gate, name):
    rows, dm = mixed.shape
    tr = _row_tile(rows)

    def body(m_ref, g_ref, o_ref, oT_ref):
        gv = g_ref[...]
        y16 = (m_ref[...] * (gv * _sigmoid(gv))).astype(BF16)
        o_ref[...] = y16
        oT_ref[...] = y16.T

    row = pl.BlockSpec((tr, dm), lambda i: (i, 0))
    return pl.pallas_call(
        body, name=name, grid=(rows // tr,), in_specs=[row, row],
        out_specs=[row, pl.BlockSpec((dm, tr), lambda i: (0, i))],
        out_shape=[jax.ShapeDtypeStruct((rows, dm), BF16), jax.ShapeDtypeStruct((dm, rows), BF16)],
        compiler_params=_cp(("parallel",)),
    )(mixed, gate)


def _gate_bwd_call(dy, mixed, gate, name):
    rows, dm = mixed.shape
    tr = _row_tile(rows)

    def body(dy_ref, m_ref, g_ref, dm_ref, dg_ref):
        gv = g_ref[...]
        dyv = dy_ref[...]
        sg = _sigmoid(gv)
        dm_ref[...] = dyv * (gv * sg)
        dg_ref[...] = dyv * m_ref[...] * (sg * (1.0 + gv * (1.0 - sg)))

    row = pl.BlockSpec((tr, dm), lambda i: (i, 0))
    out = jax.ShapeDtypeStruct((rows, dm), F32)
    return pl.pallas_call(
        body, name=name, grid=(rows // tr,), in_specs=[row, row, row], out_specs=[row, row],
        out_shape=[out, out], compiler_params=_cp(("parallel",)),
    )(dy, mixed, gate)


def _make_gate_out(name):
    def run_fwd(mixed, gate, w):
        g16, gT16 = _gate_fwd_call(mixed, gate, name + "_gate_fwd")
        w16 = w.astype(BF16)
        return _matmul(g16, w16, "nn", name + "_fwd"), (mixed, gate, gT16, w16)

    def run_bwd(saved, dy):
        mixed, gate, gT16, w16 = saved
        dy16 = dy.astype(BF16)
        dgated = _matmul(dy16, w16, "nt", name + "_dx")
        dmix, dgate = _gate_bwd_call(dgated, mixed, gate, name + "_gate_bwd")
        return dmix, dgate, _matmul(gT16, dy16, "nn", name + "_dw")

    @jax.custom_vjp
    def gate_out(mixed, gate, w):
        return run_fwd(mixed, gate, w)[0]

    gate_out.defvjp(run_fwd, run_bwd)
    return gate_out


def _loss_call(y, t, name):
    rows, dm = y.shape
    tr = _row_tile(rows)

    def body(y_ref, t_ref, l_ref, d_ref):
        i = pl.program_id(0)
        e = y_ref[...] - t_ref[...]
        d_ref[...] = e * (1.0 / dm)
        part = 0.5 * jnp.sum(jnp.mean(e * e, axis=-1, keepdims=True), axis=0, keepdims=True)

        @pl.when(i == 0)
        def _():
            l_ref[...] = part

        @pl.when(i > 0)
        def _():
            l_ref[...] += part

    row = pl.BlockSpec((tr, dm), lambda i: (i, 0))
    one = pl.BlockSpec((1, 1), lambda i: (0, 0))
    return pl.pallas_call(
        body, name=name, grid=(rows // tr,), in_specs=[row, row], out_specs=[one, row],
        out_shape=[jax.ShapeDtypeStruct((1, 1), F32), jax.ShapeDtypeStruct((rows, dm), F32)],
        compiler_params=_cp(("arbitrary",)),
    )(y, t)


@jax.custom_vjp
def _loss_op(y, t):
    return _loss_call(y, t, "loss_head")[0][0, 0]


def _loss_fwd(y, t):
    l, d = _loss_call(y, t, "loss_head")
    return l[0, 0], d


def _loss_bwd(d, ct):
    return ct * d, jnp.zeros_like(d)


_loss_op.defvjp(_loss_fwd, _loss_bwd)


def _attn_blocks(S, Sk):
    bq, bk = min(ATTN_BQ, S), min(ATTN_BK, Sk)
    assert S % bq == 0 and Sk % bk == 0
    return bq, bk


def _valid_t(i, j, bq, bk, strict):
    key = j * bk + lax.broadcasted_iota(jnp.int32, (bk, bq), 0)
    qry = i * bq + lax.broadcasted_iota(jnp.int32, (bk, bq), 1)
    return (key < qry) if strict else (key <= qry)


def _sm_fwd_t(qT, k, vT, cmul, causal, name):
    H, DK, S = qT.shape
    Sk, dv = k.shape[1], vT.shape[1]
    bq, bk = _attn_blocks(S, Sk)
    nq, nkb = S // bq, Sk // bk
    hb = PAIR * FWD_PAIRS if H % (PAIR * FWD_PAIRS) == 0 else 1
    heads = range(hb)
    if causal:
        assert S == Sk and bq == bk

    def body(qT_ref, k_ref, vT_ref, oT_ref, lse_ref):
        i = pl.program_id(1)
        qTs = [qT_ref[w] for w in heads]

        def blk(j, carry, masked):
            off = pl.multiple_of(j * bk, bk)
            sT = [_dot(k_ref[w, pl.ds(off, bk), :], qTs[w]) * cmul for w in heads]
            if masked:
                valid = _valid_t(i, j, bq, bk, False)
                sT = [jnp.where(valid, s, NEG_BIG) for s in sT]
            m_new = [jnp.maximum(carry[w][0], jnp.max(sT[w], axis=0, keepdims=True)) for w in heads]
            p = [jnp.exp2(sT[w] - m_new[w]) for w in heads]
            a = [jnp.exp2(carry[w][0] - m_new[w]) for w in heads]
            l = [a[w] * carry[w][1] + jnp.sum(p[w], axis=0, keepdims=True) for w in heads]
            acc = [a[w] * carry[w][2] + _dot(vT_ref[w, :, pl.ds(off, bk)], p[w].astype(BF16)) for w in heads]
            return tuple((m_new[w], l[w], acc[w]) for w in heads)

        carry = tuple((jnp.full((1, bq), NEG_BIG, F32), jnp.zeros((1, bq), F32), jnp.zeros((dv, bq), F32))
                      for _ in heads)
        if causal:
            carry = lax.fori_loop(0, i, lambda j, c: blk(j, c, False), carry)
            carry = blk(i, carry, True)
        else:
            carry = lax.fori_loop(0, nkb, lambda j, c: blk(j, c, False), carry)
        for w in heads:
            oT_ref[w] = carry[w][2] / carry[w][1]
            lse_ref[w] = carry[w][0] + jnp.log2(carry[w][1])

    qcol = lambda d: pl.BlockSpec((hb, d, bq), lambda h, i: (h, 0, i))
    return pl.pallas_call(
        body, name=name, grid=(H // hb, nq),
        in_specs=[qcol(DK), pl.BlockSpec((hb, Sk, DK), lambda h, i: (h, 0, 0)),
                  pl.BlockSpec((hb, dv, Sk), lambda h, i: (h, 0, 0))],
        out_specs=[qcol(dv), qcol(1)],
        out_shape=[jax.ShapeDtypeStruct((H, dv, S), F32), jax.ShapeDtypeStruct((H, 1, S), F32)],
        compiler_params=_cp(("parallel", "arbitrary"), ATTN_VMEM_LIMIT),
    )(qT, k, vT)


def _sm_bwd_t(qT, qn, k, kT, v, oT, lse, doT, do, cmul, gscale, causal, name):
    H, DK, S = qT.shape
    Sk, dv = k.shape[1], v.shape[2]
    bq, bk = _attn_blocks(S, Sk)
    nq, nkb = S // bq, Sk // bk

    def body(qT_ref, qn_ref, k_ref, kT_ref, v_ref, oT_ref, lse_ref, doT_ref, do_ref, dqT_ref, dk_ref, dv_ref):
        i = pl.program_id(1)

        @pl.when(i == 0)
        def _():
            dk_ref[...] = jnp.zeros_like(dk_ref)
            dv_ref[...] = jnp.zeros_like(dv_ref)

        qTb = qT_ref[...]
        qnb = qn_ref[...]
        dob = do_ref[...]
        doTf = doT_ref[...]
        doTb = doTf.astype(BF16)
        delta = jnp.sum(doTf * oT_ref[...], axis=0, keepdims=True)
        lse = lse_ref[...]

        def blk(j, dq, masked):
            off = pl.multiple_of(j * bk, bk)
            sT = _dot(k_ref[pl.ds(off, bk), :], qTb) * cmul
            if masked:
                sT = jnp.where(_valid_t(i, j, bq, bk, False), sT, NEG_BIG)
            p = jnp.exp2(sT - lse)
            dp = _dot(v_ref[pl.ds(off, bk), :], doTb)
            ds = p * (dp - delta)
            dsb = (ds * gscale).astype(BF16) if gscale != 1.0 else ds.astype(BF16)
            dv_ref[pl.ds(off, bk), :] += _dot(p.astype(BF16), dob)
            dk_ref[pl.ds(off, bk), :] += _dot(dsb, qnb)
            return dq + _dot(kT_ref[:, pl.ds(off, bk)], dsb)

        dq = jnp.zeros((DK, bq), F32)
        if causal:
            dq = lax.fori_loop(0, i, lambda j, c: blk(j, c, False), dq)
            dq = blk(i, dq, True)
        else:
            dq = lax.fori_loop(0, nkb, lambda j, c: blk(j, c, False), dq)
        dqT_ref[...] = dq

    qcol = lambda d: pl.BlockSpec((None, d, bq), lambda h, i: (h, 0, i))
    qrow = lambda d: pl.BlockSpec((None, bq, d), lambda h, i: (h, i, 0))
    krow = lambda d: pl.BlockSpec((None, Sk, d), lambda h, i: (h, 0, 0))
    return pl.pallas_call(
        body, name=name, grid=(H, nq),
        in_specs=[qcol(DK), qrow(DK), krow(DK), pl.BlockSpec((None, DK, Sk), lambda h, i: (h, 0, 0)), krow(dv),
                  qcol(dv), qcol(1), qcol(dv), qrow(dv)],
        out_specs=[qcol(DK), krow(DK), krow(dv)],
        out_shape=[jax.ShapeDtypeStruct((H, DK, S), F32), jax.ShapeDtypeStruct((H, Sk, DK), F32),
                   jax.ShapeDtypeStruct((H, Sk, dv), F32)],
        compiler_params=_cp(("parallel", "arbitrary"), ATTN_VMEM_LIMIT),
    )(qT, qn, k, kT, v, oT, lse, doT, do)


def _tri(n, fn):
    r = lax.broadcasted_iota(jnp.int32, (n, n), 0)
    c = lax.broadcasted_iota(jnp.int32, (n, n), 1)
    return jnp.where(fn(r, c), 1.0, 0.0).astype(BF16)


def _key_cumsum(x, tri2, suffix, base):
    bk = x.shape[0]
    c = min(CUMSUM_CHUNK, bk)
    n = bk // c
    hi32 = lax.bitcast_convert_type(lax.bitcast_convert_type(x, jnp.int32) & jnp.int32(-65536), F32)
    hi = hi32.astype(BF16)
    lo = (x - hi32).astype(BF16)
    tot = [jnp.sum(x[a * c:(a + 1) * c], axis=0, keepdims=True) for a in range(n)]
    outs = []
    for a in range(n):
        row = base
        for t in (tot[a + 1:] if suffix else tot[:a]):
            row = row + t
        stacked = jnp.concatenate([hi[a * c:(a + 1) * c], lo[a * c:(a + 1) * c]], axis=0)
        outs.append(_dot(tri2, stacked) + row)
    total = tot[0]
    for t in tot[1:]:
        total = total + t
    return (outs[0] if n == 1 else jnp.concatenate(outs, axis=0)), total


def _tri2(n, fn):
    t = _tri(n, fn)
    return jnp.concatenate([t, t], axis=1)


def _sb_logs(z):
    neg_abs = lax.bitcast_convert_type(lax.bitcast_convert_type(z, jnp.int32) | jnp.int32(-2 ** 31), F32)
    ls = jnp.minimum(z, 0.0) - jnp.log(1.0 + jnp.exp(neg_abs))
    return ls, ls - z


PAIR = LANE // HEAD_DIM
FWD_PAIRS = 2
SB_DEAD = -110.0
FOX_DEAD = -160.0


def _head_lanes(shape, w, axis):
    idx = lax.broadcasted_iota(jnp.int32, shape, axis)
    return (idx >= HEAD_DIM * w) & (idx < HEAD_DIM * (w + 1))


def _bias_rows(w, bq):
    row = lax.broadcasted_iota(jnp.int32, (LANE, bq), 0)
    return jnp.where((row >= 3 * w) & (row < 3 * w + 3), -1.0, 0.0).astype(BF16)


def _merge_pair(parts):
    return jnp.where(_head_lanes(parts[0].shape, 0, 0), parts[0], parts[1]).T


def _smp_fwd(q2, k2, v2, bias, r, causal, name, kstat=None):
    S, C = q2.shape
    Sk = k2.shape[0]
    bq, bk = _attn_blocks(S, Sk)
    nq, nkb, P = S // bq, Sk // bk, C // LANE
    gp = FWD_PAIRS if P % FWD_PAIRS == 0 else 1
    use_f = bias is not None
    if causal:
        assert S == Sk and bq == bk and use_f

    def body(*refs):
        if use_f:
            ks_ref, q_ref, k_ref, v_ref, b_ref, r_ref, o_ref, lse_ref, js_ref = refs
        else:
            q_ref, k_ref, v_ref, o_ref, lse_ref = refs
        i = pl.program_id(1)
        heads = range(PAIR * gp)
        lanes = [slice(LANE * (h // PAIR), LANE * (h // PAIR + 1)) for h in heads]
        qps = [q_ref[:, lanes[h]] for h in heads]
        qTs = [jnp.where(_head_lanes(qps[h].shape, h % PAIR, 1), qps[h], jnp.zeros_like(qps[h])).T for h in heads]
        if use_f:
            qf = [t.astype(F32) for t in qTs]
            qnorm = [jnp.sqrt(jnp.sum(t * t, axis=0, keepdims=True)) for t in qf]
            qTs = [jnp.concatenate([qTs[h], _bias_rows(h % PAIR, bq)], axis=0) for h in heads]

        def blk(j, carry, masked):
            off = pl.multiple_of(j * bk, bk)
            kbs = [k_ref[pl.ds(off, bk), LANE * g:LANE * (g + 1)] for g in range(gp)]
            if use_f:
                kbs = [jnp.concatenate([kbs[g], b_ref[pl.ds(off, bk), LANE * g:LANE * (g + 1)]], axis=1)
                       for g in range(gp)]
            vTbs = [v_ref[pl.ds(off, bk), LANE * g:LANE * (g + 1)].T for g in range(gp)]
            sT = [_dot(kbs[h // PAIR], qTs[h]) * LOG2E for h in heads]
            if masked:
                valid = _valid_t(i, j, bq, bk, False)
                sT = [jnp.where(valid, s, NEG_BIG) for s in sT]
            cm = [jnp.max(s, axis=0, keepdims=True) for s in sT]
            if use_f:
                cm = [cm[h] + r_ref[h] for h in heads]
            m_new = [jnp.maximum(carry[h][0], cm[h]) for h in heads]
            shift = [(m_new[h] - r_ref[h]) if use_f else m_new[h] for h in heads]
            p = [jnp.exp2(sT[h] - shift[h]) for h in heads]
            a = [jnp.exp2(carry[h][0] - m_new[h]) for h in heads]
            l = [a[h] * carry[h][1] + jnp.sum(p[h], axis=0, keepdims=True) for h in heads]
            acc = [a[h] * carry[h][2] + _dot(vTbs[h // PAIR], p[h].astype(BF16)) for h in heads]
            return tuple((m_new[h], l[h], acc[h]) for h in heads)

        def step(jj, state):
            carry, first = state
            j = i - jj
            h0 = pl.program_id(0) * (PAIR * gp)
            bound = [LOG2E * (qnorm[h] * ks_ref[(h0 + h) * nkb + j] - ks_ref[(PAIR * P + h0 + h) * nkb + j])
                     + r_ref[h] - carry[h][0] for h in heads]
            live = jnp.max(functools.reduce(jnp.maximum, bound)) >= FOX_DEAD
            carry = lax.cond(live, lambda cr: blk(j, cr, False), lambda cr: cr, carry)
            return carry, jnp.where(live, j, first)

        carry = tuple((jnp.full((1, bq), NEG_BIG, F32), jnp.zeros((1, bq), F32), jnp.zeros((LANE, bq), F32))
                      for _ in heads)
        if causal:
            carry = blk(i, carry, True)
            carry, first = lax.fori_loop(1, i + 1, step, (carry, i))
            js_ref[0] = jnp.full((1, bq), first, jnp.int32)
        else:
            carry = lax.fori_loop(0, nkb, lambda j, c: blk(j, c, False), carry)
            if use_f:
                js_ref[0] = jnp.zeros((1, bq), jnp.int32)
        for h in heads:
            lse_ref[h] = carry[h][0] + jnp.log2(carry[h][1])
        for g in range(gp):
            o_ref[:, LANE * g:LANE * (g + 1)] = _merge_pair(
                [carry[h][2] / carry[h][1] for h in range(PAIR * g, PAIR * (g + 1))])

    qblk = pl.BlockSpec((bq, LANE * gp), lambda p, i: (i, p))
    kres = pl.BlockSpec((Sk, LANE * gp), lambda p, i: (0, p))
    stat = pl.BlockSpec((PAIR * gp, 1, bq), lambda p, i: (p, 0, i))
    in_specs = [qblk, kres, kres]
    args = [q2, k2, v2]
    out_specs = [qblk, stat]
    out_shape = [jax.ShapeDtypeStruct((S, C), F32), jax.ShapeDtypeStruct((PAIR * P, 1, S), F32)]
    if use_f:
        in_specs = [pl.BlockSpec(memory_space=pltpu.SMEM)] + in_specs + [kres, stat]
        args = [kstat] + args + [bias, r]
        out_specs.append(pl.BlockSpec((1, 1, bq), lambda p, i: (p, 0, i)))
        out_shape.append(jax.ShapeDtypeStruct((P // gp, 1, S), jnp.int32))
    return pl.pallas_call(
        body, name=name, grid=(P // gp, nq), in_specs=in_specs, out_specs=out_specs, out_shape=out_shape,
        compiler_params=_cp(("parallel", "arbitrary"), ATTN_VMEM_LIMIT),
    )(*args)


def _smp_bwd(q2, k2, v2, o2, lse, do2, bias, r, scale, causal, name, first=None):
    S, C = q2.shape
    Sk = k2.shape[0]
    bq, bk = _attn_blocks(S, Sk)
    nq, nkb, P = S // bq, Sk // bk, C // LANE
    use_f = bias is not None

    def body(*refs):
        if use_f:
            (first_ref, q_ref, k_ref, v_ref, o_ref, lse_ref, do_ref, b_ref, r_ref,
             dq_ref, dk_ref, dv_ref, dr_ref, dkey_ref, dk_acc, dv_acc, db_ref) = refs
        else:
            q_ref, k_ref, v_ref, o_ref, lse_ref, do_ref, dq_ref, dk_ref, dv_ref, dk_acc, dv_acc = refs
        i = pl.program_id(1)

        @pl.when(i == 0)
        def _():
            dk_acc[...] = jnp.zeros_like(dk_acc)
            dv_acc[...] = jnp.zeros_like(dv_acc)
            if use_f:
                db_ref[...] = jnp.zeros_like(db_ref)

        qp = q_ref[...]
        dof = do_ref[...]
        prod = dof * o_ref[...]
        heads = range(PAIR)
        mine = [_head_lanes(qp.shape, w, 1) for w in heads]
        qz = [jnp.where(mine[w], qp, jnp.zeros_like(qp)) for w in heads]
        qTs = [qz[w].T for w in heads]
        if use_f:
            qTs = [jnp.concatenate([qTs[w], _bias_rows(w, bq)], axis=0) for w in heads]
        doz = [jnp.where(mine[w], dof, 0.0).astype(BF16) for w in heads]
        doT = [doz[w].T for w in heads]
        delta = [jnp.sum(jnp.where(mine[w], prod, 0.0).T, axis=0, keepdims=True) for w in heads]
        shift = [(lse_ref[w] - r_ref[w]) if use_f else lse_ref[w] for w in heads]

        def blk(j, carry, masked):
            off = pl.multiple_of(j * bk, bk)
            kb = k_ref[pl.ds(off, bk), :]
            kTb = kb.T
            if use_f:
                kb = jnp.concatenate([kb, b_ref[pl.ds(off, bk), :]], axis=1)
            vb = v_ref[pl.ds(off, bk), :]
            sT = [_dot(kb, qTs[w]) * LOG2E for w in heads]
            if masked:
                valid = _valid_t(i, j, bq, bk, False)
                sT = [jnp.where(valid, s, NEG_BIG) for s in sT]
            p = [jnp.exp2(sT[w] - shift[w]) for w in heads]
            dp = [_dot(vb, doT[w]) for w in heads]
            ds = [p[w] * (dp[w] - delta[w]) for w in heads]
            dsb = [d.astype(BF16) for d in ds]
            dvs = [_dot(p[w].astype(BF16), doz[w]) for w in heads]
            dks = [_dot(dsb[w], qz[w]) for w in heads]
            dv_acc[pl.ds(off, bk), :] += dvs[0] + dvs[1]
            dk_acc[pl.ds(off, bk), :] += dks[0] + dks[1]
            dr = [carry[w][1] for w in heads]
            if use_f:
                dr = [dr[w] + jnp.sum(ds[w], axis=0, keepdims=True) for w in heads]
                lane = lax.broadcasted_iota(jnp.int32, (bk, LANE), 1)
                cols = [jnp.where(lane == w, jnp.sum(ds[w], axis=1, keepdims=True), 0.0) for w in heads]
                db_ref[pl.ds(off, bk), :] += cols[0] + cols[1]
            dq = [carry[w][0] + _dot(kTb, dsb[w]) for w in heads]
            return tuple((dq[w], dr[w]) for w in heads)

        carry = tuple((jnp.zeros((LANE, bq), F32), jnp.zeros((1, bq), F32)) for _ in heads)
        if causal:
            start = first_ref[pl.program_id(0) // (P // first.shape[0]), i]
            carry = lax.fori_loop(start, i, lambda j, c: blk(j, c, False), carry)
            carry = blk(i, carry, True)
        else:
            carry = lax.fori_loop(0, nkb, lambda j, c: blk(j, c, False), carry)
        if use_f:
            for w in heads:
                dr_ref[w] = carry[w][1]
        dq_ref[...] = (_merge_pair([carry[w][0] for w in heads]) * scale).astype(BF16)

        @pl.when(i == nq - 1)
        def _():
            dk_ref[...] = dk_acc[...].astype(BF16)
            dv_ref[...] = dv_acc[...].astype(BF16)

        if use_f:
            @pl.when(i == nq - 1)
            def _():
                def chunk(cidx, carry):
                    off = pl.multiple_of(cidx * LANE, LANE)
                    t = db_ref[pl.ds(off, LANE), :].T
                    for w in range(PAIR):
                        dkey_ref[w, :, pl.ds(off, LANE)] = t[w:w + 1, :]
                    return carry

                lax.fori_loop(0, Sk // LANE, chunk, 0)

    qblk = pl.BlockSpec((bq, LANE), lambda p, i: (i, p))
    kres = pl.BlockSpec((Sk, LANE), lambda p, i: (0, p))
    stat = pl.BlockSpec((PAIR, 1, bq), lambda p, i: (p, 0, i))
    in_specs = [qblk, kres, kres, qblk, stat, qblk]
    args = [q2, k2, v2, o2, lse, do2]
    out_specs = [qblk, kres, kres]
    out_shape = [jax.ShapeDtypeStruct((S, C), BF16), jax.ShapeDtypeStruct((Sk, C), BF16),
                 jax.ShapeDtypeStruct((Sk, C), BF16)]
    scratch = [pltpu.VMEM((Sk, LANE), F32), pltpu.VMEM((Sk, LANE), F32)]
    if use_f:
        in_specs = [pl.BlockSpec(memory_space=pltpu.SMEM)] + in_specs + [kres, stat]
        args = [first] + args + [bias, r]
        out_specs += [stat, pl.BlockSpec((PAIR, 1, Sk), lambda p, i: (p, 0, 0))]
        out_shape += [jax.ShapeDtypeStruct((PAIR * P, 1, S), F32), jax.ShapeDtypeStruct((PAIR * P, 1, Sk), F32)]
        scratch.append(pltpu.VMEM((Sk, LANE), F32))
    return pl.pallas_call(
        body, name=name, grid=(P, nq), in_specs=in_specs, out_specs=out_specs, out_shape=out_shape,
        scratch_shapes=scratch, compiler_params=_cp(("parallel", "arbitrary"), ATTN_VMEM_LIMIT),
    )(*args)


def _sbp_fwd(q2, k2, v2, name):
    S, C = q2.shape
    bq, bk = _attn_blocks(S, S)
    assert bq == bk
    nq, P = S // bq, C // LANE
    gp = FWD_PAIRS if P % FWD_PAIRS == 0 else 1
    c = min(CUMSUM_CHUNK, bk)

    def body(q_ref, k_ref, v_ref, o_ref, lt_ref, js_ref):
        i = pl.program_id(1)
        after = _tri2(c, lambda s, j: j > s)
        heads = range(PAIR * gp)
        qps = [q_ref[:, LANE * (h // PAIR):LANE * (h // PAIR + 1)] for h in heads]
        qTs = [jnp.where(_head_lanes(qps[h].shape, h % PAIR, 1), qps[h], jnp.zeros_like(qps[h])).T for h in heads]

        def blk(jj, carry, masked):
            j = i - jj
            off = pl.multiple_of(j * bk, bk)
            kbs = [k_ref[pl.ds(off, bk), LANE * g:LANE * (g + 1)] for g in range(gp)]
            vTbs = [v_ref[pl.ds(off, bk), LANE * g:LANE * (g + 1)].T for g in range(gp)]
            logs = [_sb_logs(_dot(kbs[h // PAIR], qTs[h])) for h in heads]
            ls, lk = [t[0] for t in logs], [t[1] for t in logs]
            if masked:
                valid = _valid_t(i, j, bq, bk, True)
                lk = [jnp.where(valid, t, 0.0) for t in lk]
            cs = [_key_cumsum(lk[h], after, True, carry[h][0]) for h in heads]
            wgt = [jnp.exp(ls[h] + cs[h][0]) for h in heads]
            if masked:
                wgt = [jnp.where(valid, t, 0.0) for t in wgt]
            acc = [carry[h][1] + _dot(vTbs[h // PAIR], wgt[h].astype(BF16)) for h in heads]
            return tuple((carry[h][0] + cs[h][1], acc[h]) for h in heads)

        def step(jj, state):
            carry, first = state
            live = jnp.max(functools.reduce(jnp.maximum, [carry[h][0] for h in heads])) >= SB_DEAD
            carry = lax.cond(live, lambda cr: blk(jj, cr, False), lambda cr: cr, carry)
            return carry, jnp.where(live, i - jj, first)

        carry = tuple((jnp.zeros((1, bq), F32), jnp.zeros((LANE, bq), F32)) for _ in heads)
        carry = blk(0, carry, True)
        carry, first = lax.fori_loop(1, i + 1, step, (carry, i))
        js_ref[0] = jnp.full((1, bq), first, jnp.int32)
        for h in heads:
            lt_ref[h] = carry[h][0]
        for g in range(gp):
            o_ref[:, LANE * g:LANE * (g + 1)] = _merge_pair([carry[h][1] for h in range(PAIR * g, PAIR * (g + 1))])

    qblk = pl.BlockSpec((bq, LANE * gp), lambda p, i: (i, p))
    kres = pl.BlockSpec((S, LANE * gp), lambda p, i: (0, p))
    stat = pl.BlockSpec((PAIR * gp, 1, bq), lambda p, i: (p, 0, i))
    return pl.pallas_call(
        body, name=name, grid=(P // gp, nq),
        in_specs=[qblk, kres, kres],
        out_specs=[qblk, stat, pl.BlockSpec((1, 1, bq), lambda p, i: (p, 0, i))],
        out_shape=[jax.ShapeDtypeStruct((S, C), F32), jax.ShapeDtypeStruct((PAIR * P, 1, S), F32),
                   jax.ShapeDtypeStruct((P // gp, 1, S), jnp.int32)],
        compiler_params=_cp(("parallel", "arbitrary"), ATTN_VMEM_LIMIT),
    )(q2, k2, v2)


def _sbp_bwd(q2, k2, v2, lt, first, do2, scale, name):
    S, C = q2.shape
    bq, bk = _attn_blocks(S, S)
    nq, P = S // bq, C // LANE
    c = min(CUMSUM_CHUNK, bk)

    per_group = P // first.shape[0]

    def body(first_ref, q_ref, k_ref, v_ref, lt_ref, do_ref, dq_ref, dk_ref, dv_ref, dk_acc, dv_acc):
        i = pl.program_id(1)

        @pl.when(i == 0)
        def _():
            dk_acc[...] = jnp.zeros_like(dk_acc)
            dv_acc[...] = jnp.zeros_like(dv_acc)

        qp = q_ref[...]
        dof = do_ref[...]
        upto = _tri2(c, lambda s, j: j <= s)
        before = _tri2(c, lambda s, j: j < s)
        heads = range(PAIR)
        mine = [_head_lanes(qp.shape, w, 1) for w in heads]
        qz = [jnp.where(mine[w], qp, jnp.zeros_like(qp)) for w in heads]
        qTs = [qz[w].T for w in heads]
        doz = [jnp.where(mine[w], dof, 0.0).astype(BF16) for w in heads]
        doT = [doz[w].T for w in heads]
        ltot = [lt_ref[w] for w in heads]

        def blk(j, carry, masked):
            off = pl.multiple_of(j * bk, bk)
            kb = k_ref[pl.ds(off, bk), :]
            vb = v_ref[pl.ds(off, bk), :]
            kTb = kb.T
            logs = [_sb_logs(_dot(kb, qTs[w])) for w in heads]
            ls, lk = [t[0] for t in logs], [t[1] for t in logs]
            if masked:
                valid = _valid_t(i, j, bq, bk, True)
                lk = [jnp.where(valid, t, 0.0) for t in lk]
            pin = [_key_cumsum(lk[w], upto, False, carry[w][1] - ltot[w]) for w in heads]
            wgt = [jnp.exp(ls[w] - pin[w][0]) for w in heads]
            if masked:
                wgt = [jnp.where(valid, t, 0.0) for t in wgt]
            g = [_dot(vb, doT[w]) * wgt[w] for w in heads]
            cin = [_key_cumsum(g[w], before, False, carry[w][2]) for w in heads]
            sig = [jnp.exp(t) for t in ls]
            dz = [g[w] * (1.0 - sig[w]) - cin[w][0] * sig[w] for w in heads]
            if masked:
                dz = [jnp.where(valid, t, 0.0) for t in dz]
            dzb = [t.astype(BF16) for t in dz]
            dvs = [_dot(wgt[w].astype(BF16), doz[w]) for w in heads]
            dks = [_dot(dzb[w], qz[w]) for w in heads]
            dv_acc[pl.ds(off, bk), :] += dvs[0] + dvs[1]
            dk_acc[pl.ds(off, bk), :] += dks[0] + dks[1]
            return tuple((carry[w][0] + _dot(kTb, dzb[w]), carry[w][1] + pin[w][1], carry[w][2] + cin[w][1])
                         for w in heads)

        carry = tuple((jnp.zeros((LANE, bq), F32), jnp.zeros((1, bq), F32), jnp.zeros((1, bq), F32)) for _ in heads)
        start = first_ref[pl.program_id(0) // per_group, i]
        carry = lax.fori_loop(start, i, lambda j, cr: blk(j, cr, False), carry)
        carry = blk(i, carry, True)
        dq_ref[...] = (_merge_pair([carry[w][0] for w in heads]) * scale).astype(BF16)

        @pl.when(i == nq - 1)
        def _():
            dk_ref[...] = dk_acc[...].astype(BF16)
            dv_ref[...] = dv_acc[...].astype(BF16)

    qblk = pl.BlockSpec((bq, LANE), lambda p, i: (i, p))
    kres = pl.BlockSpec((S, LANE), lambda p, i: (0, p))
    stat = pl.BlockSpec((PAIR, 1, bq), lambda p, i: (p, 0, i))
    return pl.pallas_call(
        body, name=name, grid=(P, nq),
        in_specs=[pl.BlockSpec(memory_space=pltpu.SMEM), qblk, kres, kres, stat, qblk],
        out_specs=[qblk, kres, kres],
        out_shape=[jax.ShapeDtypeStruct((S, C), BF16)] * 3,
        scratch_shapes=[pltpu.VMEM((S, LANE), F32), pltpu.VMEM((S, LANE), F32)],
        compiler_params=_cp(("parallel", "arbitrary"), ATTN_VMEM_LIMIT),
    )(first, q2, k2, v2, lt, do2)


def _bias_cols(f_cum):
    H, Sk = f_cum.shape
    terms = jnp.stack(_split3(f_cum), axis=-1)
    packed = terms.reshape(H // PAIR, PAIR, Sk, 3).transpose(2, 0, 1, 3).reshape(Sk, H // PAIR, PAIR * 3)
    return jnp.pad(packed, ((0, 0), (0, 0), (0, LANE - PAIR * 3))).reshape(Sk, -1)


def _make_packed_softmax(name, scale, causal, use_f):
    assert _pow2(scale)

    def run_fwd(q16, k16, v16, f_cum):
        q16 = q16 * scale
        if not use_f:
            o, lse = _smp_fwd(q16, k16, v16, None, None, causal, name + "_fwd")
            return o, (q16, k16, v16, o, lse, None, None, None)
        bq, bk = _attn_blocks(q16.shape[0], k16.shape[0])
        n_heads = f_cum.shape[0]
        knorm = jnp.sqrt(jnp.sum(jnp.square(k16.astype(F32)).reshape(-1, bk, n_heads, HEAD_DIM), axis=3))
        kstat = jnp.concatenate([jnp.max(knorm, axis=1).T.reshape(-1), f_cum[:, bk - 1::bk].reshape(-1)])
        bias, r = _bias_cols(f_cum), (f_cum * LOG2E)[:, None, :]
        o, lse, first = _smp_fwd(q16, k16, v16, bias, r, causal, name + "_fwd", lax.stop_gradient(kstat))
        return o, (q16, k16, v16, o, lse, bias, r, first[:, 0, ::bq])

    def run_bwd(saved, do):
        q16, k16, v16, o, lse, bias, r, first = saved
        outs = _smp_bwd(q16, k16, v16, o, lse, do, bias, r, scale, causal, name + "_bwd", first)
        if use_f:
            return outs[0], outs[1], outs[2], outs[3][:, 0, :] - outs[4][:, 0, :]
        return tuple(outs)

    if use_f:
        @jax.custom_vjp
        def attn(q, k, v, f_cum):
            return run_fwd(q, k, v, f_cum)[0]

        attn.defvjp(run_fwd, run_bwd)
    else:
        @jax.custom_vjp
        def attn(q, k, v):
            return run_fwd(q, k, v, None)[0]

        attn.defvjp(lambda q, k, v: run_fwd(q, k, v, None), run_bwd)
    return attn


def _make_packed_sb(name, scale):
    assert _pow2(scale)

    def run_fwd(q16, k16, v16):
        q16 = q16 * scale
        o, lt, first = _sbp_fwd(q16, k16, v16, name + "_fwd")
        bq, _ = _attn_blocks(q16.shape[0], q16.shape[0])
        return o, (q16, k16, v16, lt, first[:, 0, ::bq])

    def run_bwd(saved, do):
        q16, k16, v16, lt, first = saved
        return tuple(_sbp_bwd(q16, k16, v16, lt, first, do, scale, name + "_bwd"))

    @jax.custom_vjp
    def attn(q, k, v):
        return run_fwd(q, k, v)[0]

    attn.defvjp(run_fwd, run_bwd)
    return attn


def _round_bf16(x):
    return lax.reduce_precision(x, exponent_bits=8, mantissa_bits=7)


def _split3(x):
    hi = _round_bf16(x)
    mid = _round_bf16(x - hi)
    lo = _round_bf16(x - hi - mid)
    return hi.astype(BF16), mid.astype(BF16), lo.astype(BF16)


def _pow2(x):
    m, _ = math.frexp(x)
    return m == 0.5


def _pad_last(x, n):
    return jnp.pad(x, [(0, 0)] * (x.ndim - 1) + [(0, n - x.shape[-1])])


def _layouts(q, k, scale):
    qh = _pad_last(jnp.transpose(q * scale if _pow2(scale) else q, (1, 0, 2)).astype(BF16), LANE)
    kh = _pad_last(jnp.transpose(k, (1, 0, 2)).astype(BF16), LANE)
    return qh, jnp.transpose(qh, (0, 2, 1)), kh, jnp.transpose(kh, (0, 2, 1))


def _make_softmax_attn(name, scale, causal, d):
    pre = _pow2(scale)
    cmul = LOG2E if pre else scale * LOG2E
    gscale = 1.0 if pre else scale

    def run_fwd(q, k, v):
        qn, qT, kn, kT = _layouts(q, k, scale)
        vn = jnp.transpose(v, (1, 0, 2)).astype(BF16)
        oT, lse = _sm_fwd_t(qT, kn, jnp.transpose(vn, (0, 2, 1)), cmul, causal, name + "_fwd")
        return jnp.transpose(oT, (2, 0, 1)), (qn, qT, kn, kT, vn, oT, lse)

    def run_bwd(saved, dout):
        qn, qT, kn, kT, vn, oT, lse = saved
        doT = jnp.transpose(dout, (1, 2, 0))
        do = jnp.transpose(dout, (1, 0, 2)).astype(BF16)
        dqT, dk, dv = _sm_bwd_t(qT, qn, kn, kT, vn, oT, lse, doT, do, cmul, gscale, causal, name + "_bwd")
        dq = jnp.transpose(dqT[:, :d, :], (2, 0, 1))
        if pre:
            dq = dq * scale
        return dq, jnp.transpose(dk[:, :, :d], (1, 0, 2)), jnp.transpose(dv, (1, 0, 2))

    @jax.custom_vjp
    def attn(q, k, v):
        return run_fwd(q, k, v)[0]

    attn.defvjp(run_fwd, run_bwd)
    return attn


def _rope(x, positions):
    half = x.shape[-1] // 2
    inv_freq = ROPE_THETA ** (-jnp.arange(half, dtype=F32) / half)
    ang = positions.astype(F32)[:, None] * inv_freq[None, :]
    ang = ang.reshape((ang.shape[0],) + (1,) * (x.ndim - 2) + (half,))
    cos, sin = jnp.cos(ang), jnp.sin(ang)
    x1, x2 = x[..., :half], x[..., half:]
    return jnp.concatenate([x1 * cos - x2 * sin, x1 * sin + x2 * cos], axis=-1)


def _permute_w_in(w):
    parts = [w[:, _ORIG_OFF[idx]:_ORIG_OFF[idx] + SPLIT_SIZES[idx]] for _, idx in _PERM]
    pad = jnp.zeros((w.shape[0], PROJ_COLS - IN_COLS), w.dtype)
    return jnp.concatenate(parts + [pad], axis=1)


_BF16_PIECES = ("fq", "fk", "fv", "sq", "sk", "sv", "mq")


def _make_ln_proj(name, has_res):
    def split(proj32, proj16):
        out, off = [], 0
        for n, idx in _PERM:
            src = proj16 if n in _BF16_PIECES else proj32
            out.append(src[:, off:off + SPLIT_SIZES[idx]])
            off += SPLIT_SIZES[idx]
        return tuple(out)

    def run_fwd(x, res, g, b, w):
        h, h16, hT16 = _ln_fwd_call(x, res, g, b, name + "_ln_fwd", also16=True)
        w16 = w.astype(BF16)
        proj32, proj16 = _matmul(h16, w16, "nn", name + "_fwd", also16=True)
        return (h, split(proj32, proj16)), (x, res, g, hT16, w16)

    def run_bwd(saved, cts):
        x, res, g, hT16, w16 = saved
        dh, dpieces = cts
        pad = jnp.zeros((x.shape[0], PROJ_COLS - IN_COLS), BF16)
        dy16 = jnp.concatenate([c.astype(BF16) for c in dpieces] + [pad], axis=1)
        da = _matmul(dy16, w16, "nt", name + "_dx")
        dw = _matmul(hT16, dy16, "nn", name + "_dw")
        outs = _ln_bwd_call(dh, x, res, g, name + "_ln_bwd", dy2=da)
        if has_res:
            dx, dr, dg, db = outs
            return dx, dr, dg.reshape(-1), db.reshape(-1), dw
        dx, dg, db = outs
        return dx, dg.reshape(-1), db.reshape(-1), dw

    if has_res:
        @jax.custom_vjp
        def op(x, res, g, b, w):
            return run_fwd(x, res, g, b, w)[0]

        op.defvjp(run_fwd, run_bwd)
    else:
        @jax.custom_vjp
        def op(x, g, b, w):
            return run_fwd(x, None, g, b, w)[0]

        op.defvjp(lambda x, g, b, w: run_fwd(x, None, g, b, w), run_bwd)

    def call(*args):
        h, pieces = op(*args)
        return h, {n: part for (n, _), part in zip(_PERM, pieces)}

    return call


def _trunk_loss(wts, x2d, mem2d, target2d):
    s = x2d.shape[0]
    positions = jnp.arange(s)
    head_scale = HEAD_DIM ** -0.5
    mla_scale = (MLA_NOPE + MLA_ROPE) ** -0.5

    mem_n = _make_ln("ln_mem", False)(mem2d, wts["mem_ln_g"], wts["mem_ln_b"])
    h, y = None, x2d
    for l in range(DEPTH):
        tag = f"l{l}_"
        w_p = _permute_w_in(wts["w_in"][l])
        if l == 0:
            h, p = _make_ln_proj(tag + "proj", False)(y, wts["ln_in_g"], wts["ln_in_b"], w_p)
        else:
            h, p = _make_ln_proj(tag + "proj", True)(y, h, wts["ln_g"][l - 1], wts["ln_b"][l - 1], w_p)

        log_f = jax.nn.log_sigmoid(p["f_logit"] + wts["b_forget"][l])
        f_cum = jnp.cumsum(log_f, axis=0).T
        out_fox = _make_packed_softmax(tag + "fox", head_scale, True, True)(p["fq"], p["fk"], p["fv"], f_cum)

        out_sb = _make_packed_sb(tag + "sb", head_scale)(p["sq"], p["sk"], p["sv"])

        cqn = _make_rms(tag + "rms_q")(p["c_q"], wts["mla_q_norm_g"][l])
        q_mla = _make_mm(tag + "q_up")(cqn, wts["w_mla_q_up"][l]).reshape(s, N_HEADS, MLA_NOPE + MLA_ROPE)
        ckvn = _make_rms(tag + "rms_kv")(p["c_kv"], wts["mla_kv_norm_g"][l])
        kv_mla = _make_mm(tag + "kv_up")(ckvn, wts["w_mla_kv_up"][l]).reshape(s, N_HEADS, MLA_NOPE + MLA_V)
        q_full = jnp.concatenate([q_mla[..., :MLA_NOPE], _rope(q_mla[..., MLA_NOPE:], positions)], axis=-1)
        k_rope = jnp.broadcast_to(_rope(p["k_rot"], positions)[:, None, :], (s, N_HEADS, MLA_ROPE))
        k_full = jnp.concatenate([kv_mla[..., :MLA_NOPE], k_rope], axis=-1)
        out_mla = _make_softmax_attn(tag + "mla", mla_scale, True, MLA_NOPE + MLA_ROPE)(
            q_full, k_full, kv_mla[..., MLA_NOPE:]).reshape(s, GROUP_W)

        mkv = _make_mm(tag + "mem_kv")(mem_n, wts["w_mem_kv"][l])
        out_mem = _make_packed_softmax(tag + "mem", head_scale, False, False)(
            p["mq"], mkv[:, :GROUP_W].astype(BF16), mkv[:, GROUP_W:].astype(BF16))

        mixed = jnp.concatenate([out_fox, out_sb, out_mla, out_mem], axis=-1)
        y = _make_gate_out(tag + "out")(mixed, p["gate"], wts["w_out"][l])

    h = _make_ln(f"l{DEPTH - 1}_ln", True)(y, h, wts["ln_g"][DEPTH - 1], wts["ln_b"][DEPTH - 1])
    return _loss_op(h, target2d)


def _mesh_pos():
    x, y, c = (lax.axis_index(a) for a in MESH_AXES)
    return x, y, c, 4 * x + 2 * y + c


def _peer(x, y, c, mask):
    return (x ^ ((mask >> 2) & 1), y ^ ((mask >> 1) & 1), c ^ (mask & 1))


_ANY = pl.BlockSpec(memory_space=pl.ANY)


def _all_gather(row_shards, stack_shards):
    n_row, n_all = len(row_shards), len(row_shards) + len(stack_shards)
    shards = list(row_shards) + list(stack_shards)

    def body(*refs):
        ins, outs = refs[:n_all], refs[n_all:2 * n_all]
        send_sems, recv_sems, local_sems = refs[2 * n_all:]
        x, y, c, me = _mesh_pos()

        def window(t, slot):
            if t < n_row:
                rows = shards[t].shape[1]
                return outs[t].at[:, pl.ds(slot * rows, rows), :]
            return outs[t].at[slot]

        local = [pltpu.make_async_copy(ins[t], window(t, me), local_sems.at[t]) for t in range(n_all)]
        for cp in local:
            cp.start()
        sends = []
        for mask in range(1, N_DEV):
            for t in range(n_all):
                cp = pltpu.make_async_remote_copy(
                    src_ref=ins[t], dst_ref=window(t, me), send_sem=send_sems.at[t, mask - 1],
                    recv_sem=recv_sems.at[t, mask - 1], device_id=_peer(x, y, c, mask),
                    device_id_type=pl.DeviceIdType.MESH)
                cp.start()
                sends.append(cp)
        for mask in range(1, N_DEV):
            for t in range(n_all):
                pltpu.make_async_remote_copy(
                    src_ref=ins[t], dst_ref=window(t, me ^ mask), send_sem=send_sems.at[t, mask - 1],
                    recv_sem=recv_sems.at[t, mask - 1], device_id=_peer(x, y, c, mask),
                    device_id_type=pl.DeviceIdType.MESH).wait_recv()
        for cp in sends:
            cp.wait_send()
        for cp in local:
            cp.wait()

    out_shape = [jax.ShapeDtypeStruct((a.shape[0], N_DEV * a.shape[1], a.shape[2]), a.dtype) for a in row_shards]
    out_shape += [jax.ShapeDtypeStruct((N_DEV,) + a.shape, a.dtype) for a in stack_shards]
    return pl.pallas_call(
        body, name="all_gather_weights", in_specs=[_ANY] * n_all, out_specs=[_ANY] * n_all, out_shape=out_shape,
        scratch_shapes=[pltpu.SemaphoreType.DMA((n_all, N_DEV - 1)), pltpu.SemaphoreType.DMA((n_all, N_DEV - 1)),
                        pltpu.SemaphoreType.DMA((n_all,))],
    )(*shards)


def _reduce_scatter(row_full, stack_full, bcast):
    n_row, n_stack = len(row_full), len(stack_full)
    n_all = n_row + n_stack + len(bcast)
    fulls = list(row_full) + list(stack_full) + list(bcast)

    def body(*refs):
        ins, outs = refs[:n_all], refs[n_all:2 * n_all]
        send_sems, recv_sems, local_sems = refs[2 * n_all:]
        x, y, c, me = _mesh_pos()

        def part(t, slot):
            if t < n_row:
                rows = fulls[t].shape[1] // N_DEV
                return ins[t].at[:, pl.ds(slot * rows, rows), :]
            if t < n_row + n_stack:
                return ins[t].at[slot]
            return ins[t]

        local = [pltpu.make_async_copy(part(t, me), outs[t].at[me], local_sems.at[t]) for t in range(n_all)]
        for cp in local:
            cp.start()
        sends = []
        for mask in range(1, N_DEV):
            for t in range(n_all):
                cp = pltpu.make_async_remote_copy(
                    src_ref=part(t, me ^ mask), dst_ref=outs[t].at[me], send_sem=send_sems.at[t, mask - 1],
                    recv_sem=recv_sems.at[t, mask - 1], device_id=_peer(x, y, c, mask),
                    device_id_type=pl.DeviceIdType.MESH)
                cp.start()
                sends.append(cp)
        for mask in range(1, N_DEV):
            for t in range(n_all):
                pltpu.make_async_remote_copy(
                    src_ref=part(t, me), dst_ref=outs[t].at[me ^ mask], send_sem=send_sems.at[t, mask - 1],
                    recv_sem=recv_sems.at[t, mask - 1], device_id=_peer(x, y, c, mask),
                    device_id_type=pl.DeviceIdType.MESH).wait_recv()
        for cp in sends:
            cp.wait_send()
        for cp in local:
            cp.wait()

    out_shape = [jax.ShapeDtypeStruct((N_DEV, a.shape[0], a.shape[1] // N_DEV, a.shape[2]), a.dtype) for a in row_full]
    out_shape += [jax.ShapeDtypeStruct(a.shape, a.dtype) for a in stack_full]
    out_shape += [jax.ShapeDtypeStruct((N_DEV,) + a.shape, a.dtype) for a in bcast]
    return pl.pallas_call(
        body, name="reduce_scatter_grads", in_specs=[_ANY] * n_all, out_specs=[_ANY] * n_all, out_shape=out_shape,
        scratch_shapes=[pltpu.SemaphoreType.DMA((n_all, N_DEV - 1)), pltpu.SemaphoreType.DMA((n_all, N_DEV - 1)),
                        pltpu.SemaphoreType.DMA((n_all,))],
    )(*fulls)


def _adamw(slots, w, m, v, name):
    shape = w.shape
    cols = shape[-1]
    rows = math.prod(shape[:-1])
    tr = _pick(rows, (64, 32, 16, 8))
    c1 = 1.0 - ADAM_B1 ** ADAM_STEP
    c2 = 1.0 - ADAM_B2 ** ADAM_STEP

    def body(s_ref, w_ref, m_ref, v_ref, g_ref, d_ref, nm_ref, nv_ref):
        g = s_ref[0].astype(F32)
        for k in range(1, N_DEV):
            g = g + s_ref[k].astype(F32)
        nm = ADAM_B1 * m_ref[...] + (1.0 - ADAM_B1) * g
        nv = ADAM_B2 * v_ref[...] + (1.0 - ADAM_B2) * (g * g)
        g_ref[...] = g
        nm_ref[...] = nm
        nv_ref[...] = nv
        d_ref[...] = -ADAM_LR * ((nm / c1) / (jnp.sqrt(nv / c2) + ADAM_EPS) + ADAM_WD * w_ref[...])

    row = pl.BlockSpec((tr, cols), lambda i: (i, 0))
    out = jax.ShapeDtypeStruct((rows, cols), F32)
    outs = pl.pallas_call(
        body, name=name, grid=(rows // tr,),
        in_specs=[pl.BlockSpec((N_DEV, tr, cols), lambda i: (0, i, 0)), row, row, row],
        out_specs=[row] * 4, out_shape=[out] * 4, compiler_params=_cp(("parallel",)),
    )(slots.reshape(N_DEV, rows, cols), w.reshape(rows, cols), m.reshape(rows, cols), v.reshape(rows, cols))
    return [o.reshape(shape) for o in outs]


_SMALL = ("ln_in_g", "ln_in_b", "mem_ln_g", "mem_ln_b", "b_forget", "mla_q_norm_g", "mla_kv_norm_g", "ln_g", "ln_b")
_ORDER = ("ln_in_g", "ln_in_b", "mem_ln_g", "mem_ln_b", "w_in", "b_forget", "mla_q_norm_g", "w_mla_q_up",
          "mla_kv_norm_g", "w_mla_kv_up", "w_mem_kv", "w_out", "ln_g", "ln_b")


def _pack_small(d):
    flat = jnp.concatenate([d[n].reshape(-1) for n in _SMALL])
    n = flat.shape[0]
    padded = ((n + 8 * LANE - 1) // (8 * LANE)) * (8 * LANE)
    return jnp.pad(flat, (0, padded - n)).reshape(-1, LANE)


def _unpack_small(packed, like):
    flat, out, off = packed.reshape(-1), {}, 0
    for n in _SMALL:
        size = math.prod(like[n].shape)
        out[n] = flat[off:off + size].reshape(like[n].shape)
        off += size
    return out


def _unstack_cols(g):
    n, l, r, c = g.shape
    return g.transpose(1, 2, 0, 3).reshape(l, r, n * c)


def _stack_cols(g):
    l, r, nc = g.shape
    return g.reshape(l, r, N_DEV, nc // N_DEV).transpose(2, 0, 1, 3)


def kernel(x, mem, ln_in_g, ln_in_b, mem_ln_g, mem_ln_b, w_in, b_forget, mla_q_norm_g, w_mla_q_up, mla_kv_norm_g, w_mla_kv_up, w_mem_kv, w_out, ln_g, ln_b, loss_target, m_ln_in_g, m_ln_in_b, m_mem_ln_g, m_mem_ln_b, m_w_in, m_b_forget, m_mla_q_norm_g, m_w_mla_q_up, m_mla_kv_norm_g, m_w_mla_kv_up, m_w_mem_kv, m_w_out, m_ln_g, m_ln_b, v_ln_in_g, v_ln_in_b, v_mem_ln_g, v_mem_ln_b, v_w_in, v_b_forget, v_mla_q_norm_g, v_w_mla_q_up, v_mla_kv_norm_g, v_w_mla_kv_up, v_w_mem_kv, v_w_out, v_ln_g, v_ln_b):
    w_shard = dict(ln_in_g=ln_in_g, ln_in_b=ln_in_b, mem_ln_g=mem_ln_g, mem_ln_b=mem_ln_b, w_in=w_in,
                   b_forget=b_forget, mla_q_norm_g=mla_q_norm_g, w_mla_q_up=w_mla_q_up,
                   mla_kv_norm_g=mla_kv_norm_g, w_mla_kv_up=w_mla_kv_up, w_mem_kv=w_mem_kv, w_out=w_out,
                   ln_g=ln_g, ln_b=ln_b)
    m_shard = dict(ln_in_g=m_ln_in_g, ln_in_b=m_ln_in_b, mem_ln_g=m_mem_ln_g, mem_ln_b=m_mem_ln_b, w_in=m_w_in,
                   b_forget=m_b_forget, mla_q_norm_g=m_mla_q_norm_g, w_mla_q_up=m_w_mla_q_up,
                   mla_kv_norm_g=m_mla_kv_norm_g, w_mla_kv_up=m_w_mla_kv_up, w_mem_kv=m_w_mem_kv, w_out=m_w_out,
                   ln_g=m_ln_g, ln_b=m_ln_b)
    v_shard = dict(ln_in_g=v_ln_in_g, ln_in_b=v_ln_in_b, mem_ln_g=v_mem_ln_g, mem_ln_b=v_mem_ln_b, w_in=v_w_in,
                   b_forget=v_b_forget, mla_q_norm_g=v_mla_q_norm_g, w_mla_q_up=v_w_mla_q_up,
                   mla_kv_norm_g=v_mla_kv_norm_g, w_mla_kv_up=v_w_mla_kv_up, w_mem_kv=v_w_mem_kv, w_out=v_w_out,
                   ln_g=v_ln_g, ln_b=v_ln_b)

    to16 = lambda ws: [a.astype(BF16) for a in ws]
    gathered = _all_gather(to16([w_in, w_mem_kv, w_out]), to16([w_mla_q_up, w_mla_kv_up]))
    g_in, g_mem, g_out, g_qup, g_kvup = [a.astype(F32) for a in gathered]
    full = dict(w_shard)
    full.update(w_in=g_in, w_mem_kv=g_mem, w_out=g_out, w_mla_q_up=_unstack_cols(g_qup),
                w_mla_kv_up=_unstack_cols(g_kvup))

    loss_local, (grad_w, grad_x) = jax.value_and_grad(_trunk_loss, argnums=(0, 1))(
        full, x[0], mem[0], loss_target[0])

    s_in, s_mem, s_out, s_qup, s_kvup, s_small = _reduce_scatter(
        to16([grad_w["w_in"], grad_w["w_mem_kv"], grad_w["w_out"]]),
        to16([_stack_cols(grad_w["w_mla_q_up"]), _stack_cols(grad_w["w_mla_kv_up"])]),
        [_pack_small(grad_w)])

    res = {}
    for name, slots in (("w_in", s_in), ("w_mem_kv", s_mem), ("w_out", s_out), ("w_mla_q_up", s_qup),
                        ("w_mla_kv_up", s_kvup)):
        res[name] = _adamw(slots, w_shard[name], m_shard[name], v_shard[name], "adamw_" + name)
    small = _adamw(s_small, _pack_small(w_shard), _pack_small(m_shard), _pack_small(v_shard), "adamw_small")
    small = [_unpack_small(a, w_shard) for a in small]
    for name in _SMALL:
        res[name] = [a[name] for a in small]

    loss = lax.psum(loss_local, MESH_AXES)
    outs = [loss, grad_x[None]]
    for k in range(4):
        outs += [res[name][k] for name in _ORDER]
    return tuple(outs)
```

```python
import functools
import math

import jax
import jax.numpy as jnp
from jax import lax
from jax.experimental import pallas as pl
from jax.experimental.pallas import tpu as pltpu

F32 = jnp.float32
BF16 = jnp.bfloat16

D_MODEL = 1024
DEPTH = 2
GROUP_W = 256
N_HEADS = 4
HEAD_DIM = 64
MLA_Q_RANK = 256
MLA_KV_RANK = 128
MLA_NOPE = 64
MLA_ROPE = 32
MLA_V = 64
ROPE_THETA = 10000.0
LN_EPS = 1e-5
RMS_EPS = 1e-6
DEEPNORM_ALPHA = (2 * DEPTH) ** 0.25
SPLIT_SIZES = (256, 256, 256, 4, 256, 256, 256, 256, 128, 32, 256, 1024)
IN_COLS = sum(SPLIT_SIZES)
_ORIG_OFF = [sum(SPLIT_SIZES[:i]) for i in range(len(SPLIT_SIZES))]
_PERM = (("fq", 0), ("fk", 1), ("fv", 2), ("sq", 4), ("sk", 5), ("sv", 6), ("c_q", 7), ("c_kv", 8),
         ("mq", 10), ("gate", 11), ("k_rot", 9), ("f_logit", 3))
LANE = 128
PROJ_COLS = ((IN_COLS + LANE - 1) // LANE) * LANE

ADAM_LR = 0.001
ADAM_B1 = 0.9
ADAM_B2 = 0.999
ADAM_EPS = 1e-08
ADAM_WD = 0.01
ADAM_STEP = 10

N_DEV = 8
MESH_AXES = ("x", "y", "c")
VMEM_LIMIT = 48 * 1024 * 1024
ATTN_VMEM_LIMIT = 56 * 1024 * 1024
ATTN_BQ = 512
ATTN_BK = 512
CUMSUM_CHUNK = 256
NEG_BIG = -1e30
LOG2E = math.log2(math.e)
MM_TM, MM_TN, MM_TK, MM_TK_NT = 1024, 1664, 1024, 3328

_NT = (((1,), (1,)), ((), ()))
_NN = (((1,), (0,)), ((), ()))


def _cp(sem, vmem=VMEM_LIMIT):
    return pltpu.CompilerParams(dimension_semantics=sem, vmem_limit_bytes=vmem)


def _dot(a, b, dims=_NN):
    return lax.dot_general(a, b, dims, preferred_element_type=F32)


def _pick(n, cands):
    for c in cands:
        if c <= n and n % c == 0:
            return c
    return n


def _tile(n, cap):
    if n <= cap:
        return n
    best = None
    for d in range(LANE, cap + 1, LANE):
        if n % d == 0:
            best = d
    assert best is not None, (n, cap)
    return best


def _matmul(a, b, mode, name, also16=False):
    if mode == "nn":
        (M, K), (K2, N) = a.shape, b.shape
    else:
        (M, K), (N, K2) = a.shape, b.shape
    assert K == K2 and a.dtype == BF16 and b.dtype == BF16, (a.shape, b.shape, mode)
    tm, tn = _tile(M, MM_TM), _tile(N, MM_TN)
    tk = _tile(K, MM_TK if mode == "nn" else MM_TK_NT)
    nk = K // tk
    dims = _NN if mode == "nn" else _NT

    def body(a_ref, b_ref, *rest):
        o_ref, acc_ref = rest[0], rest[-1]
        part = _dot(a_ref[...], b_ref[...], dims)
        if nk == 1:
            o_ref[...] = part
            if also16:
                rest[1][...] = part.astype(BF16)
        else:
            assert not also16
            k = pl.program_id(2)

            @pl.when(k == 0)
            def _():
                acc_ref[...] = part

            @pl.when(k > 0)
            def _():
                acc_ref[...] += part

            @pl.when(k == nk - 1)
            def _():
                o_ref[...] = acc_ref[...]

    a_spec = pl.BlockSpec((tm, tk), lambda j, i, k: (i, k))
    if mode == "nn":
        b_spec = pl.BlockSpec((tk, tn), lambda j, i, k: (k, j))
    else:
        b_spec = pl.BlockSpec((tn, tk), lambda j, i, k: (j, k))
    acc_shape = (tm, tn) if nk > 1 else (8, LANE)
    o_spec = pl.BlockSpec((tm, tn), lambda j, i, k: (i, j))
    outs = pl.pallas_call(
        body, name=name, grid=(N // tn, M // tm, nk),
        in_specs=[a_spec, b_spec],
        out_specs=[o_spec, o_spec] if also16 else o_spec,
        out_shape=[jax.ShapeDtypeStruct((M, N), F32), jax.ShapeDtypeStruct((M, N), BF16)] if also16
        else jax.ShapeDtypeStruct((M, N), F32),
        scratch_shapes=[pltpu.VMEM(acc_shape, F32)],
        compiler_params=_cp(("parallel", "parallel", "arbitrary")),
    )(a, b)
    return outs


def _make_mm(name):
    @jax.custom_vjp
    def mm(a, w):
        return _matmul(a.astype(BF16), w.astype(BF16), "nn", name + "_fwd")

    def fwd(a, w):
        a16, w16 = a.astype(BF16), w.astype(BF16)
        return _matmul(a16, w16, "nn", name + "_fwd"), (a16, w16)

    def bwd(res, dy):
        a16, w16 = res
        dy16 = dy.astype(BF16)
        da = _matmul(dy16, w16, "nt", name + "_dx")
        dw = _matmul(a16.T, dy16, "nn", name + "_dw")
        return da, dw

    mm.defvjp(fwd, bwd)
    return mm


def _row_tile(rows):
    return _pick(rows, (512, 256, 128, 64, 32, 16, 8))


def _ln_stats(u):
    mu = jnp.mean(u, axis=-1, keepdims=True)
    d = u - mu
    var = jnp.mean(d * d, axis=-1, keepdims=True)
    return d, lax.rsqrt(var + LN_EPS)


def _ln_fwd_call(x, res, g, b, name, also16=False):
    rows, dm = x.shape
    tr = _row_tile(rows)
    has_res = res is not None
    n_in = 2 if has_res else 1

    def body(*refs):
        if has_res:
            u = DEEPNORM_ALPHA * refs[1][...] + refs[0][...]
        else:
            u = refs[0][...]
        g_ref, b_ref = refs[n_in], refs[n_in + 1]
        d, rstd = _ln_stats(u)
        y = d * rstd * g_ref[...] + b_ref[...]
        refs[n_in + 2][...] = y
        if also16:
            y16 = y.astype(BF16)
            refs[n_in + 3][...] = y16
            refs[n_in + 4][...] = y16.T

    row = pl.BlockSpec((tr, dm), lambda i: (i, 0))
    vec = pl.BlockSpec((1, dm), lambda i: (0, 0))
    args = (x, res) if has_res else (x,)
    out_specs, out_shape = [row], [jax.ShapeDtypeStruct((rows, dm), F32)]
    if also16:
        out_specs += [row, pl.BlockSpec((dm, tr), lambda i: (0, i))]
        out_shape += [jax.ShapeDtypeStruct((rows, dm), BF16), jax.ShapeDtypeStruct((dm, rows), BF16)]
    outs = pl.pallas_call(
        body, name=name, grid=(rows // tr,),
        in_specs=[row] * n_in + [vec, vec], out_specs=out_specs, out_shape=out_shape,
        compiler_params=_cp(("parallel",)),
    )(*args, g.reshape(1, dm), b.reshape(1, dm))
    return outs if also16 else outs[0]


def _ln_bwd_call(dy, x, res, g, name, dy2=None):
    rows, dm = x.shape
    tr = _row_tile(rows)
    has_res = res is not None
    two = dy2 is not None

    def body(*refs):
        dy_ref, refs = refs[0], refs[1:]
        if two:
            dy2_ref, refs = refs[0], refs[1:]
        if has_res:
            x_ref, r_ref, g_ref, dx_ref, dr_ref, dg_ref, db_ref = refs
            u = DEEPNORM_ALPHA * r_ref[...] + x_ref[...]
        else:
            x_ref, g_ref, dx_ref, dg_ref, db_ref = refs
            u = x_ref[...]
        i = pl.program_id(0)
        d, rstd = _ln_stats(u)
        xhat = d * rstd
        dyv = dy_ref[...] + dy2_ref[...] if two else dy_ref[...]
        dxh = dyv * g_ref[...]
        m1 = jnp.mean(dxh, axis=-1, keepdims=True)
        m2 = jnp.mean(dxh * xhat, axis=-1, keepdims=True)
        du = rstd * (dxh - m1 - xhat * m2)
        dx_ref[...] = du
        if has_res:
            dr_ref[...] = DEEPNORM_ALPHA * du
        pg = jnp.sum(dyv * xhat, axis=0, keepdims=True)
        pb = jnp.sum(dyv, axis=0, keepdims=True)

        @pl.when(i == 0)
        def _():
            dg_ref[...] = pg
            db_ref[...] = pb

        @pl.when(i > 0)
        def _():
            dg_ref[...] += pg
            db_ref[...] += pb

    row = pl.BlockSpec((tr, dm), lambda i: (i, 0))
    vec = pl.BlockSpec((1, dm), lambda i: (0, 0))
    big = jax.ShapeDtypeStruct((rows, dm), F32)
    small = jax.ShapeDtypeStruct((1, dm), F32)
    args = ((dy, dy2) if two else (dy,)) + ((x, res) if has_res else (x,))
    n_big = 2 if has_res else 1
    outs = pl.pallas_call(
        body, name=name, grid=(rows // tr,),
        in_specs=[row] * len(args) + [vec],
        out_specs=[row] * n_big + [vec, vec],
        out_shape=[big] * n_big + [small, small],
        compiler_params=_cp(("arbitrary",)),
    )(*args, g.reshape(1, dm))
    return outs


def _make_ln(name, has_res):
    if has_res:
        @jax.custom_vjp
        def ln(x, res, g, b):
            return _ln_fwd_call(x, res, g, b, name + "_fwd")

        def fwd(x, res, g, b):
            return ln(x, res, g, b), (x, res, g)

        def bwd(saved, dy):
            x, res, g = saved
            dx, dr, dg, db = _ln_bwd_call(dy, x, res, g, name + "_bwd")
            return dx, dr, dg.reshape(-1), db.reshape(-1)
    else:
        @jax.custom_vjp
        def ln(x, g, b):
            return _ln_fwd_call(x, None, g, b, name + "_fwd")

        def fwd(x, g, b):
            return ln(x, g, b), (x, g)

        def bwd(saved, dy):
            x, g = saved
            dx, dg, db = _ln_bwd_call(dy, x, None, g, name + "_bwd")
            return dx, dg.reshape(-1), db.reshape(-1)

    ln.defvjp(fwd, bwd)
    return ln


def _rms_fwd_call(x, g, name):
    rows, dm = x.shape
    tr = _row_tile(rows)

    def body(x_ref, g_ref, o_ref):
        xv = x_ref[...]
        rstd = lax.rsqrt(jnp.mean(xv * xv, axis=-1, keepdims=True) + RMS_EPS)
        o_ref[...] = xv * rstd * g_ref[...]

    row = pl.BlockSpec((tr, dm), lambda i: (i, 0))
    vec = pl.BlockSpec((1, dm), lambda i: (0, 0))
    return pl.pallas_call(
        body, name=name, grid=(rows // tr,), in_specs=[row, vec], out_specs=row,
        out_shape=jax.ShapeDtypeStruct((rows, dm), F32), compiler_params=_cp(("parallel",)),
    )(x, g.reshape(1, dm))


def _rms_bwd_call(dy, x, g, name):
    rows, dm = x.shape
    tr = _row_tile(rows)

    def body(dy_ref, x_ref, g_ref, dx_ref, dg_ref):
        i = pl.program_id(0)
        xv = x_ref[...]
        dyv = dy_ref[...]
        rstd = lax.rsqrt(jnp.mean(xv * xv, axis=-1, keepdims=True) + RMS_EPS)
        xhat = xv * rstd
        dxh = dyv * g_ref[...]
        m2 = jnp.mean(dxh * xhat, axis=-1, keepdims=True)
        dx_ref[...] = rstd * (dxh - xhat * m2)
        pg = jnp.sum(dyv * xhat, axis=0, keepdims=True)

        @pl.when(i == 0)
        def _():
            dg_ref[...] = pg

        @pl.when(i > 0)
        def _():
            dg_ref[...] += pg

    row = pl.BlockSpec((tr, dm), lambda i: (i, 0))
    vec = pl.BlockSpec((1, dm), lambda i: (0, 0))
    return pl.pallas_call(
        body, name=name, grid=(rows // tr,), in_specs=[row, row, vec], out_specs=[row, vec],
        out_shape=[jax.ShapeDtypeStruct((rows, dm), F32), jax.ShapeDtypeStruct((1, dm), F32)],
        compiler_params=_cp(("arbitrary",)),
    )(dy, x, g.reshape(1, dm))


def _make_rms(name):
    @jax.custom_vjp
    def rms(x, g):
        return _rms_fwd_call(x, g, name + "_fwd")

    def fwd(x, g):
        return rms(x, g), (x, g)

    def bwd(saved, dy):
        x, g = saved
        dx, dg = _rms_bwd_call(dy, x, g, name + "_bwd")
        return dx, dg.reshape(-1)

    rms.defvjp(fwd, bwd)
    return rms


def _sigmoid(x):
    return 1.0 / (1.0 + jnp.exp(-x))


def _gate_fwd_call(mixed, gate, name):
    rows, dm = mixed.shape
    tr = _row_tile(rows)

    def body(m_ref, g_ref, o_ref, oT_ref):
        gv = g_ref[...]
        y16 = (m_ref[...] * (gv * _sigmoid(gv))).astype(BF16)
        o_ref[...] = y16
        oT_ref[...] = y16.T

    row = pl.BlockSpec((tr, dm), lambda i: (i, 0))
    return pl.pallas_call(
        body, name=name, grid=(rows // tr,), in_specs=[row, row],
        out_specs=[row, pl.BlockSpec((dm, tr), lambda i: (0, i))],
        out_shape=[jax.ShapeDtypeStruct((rows, dm), BF16), jax.ShapeDtypeStruct((dm, rows), BF16)],
        compiler_params=_cp(("parallel",)),
    )(mixed, gate)


def _gate_bwd_call(dy, mixed, gate, name):
    rows, dm = mixed.shape
    tr = _row_tile(rows)

    def body(dy_ref, m_ref, g_ref, dm_ref, dg_ref):
        gv = g_ref[...]
        dyv = dy_ref[...]
        sg = _sigmoid(gv)
        dm_ref[...] = dyv * (gv * sg)
        dg_ref[...] = dyv * m_ref[...] * (sg * (1.0 + gv * (1.0 - sg)))

    row = pl.BlockSpec((tr, dm), lambda i: (i, 0))
    out = jax.ShapeDtypeStruct((rows, dm), F32)
    return pl.pallas_call(
        body, name=name, grid=(rows // tr,), in_specs=[row, row, row], out_specs=[row, row],
        out_shape=[out, out], compiler_params=_cp(("parallel",)),
    )(dy, mixed, gate)


def _make_gate_out(name):
    def run_fwd(mixed, gate, w):
        g16, gT16 = _gate_fwd_call(mixed, gate, name + "_gate_fwd")
        w16 = w.astype(BF16)
        return _matmul(g16, w16, "nn", name + "_fwd"), (mixed, gate, gT16, w16)

    def run_bwd(saved, dy):
        mixed, gate, gT16, w16 = saved
        dy16 = dy.astype(BF16)
        dgated = _matmul(dy16, w16, "nt", name + "_dx")
        dmix, dgate = _gate_bwd_call(dgated, mixed, gate, name + "_gate_bwd")
        return dmix, dgate, _matmul(gT16, dy16, "nn", name + "_dw")

    @jax.custom_vjp
    def gate_out(mixed, gate, w):
        return run_fwd(mixed, gate, w)[0]

    gate_out.defvjp(run_fwd, run_bwd)
    return gate_out


def _loss_call(y, t, name):
    rows, dm = y.shape
    tr = _row_tile(rows)

    def body(y_ref, t_ref, l_ref, d_ref):
        i = pl.program_id(0)
        e = y_ref[...] - t_ref[...]
        d_ref[...] = e * (1.0 / dm)
        part = 0.5 * jnp.sum(jnp.mean(e * e, axis=-1, keepdims=True), axis=0, keepdims=True)

        @pl.when(i == 0)
        def _():
            l_ref[...] = part

        @pl.when(i > 0)
        def _():
            l_ref[...] += part

    row = pl.BlockSpec((tr, dm), lambda i: (i, 0))
    one = pl.BlockSpec((1, 1), lambda i: (0, 0))
    return pl.pallas_call(
        body, name=name, grid=(rows // tr,), in_specs=[row, row], out_specs=[one, row],
        out_shape=[jax.ShapeDtypeStruct((1, 1), F32), jax.ShapeDtypeStruct((rows, dm), F32)],
        compiler_params=_cp(("arbitrary",)),
    )(y, t)


@jax.custom_vjp
def _loss_op(y, t):
    return _loss_call(y, t, "loss_head")[0][0, 0]


def _loss_fwd(y, t):
    l, d = _loss_call(y, t, "loss_head")
    return l[0, 0], d


def _loss_bwd(d, ct):
    return ct * d, jnp.zeros_like(d)


_loss_op.defvjp(_loss_fwd, _loss_bwd)


def _attn_blocks(S, Sk):
    bq, bk = min(ATTN_BQ, S), min(ATTN_BK, Sk)
    assert S % bq == 0 and Sk % bk == 0
    return bq, bk


def _valid_t(i, j, bq, bk, strict):
    key = j * bk + lax.broadcasted_iota(jnp.int32, (bk, bq), 0)
    qry = i * bq + lax.broadcasted_iota(jnp.int32, (bk, bq), 1)
    return (key < qry) if strict else (key <= qry)


def _sm_fwd_t(qn, k, vT, cmul, causal, name):
    H, S, DK = qn.shape
    Sk, dv = k.shape[1], vT.shape[1]
    bq, bk = _attn_blocks(S, Sk)
    nq, nkb = S // bq, Sk // bk
    hb = PAIR * FWD_PAIRS if H % (PAIR * FWD_PAIRS) == 0 else 1
    heads = range(hb)
    if causal:
        assert S == Sk and bq == bk

    def body(qn_ref, k_ref, vT_ref, oT_ref, lse_ref):
        i = pl.program_id(1)
        qTs = [qn_ref[w].T for w in heads]

        def blk(j, carry, masked):
            off = pl.multiple_of(j * bk, bk)
            sT = [_dot(k_ref[w, pl.ds(off, bk), :], qTs[w]) * cmul for w in heads]
            if masked:
                valid = _valid_t(i, j, bq, bk, False)
                sT = [jnp.where(valid, s, NEG_BIG) for s in sT]
            m_new = [jnp.maximum(carry[w][0], jnp.max(sT[w], axis=0, keepdims=True)) for w in heads]
            p = [jnp.exp2(sT[w] - m_new[w]) for w in heads]
            a = [jnp.exp2(carry[w][0] - m_new[w]) for w in heads]
            l = [a[w] * carry[w][1] + jnp.sum(p[w], axis=0, keepdims=True) for w in heads]
            acc = [a[w] * carry[w][2] + _dot(vT_ref[w, :, pl.ds(off, bk)], p[w].astype(BF16)) for w in heads]
            return tuple((m_new[w], l[w], acc[w]) for w in heads)

        carry = tuple((jnp.full((1, bq), NEG_BIG, F32), jnp.zeros((1, bq), F32), jnp.zeros((dv, bq), F32))
                      for _ in heads)
        if causal:
            carry = lax.fori_loop(0, i, lambda j, c: blk(j, c, False), carry)
            carry = blk(i, carry, True)
        else:
            carry = lax.fori_loop(0, nkb, lambda j, c: blk(j, c, False), carry)
        for w in heads:
            oT_ref[w] = carry[w][2] / carry[w][1]
            lse_ref[w] = carry[w][0] + jnp.log2(carry[w][1])

    qcol = lambda d: pl.BlockSpec((hb, d, bq), lambda h, i: (h, 0, i))
    return pl.pallas_call(
        body, name=name, grid=(H // hb, nq),
        in_specs=[pl.BlockSpec((hb, bq, DK), lambda h, i: (h, i, 0)), pl.BlockSpec((hb, Sk, DK), lambda h, i: (h, 0, 0)),
                  pl.BlockSpec((hb, dv, Sk), lambda h, i: (h, 0, 0))],
        out_specs=[qcol(dv), qcol(1)],
        out_shape=[jax.ShapeDtypeStruct((H, dv, S), F32), jax.ShapeDtypeStruct((H, 1, S), F32)],
        compiler_params=_cp(("parallel", "arbitrary"), ATTN_VMEM_LIMIT),
    )(qn, k, vT)


def _sm_bwd_t(qn, k, v, oT, lse, doT, do, cmul, gscale, causal, name):
    H, S, DK = qn.shape
    Sk, dv = k.shape[1], v.shape[2]
    bq, bk = _attn_blocks(S, Sk)
    nq, nkb = S // bq, Sk // bk

    def body(qn_ref, k_ref, v_ref, oT_ref, lse_ref, doT_ref, do_ref, dq_ref, dk_ref, dv_ref):
        i = pl.program_id(1)

        @pl.when(i == 0)
        def _():
            dk_ref[...] = jnp.zeros_like(dk_ref)
            dv_ref[...] = jnp.zeros_like(dv_ref)

        qnb = qn_ref[...]
        qTb = qnb.T
        dob = do_ref[...]
        doTf = doT_ref[...]
        doTb = doTf.astype(BF16)
        delta = jnp.sum(doTf * oT_ref[...], axis=0, keepdims=True)
        lse = lse_ref[...]

        def blk(j, dq, masked):
            off = pl.multiple_of(j * bk, bk)
            kb = k_ref[pl.ds(off, bk), :]
            sT = _dot(kb, qTb) * cmul
            if masked:
                sT = jnp.where(_valid_t(i, j, bq, bk, False), sT, NEG_BIG)
            p = jnp.exp2(sT - lse)
            dp = _dot(v_ref[pl.ds(off, bk), :], doTb)
            ds = p * (dp - delta)
            dsb = (ds * gscale).astype(BF16) if gscale != 1.0 else ds.astype(BF16)
            dv_ref[pl.ds(off, bk), :] += _dot(p.astype(BF16), dob)
            dk_ref[pl.ds(off, bk), :] += _dot(dsb, qnb)
            return dq + _dot(kb.T, dsb)

        dq = jnp.zeros((DK, bq), F32)
        if causal:
            dq = lax.fori_loop(0, i, lambda j, c: blk(j, c, False), dq)
            dq = blk(i, dq, True)
        else:
            dq = lax.fori_loop(0, nkb, lambda j, c: blk(j, c, False), dq)
        dq_ref[...] = dq.T

    qcol = lambda d: pl.BlockSpec((None, d, bq), lambda h, i: (h, 0, i))
    qrow = lambda d: pl.BlockSpec((None, bq, d), lambda h, i: (h, i, 0))
    krow = lambda d: pl.BlockSpec((None, Sk, d), lambda h, i: (h, 0, 0))
    return pl.pallas_call(
        body, name=name, grid=(H, nq),
        in_specs=[qrow(DK), krow(DK), krow(dv), qcol(dv), qcol(1), qcol(dv), qrow(dv)],
        out_specs=[qrow(DK), krow(DK), krow(dv)],
        out_shape=[jax.ShapeDtypeStruct((H, S, DK), F32), jax.ShapeDtypeStruct((H, Sk, DK), F32),
                   jax.ShapeDtypeStruct((H, Sk, dv), F32)],
        compiler_params=_cp(("parallel", "arbitrary"), ATTN_VMEM_LIMIT),
    )(qn, k, v, oT, lse, doT, do)


def _tri(n, fn):
    r = lax.broadcasted_iota(jnp.int32, (n, n), 0)
    c = lax.broadcasted_iota(jnp.int32, (n, n), 1)
    return jnp.where(fn(r, c), 1.0, 0.0).astype(BF16)


def _key_cumsum(x, tri2, suffix, base):
    bk = x.shape[0]
    c = min(CUMSUM_CHUNK, bk)
    n = bk // c
    hi32 = lax.bitcast_convert_type(lax.bitcast_convert_type(x, jnp.int32) & jnp.int32(-65536), F32)
    hi = hi32.astype(BF16)
    lo = (x - hi32).astype(BF16)
    tot = [jnp.sum(x[a * c:(a + 1) * c], axis=0, keepdims=True) for a in range(n)]
    outs = []
    for a in range(n):
        row = base
        for t in (tot[a + 1:] if suffix else tot[:a]):
            row = row + t
        stacked = jnp.concatenate([hi[a * c:(a + 1) * c], lo[a * c:(a + 1) * c]], axis=0)
        outs.append(_dot(tri2, stacked) + row)
    total = tot[0]
    for t in tot[1:]:
        total = total + t
    return (outs[0] if n == 1 else jnp.concatenate(outs, axis=0)), total


def _tri2(n, fn):
    t = _tri(n, fn)
    return jnp.concatenate([t, t], axis=1)


def _sb_logs(z):
    neg_abs = lax.bitcast_convert_type(lax.bitcast_convert_type(z, jnp.int32) | jnp.int32(-2 ** 31), F32)
    ls = jnp.minimum(z, 0.0) - jnp.log(1.0 + jnp.exp(neg_abs))
    return ls, ls - z


PAIR = LANE // HEAD_DIM
FWD_PAIRS = 2
SB_DEAD = -110.0
FOX_DEAD = -160.0


def _head_lanes(shape, w, axis):
    idx = lax.broadcasted_iota(jnp.int32, shape, axis)
    return (idx >= HEAD_DIM * w) & (idx < HEAD_DIM * (w + 1))


def _bias_rows(w, bq):
    row = lax.broadcasted_iota(jnp.int32, (LANE, bq), 0)
    return jnp.where((row >= 3 * w) & (row < 3 * w + 3), -1.0, 0.0).astype(BF16)


def _merge_pair(parts):
    return jnp.where(_head_lanes(parts[0].shape, 0, 0), parts[0], parts[1]).T


def _smp_fwd(q2, k2, v2, bias, r, causal, name, kstat=None):
    S, C = q2.shape
    Sk = k2.shape[0]
    bq, bk = _attn_blocks(S, Sk)
    nq, nkb, P = S // bq, Sk // bk, C // LANE
    gp = FWD_PAIRS if P % FWD_PAIRS == 0 else 1
    use_f = bias is not None
    if causal:
        assert S == Sk and bq == bk and use_f

    def body(*refs):
        if use_f:
            ks_ref, q_ref, k_ref, v_ref, b_ref, r_ref, o_ref, lse_ref, js_ref = refs
        else:
            q_ref, k_ref, v_ref, o_ref, lse_ref = refs
        i = pl.program_id(1)
        heads = range(PAIR * gp)
        lanes = [slice(LANE * (h // PAIR), LANE * (h // PAIR + 1)) for h in heads]
        qps = [q_ref[:, lanes[h]] for h in heads]
        qTs = [jnp.where(_head_lanes(qps[h].shape, h % PAIR, 1), qps[h], jnp.zeros_like(qps[h])).T for h in heads]
        if use_f:
            qf = [t.astype(F32) for t in qTs]
            qnorm = [jnp.sqrt(jnp.sum(t * t, axis=0, keepdims=True)) for t in qf]
            qTs = [jnp.concatenate([qTs[h], _bias_rows(h % PAIR, bq)], axis=0) for h in heads]

        def blk(j, carry, masked):
            off = pl.multiple_of(j * bk, bk)
            kbs = [k_ref[pl.ds(off, bk), LANE * g:LANE * (g + 1)] for g in range(gp)]
            if use_f:
                kbs = [jnp.concatenate([kbs[g], b_ref[pl.ds(off, bk), LANE * g:LANE * (g + 1)]], axis=1)
                       for g in range(gp)]
            vTbs = [v_ref[pl.ds(off, bk), LANE * g:LANE * (g + 1)].T for g in range(gp)]
            sT = [_dot(kbs[h // PAIR], qTs[h]) * LOG2E for h in heads]
            if masked:
                valid = _valid_t(i, j, bq, bk, False)
                sT = [jnp.where(valid, s, NEG_BIG) for s in sT]
            cm = [jnp.max(s, axis=0, keepdims=True) for s in sT]
            if use_f:
                cm = [cm[h] + r_ref[h] for h in heads]
            m_new = [jnp.maximum(carry[h][0], cm[h]) for h in heads]
            shift = [(m_new[h] - r_ref[h]) if use_f else m_new[h] for h in heads]
            p = [jnp.exp2(sT[h] - shift[h]) for h in heads]
            a = [jnp.exp2(carry[h][0] - m_new[h]) for h in heads]
            l = [a[h] * carry[h][1] + jnp.sum(p[h], axis=0, keepdims=True) for h in heads]
            acc = [a[h] * carry[h][2] + _dot(vTbs[h // PAIR], p[h].astype(BF16)) for h in heads]
            return tuple((m_new[h], l[h], acc[h]) for h in heads)

        def step(jj, state):
            carry, first = state
            j = i - jj
            h0 = pl.program_id(0) * (PAIR * gp)
            bound = [LOG2E * (qnorm[h] * ks_ref[(h0 + h) * nkb + j] - ks_ref[(PAIR * P + h0 + h) * nkb + j])
                     + r_ref[h] - carry[h][0] for h in heads]
            live = jnp.max(functools.reduce(jnp.maximum, bound)) >= FOX_DEAD
            carry = lax.cond(live, lambda cr: blk(j, cr, False), lambda cr: cr, carry)
            return carry, jnp.where(live, j, first)

        carry = tuple((jnp.full((1, bq), NEG_BIG, F32), jnp.zeros((1, bq), F32), jnp.zeros((LANE, bq), F32))
                      for _ in heads)
        if causal:
            carry = blk(i, carry, True)
            carry, first = lax.fori_loop(1, i + 1, step, (carry, i))
            js_ref[0] = jnp.full((1, bq), first, jnp.int32)
        else:
            carry = lax.fori_loop(0, nkb, lambda j, c: blk(j, c, False), carry)
            if use_f:
                js_ref[0] = jnp.zeros((1, bq), jnp.int32)
        for h in heads:
            lse_ref[h] = carry[h][0] + jnp.log2(carry[h][1])
        for g in range(gp):
            o_ref[:, LANE * g:LANE * (g + 1)] = _merge_pair(
                [carry[h][2] / carry[h][1] for h in range(PAIR * g, PAIR * (g + 1))])

    qblk = pl.BlockSpec((bq, LANE * gp), lambda p, i: (i, p))
    kres = pl.BlockSpec((Sk, LANE * gp), lambda p, i: (0, p))
    stat = pl.BlockSpec((PAIR * gp, 1, bq), lambda p, i: (p, 0, i))
    in_specs = [qblk, kres, kres]
    args = [q2, k2, v2]
    out_specs = [qblk, stat]
    out_shape = [jax.ShapeDtypeStruct((S, C), F32), jax.ShapeDtypeStruct((PAIR * P, 1, S), F32)]
    if use_f:
        in_specs = [pl.BlockSpec(memory_space=pltpu.SMEM)] + in_specs + [kres, stat]
        args = [kstat] + args + [bias, r]
        out_specs.append(pl.BlockSpec((1, 1, bq), lambda p, i: (p, 0, i)))
        out_shape.append(jax.ShapeDtypeStruct((P // gp, 1, S), jnp.int32))
    return pl.pallas_call(
        body, name=name, grid=(P // gp, nq), in_specs=in_specs, out_specs=out_specs, out_shape=out_shape,
        compiler_params=_cp(("parallel", "arbitrary"), ATTN_VMEM_LIMIT),
    )(*args)


def _smp_bwd(q2, k2, v2, o2, lse, do2, bias, r, scale, causal, name, first=None):
    S, C = q2.shape
    Sk = k2.shape[0]
    bq, bk = _attn_blocks(S, Sk)
    nq, nkb, P = S // bq, Sk // bk, C // LANE
    use_f = bias is not None

    def body(*refs):
        if use_f:
            (first_ref, q_ref, k_ref, v_ref, o_ref, lse_ref, do_ref, b_ref, r_ref,
             dq_ref, dk_ref, dv_ref, dr_ref, dkey_ref, dk_acc, dv_acc, db_ref) = refs
        else:
            q_ref, k_ref, v_ref, o_ref, lse_ref, do_ref, dq_ref, dk_ref, dv_ref, dk_acc, dv_acc = refs
        i = pl.program_id(1)

        @pl.when(i == 0)
        def _():
            dk_acc[...] = jnp.zeros_like(dk_acc)
            dv_acc[...] = jnp.zeros_like(dv_acc)
            if use_f:
                db_ref[...] = jnp.zeros_like(db_ref)

        qp = q_ref[...]
        dof = do_ref[...]
        prod = dof * o_ref[...]
        heads = range(PAIR)
        mine = [_head_lanes(qp.shape, w, 1) for w in heads]
        qz = [jnp.where(mine[w], qp, jnp.zeros_like(qp)) for w in heads]
        qTs = [qz[w].T for w in heads]
        if use_f:
            qTs = [jnp.concatenate([qTs[w], _bias_rows(w, bq)], axis=0) for w in heads]
        doz = [jnp.where(mine[w], dof, 0.0).astype(BF16) for w in heads]
        doT = [doz[w].T for w in heads]
        delta = [jnp.sum(jnp.where(mine[w], prod, 0.0).T, axis=0, keepdims=True) for w in heads]
        shift = [(lse_ref[w] - r_ref[w]) if use_f else lse_ref[w] for w in heads]

        def blk(j, carry, masked):
            off = pl.multiple_of(j * bk, bk)
            kb = k_ref[pl.ds(off, bk), :]
            kTb = kb.T
            if use_f:
                kb = jnp.concatenate([kb, b_ref[pl.ds(off, bk), :]], axis=1)
            vb = v_ref[pl.ds(off, bk), :]
            sT = [_dot(kb, qTs[w]) * LOG2E for w in heads]
            if masked:
                valid = _valid_t(i, j, bq, bk, False)
                sT = [jnp.where(valid, s, NEG_BIG) for s in sT]
            p = [jnp.exp2(sT[w] - shift[w]) for w in heads]
            dp = [_dot(vb, doT[w]) for w in heads]
            ds = [p[w] * (dp[w] - delta[w]) for w in heads]
            dsb = [d.astype(BF16) for d in ds]
            dvs = [_dot(p[w].astype(BF16), doz[w]) for w in heads]
            dks = [_dot(dsb[w], qz[w]) for w in heads]
            dv_acc[pl.ds(off, bk), :] += dvs[0] + dvs[1]
            dk_acc[pl.ds(off, bk), :] += dks[0] + dks[1]
            dr = [carry[w][1] for w in heads]
            if use_f:
                dr = [dr[w] + jnp.sum(ds[w], axis=0, keepdims=True) for w in heads]
                lane = lax.broadcasted_iota(jnp.int32, (bk, LANE), 1)
                cols = [jnp.where(lane == w, jnp.sum(ds[w], axis=1, keepdims=True), 0.0) for w in heads]
                db_ref[pl.ds(off, bk), :] += cols[0] + cols[1]
            dq = [carry[w][0] + _dot(kTb, dsb[w]) for w in heads]
            return tuple((dq[w], dr[w]) for w in heads)

        carry = tuple((jnp.zeros((LANE, bq), F32), jnp.zeros((1, bq), F32)) for _ in heads)
        if causal:
            start = first_ref[pl.program_id(0) // (P // first.shape[0]), i]
            carry = lax.fori_loop(start, i, lambda j, c: blk(j, c, False), carry)
            carry = blk(i, carry, True)
        else:
            carry = lax.fori_loop(0, nkb, lambda j, c: blk(j, c, False), carry)
        if use_f:
            for w in heads:
                dr_ref[w] = carry[w][1]
        dq_ref[...] = (_merge_pair([carry[w][0] for w in heads]) * scale).astype(BF16)

        @pl.when(i == nq - 1)
        def _():
            dk_ref[...] = dk_acc[...].astype(BF16)
            dv_ref[...] = dv_acc[...].astype(BF16)

        if use_f:
            @pl.when(i == nq - 1)
            def _():
                def chunk(cidx, carry):
                    off = pl.multiple_of(cidx * LANE, LANE)
                    t = db_ref[pl.ds(off, LANE), :].T
                    for w in range(PAIR):
                        dkey_ref[w, :, pl.ds(off, LANE)] = t[w:w + 1, :]
                    return carry

                lax.fori_loop(0, Sk // LANE, chunk, 0)

    qblk = pl.BlockSpec((bq, LANE), lambda p, i: (i, p))
    kres = pl.BlockSpec((Sk, LANE), lambda p, i: (0, p))
    stat = pl.BlockSpec((PAIR, 1, bq), lambda p, i: (p, 0, i))
    in_specs = [qblk, kres, kres, qblk, stat, qblk]
    args = [q2, k2, v2, o2, lse, do2]
    out_specs = [qblk, kres, kres]
    out_shape = [jax.ShapeDtypeStruct((S, C), BF16), jax.ShapeDtypeStruct((Sk, C), BF16),
                 jax.ShapeDtypeStruct((Sk, C), BF16)]
    scratch = [pltpu.VMEM((Sk, LANE), F32), pltpu.VMEM((Sk, LANE), F32)]
    if use_f:
        in_specs = [pl.BlockSpec(memory_space=pltpu.SMEM)] + in_specs + [kres, stat]
        args = [first] + args + [bias, r]
        out_specs += [stat, pl.BlockSpec((PAIR, 1, Sk), lambda p, i: (p, 0, 0))]
        out_shape += [jax.ShapeDtypeStruct((PAIR * P, 1, S), F32), jax.ShapeDtypeStruct((PAIR * P, 1, Sk), F32)]
        scratch.append(pltpu.VMEM((Sk, LANE), F32))
    return pl.pallas_call(
        body, name=name, grid=(P, nq), in_specs=in_specs, out_specs=out_specs, out_shape=out_shape,
        scratch_shapes=scratch, compiler_params=_cp(("parallel", "arbitrary"), ATTN_VMEM_LIMIT),
    )(*args)


def _sbp_fwd(q2, k2, v2, name):
    S, C = q2.shape
    bq, bk = _attn_blocks(S, S)
    assert bq == bk
    nq, P = S // bq, C // LANE
    gp = FWD_PAIRS if P % FWD_PAIRS == 0 else 1
    c = min(CUMSUM_CHUNK, bk)

    def body(q_ref, k_ref, v_ref, o_ref, lt_ref, js_ref):
        i = pl.program_id(1)
        after = _tri2(c, lambda s, j: j > s)
        heads = range(PAIR * gp)
        qps = [q_ref[:, LANE * (h // PAIR):LANE * (h // PAIR + 1)] for h in heads]
        qTs = [jnp.where(_head_lanes(qps[h].shape, h % PAIR, 1), qps[h], jnp.zeros_like(qps[h])).T for h in heads]

        def blk(jj, carry, masked):
            j = i - jj
            off = pl.multiple_of(j * bk, bk)
            kbs = [k_ref[pl.ds(off, bk), LANE * g:LANE * (g + 1)] for g in range(gp)]
            vTbs = [v_ref[pl.ds(off, bk), LANE * g:LANE * (g + 1)].T for g in range(gp)]
            logs = [_sb_logs(_dot(kbs[h // PAIR], qTs[h])) for h in heads]
            ls, lk = [t[0] for t in logs], [t[1] for t in logs]
            if masked:
                valid = _valid_t(i, j, bq, bk, True)
                lk = [jnp.where(valid, t, 0.0) for t in lk]
            cs = [_key_cumsum(lk[h], after, True, carry[h][0]) for h in heads]
            wgt = [jnp.exp(ls[h] + cs[h][0]) for h in heads]
            if masked:
                wgt = [jnp.where(valid, t, 0.0) for t in wgt]
            acc = [carry[h][1] + _dot(vTbs[h // PAIR], wgt[h].astype(BF16)) for h in heads]
            return tuple((carry[h][0] + cs[h][1], acc[h]) for h in heads)

        def step(jj, state):
            carry, first = state
            live = jnp.max(functools.reduce(jnp.maximum, [carry[h][0] for h in heads])) >= SB_DEAD
            carry = lax.cond(live, lambda cr: blk(jj, cr, False), lambda cr: cr, carry)
            return carry, jnp.where(live, i - jj, first)

        carry = tuple((jnp.zeros((1, bq), F32), jnp.zeros((LANE, bq), F32)) for _ in heads)
        carry = blk(0, carry, True)
        carry, first = lax.fori_loop(1, i + 1, step, (carry, i))
        js_ref[0] = jnp.full((1, bq), first, jnp.int32)
        for h in heads:
            lt_ref[h] = carry[h][0]
        for g in range(gp):
            o_ref[:, LANE * g:LANE * (g + 1)] = _merge_pair([carry[h][1] for h in range(PAIR * g, PAIR * (g + 1))])

    qblk = pl.BlockSpec((bq, LANE * gp), lambda p, i: (i, p))
    kres = pl.BlockSpec((S, LANE * gp), lambda p, i: (0, p))
    stat = pl.BlockSpec((PAIR * gp, 1, bq), lambda p, i: (p, 0, i))
    return pl.pallas_call(
        body, name=name, grid=(P // gp, nq),
        in_specs=[qblk, kres, kres],
        out_specs=[qblk, stat, pl.BlockSpec((1, 1, bq), lambda p, i: (p, 0, i))],
        out_shape=[jax.ShapeDtypeStruct((S, C), F32), jax.ShapeDtypeStruct((PAIR * P, 1, S), F32),
                   jax.ShapeDtypeStruct((P // gp, 1, S), jnp.int32)],
        compiler_params=_cp(("parallel", "arbitrary"), ATTN_VMEM_LIMIT),
    )(q2, k2, v2)


def _sbp_bwd(q2, k2, v2, lt, first, do2, scale, name):
    S, C = q2.shape
    bq, bk = _attn_blocks(S, S)
    nq, P = S // bq, C // LANE
    c = min(CUMSUM_CHUNK, bk)

    per_group = P // first.shape[0]

    def body(first_ref, q_ref, k_ref, v_ref, lt_ref, do_ref, dq_ref, dk_ref, dv_ref, dk_acc, dv_acc):
        i = pl.program_id(1)

        @pl.when(i == 0)
        def _():
            dk_acc[...] = jnp.zeros_like(dk_acc)
            dv_acc[...] = jnp.zeros_like(dv_acc)

        qp = q_ref[...]
        dof = do_ref[...]
        upto = _tri2(c, lambda s, j: j <= s)
        before = _tri2(c, lambda s, j: j < s)
        heads = range(PAIR)
        mine = [_head_lanes(qp.shape, w, 1) for w in heads]
        qz = [jnp.where(mine[w], qp, jnp.zeros_like(qp)) for w in heads]
        qTs = [qz[w].T for w in heads]
        doz = [jnp.where(mine[w], dof, 0.0).astype(BF16) for w in heads]
        doT = [doz[w].T for w in heads]
        ltot = [lt_ref[w] for w in heads]

        def blk(j, carry, masked):
            off = pl.multiple_of(j * bk, bk)
            kb = k_ref[pl.ds(off, bk), :]
            vb = v_ref[pl.ds(off, bk), :]
            kTb = kb.T
            logs = [_sb_logs(_dot(kb, qTs[w])) for w in heads]
            ls, lk = [t[0] for t in logs], [t[1] for t in logs]
            if masked:
                valid = _valid_t(i, j, bq, bk, True)
                lk = [jnp.where(valid, t, 0.0) for t in lk]
            pin = [_key_cumsum(lk[w], upto, False, carry[w][1] - ltot[w]) for w in heads]
            wgt = [jnp.exp(ls[w] - pin[w][0]) for w in heads]
            if masked:
                wgt = [jnp.where(valid, t, 0.0) for t in wgt]
            g = [_dot(vb, doT[w]) * wgt[w] for w in heads]
            cin = [_key_cumsum(g[w], before, False, carry[w][2]) for w in heads]
            sig = [jnp.exp(t) for t in ls]
            dz = [g[w] * (1.0 - sig[w]) - cin[w][0] * sig[w] for w in heads]
            if masked:
                dz = [jnp.where(valid, t, 0.0) for t in dz]
            dzb = [t.astype(BF16) for t in dz]
            dvs = [_dot(wgt[w].astype(BF16), doz[w]) for w in heads]
            dks = [_dot(dzb[w], qz[w]) for w in heads]
            dv_acc[pl.ds(off, bk), :] += dvs[0] + dvs[1]
            dk_acc[pl.ds(off, bk), :] += dks[0] + dks[1]
            return tuple((carry[w][0] + _dot(kTb, dzb[w]), carry[w][1] + pin[w][1], carry[w][2] + cin[w][1])
                         for w in heads)

        carry = tuple((jnp.zeros((LANE, bq), F32), jnp.zeros((1, bq), F32), jnp.zeros((1, bq), F32)) for _ in heads)
        start = first_ref[pl.program_id(0) // per_group, i]
        carry = lax.fori_loop(start, i, lambda j, cr: blk(j, cr, False), carry)
        carry = blk(i, carry, True)
        dq_ref[...] = (_merge_pair([carry[w][0] for w in heads]) * scale).astype(BF16)

        @pl.when(i == nq - 1)
        def _():
            dk_ref[...] = dk_acc[...].astype(BF16)
            dv_ref[...] = dv_acc[...].astype(BF16)

    qblk = pl.BlockSpec((bq, LANE), lambda p, i: (i, p))
    kres = pl.BlockSpec((S, LANE), lambda p, i: (0, p))
    stat = pl.BlockSpec((PAIR, 1, bq), lambda p, i: (p, 0, i))
    return pl.pallas_call(
        body, name=name, grid=(P, nq),
        in_specs=[pl.BlockSpec(memory_space=pltpu.SMEM), qblk, kres, kres, stat, qblk],
        out_specs=[qblk, kres, kres],
        out_shape=[jax.ShapeDtypeStruct((S, C), BF16)] * 3,
        scratch_shapes=[pltpu.VMEM((S, LANE), F32), pltpu.VMEM((S, LANE), F32)],
        compiler_params=_cp(("parallel", "arbitrary"), ATTN_VMEM_LIMIT),
    )(first, q2, k2, v2, lt, do2)


def _bias_cols(f_cum):
    H, Sk = f_cum.shape
    terms = jnp.stack(_split3(f_cum), axis=-1)
    packed = terms.reshape(H // PAIR, PAIR, Sk, 3).transpose(2, 0, 1, 3).reshape(Sk, H // PAIR, PAIR * 3)
    return jnp.pad(packed, ((0, 0), (0, 0), (0, LANE - PAIR * 3))).reshape(Sk, -1)


def _make_packed_softmax(name, scale, causal, use_f):
    assert _pow2(scale)

    def run_fwd(q16, k16, v16, f_cum):
        q16 = q16 * scale
        if not use_f:
            o, lse = _smp_fwd(q16, k16, v16, None, None, causal, name + "_fwd")
            return o, (q16, k16, v16, o, lse, None, None, None)
        bq, bk = _attn_blocks(q16.shape[0], k16.shape[0])
        n_heads = f_cum.shape[0]
        knorm = jnp.sqrt(jnp.sum(jnp.square(k16.astype(F32)).reshape(-1, bk, n_heads, HEAD_DIM), axis=3))
        kstat = jnp.concatenate([jnp.max(knorm, axis=1).T.reshape(-1), f_cum[:, bk - 1::bk].reshape(-1)])
        bias, r = _bias_cols(f_cum), (f_cum * LOG2E)[:, None, :]
        o, lse, first = _smp_fwd(q16, k16, v16, bias, r, causal, name + "_fwd", lax.stop_gradient(kstat))
        return o, (q16, k16, v16, o, lse, bias, r, first[:, 0, ::bq])

    def run_bwd(saved, do):
        q16, k16, v16, o, lse, bias, r, first = saved
        outs = _smp_bwd(q16, k16, v16, o, lse, do, bias, r, scale, causal, name + "_bwd", first)
        if use_f:
            return outs[0], outs[1], outs[2], outs[3][:, 0, :] - outs[4][:, 0, :]
        return tuple(outs)

    if use_f:
        @jax.custom_vjp
        def attn(q, k, v, f_cum):
            return run_fwd(q, k, v, f_cum)[0]

        attn.defvjp(run_fwd, run_bwd)
    else:
        @jax.custom_vjp
        def attn(q, k, v):
            return run_fwd(q, k, v, None)[0]

        attn.defvjp(lambda q, k, v: run_fwd(q, k, v, None), run_bwd)
    return attn


def _make_packed_sb(name, scale):
    assert _pow2(scale)

    def run_fwd(q16, k16, v16):
        q16 = q16 * scale
        o, lt, first = _sbp_fwd(q16, k16, v16, name + "_fwd")
        bq, _ = _attn_blocks(q16.shape[0], q16.shape[0])
        return o, (q16, k16, v16, lt, first[:, 0, ::bq])

    def run_bwd(saved, do):
        q16, k16, v16, lt, first = saved
        return tuple(_sbp_bwd(q16, k16, v16, lt, first, do, scale, name + "_bwd"))

    @jax.custom_vjp
    def attn(q, k, v):
        return run_fwd(q, k, v)[0]

    attn.defvjp(run_fwd, run_bwd)
    return attn


def _round_bf16(x):
    return lax.reduce_precision(x, exponent_bits=8, mantissa_bits=7)


def _split3(x):
    hi = _round_bf16(x)
    mid = _round_bf16(x - hi)
    lo = _round_bf16(x - hi - mid)
    return hi.astype(BF16), mid.astype(BF16), lo.astype(BF16)


def _pow2(x):
    m, _ = math.frexp(x)
    return m == 0.5


def _pad_last(x, n):
    return jnp.pad(x, [(0, 0)] * (x.ndim - 1) + [(0, n - x.shape[-1])])


def _layouts(q, k, scale):
    qh = _pad_last(jnp.transpose(q * scale if _pow2(scale) else q, (1, 0, 2)).astype(BF16), LANE)
    return qh, _pad_last(jnp.transpose(k, (1, 0, 2)).astype(BF16), LANE)


def _make_softmax_attn(name, scale, causal, d):
    pre = _pow2(scale)
    cmul = LOG2E if pre else scale * LOG2E
    gscale = 1.0 if pre else scale

    def run_fwd(q, k, v):
        qn, kn = _layouts(q, k, scale)
        vn = jnp.transpose(v, (1, 0, 2)).astype(BF16)
        oT, lse = _sm_fwd_t(qn, kn, jnp.transpose(vn, (0, 2, 1)), cmul, causal, name + "_fwd")
        return jnp.transpose(oT, (2, 0, 1)), (qn, kn, vn, oT, lse)

    def run_bwd(saved, dout):
        qn, kn, vn, oT, lse = saved
        doT = jnp.transpose(dout, (1, 2, 0))
        do = jnp.transpose(dout, (1, 0, 2)).astype(BF16)
        dq, dk, dv = _sm_bwd_t(qn, kn, vn, oT, lse, doT, do, cmul, gscale, causal, name + "_bwd")
        dq = jnp.transpose(dq[:, :, :d], (1, 0, 2))
        if pre:
            dq = dq * scale
        return dq, jnp.transpose(dk[:, :, :d], (1, 0, 2)), jnp.transpose(dv, (1, 0, 2))

    @jax.custom_vjp
    def attn(q, k, v):
        return run_fwd(q, k, v)[0]

    attn.defvjp(run_fwd, run_bwd)
    return attn


def _rope(x, positions):
    half = x.shape[-1] // 2
    inv_freq = ROPE_THETA ** (-jnp.arange(half, dtype=F32) / half)
    ang = positions.astype(F32)[:, None] * inv_freq[None, :]
    ang = ang.reshape((ang.shape[0],) + (1,) * (x.ndim - 2) + (half,))
    cos, sin = jnp.cos(ang), jnp.sin(ang)
    x1, x2 = x[..., :half], x[..., half:]
    return jnp.concatenate([x1 * cos - x2 * sin, x1 * sin + x2 * cos], axis=-1)


def _permute_cols(w):
    parts = [w[..., _ORIG_OFF[idx]:_ORIG_OFF[idx] + SPLIT_SIZES[idx]] for _, idx in _PERM]
    pad = jnp.zeros(w.shape[:-1] + (PROJ_COLS - IN_COLS,), w.dtype)
    return jnp.concatenate(parts + [pad], axis=-1)


def _unpermute_cols(w):
    start, parts = 0, [None] * len(SPLIT_SIZES)
    for _, idx in _PERM:
        parts[idx] = w[..., start:start + SPLIT_SIZES[idx]]
        start += SPLIT_SIZES[idx]
    return jnp.concatenate(parts, axis=-1)


_BF16_PIECES = ("fq", "fk", "fv", "sq", "sk", "sv", "mq")


def _make_ln_proj(name, has_res):
    def split(proj32, proj16):
        out, off = [], 0
        for n, idx in _PERM:
            src = proj16 if n in _BF16_PIECES else proj32
            out.append(src[:, off:off + SPLIT_SIZES[idx]])
            off += SPLIT_SIZES[idx]
        return tuple(out)

    def run_fwd(x, res, g, b, w):
        h, h16, hT16 = _ln_fwd_call(x, res, g, b, name + "_ln_fwd", also16=True)
        w16 = w.astype(BF16)
        proj32, proj16 = _matmul(h16, w16, "nn", name + "_fwd", also16=True)
        return (h, split(proj32, proj16)), (x, res, g, hT16, w16)

    def run_bwd(saved, cts):
        x, res, g, hT16, w16 = saved
        dh, dpieces = cts
        pad = jnp.zeros((x.shape[0], PROJ_COLS - IN_COLS), BF16)
        dy16 = jnp.concatenate([c.astype(BF16) for c in dpieces] + [pad], axis=1)
        da = _matmul(dy16, w16, "nt", name + "_dx")
        dw = _matmul(hT16, dy16, "nn", name + "_dw")
        outs = _ln_bwd_call(dh, x, res, g, name + "_ln_bwd", dy2=da)
        if has_res:
            dx, dr, dg, db = outs
            return dx, dr, dg.reshape(-1), db.reshape(-1), dw
        dx, dg, db = outs
        return dx, dg.reshape(-1), db.reshape(-1), dw

    if has_res:
        @jax.custom_vjp
        def op(x, res, g, b, w):
            return run_fwd(x, res, g, b, w)[0]

        op.defvjp(run_fwd, run_bwd)
    else:
        @jax.custom_vjp
        def op(x, g, b, w):
            return run_fwd(x, None, g, b, w)[0]

        op.defvjp(lambda x, g, b, w: run_fwd(x, None, g, b, w), run_bwd)

    def call(*args):
        h, pieces = op(*args)
        return h, {n: part for (n, _), part in zip(_PERM, pieces)}

    return call


def _trunk_loss(wts, x2d, mem2d, target2d):
    s = x2d.shape[0]
    positions = jnp.arange(s)
    head_scale = HEAD_DIM ** -0.5
    mla_scale = (MLA_NOPE + MLA_ROPE) ** -0.5

    mem_n = _make_ln("ln_mem", False)(mem2d, wts["mem_ln_g"], wts["mem_ln_b"])
    h, y = None, x2d
    for l in range(DEPTH):
        tag = f"l{l}_"
        w_p = wts["w_in"][l]
        if l == 0:
            h, p = _make_ln_proj(tag + "proj", False)(y, wts["ln_in_g"], wts["ln_in_b"], w_p)
        else:
            h, p = _make_ln_proj(tag + "proj", True)(y, h, wts["ln_g"][l - 1], wts["ln_b"][l - 1], w_p)

        log_f = jax.nn.log_sigmoid(p["f_logit"] + wts["b_forget"][l])
        f_cum = jnp.cumsum(log_f, axis=0).T
        out_fox = _make_packed_softmax(tag + "fox", head_scale, True, True)(p["fq"], p["fk"], p["fv"], f_cum)

        out_sb = _make_packed_sb(tag + "sb", head_scale)(p["sq"], p["sk"], p["sv"])

        cqn = _make_rms(tag + "rms_q")(p["c_q"], wts["mla_q_norm_g"][l])
        q_mla = _make_mm(tag + "q_up")(cqn, wts["w_mla_q_up"][l]).reshape(s, N_HEADS, MLA_NOPE + MLA_ROPE)
        ckvn = _make_rms(tag + "rms_kv")(p["c_kv"], wts["mla_kv_norm_g"][l])
        kv_mla = _make_mm(tag + "kv_up")(ckvn, wts["w_mla_kv_up"][l]).reshape(s, N_HEADS, MLA_NOPE + MLA_V)
        q_full = jnp.concatenate([q_mla[..., :MLA_NOPE], _rope(q_mla[..., MLA_NOPE:], positions)], axis=-1)
        k_rope = jnp.broadcast_to(_rope(p["k_rot"], positions)[:, None, :], (s, N_HEADS, MLA_ROPE))
        k_full = jnp.concatenate([kv_mla[..., :MLA_NOPE], k_rope], axis=-1)
        out_mla = _make_softmax_attn(tag + "mla", mla_scale, True, MLA_NOPE + MLA_ROPE)(
            q_full, k_full, kv_mla[..., MLA_NOPE:]).reshape(s, GROUP_W)

        mkv = _make_mm(tag + "mem_kv")(mem_n, wts["w_mem_kv"][l])
        out_mem = _make_packed_softmax(tag + "mem", head_scale, False, False)(
            p["mq"], mkv[:, :GROUP_W].astype(BF16), mkv[:, GROUP_W:].astype(BF16))

        mixed = jnp.concatenate([out_fox, out_sb, out_mla, out_mem], axis=-1)
        y = _make_gate_out(tag + "out")(mixed, p["gate"], wts["w_out"][l])

    h = _make_ln(f"l{DEPTH - 1}_ln", True)(y, h, wts["ln_g"][DEPTH - 1], wts["ln_b"][DEPTH - 1])
    return _loss_op(h, target2d)


def _mesh_pos():
    x, y, c = (lax.axis_index(a) for a in MESH_AXES)
    return x, y, c, 4 * x + 2 * y + c


def _peer(x, y, c, mask):
    return (x ^ ((mask >> 2) & 1), y ^ ((mask >> 1) & 1), c ^ (mask & 1))


_ANY = pl.BlockSpec(memory_space=pl.ANY)


def _all_gather(row_shards, stack_shards):
    n_row, n_all = len(row_shards), len(row_shards) + len(stack_shards)
    shards = list(row_shards) + list(stack_shards)

    def body(*refs):
        ins, outs = refs[:n_all], refs[n_all:2 * n_all]
        send_sems, recv_sems, local_sems = refs[2 * n_all:]
        x, y, c, me = _mesh_pos()

        def window(t, slot):
            if t < n_row:
                rows = shards[t].shape[1]
                return outs[t].at[:, pl.ds(slot * rows, rows), :]
            return outs[t].at[slot]

        local = [pltpu.make_async_copy(ins[t], window(t, me), local_sems.at[t]) for t in range(n_all)]
        for cp in local:
            cp.start()
        sends = []
        for mask in range(1, N_DEV):
            for t in range(n_all):
                cp = pltpu.make_async_remote_copy(
                    src_ref=ins[t], dst_ref=window(t, me), send_sem=send_sems.at[t, mask - 1],
                    recv_sem=recv_sems.at[t, mask - 1], device_id=_peer(x, y, c, mask),
                    device_id_type=pl.DeviceIdType.MESH)
                cp.start()
                sends.append(cp)
        for mask in range(1, N_DEV):
            for t in range(n_all):
                pltpu.make_async_remote_copy(
                    src_ref=ins[t], dst_ref=window(t, me ^ mask), send_sem=send_sems.at[t, mask - 1],
                    recv_sem=recv_sems.at[t, mask - 1], device_id=_peer(x, y, c, mask),
                    device_id_type=pl.DeviceIdType.MESH).wait_recv()
        for cp in sends:
            cp.wait_send()
        for cp in local:
            cp.wait()

    out_shape = [jax.ShapeDtypeStruct((a.shape[0], N_DEV * a.shape[1], a.shape[2]), a.dtype) for a in row_shards]
    out_shape += [jax.ShapeDtypeStruct((N_DEV,) + a.shape, a.dtype) for a in stack_shards]
    return pl.pallas_call(
        body, name="all_gather_weights", in_specs=[_ANY] * n_all, out_specs=[_ANY] * n_all, out_shape=out_shape,
        scratch_shapes=[pltpu.SemaphoreType.DMA((n_all, N_DEV - 1)), pltpu.SemaphoreType.DMA((n_all, N_DEV - 1)),
                        pltpu.SemaphoreType.DMA((n_all,))],
    )(*shards)


def _reduce_scatter(row_full, stack_full, bcast):
    n_row, n_stack = len(row_full), len(stack_full)
    n_all = n_row + n_stack + len(bcast)
    fulls = list(row_full) + list(stack_full) + list(bcast)

    def body(*refs):
        ins, outs = refs[:n_all], refs[n_all:2 * n_all]
        send_sems, recv_sems, local_sems = refs[2 * n_all:]
        x, y, c, me = _mesh_pos()

        def part(t, slot):
            if t < n_row:
                rows = fulls[t].shape[1] // N_DEV
                return ins[t].at[:, pl.ds(slot * rows, rows), :]
            if t < n_row + n_stack:
                return ins[t].at[slot]
            return ins[t]

        local = [pltpu.make_async_copy(part(t, me), outs[t].at[me], local_sems.at[t]) for t in range(n_all)]
        for cp in local:
            cp.start()
        sends = []
        for mask in range(1, N_DEV):
            for t in range(n_all):
                cp = pltpu.make_async_remote_copy(
                    src_ref=part(t, me ^ mask), dst_ref=outs[t].at[me], send_sem=send_sems.at[t, mask - 1],
                    recv_sem=recv_sems.at[t, mask - 1], device_id=_peer(x, y, c, mask),
                    device_id_type=pl.DeviceIdType.MESH)
                cp.start()
                sends.append(cp)
        for mask in range(1, N_DEV):
            for t in range(n_all):
                pltpu.make_async_remote_copy(
                    src_ref=part(t, me), dst_ref=outs[t].at[me ^ mask], send_sem=send_sems.at[t, mask - 1],
                    recv_sem=recv_sems.at[t, mask - 1], device_id=_peer(x, y, c, mask),
                    device_id_type=pl.DeviceIdType.MESH).wait_recv()
        for cp in sends:
            cp.wait_send()
        for cp in local:
            cp.wait()

    out_shape = [jax.ShapeDtypeStruct((N_DEV, a.shape[0], a.shape[1] // N_DEV, a.shape[2]), a.dtype) for a in row_full]
    out_shape += [jax.ShapeDtypeStruct(a.shape, a.dtype) for a in stack_full]
    out_shape += [jax.ShapeDtypeStruct((N_DEV,) + a.shape, a.dtype) for a in bcast]
    return pl.pallas_call(
        body, name="reduce_scatter_grads", in_specs=[_ANY] * n_all, out_specs=[_ANY] * n_all, out_shape=out_shape,
        scratch_shapes=[pltpu.SemaphoreType.DMA((n_all, N_DEV - 1)), pltpu.SemaphoreType.DMA((n_all, N_DEV - 1)),
                        pltpu.SemaphoreType.DMA((n_all,))],
    )(*fulls)


def _adamw(slots, w, m, v, name):
    shape = w.shape
    cols = shape[-1]
    rows = math.prod(shape[:-1])
    tr = _pick(rows, (64, 32, 16, 8))
    c1 = 1.0 - ADAM_B1 ** ADAM_STEP
    c2 = 1.0 - ADAM_B2 ** ADAM_STEP

    def body(s_ref, w_ref, m_ref, v_ref, g_ref, d_ref, nm_ref, nv_ref):
        g = s_ref[0].astype(F32)
        for k in range(1, N_DEV):
            g = g + s_ref[k].astype(F32)
        nm = ADAM_B1 * m_ref[...] + (1.0 - ADAM_B1) * g
        nv = ADAM_B2 * v_ref[...] + (1.0 - ADAM_B2) * (g * g)
        g_ref[...] = g
        nm_ref[...] = nm
        nv_ref[...] = nv
        d_ref[...] = -ADAM_LR * ((nm / c1) / (jnp.sqrt(nv / c2) + ADAM_EPS) + ADAM_WD * w_ref[...])

    row = pl.BlockSpec((tr, cols), lambda i: (i, 0))
    out = jax.ShapeDtypeStruct((rows, cols), F32)
    outs = pl.pallas_call(
        body, name=name, grid=(rows // tr,),
        in_specs=[pl.BlockSpec((N_DEV, tr, cols), lambda i: (0, i, 0)), row, row, row],
        out_specs=[row] * 4, out_shape=[out] * 4, compiler_params=_cp(("parallel",)),
    )(slots.reshape(N_DEV, rows, cols), w.reshape(rows, cols), m.reshape(rows, cols), v.reshape(rows, cols))
    return [o.reshape(shape) for o in outs]


_SMALL = ("ln_in_g", "ln_in_b", "mem_ln_g", "mem_ln_b", "b_forget", "mla_q_norm_g", "mla_kv_norm_g", "ln_g", "ln_b")
_ORDER = ("ln_in_g", "ln_in_b", "mem_ln_g", "mem_ln_b", "w_in", "b_forget", "mla_q_norm_g", "w_mla_q_up",
          "mla_kv_norm_g", "w_mla_kv_up", "w_mem_kv", "w_out", "ln_g", "ln_b")


def _pack_small(d):
    flat = jnp.concatenate([d[n].reshape(-1) for n in _SMALL])
    n = flat.shape[0]
    padded = ((n + 8 * LANE - 1) // (8 * LANE)) * (8 * LANE)
    return jnp.pad(flat, (0, padded - n)).reshape(-1, LANE)


def _unpack_small(packed, like):
    flat, out, off = packed.reshape(-1), {}, 0
    for n in _SMALL:
        size = math.prod(like[n].shape)
        out[n] = flat[off:off + size].reshape(like[n].shape)
        off += size
    return out


def _unstack_cols(g):
    n, l, r, c = g.shape
    return g.transpose(1, 2, 0, 3).reshape(l, r, n * c)


def _stack_cols(g):
    l, r, nc = g.shape
    return g.reshape(l, r, N_DEV, nc // N_DEV).transpose(2, 0, 1, 3)


def kernel(x, mem, ln_in_g, ln_in_b, mem_ln_g, mem_ln_b, w_in, b_forget, mla_q_norm_g, w_mla_q_up, mla_kv_norm_g, w_mla_kv_up, w_mem_kv, w_out, ln_g, ln_b, loss_target, m_ln_in_g, m_ln_in_b, m_mem_ln_g, m_mem_ln_b, m_w_in, m_b_forget, m_mla_q_norm_g, m_w_mla_q_up, m_mla_kv_norm_g, m_w_mla_kv_up, m_w_mem_kv, m_w_out, m_ln_g, m_ln_b, v_ln_in_g, v_ln_in_b, v_mem_ln_g, v_mem_ln_b, v_w_in, v_b_forget, v_mla_q_norm_g, v_w_mla_q_up, v_mla_kv_norm_g, v_w_mla_kv_up, v_w_mem_kv, v_w_out, v_ln_g, v_ln_b):
    w_shard = dict(ln_in_g=ln_in_g, ln_in_b=ln_in_b, mem_ln_g=mem_ln_g, mem_ln_b=mem_ln_b, w_in=w_in,
                   b_forget=b_forget, mla_q_norm_g=mla_q_norm_g, w_mla_q_up=w_mla_q_up,
                   mla_kv_norm_g=mla_kv_norm_g, w_mla_kv_up=w_mla_kv_up, w_mem_kv=w_mem_kv, w_out=w_out,
                   ln_g=ln_g, ln_b=ln_b)
    m_shard = dict(ln_in_g=m_ln_in_g, ln_in_b=m_ln_in_b, mem_ln_g=m_mem_ln_g, mem_ln_b=m_mem_ln_b, w_in=m_w_in,
                   b_forget=m_b_forget, mla_q_norm_g=m_mla_q_norm_g, w_mla_q_up=m_w_mla_q_up,
                   mla_kv_norm_g=m_mla_kv_norm_g, w_mla_kv_up=m_w_mla_kv_up, w_mem_kv=m_w_mem_kv, w_out=m_w_out,
                   ln_g=m_ln_g, ln_b=m_ln_b)
    v_shard = dict(ln_in_g=v_ln_in_g, ln_in_b=v_ln_in_b, mem_ln_g=v_mem_ln_g, mem_ln_b=v_mem_ln_b, w_in=v_w_in,
                   b_forget=v_b_forget, mla_q_norm_g=v_mla_q_norm_g, w_mla_q_up=v_w_mla_q_up,
                   mla_kv_norm_g=v_mla_kv_norm_g, w_mla_kv_up=v_w_mla_kv_up, w_mem_kv=v_w_mem_kv, w_out=v_w_out,
                   ln_g=v_ln_g, ln_b=v_ln_b)

    to16 = lambda ws: [a.astype(BF16) for a in ws]
    gathered = _all_gather(to16([_permute_cols(w_in), w_mem_kv, w_out]), to16([w_mla_q_up, w_mla_kv_up]))
    g_in, g_mem, g_out, g_qup, g_kvup = [a.astype(F32) for a in gathered]
    full = dict(w_shard)
    full.update(w_in=g_in, w_mem_kv=g_mem, w_out=g_out, w_mla_q_up=_unstack_cols(g_qup),
                w_mla_kv_up=_unstack_cols(g_kvup))

    loss_local, (grad_w, grad_x) = jax.value_and_grad(_trunk_loss, argnums=(0, 1))(
        full, x[0], mem[0], loss_target[0])

    s_in, s_mem, s_out, s_qup, s_kvup, s_small = _reduce_scatter(
        to16([grad_w["w_in"], grad_w["w_mem_kv"], grad_w["w_out"]]),
        to16([_stack_cols(grad_w["w_mla_q_up"]), _stack_cols(grad_w["w_mla_kv_up"])]),
        [_pack_small(grad_w)])

    res = {}
    for name, slots in (("w_mem_kv", s_mem), ("w_out", s_out), ("w_mla_q_up", s_qup), ("w_mla_kv_up", s_kvup)):
        res[name] = _adamw(slots, w_shard[name], m_shard[name], v_shard[name], "adamw_" + name)
    res["w_in"] = [_unpermute_cols(a) for a in _adamw(
        s_in, _permute_cols(w_in), _permute_cols(m_w_in), _permute_cols(v_w_in), "adamw_w_in")]
    small = _adamw(s_small, _pack_small(w_shard), _pack_small(m_shard), _pack_small(v_shard), "adamw_small")
    small = [_unpack_small(a, w_shard) for a in small]
    for name in _SMALL:
        res[name] = [a[name] for a in small]

    loss = lax.psum(loss_local, MESH_AXES)
    outs = [loss, grad_x[None]]
    for k in range(4):
        outs += [res[name][k] for name in _ORDER]
    return tuple(outs)
```

```python
import functools
import math

import jax
import jax.numpy as jnp
from jax import lax
from jax.experimental import pallas as pl
from jax.experimental.pallas import tpu as pltpu

F32 = jnp.float32
BF16 = jnp.bfloat16

D_MODEL = 1024
DEPTH = 2
GROUP_W = 256
N_HEADS = 4
HEAD_DIM = 64
MLA_Q_RANK = 256
MLA_KV_RANK = 128
MLA_NOPE = 64
MLA_ROPE = 32
MLA_V = 64
ROPE_THETA = 10000.0
LN_EPS = 1e-5
RMS_EPS = 1e-6
DEEPNORM_ALPHA = (2 * DEPTH) ** 0.25
SPLIT_SIZES = (256, 256, 256, 4, 256, 256, 256, 256, 128, 32, 256, 1024)
IN_COLS = sum(SPLIT_SIZES)
_ORIG_OFF = [sum(SPLIT_SIZES[:i]) for i in range(len(SPLIT_SIZES))]
_PERM = (("fq", 0), ("fk", 1), ("fv", 2), ("sq", 4), ("sk", 5), ("sv", 6), ("c_q", 7), ("c_kv", 8),
         ("mq", 10), ("gate", 11), ("k_rot", 9), ("f_logit", 3))
LANE = 128
PROJ_COLS = ((IN_COLS + LANE - 1) // LANE) * LANE

ADAM_LR = 0.001
ADAM_B1 = 0.9
ADAM_B2 = 0.999
ADAM_EPS = 1e-08
ADAM_WD = 0.01
ADAM_STEP = 10

N_DEV = 8
MESH_AXES = ("x", "y", "c")
VMEM_LIMIT = 48 * 1024 * 1024
ATTN_VMEM_LIMIT = 56 * 1024 * 1024
ATTN_BQ = 512
ATTN_BK = 512
CUMSUM_CHUNK = 256
NEG_BIG = -1e30
LOG2E = math.log2(math.e)
MM_TM, MM_TN, MM_TK, MM_TK_NT = 1024, 1664, 1024, 3328

_NT = (((1,), (1,)), ((), ()))
_NN = (((1,), (0,)), ((), ()))


def _cp(sem, vmem=VMEM_LIMIT):
    return pltpu.CompilerParams(dimension_semantics=sem, vmem_limit_bytes=vmem)


def _dot(a, b, dims=_NN):
    return lax.dot_general(a, b, dims, preferred_element_type=F32)


def _pick(n, cands):
    for c in cands:
        if c <= n and n % c == 0:
            return c
    return n


def _tile(n, cap):
    if n <= cap:
        return n
    best = None
    for d in range(LANE, cap + 1, LANE):
        if n % d == 0:
            best = d
    assert best is not None, (n, cap)
    return best


def _matmul(a, b, mode, name, also16=False):
    if mode == "nn":
        (M, K), (K2, N) = a.shape, b.shape
    else:
        (M, K), (N, K2) = a.shape, b.shape
    assert K == K2 and a.dtype == BF16 and b.dtype == BF16, (a.shape, b.shape, mode)
    tm, tn = _tile(M, MM_TM), _tile(N, MM_TN)
    tk = _tile(K, MM_TK if mode == "nn" else MM_TK_NT)
    nk = K // tk
    dims = _NN if mode == "nn" else _NT

    def body(a_ref, b_ref, *rest):
        o_ref, acc_ref = rest[0], rest[-1]
        part = _dot(a_ref[...], b_ref[...], dims)
        if nk == 1:
            o_ref[...] = part
            if also16:
                rest[1][...] = part.astype(BF16)
        else:
            assert not also16
            k = pl.program_id(2)

            @pl.when(k == 0)
            def _():
                acc_ref[...] = part

            @pl.when(k > 0)
            def _():
                acc_ref[...] += part

            @pl.when(k == nk - 1)
            def _():
                o_ref[...] = acc_ref[...]

    a_spec = pl.BlockSpec((tm, tk), lambda j, i, k: (i, k))
    if mode == "nn":
        b_spec = pl.BlockSpec((tk, tn), lambda j, i, k: (k, j))
    else:
        b_spec = pl.BlockSpec((tn, tk), lambda j, i, k: (j, k))
    acc_shape = (tm, tn) if nk > 1 else (8, LANE)
    o_spec = pl.BlockSpec((tm, tn), lambda j, i, k: (i, j))
    outs = pl.pallas_call(
        body, name=name, grid=(N // tn, M // tm, nk),
        in_specs=[a_spec, b_spec],
        out_specs=[o_spec, o_spec] if also16 else o_spec,
        out_shape=[jax.ShapeDtypeStruct((M, N), F32), jax.ShapeDtypeStruct((M, N), BF16)] if also16
        else jax.ShapeDtypeStruct((M, N), F32),
        scratch_shapes=[pltpu.VMEM(acc_shape, F32)],
        compiler_params=_cp(("parallel", "parallel", "arbitrary")),
    )(a, b)
    return outs


def _make_mm(name):
    @jax.custom_vjp
    def mm(a, w):
        return _matmul(a.astype(BF16), w.astype(BF16), "nn", name + "_fwd")

    def fwd(a, w):
        a16, w16 = a.astype(BF16), w.astype(BF16)
        return _matmul(a16, w16, "nn", name + "_fwd"), (a16, w16)

    def bwd(res, dy):
        a16, w16 = res
        dy16 = dy.astype(BF16)
        da = _matmul(dy16, w16, "nt", name + "_dx")
        dw = _matmul(a16.T, dy16, "nn", name + "_dw")
        return da, dw

    mm.defvjp(fwd, bwd)
    return mm


def _row_tile(rows):
    return _pick(rows, (512, 256, 128, 64, 32, 16, 8))


def _ln_stats(u):
    mu = jnp.mean(u, axis=-1, keepdims=True)
    d = u - mu
    var = jnp.mean(d * d, axis=-1, keepdims=True)
    return d, lax.rsqrt(var + LN_EPS)


def _ln_fwd_call(x, res, g, b, name, also16=False):
    rows, dm = x.shape
    tr = _row_tile(rows)
    has_res = res is not None
    n_in = 2 if has_res else 1

    def body(*refs):
        if has_res:
            u = DEEPNORM_ALPHA * refs[1][...] + refs[0][...]
        else:
            u = refs[0][...]
        g_ref, b_ref = refs[n_in], refs[n_in + 1]
        d, rstd = _ln_stats(u)
        y = d * rstd * g_ref[...] + b_ref[...]
        refs[n_in + 2][...] = y
        if also16:
            y16 = y.astype(BF16)
            refs[n_in + 3][...] = y16
            refs[n_in + 4][...] = y16.T

    row = pl.BlockSpec((tr, dm), lambda i: (i, 0))
    vec = pl.BlockSpec((1, dm), lambda i: (0, 0))
    args = (x, res) if has_res else (x,)
    out_specs, out_shape = [row], [jax.ShapeDtypeStruct((rows, dm), F32)]
    if also16:
        out_specs += [row, pl.BlockSpec((dm, tr), lambda i: (0, i))]
        out_shape += [jax.ShapeDtypeStruct((rows, dm), BF16), jax.ShapeDtypeStruct((dm, rows), BF16)]
    outs = pl.pallas_call(
        body, name=name, grid=(rows // tr,),
        in_specs=[row] * n_in + [vec, vec], out_specs=out_specs, out_shape=out_shape,
        compiler_params=_cp(("parallel",)),
    )(*args, g.reshape(1, dm), b.reshape(1, dm))
    return outs if also16 else outs[0]


def _ln_bwd_call(dy, x, res, g, name, dy2=None):
    rows, dm = x.shape
    tr = _row_tile(rows)
    has_res = res is not None
    two = dy2 is not None

    def body(*refs):
        dy_ref, refs = refs[0], refs[1:]
        if two:
            dy2_ref, refs = refs[0], refs[1:]
        if has_res:
            x_ref, r_ref, g_ref, dx_ref, dr_ref, dg_ref, db_ref = refs
            u = DEEPNORM_ALPHA * r_ref[...] + x_ref[...]
        else:
            x_ref, g_ref, dx_ref, dg_ref, db_ref = refs
            u = x_ref[...]
        i = pl.program_id(0)
        d, rstd = _ln_stats(u)
        xhat = d * rstd
        dyv = dy_ref[...] + dy2_ref[...] if two else dy_ref[...]
        dxh = dyv * g_ref[...]
        m1 = jnp.mean(dxh, axis=-1, keepdims=True)
        m2 = jnp.mean(dxh * xhat, axis=-1, keepdims=True)
        du = rstd * (dxh - m1 - xhat * m2)
        dx_ref[...] = du
        if has_res:
            dr_ref[...] = DEEPNORM_ALPHA * du
        pg = jnp.sum(dyv * xhat, axis=0, keepdims=True)
        pb = jnp.sum(dyv, axis=0, keepdims=True)

        @pl.when(i == 0)
        def _():
            dg_ref[...] = pg
            db_ref[...] = pb

        @pl.when(i > 0)
        def _():
            dg_ref[...] += pg
            db_ref[...] += pb

    row = pl.BlockSpec((tr, dm), lambda i: (i, 0))
    vec = pl.BlockSpec((1, dm), lambda i: (0, 0))
    big = jax.ShapeDtypeStruct((rows, dm), F32)
    small = jax.ShapeDtypeStruct((1, dm), F32)
    args = ((dy, dy2) if two else (dy,)) + ((x, res) if has_res else (x,))
    n_big = 2 if has_res else 1
    outs = pl.pallas_call(
        body, name=name, grid=(rows // tr,),
        in_specs=[row] * len(args) + [vec],
        out_specs=[row] * n_big + [vec, vec],
        out_shape=[big] * n_big + [small, small],
        compiler_params=_cp(("arbitrary",)),
    )(*args, g.reshape(1, dm))
    return outs


def _make_ln(name, has_res):
    if has_res:
        @jax.custom_vjp
        def ln(x, res, g, b):
            return _ln_fwd_call(x, res, g, b, name + "_fwd")

        def fwd(x, res, g, b):
            return ln(x, res, g, b), (x, res, g)

        def bwd(saved, dy):
            x, res, g = saved
            dx, dr, dg, db = _ln_bwd_call(dy, x, res, g, name + "_bwd")
            return dx, dr, dg.reshape(-1), db.reshape(-1)
    else:
        @jax.custom_vjp
        def ln(x, g, b):
            return _ln_fwd_call(x, None, g, b, name + "_fwd")

        def fwd(x, g, b):
            return ln(x, g, b), (x, g)

        def bwd(saved, dy):
            x, g = saved
            dx, dg, db = _ln_bwd_call(dy, x, None, g, name + "_bwd")
            return dx, dg.reshape(-1), db.reshape(-1)

    ln.defvjp(fwd, bwd)
    return ln


def _rms_fwd_call(x, g, name):
    rows, dm = x.shape
    tr = _row_tile(rows)

    def body(x_ref, g_ref, o_ref):
        xv = x_ref[...]
        rstd = lax.rsqrt(jnp.mean(xv * xv, axis=-1, keepdims=True) + RMS_EPS)
        o_ref[...] = xv * rstd * g_ref[...]

    row = pl.BlockSpec((tr, dm), lambda i: (i, 0))
    vec = pl.BlockSpec((1, dm), lambda i: (0, 0))
    return pl.pallas_call(
        body, name=name, grid=(rows // tr,), in_specs=[row, vec], out_specs=row,
        out_shape=jax.ShapeDtypeStruct((rows, dm), F32), compiler_params=_cp(("parallel",)),
    )(x, g.reshape(1, dm))


def _rms_bwd_call(dy, x, g, name):
    rows, dm = x.shape
    tr = _row_tile(rows)

    def body(dy_ref, x_ref, g_ref, dx_ref, dg_ref):
        i = pl.program_id(0)
        xv = x_ref[...]
        dyv = dy_ref[...]
        rstd = lax.rsqrt(jnp.mean(xv * xv, axis=-1, keepdims=True) + RMS_EPS)
        xhat = xv * rstd
        dxh = dyv * g_ref[...]
        m2 = jnp.mean(dxh * xhat, axis=-1, keepdims=True)
        dx_ref[...] = rstd * (dxh - xhat * m2)
        pg = jnp.sum(dyv * xhat, axis=0, keepdims=True)

        @pl.when(i == 0)
        def _():
            dg_ref[...] = pg

        @pl.when(i > 0)
        def _():
            dg_ref[...] += pg

    row = pl.BlockSpec((tr, dm), lambda i: (i, 0))
    vec = pl.BlockSpec((1, dm), lambda i: (0, 0))
    return pl.pallas_call(
        body, name=name, grid=(rows // tr,), in_specs=[row, row, vec], out_specs=[row, vec],
        out_shape=[jax.ShapeDtypeStruct((rows, dm), F32), jax.ShapeDtypeStruct((1, dm), F32)],
        compiler_params=_cp(("arbitrary",)),
    )(dy, x, g.reshape(1, dm))


def _make_rms(name):
    @jax.custom_vjp
    def rms(x, g):
        return _rms_fwd_call(x, g, name + "_fwd")

    def fwd(x, g):
        return rms(x, g), (x, g)

    def bwd(saved, dy):
        x, g = saved
        dx, dg = _rms_bwd_call(dy, x, g, name + "_bwd")
        return dx, dg.reshape(-1)

    rms.defvjp(fwd, bwd)
    return rms


def _sigmoid(x):
    return 1.0 / (1.0 + jnp.exp(-x))


def _gate_fwd_call(mixed, gate, name):
    rows, dm = mixed.shape
    tr = _row_tile(rows)

    def body(m_ref, g_ref, o_ref, oT_ref):
        gv = g_ref[...]
        y16 = (m_ref[...] * (gv * _sigmoid(gv))).astype(BF16)
        o_ref[...] = y16
        oT_ref[...] = y16.T

    row = pl.BlockSpec((tr, dm), lambda i: (i, 0))
    return pl.pallas_call(
        body, name=name, grid=(rows // tr,), in_specs=[row, row],
        out_specs=[row, pl.BlockSpec((dm, tr), lambda i: (0, i))],
        out_shape=[jax.ShapeDtypeStruct((rows, dm), BF16), jax.ShapeDtypeStruct((dm, rows), BF16)],
        compiler_params=_cp(("parallel",)),
    )(mixed, gate)


def _gate_bwd_call(dy, mixed, gate, name):
    rows, dm = mixed.shape
    tr = _row_tile(rows)

    def body(dy_ref, m_ref, g_ref, dm_ref, dg_ref):
        gv = g_ref[...]
        dyv = dy_ref[...]
        sg = _sigmoid(gv)
        dm_ref[...] = dyv * (gv * sg)
        dg_ref[...] = dyv * m_ref[...] * (sg * (1.0 + gv * (1.0 - sg)))

    row = pl.BlockSpec((tr, dm), lambda i: (i, 0))
    out = jax.ShapeDtypeStruct((rows, dm), F32)
    return pl.pallas_call(
        body, name=name, grid=(rows // tr,), in_specs=[row, row, row], out_specs=[row, row],
        out_shape=[out, out], compiler_params=_cp(("parallel",)),
    )(dy, mixed, gate)


def _make_gate_out(name):
    def run_fwd(mixed, gate, w):
        g16, gT16 = _gate_fwd_call(mixed, gate, name + "_gate_fwd")
        w16 = w.astype(BF16)
        return _matmul(g16, w16, "nn", name + "_fwd"), (mixed, gate, gT16, w16)

    def run_bwd(saved, dy):
        mixed, gate, gT16, w16 = saved
        dy16 = dy.astype(BF16)
        dgated = _matmul(dy16, w16, "nt", name + "_dx")
        dmix, dgate = _gate_bwd_call(dgated, mixed, gate, name + "_gate_bwd")
        return dmix, dgate, _matmul(gT16, dy16, "nn", name + "_dw")

    @jax.custom_vjp
    def gate_out(mixed, gate, w):
        return run_fwd(mixed, gate, w)[0]

    gate_out.defvjp(run_fwd, run_bwd)
    return gate_out


def _loss_call(y, t, name):
    rows, dm = y.shape
    tr = _row_tile(rows)

    def body(y_ref, t_ref, l_ref, d_ref):
        i = pl.program_id(0)
        e = y_ref[...] - t_ref[...]
        d_ref[...] = e * (1.0 / dm)
        part = 0.5 * jnp.sum(jnp.mean(e * e, axis=-1, keepdims=True), axis=0, keepdims=True)

        @pl.when(i == 0)
        def _():
            l_ref[...] = part

        @pl.when(i > 0)
        def _():
            l_ref[...] += part

    row = pl.BlockSpec((tr, dm), lambda i: (i, 0))
    one = pl.BlockSpec((1, 1), lambda i: (0, 0))
    return pl.pallas_call(
        body, name=name, grid=(rows // tr,), in_specs=[row, row], out_specs=[one, row],
        out_shape=[jax.ShapeDtypeStruct((1, 1), F32), jax.ShapeDtypeStruct((rows, dm), F32)],
        compiler_params=_cp(("arbitrary",)),
    )(y, t)


@jax.custom_vjp
def _loss_op(y, t):
    return _loss_call(y, t, "loss_head")[0][0, 0]


def _loss_fwd(y, t):
    l, d = _loss_call(y, t, "loss_head")
    return l[0, 0], d


def _loss_bwd(d, ct):
    return ct * d, jnp.zeros_like(d)


_loss_op.defvjp(_loss_fwd, _loss_bwd)


def _attn_blocks(S, Sk, cap=None):
    bq, bk = min(cap or ATTN_BQ, S), min(cap or ATTN_BK, Sk)
    assert S % bq == 0 and Sk % bk == 0
    return bq, bk


def _valid_t(i, j, bq, bk, strict):
    key = j * bk + lax.broadcasted_iota(jnp.int32, (bk, bq), 0)
    qry = i * bq + lax.broadcasted_iota(jnp.int32, (bk, bq), 1)
    return (key < qry) if strict else (key <= qry)


def _sm_fwd_t(qn, k, vT, cmul, causal, name):
    H, S, DK = qn.shape
    Sk, dv = k.shape[1], vT.shape[1]
    bq, bk = _attn_blocks(S, Sk)
    nq, nkb = S // bq, Sk // bk
    hb = PAIR * FWD_PAIRS if H % (PAIR * FWD_PAIRS) == 0 else 1
    heads = range(hb)
    if causal:
        assert S == Sk and bq == bk

    def body(qn_ref, k_ref, vT_ref, oT_ref, lse_ref):
        i = pl.program_id(1)
        qTs = [qn_ref[w].T for w in heads]

        def blk(j, carry, masked):
            off = pl.multiple_of(j * bk, bk)
            sT = [_dot(k_ref[w, pl.ds(off, bk), :], qTs[w]) * cmul for w in heads]
            if masked:
                valid = _valid_t(i, j, bq, bk, False)
                sT = [jnp.where(valid, s, NEG_BIG) for s in sT]
            m_new = [jnp.maximum(carry[w][0], jnp.max(sT[w], axis=0, keepdims=True)) for w in heads]
            p = [jnp.exp2(sT[w] - m_new[w]) for w in heads]
            a = [jnp.exp2(carry[w][0] - m_new[w]) for w in heads]
            l = [a[w] * carry[w][1] + jnp.sum(p[w], axis=0, keepdims=True) for w in heads]
            acc = [a[w] * carry[w][2] + _dot(vT_ref[w, :, pl.ds(off, bk)], p[w].astype(BF16)) for w in heads]
            return tuple((m_new[w], l[w], acc[w]) for w in heads)

        carry = tuple((jnp.full((1, bq), NEG_BIG, F32), jnp.zeros((1, bq), F32), jnp.zeros((dv, bq), F32))
                      for _ in heads)
        if causal:
            carry = lax.fori_loop(0, i, lambda j, c: blk(j, c, False), carry)
            carry = blk(i, carry, True)
        else:
            carry = lax.fori_loop(0, nkb, lambda j, c: blk(j, c, False), carry)
        for w in heads:
            oT_ref[w] = carry[w][2] / carry[w][1]
            lse_ref[w] = carry[w][0] + jnp.log2(carry[w][1])

    qcol = lambda d: pl.BlockSpec((hb, d, bq), lambda h, i: (h, 0, i))
    return pl.pallas_call(
        body, name=name, grid=(H // hb, nq),
        in_specs=[pl.BlockSpec((hb, bq, DK), lambda h, i: (h, i, 0)), pl.BlockSpec((hb, Sk, DK), lambda h, i: (h, 0, 0)),
                  pl.BlockSpec((hb, dv, Sk), lambda h, i: (h, 0, 0))],
        out_specs=[qcol(dv), qcol(1)],
        out_shape=[jax.ShapeDtypeStruct((H, dv, S), F32), jax.ShapeDtypeStruct((H, 1, S), F32)],
        compiler_params=_cp(("parallel", "arbitrary"), ATTN_VMEM_LIMIT),
    )(qn, k, vT)


def _sm_bwd_t(qn, k, v, oT, lse, doT, do, cmul, gscale, causal, name):
    H, S, DK = qn.shape
    Sk, dv = k.shape[1], v.shape[2]
    bq, bk = _attn_blocks(S, Sk)
    nq, nkb = S // bq, Sk // bk

    def body(qn_ref, k_ref, v_ref, oT_ref, lse_ref, doT_ref, do_ref, dq_ref, dk_ref, dv_ref):
        i = pl.program_id(1)

        @pl.when(i == 0)
        def _():
            dk_ref[...] = jnp.zeros_like(dk_ref)
            dv_ref[...] = jnp.zeros_like(dv_ref)

        qnb = qn_ref[...]
        qTb = qnb.T
        dob = do_ref[...]
        doTf = doT_ref[...]
        doTb = doTf.astype(BF16)
        delta = jnp.sum(doTf * oT_ref[...], axis=0, keepdims=True)
        lse = lse_ref[...]

        def blk(j, dq, masked):
            off = pl.multiple_of(j * bk, bk)
            kb = k_ref[pl.ds(off, bk), :]
            sT = _dot(kb, qTb) * cmul
            if masked:
                sT = jnp.where(_valid_t(i, j, bq, bk, False), sT, NEG_BIG)
            p = jnp.exp2(sT - lse)
            dp = _dot(v_ref[pl.ds(off, bk), :], doTb)
            ds = p * (dp - delta)
            dsb = (ds * gscale).astype(BF16) if gscale != 1.0 else ds.astype(BF16)
            dv_ref[pl.ds(off, bk), :] += _dot(p.astype(BF16), dob)
            dk_ref[pl.ds(off, bk), :] += _dot(dsb, qnb)
            return dq + _dot(kb.T, dsb)

        dq = jnp.zeros((DK, bq), F32)
        if causal:
            dq = lax.fori_loop(0, i, lambda j, c: blk(j, c, False), dq)
            dq = blk(i, dq, True)
        else:
            dq = lax.fori_loop(0, nkb, lambda j, c: blk(j, c, False), dq)
        dq_ref[...] = dq.T

    qcol = lambda d: pl.BlockSpec((None, d, bq), lambda h, i: (h, 0, i))
    qrow = lambda d: pl.BlockSpec((None, bq, d), lambda h, i: (h, i, 0))
    krow = lambda d: pl.BlockSpec((None, Sk, d), lambda h, i: (h, 0, 0))
    return pl.pallas_call(
        body, name=name, grid=(H, nq),
        in_specs=[qrow(DK), krow(DK), krow(dv), qcol(dv), qcol(1), qcol(dv), qrow(dv)],
        out_specs=[qrow(DK), krow(DK), krow(dv)],
        out_shape=[jax.ShapeDtypeStruct((H, S, DK), F32), jax.ShapeDtypeStruct((H, Sk, DK), F32),
                   jax.ShapeDtypeStruct((H, Sk, dv), F32)],
        compiler_params=_cp(("parallel", "arbitrary"), ATTN_VMEM_LIMIT),
    )(qn, k, v, oT, lse, doT, do)


def _tri(n, fn):
    r = lax.broadcasted_iota(jnp.int32, (n, n), 0)
    c = lax.broadcasted_iota(jnp.int32, (n, n), 1)
    return jnp.where(fn(r, c), 1.0, 0.0).astype(BF16)


def _key_cumsum(x, tri2, suffix, base):
    bk = x.shape[0]
    c = min(CUMSUM_CHUNK, bk)
    n = bk // c
    hi32 = lax.bitcast_convert_type(lax.bitcast_convert_type(x, jnp.int32) & jnp.int32(-65536), F32)
    hi = hi32.astype(BF16)
    lo = (x - hi32).astype(BF16)
    tot = [jnp.sum(x[a * c:(a + 1) * c], axis=0, keepdims=True) for a in range(n)]
    outs = []
    for a in range(n):
        row = base
        for t in (tot[a + 1:] if suffix else tot[:a]):
            row = row + t
        stacked = jnp.concatenate([hi[a * c:(a + 1) * c], lo[a * c:(a + 1) * c]], axis=0)
        outs.append(_dot(tri2, stacked) + row)
    total = tot[0]
    for t in tot[1:]:
        total = total + t
    return (outs[0] if n == 1 else jnp.concatenate(outs, axis=0)), total


def _tri2(n, fn):
    t = _tri(n, fn)
    return jnp.concatenate([t, t], axis=1)


def _sb_logs(z):
    neg_abs = lax.bitcast_convert_type(lax.bitcast_convert_type(z, jnp.int32) | jnp.int32(-2 ** 31), F32)
    ls = jnp.minimum(z, 0.0) - jnp.log(1.0 + jnp.exp(neg_abs))
    return ls, ls - z


PAIR = LANE // HEAD_DIM
FWD_PAIRS = 2
SB_DEAD = -110.0
SB_BLOCK = 256
FOX_DEAD = -160.0


def _head_lanes(shape, w, axis):
    idx = lax.broadcasted_iota(jnp.int32, shape, axis)
    return (idx >= HEAD_DIM * w) & (idx < HEAD_DIM * (w + 1))


def _bias_rows(w, bq):
    row = lax.broadcasted_iota(jnp.int32, (LANE, bq), 0)
    return jnp.where((row >= 3 * w) & (row < 3 * w + 3), -1.0, 0.0).astype(BF16)


def _merge_pair(parts):
    return jnp.where(_head_lanes(parts[0].shape, 0, 0), parts[0], parts[1]).T


def _smp_fwd(q2, k2, v2, bias, r, causal, name, kstat=None):
    S, C = q2.shape
    Sk = k2.shape[0]
    bq, bk = _attn_blocks(S, Sk)
    nq, nkb, P = S // bq, Sk // bk, C // LANE
    gp = FWD_PAIRS if P % FWD_PAIRS == 0 else 1
    use_f = bias is not None
    if causal:
        assert S == Sk and bq == bk and use_f

    def body(*refs):
        if use_f:
            ks_ref, q_ref, k_ref, v_ref, b_ref, r_ref, o_ref, lse_ref, js_ref = refs
        else:
            q_ref, k_ref, v_ref, o_ref, lse_ref = refs
        i = pl.program_id(1)
        heads = range(PAIR * gp)
        lanes = [slice(LANE * (h // PAIR), LANE * (h // PAIR + 1)) for h in heads]
        qps = [q_ref[:, lanes[h]] for h in heads]
        qTs = [jnp.where(_head_lanes(qps[h].shape, h % PAIR, 1), qps[h], jnp.zeros_like(qps[h])).T for h in heads]
        if use_f:
            qf = [t.astype(F32) for t in qTs]
            qnorm = [jnp.sqrt(jnp.sum(t * t, axis=0, keepdims=True)) for t in qf]
            qTs = [jnp.concatenate([qTs[h], _bias_rows(h % PAIR, bq)], axis=0) for h in heads]

        def blk(j, carry, masked):
            off = pl.multiple_of(j * bk, bk)
            kbs = [k_ref[pl.ds(off, bk), LANE * g:LANE * (g + 1)] for g in range(gp)]
            if use_f:
                kbs = [jnp.concatenate([kbs[g], b_ref[pl.ds(off, bk), LANE * g:LANE * (g + 1)]], axis=1)
                       for g in range(gp)]
            vTbs = [v_ref[pl.ds(off, bk), LANE * g:LANE * (g + 1)].T for g in range(gp)]
            sT = [_dot(kbs[h // PAIR], qTs[h]) * LOG2E for h in heads]
            if masked:
                valid = _valid_t(i, j, bq, bk, False)
                sT = [jnp.where(valid, s, NEG_BIG) for s in sT]
            cm = [jnp.max(s, axis=0, keepdims=True) for s in sT]
            if use_f:
                cm = [cm[h] + r_ref[h] for h in heads]
            m_new = [jnp.maximum(carry[h][0], cm[h]) for h in heads]
            shift = [(m_new[h] - r_ref[h]) if use_f else m_new[h] for h in heads]
            p = [jnp.exp2(sT[h] - shift[h]) for h in heads]
            a = [jnp.exp2(carry[h][0] - m_new[h]) for h in heads]
            l = [a[h] * carry[h][1] + jnp.sum(p[h], axis=0, keepdims=True) for h in heads]
            acc = [a[h] * carry[h][2] + _dot(vTbs[h // PAIR], p[h].astype(BF16)) for h in heads]
            return tuple((m_new[h], l[h], acc[h]) for h in heads)

        def step(jj, state):
            carry, first = state
            j = i - jj
            h0 = pl.program_id(0) * (PAIR * gp)
            bound = [LOG2E * (qnorm[h] * ks_ref[(h0 + h) * nkb + j] - ks_ref[(PAIR * P + h0 + h) * nkb + j])
                     + r_ref[h] - carry[h][0] for h in heads]
            live = jnp.max(functools.reduce(jnp.maximum, bound)) >= FOX_DEAD
            carry = lax.cond(live, lambda cr: blk(j, cr, False), lambda cr: cr, carry)
            return carry, jnp.where(live, j, first)

        carry = tuple((jnp.full((1, bq), NEG_BIG, F32), jnp.zeros((1, bq), F32), jnp.zeros((LANE, bq), F32))
                      for _ in heads)
        if causal:
            carry = blk(i, carry, True)
            carry, first = lax.fori_loop(1, i + 1, step, (carry, i))
            js_ref[0] = jnp.full((1, bq), first, jnp.int32)
        else:
            carry = lax.fori_loop(0, nkb, lambda j, c: blk(j, c, False), carry)
            if use_f:
                js_ref[0] = jnp.zeros((1, bq), jnp.int32)
        for h in heads:
            lse_ref[h] = carry[h][0] + jnp.log2(carry[h][1])
        for g in range(gp):
            o_ref[:, LANE * g:LANE * (g + 1)] = _merge_pair(
                [carry[h][2] / carry[h][1] for h in range(PAIR * g, PAIR * (g + 1))])

    qblk = pl.BlockSpec((bq, LANE * gp), lambda p, i: (i, p))
    kres = pl.BlockSpec((Sk, LANE * gp), lambda p, i: (0, p))
    stat = pl.BlockSpec((PAIR * gp, 1, bq), lambda p, i: (p, 0, i))
    in_specs = [qblk, kres, kres]
    args = [q2, k2, v2]
    out_specs = [qblk, stat]
    out_shape = [jax.ShapeDtypeStruct((S, C), F32), jax.ShapeDtypeStruct((PAIR * P, 1, S), F32)]
    if use_f:
        in_specs = [pl.BlockSpec(memory_space=pltpu.SMEM)] + in_specs + [kres, stat]
        args = [kstat] + args + [bias, r]
        out_specs.append(pl.BlockSpec((1, 1, bq), lambda p, i: (p, 0, i)))
        out_shape.append(jax.ShapeDtypeStruct((P // gp, 1, S), jnp.int32))
    return pl.pallas_call(
        body, name=name, grid=(P // gp, nq), in_specs=in_specs, out_specs=out_specs, out_shape=out_shape,
        compiler_params=_cp(("parallel", "arbitrary"), ATTN_VMEM_LIMIT),
    )(*args)


def _smp_bwd(q2, k2, v2, o2, lse, do2, bias, r, scale, causal, name, first=None):
    S, C = q2.shape
    Sk = k2.shape[0]
    bq, bk = _attn_blocks(S, Sk)
    nq, nkb, P = S // bq, Sk // bk, C // LANE
    use_f = bias is not None

    def body(*refs):
        if use_f:
            (first_ref, q_ref, k_ref, v_ref, o_ref, lse_ref, do_ref, b_ref, r_ref,
             dq_ref, dk_ref, dv_ref, dr_ref, dkey_ref, dk_acc, dv_acc, db_ref) = refs
        else:
            q_ref, k_ref, v_ref, o_ref, lse_ref, do_ref, dq_ref, dk_ref, dv_ref, dk_acc, dv_acc = refs
        i = pl.program_id(1)

        @pl.when(i == 0)
        def _():
            dk_acc[...] = jnp.zeros_like(dk_acc)
            dv_acc[...] = jnp.zeros_like(dv_acc)
            if use_f:
                db_ref[...] = jnp.zeros_like(db_ref)

        qp = q_ref[...]
        dof = do_ref[...]
        prod = dof * o_ref[...]
        heads = range(PAIR)
        mine = [_head_lanes(qp.shape, w, 1) for w in heads]
        qz = [jnp.where(mine[w], qp, jnp.zeros_like(qp)) for w in heads]
        qTs = [qz[w].T for w in heads]
        if use_f:
            qTs = [jnp.concatenate([qTs[w], _bias_rows(w, bq)], axis=0) for w in heads]
        doz = [jnp.where(mine[w], dof, 0.0).astype(BF16) for w in heads]
        doT = [doz[w].T for w in heads]
        delta = [jnp.sum(jnp.where(mine[w], prod, 0.0).T, axis=0, keepdims=True) for w in heads]
        shift = [(lse_ref[w] - r_ref[w]) if use_f else lse_ref[w] for w in heads]

        def blk(j, carry, masked):
            off = pl.multiple_of(j * bk, bk)
            kb = k_ref[pl.ds(off, bk), :]
            kTb = kb.T
            if use_f:
                kb = jnp.concatenate([kb, b_ref[pl.ds(off, bk), :]], axis=1)
            vb = v_ref[pl.ds(off, bk), :]
            sT = [_dot(kb, qTs[w]) * LOG2E for w in heads]
            if masked:
                valid = _valid_t(i, j, bq, bk, False)
                sT = [jnp.where(valid, s, NEG_BIG) for s in sT]
            p = [jnp.exp2(sT[w] - shift[w]) for w in heads]
            dp = [_dot(vb, doT[w]) for w in heads]
            ds = [p[w] * (dp[w] - delta[w]) for w in heads]
            dsb = [d.astype(BF16) for d in ds]
            dvs = [_dot(p[w].astype(BF16), doz[w]) for w in heads]
            dks = [_dot(dsb[w], qz[w]) for w in heads]
            dv_acc[pl.ds(off, bk), :] += dvs[0] + dvs[1]
            dk_acc[pl.ds(off, bk), :] += dks[0] + dks[1]
            dr = [carry[w][1] for w in heads]
            if use_f:
                dr = [dr[w] + jnp.sum(ds[w], axis=0, keepdims=True) for w in heads]
                lane = lax.broadcasted_iota(jnp.int32, (bk, LANE), 1)
                cols = [jnp.where(lane == w, jnp.sum(ds[w], axis=1, keepdims=True), 0.0) for w in heads]
                db_ref[pl.ds(off, bk), :] += cols[0] + cols[1]
            dq = [carry[w][0] + _dot(kTb, dsb[w]) for w in heads]
            return tuple((dq[w], dr[w]) for w in heads)

        carry = tuple((jnp.zeros((LANE, bq), F32), jnp.zeros((1, bq), F32)) for _ in heads)
        if causal:
            start = first_ref[pl.program_id(0) // (P // first.shape[0]), i]
            carry = lax.fori_loop(start, i, lambda j, c: blk(j, c, False), carry)
            carry = blk(i, carry, True)
        else:
            carry = lax.fori_loop(0, nkb, lambda j, c: blk(j, c, False), carry)
        if use_f:
            for w in heads:
                dr_ref[w] = carry[w][1]
        dq_ref[...] = (_merge_pair([carry[w][0] for w in heads]) * scale).astype(BF16)

        @pl.when(i == nq - 1)
        def _():
            dk_ref[...] = dk_acc[...].astype(BF16)
            dv_ref[...] = dv_acc[...].astype(BF16)

        if use_f:
            @pl.when(i == nq - 1)
            def _():
                def chunk(cidx, carry):
                    off = pl.multiple_of(cidx * LANE, LANE)
                    t = db_ref[pl.ds(off, LANE), :].T
                    for w in range(PAIR):
                        dkey_ref[w, :, pl.ds(off, LANE)] = t[w:w + 1, :]
                    return carry

                lax.fori_loop(0, Sk // LANE, chunk, 0)

    qblk = pl.BlockSpec((bq, LANE), lambda p, i: (i, p))
    kres = pl.BlockSpec((Sk, LANE), lambda p, i: (0, p))
    stat = pl.BlockSpec((PAIR, 1, bq), lambda p, i: (p, 0, i))
    in_specs = [qblk, kres, kres, qblk, stat, qblk]
    args = [q2, k2, v2, o2, lse, do2]
    out_specs = [qblk, kres, kres]
    out_shape = [jax.ShapeDtypeStruct((S, C), BF16), jax.ShapeDtypeStruct((Sk, C), BF16),
                 jax.ShapeDtypeStruct((Sk, C), BF16)]
    scratch = [pltpu.VMEM((Sk, LANE), F32), pltpu.VMEM((Sk, LANE), F32)]
    if use_f:
        in_specs = [pl.BlockSpec(memory_space=pltpu.SMEM)] + in_specs + [kres, stat]
        args = [first] + args + [bias, r]
        out_specs += [stat, pl.BlockSpec((PAIR, 1, Sk), lambda p, i: (p, 0, 0))]
        out_shape += [jax.ShapeDtypeStruct((PAIR * P, 1, S), F32), jax.ShapeDtypeStruct((PAIR * P, 1, Sk), F32)]
        scratch.append(pltpu.VMEM((Sk, LANE), F32))
    return pl.pallas_call(
        body, name=name, grid=(P, nq), in_specs=in_specs, out_specs=out_specs, out_shape=out_shape,
        scratch_shapes=scratch, compiler_params=_cp(("parallel", "arbitrary"), ATTN_VMEM_LIMIT),
    )(*args)


def _sbp_fwd(q2, k2, v2, name):
    S, C = q2.shape
    bq, bk = _attn_blocks(S, S, SB_BLOCK)
    assert bq == bk
    nq, P = S // bq, C // LANE
    gp = FWD_PAIRS if P % FWD_PAIRS == 0 else 1
    c = min(CUMSUM_CHUNK, bk)

    def body(q_ref, k_ref, v_ref, o_ref, lt_ref, js_ref):
        i = pl.program_id(1)
        after = _tri2(c, lambda s, j: j > s)
        heads = range(PAIR * gp)
        qps = [q_ref[:, LANE * (h // PAIR):LANE * (h // PAIR + 1)] for h in heads]
        qTs = [jnp.where(_head_lanes(qps[h].shape, h % PAIR, 1), qps[h], jnp.zeros_like(qps[h])).T for h in heads]

        def blk(jj, carry, masked):
            j = i - jj
            off = pl.multiple_of(j * bk, bk)
            kbs = [k_ref[pl.ds(off, bk), LANE * g:LANE * (g + 1)] for g in range(gp)]
            vTbs = [v_ref[pl.ds(off, bk), LANE * g:LANE * (g + 1)].T for g in range(gp)]
            logs = [_sb_logs(_dot(kbs[h // PAIR], qTs[h])) for h in heads]
            ls, lk = [t[0] for t in logs], [t[1] for t in logs]
            if masked:
                valid = _valid_t(i, j, bq, bk, True)
                lk = [jnp.where(valid, t, 0.0) for t in lk]
            cs = [_key_cumsum(lk[h], after, True, carry[h][0]) for h in heads]
            wgt = [jnp.exp(ls[h] + cs[h][0]) for h in heads]
            if masked:
                wgt = [jnp.where(valid, t, 0.0) for t in wgt]
            acc = [carry[h][1] + _dot(vTbs[h // PAIR], wgt[h].astype(BF16)) for h in heads]
            return tuple((carry[h][0] + cs[h][1], acc[h]) for h in heads)

        def step(jj, state):
            carry, first = state
            live = jnp.max(functools.reduce(jnp.maximum, [carry[h][0] for h in heads])) >= SB_DEAD
            carry = lax.cond(live, lambda cr: blk(jj, cr, False), lambda cr: cr, carry)
            return carry, jnp.where(live, i - jj, first)

        carry = tuple((jnp.zeros((1, bq), F32), jnp.zeros((LANE, bq), F32)) for _ in heads)
        carry = blk(0, carry, True)
        carry, first = lax.fori_loop(1, i + 1, step, (carry, i))
        js_ref[0] = jnp.full((1, bq), first, jnp.int32)
        for h in heads:
            lt_ref[h] = carry[h][0]
        for g in range(gp):
            o_ref[:, LANE * g:LANE * (g + 1)] = _merge_pair([carry[h][1] for h in range(PAIR * g, PAIR * (g + 1))])

    qblk = pl.BlockSpec((bq, LANE * gp), lambda p, i: (i, p))
    kres = pl.BlockSpec((S, LANE * gp), lambda p, i: (0, p))
    stat = pl.BlockSpec((PAIR * gp, 1, bq), lambda p, i: (p, 0, i))
    return pl.pallas_call(
        body, name=name, grid=(P // gp, nq),
        in_specs=[qblk, kres, kres],
        out_specs=[qblk, stat, pl.BlockSpec((1, 1, bq), lambda p, i: (p, 0, i))],
        out_shape=[jax.ShapeDtypeStruct((S, C), F32), jax.ShapeDtypeStruct((PAIR * P, 1, S), F32),
                   jax.ShapeDtypeStruct((P // gp, 1, S), jnp.int32)],
        compiler_params=_cp(("parallel", "arbitrary"), ATTN_VMEM_LIMIT),
    )(q2, k2, v2)


def _sbp_bwd(q2, k2, v2, lt, first, do2, scale, name):
    S, C = q2.shape
    bq, bk = _attn_blocks(S, S, SB_BLOCK)
    nq, P = S // bq, C // LANE
    c = min(CUMSUM_CHUNK, bk)

    per_group = P // first.shape[0]

    def body(first_ref, q_ref, k_ref, v_ref, lt_ref, do_ref, dq_ref, dk_ref, dv_ref, dk_acc, dv_acc):
        i = pl.program_id(1)

        @pl.when(i == 0)
        def _():
            dk_acc[...] = jnp.zeros_like(dk_acc)
            dv_acc[...] = jnp.zeros_like(dv_acc)

        qp = q_ref[...]
        dof = do_ref[...]
        upto = _tri2(c, lambda s, j: j <= s)
        before = _tri2(c, lambda s, j: j < s)
        heads = range(PAIR)
        mine = [_head_lanes(qp.shape, w, 1) for w in heads]
        qz = [jnp.where(mine[w], qp, jnp.zeros_like(qp)) for w in heads]
        qTs = [qz[w].T for w in heads]
        doz = [jnp.where(mine[w], dof, 0.0).astype(BF16) for w in heads]
        doT = [doz[w].T for w in heads]
        ltot = [lt_ref[w] for w in heads]

        def blk(j, carry, masked):
            off = pl.multiple_of(j * bk, bk)
            kb = k_ref[pl.ds(off, bk), :]
            vb = v_ref[pl.ds(off, bk), :]
            kTb = kb.T
            logs = [_sb_logs(_dot(kb, qTs[w])) for w in heads]
            ls, lk = [t[0] for t in logs], [t[1] for t in logs]
            if masked:
                valid = _valid_t(i, j, bq, bk, True)
                lk = [jnp.where(valid, t, 0.0) for t in lk]
            pin = [_key_cumsum(lk[w], upto, False, carry[w][1] - ltot[w]) for w in heads]
            wgt = [jnp.exp(ls[w] - pin[w][0]) for w in heads]
            if masked:
                wgt = [jnp.where(valid, t, 0.0) for t in wgt]
            g = [_dot(vb, doT[w]) * wgt[w] for w in heads]
            cin = [_key_cumsum(g[w], before, False, carry[w][2]) for w in heads]
            sig = [jnp.exp(t) for t in ls]
            dz = [g[w] * (1.0 - sig[w]) - cin[w][0] * sig[w] for w in heads]
            if masked:
                dz = [jnp.where(valid, t, 0.0) for t in dz]
            dzb = [t.astype(BF16) for t in dz]
            dvs = [_dot(wgt[w].astype(BF16), doz[w]) for w in heads]
            dks = [_dot(dzb[w], qz[w]) for w in heads]
            dv_acc[pl.ds(off, bk), :] += dvs[0] + dvs[1]
            dk_acc[pl.ds(off, bk), :] += dks[0] + dks[1]
            return tuple((carry[w][0] + _dot(kTb, dzb[w]), carry[w][1] + pin[w][1], carry[w][2] + cin[w][1])
                         for w in heads)

        carry = tuple((jnp.zeros((LANE, bq), F32), jnp.zeros((1, bq), F32), jnp.zeros((1, bq), F32)) for _ in heads)
        start = first_ref[pl.program_id(0) // per_group, i]
        carry = lax.fori_loop(start, i, lambda j, cr: blk(j, cr, False), carry)
        carry = blk(i, carry, True)
        dq_ref[...] = (_merge_pair([carry[w][0] for w in heads]) * scale).astype(BF16)

        @pl.when(i == nq - 1)
        def _():
            dk_ref[...] = dk_acc[...].astype(BF16)
            dv_ref[...] = dv_acc[...].astype(BF16)

    qblk = pl.BlockSpec((bq, LANE), lambda p, i: (i, p))
    kres = pl.BlockSpec((S, LANE), lambda p, i: (0, p))
    stat = pl.BlockSpec((PAIR, 1, bq), lambda p, i: (p, 0, i))
    return pl.pallas_call(
        body, name=name, grid=(P, nq),
        in_specs=[pl.BlockSpec(memory_space=pltpu.SMEM), qblk, kres, kres, stat, qblk],
        out_specs=[qblk, kres, kres],
        out_shape=[jax.ShapeDtypeStruct((S, C), BF16)] * 3,
        scratch_shapes=[pltpu.VMEM((S, LANE), F32), pltpu.VMEM((S, LANE), F32)],
        compiler_params=_cp(("parallel", "arbitrary"), ATTN_VMEM_LIMIT),
    )(first, q2, k2, v2, lt, do2)


def _bias_cols(f_cum):
    H, Sk = f_cum.shape
    terms = jnp.stack(_split3(f_cum), axis=-1)
    packed = terms.reshape(H // PAIR, PAIR, Sk, 3).transpose(2, 0, 1, 3).reshape(Sk, H // PAIR, PAIR * 3)
    return jnp.pad(packed, ((0, 0), (0, 0), (0, LANE - PAIR * 3))).reshape(Sk, -1)


def _make_packed_softmax(name, scale, causal, use_f):
    assert _pow2(scale)

    def run_fwd(q16, k16, v16, f_cum):
        q16 = q16 * scale
        if not use_f:
            o, lse = _smp_fwd(q16, k16, v16, None, None, causal, name + "_fwd")
            return o, (q16, k16, v16, o, lse, None, None, None)
        bq, bk = _attn_blocks(q16.shape[0], k16.shape[0])
        n_heads = f_cum.shape[0]
        knorm = jnp.sqrt(jnp.sum(jnp.square(k16.astype(F32)).reshape(-1, bk, n_heads, HEAD_DIM), axis=3))
        kstat = jnp.concatenate([jnp.max(knorm, axis=1).T.reshape(-1), f_cum[:, bk - 1::bk].reshape(-1)])
        bias, r = _bias_cols(f_cum), (f_cum * LOG2E)[:, None, :]
        o, lse, first = _smp_fwd(q16, k16, v16, bias, r, causal, name + "_fwd", lax.stop_gradient(kstat))
        return o, (q16, k16, v16, o, lse, bias, r, first[:, 0, ::bq])

    def run_bwd(saved, do):
        q16, k16, v16, o, lse, bias, r, first = saved
        outs = _smp_bwd(q16, k16, v16, o, lse, do, bias, r, scale, causal, name + "_bwd", first)
        if use_f:
            return outs[0], outs[1], outs[2], outs[3][:, 0, :] - outs[4][:, 0, :]
        return tuple(outs)

    if use_f:
        @jax.custom_vjp
        def attn(q, k, v, f_cum):
            return run_fwd(q, k, v, f_cum)[0]

        attn.defvjp(run_fwd, run_bwd)
    else:
        @jax.custom_vjp
        def attn(q, k, v):
            return run_fwd(q, k, v, None)[0]

        attn.defvjp(lambda q, k, v: run_fwd(q, k, v, None), run_bwd)
    return attn


def _make_packed_sb(name, scale):
    assert _pow2(scale)

    def run_fwd(q16, k16, v16):
        q16 = q16 * scale
        o, lt, first = _sbp_fwd(q16, k16, v16, name + "_fwd")
        bq, _ = _attn_blocks(q16.shape[0], q16.shape[0], SB_BLOCK)
        return o, (q16, k16, v16, lt, first[:, 0, ::bq])

    def run_bwd(saved, do):
        q16, k16, v16, lt, first = saved
        return tuple(_sbp_bwd(q16, k16, v16, lt, first, do, scale, name + "_bwd"))

    @jax.custom_vjp
    def attn(q, k, v):
        return run_fwd(q, k, v)[0]

    attn.defvjp(run_fwd, run_bwd)
    return attn


def _round_bf16(x):
    return lax.reduce_precision(x, exponent_bits=8, mantissa_bits=7)


def _split3(x):
    hi = _round_bf16(x)
    mid = _round_bf16(x - hi)
    lo = _round_bf16(x - hi - mid)
    return hi.astype(BF16), mid.astype(BF16), lo.astype(BF16)


def _pow2(x):
    m, _ = math.frexp(x)
    return m == 0.5


def _pad_last(x, n):
    return jnp.pad(x, [(0, 0)] * (x.ndim - 1) + [(0, n - x.shape[-1])])


def _layouts(q, k, scale):
    qh = _pad_last(jnp.transpose(q * scale if _pow2(scale) else q, (1, 0, 2)).astype(BF16), LANE)
    return qh, _pad_last(jnp.transpose(k, (1, 0, 2)).astype(BF16), LANE)


def _make_softmax_attn(name, scale, causal, d):
    pre = _pow2(scale)
    cmul = LOG2E if pre else scale * LOG2E
    gscale = 1.0 if pre else scale

    def run_fwd(q, k, v):
        qn, kn = _layouts(q, k, scale)
        vn = jnp.transpose(v, (1, 0, 2)).astype(BF16)
        oT, lse = _sm_fwd_t(qn, kn, jnp.transpose(vn, (0, 2, 1)), cmul, causal, name + "_fwd")
        return jnp.transpose(oT, (2, 0, 1)), (qn, kn, vn, oT, lse)

    def run_bwd(saved, dout):
        qn, kn, vn, oT, lse = saved
        doT = jnp.transpose(dout, (1, 2, 0))
        do = jnp.transpose(dout, (1, 0, 2)).astype(BF16)
        dq, dk, dv = _sm_bwd_t(qn, kn, vn, oT, lse, doT, do, cmul, gscale, causal, name + "_bwd")
        dq = jnp.transpose(dq[:, :, :d], (1, 0, 2))
        if pre:
            dq = dq * scale
        return dq, jnp.transpose(dk[:, :, :d], (1, 0, 2)), jnp.transpose(dv, (1, 0, 2))

    @jax.custom_vjp
    def attn(q, k, v):
        return run_fwd(q, k, v)[0]

    attn.defvjp(run_fwd, run_bwd)
    return attn


def _rope(x, positions):
    half = x.shape[-1] // 2
    inv_freq = ROPE_THETA ** (-jnp.arange(half, dtype=F32) / half)
    ang = positions.astype(F32)[:, None] * inv_freq[None, :]
    ang = ang.reshape((ang.shape[0],) + (1,) * (x.ndim - 2) + (half,))
    cos, sin = jnp.cos(ang), jnp.sin(ang)
    x1, x2 = x[..., :half], x[..., half:]
    return jnp.concatenate([x1 * cos - x2 * sin, x1 * sin + x2 * cos], axis=-1)


def _permute_cols(w):
    parts = [w[..., _ORIG_OFF[idx]:_ORIG_OFF[idx] + SPLIT_SIZES[idx]] for _, idx in _PERM]
    pad = jnp.zeros(w.shape[:-1] + (PROJ_COLS - IN_COLS,), w.dtype)
    return jnp.concatenate(parts + [pad], axis=-1)


def _unpermute_cols(w):
    start, parts = 0, [None] * len(SPLIT_SIZES)
    for _, idx in _PERM:
        parts[idx] = w[..., start:start + SPLIT_SIZES[idx]]
        start += SPLIT_SIZES[idx]
    return jnp.concatenate(parts, axis=-1)


_BF16_PIECES = ("fq", "fk", "fv", "sq", "sk", "sv", "mq")


def _make_ln_proj(name, has_res):
    def split(proj32, proj16):
        out, off = [], 0
        for n, idx in _PERM:
            src = proj16 if n in _BF16_PIECES else proj32
            out.append(src[:, off:off + SPLIT_SIZES[idx]])
            off += SPLIT_SIZES[idx]
        return tuple(out)

    def run_fwd(x, res, g, b, w):
        h, h16, hT16 = _ln_fwd_call(x, res, g, b, name + "_ln_fwd", also16=True)
        w16 = w.astype(BF16)
        proj32, proj16 = _matmul(h16, w16, "nn", name + "_fwd", also16=True)
        return (h, split(proj32, proj16)), (x, res, g, hT16, w16)

    def run_bwd(saved, cts):
        x, res, g, hT16, w16 = saved
        dh, dpieces = cts
        pad = jnp.zeros((x.shape[0], PROJ_COLS - IN_COLS), BF16)
        dy16 = jnp.concatenate([c.astype(BF16) for c in dpieces] + [pad], axis=1)
        da = _matmul(dy16, w16, "nt", name + "_dx")
        dw = _matmul(hT16, dy16, "nn", name + "_dw")
        outs = _ln_bwd_call(dh, x, res, g, name + "_ln_bwd", dy2=da)
        if has_res:
            dx, dr, dg, db = outs
            return dx, dr, dg.reshape(-1), db.reshape(-1), dw
        dx, dg, db = outs
        return dx, dg.reshape(-1), db.reshape(-1), dw

    if has_res:
        @jax.custom_vjp
        def op(x, res, g, b, w):
            return run_fwd(x, res, g, b, w)[0]

        op.defvjp(run_fwd, run_bwd)
    else:
        @jax.custom_vjp
        def op(x, g, b, w):
            return run_fwd(x, None, g, b, w)[0]

        op.defvjp(lambda x, g, b, w: run_fwd(x, None, g, b, w), run_bwd)

    def call(*args):
        h, pieces = op(*args)
        return h, {n: part for (n, _), part in zip(_PERM, pieces)}

    return call


def _trunk_loss(wts, x2d, mem2d, target2d):
    s = x2d.shape[0]
    positions = jnp.arange(s)
    head_scale = HEAD_DIM ** -0.5
    mla_scale = (MLA_NOPE + MLA_ROPE) ** -0.5

    mem_n = _make_ln("ln_mem", False)(mem2d, wts["mem_ln_g"], wts["mem_ln_b"])
    h, y = None, x2d
    for l in range(DEPTH):
        tag = f"l{l}_"
        w_p = wts["w_in"][l]
        if l == 0:
            h, p = _make_ln_proj(tag + "proj", False)(y, wts["ln_in_g"], wts["ln_in_b"], w_p)
        else:
            h, p = _make_ln_proj(tag + "proj", True)(y, h, wts["ln_g"][l - 1], wts["ln_b"][l - 1], w_p)

        log_f = jax.nn.log_sigmoid(p["f_logit"] + wts["b_forget"][l])
        f_cum = jnp.cumsum(log_f, axis=0).T
        out_fox = _make_packed_softmax(tag + "fox", head_scale, True, True)(p["fq"], p["fk"], p["fv"], f_cum)

        out_sb = _make_packed_sb(tag + "sb", head_scale)(p["sq"], p["sk"], p["sv"])

        cqn = _make_rms(tag + "rms_q")(p["c_q"], wts["mla_q_norm_g"][l])
        q_mla = _make_mm(tag + "q_up")(cqn, wts["w_mla_q_up"][l]).reshape(s, N_HEADS, MLA_NOPE + MLA_ROPE)
        ckvn = _make_rms(tag + "rms_kv")(p["c_kv"], wts["mla_kv_norm_g"][l])
        kv_mla = _make_mm(tag + "kv_up")(ckvn, wts["w_mla_kv_up"][l]).reshape(s, N_HEADS, MLA_NOPE + MLA_V)
        q_full = jnp.concatenate([q_mla[..., :MLA_NOPE], _rope(q_mla[..., MLA_NOPE:], positions)], axis=-1)
        k_rope = jnp.broadcast_to(_rope(p["k_rot"], positions)[:, None, :], (s, N_HEADS, MLA_ROPE))
        k_full = jnp.concatenate([kv_mla[..., :MLA_NOPE], k_rope], axis=-1)
        out_mla = _make_softmax_attn(tag + "mla", mla_scale, True, MLA_NOPE + MLA_ROPE)(
            q_full, k_full, kv_mla[..., MLA_NOPE:]).reshape(s, GROUP_W)

        mkv = _make_mm(tag + "mem_kv")(mem_n, wts["w_mem_kv"][l])
        out_mem = _make_packed_softmax(tag + "mem", head_scale, False, False)(
            p["mq"], mkv[:, :GROUP_W].astype(BF16), mkv[:, GROUP_W:].astype(BF16))

        mixed = jnp.concatenate([out_fox, out_sb, out_mla, out_mem], axis=-1)
        y = _make_gate_out(tag + "out")(mixed, p["gate"], wts["w_out"][l])

    h = _make_ln(f"l{DEPTH - 1}_ln", True)(y, h, wts["ln_g"][DEPTH - 1], wts["ln_b"][DEPTH - 1])
    return _loss_op(h, target2d)


def _mesh_pos():
    x, y, c = (lax.axis_index(a) for a in MESH_AXES)
    return x, y, c, 4 * x + 2 * y + c


def _peer(x, y, c, mask):
    return (x ^ ((mask >> 2) & 1), y ^ ((mask >> 1) & 1), c ^ (mask & 1))


_ANY = pl.BlockSpec(memory_space=pl.ANY)


def _all_gather(row_shards, stack_shards):
    n_row, n_all = len(row_shards), len(row_shards) + len(stack_shards)
    shards = list(row_shards) + list(stack_shards)

    def body(*refs):
        ins, outs = refs[:n_all], refs[n_all:2 * n_all]
        send_sems, recv_sems, local_sems = refs[2 * n_all:]
        x, y, c, me = _mesh_pos()

        def window(t, slot):
            if t < n_row:
                rows = shards[t].shape[1]
                return outs[t].at[:, pl.ds(slot * rows, rows), :]
            return outs[t].at[slot]

        local = [pltpu.make_async_copy(ins[t], window(t, me), local_sems.at[t]) for t in range(n_all)]
        for cp in local:
            cp.start()
        sends = []
        for mask in range(1, N_DEV):
            for t in range(n_all):
                cp = pltpu.make_async_remote_copy(
                    src_ref=ins[t], dst_ref=window(t, me), send_sem=send_sems.at[t, mask - 1],
                    recv_sem=recv_sems.at[t, mask - 1], device_id=_peer(x, y, c, mask),
                    device_id_type=pl.DeviceIdType.MESH)
                cp.start()
                sends.append(cp)
        for mask in range(1, N_DEV):
            for t in range(n_all):
                pltpu.make_async_remote_copy(
                    src_ref=ins[t], dst_ref=window(t, me ^ mask), send_sem=send_sems.at[t, mask - 1],
                    recv_sem=recv_sems.at[t, mask - 1], device_id=_peer(x, y, c, mask),
                    device_id_type=pl.DeviceIdType.MESH).wait_recv()
        for cp in sends:
            cp.wait_send()
        for cp in local:
            cp.wait()

    out_shape = [jax.ShapeDtypeStruct((a.shape[0], N_DEV * a.shape[1], a.shape[2]), a.dtype) for a in row_shards]
    out_shape += [jax.ShapeDtypeStruct((N_DEV,) + a.shape, a.dtype) for a in stack_shards]
    return pl.pallas_call(
        body, name="all_gather_weights", in_specs=[_ANY] * n_all, out_specs=[_ANY] * n_all, out_shape=out_shape,
        scratch_shapes=[pltpu.SemaphoreType.DMA((n_all, N_DEV - 1)), pltpu.SemaphoreType.DMA((n_all, N_DEV - 1)),
                        pltpu.SemaphoreType.DMA((n_all,))],
    )(*shards)


def _reduce_scatter(row_full, stack_full, bcast):
    n_row, n_stack = len(row_full), len(stack_full)
    n_all = n_row + n_stack + len(bcast)
    fulls = list(row_full) + list(stack_full) + list(bcast)

    def body(*refs):
        ins, outs = refs[:n_all], refs[n_all:2 * n_all]
        send_sems, recv_sems, local_sems = refs[2 * n_all:]
        x, y, c, me = _mesh_pos()

        def part(t, slot):
            if t < n_row:
                rows = fulls[t].shape[1] // N_DEV
                return ins[t].at[:, pl.ds(slot * rows, rows), :]
            if t < n_row + n_stack:
                return ins[t].at[slot]
            return ins[t]

        local = [pltpu.make_async_copy(part(t, me), outs[t].at[me], local_sems.at[t]) for t in range(n_all)]
        for cp in local:
            cp.start()
        sends = []
        for mask in range(1, N_DEV):
            for t in range(n_all):
                cp = pltpu.make_async_remote_copy(
                    src_ref=part(t, me ^ mask), dst_ref=outs[t].at[me], send_sem=send_sems.at[t, mask - 1],
                    recv_sem=recv_sems.at[t, mask - 1], device_id=_peer(x, y, c, mask),
                    device_id_type=pl.DeviceIdType.MESH)
                cp.start()
                sends.append(cp)
        for mask in range(1, N_DEV):
            for t in range(n_all):
                pltpu.make_async_remote_copy(
                    src_ref=part(t, me), dst_ref=outs[t].at[me ^ mask], send_sem=send_sems.at[t, mask - 1],
                    recv_sem=recv_sems.at[t, mask - 1], device_id=_peer(x, y, c, mask),
                    device_id_type=pl.DeviceIdType.MESH).wait_recv()
        for cp in sends:
            cp.wait_send()
        for cp in local:
            cp.wait()

    out_shape = [jax.ShapeDtypeStruct((N_DEV, a.shape[0], a.shape[1] // N_DEV, a.shape[2]), a.dtype) for a in row_full]
    out_shape += [jax.ShapeDtypeStruct(a.shape, a.dtype) for a in stack_full]
    out_shape += [jax.ShapeDtypeStruct((N_DEV,) + a.shape, a.dtype) for a in bcast]
    return pl.pallas_call(
        body, name="reduce_scatter_grads", in_specs=[_ANY] * n_all, out_specs=[_ANY] * n_all, out_shape=out_shape,
        scratch_shapes=[pltpu.SemaphoreType.DMA((n_all, N_DEV - 1)), pltpu.SemaphoreType.DMA((n_all, N_DEV - 1)),
                        pltpu.SemaphoreType.DMA((n_all,))],
    )(*fulls)


def _adamw(slots, w, m, v, name):
    shape = w.shape
    cols = shape[-1]
    rows = math.prod(shape[:-1])
    tr = _pick(rows, (64, 32, 16, 8))
    c1 = 1.0 - ADAM_B1 ** ADAM_STEP
    c2 = 1.0 - ADAM_B2 ** ADAM_STEP

    def body(s_ref, w_ref, m_ref, v_ref, g_ref, d_ref, nm_ref, nv_ref):
        g = s_ref[0].astype(F32)
        for k in range(1, N_DEV):
            g = g + s_ref[k].astype(F32)
        nm = ADAM_B1 * m_ref[...] + (1.0 - ADAM_B1) * g
        nv = ADAM_B2 * v_ref[...] + (1.0 - ADAM_B2) * (g * g)
        g_ref[...] = g
        nm_ref[...] = nm
        nv_ref[...] = nv
        d_ref[...] = -ADAM_LR * ((nm / c1) / (jnp.sqrt(nv / c2) + ADAM_EPS) + ADAM_WD * w_ref[...])

    row = pl.BlockSpec((tr, cols), lambda i: (i, 0))
    out = jax.ShapeDtypeStruct((rows, cols), F32)
    outs = pl.pallas_call(
        body, name=name, grid=(rows // tr,),
        in_specs=[pl.BlockSpec((N_DEV, tr, cols), lambda i: (0, i, 0)), row, row, row],
        out_specs=[row] * 4, out_shape=[out] * 4, compiler_params=_cp(("parallel",)),
    )(slots.reshape(N_DEV, rows, cols), w.reshape(rows, cols), m.reshape(rows, cols), v.reshape(rows, cols))
    return [o.reshape(shape) for o in outs]


_SMALL = ("ln_in_g", "ln_in_b", "mem_ln_g", "mem_ln_b", "b_forget", "mla_q_norm_g", "mla_kv_norm_g", "ln_g", "ln_b")
_ORDER = ("ln_in_g", "ln_in_b", "mem_ln_g", "mem_ln_b", "w_in", "b_forget", "mla_q_norm_g", "w_mla_q_up",
          "mla_kv_norm_g", "w_mla_kv_up", "w_mem_kv", "w_out", "ln_g", "ln_b")


def _pack_small(d):
    flat = jnp.concatenate([d[n].reshape(-1) for n in _SMALL])
    n = flat.shape[0]
    padded = ((n + 8 * LANE - 1) // (8 * LANE)) * (8 * LANE)
    return jnp.pad(flat, (0, padded - n)).reshape(-1, LANE)


def _unpack_small(packed, like):
    flat, out, off = packed.reshape(-1), {}, 0
    for n in _SMALL:
        size = math.prod(like[n].shape)
        out[n] = flat[off:off + size].reshape(like[n].shape)
        off += size
    return out


def _unstack_cols(g):
    n, l, r, c = g.shape
    return g.transpose(1, 2, 0, 3).reshape(l, r, n * c)


def _stack_cols(g):
    l, r, nc = g.shape
    return g.reshape(l, r, N_DEV, nc // N_DEV).transpose(2, 0, 1, 3)


def kernel(x, mem, ln_in_g, ln_in_b, mem_ln_g, mem_ln_b, w_in, b_forget, mla_q_norm_g, w_mla_q_up, mla_kv_norm_g, w_mla_kv_up, w_mem_kv, w_out, ln_g, ln_b, loss_target, m_ln_in_g, m_ln_in_b, m_mem_ln_g, m_mem_ln_b, m_w_in, m_b_forget, m_mla_q_norm_g, m_w_mla_q_up, m_mla_kv_norm_g, m_w_mla_kv_up, m_w_mem_kv, m_w_out, m_ln_g, m_ln_b, v_ln_in_g, v_ln_in_b, v_mem_ln_g, v_mem_ln_b, v_w_in, v_b_forget, v_mla_q_norm_g, v_w_mla_q_up, v_mla_kv_norm_g, v_w_mla_kv_up, v_w_mem_kv, v_w_out, v_ln_g, v_ln_b):
    w_shard = dict(ln_in_g=ln_in_g, ln_in_b=ln_in_b, mem_ln_g=mem_ln_g, mem_ln_b=mem_ln_b, w_in=w_in,
                   b_forget=b_forget, mla_q_norm_g=mla_q_norm_g, w_mla_q_up=w_mla_q_up,
                   mla_kv_norm_g=mla_kv_norm_g, w_mla_kv_up=w_mla_kv_up, w_mem_kv=w_mem_kv, w_out=w_out,
                   ln_g=ln_g, ln_b=ln_b)
    m_shard = dict(ln_in_g=m_ln_in_g, ln_in_b=m_ln_in_b, mem_ln_g=m_mem_ln_g, mem_ln_b=m_mem_ln_b, w_in=m_w_in,
                   b_forget=m_b_forget, mla_q_norm_g=m_mla_q_norm_g, w_mla_q_up=m_w_mla_q_up,
                   mla_kv_norm_g=m_mla_kv_norm_g, w_mla_kv_up=m_w_mla_kv_up, w_mem_kv=m_w_mem_kv, w_out=m_w_out,
                   ln_g=m_ln_g, ln_b=m_ln_b)
    v_shard = dict(ln_in_g=v_ln_in_g, ln_in_b=v_ln_in_b, mem_ln_g=v_mem_ln_g, mem_ln_b=v_mem_ln_b, w_in=v_w_in,
                   b_forget=v_b_forget, mla_q_norm_g=v_mla_q_norm_g, w_mla_q_up=v_w_mla_q_up,
                   mla_kv_norm_g=v_mla_kv_norm_g, w_mla_kv_up=v_w_mla_kv_up, w_mem_kv=v_w_mem_kv, w_out=v_w_out,
                   ln_g=v_ln_g, ln_b=v_ln_b)

    to16 = lambda ws: [a.astype(BF16) for a in ws]
    gathered = _all_gather(to16([_permute_cols(w_in), w_mem_kv, w_out]), to16([w_mla_q_up, w_mla_kv_up]))
    g_in, g_mem, g_out, g_qup, g_kvup = [a.astype(F32) for a in gathered]
    full = dict(w_shard)
    full.update(w_in=g_in, w_mem_kv=g_mem, w_out=g_out, w_mla_q_up=_unstack_cols(g_qup),
                w_mla_kv_up=_unstack_cols(g_kvup))

    loss_local, (grad_w, grad_x) = jax.value_and_grad(_trunk_loss, argnums=(0, 1))(
        full, x[0], mem[0], loss_target[0])

    s_in, s_mem, s_out, s_qup, s_kvup, s_small = _reduce_scatter(
        to16([grad_w["w_in"], grad_w["w_mem_kv"], grad_w["w_out"]]),
        to16([_stack_cols(grad_w["w_mla_q_up"]), _stack_cols(grad_w["w_mla_kv_up"])]),
        [_pack_small(grad_w)])

    res = {}
    for name, slots in (("w_mem_kv", s_mem), ("w_out", s_out), ("w_mla_q_up", s_qup), ("w_mla_kv_up", s_kvup)):
        res[name] = _adamw(slots, w_shard[name], m_shard[name], v_shard[name], "adamw_" + name)
    res["w_in"] = [_unpermute_cols(a) for a in _adamw(
        s_in, _permute_cols(w_in), _permute_cols(m_w_in), _permute_cols(v_w_in), "adamw_w_in")]
    small = _adamw(s_small, _pack_small(w_shard), _pack_small(m_shard), _pack_small(v_shard), "adamw_small")
    small = [_unpack_small(a, w_shard) for a in small]
    for name in _SMALL:
        res[name] = [a[name] for a in small]

    loss = lax.psum(loss_local, MESH_AXES)
    outs = [loss, grad_x[None]]
    for k in range(4):
        outs += [res[name][k] for name in _ORDER]
    return tuple(outs)
```

```python
import functools
import math

import jax
import jax.numpy as jnp
from jax import lax
from jax.experimental import pallas as pl
from jax.experimental.pallas import tpu as pltpu

F32 = jnp.float32
BF16 = jnp.bfloat16

D_MODEL = 1024
DEPTH = 2
GROUP_W = 256
N_HEADS = 4
HEAD_DIM = 64
MLA_Q_RANK = 256
MLA_KV_RANK = 128
MLA_NOPE = 64
MLA_ROPE = 32
MLA_V = 64
ROPE_THETA = 10000.0
LN_EPS = 1e-5
RMS_EPS = 1e-6
DEEPNORM_ALPHA = (2 * DEPTH) ** 0.25
SPLIT_SIZES = (256, 256, 256, 4, 256, 256, 256, 256, 128, 32, 256, 1024)
IN_COLS = sum(SPLIT_SIZES)
_ORIG_OFF = [sum(SPLIT_SIZES[:i]) for i in range(len(SPLIT_SIZES))]
_PERM = (("fq", 0), ("fk", 1), ("fv", 2), ("sq", 4), ("sk", 5), ("sv", 6), ("c_q", 7), ("c_kv", 8),
         ("mq", 10), ("gate", 11), ("k_rot", 9), ("f_logit", 3))
LANE = 128
PROJ_COLS = ((IN_COLS + LANE - 1) // LANE) * LANE

ADAM_LR = 0.001
ADAM_B1 = 0.9
ADAM_B2 = 0.999
ADAM_EPS = 1e-08
ADAM_WD = 0.01
ADAM_STEP = 10

N_DEV = 8
MESH_AXES = ("x", "y", "c")
VMEM_LIMIT = 48 * 1024 * 1024
ATTN_VMEM_LIMIT = 56 * 1024 * 1024
ATTN_BQ = 512
ATTN_BK = 512
CUMSUM_CHUNK = 256
NEG_BIG = -1e30
LOG2E = math.log2(math.e)
MM_TM, MM_TN, MM_TK, MM_TK_NT = 1024, 1664, 1024, 3328

_NT = (((1,), (1,)), ((), ()))
_NN = (((1,), (0,)), ((), ()))


def _cp(sem, vmem=VMEM_LIMIT):
    return pltpu.CompilerParams(dimension_semantics=sem, vmem_limit_bytes=vmem)


def _dot(a, b, dims=_NN):
    return lax.dot_general(a, b, dims, preferred_element_type=F32)


def _pick(n, cands):
    for c in cands:
        if c <= n and n % c == 0:
            return c
    return n


def _tile(n, cap):
    if n <= cap:
        return n
    best = None
    for d in range(LANE, cap + 1, LANE):
        if n % d == 0:
            best = d
    assert best is not None, (n, cap)
    return best


def _matmul(a, b, mode, name, also16=False):
    if mode == "nn":
        (M, K), (K2, N) = a.shape, b.shape
    else:
        (M, K), (N, K2) = a.shape, b.shape
    assert K == K2 and a.dtype == BF16 and b.dtype == BF16, (a.shape, b.shape, mode)
    tm, tn = _tile(M, MM_TM), _tile(N, MM_TN)
    tk = _tile(K, MM_TK if mode == "nn" else MM_TK_NT)
    nk = K // tk
    dims = _NN if mode == "nn" else _NT

    def body(a_ref, b_ref, *rest):
        o_ref, acc_ref = rest[0], rest[-1]
        part = _dot(a_ref[...], b_ref[...], dims)
        if nk == 1:
            o_ref[...] = part
            if also16:
                rest[1][...] = part.astype(BF16)
        else:
            assert not also16
            k = pl.program_id(2)

            @pl.when(k == 0)
            def _():
                acc_ref[...] = part

            @pl.when(k > 0)
            def _():
                acc_ref[...] += part

            @pl.when(k == nk - 1)
            def _():
                o_ref[...] = acc_ref[...]

    a_spec = pl.BlockSpec((tm, tk), lambda j, i, k: (i, k))
    if mode == "nn":
        b_spec = pl.BlockSpec((tk, tn), lambda j, i, k: (k, j))
    else:
        b_spec = pl.BlockSpec((tn, tk), lambda j, i, k: (j, k))
    acc_shape = (tm, tn) if nk > 1 else (8, LANE)
    o_spec = pl.BlockSpec((tm, tn), lambda j, i, k: (i, j))
    outs = pl.pallas_call(
        body, name=name, grid=(N // tn, M // tm, nk),
        in_specs=[a_spec, b_spec],
        out_specs=[o_spec, o_spec] if also16 else o_spec,
        out_shape=[jax.ShapeDtypeStruct((M, N), F32), jax.ShapeDtypeStruct((M, N), BF16)] if also16
        else jax.ShapeDtypeStruct((M, N), F32),
        scratch_shapes=[pltpu.VMEM(acc_shape, F32)],
        compiler_params=_cp(("parallel", "parallel", "arbitrary")),
    )(a, b)
    return outs


def _make_mm(name):
    @jax.custom_vjp
    def mm(a, w):
        return _matmul(a.astype(BF16), w.astype(BF16), "nn", name + "_fwd")

    def fwd(a, w):
        a16, w16 = a.astype(BF16), w.astype(BF16)
        return _matmul(a16, w16, "nn", name + "_fwd"), (a16, w16)

    def bwd(res, dy):
        a16, w16 = res
        dy16 = dy.astype(BF16)
        da = _matmul(dy16, w16, "nt", name + "_dx")
        dw = _matmul(a16.T, dy16, "nn", name + "_dw")
        return da, dw

    mm.defvjp(fwd, bwd)
    return mm


def _row_tile(rows):
    return _pick(rows, (512, 256, 128, 64, 32, 16, 8))


def _ln_stats(u):
    mu = jnp.mean(u, axis=-1, keepdims=True)
    d = u - mu
    var = jnp.mean(d * d, axis=-1, keepdims=True)
    return d, lax.rsqrt(var + LN_EPS)


def _ln_fwd_call(x, res, g, b, name, also16=False):
    rows, dm = x.shape
    tr = _row_tile(rows)
    has_res = res is not None
    n_in = 2 if has_res else 1

    def body(*refs):
        if has_res:
            u = DEEPNORM_ALPHA * refs[1][...] + refs[0][...]
        else:
            u = refs[0][...]
        g_ref, b_ref = refs[n_in], refs[n_in + 1]
        d, rstd = _ln_stats(u)
        y = d * rstd * g_ref[...] + b_ref[...]
        refs[n_in + 2][...] = y
        if also16:
            y16 = y.astype(BF16)
            refs[n_in + 3][...] = y16
            refs[n_in + 4][...] = y16.T

    row = pl.BlockSpec((tr, dm), lambda i: (i, 0))
    vec = pl.BlockSpec((1, dm), lambda i: (0, 0))
    args = (x, res) if has_res else (x,)
    out_specs, out_shape = [row], [jax.ShapeDtypeStruct((rows, dm), F32)]
    if also16:
        out_specs += [row, pl.BlockSpec((dm, tr), lambda i: (0, i))]
        out_shape += [jax.ShapeDtypeStruct((rows, dm), BF16), jax.ShapeDtypeStruct((dm, rows), BF16)]
    outs = pl.pallas_call(
        body, name=name, grid=(rows // tr,),
        in_specs=[row] * n_in + [vec, vec], out_specs=out_specs, out_shape=out_shape,
        compiler_params=_cp(("parallel",)),
    )(*args, g.reshape(1, dm), b.reshape(1, dm))
    return outs if also16 else outs[0]


def _ln_bwd_call(dy, x, res, g, name, dy2=None):
    rows, dm = x.shape
    tr = _row_tile(rows)
    has_res = res is not None
    two = dy2 is not None

    def body(*refs):
        dy_ref, refs = refs[0], refs[1:]
        if two:
            dy2_ref, refs = refs[0], refs[1:]
        if has_res:
            x_ref, r_ref, g_ref, dx_ref, dr_ref, dg_ref, db_ref = refs
            u = DEEPNORM_ALPHA * r_ref[...] + x_ref[...]
        else:
            x_ref, g_ref, dx_ref, dg_ref, db_ref = refs
            u = x_ref[...]
        i = pl.program_id(0)
        d, rstd = _ln_stats(u)
        xhat = d * rstd
        dyv = dy_ref[...] + dy2_ref[...] if two else dy_ref[...]
        dxh = dyv * g_ref[...]
        m1 = jnp.mean(dxh, axis=-1, keepdims=True)
        m2 = jnp.mean(dxh * xhat, axis=-1, keepdims=True)
        du = rstd * (dxh - m1 - xhat * m2)
        dx_ref[...] = du
        if has_res:
            dr_ref[...] = DEEPNORM_ALPHA * du
        pg = jnp.sum(dyv * xhat, axis=0, keepdims=True)
        pb = jnp.sum(dyv, axis=0, keepdims=True)

        @pl.when(i == 0)
        def _():
            dg_ref[...] = pg
            db_ref[...] = pb

        @pl.when(i > 0)
        def _():
            dg_ref[...] += pg
            db_ref[...] += pb

    row = pl.BlockSpec((tr, dm), lambda i: (i, 0))
    vec = pl.BlockSpec((1, dm), lambda i: (0, 0))
    big = jax.ShapeDtypeStruct((rows, dm), F32)
    small = jax.ShapeDtypeStruct((1, dm), F32)
    args = ((dy, dy2) if two else (dy,)) + ((x, res) if has_res else (x,))
    n_big = 2 if has_res else 1
    outs = pl.pallas_call(
        body, name=name, grid=(rows // tr,),
        in_specs=[row] * len(args) + [vec],
        out_specs=[row] * n_big + [vec, vec],
        out_shape=[big] * n_big + [small, small],
        compiler_params=_cp(("arbitrary",)),
    )(*args, g.reshape(1, dm))
    return outs


def _make_ln(name, has_res):
    if has_res:
        @jax.custom_vjp
        def ln(x, res, g, b):
            return _ln_fwd_call(x, res, g, b, name + "_fwd")

        def fwd(x, res, g, b):
            return ln(x, res, g, b), (x, res, g)

        def bwd(saved, dy):
            x, res, g = saved
            dx, dr, dg, db = _ln_bwd_call(dy, x, res, g, name + "_bwd")
            return dx, dr, dg.reshape(-1), db.reshape(-1)
    else:
        @jax.custom_vjp
        def ln(x, g, b):
            return _ln_fwd_call(x, None, g, b, name + "_fwd")

        def fwd(x, g, b):
            return ln(x, g, b), (x, g)

        def bwd(saved, dy):
            x, g = saved
            dx, dg, db = _ln_bwd_call(dy, x, None, g, name + "_bwd")
            return dx, dg.reshape(-1), db.reshape(-1)

    ln.defvjp(fwd, bwd)
    return ln


def _rms_fwd_call(x, g, name):
    rows, dm = x.shape
    tr = _row_tile(rows)

    def body(x_ref, g_ref, o_ref):
        xv = x_ref[...]
        rstd = lax.rsqrt(jnp.mean(xv * xv, axis=-1, keepdims=True) + RMS_EPS)
        o_ref[...] = xv * rstd * g_ref[...]

    row = pl.BlockSpec((tr, dm), lambda i: (i, 0))
    vec = pl.BlockSpec((1, dm), lambda i: (0, 0))
    return pl.pallas_call(
        body, name=name, grid=(rows // tr,), in_specs=[row, vec], out_specs=row,
        out_shape=jax.ShapeDtypeStruct((rows, dm), F32), compiler_params=_cp(("parallel",)),
    )(x, g.reshape(1, dm))


def _rms_bwd_call(dy, x, g, name):
    rows, dm = x.shape
    tr = _row_tile(rows)

    def body(dy_ref, x_ref, g_ref, dx_ref, dg_ref):
        i = pl.program_id(0)
        xv = x_ref[...]
        dyv = dy_ref[...]
        rstd = lax.rsqrt(jnp.mean(xv * xv, axis=-1, keepdims=True) + RMS_EPS)
        xhat = xv * rstd
        dxh = dyv * g_ref[...]
        m2 = jnp.mean(dxh * xhat, axis=-1, keepdims=True)
        dx_ref[...] = rstd * (dxh - xhat * m2)
        pg = jnp.sum(dyv * xhat, axis=0, keepdims=True)

        @pl.when(i == 0)
        def _():
            dg_ref[...] = pg

        @pl.when(i > 0)
        def _():
            dg_ref[...] += pg

    row = pl.BlockSpec((tr, dm), lambda i: (i, 0))
    vec = pl.BlockSpec((1, dm), lambda i: (0, 0))
    return pl.pallas_call(
        body, name=name, grid=(rows // tr,), in_specs=[row, row, vec], out_specs=[row, vec],
        out_shape=[jax.ShapeDtypeStruct((rows, dm), F32), jax.ShapeDtypeStruct((1, dm), F32)],
        compiler_params=_cp(("arbitrary",)),
    )(dy, x, g.reshape(1, dm))


def _make_rms(name):
    @jax.custom_vjp
    def rms(x, g):
        return _rms_fwd_call(x, g, name + "_fwd")

    def fwd(x, g):
        return rms(x, g), (x, g)

    def bwd(saved, dy):
        x, g = saved
        dx, dg = _rms_bwd_call(dy, x, g, name + "_bwd")
        return dx, dg.reshape(-1)

    rms.defvjp(fwd, bwd)
    return rms


def _sigmoid(x):
    return 1.0 / (1.0 + jnp.exp(-x))


def _gate_fwd_call(mixed, gate, name):
    rows, dm = mixed.shape
    tr = _row_tile(rows)

    def body(m_ref, g_ref, o_ref, oT_ref):
        gv = g_ref[...]
        y16 = (m_ref[...] * (gv * _sigmoid(gv))).astype(BF16)
        o_ref[...] = y16
        oT_ref[...] = y16.T

    row = pl.BlockSpec((tr, dm), lambda i: (i, 0))
    return pl.pallas_call(
        body, name=name, grid=(rows // tr,), in_specs=[row, row],
        out_specs=[row, pl.BlockSpec((dm, tr), lambda i: (0, i))],
        out_shape=[jax.ShapeDtypeStruct((rows, dm), BF16), jax.ShapeDtypeStruct((dm, rows), BF16)],
        compiler_params=_cp(("parallel",)),
    )(mixed, gate)


def _gate_bwd_call(dy, mixed, gate, name):
    rows, dm = mixed.shape
    tr = _row_tile(rows)

    def body(dy_ref, m_ref, g_ref, dm_ref, dg_ref):
        gv = g_ref[...]
        dyv = dy_ref[...]
        sg = _sigmoid(gv)
        dm_ref[...] = dyv * (gv * sg)
        dg_ref[...] = dyv * m_ref[...] * (sg * (1.0 + gv * (1.0 - sg)))

    row = pl.BlockSpec((tr, dm), lambda i: (i, 0))
    out = jax.ShapeDtypeStruct((rows, dm), F32)
    return pl.pallas_call(
        body, name=name, grid=(rows // tr,), in_specs=[row, row, row], out_specs=[row, row],
        out_shape=[out, out], compiler_params=_cp(("parallel",)),
    )(dy, mixed, gate)


def _make_gate_out(name):
    def run_fwd(mixed, gate, w):
        g16, gT16 = _gate_fwd_call(mixed, gate, name + "_gate_fwd")
        w16 = w.astype(BF16)
        return _matmul(g16, w16, "nn", name + "_fwd"), (mixed, gate, gT16, w16)

    def run_bwd(saved, dy):
        mixed, gate, gT16, w16 = saved
        dy16 = dy.astype(BF16)
        dgated = _matmul(dy16, w16, "nt", name + "_dx")
        dmix, dgate = _gate_bwd_call(dgated, mixed, gate, name + "_gate_bwd")
        return dmix, dgate, _matmul(gT16, dy16, "nn", name + "_dw")

    @jax.custom_vjp
    def gate_out(mixed, gate, w):
        return run_fwd(mixed, gate, w)[0]

    gate_out.defvjp(run_fwd, run_bwd)
    return gate_out


def _loss_call(y, t, name):
    rows, dm = y.shape
    tr = _row_tile(rows)

    def body(y_ref, t_ref, l_ref, d_ref):
        i = pl.program_id(0)
        e = y_ref[...] - t_ref[...]
        d_ref[...] = e * (1.0 / dm)
        part = 0.5 * jnp.sum(jnp.mean(e * e, axis=-1, keepdims=True), axis=0, keepdims=True)

        @pl.when(i == 0)
        def _():
            l_ref[...] = part

        @pl.when(i > 0)
        def _():
            l_ref[...] += part

    row = pl.BlockSpec((tr, dm), lambda i: (i, 0))
    one = pl.BlockSpec((1, 1), lambda i: (0, 0))
    return pl.pallas_call(
        body, name=name, grid=(rows // tr,), in_specs=[row, row], out_specs=[one, row],
        out_shape=[jax.ShapeDtypeStruct((1, 1), F32), jax.ShapeDtypeStruct((rows, dm), F32)],
        compiler_params=_cp(("arbitrary",)),
    )(y, t)


@jax.custom_vjp
def _loss_op(y, t):
    return _loss_call(y, t, "loss_head")[0][0, 0]


def _loss_fwd(y, t):
    l, d = _loss_call(y, t, "loss_head")
    return l[0, 0], d


def _loss_bwd(d, ct):
    return ct * d, jnp.zeros_like(d)


_loss_op.defvjp(_loss_fwd, _loss_bwd)


def _attn_blocks(S, Sk, cap=None):
    bq, bk = min(cap or ATTN_BQ, S), min(cap or ATTN_BK, Sk)
    assert S % bq == 0 and Sk % bk == 0
    return bq, bk


def _valid_t(i, j, bq, bk, strict):
    key = j * bk + lax.broadcasted_iota(jnp.int32, (bk, bq), 0)
    qry = i * bq + lax.broadcasted_iota(jnp.int32, (bk, bq), 1)
    return (key < qry) if strict else (key <= qry)


def _sm_fwd_t(qn, k, vT, cmul, causal, name):
    H, S, DK = qn.shape
    Sk, dv = k.shape[1], vT.shape[1]
    bq, bk = _attn_blocks(S, Sk)
    nq, nkb = S // bq, Sk // bk
    hb = PAIR * FWD_PAIRS if H % (PAIR * FWD_PAIRS) == 0 else 1
    heads = range(hb)
    if causal:
        assert S == Sk and bq == bk

    def body(qn_ref, k_ref, vT_ref, oT_ref, lse_ref):
        i = pl.program_id(1)
        qTs = [qn_ref[w].T for w in heads]

        def blk(j, carry, masked):
            off = pl.multiple_of(j * bk, bk)
            sT = [_dot(k_ref[w, pl.ds(off, bk), :], qTs[w]) * cmul for w in heads]
            if masked:
                valid = _valid_t(i, j, bq, bk, False)
                sT = [jnp.where(valid, s, NEG_BIG) for s in sT]
            m_new = [jnp.maximum(carry[w][0], jnp.max(sT[w], axis=0, keepdims=True)) for w in heads]
            p = [jnp.exp2(sT[w] - m_new[w]) for w in heads]
            a = [jnp.exp2(carry[w][0] - m_new[w]) for w in heads]
            l = [a[w] * carry[w][1] + jnp.sum(p[w], axis=0, keepdims=True) for w in heads]
            acc = [a[w] * carry[w][2] + _dot(vT_ref[w, :, pl.ds(off, bk)], p[w].astype(BF16)) for w in heads]
            return tuple((m_new[w], l[w], acc[w]) for w in heads)

        carry = tuple((jnp.full((1, bq), NEG_BIG, F32), jnp.zeros((1, bq), F32), jnp.zeros((dv, bq), F32))
                      for _ in heads)
        if causal:
            carry = lax.fori_loop(0, i, lambda j, c: blk(j, c, False), carry)
            carry = blk(i, carry, True)
        else:
            carry = lax.fori_loop(0, nkb, lambda j, c: blk(j, c, False), carry)
        for w in heads:
            oT_ref[w] = carry[w][2] / carry[w][1]
            lse_ref[w] = carry[w][0] + jnp.log2(carry[w][1])

    qcol = lambda d: pl.BlockSpec((hb, d, bq), lambda h, i: (h, 0, i))
    return pl.pallas_call(
        body, name=name, grid=(H // hb, nq),
        in_specs=[pl.BlockSpec((hb, bq, DK), lambda h, i: (h, i, 0)), pl.BlockSpec((hb, Sk, DK), lambda h, i: (h, 0, 0)),
                  pl.BlockSpec((hb, dv, Sk), lambda h, i: (h, 0, 0))],
        out_specs=[qcol(dv), qcol(1)],
        out_shape=[jax.ShapeDtypeStruct((H, dv, S), F32), jax.ShapeDtypeStruct((H, 1, S), F32)],
        compiler_params=_cp(("parallel", "arbitrary"), ATTN_VMEM_LIMIT),
    )(qn, k, vT)


def _sm_bwd_t(qn, k, v, oT, lse, doT, do, cmul, gscale, causal, name):
    H, S, DK = qn.shape
    Sk, dv = k.shape[1], v.shape[2]
    bq, bk = _attn_blocks(S, Sk)
    nq, nkb = S // bq, Sk // bk

    def body(qn_ref, k_ref, v_ref, oT_ref, lse_ref, doT_ref, do_ref, dq_ref, dk_ref, dv_ref):
        i = pl.program_id(1)

        @pl.when(i == 0)
        def _():
            dk_ref[...] = jnp.zeros_like(dk_ref)
            dv_ref[...] = jnp.zeros_like(dv_ref)

        qnb = qn_ref[...]
        qTb = qnb.T
        dob = do_ref[...]
        doTf = doT_ref[...]
        doTb = doTf.astype(BF16)
        delta = jnp.sum(doTf * oT_ref[...], axis=0, keepdims=True)
        lse = lse_ref[...]

        def blk(j, dq, masked):
            off = pl.multiple_of(j * bk, bk)
            kb = k_ref[pl.ds(off, bk), :]
            sT = _dot(kb, qTb) * cmul
            if masked:
                sT = jnp.where(_valid_t(i, j, bq, bk, False), sT, NEG_BIG)
            p = jnp.exp2(sT - lse)
            dp = _dot(v_ref[pl.ds(off, bk), :], doTb)
            ds = p * (dp - delta)
            dsb = (ds * gscale).astype(BF16) if gscale != 1.0 else ds.astype(BF16)
            dv_ref[pl.ds(off, bk), :] += _dot(p.astype(BF16), dob)
            dk_ref[pl.ds(off, bk), :] += _dot(dsb, qnb)
            return dq + _dot(kb.T, dsb)

        dq = jnp.zeros((DK, bq), F32)
        if causal:
            dq = lax.fori_loop(0, i, lambda j, c: blk(j, c, False), dq)
            dq = blk(i, dq, True)
        else:
            dq = lax.fori_loop(0, nkb, lambda j, c: blk(j, c, False), dq)
        dq_ref[...] = dq.T

    qcol = lambda d: pl.BlockSpec((None, d, bq), lambda h, i: (h, 0, i))
    qrow = lambda d: pl.BlockSpec((None, bq, d), lambda h, i: (h, i, 0))
    krow = lambda d: pl.BlockSpec((None, Sk, d), lambda h, i: (h, 0, 0))
    return pl.pallas_call(
        body, name=name, grid=(H, nq),
        in_specs=[qrow(DK), krow(DK), krow(dv), qcol(dv), qcol(1), qcol(dv), qrow(dv)],
        out_specs=[qrow(DK), krow(DK), krow(dv)],
        out_shape=[jax.ShapeDtypeStruct((H, S, DK), F32), jax.ShapeDtypeStruct((H, Sk, DK), F32),
                   jax.ShapeDtypeStruct((H, Sk, dv), F32)],
        compiler_params=_cp(("parallel", "arbitrary"), ATTN_VMEM_LIMIT),
    )(qn, k, v, oT, lse, doT, do)


def _tri(n, fn):
    r = lax.broadcasted_iota(jnp.int32, (n, n), 0)
    c = lax.broadcasted_iota(jnp.int32, (n, n), 1)
    return jnp.where(fn(r, c), 1.0, 0.0).astype(BF16)


def _key_cumsum(x, tri2, suffix, base):
    bk = x.shape[0]
    c = min(CUMSUM_CHUNK, bk)
    n = bk // c
    hi32 = lax.bitcast_convert_type(lax.bitcast_convert_type(x, jnp.int32) & jnp.int32(-65536), F32)
    hi = hi32.astype(BF16)
    lo = (x - hi32).astype(BF16)
    tot = [jnp.sum(x[a * c:(a + 1) * c], axis=0, keepdims=True) for a in range(n)]
    outs = []
    for a in range(n):
        row = base
        for t in (tot[a + 1:] if suffix else tot[:a]):
            row = row + t
        stacked = jnp.concatenate([hi[a * c:(a + 1) * c], lo[a * c:(a + 1) * c]], axis=0)
        outs.append(_dot(tri2, stacked) + row)
    total = tot[0]
    for t in tot[1:]:
        total = total + t
    return (outs[0] if n == 1 else jnp.concatenate(outs, axis=0)), total


def _tri2(n, fn):
    t = _tri(n, fn)
    return jnp.concatenate([t, t], axis=1)


def _sb_logs(z):
    neg_abs = lax.bitcast_convert_type(lax.bitcast_convert_type(z, jnp.int32) | jnp.int32(-2 ** 31), F32)
    ls = jnp.minimum(z, 0.0) - jnp.log(1.0 + jnp.exp(neg_abs))
    return ls, ls - z


PAIR = LANE // HEAD_DIM
FWD_PAIRS = 2
SB_DEAD = -110.0
SB_BLOCK = 256
FOX_DEAD = -160.0
FOX_BLOCK = 256


def _head_lanes(shape, w, axis):
    idx = lax.broadcasted_iota(jnp.int32, shape, axis)
    return (idx >= HEAD_DIM * w) & (idx < HEAD_DIM * (w + 1))


def _bias_rows(w, bq):
    row = lax.broadcasted_iota(jnp.int32, (LANE, bq), 0)
    return jnp.where((row >= 3 * w) & (row < 3 * w + 3), -1.0, 0.0).astype(BF16)


def _merge_pair(parts):
    return jnp.where(_head_lanes(parts[0].shape, 0, 0), parts[0], parts[1]).T


def _smp_fwd(q2, k2, v2, bias, r, causal, name, kstat=None):
    S, C = q2.shape
    Sk = k2.shape[0]
    bq, bk = _attn_blocks(S, Sk, FOX_BLOCK if bias is not None else None)
    nq, nkb, P = S // bq, Sk // bk, C // LANE
    gp = FWD_PAIRS if P % FWD_PAIRS == 0 else 1
    use_f = bias is not None
    if causal:
        assert S == Sk and bq == bk and use_f

    def body(*refs):
        if use_f:
            ks_ref, q_ref, k_ref, v_ref, b_ref, r_ref, o_ref, lse_ref, js_ref = refs
        else:
            q_ref, k_ref, v_ref, o_ref, lse_ref = refs
        i = pl.program_id(1)
        heads = range(PAIR * gp)
        lanes = [slice(LANE * (h // PAIR), LANE * (h // PAIR + 1)) for h in heads]
        qps = [q_ref[:, lanes[h]] for h in heads]
        qTs = [jnp.where(_head_lanes(qps[h].shape, h % PAIR, 1), qps[h], jnp.zeros_like(qps[h])).T for h in heads]
        if use_f:
            qf = [t.astype(F32) for t in qTs]
            qnorm = [jnp.sqrt(jnp.sum(t * t, axis=0, keepdims=True)) for t in qf]
            qTs = [jnp.concatenate([qTs[h], _bias_rows(h % PAIR, bq)], axis=0) for h in heads]

        def blk(j, carry, masked):
            off = pl.multiple_of(j * bk, bk)
            kbs = [k_ref[pl.ds(off, bk), LANE * g:LANE * (g + 1)] for g in range(gp)]
            if use_f:
                kbs = [jnp.concatenate([kbs[g], b_ref[pl.ds(off, bk), LANE * g:LANE * (g + 1)]], axis=1)
                       for g in range(gp)]
            vTbs = [v_ref[pl.ds(off, bk), LANE * g:LANE * (g + 1)].T for g in range(gp)]
            sT = [_dot(kbs[h // PAIR], qTs[h]) * LOG2E for h in heads]
            if masked:
                valid = _valid_t(i, j, bq, bk, False)
                sT = [jnp.where(valid, s, NEG_BIG) for s in sT]
            cm = [jnp.max(s, axis=0, keepdims=True) for s in sT]
            if use_f:
                cm = [cm[h] + r_ref[h] for h in heads]
            m_new = [jnp.maximum(carry[h][0], cm[h]) for h in heads]
            shift = [(m_new[h] - r_ref[h]) if use_f else m_new[h] for h in heads]
            p = [jnp.exp2(sT[h] - shift[h]) for h in heads]
            a = [jnp.exp2(carry[h][0] - m_new[h]) for h in heads]
            l = [a[h] * carry[h][1] + jnp.sum(p[h], axis=0, keepdims=True) for h in heads]
            acc = [a[h] * carry[h][2] + _dot(vTbs[h // PAIR], p[h].astype(BF16)) for h in heads]
            return tuple((m_new[h], l[h], acc[h]) for h in heads)

        def step(jj, state):
            carry, first = state
            j = i - jj
            h0 = pl.program_id(0) * (PAIR * gp)
            bound = [LOG2E * (qnorm[h] * ks_ref[(h0 + h) * nkb + j] - ks_ref[(PAIR * P + h0 + h) * nkb + j])
                     + r_ref[h] - carry[h][0] for h in heads]
            live = jnp.max(functools.reduce(jnp.maximum, bound)) >= FOX_DEAD
            carry = lax.cond(live, lambda cr: blk(j, cr, False), lambda cr: cr, carry)
            return carry, jnp.where(live, j, first)

        carry = tuple((jnp.full((1, bq), NEG_BIG, F32), jnp.zeros((1, bq), F32), jnp.zeros((LANE, bq), F32))
                      for _ in heads)
        if causal:
            carry = blk(i, carry, True)
            carry, first = lax.fori_loop(1, i + 1, step, (carry, i))
            js_ref[0] = jnp.full((1, bq), first, jnp.int32)
        else:
            carry = lax.fori_loop(0, nkb, lambda j, c: blk(j, c, False), carry)
            if use_f:
                js_ref[0] = jnp.zeros((1, bq), jnp.int32)
        for h in heads:
            lse_ref[h] = carry[h][0] + jnp.log2(carry[h][1])
        for g in range(gp):
            o_ref[:, LANE * g:LANE * (g + 1)] = _merge_pair(
                [carry[h][2] / carry[h][1] for h in range(PAIR * g, PAIR * (g + 1))])

    qblk = pl.BlockSpec((bq, LANE * gp), lambda p, i: (i, p))
    kres = pl.BlockSpec((Sk, LANE * gp), lambda p, i: (0, p))
    stat = pl.BlockSpec((PAIR * gp, 1, bq), lambda p, i: (p, 0, i))
    in_specs = [qblk, kres, kres]
    args = [q2, k2, v2]
    out_specs = [qblk, stat]
    out_shape = [jax.ShapeDtypeStruct((S, C), F32), jax.ShapeDtypeStruct((PAIR * P, 1, S), F32)]
    if use_f:
        in_specs = [pl.BlockSpec(memory_space=pltpu.SMEM)] + in_specs + [kres, stat]
        args = [kstat] + args + [bias, r]
        out_specs.append(pl.BlockSpec((1, 1, bq), lambda p, i: (p, 0, i)))
        out_shape.append(jax.ShapeDtypeStruct((P // gp, 1, S), jnp.int32))
    return pl.pallas_call(
        body, name=name, grid=(P // gp, nq), in_specs=in_specs, out_specs=out_specs, out_shape=out_shape,
        compiler_params=_cp(("parallel", "arbitrary"), ATTN_VMEM_LIMIT),
    )(*args)


def _smp_bwd(q2, k2, v2, o2, lse, do2, bias, r, scale, causal, name, first=None):
    S, C = q2.shape
    Sk = k2.shape[0]
    bq, bk = _attn_blocks(S, Sk, FOX_BLOCK if bias is not None else None)
    nq, nkb, P = S // bq, Sk // bk, C // LANE
    use_f = bias is not None

    def body(*refs):
        if use_f:
            (first_ref, q_ref, k_ref, v_ref, o_ref, lse_ref, do_ref, b_ref, r_ref,
             dq_ref, dk_ref, dv_ref, dr_ref, dkey_ref, dk_acc, dv_acc, db_ref) = refs
        else:
            q_ref, k_ref, v_ref, o_ref, lse_ref, do_ref, dq_ref, dk_ref, dv_ref, dk_acc, dv_acc = refs
        i = pl.program_id(1)

        @pl.when(i == 0)
        def _():
            dk_acc[...] = jnp.zeros_like(dk_acc)
            dv_acc[...] = jnp.zeros_like(dv_acc)
            if use_f:
                db_ref[...] = jnp.zeros_like(db_ref)

        qp = q_ref[...]
        dof = do_ref[...]
        prod = dof * o_ref[...]
        heads = range(PAIR)
        mine = [_head_lanes(qp.shape, w, 1) for w in heads]
        qz = [jnp.where(mine[w], qp, jnp.zeros_like(qp)) for w in heads]
        qTs = [qz[w].T for w in heads]
        if use_f:
            qTs = [jnp.concatenate([qTs[w], _bias_rows(w, bq)], axis=0) for w in heads]
        doz = [jnp.where(mine[w], dof, 0.0).astype(BF16) for w in heads]
        doT = [doz[w].T for w in heads]
        delta = [jnp.sum(jnp.where(mine[w], prod, 0.0).T, axis=0, keepdims=True) for w in heads]
        shift = [(lse_ref[w] - r_ref[w]) if use_f else lse_ref[w] for w in heads]

        def blk(j, carry, masked):
            off = pl.multiple_of(j * bk, bk)
            kb = k_ref[pl.ds(off, bk), :]
            kTb = kb.T
            if use_f:
                kb = jnp.concatenate([kb, b_ref[pl.ds(off, bk), :]], axis=1)
            vb = v_ref[pl.ds(off, bk), :]
            sT = [_dot(kb, qTs[w]) * LOG2E for w in heads]
            if masked:
                valid = _valid_t(i, j, bq, bk, False)
                sT = [jnp.where(valid, s, NEG_BIG) for s in sT]
            p = [jnp.exp2(sT[w] - shift[w]) for w in heads]
            dp = [_dot(vb, doT[w]) for w in heads]
            ds = [p[w] * (dp[w] - delta[w]) for w in heads]
            dsb = [d.astype(BF16) for d in ds]
            dvs = [_dot(p[w].astype(BF16), doz[w]) for w in heads]
            dks = [_dot(dsb[w], qz[w]) for w in heads]
            dv_acc[pl.ds(off, bk), :] += dvs[0] + dvs[1]
            dk_acc[pl.ds(off, bk), :] += dks[0] + dks[1]
            dr = [carry[w][1] for w in heads]
            if use_f:
                dr = [dr[w] + jnp.sum(ds[w], axis=0, keepdims=True) for w in heads]
                lane = lax.broadcasted_iota(jnp.int32, (bk, LANE), 1)
                cols = [jnp.where(lane == w, jnp.sum(ds[w], axis=1, keepdims=True), 0.0) for w in heads]
                db_ref[pl.ds(off, bk), :] += cols[0] + cols[1]
            dq = [carry[w][0] + _dot(kTb, dsb[w]) for w in heads]
            return tuple((dq[w], dr[w]) for w in heads)

        carry = tuple((jnp.zeros((LANE, bq), F32), jnp.zeros((1, bq), F32)) for _ in heads)
        if causal:
            start = first_ref[pl.program_id(0) // (P // first.shape[0]), i]
            carry = lax.fori_loop(start, i, lambda j, c: blk(j, c, False), carry)
            carry = blk(i, carry, True)
        else:
            carry = lax.fori_loop(0, nkb, lambda j, c: blk(j, c, False), carry)
        if use_f:
            for w in heads:
                dr_ref[w] = carry[w][1]
        dq_ref[...] = (_merge_pair([carry[w][0] for w in heads]) * scale).astype(BF16)

        @pl.when(i == nq - 1)
        def _():
            dk_ref[...] = dk_acc[...].astype(BF16)
            dv_ref[...] = dv_acc[...].astype(BF16)

        if use_f:
            @pl.when(i == nq - 1)
            def _():
                def chunk(cidx, carry):
                    off = pl.multiple_of(cidx * LANE, LANE)
                    t = db_ref[pl.ds(off, LANE), :].T
                    for w in range(PAIR):
                        dkey_ref[w, :, pl.ds(off, LANE)] = t[w:w + 1, :]
                    return carry

                lax.fori_loop(0, Sk // LANE, chunk, 0)

    qblk = pl.BlockSpec((bq, LANE), lambda p, i: (i, p))
    kres = pl.BlockSpec((Sk, LANE), lambda p, i: (0, p))
    stat = pl.BlockSpec((PAIR, 1, bq), lambda p, i: (p, 0, i))
    in_specs = [qblk, kres, kres, qblk, stat, qblk]
    args = [q2, k2, v2, o2, lse, do2]
    out_specs = [qblk, kres, kres]
    out_shape = [jax.ShapeDtypeStruct((S, C), BF16), jax.ShapeDtypeStruct((Sk, C), BF16),
                 jax.ShapeDtypeStruct((Sk, C), BF16)]
    scratch = [pltpu.VMEM((Sk, LANE), F32), pltpu.VMEM((Sk, LANE), F32)]
    if use_f:
        in_specs = [pl.BlockSpec(memory_space=pltpu.SMEM)] + in_specs + [kres, stat]
        args = [first] + args + [bias, r]
        out_specs += [stat, pl.BlockSpec((PAIR, 1, Sk), lambda p, i: (p, 0, 0))]
        out_shape += [jax.ShapeDtypeStruct((PAIR * P, 1, S), F32), jax.ShapeDtypeStruct((PAIR * P, 1, Sk), F32)]
        scratch.append(pltpu.VMEM((Sk, LANE), F32))
    return pl.pallas_call(
        body, name=name, grid=(P, nq), in_specs=in_specs, out_specs=out_specs, out_shape=out_shape,
        scratch_shapes=scratch, compiler_params=_cp(("parallel", "arbitrary"), ATTN_VMEM_LIMIT),
    )(*args)


def _sbp_fwd(q2, k2, v2, name):
    S, C = q2.shape
    bq, bk = _attn_blocks(S, S, SB_BLOCK)
    assert bq == bk
    nq, P = S // bq, C // LANE
    gp = FWD_PAIRS if P % FWD_PAIRS == 0 else 1
    c = min(CUMSUM_CHUNK, bk)

    def body(q_ref, k_ref, v_ref, o_ref, lt_ref, js_ref):
        i = pl.program_id(1)
        after = _tri2(c, lambda s, j: j > s)
        heads = range(PAIR * gp)
        qps = [q_ref[:, LANE * (h // PAIR):LANE * (h // PAIR + 1)] for h in heads]
        qTs = [jnp.where(_head_lanes(qps[h].shape, h % PAIR, 1), qps[h], jnp.zeros_like(qps[h])).T for h in heads]

        def blk(jj, carry, masked):
            j = i - jj
            off = pl.multiple_of(j * bk, bk)
            kbs = [k_ref[pl.ds(off, bk), LANE * g:LANE * (g + 1)] for g in range(gp)]
            vTbs = [v_ref[pl.ds(off, bk), LANE * g:LANE * (g + 1)].T for g in range(gp)]
            logs = [_sb_logs(_dot(kbs[h // PAIR], qTs[h])) for h in heads]
            ls, lk = [t[0] for t in logs], [t[1] for t in logs]
            if masked:
                valid = _valid_t(i, j, bq, bk, True)
                lk = [jnp.where(valid, t, 0.0) for t in lk]
            cs = [_key_cumsum(lk[h], after, True, carry[h][0]) for h in heads]
            wgt = [jnp.exp(ls[h] + cs[h][0]) for h in heads]
            if masked:
                wgt = [jnp.where(valid, t, 0.0) for t in wgt]
            acc = [carry[h][1] + _dot(vTbs[h // PAIR], wgt[h].astype(BF16)) for h in heads]
            return tuple((carry[h][0] + cs[h][1], acc[h]) for h in heads)

        def step(jj, state):
            carry, first = state
            live = jnp.max(functools.reduce(jnp.maximum, [carry[h][0] for h in heads])) >= SB_DEAD
            carry = lax.cond(live, lambda cr: blk(jj, cr, False), lambda cr: cr, carry)
            return carry, jnp.where(live, i - jj, first)

        carry = tuple((jnp.zeros((1, bq), F32), jnp.zeros((LANE, bq), F32)) for _ in heads)
        carry = blk(0, carry, True)
        carry, first = lax.fori_loop(1, i + 1, step, (carry, i))
        js_ref[0] = jnp.full((1, bq), first, jnp.int32)
        for h in heads:
            lt_ref[h] = carry[h][0]
        for g in range(gp):
            o_ref[:, LANE * g:LANE * (g + 1)] = _merge_pair([carry[h][1] for h in range(PAIR * g, PAIR * (g + 1))])

    qblk = pl.BlockSpec((bq, LANE * gp), lambda p, i: (i, p))
    kres = pl.BlockSpec((S, LANE * gp), lambda p, i: (0, p))
    stat = pl.BlockSpec((PAIR * gp, 1, bq), lambda p, i: (p, 0, i))
    return pl.pallas_call(
        body, name=name, grid=(P // gp, nq),
        in_specs=[qblk, kres, kres],
        out_specs=[qblk, stat, pl.BlockSpec((1, 1, bq), lambda p, i: (p, 0, i))],
        out_shape=[jax.ShapeDtypeStruct((S, C), F32), jax.ShapeDtypeStruct((PAIR * P, 1, S), F32),
                   jax.ShapeDtypeStruct((P // gp, 1, S), jnp.int32)],
        compiler_params=_cp(("parallel", "arbitrary"), ATTN_VMEM_LIMIT),
    )(q2, k2, v2)


def _sbp_bwd(q2, k2, v2, lt, first, do2, scale, name):
    S, C = q2.shape
    bq, bk = _attn_blocks(S, S, SB_BLOCK)
    nq, P = S // bq, C // LANE
    c = min(CUMSUM_CHUNK, bk)

    per_group = P // first.shape[0]

    def body(first_ref, q_ref, k_ref, v_ref, lt_ref, do_ref, dq_ref, dk_ref, dv_ref, dk_acc, dv_acc):
        i = pl.program_id(1)

        @pl.when(i == 0)
        def _():
            dk_acc[...] = jnp.zeros_like(dk_acc)
            dv_acc[...] = jnp.zeros_like(dv_acc)

        qp = q_ref[...]
        dof = do_ref[...]
        upto = _tri2(c, lambda s, j: j <= s)
        before = _tri2(c, lambda s, j: j < s)
        heads = range(PAIR)
        mine = [_head_lanes(qp.shape, w, 1) for w in heads]
        qz = [jnp.where(mine[w], qp, jnp.zeros_like(qp)) for w in heads]
        qTs = [qz[w].T for w in heads]
        doz = [jnp.where(mine[w], dof, 0.0).astype(BF16) for w in heads]
        doT = [doz[w].T for w in heads]
        ltot = [lt_ref[w] for w in heads]

        def blk(j, carry, masked):
            off = pl.multiple_of(j * bk, bk)
            kb = k_ref[pl.ds(off, bk), :]
            vb = v_ref[pl.ds(off, bk), :]
            kTb = kb.T
            logs = [_sb_logs(_dot(kb, qTs[w])) for w in heads]
            ls, lk = [t[0] for t in logs], [t[1] for t in logs]
            if masked:
                valid = _valid_t(i, j, bq, bk, True)
                lk = [jnp.where(valid, t, 0.0) for t in lk]
            pin = [_key_cumsum(lk[w], upto, False, carry[w][1] - ltot[w]) for w in heads]
            wgt = [jnp.exp(ls[w] - pin[w][0]) for w in heads]
            if masked:
                wgt = [jnp.where(valid, t, 0.0) for t in wgt]
            g = [_dot(vb, doT[w]) * wgt[w] for w in heads]
            cin = [_key_cumsum(g[w], before, False, carry[w][2]) for w in heads]
            sig = [jnp.exp(t) for t in ls]
            dz = [g[w] * (1.0 - sig[w]) - cin[w][0] * sig[w] for w in heads]
            if masked:
                dz = [jnp.where(valid, t, 0.0) for t in dz]
            dzb = [t.astype(BF16) for t in dz]
            dvs = [_dot(wgt[w].astype(BF16), doz[w]) for w in heads]
            dks = [_dot(dzb[w], qz[w]) for w in heads]
            dv_acc[pl.ds(off, bk), :] += dvs[0] + dvs[1]
            dk_acc[pl.ds(off, bk), :] += dks[0] + dks[1]
            return tuple((carry[w][0] + _dot(kTb, dzb[w]), carry[w][1] + pin[w][1], carry[w][2] + cin[w][1])
                         for w in heads)

        carry = tuple((jnp.zeros((LANE, bq), F32), jnp.zeros((1, bq), F32), jnp.zeros((1, bq), F32)) for _ in heads)
        start = first_ref[pl.program_id(0) // per_group, i]
        carry = lax.fori_loop(start, i, lambda j, cr: blk(j, cr, False), carry)
        carry = blk(i, carry, True)
        dq_ref[...] = (_merge_pair([carry[w][0] for w in heads]) * scale).astype(BF16)

        @pl.when(i == nq - 1)
        def _():
            dk_ref[...] = dk_acc[...].astype(BF16)
            dv_ref[...] = dv_acc[...].astype(BF16)

    qblk = pl.BlockSpec((bq, LANE), lambda p, i: (i, p))
    kres = pl.BlockSpec((S, LANE), lambda p, i: (0, p))
    stat = pl.BlockSpec((PAIR, 1, bq), lambda p, i: (p, 0, i))
    return pl.pallas_call(
        body, name=name, grid=(P, nq),
        in_specs=[pl.BlockSpec(memory_space=pltpu.SMEM), qblk, kres, kres, stat, qblk],
        out_specs=[qblk, kres, kres],
        out_shape=[jax.ShapeDtypeStruct((S, C), BF16)] * 3,
        scratch_shapes=[pltpu.VMEM((S, LANE), F32), pltpu.VMEM((S, LANE), F32)],
        compiler_params=_cp(("parallel", "arbitrary"), ATTN_VMEM_LIMIT),
    )(first, q2, k2, v2, lt, do2)


def _bias_cols(f_cum):
    H, Sk = f_cum.shape
    terms = jnp.stack(_split3(f_cum), axis=-1)
    packed = terms.reshape(H // PAIR, PAIR, Sk, 3).transpose(2, 0, 1, 3).reshape(Sk, H // PAIR, PAIR * 3)
    return jnp.pad(packed, ((0, 0), (0, 0), (0, LANE - PAIR * 3))).reshape(Sk, -1)


def _make_packed_softmax(name, scale, causal, use_f):
    assert _pow2(scale)

    def run_fwd(q16, k16, v16, f_cum):
        q16 = q16 * scale
        if not use_f:
            o, lse = _smp_fwd(q16, k16, v16, None, None, causal, name + "_fwd")
            return o, (q16, k16, v16, o, lse, None, None, None)
        bq, bk = _attn_blocks(q16.shape[0], k16.shape[0], FOX_BLOCK)
        n_heads = f_cum.shape[0]
        knorm = jnp.sqrt(jnp.sum(jnp.square(k16.astype(F32)).reshape(-1, bk, n_heads, HEAD_DIM), axis=3))
        kstat = jnp.concatenate([jnp.max(knorm, axis=1).T.reshape(-1), f_cum[:, bk - 1::bk].reshape(-1)])
        bias, r = _bias_cols(f_cum), (f_cum * LOG2E)[:, None, :]
        o, lse, first = _smp_fwd(q16, k16, v16, bias, r, causal, name + "_fwd", lax.stop_gradient(kstat))
        return o, (q16, k16, v16, o, lse, bias, r, first[:, 0, ::bq])

    def run_bwd(saved, do):
        q16, k16, v16, o, lse, bias, r, first = saved
        outs = _smp_bwd(q16, k16, v16, o, lse, do, bias, r, scale, causal, name + "_bwd", first)
        if use_f:
            return outs[0], outs[1], outs[2], outs[3][:, 0, :] - outs[4][:, 0, :]
        return tuple(outs)

    if use_f:
        @jax.custom_vjp
        def attn(q, k, v, f_cum):
            return run_fwd(q, k, v, f_cum)[0]

        attn.defvjp(run_fwd, run_bwd)
    else:
        @jax.custom_vjp
        def attn(q, k, v):
            return run_fwd(q, k, v, None)[0]

        attn.defvjp(lambda q, k, v: run_fwd(q, k, v, None), run_bwd)
    return attn


def _make_packed_sb(name, scale):
    assert _pow2(scale)

    def run_fwd(q16, k16, v16):
        q16 = q16 * scale
        o, lt, first = _sbp_fwd(q16, k16, v16, name + "_fwd")
        bq, _ = _attn_blocks(q16.shape[0], q16.shape[0], SB_BLOCK)
        return o, (q16, k16, v16, lt, first[:, 0, ::bq])

    def run_bwd(saved, do):
        q16, k16, v16, lt, first = saved
        return tuple(_sbp_bwd(q16, k16, v16, lt, first, do, scale, name + "_bwd"))

    @jax.custom_vjp
    def attn(q, k, v):
        return run_fwd(q, k, v)[0]

    attn.defvjp(run_fwd, run_bwd)
    return attn


def _round_bf16(x):
    return lax.reduce_precision(x, exponent_bits=8, mantissa_bits=7)


def _split3(x):
    hi = _round_bf16(x)
    mid = _round_bf16(x - hi)
    lo = _round_bf16(x - hi - mid)
    return hi.astype(BF16), mid.astype(BF16), lo.astype(BF16)


def _pow2(x):
    m, _ = math.frexp(x)
    return m == 0.5


def _pad_last(x, n):
    return jnp.pad(x, [(0, 0)] * (x.ndim - 1) + [(0, n - x.shape[-1])])


def _layouts(q, k, scale):
    qh = _pad_last(jnp.transpose(q * scale if _pow2(scale) else q, (1, 0, 2)).astype(BF16), LANE)
    return qh, _pad_last(jnp.transpose(k, (1, 0, 2)).astype(BF16), LANE)


def _make_softmax_attn(name, scale, causal, d):
    pre = _pow2(scale)
    cmul = LOG2E if pre else scale * LOG2E
    gscale = 1.0 if pre else scale

    def run_fwd(q, k, v):
        qn, kn = _layouts(q, k, scale)
        vn = jnp.transpose(v, (1, 0, 2)).astype(BF16)
        oT, lse = _sm_fwd_t(qn, kn, jnp.transpose(vn, (0, 2, 1)), cmul, causal, name + "_fwd")
        return jnp.transpose(oT, (2, 0, 1)), (qn, kn, vn, oT, lse)

    def run_bwd(saved, dout):
        qn, kn, vn, oT, lse = saved
        doT = jnp.transpose(dout, (1, 2, 0))
        do = jnp.transpose(dout, (1, 0, 2)).astype(BF16)
        dq, dk, dv = _sm_bwd_t(qn, kn, vn, oT, lse, doT, do, cmul, gscale, causal, name + "_bwd")
        dq = jnp.transpose(dq[:, :, :d], (1, 0, 2))
        if pre:
            dq = dq * scale
        return dq, jnp.transpose(dk[:, :, :d], (1, 0, 2)), jnp.transpose(dv, (1, 0, 2))

    @jax.custom_vjp
    def attn(q, k, v):
        return run_fwd(q, k, v)[0]

    attn.defvjp(run_fwd, run_bwd)
    return attn


def _rope(x, positions):
    half = x.shape[-1] // 2
    inv_freq = ROPE_THETA ** (-jnp.arange(half, dtype=F32) / half)
    ang = positions.astype(F32)[:, None] * inv_freq[None, :]
    ang = ang.reshape((ang.shape[0],) + (1,) * (x.ndim - 2) + (half,))
    cos, sin = jnp.cos(ang), jnp.sin(ang)
    x1, x2 = x[..., :half], x[..., half:]
    return jnp.concatenate([x1 * cos - x2 * sin, x1 * sin + x2 * cos], axis=-1)


def _permute_cols(w):
    parts = [w[..., _ORIG_OFF[idx]:_ORIG_OFF[idx] + SPLIT_SIZES[idx]] for _, idx in _PERM]
    pad = jnp.zeros(w.shape[:-1] + (PROJ_COLS - IN_COLS,), w.dtype)
    return jnp.concatenate(parts + [pad], axis=-1)


def _unpermute_cols(w):
    start, parts = 0, [None] * len(SPLIT_SIZES)
    for _, idx in _PERM:
        parts[idx] = w[..., start:start + SPLIT_SIZES[idx]]
        start += SPLIT_SIZES[idx]
    return jnp.concatenate(parts, axis=-1)


_BF16_PIECES = ("fq", "fk", "fv", "sq", "sk", "sv", "mq")


def _make_ln_proj(name, has_res):
    def split(proj32, proj16):
        out, off = [], 0
        for n, idx in _PERM:
            src = proj16 if n in _BF16_PIECES else proj32
            out.append(src[:, off:off + SPLIT_SIZES[idx]])
            off += SPLIT_SIZES[idx]
        return tuple(out)

    def run_fwd(x, res, g, b, w):
        h, h16, hT16 = _ln_fwd_call(x, res, g, b, name + "_ln_fwd", also16=True)
        w16 = w.astype(BF16)
        proj32, proj16 = _matmul(h16, w16, "nn", name + "_fwd", also16=True)
        return (h, split(proj32, proj16)), (x, res, g, hT16, w16)

    def run_bwd(saved, cts):
        x, res, g, hT16, w16 = saved
        dh, dpieces = cts
        pad = jnp.zeros((x.shape[0], PROJ_COLS - IN_COLS), BF16)
        dy16 = jnp.concatenate([c.astype(BF16) for c in dpieces] + [pad], axis=1)
        da = _matmul(dy16, w16, "nt", name + "_dx")
        dw = _matmul(hT16, dy16, "nn", name + "_dw")
        outs = _ln_bwd_call(dh, x, res, g, name + "_ln_bwd", dy2=da)
        if has_res:
            dx, dr, dg, db = outs
            return dx, dr, dg.reshape(-1), db.reshape(-1), dw
        dx, dg, db = outs
        return dx, dg.reshape(-1), db.reshape(-1), dw

    if has_res:
        @jax.custom_vjp
        def op(x, res, g, b, w):
            return run_fwd(x, res, g, b, w)[0]

        op.defvjp(run_fwd, run_bwd)
    else:
        @jax.custom_vjp
        def op(x, g, b, w):
            return run_fwd(x, None, g, b, w)[0]

        op.defvjp(lambda x, g, b, w: run_fwd(x, None, g, b, w), run_bwd)

    def call(*args):
        h, pieces = op(*args)
        return h, {n: part for (n, _), part in zip(_PERM, pieces)}

    return call


def _trunk_loss(wts, x2d, mem2d, target2d):
    s = x2d.shape[0]
    positions = jnp.arange(s)
    head_scale = HEAD_DIM ** -0.5
    mla_scale = (MLA_NOPE + MLA_ROPE) ** -0.5

    mem_n = _make_ln("ln_mem", False)(mem2d, wts["mem_ln_g"], wts["mem_ln_b"])
    h, y = None, x2d
    for l in range(DEPTH):
        tag = f"l{l}_"
        w_p = wts["w_in"][l]
        if l == 0:
            h, p = _make_ln_proj(tag + "proj", False)(y, wts["ln_in_g"], wts["ln_in_b"], w_p)
        else:
            h, p = _make_ln_proj(tag + "proj", True)(y, h, wts["ln_g"][l - 1], wts["ln_b"][l - 1], w_p)

        log_f = jax.nn.log_sigmoid(p["f_logit"] + wts["b_forget"][l])
        f_cum = jnp.cumsum(log_f, axis=0).T
        out_fox = _make_packed_softmax(tag + "fox", head_scale, True, True)(p["fq"], p["fk"], p["fv"], f_cum)

        out_sb = _make_packed_sb(tag + "sb", head_scale)(p["sq"], p["sk"], p["sv"])

        cqn = _make_rms(tag + "rms_q")(p["c_q"], wts["mla_q_norm_g"][l])
        q_mla = _make_mm(tag + "q_up")(cqn, wts["w_mla_q_up"][l]).reshape(s, N_HEADS, MLA_NOPE + MLA_ROPE)
        ckvn = _make_rms(tag + "rms_kv")(p["c_kv"], wts["mla_kv_norm_g"][l])
        kv_mla = _make_mm(tag + "kv_up")(ckvn, wts["w_mla_kv_up"][l]).reshape(s, N_HEADS, MLA_NOPE + MLA_V)
        q_full = jnp.concatenate([q_mla[..., :MLA_NOPE], _rope(q_mla[..., MLA_NOPE:], positions)], axis=-1)
        k_rope = jnp.broadcast_to(_rope(p["k_rot"], positions)[:, None, :], (s, N_HEADS, MLA_ROPE))
        k_full = jnp.concatenate([kv_mla[..., :MLA_NOPE], k_rope], axis=-1)
        out_mla = _make_softmax_attn(tag + "mla", mla_scale, True, MLA_NOPE + MLA_ROPE)(
            q_full, k_full, kv_mla[..., MLA_NOPE:]).reshape(s, GROUP_W)

        mkv = _make_mm(tag + "mem_kv")(mem_n, wts["w_mem_kv"][l])
        out_mem = _make_packed_softmax(tag + "mem", head_scale, False, False)(
            p["mq"], mkv[:, :GROUP_W].astype(BF16), mkv[:, GROUP_W:].astype(BF16))

        mixed = jnp.concatenate([out_fox, out_sb, out_mla, out_mem], axis=-1)
        y = _make_gate_out(tag + "out")(mixed, p["gate"], wts["w_out"][l])

    h = _make_ln(f"l{DEPTH - 1}_ln", True)(y, h, wts["ln_g"][DEPTH - 1], wts["ln_b"][DEPTH - 1])
    return _loss_op(h, target2d)


def _mesh_pos():
    x, y, c = (lax.axis_index(a) for a in MESH_AXES)
    return x, y, c, 4 * x + 2 * y + c


def _peer(x, y, c, mask):
    return (x ^ ((mask >> 2) & 1), y ^ ((mask >> 1) & 1), c ^ (mask & 1))


_ANY = pl.BlockSpec(memory_space=pl.ANY)


def _all_gather(row_shards, stack_shards):
    n_row, n_all = len(row_shards), len(row_shards) + len(stack_shards)
    shards = list(row_shards) + list(stack_shards)

    def body(*refs):
        ins, outs = refs[:n_all], refs[n_all:2 * n_all]
        send_sems, recv_sems, local_sems = refs[2 * n_all:]
        x, y, c, me = _mesh_pos()

        def window(t, slot):
            if t < n_row:
                rows = shards[t].shape[1]
                return outs[t].at[:, pl.ds(slot * rows, rows), :]
            return outs[t].at[slot]

        local = [pltpu.make_async_copy(ins[t], window(t, me), local_sems.at[t]) for t in range(n_all)]
        for cp in local:
            cp.start()
        sends = []
        for mask in range(1, N_DEV):
            for t in range(n_all):
                cp = pltpu.make_async_remote_copy(
                    src_ref=ins[t], dst_ref=window(t, me), send_sem=send_sems.at[t, mask - 1],
                    recv_sem=recv_sems.at[t, mask - 1], device_id=_peer(x, y, c, mask),
                    device_id_type=pl.DeviceIdType.MESH)
                cp.start()
                sends.append(cp)
        for mask in range(1, N_DEV):
            for t in range(n_all):
                pltpu.make_async_remote_copy(
                    src_ref=ins[t], dst_ref=window(t, me ^ mask), send_sem=send_sems.at[t, mask - 1],
                    recv_sem=recv_sems.at[t, mask - 1], device_id=_peer(x, y, c, mask),
                    device_id_type=pl.DeviceIdType.MESH).wait_recv()
        for cp in sends:
            cp.wait_send()
        for cp in local:
            cp.wait()

    out_shape = [jax.ShapeDtypeStruct((a.shape[0], N_DEV * a.shape[1], a.shape[2]), a.dtype) for a in row_shards]
    out_shape += [jax.ShapeDtypeStruct((N_DEV,) + a.shape, a.dtype) for a in stack_shards]
    return pl.pallas_call(
        body, name="all_gather_weights", in_specs=[_ANY] * n_all, out_specs=[_ANY] * n_all, out_shape=out_shape,
        scratch_shapes=[pltpu.SemaphoreType.DMA((n_all, N_DEV - 1)), pltpu.SemaphoreType.DMA((n_all, N_DEV - 1)),
                        pltpu.SemaphoreType.DMA((n_all,))],
    )(*shards)


def _reduce_scatter(row_full, stack_full, bcast):
    n_row, n_stack = len(row_full), len(stack_full)
    n_all = n_row + n_stack + len(bcast)
    fulls = list(row_full) + list(stack_full) + list(bcast)

    def body(*refs):
        ins, outs = refs[:n_all], refs[n_all:2 * n_all]
        send_sems, recv_sems, local_sems = refs[2 * n_all:]
        x, y, c, me = _mesh_pos()

        def part(t, slot):
            if t < n_row:
                rows = fulls[t].shape[1] // N_DEV
                return ins[t].at[:, pl.ds(slot * rows, rows), :]
            if t < n_row + n_stack:
                return ins[t].at[slot]
            return ins[t]

        local = [pltpu.make_async_copy(part(t, me), outs[t].at[me], local_sems.at[t]) for t in range(n_all)]
        for cp in local:
            cp.start()
        sends = []
        for mask in range(1, N_DEV):
            for t in range(n_all):
                cp = pltpu.make_async_remote_copy(
                    src_ref=part(t, me ^ mask), dst_ref=outs[t].at[me], send_sem=send_sems.at[t, mask - 1],
                    recv_sem=recv_sems.at[t, mask - 1], device_id=_peer(x, y, c, mask),
                    device_id_type=pl.DeviceIdType.MESH)
                cp.start()
                sends.append(cp)
        for mask in range(1, N_DEV):
            for t in range(n_all):
                pltpu.make_async_remote_copy(
                    src_ref=part(t, me), dst_ref=outs[t].at[me ^ mask], send_sem=send_sems.at[t, mask - 1],
                    recv_sem=recv_sems.at[t, mask - 1], device_id=_peer(x, y, c, mask),
                    device_id_type=pl.DeviceIdType.MESH).wait_recv()
        for cp in sends:
            cp.wait_send()
        for cp in local:
            cp.wait()

    out_shape = [jax.ShapeDtypeStruct((N_DEV, a.shape[0], a.shape[1] // N_DEV, a.shape[2]), a.dtype) for a in row_full]
    out_shape += [jax.ShapeDtypeStruct(a.shape, a.dtype) for a in stack_full]
    out_shape += [jax.ShapeDtypeStruct((N_DEV,) + a.shape, a.dtype) for a in bcast]
    return pl.pallas_call(
        body, name="reduce_scatter_grads", in_specs=[_ANY] * n_all, out_specs=[_ANY] * n_all, out_shape=out_shape,
        scratch_shapes=[pltpu.SemaphoreType.DMA((n_all, N_DEV - 1)), pltpu.SemaphoreType.DMA((n_all, N_DEV - 1)),
                        pltpu.SemaphoreType.DMA((n_all,))],
    )(*fulls)


def _adamw(slots, w, m, v, name):
    shape = w.shape
    cols = shape[-1]
    rows = math.prod(shape[:-1])
    tr = _pick(rows, (64, 32, 16, 8))
    c1 = 1.0 - ADAM_B1 ** ADAM_STEP
    c2 = 1.0 - ADAM_B2 ** ADAM_STEP

    def body(s_ref, w_ref, m_ref, v_ref, g_ref, d_ref, nm_ref, nv_ref):
        g = s_ref[0].astype(F32)
        for k in range(1, N_DEV):
            g = g + s_ref[k].astype(F32)
        nm = ADAM_B1 * m_ref[...] + (1.0 - ADAM_B1) * g
        nv = ADAM_B2 * v_ref[...] + (1.0 - ADAM_B2) * (g * g)
        g_ref[...] = g
        nm_ref[...] = nm
        nv_ref[...] = nv
        d_ref[...] = -ADAM_LR * ((nm / c1) / (jnp.sqrt(nv / c2) + ADAM_EPS) + ADAM_WD * w_ref[...])

    row = pl.BlockSpec((tr, cols), lambda i: (i, 0))
    out = jax.ShapeDtypeStruct((rows, cols), F32)
    outs = pl.pallas_call(
        body, name=name, grid=(rows // tr,),
        in_specs=[pl.BlockSpec((N_DEV, tr, cols), lambda i: (0, i, 0)), row, row, row],
        out_specs=[row] * 4, out_shape=[out] * 4, compiler_params=_cp(("parallel",)),
    )(slots.reshape(N_DEV, rows, cols), w.reshape(rows, cols), m.reshape(rows, cols), v.reshape(rows, cols))
    return [o.reshape(shape) for o in outs]


_SMALL = ("ln_in_g", "ln_in_b", "mem_ln_g", "mem_ln_b", "b_forget", "mla_q_norm_g", "mla_kv_norm_g", "ln_g", "ln_b")
_ORDER = ("ln_in_g", "ln_in_b", "mem_ln_g", "mem_ln_b", "w_in", "b_forget", "mla_q_norm_g", "w_mla_q_up",
          "mla_kv_norm_g", "w_mla_kv_up", "w_mem_kv", "w_out", "ln_g", "ln_b")


def _pack_small(d):
    flat = jnp.concatenate([d[n].reshape(-1) for n in _SMALL])
    n = flat.shape[0]
    padded = ((n + 8 * LANE - 1) // (8 * LANE)) * (8 * LANE)
    return jnp.pad(flat, (0, padded - n)).reshape(-1, LANE)


def _unpack_small(packed, like):
    flat, out, off = packed.reshape(-1), {}, 0
    for n in _SMALL:
        size = math.prod(like[n].shape)
        out[n] = flat[off:off + size].reshape(like[n].shape)
        off += size
    return out


def _unstack_cols(g):
    n, l, r, c = g.shape
    return g.transpose(1, 2, 0, 3).reshape(l, r, n * c)


def _stack_cols(g):
    l, r, nc = g.shape
    return g.reshape(l, r, N_DEV, nc // N_DEV).transpose(2, 0, 1, 3)


def kernel(x, mem, ln_in_g, ln_in_b, mem_ln_g, mem_ln_b, w_in, b_forget, mla_q_norm_g, w_mla_q_up, mla_kv_norm_g, w_mla_kv_up, w_mem_kv, w_out, ln_g, ln_b, loss_target, m_ln_in_g, m_ln_in_b, m_mem_ln_g, m_mem_ln_b, m_w_in, m_b_forget, m_mla_q_norm_g, m_w_mla_q_up, m_mla_kv_norm_g, m_w_mla_kv_up, m_w_mem_kv, m_w_out, m_ln_g, m_ln_b, v_ln_in_g, v_ln_in_b, v_mem_ln_g, v_mem_ln_b, v_w_in, v_b_forget, v_mla_q_norm_g, v_w_mla_q_up, v_mla_kv_norm_g, v_w_mla_kv_up, v_w_mem_kv, v_w_out, v_ln_g, v_ln_b):
    w_shard = dict(ln_in_g=ln_in_g, ln_in_b=ln_in_b, mem_ln_g=mem_ln_g, mem_ln_b=mem_ln_b, w_in=w_in,
                   b_forget=b_forget, mla_q_norm_g=mla_q_norm_g, w_mla_q_up=w_mla_q_up,
                   mla_kv_norm_g=mla_kv_norm_g, w_mla_kv_up=w_mla_kv_up, w_mem_kv=w_mem_kv, w_out=w_out,
                   ln_g=ln_g, ln_b=ln_b)
    m_shard = dict(ln_in_g=m_ln_in_g, ln_in_b=m_ln_in_b, mem_ln_g=m_mem_ln_g, mem_ln_b=m_mem_ln_b, w_in=m_w_in,
                   b_forget=m_b_forget, mla_q_norm_g=m_mla_q_norm_g, w_mla_q_up=m_w_mla_q_up,
                   mla_kv_norm_g=m_mla_kv_norm_g, w_mla_kv_up=m_w_mla_kv_up, w_mem_kv=m_w_mem_kv, w_out=m_w_out,
                   ln_g=m_ln_g, ln_b=m_ln_b)
    v_shard = dict(ln_in_g=v_ln_in_g, ln_in_b=v_ln_in_b, mem_ln_g=v_mem_ln_g, mem_ln_b=v_mem_ln_b, w_in=v_w_in,
                   b_forget=v_b_forget, mla_q_norm_g=v_mla_q_norm_g, w_mla_q_up=v_w_mla_q_up,
                   mla_kv_norm_g=v_mla_kv_norm_g, w_mla_kv_up=v_w_mla_kv_up, w_mem_kv=v_w_mem_kv, w_out=v_w_out,
                   ln_g=v_ln_g, ln_b=v_ln_b)

    to16 = lambda ws: [a.astype(BF16) for a in ws]
    gathered = _all_gather(to16([_permute_cols(w_in), w_mem_kv, w_out]), to16([w_mla_q_up, w_mla_kv_up]))
    g_in, g_mem, g_out, g_qup, g_kvup = [a.astype(F32) for a in gathered]
    full = dict(w_shard)
    full.update(w_in=g_in, w_mem_kv=g_mem, w_out=g_out, w_mla_q_up=_unstack_cols(g_qup),
                w_mla_kv_up=_unstack_cols(g_kvup))

    loss_local, (grad_w, grad_x) = jax.value_and_grad(_trunk_loss, argnums=(0, 1))(
        full, x[0], mem[0], loss_target[0])

    s_in, s_mem, s_out, s_qup, s_kvup, s_small = _reduce_scatter(
        to16([grad_w["w_in"], grad_w["w_mem_kv"], grad_w["w_out"]]),
        to16([_stack_cols(grad_w["w_mla_q_up"]), _stack_cols(grad_w["w_mla_kv_up"])]),
        [_pack_small(grad_w)])

    res = {}
    for name, slots in (("w_mem_kv", s_mem), ("w_out", s_out), ("w_mla_q_up", s_qup), ("w_mla_kv_up", s_kvup)):
        res[name] = _adamw(slots, w_shard[name], m_shard[name], v_shard[name], "adamw_" + name)
    res["w_in"] = [_unpermute_cols(a) for a in _adamw(
        s_in, _permute_cols(w_in), _permute_cols(m_w_in), _permute_cols(v_w_in), "adamw_w_in")]
    small = _adamw(s_small, _pack_small(w_shard), _pack_small(m_shard), _pack_small(v_shard), "adamw_small")
    small = [_unpack_small(a, w_shard) for a in small]
    for name in _SMALL:
        res[name] = [a[name] for a in small]

    loss = lax.psum(loss_local, MESH_AXES)
    outs = [loss, grad_x[None]]
    for k in range(4):
        outs += [res[name][k] for name in _ORDER]
    return tuple(outs)
```

```python
import functools
import math

import jax
import jax.numpy as jnp
from jax import lax
from jax.experimental import pallas as pl
from jax.experimental.pallas import tpu as pltpu

F32 = jnp.float32
BF16 = jnp.bfloat16

D_MODEL = 1024
DEPTH = 2
GROUP_W = 256
N_HEADS = 4
HEAD_DIM = 64
MLA_Q_RANK = 256
MLA_KV_RANK = 128
MLA_NOPE = 64
MLA_ROPE = 32
MLA_V = 64
ROPE_THETA = 10000.0
LN_EPS = 1e-5
RMS_EPS = 1e-6
DEEPNORM_ALPHA = (2 * DEPTH) ** 0.25
SPLIT_SIZES = (256, 256, 256, 4, 256, 256, 256, 256, 128, 32, 256, 1024)
IN_COLS = sum(SPLIT_SIZES)
_ORIG_OFF = [sum(SPLIT_SIZES[:i]) for i in range(len(SPLIT_SIZES))]
_PERM = (("fq", 0), ("fk", 1), ("fv", 2), ("sq", 4), ("sk", 5), ("sv", 6), ("c_q", 7), ("c_kv", 8),
         ("mq", 10), ("gate", 11), ("k_rot", 9), ("f_logit", 3))
LANE = 128
PROJ_COLS = ((IN_COLS + LANE - 1) // LANE) * LANE

ADAM_LR = 0.001
ADAM_B1 = 0.9
ADAM_B2 = 0.999
ADAM_EPS = 1e-08
ADAM_WD = 0.01
ADAM_STEP = 10

N_DEV = 8
MESH_AXES = ("x", "y", "c")
VMEM_LIMIT = 48 * 1024 * 1024
ATTN_VMEM_LIMIT = 56 * 1024 * 1024
ATTN_BQ = 512
ATTN_BK = 512
CUMSUM_CHUNK = 256
NEG_BIG = -1e30
LOG2E = math.log2(math.e)
MM_TM, MM_TN, MM_TK, MM_TK_NT = 1024, 1664, 1024, 3328

_NT = (((1,), (1,)), ((), ()))
_NN = (((1,), (0,)), ((), ()))


def _cp(sem, vmem=VMEM_LIMIT):
    return pltpu.CompilerParams(dimension_semantics=sem, vmem_limit_bytes=vmem)


def _dot(a, b, dims=_NN):
    return lax.dot_general(a, b, dims, preferred_element_type=F32)


def _pick(n, cands):
    for c in cands:
        if c <= n and n % c == 0:
            return c
    return n


def _tile(n, cap):
    if n <= cap:
        return n
    best = None
    for d in range(LANE, cap + 1, LANE):
        if n % d == 0:
            best = d
    assert best is not None, (n, cap)
    return best


def _matmul(a, b, mode, name, also16=False):
    if mode == "nn":
        (M, K), (K2, N) = a.shape, b.shape
    else:
        (M, K), (N, K2) = a.shape, b.shape
    assert K == K2 and a.dtype == BF16 and b.dtype == BF16, (a.shape, b.shape, mode)
    tm, tn = _tile(M, MM_TM), _tile(N, MM_TN)
    tk = _tile(K, MM_TK if mode == "nn" else MM_TK_NT)
    nk = K // tk
    dims = _NN if mode == "nn" else _NT

    def body(a_ref, b_ref, *rest):
        o_ref, acc_ref = rest[0], rest[-1]
        part = _dot(a_ref[...], b_ref[...], dims)
        if nk == 1:
            o_ref[...] = part
            if also16:
                rest[1][...] = part.astype(BF16)
        else:
            assert not also16
            k = pl.program_id(2)

            @pl.when(k == 0)
            def _():
                acc_ref[...] = part

            @pl.when(k > 0)
            def _():
                acc_ref[...] += part

            @pl.when(k == nk - 1)
            def _():
                o_ref[...] = acc_ref[...]

    a_spec = pl.BlockSpec((tm, tk), lambda j, i, k: (i, k))
    if mode == "nn":
        b_spec = pl.BlockSpec((tk, tn), lambda j, i, k: (k, j))
    else:
        b_spec = pl.BlockSpec((tn, tk), lambda j, i, k: (j, k))
    acc_shape = (tm, tn) if nk > 1 else (8, LANE)
    o_spec = pl.BlockSpec((tm, tn), lambda j, i, k: (i, j))
    outs = pl.pallas_call(
        body, name=name, grid=(N // tn, M // tm, nk),
        in_specs=[a_spec, b_spec],
        out_specs=[o_spec, o_spec] if also16 else o_spec,
        out_shape=[jax.ShapeDtypeStruct((M, N), F32), jax.ShapeDtypeStruct((M, N), BF16)] if also16
        else jax.ShapeDtypeStruct((M, N), F32),
        scratch_shapes=[pltpu.VMEM(acc_shape, F32)],
        compiler_params=_cp(("parallel", "parallel", "arbitrary")),
    )(a, b)
    return outs


def _make_mm(name):
    @jax.custom_vjp
    def mm(a, w):
        return _matmul(a.astype(BF16), w.astype(BF16), "nn", name + "_fwd")

    def fwd(a, w):
        a16, w16 = a.astype(BF16), w.astype(BF16)
        return _matmul(a16, w16, "nn", name + "_fwd"), (a16, w16)

    def bwd(res, dy):
        a16, w16 = res
        dy16 = dy.astype(BF16)
        da = _matmul(dy16, w16, "nt", name + "_dx")
        dw = _matmul(a16.T, dy16, "nn", name + "_dw")
        return da, dw

    mm.defvjp(fwd, bwd)
    return mm


def _row_tile(rows):
    return _pick(rows, (512, 256, 128, 64, 32, 16, 8))


def _ln_stats(u):
    mu = jnp.mean(u, axis=-1, keepdims=True)
    d = u - mu
    var = jnp.mean(d * d, axis=-1, keepdims=True)
    return d, lax.rsqrt(var + LN_EPS)


def _ln_fwd_call(x, res, g, b, name, also16=False):
    rows, dm = x.shape
    tr = _row_tile(rows)
    has_res = res is not None
    n_in = 2 if has_res else 1

    def body(*refs):
        if has_res:
            u = DEEPNORM_ALPHA * refs[1][...] + refs[0][...]
        else:
            u = refs[0][...]
        g_ref, b_ref = refs[n_in], refs[n_in + 1]
        d, rstd = _ln_stats(u)
        y = d * rstd * g_ref[...] + b_ref[...]
        refs[n_in + 2][...] = y
        if also16:
            y16 = y.astype(BF16)
            refs[n_in + 3][...] = y16
            refs[n_in + 4][...] = y16.T

    row = pl.BlockSpec((tr, dm), lambda i: (i, 0))
    vec = pl.BlockSpec((1, dm), lambda i: (0, 0))
    args = (x, res) if has_res else (x,)
    out_specs, out_shape = [row], [jax.ShapeDtypeStruct((rows, dm), F32)]
    if also16:
        out_specs += [row, pl.BlockSpec((dm, tr), lambda i: (0, i))]
        out_shape += [jax.ShapeDtypeStruct((rows, dm), BF16), jax.ShapeDtypeStruct((dm, rows), BF16)]
    outs = pl.pallas_call(
        body, name=name, grid=(rows // tr,),
        in_specs=[row] * n_in + [vec, vec], out_specs=out_specs, out_shape=out_shape,
        compiler_params=_cp(("parallel",)),
    )(*args, g.reshape(1, dm), b.reshape(1, dm))
    return outs if also16 else outs[0]


def _ln_bwd_call(dy, x, res, g, name, dy2=None):
    rows, dm = x.shape
    tr = _row_tile(rows)
    has_res = res is not None
    two = dy2 is not None

    def body(*refs):
        dy_ref, refs = refs[0], refs[1:]
        if two:
            dy2_ref, refs = refs[0], refs[1:]
        if has_res:
            x_ref, r_ref, g_ref, dx_ref, dr_ref, dg_ref, db_ref = refs
            u = DEEPNORM_ALPHA * r_ref[...] + x_ref[...]
        else:
            x_ref, g_ref, dx_ref, dg_ref, db_ref = refs
            u = x_ref[...]
        i = pl.program_id(0)
        d, rstd = _ln_stats(u)
        xhat = d * rstd
        dyv = dy_ref[...] + dy2_ref[...] if two else dy_ref[...]
        dxh = dyv * g_ref[...]
        m1 = jnp.mean(dxh, axis=-1, keepdims=True)
        m2 = jnp.mean(dxh * xhat, axis=-1, keepdims=True)
        du = rstd * (dxh - m1 - xhat * m2)
        dx_ref[...] = du
        if has_res:
            dr_ref[...] = DEEPNORM_ALPHA * du
        pg = jnp.sum(dyv * xhat, axis=0, keepdims=True)
        pb = jnp.sum(dyv, axis=0, keepdims=True)

        @pl.when(i == 0)
        def _():
            dg_ref[...] = pg
            db_ref[...] = pb

        @pl.when(i > 0)
        def _():
            dg_ref[...] += pg
            db_ref[...] += pb

    row = pl.BlockSpec((tr, dm), lambda i: (i, 0))
    vec = pl.BlockSpec((1, dm), lambda i: (0, 0))
    big = jax.ShapeDtypeStruct((rows, dm), F32)
    small = jax.ShapeDtypeStruct((1, dm), F32)
    args = ((dy, dy2) if two else (dy,)) + ((x, res) if has_res else (x,))
    n_big = 2 if has_res else 1
    outs = pl.pallas_call(
        body, name=name, grid=(rows // tr,),
        in_specs=[row] * len(args) + [vec],
        out_specs=[row] * n_big + [vec, vec],
        out_shape=[big] * n_big + [small, small],
        compiler_params=_cp(("arbitrary",)),
    )(*args, g.reshape(1, dm))
    return outs


def _make_ln(name, has_res):
    if has_res:
        @jax.custom_vjp
        def ln(x, res, g, b):
            return _ln_fwd_call(x, res, g, b, name + "_fwd")

        def fwd(x, res, g, b):
            return ln(x, res, g, b), (x, res, g)

        def bwd(saved, dy):
            x, res, g = saved
            dx, dr, dg, db = _ln_bwd_call(dy, x, res, g, name + "_bwd")
            return dx, dr, dg.reshape(-1), db.reshape(-1)
    else:
        @jax.custom_vjp
        def ln(x, g, b):
            return _ln_fwd_call(x, None, g, b, name + "_fwd")

        def fwd(x, g, b):
            return ln(x, g, b), (x, g)

        def bwd(saved, dy):
            x, g = saved
            dx, dg, db = _ln_bwd_call(dy, x, None, g, name + "_bwd")
            return dx, dg.reshape(-1), db.reshape(-1)

    ln.defvjp(fwd, bwd)
    return ln


def _rms_fwd_call(x, g, name):
    rows, dm = x.shape
    tr = _row_tile(rows)

    def body(x_ref, g_ref, o_ref):
        xv = x_ref[...]
        rstd = lax.rsqrt(jnp.mean(xv * xv, axis=-1, keepdims=True) + RMS_EPS)
        o_ref[...] = xv * rstd * g_ref[...]

    row = pl.BlockSpec((tr, dm), lambda i: (i, 0))
    vec = pl.BlockSpec((1, dm), lambda i: (0, 0))
    return pl.pallas_call(
        body, name=name, grid=(rows // tr,), in_specs=[row, vec], out_specs=row,
        out_shape=jax.ShapeDtypeStruct((rows, dm), F32), compiler_params=_cp(("parallel",)),
    )(x, g.reshape(1, dm))


def _rms_bwd_call(dy, x, g, name):
    rows, dm = x.shape
    tr = _row_tile(rows)

    def body(dy_ref, x_ref, g_ref, dx_ref, dg_ref):
        i = pl.program_id(0)
        xv = x_ref[...]
        dyv = dy_ref[...]
        rstd = lax.rsqrt(jnp.mean(xv * xv, axis=-1, keepdims=True) + RMS_EPS)
        xhat = xv * rstd
        dxh = dyv * g_ref[...]
        m2 = jnp.mean(dxh * xhat, axis=-1, keepdims=True)
        dx_ref[...] = rstd * (dxh - xhat * m2)
        pg = jnp.sum(dyv * xhat, axis=0, keepdims=True)

        @pl.when(i == 0)
        def _():
            dg_ref[...] = pg

        @pl.when(i > 0)
        def _():
            dg_ref[...] += pg

    row = pl.BlockSpec((tr, dm), lambda i: (i, 0))
    vec = pl.BlockSpec((1, dm), lambda i: (0, 0))
    return pl.pallas_call(
        body, name=name, grid=(rows // tr,), in_specs=[row, row, vec], out_specs=[row, vec],
        out_shape=[jax.ShapeDtypeStruct((rows, dm), F32), jax.ShapeDtypeStruct((1, dm), F32)],
        compiler_params=_cp(("arbitrary",)),
    )(dy, x, g.reshape(1, dm))


def _make_rms(name):
    @jax.custom_vjp
    def rms(x, g):
        return _rms_fwd_call(x, g, name + "_fwd")

    def fwd(x, g):
        return rms(x, g), (x, g)

    def bwd(saved, dy):
        x, g = saved
        dx, dg = _rms_bwd_call(dy, x, g, name + "_bwd")
        return dx, dg.reshape(-1)

    rms.defvjp(fwd, bwd)
    return rms


def _sigmoid(x):
    return 1.0 / (1.0 + jnp.exp(-x))


def _gate_fwd_call(mixed, gate, name):
    rows, dm = mixed.shape
    tr = _row_tile(rows)

    def body(m_ref, g_ref, o_ref, oT_ref):
        gv = g_ref[...]
        y16 = (m_ref[...] * (gv * _sigmoid(gv))).astype(BF16)
        o_ref[...] = y16
        oT_ref[...] = y16.T

    row = pl.BlockSpec((tr, dm), lambda i: (i, 0))
    return pl.pallas_call(
        body, name=name, grid=(rows // tr,), in_specs=[row, row],
        out_specs=[row, pl.BlockSpec((dm, tr), lambda i: (0, i))],
        out_shape=[jax.ShapeDtypeStruct((rows, dm), BF16), jax.ShapeDtypeStruct((dm, rows), BF16)],
        compiler_params=_cp(("parallel",)),
    )(mixed, gate)


def _gate_bwd_call(dy, mixed, gate, name):
    rows, dm = mixed.shape
    tr = _row_tile(rows)

    def body(dy_ref, m_ref, g_ref, dm_ref, dg_ref):
        gv = g_ref[...]
        dyv = dy_ref[...]
        sg = _sigmoid(gv)
        dm_ref[...] = dyv * (gv * sg)
        dg_ref[...] = dyv * m_ref[...] * (sg * (1.0 + gv * (1.0 - sg)))

    row = pl.BlockSpec((tr, dm), lambda i: (i, 0))
    out = jax.ShapeDtypeStruct((rows, dm), F32)
    return pl.pallas_call(
        body, name=name, grid=(rows // tr,), in_specs=[row, row, row], out_specs=[row, row],
        out_shape=[out, out], compiler_params=_cp(("parallel",)),
    )(dy, mixed, gate)


def _make_gate_out(name):
    def run_fwd(mixed, gate, w):
        g16, gT16 = _gate_fwd_call(mixed, gate, name + "_gate_fwd")
        w16 = w.astype(BF16)
        return _matmul(g16, w16, "nn", name + "_fwd"), (mixed, gate, gT16, w16)

    def run_bwd(saved, dy):
        mixed, gate, gT16, w16 = saved
        dy16 = dy.astype(BF16)
        dgated = _matmul(dy16, w16, "nt", name + "_dx")
        dmix, dgate = _gate_bwd_call(dgated, mixed, gate, name + "_gate_bwd")
        return dmix, dgate, _matmul(gT16, dy16, "nn", name + "_dw")

    @jax.custom_vjp
    def gate_out(mixed, gate, w):
        return run_fwd(mixed, gate, w)[0]

    gate_out.defvjp(run_fwd, run_bwd)
    return gate_out


def _loss_call(y, t, name):
    rows, dm = y.shape
    tr = _row_tile(rows)

    def body(y_ref, t_ref, l_ref, d_ref):
        i = pl.program_id(0)
        e = y_ref[...] - t_ref[...]
        d_ref[...] = e * (1.0 / dm)
        part = 0.5 * jnp.sum(jnp.mean(e * e, axis=-1, keepdims=True), axis=0, keepdims=True)

        @pl.when(i == 0)
        def _():
            l_ref[...] = part

        @pl.when(i > 0)
        def _():
            l_ref[...] += part

    row = pl.BlockSpec((tr, dm), lambda i: (i, 0))
    one = pl.BlockSpec((1, 1), lambda i: (0, 0))
    return pl.pallas_call(
        body, name=name, grid=(rows // tr,), in_specs=[row, row], out_specs=[one, row],
        out_shape=[jax.ShapeDtypeStruct((1, 1), F32), jax.ShapeDtypeStruct((rows, dm), F32)],
        compiler_params=_cp(("arbitrary",)),
    )(y, t)


@jax.custom_vjp
def _loss_op(y, t):
    return _loss_call(y, t, "loss_head")[0][0, 0]


def _loss_fwd(y, t):
    l, d = _loss_call(y, t, "loss_head")
    return l[0, 0], d


def _loss_bwd(d, ct):
    return ct * d, jnp.zeros_like(d)


_loss_op.defvjp(_loss_fwd, _loss_bwd)


def _attn_blocks(S, Sk, cap=None):
    bq, bk = min(cap or ATTN_BQ, S), min(cap or ATTN_BK, Sk)
    assert S % bq == 0 and Sk % bk == 0
    return bq, bk


def _valid_t(i, j, bq, bk, strict):
    key = j * bk + lax.broadcasted_iota(jnp.int32, (bk, bq), 0)
    qry = i * bq + lax.broadcasted_iota(jnp.int32, (bk, bq), 1)
    return (key < qry) if strict else (key <= qry)


def _sm_fwd_t(qn, k, vT, cmul, causal, name):
    H, S, DK = qn.shape
    Sk, dv = k.shape[1], vT.shape[1]
    bq, bk = _attn_blocks(S, Sk)
    nq, nkb = S // bq, Sk // bk
    hb = PAIR * FWD_PAIRS if H % (PAIR * FWD_PAIRS) == 0 else 1
    heads = range(hb)
    if causal:
        assert S == Sk and bq == bk

    def body(qn_ref, k_ref, vT_ref, oT_ref, lse_ref):
        i = pl.program_id(1)
        qTs = [qn_ref[w].T for w in heads]

        def blk(j, carry, masked):
            off = pl.multiple_of(j * bk, bk)
            sT = [_dot(k_ref[w, pl.ds(off, bk), :], qTs[w]) * cmul for w in heads]
            if masked:
                valid = _valid_t(i, j, bq, bk, False)
                sT = [jnp.where(valid, s, NEG_BIG) for s in sT]
            m_new = [jnp.maximum(carry[w][0], jnp.max(sT[w], axis=0, keepdims=True)) for w in heads]
            p = [jnp.exp2(sT[w] - m_new[w]) for w in heads]
            a = [jnp.exp2(carry[w][0] - m_new[w]) for w in heads]
            l = [a[w] * carry[w][1] + jnp.sum(p[w], axis=0, keepdims=True) for w in heads]
            acc = [a[w] * carry[w][2] + _dot(vT_ref[w, :, pl.ds(off, bk)], p[w].astype(BF16)) for w in heads]
            return tuple((m_new[w], l[w], acc[w]) for w in heads)

        carry = tuple((jnp.full((1, bq), NEG_BIG, F32), jnp.zeros((1, bq), F32), jnp.zeros((dv, bq), F32))
                      for _ in heads)
        if causal:
            carry = lax.fori_loop(0, i, lambda j, c: blk(j, c, False), carry)
            carry = blk(i, carry, True)
        else:
            carry = lax.fori_loop(0, nkb, lambda j, c: blk(j, c, False), carry)
        for w in heads:
            oT_ref[w] = carry[w][2] / carry[w][1]
            lse_ref[w] = carry[w][0] + jnp.log2(carry[w][1])

    qcol = lambda d: pl.BlockSpec((hb, d, bq), lambda h, i: (h, 0, i))
    return pl.pallas_call(
        body, name=name, grid=(H // hb, nq),
        in_specs=[pl.BlockSpec((hb, bq, DK), lambda h, i: (h, i, 0)), pl.BlockSpec((hb, Sk, DK), lambda h, i: (h, 0, 0)),
                  pl.BlockSpec((hb, dv, Sk), lambda h, i: (h, 0, 0))],
        out_specs=[qcol(dv), qcol(1)],
        out_shape=[jax.ShapeDtypeStruct((H, dv, S), F32), jax.ShapeDtypeStruct((H, 1, S), F32)],
        compiler_params=_cp(("parallel", "arbitrary"), ATTN_VMEM_LIMIT),
    )(qn, k, vT)


def _sm_bwd_t(qn, k, v, oT, lse, doT, do, cmul, gscale, causal, name):
    H, S, DK = qn.shape
    Sk, dv = k.shape[1], v.shape[2]
    bq, bk = _attn_blocks(S, Sk)
    nq, nkb = S // bq, Sk // bk

    def body(qn_ref, k_ref, v_ref, oT_ref, lse_ref, doT_ref, do_ref, dq_ref, dk_ref, dv_ref):
        i = pl.program_id(1)

        @pl.when(i == 0)
        def _():
            dk_ref[...] = jnp.zeros_like(dk_ref)
            dv_ref[...] = jnp.zeros_like(dv_ref)

        qnb = qn_ref[...]
        qTb = qnb.T
        dob = do_ref[...]
        doTf = doT_ref[...]
        doTb = doTf.astype(BF16)
        delta = jnp.sum(doTf * oT_ref[...], axis=0, keepdims=True)
        lse = lse_ref[...]

        def blk(j, dq, masked):
            off = pl.multiple_of(j * bk, bk)
            kb = k_ref[pl.ds(off, bk), :]
            sT = _dot(kb, qTb) * cmul
            if masked:
                sT = jnp.where(_valid_t(i, j, bq, bk, False), sT, NEG_BIG)
            p = jnp.exp2(sT - lse)
            dp = _dot(v_ref[pl.ds(off, bk), :], doTb)
            ds = p * (dp - delta)
            dsb = (ds * gscale).astype(BF16) if gscale != 1.0 else ds.astype(BF16)
            dv_ref[pl.ds(off, bk), :] += _dot(p.astype(BF16), dob)
            dk_ref[pl.ds(off, bk), :] += _dot(dsb, qnb)
            return dq + _dot(kb.T, dsb)

        dq = jnp.zeros((DK, bq), F32)
        if causal:
            dq = lax.fori_loop(0, i, lambda j, c: blk(j, c, False), dq)
            dq = blk(i, dq, True)
        else:
            dq = lax.fori_loop(0, nkb, lambda j, c: blk(j, c, False), dq)
        dq_ref[...] = dq.T

    qcol = lambda d: pl.BlockSpec((None, d, bq), lambda h, i: (h, 0, i))
    qrow = lambda d: pl.BlockSpec((None, bq, d), lambda h, i: (h, i, 0))
    krow = lambda d: pl.BlockSpec((None, Sk, d), lambda h, i: (h, 0, 0))
    return pl.pallas_call(
        body, name=name, grid=(H, nq),
        in_specs=[qrow(DK), krow(DK), krow(dv), qcol(dv), qcol(1), qcol(dv), qrow(dv)],
        out_specs=[qrow(DK), krow(DK), krow(dv)],
        out_shape=[jax.ShapeDtypeStruct((H, S, DK), F32), jax.ShapeDtypeStruct((H, Sk, DK), F32),
                   jax.ShapeDtypeStruct((H, Sk, dv), F32)],
        compiler_params=_cp(("parallel", "arbitrary"), ATTN_VMEM_LIMIT),
    )(qn, k, v, oT, lse, doT, do)


def _tri(n, fn):
    r = lax.broadcasted_iota(jnp.int32, (n, n), 0)
    c = lax.broadcasted_iota(jnp.int32, (n, n), 1)
    return jnp.where(fn(r, c), 1.0, 0.0).astype(BF16)


def _key_cumsum(x, tri2, suffix, base):
    bk = x.shape[0]
    c = min(CUMSUM_CHUNK, bk)
    n = bk // c
    hi32 = lax.bitcast_convert_type(lax.bitcast_convert_type(x, jnp.int32) & jnp.int32(-65536), F32)
    hi = hi32.astype(BF16)
    lo = (x - hi32).astype(BF16)
    tot = [jnp.sum(x[a * c:(a + 1) * c], axis=0, keepdims=True) for a in range(n)]
    outs = []
    for a in range(n):
        row = base
        for t in (tot[a + 1:] if suffix else tot[:a]):
            row = row + t
        stacked = jnp.concatenate([hi[a * c:(a + 1) * c], lo[a * c:(a + 1) * c]], axis=0)
        outs.append(_dot(tri2, stacked) + row)
    total = tot[0]
    for t in tot[1:]:
        total = total + t
    return (outs[0] if n == 1 else jnp.concatenate(outs, axis=0)), total


def _tri2(n, fn):
    t = _tri(n, fn)
    return jnp.concatenate([t, t], axis=1)


def _sb_logs(z):
    neg_abs = lax.bitcast_convert_type(lax.bitcast_convert_type(z, jnp.int32) | jnp.int32(-2 ** 31), F32)
    ls = jnp.minimum(z, 0.0) - jnp.log(1.0 + jnp.exp(neg_abs))
    return ls, ls - z


PAIR = LANE // HEAD_DIM
FWD_PAIRS = 2
SB_DEAD = -110.0
SB_BLOCK = 256
FOX_DEAD = -160.0
FOX_BLOCK = 512


def _head_lanes(shape, w, axis):
    idx = lax.broadcasted_iota(jnp.int32, shape, axis)
    return (idx >= HEAD_DIM * w) & (idx < HEAD_DIM * (w + 1))


def _bias_rows(w, bq):
    row = lax.broadcasted_iota(jnp.int32, (LANE, bq), 0)
    return jnp.where((row >= 3 * w) & (row < 3 * w + 3), -1.0, 0.0).astype(BF16)


def _merge_pair(parts):
    return jnp.where(_head_lanes(parts[0].shape, 0, 0), parts[0], parts[1]).T


def _smp_fwd(q2, k2, v2, bias, r, causal, name, kstat=None):
    S, C = q2.shape
    Sk = k2.shape[0]
    bq, bk = _attn_blocks(S, Sk, FOX_BLOCK if bias is not None else None)
    nq, nkb, P = S // bq, Sk // bk, C // LANE
    gp = FWD_PAIRS if P % FWD_PAIRS == 0 else 1
    use_f = bias is not None
    if causal:
        assert S == Sk and bq == bk and use_f

    def body(*refs):
        if use_f:
            ks_ref, q_ref, k_ref, v_ref, b_ref, r_ref, o_ref, lse_ref, js_ref = refs
        else:
            q_ref, k_ref, v_ref, o_ref, lse_ref = refs
        i = pl.program_id(1)
        heads = range(PAIR * gp)
        lanes = [slice(LANE * (h // PAIR), LANE * (h // PAIR + 1)) for h in heads]
        qps = [q_ref[:, lanes[h]] for h in heads]
        qTs = [jnp.where(_head_lanes(qps[h].shape, h % PAIR, 1), qps[h], jnp.zeros_like(qps[h])).T for h in heads]
        if use_f:
            qf = [t.astype(F32) for t in qTs]
            qnorm = [jnp.sqrt(jnp.sum(t * t, axis=0, keepdims=True)) for t in qf]
            qTs = [jnp.concatenate([qTs[h], _bias_rows(h % PAIR, bq)], axis=0) for h in heads]

        def blk(j, carry, masked):
            off = pl.multiple_of(j * bk, bk)
            kbs = [k_ref[pl.ds(off, bk), LANE * g:LANE * (g + 1)] for g in range(gp)]
            if use_f:
                kbs = [jnp.concatenate([kbs[g], b_ref[pl.ds(off, bk), LANE * g:LANE * (g + 1)]], axis=1)
                       for g in range(gp)]
            vTbs = [v_ref[pl.ds(off, bk), LANE * g:LANE * (g + 1)].T for g in range(gp)]
            sT = [_dot(kbs[h // PAIR], qTs[h]) * LOG2E for h in heads]
            if masked:
                valid = _valid_t(i, j, bq, bk, False)
                sT = [jnp.where(valid, s, NEG_BIG) for s in sT]
            cm = [jnp.max(s, axis=0, keepdims=True) for s in sT]
            if use_f:
                cm = [cm[h] + r_ref[h] for h in heads]
            m_new = [jnp.maximum(carry[h][0], cm[h]) for h in heads]
            shift = [(m_new[h] - r_ref[h]) if use_f else m_new[h] for h in heads]
            p = [jnp.exp2(sT[h] - shift[h]) for h in heads]
            a = [jnp.exp2(carry[h][0] - m_new[h]) for h in heads]
            l = [a[h] * carry[h][1] + jnp.sum(p[h], axis=0, keepdims=True) for h in heads]
            acc = [a[h] * carry[h][2] + _dot(vTbs[h // PAIR], p[h].astype(BF16)) for h in heads]
            return tuple((m_new[h], l[h], acc[h]) for h in heads)

        def step(jj, state):
            carry, first = state
            j = i - jj
            h0 = pl.program_id(0) * (PAIR * gp)
            bound = [LOG2E * (qnorm[h] * ks_ref[(h0 + h) * nkb + j] - ks_ref[(PAIR * P + h0 + h) * nkb + j])
                     + r_ref[h] - carry[h][0] for h in heads]
            live = jnp.max(functools.reduce(jnp.maximum, bound)) >= FOX_DEAD
            carry = lax.cond(live, lambda cr: blk(j, cr, False), lambda cr: cr, carry)
            return carry, jnp.where(live, j, first)

        carry = tuple((jnp.full((1, bq), NEG_BIG, F32), jnp.zeros((1, bq), F32), jnp.zeros((LANE, bq), F32))
                      for _ in heads)
        if causal:
            carry = blk(i, carry, True)
            carry, first = lax.fori_loop(1, i + 1, step, (carry, i))
            js_ref[0] = jnp.full((1, bq), first, jnp.int32)
        else:
            carry = lax.fori_loop(0, nkb, lambda j, c: blk(j, c, False), carry)
            if use_f:
                js_ref[0] = jnp.zeros((1, bq), jnp.int32)
        for h in heads:
            lse_ref[h] = carry[h][0] + jnp.log2(carry[h][1])
        for g in range(gp):
            o_ref[:, LANE * g:LANE * (g + 1)] = _merge_pair(
                [carry[h][2] / carry[h][1] for h in range(PAIR * g, PAIR * (g + 1))])

    qblk = pl.BlockSpec((bq, LANE * gp), lambda p, i: (i, p))
    kres = pl.BlockSpec((Sk, LANE * gp), lambda p, i: (0, p))
    stat = pl.BlockSpec((PAIR * gp, 1, bq), lambda p, i: (p, 0, i))
    in_specs = [qblk, kres, kres]
    args = [q2, k2, v2]
    out_specs = [qblk, stat]
    out_shape = [jax.ShapeDtypeStruct((S, C), F32), jax.ShapeDtypeStruct((PAIR * P, 1, S), F32)]
    if use_f:
        in_specs = [pl.BlockSpec(memory_space=pltpu.SMEM)] + in_specs + [kres, stat]
        args = [kstat] + args + [bias, r]
        out_specs.append(pl.BlockSpec((1, 1, bq), lambda p, i: (p, 0, i)))
        out_shape.append(jax.ShapeDtypeStruct((P // gp, 1, S), jnp.int32))
    return pl.pallas_call(
        body, name=name, grid=(P // gp, nq), in_specs=in_specs, out_specs=out_specs, out_shape=out_shape,
        compiler_params=_cp(("parallel", "arbitrary"), ATTN_VMEM_LIMIT),
    )(*args)


def _smp_bwd(q2, k2, v2, o2, lse, do2, bias, r, scale, causal, name, first=None):
    S, C = q2.shape
    Sk = k2.shape[0]
    bq, bk = _attn_blocks(S, Sk, FOX_BLOCK if bias is not None else None)
    nq, nkb, P = S // bq, Sk // bk, C // LANE
    use_f = bias is not None

    def body(*refs):
        if use_f:
            (first_ref, q_ref, k_ref, v_ref, o_ref, lse_ref, do_ref, b_ref, r_ref,
             dq_ref, dk_ref, dv_ref, dr_ref, dkey_ref, dk_acc, dv_acc, db_ref) = refs
        else:
            q_ref, k_ref, v_ref, o_ref, lse_ref, do_ref, dq_ref, dk_ref, dv_ref, dk_acc, dv_acc = refs
        i = pl.program_id(1)

        @pl.when(i == 0)
        def _():
            dk_acc[...] = jnp.zeros_like(dk_acc)
            dv_acc[...] = jnp.zeros_like(dv_acc)
            if use_f:
                db_ref[...] = jnp.zeros_like(db_ref)

        qp = q_ref[...]
        dof = do_ref[...]
        prod = dof * o_ref[...]
        heads = range(PAIR)
        mine = [_head_lanes(qp.shape, w, 1) for w in heads]
        qz = [jnp.where(mine[w], qp, jnp.zeros_like(qp)) for w in heads]
        qTs = [qz[w].T for w in heads]
        if use_f:
            qTs = [jnp.concatenate([qTs[w], _bias_rows(w, bq)], axis=0) for w in heads]
        doz = [jnp.where(mine[w], dof, 0.0).astype(BF16) for w in heads]
        doT = [doz[w].T for w in heads]
        delta = [jnp.sum(jnp.where(mine[w], prod, 0.0).T, axis=0, keepdims=True) for w in heads]
        shift = [(lse_ref[w] - r_ref[w]) if use_f else lse_ref[w] for w in heads]

        def blk(j, carry, masked):
            off = pl.multiple_of(j * bk, bk)
            kb = k_ref[pl.ds(off, bk), :]
            kTb = kb.T
            if use_f:
                kb = jnp.concatenate([kb, b_ref[pl.ds(off, bk), :]], axis=1)
            vb = v_ref[pl.ds(off, bk), :]
            sT = [_dot(kb, qTs[w]) * LOG2E for w in heads]
            if masked:
                valid = _valid_t(i, j, bq, bk, False)
                sT = [jnp.where(valid, s, NEG_BIG) for s in sT]
            p = [jnp.exp2(sT[w] - shift[w]) for w in heads]
            dp = [_dot(vb, doT[w]) for w in heads]
            ds = [p[w] * (dp[w] - delta[w]) for w in heads]
            dsb = [d.astype(BF16) for d in ds]
            dvs = [_dot(p[w].astype(BF16), doz[w]) for w in heads]
            dks = [_dot(dsb[w], qz[w]) for w in heads]
            dv_acc[pl.ds(off, bk), :] += dvs[0] + dvs[1]
            dk_acc[pl.ds(off, bk), :] += dks[0] + dks[1]
            dr = [carry[w][1] for w in heads]
            if use_f:
                dr = [dr[w] + jnp.sum(ds[w], axis=0, keepdims=True) for w in heads]
                lane = lax.broadcasted_iota(jnp.int32, (bk, LANE), 1)
                cols = [jnp.where(lane == w, jnp.sum(ds[w], axis=1, keepdims=True), 0.0) for w in heads]
                db_ref[pl.ds(off, bk), :] += cols[0] + cols[1]
            dq = [carry[w][0] + _dot(kTb, dsb[w]) for w in heads]
            return tuple((dq[w], dr[w]) for w in heads)

        carry = tuple((jnp.zeros((LANE, bq), F32), jnp.zeros((1, bq), F32)) for _ in heads)
        if causal:
            start = first_ref[pl.program_id(0) // (P // first.shape[0]), i]
            carry = lax.fori_loop(start, i, lambda j, c: blk(j, c, False), carry)
            carry = blk(i, carry, True)
        else:
            carry = lax.fori_loop(0, nkb, lambda j, c: blk(j, c, False), carry)
        if use_f:
            for w in heads:
                dr_ref[w] = carry[w][1]
        dq_ref[...] = (_merge_pair([carry[w][0] for w in heads]) * scale).astype(BF16)

        @pl.when(i == nq - 1)
        def _():
            dk_ref[...] = dk_acc[...].astype(BF16)
            dv_ref[...] = dv_acc[...].astype(BF16)

        if use_f:
            @pl.when(i == nq - 1)
            def _():
                def chunk(cidx, carry):
                    off = pl.multiple_of(cidx * LANE, LANE)
                    t = db_ref[pl.ds(off, LANE), :].T
                    for w in range(PAIR):
                        dkey_ref[w, :, pl.ds(off, LANE)] = t[w:w + 1, :]
                    return carry

                lax.fori_loop(0, Sk // LANE, chunk, 0)

    qblk = pl.BlockSpec((bq, LANE), lambda p, i: (i, p))
    kres = pl.BlockSpec((Sk, LANE), lambda p, i: (0, p))
    stat = pl.BlockSpec((PAIR, 1, bq), lambda p, i: (p, 0, i))
    in_specs = [qblk, kres, kres, qblk, stat, qblk]
    args = [q2, k2, v2, o2, lse, do2]
    out_specs = [qblk, kres, kres]
    out_shape = [jax.ShapeDtypeStruct((S, C), BF16), jax.ShapeDtypeStruct((Sk, C), BF16),
                 jax.ShapeDtypeStruct((Sk, C), BF16)]
    scratch = [pltpu.VMEM((Sk, LANE), F32), pltpu.VMEM((Sk, LANE), F32)]
    if use_f:
        in_specs = [pl.BlockSpec(memory_space=pltpu.SMEM)] + in_specs + [kres, stat]
        args = [first] + args + [bias, r]
        out_specs += [stat, pl.BlockSpec((PAIR, 1, Sk), lambda p, i: (p, 0, 0))]
        out_shape += [jax.ShapeDtypeStruct((PAIR * P, 1, S), F32), jax.ShapeDtypeStruct((PAIR * P, 1, Sk), F32)]
        scratch.append(pltpu.VMEM((Sk, LANE), F32))
    return pl.pallas_call(
        body, name=name, grid=(P, nq), in_specs=in_specs, out_specs=out_specs, out_shape=out_shape,
        scratch_shapes=scratch, compiler_params=_cp(("parallel", "arbitrary"), ATTN_VMEM_LIMIT),
    )(*args)


def _sbp_fwd(q2, k2, v2, name):
    S, C = q2.shape
    bq, bk = _attn_blocks(S, S, SB_BLOCK)
    assert bq == bk
    nq, P = S // bq, C // LANE
    gp = FWD_PAIRS if P % FWD_PAIRS == 0 else 1
    c = min(CUMSUM_CHUNK, bk)

    def body(q_ref, k_ref, v_ref, o_ref, lt_ref, js_ref):
        i = pl.program_id(1)
        after = _tri2(c, lambda s, j: j > s)
        heads = range(PAIR * gp)
        qps = [q_ref[:, LANE * (h // PAIR):LANE * (h // PAIR + 1)] for h in heads]
        qTs = [jnp.where(_head_lanes(qps[h].shape, h % PAIR, 1), qps[h], jnp.zeros_like(qps[h])).T for h in heads]

        def blk(jj, carry, masked):
            j = i - jj
            off = pl.multiple_of(j * bk, bk)
            kbs = [k_ref[pl.ds(off, bk), LANE * g:LANE * (g + 1)] for g in range(gp)]
            vTbs = [v_ref[pl.ds(off, bk), LANE * g:LANE * (g + 1)].T for g in range(gp)]
            logs = [_sb_logs(_dot(kbs[h // PAIR], qTs[h])) for h in heads]
            ls, lk = [t[0] for t in logs], [t[1] for t in logs]
            if masked:
                valid = _valid_t(i, j, bq, bk, True)
                lk = [jnp.where(valid, t, 0.0) for t in lk]
            cs = [_key_cumsum(lk[h], after, True, carry[h][0]) for h in heads]
            wgt = [jnp.exp(ls[h] + cs[h][0]) for h in heads]
            if masked:
                wgt = [jnp.where(valid, t, 0.0) for t in wgt]
            acc = [carry[h][1] + _dot(vTbs[h // PAIR], wgt[h].astype(BF16)) for h in heads]
            return tuple((carry[h][0] + cs[h][1], acc[h]) for h in heads)

        def step(jj, state):
            carry, first = state
            live = jnp.max(functools.reduce(jnp.maximum, [carry[h][0] for h in heads])) >= SB_DEAD
            carry = lax.cond(live, lambda cr: blk(jj, cr, False), lambda cr: cr, carry)
            return carry, jnp.where(live, i - jj, first)

        carry = tuple((jnp.zeros((1, bq), F32), jnp.zeros((LANE, bq), F32)) for _ in heads)
        carry = blk(0, carry, True)
        carry, first = lax.fori_loop(1, i + 1, step, (carry, i))
        js_ref[0] = jnp.full((1, bq), first, jnp.int32)
        for h in heads:
            lt_ref[h] = carry[h][0]
        for g in range(gp):
            o_ref[:, LANE * g:LANE * (g + 1)] = _merge_pair([carry[h][1] for h in range(PAIR * g, PAIR * (g + 1))])

    qblk = pl.BlockSpec((bq, LANE * gp), lambda p, i: (i, p))
    kres = pl.BlockSpec((S, LANE * gp), lambda p, i: (0, p))
    stat = pl.BlockSpec((PAIR * gp, 1, bq), lambda p, i: (p, 0, i))
    return pl.pallas_call(
        body, name=name, grid=(P // gp, nq),
        in_specs=[qblk, kres, kres],
        out_specs=[qblk, stat, pl.BlockSpec((1, 1, bq), lambda p, i: (p, 0, i))],
        out_shape=[jax.ShapeDtypeStruct((S, C), F32), jax.ShapeDtypeStruct((PAIR * P, 1, S), F32),
                   jax.ShapeDtypeStruct((P // gp, 1, S), jnp.int32)],
        compiler_params=_cp(("parallel", "arbitrary"), ATTN_VMEM_LIMIT),
    )(q2, k2, v2)


def _sbp_bwd(q2, k2, v2, lt, first, do2, scale, name):
    S, C = q2.shape
    bq, bk = _attn_blocks(S, S, SB_BLOCK)
    nq, P = S // bq, C // LANE
    c = min(CUMSUM_CHUNK, bk)

    per_group = P // first.shape[0]

    def body(first_ref, q_ref, k_ref, v_ref, lt_ref, do_ref, dq_ref, dk_ref, dv_ref, dk_acc, dv_acc):
        i = pl.program_id(1)

        @pl.when(i == 0)
        def _():
            dk_acc[...] = jnp.zeros_like(dk_acc)
            dv_acc[...] = jnp.zeros_like(dv_acc)

        qp = q_ref[...]
        dof = do_ref[...]
        upto = _tri2(c, lambda s, j: j <= s)
        before = _tri2(c, lambda s, j: j < s)
        heads = range(PAIR)
        mine = [_head_lanes(qp.shape, w, 1) for w in heads]
        qz = [jnp.where(mine[w], qp, jnp.zeros_like(qp)) for w in heads]
        qTs = [qz[w].T for w in heads]
        doz = [jnp.where(mine[w], dof, 0.0).astype(BF16) for w in heads]
        doT = [doz[w].T for w in heads]
        ltot = [lt_ref[w] for w in heads]

        def blk(j, carry, masked):
            off = pl.multiple_of(j * bk, bk)
            kb = k_ref[pl.ds(off, bk), :]
            vb = v_ref[pl.ds(off, bk), :]
            kTb = kb.T
            logs = [_sb_logs(_dot(kb, qTs[w])) for w in heads]
            ls, lk = [t[0] for t in logs], [t[1] for t in logs]
            if masked:
                valid = _valid_t(i, j, bq, bk, True)
                lk = [jnp.where(valid, t, 0.0) for t in lk]
            pin = [_key_cumsum(lk[w], upto, False, carry[w][1] - ltot[w]) for w in heads]
            wgt = [jnp.exp(ls[w] - pin[w][0]) for w in heads]
            if masked:
                wgt = [jnp.where(valid, t, 0.0) for t in wgt]
            g = [_dot(vb, doT[w]) * wgt[w] for w in heads]
            cin = [_key_cumsum(g[w], before, False, carry[w][2]) for w in heads]
            sig = [jnp.exp(t) for t in ls]
            dz = [g[w] * (1.0 - sig[w]) - cin[w][0] * sig[w] for w in heads]
            if masked:
                dz = [jnp.where(valid, t, 0.0) for t in dz]
            dzb = [t.astype(BF16) for t in dz]
            dvs = [_dot(wgt[w].astype(BF16), doz[w]) for w in heads]
            dks = [_dot(dzb[w], qz[w]) for w in heads]
            dv_acc[pl.ds(off, bk), :] += dvs[0] + dvs[1]
            dk_acc[pl.ds(off, bk), :] += dks[0] + dks[1]
            return tuple((carry[w][0] + _dot(kTb, dzb[w]), carry[w][1] + pin[w][1], carry[w][2] + cin[w][1])
                         for w in heads)

        carry = tuple((jnp.zeros((LANE, bq), F32), jnp.zeros((1, bq), F32), jnp.zeros((1, bq), F32)) for _ in heads)
        start = first_ref[pl.program_id(0) // per_group, i]
        carry = lax.fori_loop(start, i, lambda j, cr: blk(j, cr, False), carry)
        carry = blk(i, carry, True)
        dq_ref[...] = (_merge_pair([carry[w][0] for w in heads]) * scale).astype(BF16)

        @pl.when(i == nq - 1)
        def _():
            dk_ref[...] = dk_acc[...].astype(BF16)
            dv_ref[...] = dv_acc[...].astype(BF16)

    qblk = pl.BlockSpec((bq, LANE), lambda p, i: (i, p))
    kres = pl.BlockSpec((S, LANE), lambda p, i: (0, p))
    stat = pl.BlockSpec((PAIR, 1, bq), lambda p, i: (p, 0, i))
    return pl.pallas_call(
        body, name=name, grid=(P, nq),
        in_specs=[pl.BlockSpec(memory_space=pltpu.SMEM), qblk, kres, kres, stat, qblk],
        out_specs=[qblk, kres, kres],
        out_shape=[jax.ShapeDtypeStruct((S, C), BF16)] * 3,
        scratch_shapes=[pltpu.VMEM((S, LANE), F32), pltpu.VMEM((S, LANE), F32)],
        compiler_params=_cp(("parallel", "arbitrary"), ATTN_VMEM_LIMIT),
    )(first, q2, k2, v2, lt, do2)


def _bias_cols(f_cum):
    H, Sk = f_cum.shape
    terms = jnp.stack(_split3(f_cum), axis=-1)
    packed = terms.reshape(H // PAIR, PAIR, Sk, 3).transpose(2, 0, 1, 3).reshape(Sk, H // PAIR, PAIR * 3)
    return jnp.pad(packed, ((0, 0), (0, 0), (0, LANE - PAIR * 3))).reshape(Sk, -1)


def _make_packed_softmax(name, scale, causal, use_f):
    assert _pow2(scale)

    def run_fwd(q16, k16, v16, f_cum):
        q16 = q16 * scale
        if not use_f:
            o, lse = _smp_fwd(q16, k16, v16, None, None, causal, name + "_fwd")
            return o, (q16, k16, v16, o, lse, None, None, None)
        bq, bk = _attn_blocks(q16.shape[0], k16.shape[0], FOX_BLOCK)
        n_heads = f_cum.shape[0]
        knorm = jnp.sqrt(jnp.sum(jnp.square(k16.astype(F32)).reshape(-1, bk, n_heads, HEAD_DIM), axis=3))
        kstat = jnp.concatenate([jnp.max(knorm, axis=1).T.reshape(-1), f_cum[:, bk - 1::bk].reshape(-1)])
        bias, r = _bias_cols(f_cum), (f_cum * LOG2E)[:, None, :]
        o, lse, first = _smp_fwd(q16, k16, v16, bias, r, causal, name + "_fwd", lax.stop_gradient(kstat))
        return o, (q16, k16, v16, o, lse, bias, r, first[:, 0, ::bq])

    def run_bwd(saved, do):
        q16, k16, v16, o, lse, bias, r, first = saved
        outs = _smp_bwd(q16, k16, v16, o, lse, do, bias, r, scale, causal, name + "_bwd", first)
        if use_f:
            return outs[0], outs[1], outs[2], outs[3][:, 0, :] - outs[4][:, 0, :]
        return tuple(outs)

    if use_f:
        @jax.custom_vjp
        def attn(q, k, v, f_cum):
            return run_fwd(q, k, v, f_cum)[0]

        attn.defvjp(run_fwd, run_bwd)
    else:
        @jax.custom_vjp
        def attn(q, k, v):
            return run_fwd(q, k, v, None)[0]

        attn.defvjp(lambda q, k, v: run_fwd(q, k, v, None), run_bwd)
    return attn


def _make_packed_sb(name, scale):
    assert _pow2(scale)

    def run_fwd(q16, k16, v16):
        q16 = q16 * scale
        o, lt, first = _sbp_fwd(q16, k16, v16, name + "_fwd")
        bq, _ = _attn_blocks(q16.shape[0], q16.shape[0], SB_BLOCK)
        return o, (q16, k16, v16, lt, first[:, 0, ::bq])

    def run_bwd(saved, do):
        q16, k16, v16, lt, first = saved
        return tuple(_sbp_bwd(q16, k16, v16, lt, first, do, scale, name + "_bwd"))

    @jax.custom_vjp
    def attn(q, k, v):
        return run_fwd(q, k, v)[0]

    attn.defvjp(run_fwd, run_bwd)
    return attn


def _round_bf16(x):
    return lax.reduce_precision(x, exponent_bits=8, mantissa_bits=7)


def _split3(x):
    hi = _round_bf16(x)
    mid = _round_bf16(x - hi)
    lo = _round_bf16(x - hi - mid)
    return hi.astype(BF16), mid.astype(BF16), lo.astype(BF16)


def _pow2(x):
    m, _ = math.frexp(x)
    return m == 0.5


def _pad_last(x, n):
    return jnp.pad(x, [(0, 0)] * (x.ndim - 1) + [(0, n - x.shape[-1])])


def _layouts(q, k, scale):
    qh = _pad_last(jnp.transpose(q * scale if _pow2(scale) else q, (1, 0, 2)).astype(BF16), LANE)
    return qh, _pad_last(jnp.transpose(k, (1, 0, 2)).astype(BF16), LANE)


def _make_softmax_attn(name, scale, causal, d):
    pre = _pow2(scale)
    cmul = LOG2E if pre else scale * LOG2E
    gscale = 1.0 if pre else scale

    def run_fwd(q, k, v):
        qn, kn = _layouts(q, k, scale)
        vn = jnp.transpose(v, (1, 0, 2)).astype(BF16)
        oT, lse = _sm_fwd_t(qn, kn, jnp.transpose(vn, (0, 2, 1)), cmul, causal, name + "_fwd")
        return jnp.transpose(oT, (2, 0, 1)), (qn, kn, vn, oT, lse)

    def run_bwd(saved, dout):
        qn, kn, vn, oT, lse = saved
        doT = jnp.transpose(dout, (1, 2, 0))
        do = jnp.transpose(dout, (1, 0, 2)).astype(BF16)
        dq, dk, dv = _sm_bwd_t(qn, kn, vn, oT, lse, doT, do, cmul, gscale, causal, name + "_bwd")
        dq = jnp.transpose(dq[:, :, :d], (1, 0, 2))
        if pre:
            dq = dq * scale
        return dq, jnp.transpose(dk[:, :, :d], (1, 0, 2)), jnp.transpose(dv, (1, 0, 2))

    @jax.custom_vjp
    def attn(q, k, v):
        return run_fwd(q, k, v)[0]

    attn.defvjp(run_fwd, run_bwd)
    return attn


def _rope(x, positions):
    half = x.shape[-1] // 2
    inv_freq = ROPE_THETA ** (-jnp.arange(half, dtype=F32) / half)
    ang = positions.astype(F32)[:, None] * inv_freq[None, :]
    ang = ang.reshape((ang.shape[0],) + (1,) * (x.ndim - 2) + (half,))
    cos, sin = jnp.cos(ang), jnp.sin(ang)
    x1, x2 = x[..., :half], x[..., half:]
    return jnp.concatenate([x1 * cos - x2 * sin, x1 * sin + x2 * cos], axis=-1)


def _permute_cols(w):
    parts = [w[..., _ORIG_OFF[idx]:_ORIG_OFF[idx] + SPLIT_SIZES[idx]] for _, idx in _PERM]
    pad = jnp.zeros(w.shape[:-1] + (PROJ_COLS - IN_COLS,), w.dtype)
    return jnp.concatenate(parts + [pad], axis=-1)


def _unpermute_cols(w):
    start, parts = 0, [None] * len(SPLIT_SIZES)
    for _, idx in _PERM:
        parts[idx] = w[..., start:start + SPLIT_SIZES[idx]]
        start += SPLIT_SIZES[idx]
    return jnp.concatenate(parts, axis=-1)


_BF16_PIECES = ("fq", "fk", "fv", "sq", "sk", "sv", "mq")


def _make_ln_proj(name, has_res):
    def split(proj32, proj16):
        out, off = [], 0
        for n, idx in _PERM:
            src = proj16 if n in _BF16_PIECES else proj32
            out.append(src[:, off:off + SPLIT_SIZES[idx]])
            off += SPLIT_SIZES[idx]
        return tuple(out)

    def run_fwd(x, res, g, b, w):
        h, h16, hT16 = _ln_fwd_call(x, res, g, b, name + "_ln_fwd", also16=True)
        w16 = w.astype(BF16)
        proj32, proj16 = _matmul(h16, w16, "nn", name + "_fwd", also16=True)
        return (h, split(proj32, proj16)), (x, res, g, hT16, w16)

    def run_bwd(saved, cts):
        x, res, g, hT16, w16 = saved
        dh, dpieces = cts
        pad = jnp.zeros((x.shape[0], PROJ_COLS - IN_COLS), BF16)
        dy16 = jnp.concatenate([c.astype(BF16) for c in dpieces] + [pad], axis=1)
        da = _matmul(dy16, w16, "nt", name + "_dx")
        dw = _matmul(hT16, dy16, "nn", name + "_dw")
        outs = _ln_bwd_call(dh, x, res, g, name + "_ln_bwd", dy2=da)
        if has_res:
            dx, dr, dg, db = outs
            return dx, dr, dg.reshape(-1), db.reshape(-1), dw
        dx, dg, db = outs
        return dx, dg.reshape(-1), db.reshape(-1), dw

    if has_res:
        @jax.custom_vjp
        def op(x, res, g, b, w):
            return run_fwd(x, res, g, b, w)[0]

        op.defvjp(run_fwd, run_bwd)
    else:
        @jax.custom_vjp
        def op(x, g, b, w):
            return run_fwd(x, None, g, b, w)[0]

        op.defvjp(lambda x, g, b, w: run_fwd(x, None, g, b, w), run_bwd)

    def call(*args):
        h, pieces = op(*args)
        return h, {n: part for (n, _), part in zip(_PERM, pieces)}

    return call


def _trunk_loss(wts, x2d, mem2d, target2d):
    s = x2d.shape[0]
    positions = jnp.arange(s)
    head_scale = HEAD_DIM ** -0.5
    mla_scale = (MLA_NOPE + MLA_ROPE) ** -0.5

    mem_n = _make_ln("ln_mem", False)(mem2d, wts["mem_ln_g"], wts["mem_ln_b"])
    h, y = None, x2d
    for l in range(DEPTH):
        tag = f"l{l}_"
        w_p = wts["w_in"][l]
        if l == 0:
            h, p = _make_ln_proj(tag + "proj", False)(y, wts["ln_in_g"], wts["ln_in_b"], w_p)
        else:
            h, p = _make_ln_proj(tag + "proj", True)(y, h, wts["ln_g"][l - 1], wts["ln_b"][l - 1], w_p)

        log_f = jax.nn.log_sigmoid(p["f_logit"] + wts["b_forget"][l])
        f_cum = jnp.cumsum(log_f, axis=0).T
        out_fox = _make_packed_softmax(tag + "fox", head_scale, True, True)(p["fq"], p["fk"], p["fv"], f_cum)

        out_sb = _make_packed_sb(tag + "sb", head_scale)(p["sq"], p["sk"], p["sv"])

        cqn = _make_rms(tag + "rms_q")(p["c_q"], wts["mla_q_norm_g"][l])
        q_mla = _make_mm(tag + "q_up")(cqn, wts["w_mla_q_up"][l]).reshape(s, N_HEADS, MLA_NOPE + MLA_ROPE)
        ckvn = _make_rms(tag + "rms_kv")(p["c_kv"], wts["mla_kv_norm_g"][l])
        kv_mla = _make_mm(tag + "kv_up")(ckvn, wts["w_mla_kv_up"][l]).reshape(s, N_HEADS, MLA_NOPE + MLA_V)
        q_full = jnp.concatenate([q_mla[..., :MLA_NOPE], _rope(q_mla[..., MLA_NOPE:], positions)], axis=-1)
        k_rope = jnp.broadcast_to(_rope(p["k_rot"], positions)[:, None, :], (s, N_HEADS, MLA_ROPE))
        k_full = jnp.concatenate([kv_mla[..., :MLA_NOPE], k_rope], axis=-1)
        out_mla = _make_softmax_attn(tag + "mla", mla_scale, True, MLA_NOPE + MLA_ROPE)(
            q_full, k_full, kv_mla[..., MLA_NOPE:]).reshape(s, GROUP_W)

        mkv = _make_mm(tag + "mem_kv")(mem_n, wts["w_mem_kv"][l])
        out_mem = _make_packed_softmax(tag + "mem", head_scale, False, False)(
            p["mq"], mkv[:, :GROUP_W].astype(BF16), mkv[:, GROUP_W:].astype(BF16))

        mixed = jnp.concatenate([out_fox, out_sb, out_mla, out_mem], axis=-1)
        y = _make_gate_out(tag + "out")(mixed, p["gate"], wts["w_out"][l])

    h = _make_ln(f"l{DEPTH - 1}_ln", True)(y, h, wts["ln_g"][DEPTH - 1], wts["ln_b"][DEPTH - 1])
    return _loss_op(h, target2d)


def _mesh_pos():
    x, y, c = (lax.axis_index(a) for a in MESH_AXES)
    return x, y, c, 4 * x + 2 * y + c


def _peer(x, y, c, mask):
    return (x ^ ((mask >> 2) & 1), y ^ ((mask >> 1) & 1), c ^ (mask & 1))


_ANY = pl.BlockSpec(memory_space=pl.ANY)


def _all_gather(row_shards, stack_shards):
    n_row, n_all = len(row_shards), len(row_shards) + len(stack_shards)
    shards = list(row_shards) + list(stack_shards)
    chip_masks = (4, 2, 6)
    tensors = range(n_all)

    def body(*refs):
        ins, outs = refs[:n_all], refs[n_all:2 * n_all]
        send_sems, recv_sems, local_sems = refs[2 * n_all:]
        x, y, c, me = _mesh_pos()
        sibling = _peer(x, y, c, 1)

        def window(t, slot):
            if t < n_row:
                rows = shards[t].shape[1]
                return outs[t].at[:, pl.ds(slot * rows, rows), :]
            return outs[t].at[slot]

        def copy(t, k, slot, to, src=None):
            return pltpu.make_async_remote_copy(
                src_ref=window(t, slot) if src is None else src, dst_ref=window(t, slot),
                send_sem=send_sems.at[t, k], recv_sem=recv_sems.at[t, k], device_id=to,
                device_id_type=pl.DeviceIdType.MESH)

        local = [pltpu.make_async_copy(ins[t], window(t, me), local_sems.at[t]) for t in tensors]
        for cp in local:
            cp.start()
        first = [copy(t, 0, me, sibling, src=ins[t]) for t in tensors]
        first += [copy(t, 1 + j, me, _peer(x, y, c, m), src=ins[t]) for j, m in enumerate(chip_masks) for t in tensors]
        for cp in first:
            cp.start()
        passed = []
        for j, m in enumerate(chip_masks):
            for t in tensors:
                copy(t, 1 + j, me ^ m, sibling).wait_recv()
            for t in tensors:
                cp = copy(t, 4 + j, me ^ m, sibling)
                cp.start()
                passed.append(cp)
        for t in tensors:
            copy(t, 0, me ^ 1, sibling).wait_recv()
        for j, m in enumerate(chip_masks):
            for t in tensors:
                copy(t, 4 + j, me ^ m ^ 1, sibling).wait_recv()
        for cp in first + passed:
            cp.wait_send()
        for cp in local:
            cp.wait()

    out_shape = [jax.ShapeDtypeStruct((a.shape[0], N_DEV * a.shape[1], a.shape[2]), a.dtype) for a in row_shards]
    out_shape += [jax.ShapeDtypeStruct((N_DEV,) + a.shape, a.dtype) for a in stack_shards]
    return pl.pallas_call(
        body, name="all_gather_weights", in_specs=[_ANY] * n_all, out_specs=[_ANY] * n_all, out_shape=out_shape,
        scratch_shapes=[pltpu.SemaphoreType.DMA((n_all, N_DEV - 1)), pltpu.SemaphoreType.DMA((n_all, N_DEV - 1)),
                        pltpu.SemaphoreType.DMA((n_all,))],
    )(*shards)


def _reduce_scatter(row_full, stack_full, bcast):
    n_row, n_stack = len(row_full), len(stack_full)
    n_all = n_row + n_stack + len(bcast)
    fulls = list(row_full) + list(stack_full) + list(bcast)

    def body(*refs):
        ins, outs = refs[:n_all], refs[n_all:2 * n_all]
        send_sems, recv_sems, local_sems = refs[2 * n_all:]
        x, y, c, me = _mesh_pos()

        def part(t, slot):
            if t < n_row:
                rows = fulls[t].shape[1] // N_DEV
                return ins[t].at[:, pl.ds(slot * rows, rows), :]
            if t < n_row + n_stack:
                return ins[t].at[slot]
            return ins[t]

        local = [pltpu.make_async_copy(part(t, me), outs[t].at[me], local_sems.at[t]) for t in range(n_all)]
        for cp in local:
            cp.start()
        sends = []
        for mask in range(1, N_DEV):
            for t in range(n_all):
                cp = pltpu.make_async_remote_copy(
                    src_ref=part(t, me ^ mask), dst_ref=outs[t].at[me], send_sem=send_sems.at[t, mask - 1],
                    recv_sem=recv_sems.at[t, mask - 1], device_id=_peer(x, y, c, mask),
                    device_id_type=pl.DeviceIdType.MESH)
                cp.start()
                sends.append(cp)
        for mask in range(1, N_DEV):
            for t in range(n_all):
                pltpu.make_async_remote_copy(
                    src_ref=part(t, me), dst_ref=outs[t].at[me ^ mask], send_sem=send_sems.at[t, mask - 1],
                    recv_sem=recv_sems.at[t, mask - 1], device_id=_peer(x, y, c, mask),
                    device_id_type=pl.DeviceIdType.MESH).wait_recv()
        for cp in sends:
            cp.wait_send()
        for cp in local:
            cp.wait()

    out_shape = [jax.ShapeDtypeStruct((N_DEV, a.shape[0], a.shape[1] // N_DEV, a.shape[2]), a.dtype) for a in row_full]
    out_shape += [jax.ShapeDtypeStruct(a.shape, a.dtype) for a in stack_full]
    out_shape += [jax.ShapeDtypeStruct((N_DEV,) + a.shape, a.dtype) for a in bcast]
    return pl.pallas_call(
        body, name="reduce_scatter_grads", in_specs=[_ANY] * n_all, out_specs=[_ANY] * n_all, out_shape=out_shape,
        scratch_shapes=[pltpu.SemaphoreType.DMA((n_all, N_DEV - 1)), pltpu.SemaphoreType.DMA((n_all, N_DEV - 1)),
                        pltpu.SemaphoreType.DMA((n_all,))],
    )(*fulls)


def _adamw(slots, w, m, v, name):
    shape = w.shape
    cols = shape[-1]
    rows = math.prod(shape[:-1])
    tr = _pick(rows, (64, 32, 16, 8))
    c1 = 1.0 - ADAM_B1 ** ADAM_STEP
    c2 = 1.0 - ADAM_B2 ** ADAM_STEP

    def body(s_ref, w_ref, m_ref, v_ref, g_ref, d_ref, nm_ref, nv_ref):
        g = s_ref[0].astype(F32)
        for k in range(1, N_DEV):
            g = g + s_ref[k].astype(F32)
        nm = ADAM_B1 * m_ref[...] + (1.0 - ADAM_B1) * g
        nv = ADAM_B2 * v_ref[...] + (1.0 - ADAM_B2) * (g * g)
        g_ref[...] = g
        nm_ref[...] = nm
        nv_ref[...] = nv
        d_ref[...] = -ADAM_LR * ((nm / c1) / (jnp.sqrt(nv / c2) + ADAM_EPS) + ADAM_WD * w_ref[...])

    row = pl.BlockSpec((tr, cols), lambda i: (i, 0))
    out = jax.ShapeDtypeStruct((rows, cols), F32)
    outs = pl.pallas_call(
        body, name=name, grid=(rows // tr,),
        in_specs=[pl.BlockSpec((N_DEV, tr, cols), lambda i: (0, i, 0)), row, row, row],
        out_specs=[row] * 4, out_shape=[out] * 4, compiler_params=_cp(("parallel",)),
    )(slots.reshape(N_DEV, rows, cols), w.reshape(rows, cols), m.reshape(rows, cols), v.reshape(rows, cols))
    return [o.reshape(shape) for o in outs]


_SMALL = ("ln_in_g", "ln_in_b", "mem_ln_g", "mem_ln_b", "b_forget", "mla_q_norm_g", "mla_kv_norm_g", "ln_g", "ln_b")
_ORDER = ("ln_in_g", "ln_in_b", "mem_ln_g", "mem_ln_b", "w_in", "b_forget", "mla_q_norm_g", "w_mla_q_up",
          "mla_kv_norm_g", "w_mla_kv_up", "w_mem_kv", "w_out", "ln_g", "ln_b")


def _pack_small(d):
    flat = jnp.concatenate([d[n].reshape(-1) for n in _SMALL])
    n = flat.shape[0]
    padded = ((n + 8 * LANE - 1) // (8 * LANE)) * (8 * LANE)
    return jnp.pad(flat, (0, padded - n)).reshape(-1, LANE)


def _unpack_small(packed, like):
    flat, out, off = packed.reshape(-1), {}, 0
    for n in _SMALL:
        size = math.prod(like[n].shape)
        out[n] = flat[off:off + size].reshape(like[n].shape)
        off += size
    return out


def _unstack_cols(g):
    n, l, r, c = g.shape
    return g.transpose(1, 2, 0, 3).reshape(l, r, n * c)


def _stack_cols(g):
    l, r, nc = g.shape
    return g.reshape(l, r, N_DEV, nc // N_DEV).transpose(2, 0, 1, 3)


def kernel(x, mem, ln_in_g, ln_in_b, mem_ln_g, mem_ln_b, w_in, b_forget, mla_q_norm_g, w_mla_q_up, mla_kv_norm_g, w_mla_kv_up, w_mem_kv, w_out, ln_g, ln_b, loss_target, m_ln_in_g, m_ln_in_b, m_mem_ln_g, m_mem_ln_b, m_w_in, m_b_forget, m_mla_q_norm_g, m_w_mla_q_up, m_mla_kv_norm_g, m_w_mla_kv_up, m_w_mem_kv, m_w_out, m_ln_g, m_ln_b, v_ln_in_g, v_ln_in_b, v_mem_ln_g, v_mem_ln_b, v_w_in, v_b_forget, v_mla_q_norm_g, v_w_mla_q_up, v_mla_kv_norm_g, v_w_mla_kv_up, v_w_mem_kv, v_w_out, v_ln_g, v_ln_b):
    w_shard = dict(ln_in_g=ln_in_g, ln_in_b=ln_in_b, mem_ln_g=mem_ln_g, mem_ln_b=mem_ln_b, w_in=w_in,
                   b_forget=b_forget, mla_q_norm_g=mla_q_norm_g, w_mla_q_up=w_mla_q_up,
                   mla_kv_norm_g=mla_kv_norm_g, w_mla_kv_up=w_mla_kv_up, w_mem_kv=w_mem_kv, w_out=w_out,
                   ln_g=ln_g, ln_b=ln_b)
    m_shard = dict(ln_in_g=m_ln_in_g, ln_in_b=m_ln_in_b, mem_ln_g=m_mem_ln_g, mem_ln_b=m_mem_ln_b, w_in=m_w_in,
                   b_forget=m_b_forget, mla_q_norm_g=m_mla_q_norm_g, w_mla_q_up=m_w_mla_q_up,
                   mla_kv_norm_g=m_mla_kv_norm_g, w_mla_kv_up=m_w_mla_kv_up, w_mem_kv=m_w_mem_kv, w_out=m_w_out,
                   ln_g=m_ln_g, ln_b=m_ln_b)
    v_shard = dict(ln_in_g=v_ln_in_g, ln_in_b=v_ln_in_b, mem_ln_g=v_mem_ln_g, mem_ln_b=v_mem_ln_b, w_in=v_w_in,
                   b_forget=v_b_forget, mla_q_norm_g=v_mla_q_norm_g, w_mla_q_up=v_w_mla_q_up,
                   mla_kv_norm_g=v_mla_kv_norm_g, w_mla_kv_up=v_w_mla_kv_up, w_mem_kv=v_w_mem_kv, w_out=v_w_out,
                   ln_g=v_ln_g, ln_b=v_ln_b)

    to16 = lambda ws: [a.astype(BF16) for a in ws]
    gathered = _all_gather(to16([_permute_cols(w_in), w_mem_kv, w_out]), to16([w_mla_q_up, w_mla_kv_up]))
    g_in, g_mem, g_out, g_qup, g_kvup = [a.astype(F32) for a in gathered]
    full = dict(w_shard)
    full.update(w_in=g_in, w_mem_kv=g_mem, w_out=g_out, w_mla_q_up=_unstack_cols(g_qup),
                w_mla_kv_up=_unstack_cols(g_kvup))

    loss_local, (grad_w, grad_x) = jax.value_and_grad(_trunk_loss, argnums=(0, 1))(
        full, x[0], mem[0], loss_target[0])

    s_in, s_mem, s_out, s_qup, s_kvup, s_small = _reduce_scatter(
        to16([grad_w["w_in"], grad_w["w_mem_kv"], grad_w["w_out"]]),
        to16([_stack_cols(grad_w["w_mla_q_up"]), _stack_cols(grad_w["w_mla_kv_up"])]),
        [_pack_small(grad_w)])

    res = {}
    for name, slots in (("w_mem_kv", s_mem), ("w_out", s_out), ("w_mla_q_up", s_qup), ("w_mla_kv_up", s_kvup)):
        res[name] = _adamw(slots, w_shard[name], m_shard[name], v_shard[name], "adamw_" + name)
    res["w_in"] = [_unpermute_cols(a) for a in _adamw(
        s_in, _permute_cols(w_in), _permute_cols(m_w_in), _permute_cols(v_w_in), "adamw_w_in")]
    small = _adamw(s_small, _pack_small(w_shard), _pack_small(m_shard), _pack_small(v_shard), "adamw_small")
    small = [_unpack_small(a, w_shard) for a in small]
    for name in _SMALL:
        res[name] = [a[name] for a in small]

    loss = lax.psum(loss_local, MESH_AXES)
    outs = [loss, grad_x[None]]
    for k in range(4):
        outs += [res[name][k] for name in _ORDER]
    return tuple(outs)
```

```python
import functools
import math

import jax
import jax.numpy as jnp
from jax import lax
from jax.experimental import pallas as pl
from jax.experimental.pallas import tpu as pltpu

F32 = jnp.float32
BF16 = jnp.bfloat16

D_MODEL = 1024
DEPTH = 2
GROUP_W = 256
N_HEADS = 4
HEAD_DIM = 64
MLA_Q_RANK = 256
MLA_KV_RANK = 128
MLA_NOPE = 64
MLA_ROPE = 32
MLA_V = 64
ROPE_THETA = 10000.0
LN_EPS = 1e-5
RMS_EPS = 1e-6
DEEPNORM_ALPHA = (2 * DEPTH) ** 0.25
SPLIT_SIZES = (256, 256, 256, 4, 256, 256, 256, 256, 128, 32, 256, 1024)
IN_COLS = sum(SPLIT_SIZES)
_ORIG_OFF = [sum(SPLIT_SIZES[:i]) for i in range(len(SPLIT_SIZES))]
_PERM = (("fq", 0), ("fk", 1), ("fv", 2), ("sq", 4), ("sk", 5), ("sv", 6), ("c_q", 7), ("c_kv", 8),
         ("mq", 10), ("gate", 11), ("k_rot", 9), ("f_logit", 3))
LANE = 128
PROJ_COLS = ((IN_COLS + LANE - 1) // LANE) * LANE

ADAM_LR = 0.001
ADAM_B1 = 0.9
ADAM_B2 = 0.999
ADAM_EPS = 1e-08
ADAM_WD = 0.01
ADAM_STEP = 10

N_DEV = 8
MESH_AXES = ("x", "y", "c")
VMEM_LIMIT = 48 * 1024 * 1024
ATTN_VMEM_LIMIT = 56 * 1024 * 1024
ATTN_BQ = 512
ATTN_BK = 512
CUMSUM_CHUNK = 256
NEG_BIG = -1e30
LOG2E = math.log2(math.e)
MM_TM, MM_TN, MM_TK, MM_TK_NT = 1024, 1664, 1024, 3328

_NT = (((1,), (1,)), ((), ()))
_NN = (((1,), (0,)), ((), ()))


def _cp(sem, vmem=VMEM_LIMIT):
    return pltpu.CompilerParams(dimension_semantics=sem, vmem_limit_bytes=vmem)


def _dot(a, b, dims=_NN):
    return lax.dot_general(a, b, dims, preferred_element_type=F32)


def _pick(n, cands):
    for c in cands:
        if c <= n and n % c == 0:
            return c
    return n


def _tile(n, cap):
    if n <= cap:
        return n
    best = None
    for d in range(LANE, cap + 1, LANE):
        if n % d == 0:
            best = d
    assert best is not None, (n, cap)
    return best


def _matmul(a, b, mode, name, also16=False):
    if mode == "nn":
        (M, K), (K2, N) = a.shape, b.shape
    else:
        (M, K), (N, K2) = a.shape, b.shape
    assert K == K2 and a.dtype == BF16 and b.dtype == BF16, (a.shape, b.shape, mode)
    tm, tn = _tile(M, MM_TM), _tile(N, MM_TN)
    tk = _tile(K, MM_TK if mode == "nn" else MM_TK_NT)
    nk = K // tk
    dims = _NN if mode == "nn" else _NT

    def body(a_ref, b_ref, *rest):
        o_ref, acc_ref = rest[0], rest[-1]
        part = _dot(a_ref[...], b_ref[...], dims)
        if nk == 1:
            o_ref[...] = part
            if also16:
                rest[1][...] = part.astype(BF16)
        else:
            assert not also16
            k = pl.program_id(2)

            @pl.when(k == 0)
            def _():
                acc_ref[...] = part

            @pl.when(k > 0)
            def _():
                acc_ref[...] += part

            @pl.when(k == nk - 1)
            def _():
                o_ref[...] = acc_ref[...]

    a_spec = pl.BlockSpec((tm, tk), lambda j, i, k: (i, k))
    if mode == "nn":
        b_spec = pl.BlockSpec((tk, tn), lambda j, i, k: (k, j))
    else:
        b_spec = pl.BlockSpec((tn, tk), lambda j, i, k: (j, k))
    acc_shape = (tm, tn) if nk > 1 else (8, LANE)
    o_spec = pl.BlockSpec((tm, tn), lambda j, i, k: (i, j))
    outs = pl.pallas_call(
        body, name=name, grid=(N // tn, M // tm, nk),
        in_specs=[a_spec, b_spec],
        out_specs=[o_spec, o_spec] if also16 else o_spec,
        out_shape=[jax.ShapeDtypeStruct((M, N), F32), jax.ShapeDtypeStruct((M, N), BF16)] if also16
        else jax.ShapeDtypeStruct((M, N), F32),
        scratch_shapes=[pltpu.VMEM(acc_shape, F32)],
        compiler_params=_cp(("parallel", "parallel", "arbitrary")),
    )(a, b)
    return outs


def _make_mm(name):
    @jax.custom_vjp
    def mm(a, w):
        return _matmul(a.astype(BF16), w.astype(BF16), "nn", name + "_fwd")

    def fwd(a, w):
        a16, w16 = a.astype(BF16), w.astype(BF16)
        return _matmul(a16, w16, "nn", name + "_fwd"), (a16, w16)

    def bwd(res, dy):
        a16, w16 = res
        dy16 = dy.astype(BF16)
        da = _matmul(dy16, w16, "nt", name + "_dx")
        dw = _matmul(a16.T, dy16, "nn", name + "_dw")
        return da, dw

    mm.defvjp(fwd, bwd)
    return mm


def _row_tile(rows):
    return _pick(rows, (512, 256, 128, 64, 32, 16, 8))


def _ln_stats(u):
    mu = jnp.mean(u, axis=-1, keepdims=True)
    d = u - mu
    var = jnp.mean(d * d, axis=-1, keepdims=True)
    return d, lax.rsqrt(var + LN_EPS)


def _ln_fwd_call(x, res, g, b, name, also16=False):
    rows, dm = x.shape
    tr = _row_tile(rows)
    has_res = res is not None
    n_in = 2 if has_res else 1

    def body(*refs):
        if has_res:
            u = DEEPNORM_ALPHA * refs[1][...] + refs[0][...]
        else:
            u = refs[0][...]
        g_ref, b_ref = refs[n_in], refs[n_in + 1]
        d, rstd = _ln_stats(u)
        y = d * rstd * g_ref[...] + b_ref[...]
        refs[n_in + 2][...] = y
        if also16:
            y16 = y.astype(BF16)
            refs[n_in + 3][...] = y16
            refs[n_in + 4][...] = y16.T

    row = pl.BlockSpec((tr, dm), lambda i: (i, 0))
    vec = pl.BlockSpec((1, dm), lambda i: (0, 0))
    args = (x, res) if has_res else (x,)
    out_specs, out_shape = [row], [jax.ShapeDtypeStruct((rows, dm), F32)]
    if also16:
        out_specs += [row, pl.BlockSpec((dm, tr), lambda i: (0, i))]
        out_shape += [jax.ShapeDtypeStruct((rows, dm), BF16), jax.ShapeDtypeStruct((dm, rows), BF16)]
    outs = pl.pallas_call(
        body, name=name, grid=(rows // tr,),
        in_specs=[row] * n_in + [vec, vec], out_specs=out_specs, out_shape=out_shape,
        compiler_params=_cp(("parallel",)),
    )(*args, g.reshape(1, dm), b.reshape(1, dm))
    return outs if also16 else outs[0]


def _ln_bwd_call(dy, x, res, g, name, dy2=None):
    rows, dm = x.shape
    tr = _row_tile(rows)
    has_res = res is not None
    two = dy2 is not None

    def body(*refs):
        dy_ref, refs = refs[0], refs[1:]
        if two:
            dy2_ref, refs = refs[0], refs[1:]
        if has_res:
            x_ref, r_ref, g_ref, dx_ref, dr_ref, dg_ref, db_ref = refs
            u = DEEPNORM_ALPHA * r_ref[...] + x_ref[...]
        else:
            x_ref, g_ref, dx_ref, dg_ref, db_ref = refs
            u = x_ref[...]
        i = pl.program_id(0)
        d, rstd = _ln_stats(u)
        xhat = d * rstd
        dyv = dy_ref[...] + dy2_ref[...] if two else dy_ref[...]
        dxh = dyv * g_ref[...]
        m1 = jnp.mean(dxh, axis=-1, keepdims=True)
        m2 = jnp.mean(dxh * xhat, axis=-1, keepdims=True)
        du = rstd * (dxh - m1 - xhat * m2)
        dx_ref[...] = du
        if has_res:
            dr_ref[...] = DEEPNORM_ALPHA * du
        pg = jnp.sum(dyv * xhat, axis=0, keepdims=True)
        pb = jnp.sum(dyv, axis=0, keepdims=True)

        @pl.when(i == 0)
        def _():
            dg_ref[...] = pg
            db_ref[...] = pb

        @pl.when(i > 0)
        def _():
            dg_ref[...] += pg
            db_ref[...] += pb

    row = pl.BlockSpec((tr, dm), lambda i: (i, 0))
    vec = pl.BlockSpec((1, dm), lambda i: (0, 0))
    big = jax.ShapeDtypeStruct((rows, dm), F32)
    small = jax.ShapeDtypeStruct((1, dm), F32)
    args = ((dy, dy2) if two else (dy,)) + ((x, res) if has_res else (x,))
    n_big = 2 if has_res else 1
    outs = pl.pallas_call(
        body, name=name, grid=(rows // tr,),
        in_specs=[row] * len(args) + [vec],
        out_specs=[row] * n_big + [vec, vec],
        out_shape=[big] * n_big + [small, small],
        compiler_params=_cp(("arbitrary",)),
    )(*args, g.reshape(1, dm))
    return outs


def _make_ln(name, has_res):
    if has_res:
        @jax.custom_vjp
        def ln(x, res, g, b):
            return _ln_fwd_call(x, res, g, b, name + "_fwd")

        def fwd(x, res, g, b):
            return ln(x, res, g, b), (x, res, g)

        def bwd(saved, dy):
            x, res, g = saved
            dx, dr, dg, db = _ln_bwd_call(dy, x, res, g, name + "_bwd")
            return dx, dr, dg.reshape(-1), db.reshape(-1)
    else:
        @jax.custom_vjp
        def ln(x, g, b):
            return _ln_fwd_call(x, None, g, b, name + "_fwd")

        def fwd(x, g, b):
            return ln(x, g, b), (x, g)

        def bwd(saved, dy):
            x, g = saved
            dx, dg, db = _ln_bwd_call(dy, x, None, g, name + "_bwd")
            return dx, dg.reshape(-1), db.reshape(-1)

    ln.defvjp(fwd, bwd)
    return ln


def _rms_fwd_call(x, g, name):
    rows, dm = x.shape
    tr = _row_tile(rows)

    def body(x_ref, g_ref, o_ref):
        xv = x_ref[...]
        rstd = lax.rsqrt(jnp.mean(xv * xv, axis=-1, keepdims=True) + RMS_EPS)
        o_ref[...] = xv * rstd * g_ref[...]

    row = pl.BlockSpec((tr, dm), lambda i: (i, 0))
    vec = pl.BlockSpec((1, dm), lambda i: (0, 0))
    return pl.pallas_call(
        body, name=name, grid=(rows // tr,), in_specs=[row, vec], out_specs=row,
        out_shape=jax.ShapeDtypeStruct((rows, dm), F32), compiler_params=_cp(("parallel",)),
    )(x, g.reshape(1, dm))


def _rms_bwd_call(dy, x, g, name):
    rows, dm = x.shape
    tr = _row_tile(rows)

    def body(dy_ref, x_ref, g_ref, dx_ref, dg_ref):
        i = pl.program_id(0)
        xv = x_ref[...]
        dyv = dy_ref[...]
        rstd = lax.rsqrt(jnp.mean(xv * xv, axis=-1, keepdims=True) + RMS_EPS)
        xhat = xv * rstd
        dxh = dyv * g_ref[...]
        m2 = jnp.mean(dxh * xhat, axis=-1, keepdims=True)
        dx_ref[...] = rstd * (dxh - xhat * m2)
        pg = jnp.sum(dyv * xhat, axis=0, keepdims=True)

        @pl.when(i == 0)
        def _():
            dg_ref[...] = pg

        @pl.when(i > 0)
        def _():
            dg_ref[...] += pg

    row = pl.BlockSpec((tr, dm), lambda i: (i, 0))
    vec = pl.BlockSpec((1, dm), lambda i: (0, 0))
    return pl.pallas_call(
        body, name=name, grid=(rows // tr,), in_specs=[row, row, vec], out_specs=[row, vec],
        out_shape=[jax.ShapeDtypeStruct((rows, dm), F32), jax.ShapeDtypeStruct((1, dm), F32)],
        compiler_params=_cp(("arbitrary",)),
    )(dy, x, g.reshape(1, dm))


def _make_rms(name):
    @jax.custom_vjp
    def rms(x, g):
        return _rms_fwd_call(x, g, name + "_fwd")

    def fwd(x, g):
        return rms(x, g), (x, g)

    def bwd(saved, dy):
        x, g = saved
        dx, dg = _rms_bwd_call(dy, x, g, name + "_bwd")
        return dx, dg.reshape(-1)

    rms.defvjp(fwd, bwd)
    return rms


def _sigmoid(x):
    return 1.0 / (1.0 + jnp.exp(-x))


def _gate_fwd_call(parts, gate, name):
    rows, dm = gate.shape
    tr = _row_tile(rows)
    n = len(parts)

    def body(*refs):
        g_ref, o_ref, oT_ref = refs[n:]
        gv = g_ref[...]
        mixed = jnp.concatenate([r[...] for r in refs[:n]], axis=1)
        y16 = (mixed * (gv * _sigmoid(gv))).astype(BF16)
        o_ref[...] = y16
        oT_ref[...] = y16.T

    row = pl.BlockSpec((tr, dm), lambda i: (i, 0))
    part_specs = [pl.BlockSpec((tr, p.shape[1]), lambda i: (i, 0)) for p in parts]
    return pl.pallas_call(
        body, name=name, grid=(rows // tr,), in_specs=part_specs + [row],
        out_specs=[row, pl.BlockSpec((dm, tr), lambda i: (0, i))],
        out_shape=[jax.ShapeDtypeStruct((rows, dm), BF16), jax.ShapeDtypeStruct((dm, rows), BF16)],
        compiler_params=_cp(("parallel",)),
    )(*parts, gate)


def _gate_bwd_call(dy, parts, gate, name):
    rows, dm = gate.shape
    tr = _row_tile(rows)
    n = len(parts)
    widths = [p.shape[1] for p in parts]

    def body(*refs):
        dy_ref, g_ref = refs[0], refs[n + 1]
        dm_refs, dg_ref = refs[n + 2:2 * n + 2], refs[2 * n + 2]
        gv = g_ref[...]
        dyv = dy_ref[...]
        sg = _sigmoid(gv)
        mixed = jnp.concatenate([r[...] for r in refs[1:n + 1]], axis=1)
        dmixed = dyv * (gv * sg)
        off = 0
        for r, w in zip(dm_refs, widths):
            r[...] = dmixed[:, off:off + w]
            off += w
        dg_ref[...] = dyv * mixed * (sg * (1.0 + gv * (1.0 - sg)))

    row = pl.BlockSpec((tr, dm), lambda i: (i, 0))
    part_specs = [pl.BlockSpec((tr, w), lambda i: (i, 0)) for w in widths]
    return pl.pallas_call(
        body, name=name, grid=(rows // tr,), in_specs=[row] + part_specs + [row], out_specs=part_specs + [row],
        out_shape=[jax.ShapeDtypeStruct((rows, w), F32) for w in widths] + [jax.ShapeDtypeStruct((rows, dm), F32)],
        compiler_params=_cp(("parallel",)),
    )(dy, *parts, gate)


def _make_gate_out(name):
    def run_fwd(parts, gate, w):
        g16, gT16 = _gate_fwd_call(parts, gate, name + "_gate_fwd")
        w16 = w.astype(BF16)
        return _matmul(g16, w16, "nn", name + "_fwd"), (parts, gate, gT16, w16)

    def run_bwd(saved, dy):
        parts, gate, gT16, w16 = saved
        dy16 = dy.astype(BF16)
        dgated = _matmul(dy16, w16, "nt", name + "_dx")
        *dparts, dgate = _gate_bwd_call(dgated, parts, gate, name + "_gate_bwd")
        return tuple(dparts), dgate, _matmul(gT16, dy16, "nn", name + "_dw")

    @jax.custom_vjp
    def gate_out(parts, gate, w):
        return run_fwd(parts, gate, w)[0]

    gate_out.defvjp(run_fwd, run_bwd)
    return gate_out


def _loss_call(y, t, name):
    rows, dm = y.shape
    tr = _row_tile(rows)

    def body(y_ref, t_ref, l_ref, d_ref):
        i = pl.program_id(0)
        e = y_ref[...] - t_ref[...]
        d_ref[...] = e * (1.0 / dm)
        part = 0.5 * jnp.sum(jnp.mean(e * e, axis=-1, keepdims=True), axis=0, keepdims=True)

        @pl.when(i == 0)
        def _():
            l_ref[...] = part

        @pl.when(i > 0)
        def _():
            l_ref[...] += part

    row = pl.BlockSpec((tr, dm), lambda i: (i, 0))
    one = pl.BlockSpec((1, 1), lambda i: (0, 0))
    return pl.pallas_call(
        body, name=name, grid=(rows // tr,), in_specs=[row, row], out_specs=[one, row],
        out_shape=[jax.ShapeDtypeStruct((1, 1), F32), jax.ShapeDtypeStruct((rows, dm), F32)],
        compiler_params=_cp(("arbitrary",)),
    )(y, t)


@jax.custom_vjp
def _loss_op(y, t):
    return _loss_call(y, t, "loss_head")[0][0, 0]


def _loss_fwd(y, t):
    l, d = _loss_call(y, t, "loss_head")
    return l[0, 0], d


def _loss_bwd(d, ct):
    return ct * d, jnp.zeros_like(d)


_loss_op.defvjp(_loss_fwd, _loss_bwd)


def _attn_blocks(S, Sk, cap=None):
    bq, bk = min(cap or ATTN_BQ, S), min(cap or ATTN_BK, Sk)
    assert S % bq == 0 and Sk % bk == 0
    return bq, bk


def _valid_t(i, j, bq, bk, strict):
    key = j * bk + lax.broadcasted_iota(jnp.int32, (bk, bq), 0)
    qry = i * bq + lax.broadcasted_iota(jnp.int32, (bk, bq), 1)
    return (key < qry) if strict else (key <= qry)


def _sm_fwd_t(qn, k, vT, cmul, causal, name):
    H, S, DK = qn.shape
    Sk, dv = k.shape[1], vT.shape[1]
    bq, bk = _attn_blocks(S, Sk)
    nq, nkb = S // bq, Sk // bk
    hb = PAIR * FWD_PAIRS if H % (PAIR * FWD_PAIRS) == 0 else 1
    heads = range(hb)
    if causal:
        assert S == Sk and bq == bk

    def body(qn_ref, k_ref, vT_ref, oT_ref, lse_ref):
        i = pl.program_id(1)
        qTs = [qn_ref[w].T for w in heads]

        def blk(j, carry, masked):
            off = pl.multiple_of(j * bk, bk)
            sT = [_dot(k_ref[w, pl.ds(off, bk), :], qTs[w]) * cmul for w in heads]
            if masked:
                valid = _valid_t(i, j, bq, bk, False)
                sT = [jnp.where(valid, s, NEG_BIG) for s in sT]
            m_new = [jnp.maximum(carry[w][0], jnp.max(sT[w], axis=0, keepdims=True)) for w in heads]
            p = [jnp.exp2(sT[w] - m_new[w]) for w in heads]
            a = [jnp.exp2(carry[w][0] - m_new[w]) for w in heads]
            l = [a[w] * carry[w][1] + jnp.sum(p[w], axis=0, keepdims=True) for w in heads]
            acc = [a[w] * carry[w][2] + _dot(vT_ref[w, :, pl.ds(off, bk)], p[w].astype(BF16)) for w in heads]
            return tuple((m_new[w], l[w], acc[w]) for w in heads)

        carry = tuple((jnp.full((1, bq), NEG_BIG, F32), jnp.zeros((1, bq), F32), jnp.zeros((dv, bq), F32))
                      for _ in heads)
        if causal:
            carry = lax.fori_loop(0, i, lambda j, c: blk(j, c, False), carry)
            carry = blk(i, carry, True)
        else:
            carry = lax.fori_loop(0, nkb, lambda j, c: blk(j, c, False), carry)
        for w in heads:
            oT_ref[w] = carry[w][2] / carry[w][1]
            lse_ref[w] = carry[w][0] + jnp.log2(carry[w][1])

    qcol = lambda d: pl.BlockSpec((hb, d, bq), lambda h, i: (h, 0, i))
    return pl.pallas_call(
        body, name=name, grid=(H // hb, nq),
        in_specs=[pl.BlockSpec((hb, bq, DK), lambda h, i: (h, i, 0)), pl.BlockSpec((hb, Sk, DK), lambda h, i: (h, 0, 0)),
                  pl.BlockSpec((hb, dv, Sk), lambda h, i: (h, 0, 0))],
        out_specs=[qcol(dv), qcol(1)],
        out_shape=[jax.ShapeDtypeStruct((H, dv, S), F32), jax.ShapeDtypeStruct((H, 1, S), F32)],
        compiler_params=_cp(("parallel", "arbitrary"), ATTN_VMEM_LIMIT),
    )(qn, k, vT)


def _sm_bwd_t(qn, k, v, oT, lse, doT, do, cmul, gscale, causal, name):
    H, S, DK = qn.shape
    Sk, dv = k.shape[1], v.shape[2]
    bq, bk = _attn_blocks(S, Sk)
    nq, nkb = S // bq, Sk // bk

    def body(qn_ref, k_ref, v_ref, oT_ref, lse_ref, doT_ref, do_ref, dq_ref, dk_ref, dv_ref):
        i = pl.program_id(1)

        @pl.when(i == 0)
        def _():
            dk_ref[...] = jnp.zeros_like(dk_ref)
            dv_ref[...] = jnp.zeros_like(dv_ref)

        qnb = qn_ref[...]
        qTb = qnb.T
        dob = do_ref[...]
        doTf = doT_ref[...]
        doTb = doTf.astype(BF16)
        delta = jnp.sum(doTf * oT_ref[...], axis=0, keepdims=True)
        lse = lse_ref[...]

        def blk(j, dq, masked):
            off = pl.multiple_of(j * bk, bk)
            kb = k_ref[pl.ds(off, bk), :]
            sT = _dot(kb, qTb) * cmul
            if masked:
                sT = jnp.where(_valid_t(i, j, bq, bk, False), sT, NEG_BIG)
            p = jnp.exp2(sT - lse)
            dp = _dot(v_ref[pl.ds(off, bk), :], doTb)
            ds = p * (dp - delta)
            dsb = (ds * gscale).astype(BF16) if gscale != 1.0 else ds.astype(BF16)
            dv_ref[pl.ds(off, bk), :] += _dot(p.astype(BF16), dob)
            dk_ref[pl.ds(off, bk), :] += _dot(dsb, qnb)
            return dq + _dot(kb.T, dsb)

        dq = jnp.zeros((DK, bq), F32)
        if causal:
            dq = lax.fori_loop(0, i, lambda j, c: blk(j, c, False), dq)
            dq = blk(i, dq, True)
        else:
            dq = lax.fori_loop(0, nkb, lambda j, c: blk(j, c, False), dq)
        dq_ref[...] = dq.T

    qcol = lambda d: pl.BlockSpec((None, d, bq), lambda h, i: (h, 0, i))
    qrow = lambda d: pl.BlockSpec((None, bq, d), lambda h, i: (h, i, 0))
    krow = lambda d: pl.BlockSpec((None, Sk, d), lambda h, i: (h, 0, 0))
    return pl.pallas_call(
        body, name=name, grid=(H, nq),
        in_specs=[qrow(DK), krow(DK), krow(dv), qcol(dv), qcol(1), qcol(dv), qrow(dv)],
        out_specs=[qrow(DK), krow(DK), krow(dv)],
        out_shape=[jax.ShapeDtypeStruct((H, S, DK), F32), jax.ShapeDtypeStruct((H, Sk, DK), F32),
                   jax.ShapeDtypeStruct((H, Sk, dv), F32)],
        compiler_params=_cp(("parallel", "arbitrary"), ATTN_VMEM_LIMIT),
    )(qn, k, v, oT, lse, doT, do)


def _tri(n, fn):
    r = lax.broadcasted_iota(jnp.int32, (n, n), 0)
    c = lax.broadcasted_iota(jnp.int32, (n, n), 1)
    return jnp.where(fn(r, c), 1.0, 0.0).astype(BF16)


def _key_cumsum(x, tri2, suffix, base):
    bk = x.shape[0]
    c = min(CUMSUM_CHUNK, bk)
    n = bk // c
    hi32 = lax.bitcast_convert_type(lax.bitcast_convert_type(x, jnp.int32) & jnp.int32(-65536), F32)
    hi = hi32.astype(BF16)
    lo = (x - hi32).astype(BF16)
    tot = [jnp.sum(x[a * c:(a + 1) * c], axis=0, keepdims=True) for a in range(n)]
    outs = []
    for a in range(n):
        row = base
        for t in (tot[a + 1:] if suffix else tot[:a]):
            row = row + t
        stacked = jnp.concatenate([hi[a * c:(a + 1) * c], lo[a * c:(a + 1) * c]], axis=0)
        outs.append(_dot(tri2, stacked) + row)
    total = tot[0]
    for t in tot[1:]:
        total = total + t
    return (outs[0] if n == 1 else jnp.concatenate(outs, axis=0)), total


def _tri2(n, fn):
    t = _tri(n, fn)
    return jnp.concatenate([t, t], axis=1)


def _sb_logs(z):
    neg_abs = lax.bitcast_convert_type(lax.bitcast_convert_type(z, jnp.int32) | jnp.int32(-2 ** 31), F32)
    ls = jnp.minimum(z, 0.0) - jnp.log(1.0 + jnp.exp(neg_abs))
    return ls, ls - z


PAIR = LANE // HEAD_DIM
FWD_PAIRS = 2
SB_DEAD = -110.0
SB_BLOCK = 256
FOX_DEAD = -160.0
FOX_BLOCK = 512


def _head_lanes(shape, w, axis):
    idx = lax.broadcasted_iota(jnp.int32, shape, axis)
    return (idx >= HEAD_DIM * w) & (idx < HEAD_DIM * (w + 1))


def _bias_rows(w, bq):
    row = lax.broadcasted_iota(jnp.int32, (LANE, bq), 0)
    return jnp.where((row >= 3 * w) & (row < 3 * w + 3), -1.0, 0.0).astype(BF16)


def _merge_pair(parts):
    return jnp.where(_head_lanes(parts[0].shape, 0, 0), parts[0], parts[1]).T


def _smp_fwd(q2, k2, v2, bias, r, causal, name, kstat=None):
    S, C = q2.shape
    Sk = k2.shape[0]
    bq, bk = _attn_blocks(S, Sk, FOX_BLOCK if bias is not None else None)
    nq, nkb, P = S // bq, Sk // bk, C // LANE
    gp = FWD_PAIRS if P % FWD_PAIRS == 0 else 1
    use_f = bias is not None
    if causal:
        assert S == Sk and bq == bk and use_f

    def body(*refs):
        if use_f:
            ks_ref, q_ref, k_ref, v_ref, b_ref, r_ref, o_ref, lse_ref, js_ref = refs
        else:
            q_ref, k_ref, v_ref, o_ref, lse_ref = refs
        i = pl.program_id(1)
        heads = range(PAIR * gp)
        lanes = [slice(LANE * (h // PAIR), LANE * (h // PAIR + 1)) for h in heads]
        qps = [q_ref[:, lanes[h]] for h in heads]
        qTs = [jnp.where(_head_lanes(qps[h].shape, h % PAIR, 1), qps[h], jnp.zeros_like(qps[h])).T for h in heads]
        if use_f:
            qf = [t.astype(F32) for t in qTs]
            qnorm = [jnp.sqrt(jnp.sum(t * t, axis=0, keepdims=True)) for t in qf]
            qTs = [jnp.concatenate([qTs[h], _bias_rows(h % PAIR, bq)], axis=0) for h in heads]

        def blk(j, carry, masked):
            off = pl.multiple_of(j * bk, bk)
            kbs = [k_ref[pl.ds(off, bk), LANE * g:LANE * (g + 1)] for g in range(gp)]
            if use_f:
                kbs = [jnp.concatenate([kbs[g], b_ref[pl.ds(off, bk), LANE * g:LANE * (g + 1)]], axis=1)
                       for g in range(gp)]
            vTbs = [v_ref[pl.ds(off, bk), LANE * g:LANE * (g + 1)].T for g in range(gp)]
            sT = [_dot(kbs[h // PAIR], qTs[h]) * LOG2E for h in heads]
            if masked:
                valid = _valid_t(i, j, bq, bk, False)
                sT = [jnp.where(valid, s, NEG_BIG) for s in sT]
            cm = [jnp.max(s, axis=0, keepdims=True) for s in sT]
            if use_f:
                cm = [cm[h] + r_ref[h] for h in heads]
            m_new = [jnp.maximum(carry[h][0], cm[h]) for h in heads]
            shift = [(m_new[h] - r_ref[h]) if use_f else m_new[h] for h in heads]
            p = [jnp.exp2(sT[h] - shift[h]) for h in heads]
            a = [jnp.exp2(carry[h][0] - m_new[h]) for h in heads]
            l = [a[h] * carry[h][1] + jnp.sum(p[h], axis=0, keepdims=True) for h in heads]
            acc = [a[h] * carry[h][2] + _dot(vTbs[h // PAIR], p[h].astype(BF16)) for h in heads]
            return tuple((m_new[h], l[h], acc[h]) for h in heads)

        def step(jj, state):
            carry, first = state
            j = i - jj
            h0 = pl.program_id(0) * (PAIR * gp)
            bound = [LOG2E * (qnorm[h] * ks_ref[(h0 + h) * nkb + j] - ks_ref[(PAIR * P + h0 + h) * nkb + j])
                     + r_ref[h] - carry[h][0] for h in heads]
            live = jnp.max(functools.reduce(jnp.maximum, bound)) >= FOX_DEAD
            carry = lax.cond(live, lambda cr: blk(j, cr, False), lambda cr: cr, carry)
            return carry, jnp.where(live, j, first)

        carry = tuple((jnp.full((1, bq), NEG_BIG, F32), jnp.zeros((1, bq), F32), jnp.zeros((LANE, bq), F32))
                      for _ in heads)
        if causal:
            carry = blk(i, carry, True)
            carry, first = lax.fori_loop(1, i + 1, step, (carry, i))
            js_ref[0] = jnp.full((1, bq), first, jnp.int32)
        else:
            carry = lax.fori_loop(0, nkb, lambda j, c: blk(j, c, False), carry)
            if use_f:
                js_ref[0] = jnp.zeros((1, bq), jnp.int32)
        for h in heads:
            lse_ref[h] = carry[h][0] + jnp.log2(carry[h][1])
        for g in range(gp):
            o_ref[:, LANE * g:LANE * (g + 1)] = _merge_pair(
                [carry[h][2] / carry[h][1] for h in range(PAIR * g, PAIR * (g + 1))])

    qblk = pl.BlockSpec((bq, LANE * gp), lambda p, i: (i, p))
    kres = pl.BlockSpec((Sk, LANE * gp), lambda p, i: (0, p))
    stat = pl.BlockSpec((PAIR * gp, 1, bq), lambda p, i: (p, 0, i))
    in_specs = [qblk, kres, kres]
    args = [q2, k2, v2]
    out_specs = [qblk, stat]
    out_shape = [jax.ShapeDtypeStruct((S, C), F32), jax.ShapeDtypeStruct((PAIR * P, 1, S), F32)]
    if use_f:
        in_specs = [pl.BlockSpec(memory_space=pltpu.SMEM)] + in_specs + [kres, stat]
        args = [kstat] + args + [bias, r]
        out_specs.append(pl.BlockSpec((1, 1, bq), lambda p, i: (p, 0, i)))
        out_shape.append(jax.ShapeDtypeStruct((P // gp, 1, S), jnp.int32))
    return pl.pallas_call(
        body, name=name, grid=(P // gp, nq), in_specs=in_specs, out_specs=out_specs, out_shape=out_shape,
        compiler_params=_cp(("parallel", "arbitrary"), ATTN_VMEM_LIMIT),
    )(*args)


def _smp_bwd(q2, k2, v2, o2, lse, do2, bias, r, scale, causal, name, first=None):
    S, C = q2.shape
    Sk = k2.shape[0]
    bq, bk = _attn_blocks(S, Sk, FOX_BLOCK if bias is not None else None)
    nq, nkb, P = S // bq, Sk // bk, C // LANE
    use_f = bias is not None

    def body(*refs):
        if use_f:
            (first_ref, q_ref, k_ref, v_ref, o_ref, lse_ref, do_ref, b_ref, r_ref,
             dq_ref, dk_ref, dv_ref, dr_ref, dkey_ref, dk_acc, dv_acc, db_ref) = refs
        else:
            q_ref, k_ref, v_ref, o_ref, lse_ref, do_ref, dq_ref, dk_ref, dv_ref, dk_acc, dv_acc = refs
        i = pl.program_id(1)

        @pl.when(i == 0)
        def _():
            dk_acc[...] = jnp.zeros_like(dk_acc)
            dv_acc[...] = jnp.zeros_like(dv_acc)
            if use_f:
                db_ref[...] = jnp.zeros_like(db_ref)

        qp = q_ref[...]
        dof = do_ref[...]
        prod = dof * o_ref[...]
        heads = range(PAIR)
        mine = [_head_lanes(qp.shape, w, 1) for w in heads]
        qz = [jnp.where(mine[w], qp, jnp.zeros_like(qp)) for w in heads]
        qTs = [qz[w].T for w in heads]
        if use_f:
            qTs = [jnp.concatenate([qTs[w], _bias_rows(w, bq)], axis=0) for w in heads]
        doz = [jnp.where(mine[w], dof, 0.0).astype(BF16) for w in heads]
        doT = [doz[w].T for w in heads]
        delta = [jnp.sum(jnp.where(mine[w], prod, 0.0).T, axis=0, keepdims=True) for w in heads]
        shift = [(lse_ref[w] - r_ref[w]) if use_f else lse_ref[w] for w in heads]

        def blk(j, carry, masked):
            off = pl.multiple_of(j * bk, bk)
            kb = k_ref[pl.ds(off, bk), :]
            kTb = kb.T
            if use_f:
                kb = jnp.concatenate([kb, b_ref[pl.ds(off, bk), :]], axis=1)
            vb = v_ref[pl.ds(off, bk), :]
            sT = [_dot(kb, qTs[w]) * LOG2E for w in heads]
            if masked:
                valid = _valid_t(i, j, bq, bk, False)
                sT = [jnp.where(valid, s, NEG_BIG) for s in sT]
            p = [jnp.exp2(sT[w] - shift[w]) for w in heads]
            dp = [_dot(vb, doT[w]) for w in heads]
            ds = [p[w] * (dp[w] - delta[w]) for w in heads]
            dsb = [d.astype(BF16) for d in ds]
            dvs = [_dot(p[w].astype(BF16), doz[w]) for w in heads]
            dks = [_dot(dsb[w], qz[w]) for w in heads]
            dv_acc[pl.ds(off, bk), :] += dvs[0] + dvs[1]
            dk_acc[pl.ds(off, bk), :] += dks[0] + dks[1]
            dr = [carry[w][1] for w in heads]
            if use_f:
                dr = [dr[w] + jnp.sum(ds[w], axis=0, keepdims=True) for w in heads]
                lane = lax.broadcasted_iota(jnp.int32, (bk, LANE), 1)
                cols = [jnp.where(lane == w, jnp.sum(ds[w], axis=1, keepdims=True), 0.0) for w in heads]
                db_ref[pl.ds(off, bk), :] += cols[0] + cols[1]
            dq = [carry[w][0] + _dot(kTb, dsb[w]) for w in heads]
            return tuple((dq[w], dr[w]) for w in heads)

        carry = tuple((jnp.zeros((LANE, bq), F32), jnp.zeros((1, bq), F32)) for _ in heads)
        if causal:
            start = first_ref[pl.program_id(0) // (P // first.shape[0]), i]
            carry = lax.fori_loop(start, i, lambda j, c: blk(j, c, False), carry)
            carry = blk(i, carry, True)
        else:
            carry = lax.fori_loop(0, nkb, lambda j, c: blk(j, c, False), carry)
        if use_f:
            for w in heads:
                dr_ref[w] = carry[w][1]
        dq_ref[...] = (_merge_pair([carry[w][0] for w in heads]) * scale).astype(BF16)

        @pl.when(i == nq - 1)
        def _():
            dk_ref[...] = dk_acc[...].astype(BF16)
            dv_ref[...] = dv_acc[...].astype(BF16)

        if use_f:
            @pl.when(i == nq - 1)
            def _():
                def chunk(cidx, carry):
                    off = pl.multiple_of(cidx * LANE, LANE)
                    t = db_ref[pl.ds(off, LANE), :].T
                    for w in range(PAIR):
                        dkey_ref[w, :, pl.ds(off, LANE)] = t[w:w + 1, :]
                    return carry

                lax.fori_loop(0, Sk // LANE, chunk, 0)

    qblk = pl.BlockSpec((bq, LANE), lambda p, i: (i, p))
    kres = pl.BlockSpec((Sk, LANE), lambda p, i: (0, p))
    stat = pl.BlockSpec((PAIR, 1, bq), lambda p, i: (p, 0, i))
    in_specs = [qblk, kres, kres, qblk, stat, qblk]
    args = [q2, k2, v2, o2, lse, do2]
    out_specs = [qblk, kres, kres]
    out_shape = [jax.ShapeDtypeStruct((S, C), BF16), jax.ShapeDtypeStruct((Sk, C), BF16),
                 jax.ShapeDtypeStruct((Sk, C), BF16)]
    scratch = [pltpu.VMEM((Sk, LANE), F32), pltpu.VMEM((Sk, LANE), F32)]
    if use_f:
        in_specs = [pl.BlockSpec(memory_space=pltpu.SMEM)] + in_specs + [kres, stat]
        args = [first] + args + [bias, r]
        out_specs += [stat, pl.BlockSpec((PAIR, 1, Sk), lambda p, i: (p, 0, 0))]
        out_shape += [jax.ShapeDtypeStruct((PAIR * P, 1, S), F32), jax.ShapeDtypeStruct((PAIR * P, 1, Sk), F32)]
        scratch.append(pltpu.VMEM((Sk, LANE), F32))
    return pl.pallas_call(
        body, name=name, grid=(P, nq), in_specs=in_specs, out_specs=out_specs, out_shape=out_shape,
        scratch_shapes=scratch, compiler_params=_cp(("parallel", "arbitrary"), ATTN_VMEM_LIMIT),
    )(*args)


def _sbp_fwd(q2, k2, v2, name):
    S, C = q2.shape
    bq, bk = _attn_blocks(S, S, SB_BLOCK)
    assert bq == bk
    nq, P = S // bq, C // LANE
    gp = FWD_PAIRS if P % FWD_PAIRS == 0 else 1
    c = min(CUMSUM_CHUNK, bk)

    def body(q_ref, k_ref, v_ref, o_ref, lt_ref, js_ref):
        i = pl.program_id(1)
        after = _tri2(c, lambda s, j: j > s)
        heads = range(PAIR * gp)
        qps = [q_ref[:, LANE * (h // PAIR):LANE * (h // PAIR + 1)] for h in heads]
        qTs = [jnp.where(_head_lanes(qps[h].shape, h % PAIR, 1), qps[h], jnp.zeros_like(qps[h])).T for h in heads]

        def blk(jj, carry, masked):
            j = i - jj
            off = pl.multiple_of(j * bk, bk)
            kbs = [k_ref[pl.ds(off, bk), LANE * g:LANE * (g + 1)] for g in range(gp)]
            vTbs = [v_ref[pl.ds(off, bk), LANE * g:LANE * (g + 1)].T for g in range(gp)]
            logs = [_sb_logs(_dot(kbs[h // PAIR], qTs[h])) for h in heads]
            ls, lk = [t[0] for t in logs], [t[1] for t in logs]
            if masked:
                valid = _valid_t(i, j, bq, bk, True)
                lk = [jnp.where(valid, t, 0.0) for t in lk]
            cs = [_key_cumsum(lk[h], after, True, carry[h][0]) for h in heads]
            wgt = [jnp.exp(ls[h] + cs[h][0]) for h in heads]
            if masked:
                wgt = [jnp.where(valid, t, 0.0) for t in wgt]
            acc = [carry[h][1] + _dot(vTbs[h // PAIR], wgt[h].astype(BF16)) for h in heads]
            return tuple((carry[h][0] + cs[h][1], acc[h]) for h in heads)

        def step(jj, state):
            carry, first = state
            live = jnp.max(functools.reduce(jnp.maximum, [carry[h][0] for h in heads])) >= SB_DEAD
            carry = lax.cond(live, lambda cr: blk(jj, cr, False), lambda cr: cr, carry)
            return carry, jnp.where(live, i - jj, first)

        carry = tuple((jnp.zeros((1, bq), F32), jnp.zeros((LANE, bq), F32)) for _ in heads)
        carry = blk(0, carry, True)
        carry, first = lax.fori_loop(1, i + 1, step, (carry, i))
        js_ref[0] = jnp.full((1, bq), first, jnp.int32)
        for h in heads:
            lt_ref[h] = carry[h][0]
        for g in range(gp):
            o_ref[:, LANE * g:LANE * (g + 1)] = _merge_pair([carry[h][1] for h in range(PAIR * g, PAIR * (g + 1))])

    qblk = pl.BlockSpec((bq, LANE * gp), lambda p, i: (i, p))
    kres = pl.BlockSpec((S, LANE * gp), lambda p, i: (0, p))
    stat = pl.BlockSpec((PAIR * gp, 1, bq), lambda p, i: (p, 0, i))
    return pl.pallas_call(
        body, name=name, grid=(P // gp, nq),
        in_specs=[qblk, kres, kres],
        out_specs=[qblk, stat, pl.BlockSpec((1, 1, bq), lambda p, i: (p, 0, i))],
        out_shape=[jax.ShapeDtypeStruct((S, C), F32), jax.ShapeDtypeStruct((PAIR * P, 1, S), F32),
                   jax.ShapeDtypeStruct((P // gp, 1, S), jnp.int32)],
        compiler_params=_cp(("parallel", "arbitrary"), ATTN_VMEM_LIMIT),
    )(q2, k2, v2)


def _sbp_bwd(q2, k2, v2, lt, first, do2, scale, name):
    S, C = q2.shape
    bq, bk = _attn_blocks(S, S, SB_BLOCK)
    nq, P = S // bq, C // LANE
    c = min(CUMSUM_CHUNK, bk)

    per_group = P // first.shape[0]

    def body(first_ref, q_ref, k_ref, v_ref, lt_ref, do_ref, dq_ref, dk_ref, dv_ref, dk_acc, dv_acc):
        i = pl.program_id(1)

        @pl.when(i == 0)
        def _():
            dk_acc[...] = jnp.zeros_like(dk_acc)
            dv_acc[...] = jnp.zeros_like(dv_acc)

        qp = q_ref[...]
        dof = do_ref[...]
        upto = _tri2(c, lambda s, j: j <= s)
        before = _tri2(c, lambda s, j: j < s)
        heads = range(PAIR)
        mine = [_head_lanes(qp.shape, w, 1) for w in heads]
        qz = [jnp.where(mine[w], qp, jnp.zeros_like(qp)) for w in heads]
        qTs = [qz[w].T for w in heads]
        doz = [jnp.where(mine[w], dof, 0.0).astype(BF16) for w in heads]
        doT = [doz[w].T for w in heads]
        ltot = [lt_ref[w] for w in heads]

        def blk(j, carry, masked):
            off = pl.multiple_of(j * bk, bk)
            kb = k_ref[pl.ds(off, bk), :]
            vb = v_ref[pl.ds(off, bk), :]
            kTb = kb.T
            logs = [_sb_logs(_dot(kb, qTs[w])) for w in heads]
            ls, lk = [t[0] for t in logs], [t[1] for t in logs]
            if masked:
                valid = _valid_t(i, j, bq, bk, True)
                lk = [jnp.where(valid, t, 0.0) for t in lk]
            pin = [_key_cumsum(lk[w], upto, False, carry[w][1] - ltot[w]) for w in heads]
            wgt = [jnp.exp(ls[w] - pin[w][0]) for w in heads]
            if masked:
                wgt = [jnp.where(valid, t, 0.0) for t in wgt]
            g = [_dot(vb, doT[w]) * wgt[w] for w in heads]
            cin = [_key_cumsum(g[w], before, False, carry[w][2]) for w in heads]
            sig = [jnp.exp(t) for t in ls]
            dz = [g[w] * (1.0 - sig[w]) - cin[w][0] * sig[w] for w in heads]
            if masked:
                dz = [jnp.where(valid, t, 0.0) for t in dz]
            dzb = [t.astype(BF16) for t in dz]
            dvs = [_dot(wgt[w].astype(BF16), doz[w]) for w in heads]
            dks = [_dot(dzb[w], qz[w]) for w in heads]
            dv_acc[pl.ds(off, bk), :] += dvs[0] + dvs[1]
            dk_acc[pl.ds(off, bk), :] += dks[0] + dks[1]
            return tuple((carry[w][0] + _dot(kTb, dzb[w]), carry[w][1] + pin[w][1], carry[w][2] + cin[w][1])
                         for w in heads)

        carry = tuple((jnp.zeros((LANE, bq), F32), jnp.zeros((1, bq), F32), jnp.zeros((1, bq), F32)) for _ in heads)
        start = first_ref[pl.program_id(0) // per_group, i]
        carry = lax.fori_loop(start, i, lambda j, cr: blk(j, cr, False), carry)
        carry = blk(i, carry, True)
        dq_ref[...] = (_merge_pair([carry[w][0] for w in heads]) * scale).astype(BF16)

        @pl.when(i == nq - 1)
        def _():
            dk_ref[...] = dk_acc[...].astype(BF16)
            dv_ref[...] = dv_acc[...].astype(BF16)

    qblk = pl.BlockSpec((bq, LANE), lambda p, i: (i, p))
    kres = pl.BlockSpec((S, LANE), lambda p, i: (0, p))
    stat = pl.BlockSpec((PAIR, 1, bq), lambda p, i: (p, 0, i))
    return pl.pallas_call(
        body, name=name, grid=(P, nq),
        in_specs=[pl.BlockSpec(memory_space=pltpu.SMEM), qblk, kres, kres, stat, qblk],
        out_specs=[qblk, kres, kres],
        out_shape=[jax.ShapeDtypeStruct((S, C), BF16)] * 3,
        scratch_shapes=[pltpu.VMEM((S, LANE), F32), pltpu.VMEM((S, LANE), F32)],
        compiler_params=_cp(("parallel", "arbitrary"), ATTN_VMEM_LIMIT),
    )(first, q2, k2, v2, lt, do2)


def _bias_cols(f_cum):
    H, Sk = f_cum.shape
    terms = jnp.stack(_split3(f_cum), axis=-1)
    packed = terms.reshape(H // PAIR, PAIR, Sk, 3).transpose(2, 0, 1, 3).reshape(Sk, H // PAIR, PAIR * 3)
    return jnp.pad(packed, ((0, 0), (0, 0), (0, LANE - PAIR * 3))).reshape(Sk, -1)


def _make_packed_softmax(name, scale, causal, use_f):
    assert _pow2(scale)

    def run_fwd(q16, k16, v16, f_cum):
        q16 = q16 * scale
        if not use_f:
            o, lse = _smp_fwd(q16, k16, v16, None, None, causal, name + "_fwd")
            return o, (q16, k16, v16, o, lse, None, None, None)
        bq, bk = _attn_blocks(q16.shape[0], k16.shape[0], FOX_BLOCK)
        n_heads = f_cum.shape[0]
        knorm = jnp.sqrt(jnp.sum(jnp.square(k16.astype(F32)).reshape(-1, bk, n_heads, HEAD_DIM), axis=3))
        kstat = jnp.concatenate([jnp.max(knorm, axis=1).T.reshape(-1), f_cum[:, bk - 1::bk].reshape(-1)])
        bias, r = _bias_cols(f_cum), (f_cum * LOG2E)[:, None, :]
        o, lse, first = _smp_fwd(q16, k16, v16, bias, r, causal, name + "_fwd", lax.stop_gradient(kstat))
        return o, (q16, k16, v16, o, lse, bias, r, first[:, 0, ::bq])

    def run_bwd(saved, do):
        q16, k16, v16, o, lse, bias, r, first = saved
        outs = _smp_bwd(q16, k16, v16, o, lse, do, bias, r, scale, causal, name + "_bwd", first)
        if use_f:
            return outs[0], outs[1], outs[2], outs[3][:, 0, :] - outs[4][:, 0, :]
        return tuple(outs)

    if use_f:
        @jax.custom_vjp
        def attn(q, k, v, f_cum):
            return run_fwd(q, k, v, f_cum)[0]

        attn.defvjp(run_fwd, run_bwd)
    else:
        @jax.custom_vjp
        def attn(q, k, v):
            return run_fwd(q, k, v, None)[0]

        attn.defvjp(lambda q, k, v: run_fwd(q, k, v, None), run_bwd)
    return attn


def _make_packed_sb(name, scale):
    assert _pow2(scale)

    def run_fwd(q16, k16, v16):
        q16 = q16 * scale
        o, lt, first = _sbp_fwd(q16, k16, v16, name + "_fwd")
        bq, _ = _attn_blocks(q16.shape[0], q16.shape[0], SB_BLOCK)
        return o, (q16, k16, v16, lt, first[:, 0, ::bq])

    def run_bwd(saved, do):
        q16, k16, v16, lt, first = saved
        return tuple(_sbp_bwd(q16, k16, v16, lt, first, do, scale, name + "_bwd"))

    @jax.custom_vjp
    def attn(q, k, v):
        return run_fwd(q, k, v)[0]

    attn.defvjp(run_fwd, run_bwd)
    return attn


def _round_bf16(x):
    return lax.reduce_precision(x, exponent_bits=8, mantissa_bits=7)


def _split3(x):
    hi = _round_bf16(x)
    mid = _round_bf16(x - hi)
    lo = _round_bf16(x - hi - mid)
    return hi.astype(BF16), mid.astype(BF16), lo.astype(BF16)


def _pow2(x):
    m, _ = math.frexp(x)
    return m == 0.5


def _pad_last(x, n):
    return jnp.pad(x, [(0, 0)] * (x.ndim - 1) + [(0, n - x.shape[-1])])


def _layouts(q, k, scale):
    qh = _pad_last(jnp.transpose(q * scale if _pow2(scale) else q, (1, 0, 2)).astype(BF16), LANE)
    return qh, _pad_last(jnp.transpose(k, (1, 0, 2)).astype(BF16), LANE)


def _make_softmax_attn(name, scale, causal, d):
    pre = _pow2(scale)
    cmul = LOG2E if pre else scale * LOG2E
    gscale = 1.0 if pre else scale

    def run_fwd(q, k, v):
        qn, kn = _layouts(q, k, scale)
        vn = jnp.transpose(v, (1, 0, 2)).astype(BF16)
        oT, lse = _sm_fwd_t(qn, kn, jnp.transpose(vn, (0, 2, 1)), cmul, causal, name + "_fwd")
        return jnp.transpose(oT, (2, 0, 1)), (qn, kn, vn, oT, lse)

    def run_bwd(saved, dout):
        qn, kn, vn, oT, lse = saved
        doT = jnp.transpose(dout, (1, 2, 0))
        do = jnp.transpose(dout, (1, 0, 2)).astype(BF16)
        dq, dk, dv = _sm_bwd_t(qn, kn, vn, oT, lse, doT, do, cmul, gscale, causal, name + "_bwd")
        dq = jnp.transpose(dq[:, :, :d], (1, 0, 2))
        if pre:
            dq = dq * scale
        return dq, jnp.transpose(dk[:, :, :d], (1, 0, 2)), jnp.transpose(dv, (1, 0, 2))

    @jax.custom_vjp
    def attn(q, k, v):
        return run_fwd(q, k, v)[0]

    attn.defvjp(run_fwd, run_bwd)
    return attn


def _rope(x, positions):
    half = x.shape[-1] // 2
    inv_freq = ROPE_THETA ** (-jnp.arange(half, dtype=F32) / half)
    ang = positions.astype(F32)[:, None] * inv_freq[None, :]
    ang = ang.reshape((ang.shape[0],) + (1,) * (x.ndim - 2) + (half,))
    cos, sin = jnp.cos(ang), jnp.sin(ang)
    x1, x2 = x[..., :half], x[..., half:]
    return jnp.concatenate([x1 * cos - x2 * sin, x1 * sin + x2 * cos], axis=-1)


def _permute_cols(w):
    parts = [w[..., _ORIG_OFF[idx]:_ORIG_OFF[idx] + SPLIT_SIZES[idx]] for _, idx in _PERM]
    pad = jnp.zeros(w.shape[:-1] + (PROJ_COLS - IN_COLS,), w.dtype)
    return jnp.concatenate(parts + [pad], axis=-1)


def _unpermute_cols(w):
    start, parts = 0, [None] * len(SPLIT_SIZES)
    for _, idx in _PERM:
        parts[idx] = w[..., start:start + SPLIT_SIZES[idx]]
        start += SPLIT_SIZES[idx]
    return jnp.concatenate(parts, axis=-1)


_BF16_PIECES = ("fq", "fk", "fv", "sq", "sk", "sv", "mq")


def _make_ln_proj(name, has_res):
    def split(proj32, proj16):
        out, off = [], 0
        for n, idx in _PERM:
            src = proj16 if n in _BF16_PIECES else proj32
            out.append(src[:, off:off + SPLIT_SIZES[idx]])
            off += SPLIT_SIZES[idx]
        return tuple(out)

    def run_fwd(x, res, g, b, w):
        h, h16, hT16 = _ln_fwd_call(x, res, g, b, name + "_ln_fwd", also16=True)
        w16 = w.astype(BF16)
        proj32, proj16 = _matmul(h16, w16, "nn", name + "_fwd", also16=True)
        return (h, split(proj32, proj16)), (x, res, g, hT16, w16)

    def run_bwd(saved, cts):
        x, res, g, hT16, w16 = saved
        dh, dpieces = cts
        pad = jnp.zeros((x.shape[0], PROJ_COLS - IN_COLS), BF16)
        dy16 = jnp.concatenate([c.astype(BF16) for c in dpieces] + [pad], axis=1)
        da = _matmul(dy16, w16, "nt", name + "_dx")
        dw = _matmul(hT16, dy16, "nn", name + "_dw")
        outs = _ln_bwd_call(dh, x, res, g, name + "_ln_bwd", dy2=da)
        if has_res:
            dx, dr, dg, db = outs
            return dx, dr, dg.reshape(-1), db.reshape(-1), dw
        dx, dg, db = outs
        return dx, dg.reshape(-1), db.reshape(-1), dw

    if has_res:
        @jax.custom_vjp
        def op(x, res, g, b, w):
            return run_fwd(x, res, g, b, w)[0]

        op.defvjp(run_fwd, run_bwd)
    else:
        @jax.custom_vjp
        def op(x, g, b, w):
            return run_fwd(x, None, g, b, w)[0]

        op.defvjp(lambda x, g, b, w: run_fwd(x, None, g, b, w), run_bwd)

    def call(*args):
        h, pieces = op(*args)
        return h, {n: part for (n, _), part in zip(_PERM, pieces)}

    return call


def _trunk_loss(wts, x2d, mem2d, target2d):
    s = x2d.shape[0]
    positions = jnp.arange(s)
    head_scale = HEAD_DIM ** -0.5
    mla_scale = (MLA_NOPE + MLA_ROPE) ** -0.5

    mem_n = _make_ln("ln_mem", False)(mem2d, wts["mem_ln_g"], wts["mem_ln_b"])
    h, y = None, x2d
    for l in range(DEPTH):
        tag = f"l{l}_"
        w_p = wts["w_in"][l]
        if l == 0:
            h, p = _make_ln_proj(tag + "proj", False)(y, wts["ln_in_g"], wts["ln_in_b"], w_p)
        else:
            h, p = _make_ln_proj(tag + "proj", True)(y, h, wts["ln_g"][l - 1], wts["ln_b"][l - 1], w_p)

        log_f = jax.nn.log_sigmoid(p["f_logit"] + wts["b_forget"][l])
        f_cum = jnp.cumsum(log_f, axis=0).T
        out_fox = _make_packed_softmax(tag + "fox", head_scale, True, True)(p["fq"], p["fk"], p["fv"], f_cum)

        out_sb = _make_packed_sb(tag + "sb", head_scale)(p["sq"], p["sk"], p["sv"])

        cqn = _make_rms(tag + "rms_q")(p["c_q"], wts["mla_q_norm_g"][l])
        q_mla = _make_mm(tag + "q_up")(cqn, wts["w_mla_q_up"][l]).reshape(s, N_HEADS, MLA_NOPE + MLA_ROPE)
        ckvn = _make_rms(tag + "rms_kv")(p["c_kv"], wts["mla_kv_norm_g"][l])
        kv_mla = _make_mm(tag + "kv_up")(ckvn, wts["w_mla_kv_up"][l]).reshape(s, N_HEADS, MLA_NOPE + MLA_V)
        q_full = jnp.concatenate([q_mla[..., :MLA_NOPE], _rope(q_mla[..., MLA_NOPE:], positions)], axis=-1)
        k_rope = jnp.broadcast_to(_rope(p["k_rot"], positions)[:, None, :], (s, N_HEADS, MLA_ROPE))
        k_full = jnp.concatenate([kv_mla[..., :MLA_NOPE], k_rope], axis=-1)
        out_mla = _make_softmax_attn(tag + "mla", mla_scale, True, MLA_NOPE + MLA_ROPE)(
            q_full, k_full, kv_mla[..., MLA_NOPE:]).reshape(s, GROUP_W)

        mkv = _make_mm(tag + "mem_kv")(mem_n, wts["w_mem_kv"][l])
        out_mem = _make_packed_softmax(tag + "mem", head_scale, False, False)(
            p["mq"], mkv[:, :GROUP_W].astype(BF16), mkv[:, GROUP_W:].astype(BF16))

        y = _make_gate_out(tag + "out")((out_fox, out_sb, out_mla, out_mem), p["gate"], wts["w_out"][l])

    h = _make_ln(f"l{DEPTH - 1}_ln", True)(y, h, wts["ln_g"][DEPTH - 1], wts["ln_b"][DEPTH - 1])
    return _loss_op(h, target2d)


def _mesh_pos():
    x, y, c = (lax.axis_index(a) for a in MESH_AXES)
    return x, y, c, 4 * x + 2 * y + c


def _peer(x, y, c, mask):
    return (x ^ ((mask >> 2) & 1), y ^ ((mask >> 1) & 1), c ^ (mask & 1))


_ANY = pl.BlockSpec(memory_space=pl.ANY)


def _all_gather(row_shards, stack_shards):
    n_row, n_all = len(row_shards), len(row_shards) + len(stack_shards)
    shards = list(row_shards) + list(stack_shards)
    chip_masks = (4, 2, 6)
    tensors = range(n_all)

    def body(*refs):
        ins, outs = refs[:n_all], refs[n_all:2 * n_all]
        send_sems, recv_sems, local_sems = refs[2 * n_all:]
        x, y, c, me = _mesh_pos()
        sibling = _peer(x, y, c, 1)

        def window(t, slot):
            if t < n_row:
                rows = shards[t].shape[1]
                return outs[t].at[:, pl.ds(slot * rows, rows), :]
            return outs[t].at[slot]

        def copy(t, k, slot, to, src=None):
            return pltpu.make_async_remote_copy(
                src_ref=window(t, slot) if src is None else src, dst_ref=window(t, slot),
                send_sem=send_sems.at[t, k], recv_sem=recv_sems.at[t, k], device_id=to,
                device_id_type=pl.DeviceIdType.MESH)

        local = [pltpu.make_async_copy(ins[t], window(t, me), local_sems.at[t]) for t in tensors]
        for cp in local:
            cp.start()
        first = [copy(t, 0, me, sibling, src=ins[t]) for t in tensors]
        first += [copy(t, 1 + j, me, _peer(x, y, c, m), src=ins[t]) for j, m in enumerate(chip_masks) for t in tensors]
        for cp in first:
            cp.start()
        passed = []
        for j, m in enumerate(chip_masks):
            for t in tensors:
                copy(t, 1 + j, me ^ m, sibling).wait_recv()
            for t in tensors:
                cp = copy(t, 4 + j, me ^ m, sibling)
                cp.start()
                passed.append(cp)
        for t in tensors:
            copy(t, 0, me ^ 1, sibling).wait_recv()
        for j, m in enumerate(chip_masks):
            for t in tensors:
                copy(t, 4 + j, me ^ m ^ 1, sibling).wait_recv()
        for cp in first + passed:
            cp.wait_send()
        for cp in local:
            cp.wait()

    out_shape = [jax.ShapeDtypeStruct((a.shape[0], N_DEV * a.shape[1], a.shape[2]), a.dtype) for a in row_shards]
    out_shape += [jax.ShapeDtypeStruct((N_DEV,) + a.shape, a.dtype) for a in stack_shards]
    return pl.pallas_call(
        body, name="all_gather_weights", in_specs=[_ANY] * n_all, out_specs=[_ANY] * n_all, out_shape=out_shape,
        scratch_shapes=[pltpu.SemaphoreType.DMA((n_all, N_DEV - 1)), pltpu.SemaphoreType.DMA((n_all, N_DEV - 1)),
                        pltpu.SemaphoreType.DMA((n_all,))],
    )(*shards)


def _reduce_scatter(row_full, stack_full, bcast):
    n_row, n_stack = len(row_full), len(stack_full)
    n_all = n_row + n_stack + len(bcast)
    fulls = list(row_full) + list(stack_full) + list(bcast)

    def body(*refs):
        ins, outs = refs[:n_all], refs[n_all:2 * n_all]
        send_sems, recv_sems, local_sems = refs[2 * n_all:]
        x, y, c, me = _mesh_pos()

        def part(t, slot):
            if t < n_row:
                rows = fulls[t].shape[1] // N_DEV
                return ins[t].at[:, pl.ds(slot * rows, rows), :]
            if t < n_row + n_stack:
                return ins[t].at[slot]
            return ins[t]

        local = [pltpu.make_async_copy(part(t, me), outs[t].at[me], local_sems.at[t]) for t in range(n_all)]
        for cp in local:
            cp.start()
        sends = []
        for mask in range(1, N_DEV):
            for t in range(n_all):
                cp = pltpu.make_async_remote_copy(
                    src_ref=part(t, me ^ mask), dst_ref=outs[t].at[me], send_sem=send_sems.at[t, mask - 1],
                    recv_sem=recv_sems.at[t, mask - 1], device_id=_peer(x, y, c, mask),
                    device_id_type=pl.DeviceIdType.MESH)
                cp.start()
                sends.append(cp)
        for mask in range(1, N_DEV):
            for t in range(n_all):
                pltpu.make_async_remote_copy(
                    src_ref=part(t, me), dst_ref=outs[t].at[me ^ mask], send_sem=send_sems.at[t, mask - 1],
                    recv_sem=recv_sems.at[t, mask - 1], device_id=_peer(x, y, c, mask),
                    device_id_type=pl.DeviceIdType.MESH).wait_recv()
        for cp in sends:
            cp.wait_send()
        for cp in local:
            cp.wait()

    out_shape = [jax.ShapeDtypeStruct((N_DEV, a.shape[0], a.shape[1] // N_DEV, a.shape[2]), a.dtype) for a in row_full]
    out_shape += [jax.ShapeDtypeStruct(a.shape, a.dtype) for a in stack_full]
    out_shape += [jax.ShapeDtypeStruct((N_DEV,) + a.shape, a.dtype) for a in bcast]
    return pl.pallas_call(
        body, name="reduce_scatter_grads", in_specs=[_ANY] * n_all, out_specs=[_ANY] * n_all, out_shape=out_shape,
        scratch_shapes=[pltpu.SemaphoreType.DMA((n_all, N_DEV - 1)), pltpu.SemaphoreType.DMA((n_all, N_DEV - 1)),
                        pltpu.SemaphoreType.DMA((n_all,))],
    )(*fulls)


def _adamw(slots, w, m, v, name):
    shape = w.shape
    cols = shape[-1]
    rows = math.prod(shape[:-1])
    tr = _pick(rows, (64, 32, 16, 8))
    c1 = 1.0 - ADAM_B1 ** ADAM_STEP
    c2 = 1.0 - ADAM_B2 ** ADAM_STEP

    def body(s_ref, w_ref, m_ref, v_ref, g_ref, d_ref, nm_ref, nv_ref):
        g = s_ref[0].astype(F32)
        for k in range(1, N_DEV):
            g = g + s_ref[k].astype(F32)
        nm = ADAM_B1 * m_ref[...] + (1.0 - ADAM_B1) * g
        nv = ADAM_B2 * v_ref[...] + (1.0 - ADAM_B2) * (g * g)
        g_ref[...] = g
        nm_ref[...] = nm
        nv_ref[...] = nv
        d_ref[...] = -ADAM_LR * ((nm / c1) / (jnp.sqrt(nv / c2) + ADAM_EPS) + ADAM_WD * w_ref[...])

    row = pl.BlockSpec((tr, cols), lambda i: (i, 0))
    out = jax.ShapeDtypeStruct((rows, cols), F32)
    outs = pl.pallas_call(
        body, name=name, grid=(rows // tr,),
        in_specs=[pl.BlockSpec((N_DEV, tr, cols), lambda i: (0, i, 0)), row, row, row],
        out_specs=[row] * 4, out_shape=[out] * 4, compiler_params=_cp(("parallel",)),
    )(slots.reshape(N_DEV, rows, cols), w.reshape(rows, cols), m.reshape(rows, cols), v.reshape(rows, cols))
    return [o.reshape(shape) for o in outs]


_SMALL = ("ln_in_g", "ln_in_b", "mem_ln_g", "mem_ln_b", "b_forget", "mla_q_norm_g", "mla_kv_norm_g", "ln_g", "ln_b")
_ORDER = ("ln_in_g", "ln_in_b", "mem_ln_g", "mem_ln_b", "w_in", "b_forget", "mla_q_norm_g", "w_mla_q_up",
          "mla_kv_norm_g", "w_mla_kv_up", "w_mem_kv", "w_out", "ln_g", "ln_b")


def _pack_small(d):
    flat = jnp.concatenate([d[n].reshape(-1) for n in _SMALL])
    n = flat.shape[0]
    padded = ((n + 8 * LANE - 1) // (8 * LANE)) * (8 * LANE)
    return jnp.pad(flat, (0, padded - n)).reshape(-1, LANE)


def _unpack_small(packed, like):
    flat, out, off = packed.reshape(-1), {}, 0
    for n in _SMALL:
        size = math.prod(like[n].shape)
        out[n] = flat[off:off + size].reshape(like[n].shape)
        off += size
    return out


def _unstack_cols(g):
    n, l, r, c = g.shape
    return g.transpose(1, 2, 0, 3).reshape(l, r, n * c)


def _stack_cols(g):
    l, r, nc = g.shape
    return g.reshape(l, r, N_DEV, nc // N_DEV).transpose(2, 0, 1, 3)


def kernel(x, mem, ln_in_g, ln_in_b, mem_ln_g, mem_ln_b, w_in, b_forget, mla_q_norm_g, w_mla_q_up, mla_kv_norm_g, w_mla_kv_up, w_mem_kv, w_out, ln_g, ln_b, loss_target, m_ln_in_g, m_ln_in_b, m_mem_ln_g, m_mem_ln_b, m_w_in, m_b_forget, m_mla_q_norm_g, m_w_mla_q_up, m_mla_kv_norm_g, m_w_mla_kv_up, m_w_mem_kv, m_w_out, m_ln_g, m_ln_b, v_ln_in_g, v_ln_in_b, v_mem_ln_g, v_mem_ln_b, v_w_in, v_b_forget, v_mla_q_norm_g, v_w_mla_q_up, v_mla_kv_norm_g, v_w_mla_kv_up, v_w_mem_kv, v_w_out, v_ln_g, v_ln_b):
    w_shard = dict(ln_in_g=ln_in_g, ln_in_b=ln_in_b, mem_ln_g=mem_ln_g, mem_ln_b=mem_ln_b, w_in=w_in,
                   b_forget=b_forget, mla_q_norm_g=mla_q_norm_g, w_mla_q_up=w_mla_q_up,
                   mla_kv_norm_g=mla_kv_norm_g, w_mla_kv_up=w_mla_kv_up, w_mem_kv=w_mem_kv, w_out=w_out,
                   ln_g=ln_g, ln_b=ln_b)
    m_shard = dict(ln_in_g=m_ln_in_g, ln_in_b=m_ln_in_b, mem_ln_g=m_mem_ln_g, mem_ln_b=m_mem_ln_b, w_in=m_w_in,
                   b_forget=m_b_forget, mla_q_norm_g=m_mla_q_norm_g, w_mla_q_up=m_w_mla_q_up,
                   mla_kv_norm_g=m_mla_kv_norm_g, w_mla_kv_up=m_w_mla_kv_up, w_mem_kv=m_w_mem_kv, w_out=m_w_out,
                   ln_g=m_ln_g, ln_b=m_ln_b)
    v_shard = dict(ln_in_g=v_ln_in_g, ln_in_b=v_ln_in_b, mem_ln_g=v_mem_ln_g, mem_ln_b=v_mem_ln_b, w_in=v_w_in,
                   b_forget=v_b_forget, mla_q_norm_g=v_mla_q_norm_g, w_mla_q_up=v_w_mla_q_up,
                   mla_kv_norm_g=v_mla_kv_norm_g, w_mla_kv_up=v_w_mla_kv_up, w_mem_kv=v_w_mem_kv, w_out=v_w_out,
                   ln_g=v_ln_g, ln_b=v_ln_b)

    to16 = lambda ws: [a.astype(BF16) for a in ws]
    gathered = _all_gather(to16([_permute_cols(w_in), w_mem_kv, w_out]), to16([w_mla_q_up, w_mla_kv_up]))
    g_in, g_mem, g_out, g_qup, g_kvup = [a.astype(F32) for a in gathered]
    full = dict(w_shard)
    full.update(w_in=g_in, w_mem_kv=g_mem, w_out=g_out, w_mla_q_up=_unstack_cols(g_qup),
                w_mla_kv_up=_unstack_cols(g_kvup))

    loss_local, (grad_w, grad_x) = jax.value_and_grad(_trunk_loss, argnums=(0, 1))(
        full, x[0], mem[0], loss_target[0])

    s_in, s_mem, s_out, s_qup, s_kvup, s_small = _reduce_scatter(
        to16([grad_w["w_in"], grad_w["w_mem_kv"], grad_w["w_out"]]),
        to16([_stack_cols(grad_w["w_mla_q_up"]), _stack_cols(grad_w["w_mla_kv_up"])]),
        [_pack_small(grad_w)])

    res = {}
    for name, slots in (("w_mem_kv", s_mem), ("w_out", s_out), ("w_mla_q_up", s_qup), ("w_mla_kv_up", s_kvup)):
        res[name] = _adamw(slots, w_shard[name], m_shard[name], v_shard[name], "adamw_" + name)
    res["w_in"] = [_unpermute_cols(a) for a in _adamw(
        s_in, _permute_cols(w_in), _permute_cols(m_w_in), _permute_cols(v_w_in), "adamw_w_in")]
    small = _adamw(s_small, _pack_small(w_shard), _pack_small(m_shard), _pack_small(v_shard), "adamw_small")
    small = [_unpack_small(a, w_shard) for a in small]
    for name in _SMALL:
        res[name] = [a[name] for a in small]

    loss = lax.psum(loss_local, MESH_AXES)
    outs = [loss, grad_x[None]]
    for k in range(4):
        outs += [res[name][k] for name in _ORDER]
    return tuple(outs)
```

```python
import functools
import math

import jax
import jax.numpy as jnp
from jax import lax
from jax.experimental import pallas as pl
from jax.experimental.pallas import tpu as pltpu

F32 = jnp.float32
BF16 = jnp.bfloat16

D_MODEL = 1024
DEPTH = 2
GROUP_W = 256
N_HEADS = 4
HEAD_DIM = 64
MLA_Q_RANK = 256
MLA_KV_RANK = 128
MLA_NOPE = 64
MLA_ROPE = 32
MLA_V = 64
ROPE_THETA = 10000.0
LN_EPS = 1e-5
RMS_EPS = 1e-6
DEEPNORM_ALPHA = (2 * DEPTH) ** 0.25
SPLIT_SIZES = (256, 256, 256, 4, 256, 256, 256, 256, 128, 32, 256, 1024)
IN_COLS = sum(SPLIT_SIZES)
_ORIG_OFF = [sum(SPLIT_SIZES[:i]) for i in range(len(SPLIT_SIZES))]
_PERM = (("fq", 0), ("fk", 1), ("fv", 2), ("sq", 4), ("sk", 5), ("sv", 6), ("c_q", 7), ("c_kv", 8),
         ("mq", 10), ("gate", 11), ("k_rot", 9), ("f_logit", 3))
LANE = 128
PROJ_COLS = ((IN_COLS + LANE - 1) // LANE) * LANE

ADAM_LR = 0.001
ADAM_B1 = 0.9
ADAM_B2 = 0.999
ADAM_EPS = 1e-08
ADAM_WD = 0.01
ADAM_STEP = 10

N_DEV = 8
MESH_AXES = ("x", "y", "c")
VMEM_LIMIT = 48 * 1024 * 1024
ATTN_VMEM_LIMIT = 56 * 1024 * 1024
ATTN_BQ = 512
ATTN_BK = 512
CUMSUM_CHUNK = 256
NEG_BIG = -1e30
LOG2E = math.log2(math.e)
MM_TM, MM_TN, MM_TK, MM_TK_NT = 1024, 1664, 1024, 3328
CONCAT_MM_TM = 512

_NT = (((1,), (1,)), ((), ()))
_NN = (((1,), (0,)), ((), ()))


def _cp(sem, vmem=VMEM_LIMIT):
    return pltpu.CompilerParams(dimension_semantics=sem, vmem_limit_bytes=vmem)


def _dot(a, b, dims=_NN):
    return lax.dot_general(a, b, dims, preferred_element_type=F32)


def _pick(n, cands):
    for c in cands:
        if c <= n and n % c == 0:
            return c
    return n


def _tile(n, cap):
    if n <= cap:
        return n
    best = None
    for d in range(LANE, cap + 1, LANE):
        if n % d == 0:
            best = d
    assert best is not None, (n, cap)
    return best


def _matmul(a, b, mode, name, also16=False):
    if mode == "nn":
        (M, K), (K2, N) = a.shape, b.shape
    else:
        (M, K), (N, K2) = a.shape, b.shape
    assert K == K2 and a.dtype == BF16 and b.dtype == BF16, (a.shape, b.shape, mode)
    tm, tn = _tile(M, MM_TM), _tile(N, MM_TN)
    tk = _tile(K, MM_TK if mode == "nn" else MM_TK_NT)
    nk = K // tk
    dims = _NN if mode == "nn" else _NT

    def body(a_ref, b_ref, *rest):
        o_ref, acc_ref = rest[0], rest[-1]
        part = _dot(a_ref[...], b_ref[...], dims)
        if nk == 1:
            o_ref[...] = part
            if also16:
                rest[1][...] = part.astype(BF16)
        else:
            assert not also16
            k = pl.program_id(2)

            @pl.when(k == 0)
            def _():
                acc_ref[...] = part

            @pl.when(k > 0)
            def _():
                acc_ref[...] += part

            @pl.when(k == nk - 1)
            def _():
                o_ref[...] = acc_ref[...]

    a_spec = pl.BlockSpec((tm, tk), lambda j, i, k: (i, k))
    if mode == "nn":
        b_spec = pl.BlockSpec((tk, tn), lambda j, i, k: (k, j))
    else:
        b_spec = pl.BlockSpec((tn, tk), lambda j, i, k: (j, k))
    acc_shape = (tm, tn) if nk > 1 else (8, LANE)
    o_spec = pl.BlockSpec((tm, tn), lambda j, i, k: (i, j))
    outs = pl.pallas_call(
        body, name=name, grid=(N // tn, M // tm, nk),
        in_specs=[a_spec, b_spec],
        out_specs=[o_spec, o_spec] if also16 else o_spec,
        out_shape=[jax.ShapeDtypeStruct((M, N), F32), jax.ShapeDtypeStruct((M, N), BF16)] if also16
        else jax.ShapeDtypeStruct((M, N), F32),
        scratch_shapes=[pltpu.VMEM(acc_shape, F32)],
        compiler_params=_cp(("parallel", "parallel", "arbitrary")),
    )(a, b)
    return outs


def _concat_matmul_nt(pieces, b, name):
    M, (N, K) = pieces[0].shape[0], b.shape
    widths = [p.shape[1] for p in pieces]
    assert sum(widths) == K and all(w % LANE == 0 for w in widths) and b.dtype == BF16, (widths, b.shape)
    tm = _tile(M, CONCAT_MM_TM)
    n = len(pieces)

    def body(*refs):
        b_ref, o_ref, a_ref = refs[n:]
        a_ref[...] = jnp.concatenate([r[...].astype(BF16) for r in refs[:n]], axis=1)
        o_ref[...] = _dot(a_ref[...], b_ref[...], _NT)

    rows = lambda w: pl.BlockSpec((tm, w), lambda i: (i, 0))
    return pl.pallas_call(
        body, name=name, grid=(M // tm,),
        in_specs=[rows(w) for w in widths] + [pl.BlockSpec((N, K), lambda i: (0, 0))],
        out_specs=[rows(N), rows(K)],
        out_shape=[jax.ShapeDtypeStruct((M, N), F32), jax.ShapeDtypeStruct((M, K), BF16)],
        compiler_params=_cp(("parallel",)),
    )(*pieces, b)


def _make_mm(name):
    @jax.custom_vjp
    def mm(a, w):
        return _matmul(a.astype(BF16), w.astype(BF16), "nn", name + "_fwd")

    def fwd(a, w):
        a16, w16 = a.astype(BF16), w.astype(BF16)
        return _matmul(a16, w16, "nn", name + "_fwd"), (a16, w16)

    def bwd(res, dy):
        a16, w16 = res
        dy16 = dy.astype(BF16)
        da = _matmul(dy16, w16, "nt", name + "_dx")
        dw = _matmul(a16.T, dy16, "nn", name + "_dw")
        return da, dw

    mm.defvjp(fwd, bwd)
    return mm


def _row_tile(rows):
    return _pick(rows, (512, 256, 128, 64, 32, 16, 8))


def _ln_stats(u):
    mu = jnp.mean(u, axis=-1, keepdims=True)
    d = u - mu
    var = jnp.mean(d * d, axis=-1, keepdims=True)
    return d, lax.rsqrt(var + LN_EPS)


def _ln_fwd_call(x, res, g, b, name, also16=False):
    rows, dm = x.shape
    tr = _row_tile(rows)
    has_res = res is not None
    n_in = 2 if has_res else 1

    def body(*refs):
        if has_res:
            u = DEEPNORM_ALPHA * refs[1][...] + refs[0][...]
        else:
            u = refs[0][...]
        g_ref, b_ref = refs[n_in], refs[n_in + 1]
        d, rstd = _ln_stats(u)
        y = d * rstd * g_ref[...] + b_ref[...]
        refs[n_in + 2][...] = y
        if also16:
            y16 = y.astype(BF16)
            refs[n_in + 3][...] = y16
            refs[n_in + 4][...] = y16.T

    row = pl.BlockSpec((tr, dm), lambda i: (i, 0))
    vec = pl.BlockSpec((1, dm), lambda i: (0, 0))
    args = (x, res) if has_res else (x,)
    out_specs, out_shape = [row], [jax.ShapeDtypeStruct((rows, dm), F32)]
    if also16:
        out_specs += [row, pl.BlockSpec((dm, tr), lambda i: (0, i))]
        out_shape += [jax.ShapeDtypeStruct((rows, dm), BF16), jax.ShapeDtypeStruct((dm, rows), BF16)]
    outs = pl.pallas_call(
        body, name=name, grid=(rows // tr,),
        in_specs=[row] * n_in + [vec, vec], out_specs=out_specs, out_shape=out_shape,
        compiler_params=_cp(("parallel",)),
    )(*args, g.reshape(1, dm), b.reshape(1, dm))
    return outs if also16 else outs[0]


def _ln_bwd_call(dy, x, res, g, name, dy2=None):
    rows, dm = x.shape
    tr = _row_tile(rows)
    has_res = res is not None
    two = dy2 is not None

    def body(*refs):
        dy_ref, refs = refs[0], refs[1:]
        if two:
            dy2_ref, refs = refs[0], refs[1:]
        if has_res:
            x_ref, r_ref, g_ref, dx_ref, dr_ref, dg_ref, db_ref = refs
            u = DEEPNORM_ALPHA * r_ref[...] + x_ref[...]
        else:
            x_ref, g_ref, dx_ref, dg_ref, db_ref = refs
            u = x_ref[...]
        i = pl.program_id(0)
        d, rstd = _ln_stats(u)
        xhat = d * rstd
        dyv = dy_ref[...] + dy2_ref[...] if two else dy_ref[...]
        dxh = dyv * g_ref[...]
        m1 = jnp.mean(dxh, axis=-1, keepdims=True)
        m2 = jnp.mean(dxh * xhat, axis=-1, keepdims=True)
        du = rstd * (dxh - m1 - xhat * m2)
        dx_ref[...] = du
        if has_res:
            dr_ref[...] = DEEPNORM_ALPHA * du
        pg = jnp.sum(dyv * xhat, axis=0, keepdims=True)
        pb = jnp.sum(dyv, axis=0, keepdims=True)

        @pl.when(i == 0)
        def _():
            dg_ref[...] = pg
            db_ref[...] = pb

        @pl.when(i > 0)
        def _():
            dg_ref[...] += pg
            db_ref[...] += pb

    row = pl.BlockSpec((tr, dm), lambda i: (i, 0))
    vec = pl.BlockSpec((1, dm), lambda i: (0, 0))
    big = jax.ShapeDtypeStruct((rows, dm), F32)
    small = jax.ShapeDtypeStruct((1, dm), F32)
    args = ((dy, dy2) if two else (dy,)) + ((x, res) if has_res else (x,))
    n_big = 2 if has_res else 1
    outs = pl.pallas_call(
        body, name=name, grid=(rows // tr,),
        in_specs=[row] * len(args) + [vec],
        out_specs=[row] * n_big + [vec, vec],
        out_shape=[big] * n_big + [small, small],
        compiler_params=_cp(("arbitrary",)),
    )(*args, g.reshape(1, dm))
    return outs


def _make_ln(name, has_res):
    if has_res:
        @jax.custom_vjp
        def ln(x, res, g, b):
            return _ln_fwd_call(x, res, g, b, name + "_fwd")

        def fwd(x, res, g, b):
            return ln(x, res, g, b), (x, res, g)

        def bwd(saved, dy):
            x, res, g = saved
            dx, dr, dg, db = _ln_bwd_call(dy, x, res, g, name + "_bwd")
            return dx, dr, dg.reshape(-1), db.reshape(-1)
    else:
        @jax.custom_vjp
        def ln(x, g, b):
            return _ln_fwd_call(x, None, g, b, name + "_fwd")

        def fwd(x, g, b):
            return ln(x, g, b), (x, g)

        def bwd(saved, dy):
            x, g = saved
            dx, dg, db = _ln_bwd_call(dy, x, None, g, name + "_bwd")
            return dx, dg.reshape(-1), db.reshape(-1)

    ln.defvjp(fwd, bwd)
    return ln


def _rms_fwd_call(x, g, name):
    rows, dm = x.shape
    tr = _row_tile(rows)

    def body(x_ref, g_ref, o_ref):
        xv = x_ref[...]
        rstd = lax.rsqrt(jnp.mean(xv * xv, axis=-1, keepdims=True) + RMS_EPS)
        o_ref[...] = xv * rstd * g_ref[...]

    row = pl.BlockSpec((tr, dm), lambda i: (i, 0))
    vec = pl.BlockSpec((1, dm), lambda i: (0, 0))
    return pl.pallas_call(
        body, name=name, grid=(rows // tr,), in_specs=[row, vec], out_specs=row,
        out_shape=jax.ShapeDtypeStruct((rows, dm), F32), compiler_params=_cp(("parallel",)),
    )(x, g.reshape(1, dm))


def _rms_bwd_call(dy, x, g, name):
    rows, dm = x.shape
    tr = _row_tile(rows)

    def body(dy_ref, x_ref, g_ref, dx_ref, dg_ref):
        i = pl.program_id(0)
        xv = x_ref[...]
        dyv = dy_ref[...]
        rstd = lax.rsqrt(jnp.mean(xv * xv, axis=-1, keepdims=True) + RMS_EPS)
        xhat = xv * rstd
        dxh = dyv * g_ref[...]
        m2 = jnp.mean(dxh * xhat, axis=-1, keepdims=True)
        dx_ref[...] = rstd * (dxh - xhat * m2)
        pg = jnp.sum(dyv * xhat, axis=0, keepdims=True)

        @pl.when(i == 0)
        def _():
            dg_ref[...] = pg

        @pl.when(i > 0)
        def _():
            dg_ref[...] += pg

    row = pl.BlockSpec((tr, dm), lambda i: (i, 0))
    vec = pl.BlockSpec((1, dm), lambda i: (0, 0))
    return pl.pallas_call(
        body, name=name, grid=(rows // tr,), in_specs=[row, row, vec], out_specs=[row, vec],
        out_shape=[jax.ShapeDtypeStruct((rows, dm), F32), jax.ShapeDtypeStruct((1, dm), F32)],
        compiler_params=_cp(("arbitrary",)),
    )(dy, x, g.reshape(1, dm))


def _make_rms(name):
    @jax.custom_vjp
    def rms(x, g):
        return _rms_fwd_call(x, g, name + "_fwd")

    def fwd(x, g):
        return rms(x, g), (x, g)

    def bwd(saved, dy):
        x, g = saved
        dx, dg = _rms_bwd_call(dy, x, g, name + "_bwd")
        return dx, dg.reshape(-1)

    rms.defvjp(fwd, bwd)
    return rms


def _sigmoid(x):
    return 1.0 / (1.0 + jnp.exp(-x))


def _gate_fwd_call(parts, gate, name):
    rows, dm = gate.shape
    tr = _row_tile(rows)
    n = len(parts)

    def body(*refs):
        g_ref, o_ref, oT_ref = refs[n:]
        gv = g_ref[...]
        mixed = jnp.concatenate([r[...] for r in refs[:n]], axis=1)
        y16 = (mixed * (gv * _sigmoid(gv))).astype(BF16)
        o_ref[...] = y16
        oT_ref[...] = y16.T

    row = pl.BlockSpec((tr, dm), lambda i: (i, 0))
    part_specs = [pl.BlockSpec((tr, p.shape[1]), lambda i: (i, 0)) for p in parts]
    return pl.pallas_call(
        body, name=name, grid=(rows // tr,), in_specs=part_specs + [row],
        out_specs=[row, pl.BlockSpec((dm, tr), lambda i: (0, i))],
        out_shape=[jax.ShapeDtypeStruct((rows, dm), BF16), jax.ShapeDtypeStruct((dm, rows), BF16)],
        compiler_params=_cp(("parallel",)),
    )(*parts, gate)


def _gate_bwd_call(dy, parts, gate, name):
    rows, dm = gate.shape
    tr = _row_tile(rows)
    n = len(parts)
    widths = [p.shape[1] for p in parts]

    def body(*refs):
        dy_ref, g_ref = refs[0], refs[n + 1]
        dm_refs, dg_ref = refs[n + 2:2 * n + 2], refs[2 * n + 2]
        gv = g_ref[...]
        dyv = dy_ref[...]
        sg = _sigmoid(gv)
        mixed = jnp.concatenate([r[...] for r in refs[1:n + 1]], axis=1)
        dmixed = dyv * (gv * sg)
        off = 0
        for r, w in zip(dm_refs, widths):
            r[...] = dmixed[:, off:off + w]
            off += w
        dg_ref[...] = dyv * mixed * (sg * (1.0 + gv * (1.0 - sg)))

    row = pl.BlockSpec((tr, dm), lambda i: (i, 0))
    part_specs = [pl.BlockSpec((tr, w), lambda i: (i, 0)) for w in widths]
    return pl.pallas_call(
        body, name=name, grid=(rows // tr,), in_specs=[row] + part_specs + [row], out_specs=part_specs + [row],
        out_shape=[jax.ShapeDtypeStruct((rows, w), F32) for w in widths] + [jax.ShapeDtypeStruct((rows, dm), F32)],
        compiler_params=_cp(("parallel",)),
    )(dy, *parts, gate)


def _make_gate_out(name):
    def run_fwd(parts, gate, w):
        g16, gT16 = _gate_fwd_call(parts, gate, name + "_gate_fwd")
        w16 = w.astype(BF16)
        return _matmul(g16, w16, "nn", name + "_fwd"), (parts, gate, gT16, w16)

    def run_bwd(saved, dy):
        parts, gate, gT16, w16 = saved
        dy16 = dy.astype(BF16)
        dgated = _matmul(dy16, w16, "nt", name + "_dx")
        *dparts, dgate = _gate_bwd_call(dgated, parts, gate, name + "_gate_bwd")
        return tuple(dparts), dgate, _matmul(gT16, dy16, "nn", name + "_dw")

    @jax.custom_vjp
    def gate_out(parts, gate, w):
        return run_fwd(parts, gate, w)[0]

    gate_out.defvjp(run_fwd, run_bwd)
    return gate_out


def _loss_call(y, t, name):
    rows, dm = y.shape
    tr = _row_tile(rows)

    def body(y_ref, t_ref, l_ref, d_ref):
        i = pl.program_id(0)
        e = y_ref[...] - t_ref[...]
        d_ref[...] = e * (1.0 / dm)
        part = 0.5 * jnp.sum(jnp.mean(e * e, axis=-1, keepdims=True), axis=0, keepdims=True)

        @pl.when(i == 0)
        def _():
            l_ref[...] = part

        @pl.when(i > 0)
        def _():
            l_ref[...] += part

    row = pl.BlockSpec((tr, dm), lambda i: (i, 0))
    one = pl.BlockSpec((1, 1), lambda i: (0, 0))
    return pl.pallas_call(
        body, name=name, grid=(rows // tr,), in_specs=[row, row], out_specs=[one, row],
        out_shape=[jax.ShapeDtypeStruct((1, 1), F32), jax.ShapeDtypeStruct((rows, dm), F32)],
        compiler_params=_cp(("arbitrary",)),
    )(y, t)


@jax.custom_vjp
def _loss_op(y, t):
    return _loss_call(y, t, "loss_head")[0][0, 0]


def _loss_fwd(y, t):
    l, d = _loss_call(y, t, "loss_head")
    return l[0, 0], d


def _loss_bwd(d, ct):
    return ct * d, jnp.zeros_like(d)


_loss_op.defvjp(_loss_fwd, _loss_bwd)


def _attn_blocks(S, Sk, cap=None):
    bq, bk = min(cap or ATTN_BQ, S), min(cap or ATTN_BK, Sk)
    assert S % bq == 0 and Sk % bk == 0
    return bq, bk


def _valid_t(i, j, bq, bk, strict):
    key = j * bk + lax.broadcasted_iota(jnp.int32, (bk, bq), 0)
    qry = i * bq + lax.broadcasted_iota(jnp.int32, (bk, bq), 1)
    return (key < qry) if strict else (key <= qry)


def _sm_fwd_t(qn, k, vT, cmul, causal, name):
    H, S, DK = qn.shape
    Sk, dv = k.shape[1], vT.shape[1]
    bq, bk = _attn_blocks(S, Sk)
    nq, nkb = S // bq, Sk // bk
    hb = PAIR * FWD_PAIRS if H % (PAIR * FWD_PAIRS) == 0 else 1
    heads = range(hb)
    if causal:
        assert S == Sk and bq == bk

    def body(qn_ref, k_ref, vT_ref, oT_ref, lse_ref):
        i = pl.program_id(1)
        qTs = [qn_ref[w].T for w in heads]

        def blk(j, carry, masked):
            off = pl.multiple_of(j * bk, bk)
            sT = [_dot(k_ref[w, pl.ds(off, bk), :], qTs[w]) * cmul for w in heads]
            if masked:
                valid = _valid_t(i, j, bq, bk, False)
                sT = [jnp.where(valid, s, NEG_BIG) for s in sT]
            m_new = [jnp.maximum(carry[w][0], jnp.max(sT[w], axis=0, keepdims=True)) for w in heads]
            p = [jnp.exp2(sT[w] - m_new[w]) for w in heads]
            a = [jnp.exp2(carry[w][0] - m_new[w]) for w in heads]
            l = [a[w] * carry[w][1] + jnp.sum(p[w], axis=0, keepdims=True) for w in heads]
            acc = [a[w] * carry[w][2] + _dot(vT_ref[w, :, pl.ds(off, bk)], p[w].astype(BF16)) for w in heads]
            return tuple((m_new[w], l[w], acc[w]) for w in heads)

        carry = tuple((jnp.full((1, bq), NEG_BIG, F32), jnp.zeros((1, bq), F32), jnp.zeros((dv, bq), F32))
                      for _ in heads)
        if causal:
            carry = lax.fori_loop(0, i, lambda j, c: blk(j, c, False), carry)
            carry = blk(i, carry, True)
        else:
            carry = lax.fori_loop(0, nkb, lambda j, c: blk(j, c, False), carry)
        for w in heads:
            oT_ref[w] = carry[w][2] / carry[w][1]
            lse_ref[w] = carry[w][0] + jnp.log2(carry[w][1])

    qcol = lambda d: pl.BlockSpec((hb, d, bq), lambda h, i: (h, 0, i))
    return pl.pallas_call(
        body, name=name, grid=(H // hb, nq),
        in_specs=[pl.BlockSpec((hb, bq, DK), lambda h, i: (h, i, 0)), pl.BlockSpec((hb, Sk, DK), lambda h, i: (h, 0, 0)),
                  pl.BlockSpec((hb, dv, Sk), lambda h, i: (h, 0, 0))],
        out_specs=[qcol(dv), qcol(1)],
        out_shape=[jax.ShapeDtypeStruct((H, dv, S), F32), jax.ShapeDtypeStruct((H, 1, S), F32)],
        compiler_params=_cp(("parallel", "arbitrary"), ATTN_VMEM_LIMIT),
    )(qn, k, vT)


def _sm_bwd_t(qn, k, v, oT, lse, doT, do, cmul, gscale, causal, name):
    H, S, DK = qn.shape
    Sk, dv = k.shape[1], v.shape[2]
    bq, bk = _attn_blocks(S, Sk)
    nq, nkb = S // bq, Sk // bk

    def body(qn_ref, k_ref, v_ref, oT_ref, lse_ref, doT_ref, do_ref, dq_ref, dk_ref, dv_ref):
        i = pl.program_id(1)

        @pl.when(i == 0)
        def _():
            dk_ref[...] = jnp.zeros_like(dk_ref)
            dv_ref[...] = jnp.zeros_like(dv_ref)

        qnb = qn_ref[...]
        qTb = qnb.T
        dob = do_ref[...]
        doTf = doT_ref[...]
        doTb = doTf.astype(BF16)
        delta = jnp.sum(doTf * oT_ref[...], axis=0, keepdims=True)
        lse = lse_ref[...]

        def blk(j, dq, masked):
            off = pl.multiple_of(j * bk, bk)
            kb = k_ref[pl.ds(off, bk), :]
            sT = _dot(kb, qTb) * cmul
            if masked:
                sT = jnp.where(_valid_t(i, j, bq, bk, False), sT, NEG_BIG)
            p = jnp.exp2(sT - lse)
            dp = _dot(v_ref[pl.ds(off, bk), :], doTb)
            ds = p * (dp - delta)
            dsb = (ds * gscale).astype(BF16) if gscale != 1.0 else ds.astype(BF16)
            dv_ref[pl.ds(off, bk), :] += _dot(p.astype(BF16), dob)
            dk_ref[pl.ds(off, bk), :] += _dot(dsb, qnb)
            return dq + _dot(kb.T, dsb)

        dq = jnp.zeros((DK, bq), F32)
        if causal:
            dq = lax.fori_loop(0, i, lambda j, c: blk(j, c, False), dq)
            dq = blk(i, dq, True)
        else:
            dq = lax.fori_loop(0, nkb, lambda j, c: blk(j, c, False), dq)
        dq_ref[...] = dq.T

    qcol = lambda d: pl.BlockSpec((None, d, bq), lambda h, i: (h, 0, i))
    qrow = lambda d: pl.BlockSpec((None, bq, d), lambda h, i: (h, i, 0))
    krow = lambda d: pl.BlockSpec((None, Sk, d), lambda h, i: (h, 0, 0))
    return pl.pallas_call(
        body, name=name, grid=(H, nq),
        in_specs=[qrow(DK), krow(DK), krow(dv), qcol(dv), qcol(1), qcol(dv), qrow(dv)],
        out_specs=[qrow(DK), krow(DK), krow(dv)],
        out_shape=[jax.ShapeDtypeStruct((H, S, DK), F32), jax.ShapeDtypeStruct((H, Sk, DK), F32),
                   jax.ShapeDtypeStruct((H, Sk, dv), F32)],
        compiler_params=_cp(("parallel", "arbitrary"), ATTN_VMEM_LIMIT),
    )(qn, k, v, oT, lse, doT, do)


def _tri(n, fn):
    r = lax.broadcasted_iota(jnp.int32, (n, n), 0)
    c = lax.broadcasted_iota(jnp.int32, (n, n), 1)
    return jnp.where(fn(r, c), 1.0, 0.0).astype(BF16)


def _key_cumsum(x, tri2, suffix, base):
    bk = x.shape[0]
    c = min(CUMSUM_CHUNK, bk)
    n = bk // c
    hi32 = lax.bitcast_convert_type(lax.bitcast_convert_type(x, jnp.int32) & jnp.int32(-65536), F32)
    hi = hi32.astype(BF16)
    lo = (x - hi32).astype(BF16)
    tot = [jnp.sum(x[a * c:(a + 1) * c], axis=0, keepdims=True) for a in range(n)]
    outs = []
    for a in range(n):
        row = base
        for t in (tot[a + 1:] if suffix else tot[:a]):
            row = row + t
        stacked = jnp.concatenate([hi[a * c:(a + 1) * c], lo[a * c:(a + 1) * c]], axis=0)
        outs.append(_dot(tri2, stacked) + row)
    total = tot[0]
    for t in tot[1:]:
        total = total + t
    return (outs[0] if n == 1 else jnp.concatenate(outs, axis=0)), total


def _tri2(n, fn):
    t = _tri(n, fn)
    return jnp.concatenate([t, t], axis=1)


def _sb_logs(z):
    neg_abs = lax.bitcast_convert_type(lax.bitcast_convert_type(z, jnp.int32) | jnp.int32(-2 ** 31), F32)
    ls = jnp.minimum(z, 0.0) - jnp.log(1.0 + jnp.exp(neg_abs))
    return ls, ls - z


PAIR = LANE // HEAD_DIM
FWD_PAIRS = 2
SB_DEAD = -110.0
SB_BLOCK = 256
FOX_DEAD = -160.0
FOX_BLOCK = 512


def _head_lanes(shape, w, axis):
    idx = lax.broadcasted_iota(jnp.int32, shape, axis)
    return (idx >= HEAD_DIM * w) & (idx < HEAD_DIM * (w + 1))


def _bias_rows(w, bq):
    row = lax.broadcasted_iota(jnp.int32, (LANE, bq), 0)
    return jnp.where((row >= 3 * w) & (row < 3 * w + 3), -1.0, 0.0).astype(BF16)


def _merge_pair(parts):
    return jnp.where(_head_lanes(parts[0].shape, 0, 0), parts[0], parts[1]).T


def _smp_fwd(q2, k2, v2, bias, r, causal, name, kstat=None):
    S, C = q2.shape
    Sk = k2.shape[0]
    bq, bk = _attn_blocks(S, Sk, FOX_BLOCK if bias is not None else None)
    nq, nkb, P = S // bq, Sk // bk, C // LANE
    gp = FWD_PAIRS if P % FWD_PAIRS == 0 else 1
    use_f = bias is not None
    if causal:
        assert S == Sk and bq == bk and use_f

    def body(*refs):
        if use_f:
            ks_ref, q_ref, k_ref, v_ref, b_ref, r_ref, o_ref, lse_ref, js_ref = refs
        else:
            q_ref, k_ref, v_ref, o_ref, lse_ref = refs
        i = pl.program_id(1)
        heads = range(PAIR * gp)
        lanes = [slice(LANE * (h // PAIR), LANE * (h // PAIR + 1)) for h in heads]
        qps = [q_ref[:, lanes[h]] for h in heads]
        qTs = [jnp.where(_head_lanes(qps[h].shape, h % PAIR, 1), qps[h], jnp.zeros_like(qps[h])).T for h in heads]
        if use_f:
            qf = [t.astype(F32) for t in qTs]
            qnorm = [jnp.sqrt(jnp.sum(t * t, axis=0, keepdims=True)) for t in qf]
            qTs = [jnp.concatenate([qTs[h], _bias_rows(h % PAIR, bq)], axis=0) for h in heads]

        def blk(j, carry, masked):
            off = pl.multiple_of(j * bk, bk)
            kbs = [k_ref[pl.ds(off, bk), LANE * g:LANE * (g + 1)] for g in range(gp)]
            if use_f:
                kbs = [jnp.concatenate([kbs[g], b_ref[pl.ds(off, bk), LANE * g:LANE * (g + 1)]], axis=1)
                       for g in range(gp)]
            vTbs = [v_ref[pl.ds(off, bk), LANE * g:LANE * (g + 1)].T for g in range(gp)]
            sT = [_dot(kbs[h // PAIR], qTs[h]) * LOG2E for h in heads]
            if masked:
                valid = _valid_t(i, j, bq, bk, False)
                sT = [jnp.where(valid, s, NEG_BIG) for s in sT]
            cm = [jnp.max(s, axis=0, keepdims=True) for s in sT]
            if use_f:
                cm = [cm[h] + r_ref[h] for h in heads]
            m_new = [jnp.maximum(carry[h][0], cm[h]) for h in heads]
            shift = [(m_new[h] - r_ref[h]) if use_f else m_new[h] for h in heads]
            p = [jnp.exp2(sT[h] - shift[h]) for h in heads]
            a = [jnp.exp2(carry[h][0] - m_new[h]) for h in heads]
            l = [a[h] * carry[h][1] + jnp.sum(p[h], axis=0, keepdims=True) for h in heads]
            acc = [a[h] * carry[h][2] + _dot(vTbs[h // PAIR], p[h].astype(BF16)) for h in heads]
            return tuple((m_new[h], l[h], acc[h]) for h in heads)

        def step(jj, state):
            carry, first = state
            j = i - jj
            h0 = pl.program_id(0) * (PAIR * gp)
            bound = [LOG2E * (qnorm[h] * ks_ref[(h0 + h) * nkb + j] - ks_ref[(PAIR * P + h0 + h) * nkb + j])
                     + r_ref[h] - carry[h][0] for h in heads]
            live = jnp.max(functools.reduce(jnp.maximum, bound)) >= FOX_DEAD
            carry = lax.cond(live, lambda cr: blk(j, cr, False), lambda cr: cr, carry)
            return carry, jnp.where(live, j, first)

        carry = tuple((jnp.full((1, bq), NEG_BIG, F32), jnp.zeros((1, bq), F32), jnp.zeros((LANE, bq), F32))
                      for _ in heads)
        if causal:
            carry = blk(i, carry, True)
            carry, first = lax.fori_loop(1, i + 1, step, (carry, i))
            js_ref[0] = jnp.full((1, bq), first, jnp.int32)
        else:
            carry = lax.fori_loop(0, nkb, lambda j, c: blk(j, c, False), carry)
            if use_f:
                js_ref[0] = jnp.zeros((1, bq), jnp.int32)
        for h in heads:
            lse_ref[h] = carry[h][0] + jnp.log2(carry[h][1])
        for g in range(gp):
            o_ref[:, LANE * g:LANE * (g + 1)] = _merge_pair(
                [carry[h][2] / carry[h][1] for h in range(PAIR * g, PAIR * (g + 1))])

    qblk = pl.BlockSpec((bq, LANE * gp), lambda p, i: (i, p))
    kres = pl.BlockSpec((Sk, LANE * gp), lambda p, i: (0, p))
    stat = pl.BlockSpec((PAIR * gp, 1, bq), lambda p, i: (p, 0, i))
    in_specs = [qblk, kres, kres]
    args = [q2, k2, v2]
    out_specs = [qblk, stat]
    out_shape = [jax.ShapeDtypeStruct((S, C), F32), jax.ShapeDtypeStruct((PAIR * P, 1, S), F32)]
    if use_f:
        in_specs = [pl.BlockSpec(memory_space=pltpu.SMEM)] + in_specs + [kres, stat]
        args = [kstat] + args + [bias, r]
        out_specs.append(pl.BlockSpec((1, 1, bq), lambda p, i: (p, 0, i)))
        out_shape.append(jax.ShapeDtypeStruct((P // gp, 1, S), jnp.int32))
    return pl.pallas_call(
        body, name=name, grid=(P // gp, nq), in_specs=in_specs, out_specs=out_specs, out_shape=out_shape,
        compiler_params=_cp(("parallel", "arbitrary"), ATTN_VMEM_LIMIT),
    )(*args)


def _smp_bwd(q2, k2, v2, o2, lse, do2, bias, r, scale, causal, name, first=None):
    S, C = q2.shape
    Sk = k2.shape[0]
    bq, bk = _attn_blocks(S, Sk, FOX_BLOCK if bias is not None else None)
    nq, nkb, P = S // bq, Sk // bk, C // LANE
    use_f = bias is not None

    def body(*refs):
        if use_f:
            (first_ref, q_ref, k_ref, v_ref, o_ref, lse_ref, do_ref, b_ref, r_ref,
             dq_ref, dk_ref, dv_ref, dr_ref, dkey_ref, dk_acc, dv_acc, db_ref) = refs
        else:
            q_ref, k_ref, v_ref, o_ref, lse_ref, do_ref, dq_ref, dk_ref, dv_ref, dk_acc, dv_acc = refs
        i = pl.program_id(1)

        @pl.when(i == 0)
        def _():
            dk_acc[...] = jnp.zeros_like(dk_acc)
            dv_acc[...] = jnp.zeros_like(dv_acc)
            if use_f:
                db_ref[...] = jnp.zeros_like(db_ref)

        qp = q_ref[...]
        dof = do_ref[...]
        prod = dof * o_ref[...]
        heads = range(PAIR)
        mine = [_head_lanes(qp.shape, w, 1) for w in heads]
        qz = [jnp.where(mine[w], qp, jnp.zeros_like(qp)) for w in heads]
        qTs = [qz[w].T for w in heads]
        if use_f:
            qTs = [jnp.concatenate([qTs[w], _bias_rows(w, bq)], axis=0) for w in heads]
        doz = [jnp.where(mine[w], dof, 0.0).astype(BF16) for w in heads]
        doT = [doz[w].T for w in heads]
        delta = [jnp.sum(jnp.where(mine[w], prod, 0.0).T, axis=0, keepdims=True) for w in heads]
        shift = [(lse_ref[w] - r_ref[w]) if use_f else lse_ref[w] for w in heads]

        def blk(j, carry, masked):
            off = pl.multiple_of(j * bk, bk)
            kb = k_ref[pl.ds(off, bk), :]
            kTb = kb.T
            if use_f:
                kb = jnp.concatenate([kb, b_ref[pl.ds(off, bk), :]], axis=1)
            vb = v_ref[pl.ds(off, bk), :]
            sT = [_dot(kb, qTs[w]) * LOG2E for w in heads]
            if masked:
                valid = _valid_t(i, j, bq, bk, False)
                sT = [jnp.where(valid, s, NEG_BIG) for s in sT]
            p = [jnp.exp2(sT[w] - shift[w]) for w in heads]
            dp = [_dot(vb, doT[w]) for w in heads]
            ds = [p[w] * (dp[w] - delta[w]) for w in heads]
            dsb = [d.astype(BF16) for d in ds]
            dvs = [_dot(p[w].astype(BF16), doz[w]) for w in heads]
            dks = [_dot(dsb[w], qz[w]) for w in heads]
            dv_acc[pl.ds(off, bk), :] += dvs[0] + dvs[1]
            dk_acc[pl.ds(off, bk), :] += dks[0] + dks[1]
            dr = [carry[w][1] for w in heads]
            if use_f:
                dr = [dr[w] + jnp.sum(ds[w], axis=0, keepdims=True) for w in heads]
                lane = lax.broadcasted_iota(jnp.int32, (bk, LANE), 1)
                cols = [jnp.where(lane == w, jnp.sum(ds[w], axis=1, keepdims=True), 0.0) for w in heads]
                db_ref[pl.ds(off, bk), :] += cols[0] + cols[1]
            dq = [carry[w][0] + _dot(kTb, dsb[w]) for w in heads]
            return tuple((dq[w], dr[w]) for w in heads)

        carry = tuple((jnp.zeros((LANE, bq), F32), jnp.zeros((1, bq), F32)) for _ in heads)
        if causal:
            start = first_ref[pl.program_id(0) // (P // first.shape[0]), i]
            carry = lax.fori_loop(start, i, lambda j, c: blk(j, c, False), carry)
            carry = blk(i, carry, True)
        else:
            carry = lax.fori_loop(0, nkb, lambda j, c: blk(j, c, False), carry)
        if use_f:
            for w in heads:
                dr_ref[w] = carry[w][1]
        dq_ref[...] = (_merge_pair([carry[w][0] for w in heads]) * scale).astype(BF16)

        @pl.when(i == nq - 1)
        def _():
            dk_ref[...] = dk_acc[...].astype(BF16)
            dv_ref[...] = dv_acc[...].astype(BF16)

        if use_f:
            @pl.when(i == nq - 1)
            def _():
                def chunk(cidx, carry):
                    off = pl.multiple_of(cidx * LANE, LANE)
                    t = db_ref[pl.ds(off, LANE), :].T
                    for w in range(PAIR):
                        dkey_ref[w, :, pl.ds(off, LANE)] = t[w:w + 1, :]
                    return carry

                lax.fori_loop(0, Sk // LANE, chunk, 0)

    qblk = pl.BlockSpec((bq, LANE), lambda p, i: (i, p))
    kres = pl.BlockSpec((Sk, LANE), lambda p, i: (0, p))
    stat = pl.BlockSpec((PAIR, 1, bq), lambda p, i: (p, 0, i))
    in_specs = [qblk, kres, kres, qblk, stat, qblk]
    args = [q2, k2, v2, o2, lse, do2]
    out_specs = [qblk, kres, kres]
    out_shape = [jax.ShapeDtypeStruct((S, C), BF16), jax.ShapeDtypeStruct((Sk, C), BF16),
                 jax.ShapeDtypeStruct((Sk, C), BF16)]
    scratch = [pltpu.VMEM((Sk, LANE), F32), pltpu.VMEM((Sk, LANE), F32)]
    if use_f:
        in_specs = [pl.BlockSpec(memory_space=pltpu.SMEM)] + in_specs + [kres, stat]
        args = [first] + args + [bias, r]
        out_specs += [stat, pl.BlockSpec((PAIR, 1, Sk), lambda p, i: (p, 0, 0))]
        out_shape += [jax.ShapeDtypeStruct((PAIR * P, 1, S), F32), jax.ShapeDtypeStruct((PAIR * P, 1, Sk), F32)]
        scratch.append(pltpu.VMEM((Sk, LANE), F32))
    return pl.pallas_call(
        body, name=name, grid=(P, nq), in_specs=in_specs, out_specs=out_specs, out_shape=out_shape,
        scratch_shapes=scratch, compiler_params=_cp(("parallel", "arbitrary"), ATTN_VMEM_LIMIT),
    )(*args)


def _sbp_fwd(q2, k2, v2, name):
    S, C = q2.shape
    bq, bk = _attn_blocks(S, S, SB_BLOCK)
    assert bq == bk
    nq, P = S // bq, C // LANE
    gp = FWD_PAIRS if P % FWD_PAIRS == 0 else 1
    c = min(CUMSUM_CHUNK, bk)

    def body(q_ref, k_ref, v_ref, o_ref, lt_ref, js_ref):
        i = pl.program_id(1)
        after = _tri2(c, lambda s, j: j > s)
        heads = range(PAIR * gp)
        qps = [q_ref[:, LANE * (h // PAIR):LANE * (h // PAIR + 1)] for h in heads]
        qTs = [jnp.where(_head_lanes(qps[h].shape, h % PAIR, 1), qps[h], jnp.zeros_like(qps[h])).T for h in heads]

        def blk(jj, carry, masked):
            j = i - jj
            off = pl.multiple_of(j * bk, bk)
            kbs = [k_ref[pl.ds(off, bk), LANE * g:LANE * (g + 1)] for g in range(gp)]
            vTbs = [v_ref[pl.ds(off, bk), LANE * g:LANE * (g + 1)].T for g in range(gp)]
            logs = [_sb_logs(_dot(kbs[h // PAIR], qTs[h])) for h in heads]
            ls, lk = [t[0] for t in logs], [t[1] for t in logs]
            if masked:
                valid = _valid_t(i, j, bq, bk, True)
                lk = [jnp.where(valid, t, 0.0) for t in lk]
            cs = [_key_cumsum(lk[h], after, True, carry[h][0]) for h in heads]
            wgt = [jnp.exp(ls[h] + cs[h][0]) for h in heads]
            if masked:
                wgt = [jnp.where(valid, t, 0.0) for t in wgt]
            acc = [carry[h][1] + _dot(vTbs[h // PAIR], wgt[h].astype(BF16)) for h in heads]
            return tuple((carry[h][0] + cs[h][1], acc[h]) for h in heads)

        def step(jj, state):
            carry, first = state
            live = jnp.max(functools.reduce(jnp.maximum, [carry[h][0] for h in heads])) >= SB_DEAD
            carry = lax.cond(live, lambda cr: blk(jj, cr, False), lambda cr: cr, carry)
            return carry, jnp.where(live, i - jj, first)

        carry = tuple((jnp.zeros((1, bq), F32), jnp.zeros((LANE, bq), F32)) for _ in heads)
        carry = blk(0, carry, True)
        carry, first = lax.fori_loop(1, i + 1, step, (carry, i))
        js_ref[0] = jnp.full((1, bq), first, jnp.int32)
        for h in heads:
            lt_ref[h] = carry[h][0]
        for g in range(gp):
            o_ref[:, LANE * g:LANE * (g + 1)] = _merge_pair([carry[h][1] for h in range(PAIR * g, PAIR * (g + 1))])

    qblk = pl.BlockSpec((bq, LANE * gp), lambda p, i: (i, p))
    kres = pl.BlockSpec((S, LANE * gp), lambda p, i: (0, p))
    stat = pl.BlockSpec((PAIR * gp, 1, bq), lambda p, i: (p, 0, i))
    return pl.pallas_call(
        body, name=name, grid=(P // gp, nq),
        in_specs=[qblk, kres, kres],
        out_specs=[qblk, stat, pl.BlockSpec((1, 1, bq), lambda p, i: (p, 0, i))],
        out_shape=[jax.ShapeDtypeStruct((S, C), F32), jax.ShapeDtypeStruct((PAIR * P, 1, S), F32),
                   jax.ShapeDtypeStruct((P // gp, 1, S), jnp.int32)],
        compiler_params=_cp(("parallel", "arbitrary"), ATTN_VMEM_LIMIT),
    )(q2, k2, v2)


def _sbp_bwd(q2, k2, v2, lt, first, do2, scale, name):
    S, C = q2.shape
    bq, bk = _attn_blocks(S, S, SB_BLOCK)
    nq, P = S // bq, C // LANE
    c = min(CUMSUM_CHUNK, bk)

    per_group = P // first.shape[0]

    def body(first_ref, q_ref, k_ref, v_ref, lt_ref, do_ref, dq_ref, dk_ref, dv_ref, dk_acc, dv_acc):
        i = pl.program_id(1)

        @pl.when(i == 0)
        def _():
            dk_acc[...] = jnp.zeros_like(dk_acc)
            dv_acc[...] = jnp.zeros_like(dv_acc)

        qp = q_ref[...]
        dof = do_ref[...]
        upto = _tri2(c, lambda s, j: j <= s)
        before = _tri2(c, lambda s, j: j < s)
        heads = range(PAIR)
        mine = [_head_lanes(qp.shape, w, 1) for w in heads]
        qz = [jnp.where(mine[w], qp, jnp.zeros_like(qp)) for w in heads]
        qTs = [qz[w].T for w in heads]
        doz = [jnp.where(mine[w], dof, 0.0).astype(BF16) for w in heads]
        doT = [doz[w].T for w in heads]
        ltot = [lt_ref[w] for w in heads]

        def blk(j, carry, masked):
            off = pl.multiple_of(j * bk, bk)
            kb = k_ref[pl.ds(off, bk), :]
            vb = v_ref[pl.ds(off, bk), :]
            kTb = kb.T
            logs = [_sb_logs(_dot(kb, qTs[w])) for w in heads]
            ls, lk = [t[0] for t in logs], [t[1] for t in logs]
            if masked:
                valid = _valid_t(i, j, bq, bk, True)
                lk = [jnp.where(valid, t, 0.0) for t in lk]
            pin = [_key_cumsum(lk[w], upto, False, carry[w][1] - ltot[w]) for w in heads]
            wgt = [jnp.exp(ls[w] - pin[w][0]) for w in heads]
            if masked:
                wgt = [jnp.where(valid, t, 0.0) for t in wgt]
            g = [_dot(vb, doT[w]) * wgt[w] for w in heads]
            cin = [_key_cumsum(g[w], before, False, carry[w][2]) for w in heads]
            sig = [jnp.exp(t) for t in ls]
            dz = [g[w] * (1.0 - sig[w]) - cin[w][0] * sig[w] for w in heads]
            if masked:
                dz = [jnp.where(valid, t, 0.0) for t in dz]
            dzb = [t.astype(BF16) for t in dz]
            dvs = [_dot(wgt[w].astype(BF16), doz[w]) for w in heads]
            dks = [_dot(dzb[w], qz[w]) for w in heads]
            dv_acc[pl.ds(off, bk), :] += dvs[0] + dvs[1]
            dk_acc[pl.ds(off, bk), :] += dks[0] + dks[1]
            return tuple((carry[w][0] + _dot(kTb, dzb[w]), carry[w][1] + pin[w][1], carry[w][2] + cin[w][1])
                         for w in heads)

        carry = tuple((jnp.zeros((LANE, bq), F32), jnp.zeros((1, bq), F32), jnp.zeros((1, bq), F32)) for _ in heads)
        start = first_ref[pl.program_id(0) // per_group, i]
        carry = lax.fori_loop(start, i, lambda j, cr: blk(j, cr, False), carry)
        carry = blk(i, carry, True)
        dq_ref[...] = (_merge_pair([carry[w][0] for w in heads]) * scale).astype(BF16)

        @pl.when(i == nq - 1)
        def _():
            dk_ref[...] = dk_acc[...].astype(BF16)
            dv_ref[...] = dv_acc[...].astype(BF16)

    qblk = pl.BlockSpec((bq, LANE), lambda p, i: (i, p))
    kres = pl.BlockSpec((S, LANE), lambda p, i: (0, p))
    stat = pl.BlockSpec((PAIR, 1, bq), lambda p, i: (p, 0, i))
    return pl.pallas_call(
        body, name=name, grid=(P, nq),
        in_specs=[pl.BlockSpec(memory_space=pltpu.SMEM), qblk, kres, kres, stat, qblk],
        out_specs=[qblk, kres, kres],
        out_shape=[jax.ShapeDtypeStruct((S, C), BF16)] * 3,
        scratch_shapes=[pltpu.VMEM((S, LANE), F32), pltpu.VMEM((S, LANE), F32)],
        compiler_params=_cp(("parallel", "arbitrary"), ATTN_VMEM_LIMIT),
    )(first, q2, k2, v2, lt, do2)


def _bias_cols(f_cum):
    H, Sk = f_cum.shape
    terms = jnp.stack(_split3(f_cum), axis=-1)
    packed = terms.reshape(H // PAIR, PAIR, Sk, 3).transpose(2, 0, 1, 3).reshape(Sk, H // PAIR, PAIR * 3)
    return jnp.pad(packed, ((0, 0), (0, 0), (0, LANE - PAIR * 3))).reshape(Sk, -1)


def _make_packed_softmax(name, scale, causal, use_f):
    assert _pow2(scale)

    def run_fwd(q16, k16, v16, f_cum):
        q16 = q16 * scale
        if not use_f:
            o, lse = _smp_fwd(q16, k16, v16, None, None, causal, name + "_fwd")
            return o, (q16, k16, v16, o, lse, None, None, None)
        bq, bk = _attn_blocks(q16.shape[0], k16.shape[0], FOX_BLOCK)
        n_heads = f_cum.shape[0]
        knorm = jnp.sqrt(jnp.sum(jnp.square(k16.astype(F32)).reshape(-1, bk, n_heads, HEAD_DIM), axis=3))
        kstat = jnp.concatenate([jnp.max(knorm, axis=1).T.reshape(-1), f_cum[:, bk - 1::bk].reshape(-1)])
        bias, r = _bias_cols(f_cum), (f_cum * LOG2E)[:, None, :]
        o, lse, first = _smp_fwd(q16, k16, v16, bias, r, causal, name + "_fwd", lax.stop_gradient(kstat))
        return o, (q16, k16, v16, o, lse, bias, r, first[:, 0, ::bq])

    def run_bwd(saved, do):
        q16, k16, v16, o, lse, bias, r, first = saved
        outs = _smp_bwd(q16, k16, v16, o, lse, do, bias, r, scale, causal, name + "_bwd", first)
        if use_f:
            return outs[0], outs[1], outs[2], outs[3][:, 0, :] - outs[4][:, 0, :]
        return tuple(outs)

    if use_f:
        @jax.custom_vjp
        def attn(q, k, v, f_cum):
            return run_fwd(q, k, v, f_cum)[0]

        attn.defvjp(run_fwd, run_bwd)
    else:
        @jax.custom_vjp
        def attn(q, k, v):
            return run_fwd(q, k, v, None)[0]

        attn.defvjp(lambda q, k, v: run_fwd(q, k, v, None), run_bwd)
    return attn


def _make_packed_sb(name, scale):
    assert _pow2(scale)

    def run_fwd(q16, k16, v16):
        q16 = q16 * scale
        o, lt, first = _sbp_fwd(q16, k16, v16, name + "_fwd")
        bq, _ = _attn_blocks(q16.shape[0], q16.shape[0], SB_BLOCK)
        return o, (q16, k16, v16, lt, first[:, 0, ::bq])

    def run_bwd(saved, do):
        q16, k16, v16, lt, first = saved
        return tuple(_sbp_bwd(q16, k16, v16, lt, first, do, scale, name + "_bwd"))

    @jax.custom_vjp
    def attn(q, k, v):
        return run_fwd(q, k, v)[0]

    attn.defvjp(run_fwd, run_bwd)
    return attn


def _round_bf16(x):
    return lax.reduce_precision(x, exponent_bits=8, mantissa_bits=7)


def _split3(x):
    hi = _round_bf16(x)
    mid = _round_bf16(x - hi)
    lo = _round_bf16(x - hi - mid)
    return hi.astype(BF16), mid.astype(BF16), lo.astype(BF16)


def _pow2(x):
    m, _ = math.frexp(x)
    return m == 0.5


def _pad_last(x, n):
    return jnp.pad(x, [(0, 0)] * (x.ndim - 1) + [(0, n - x.shape[-1])])


def _layouts(q, k, scale):
    qh = _pad_last(jnp.transpose(q * scale if _pow2(scale) else q, (1, 0, 2)).astype(BF16), LANE)
    return qh, _pad_last(jnp.transpose(k, (1, 0, 2)).astype(BF16), LANE)


def _make_softmax_attn(name, scale, causal, d):
    pre = _pow2(scale)
    cmul = LOG2E if pre else scale * LOG2E
    gscale = 1.0 if pre else scale

    def run_fwd(q, k, v):
        qn, kn = _layouts(q, k, scale)
        vn = jnp.transpose(v, (1, 0, 2)).astype(BF16)
        oT, lse = _sm_fwd_t(qn, kn, jnp.transpose(vn, (0, 2, 1)), cmul, causal, name + "_fwd")
        return jnp.transpose(oT, (2, 0, 1)), (qn, kn, vn, oT, lse)

    def run_bwd(saved, dout):
        qn, kn, vn, oT, lse = saved
        doT = jnp.transpose(dout, (1, 2, 0))
        do = jnp.transpose(dout, (1, 0, 2)).astype(BF16)
        dq, dk, dv = _sm_bwd_t(qn, kn, vn, oT, lse, doT, do, cmul, gscale, causal, name + "_bwd")
        dq = jnp.transpose(dq[:, :, :d], (1, 0, 2))
        if pre:
            dq = dq * scale
        return dq, jnp.transpose(dk[:, :, :d], (1, 0, 2)), jnp.transpose(dv, (1, 0, 2))

    @jax.custom_vjp
    def attn(q, k, v):
        return run_fwd(q, k, v)[0]

    attn.defvjp(run_fwd, run_bwd)
    return attn


def _rope(x, positions):
    half = x.shape[-1] // 2
    inv_freq = ROPE_THETA ** (-jnp.arange(half, dtype=F32) / half)
    ang = positions.astype(F32)[:, None] * inv_freq[None, :]
    ang = ang.reshape((ang.shape[0],) + (1,) * (x.ndim - 2) + (half,))
    cos, sin = jnp.cos(ang), jnp.sin(ang)
    x1, x2 = x[..., :half], x[..., half:]
    return jnp.concatenate([x1 * cos - x2 * sin, x1 * sin + x2 * cos], axis=-1)


def _permute_cols(w):
    parts = [w[..., _ORIG_OFF[idx]:_ORIG_OFF[idx] + SPLIT_SIZES[idx]] for _, idx in _PERM]
    pad = jnp.zeros(w.shape[:-1] + (PROJ_COLS - IN_COLS,), w.dtype)
    return jnp.concatenate(parts + [pad], axis=-1)


def _unpermute_cols(w):
    start, parts = 0, [None] * len(SPLIT_SIZES)
    for _, idx in _PERM:
        parts[idx] = w[..., start:start + SPLIT_SIZES[idx]]
        start += SPLIT_SIZES[idx]
    return jnp.concatenate(parts, axis=-1)


_BF16_PIECES = ("fq", "fk", "fv", "sq", "sk", "sv", "mq")


def _make_ln_proj(name, has_res):
    def split(proj32, proj16):
        out, off = [], 0
        for n, idx in _PERM:
            src = proj16 if n in _BF16_PIECES else proj32
            out.append(src[:, off:off + SPLIT_SIZES[idx]])
            off += SPLIT_SIZES[idx]
        return tuple(out)

    def run_fwd(x, res, g, b, w):
        h, h16, hT16 = _ln_fwd_call(x, res, g, b, name + "_ln_fwd", also16=True)
        w16 = w.astype(BF16)
        proj32, proj16 = _matmul(h16, w16, "nn", name + "_fwd", also16=True)
        return (h, split(proj32, proj16)), (x, res, g, hT16, w16)

    def run_bwd(saved, cts):
        x, res, g, hT16, w16 = saved
        dh, dpieces = cts
        wide = [c for c in dpieces if c.shape[1] % LANE == 0]
        narrow = [c.astype(BF16) for c in dpieces if c.shape[1] % LANE]
        pad = jnp.zeros((x.shape[0], PROJ_COLS - IN_COLS), BF16)
        da, dy16 = _concat_matmul_nt(wide + [jnp.concatenate(narrow + [pad], axis=1)], w16, name + "_dx")
        dw = _matmul(hT16, dy16, "nn", name + "_dw")
        outs = _ln_bwd_call(dh, x, res, g, name + "_ln_bwd", dy2=da)
        if has_res:
            dx, dr, dg, db = outs
            return dx, dr, dg.reshape(-1), db.reshape(-1), dw
        dx, dg, db = outs
        return dx, dg.reshape(-1), db.reshape(-1), dw

    if has_res:
        @jax.custom_vjp
        def op(x, res, g, b, w):
            return run_fwd(x, res, g, b, w)[0]

        op.defvjp(run_fwd, run_bwd)
    else:
        @jax.custom_vjp
        def op(x, g, b, w):
            return run_fwd(x, None, g, b, w)[0]

        op.defvjp(lambda x, g, b, w: run_fwd(x, None, g, b, w), run_bwd)

    def call(*args):
        h, pieces = op(*args)
        return h, {n: part for (n, _), part in zip(_PERM, pieces)}

    return call


def _trunk_loss(wts, x2d, mem2d, target2d):
    s = x2d.shape[0]
    positions = jnp.arange(s)
    head_scale = HEAD_DIM ** -0.5
    mla_scale = (MLA_NOPE + MLA_ROPE) ** -0.5

    mem_n = _make_ln("ln_mem", False)(mem2d, wts["mem_ln_g"], wts["mem_ln_b"])
    h, y = None, x2d
    for l in range(DEPTH):
        tag = f"l{l}_"
        w_p = wts["w_in"][l]
        if l == 0:
            h, p = _make_ln_proj(tag + "proj", False)(y, wts["ln_in_g"], wts["ln_in_b"], w_p)
        else:
            h, p = _make_ln_proj(tag + "proj", True)(y, h, wts["ln_g"][l - 1], wts["ln_b"][l - 1], w_p)

        log_f = jax.nn.log_sigmoid(p["f_logit"] + wts["b_forget"][l])
        f_cum = jnp.cumsum(log_f, axis=0).T
        out_fox = _make_packed_softmax(tag + "fox", head_scale, True, True)(p["fq"], p["fk"], p["fv"], f_cum)

        out_sb = _make_packed_sb(tag + "sb", head_scale)(p["sq"], p["sk"], p["sv"])

        cqn = _make_rms(tag + "rms_q")(p["c_q"], wts["mla_q_norm_g"][l])
        q_mla = _make_mm(tag + "q_up")(cqn, wts["w_mla_q_up"][l]).reshape(s, N_HEADS, MLA_NOPE + MLA_ROPE)
        ckvn = _make_rms(tag + "rms_kv")(p["c_kv"], wts["mla_kv_norm_g"][l])
        kv_mla = _make_mm(tag + "kv_up")(ckvn, wts["w_mla_kv_up"][l]).reshape(s, N_HEADS, MLA_NOPE + MLA_V)
        q_full = jnp.concatenate([q_mla[..., :MLA_NOPE], _rope(q_mla[..., MLA_NOPE:], positions)], axis=-1)
        k_rope = jnp.broadcast_to(_rope(p["k_rot"], positions)[:, None, :], (s, N_HEADS, MLA_ROPE))
        k_full = jnp.concatenate([kv_mla[..., :MLA_NOPE], k_rope], axis=-1)
        out_mla = _make_softmax_attn(tag + "mla", mla_scale, True, MLA_NOPE + MLA_ROPE)(
            q_full, k_full, kv_mla[..., MLA_NOPE:]).reshape(s, GROUP_W)

        mkv = _make_mm(tag + "mem_kv")(mem_n, wts["w_mem_kv"][l])
        out_mem = _make_packed_softmax(tag + "mem", head_scale, False, False)(
            p["mq"], mkv[:, :GROUP_W].astype(BF16), mkv[:, GROUP_W:].astype(BF16))

        y = _make_gate_out(tag + "out")((out_fox, out_sb, out_mla, out_mem), p["gate"], wts["w_out"][l])

    h = _make_ln(f"l{DEPTH - 1}_ln", True)(y, h, wts["ln_g"][DEPTH - 1], wts["ln_b"][DEPTH - 1])
    return _loss_op(h, target2d)


def _mesh_pos():
    x, y, c = (lax.axis_index(a) for a in MESH_AXES)
    return x, y, c, 4 * x + 2 * y + c


def _peer(x, y, c, mask):
    return (x ^ ((mask >> 2) & 1), y ^ ((mask >> 1) & 1), c ^ (mask & 1))


_ANY = pl.BlockSpec(memory_space=pl.ANY)


def _all_gather(row_shards, stack_shards):
    n_row, n_all = len(row_shards), len(row_shards) + len(stack_shards)
    shards = list(row_shards) + list(stack_shards)
    chip_masks = (4, 2, 6)
    tensors = range(n_all)

    def body(*refs):
        ins, outs = refs[:n_all], refs[n_all:2 * n_all]
        send_sems, recv_sems, local_sems = refs[2 * n_all:]
        x, y, c, me = _mesh_pos()
        sibling = _peer(x, y, c, 1)

        def window(t, slot):
            if t < n_row:
                rows = shards[t].shape[1]
                return outs[t].at[:, pl.ds(slot * rows, rows), :]
            return outs[t].at[slot]

        def copy(t, k, slot, to, src=None):
            return pltpu.make_async_remote_copy(
                src_ref=window(t, slot) if src is None else src, dst_ref=window(t, slot),
                send_sem=send_sems.at[t, k], recv_sem=recv_sems.at[t, k], device_id=to,
                device_id_type=pl.DeviceIdType.MESH)

        local = [pltpu.make_async_copy(ins[t], window(t, me), local_sems.at[t]) for t in tensors]
        for cp in local:
            cp.start()
        first = [copy(t, 0, me, sibling, src=ins[t]) for t in tensors]
        first += [copy(t, 1 + j, me, _peer(x, y, c, m), src=ins[t]) for j, m in enumerate(chip_masks) for t in tensors]
        for cp in first:
            cp.start()
        passed = []
        for j, m in enumerate(chip_masks):
            for t in tensors:
                copy(t, 1 + j, me ^ m, sibling).wait_recv()
            for t in tensors:
                cp = copy(t, 4 + j, me ^ m, sibling)
                cp.start()
                passed.append(cp)
        for t in tensors:
            copy(t, 0, me ^ 1, sibling).wait_recv()
        for j, m in enumerate(chip_masks):
            for t in tensors:
                copy(t, 4 + j, me ^ m ^ 1, sibling).wait_recv()
        for cp in first + passed:
            cp.wait_send()
        for cp in local:
            cp.wait()

    out_shape = [jax.ShapeDtypeStruct((a.shape[0], N_DEV * a.shape[1], a.shape[2]), a.dtype) for a in row_shards]
    out_shape += [jax.ShapeDtypeStruct((N_DEV,) + a.shape, a.dtype) for a in stack_shards]
    return pl.pallas_call(
        body, name="all_gather_weights", in_specs=[_ANY] * n_all, out_specs=[_ANY] * n_all, out_shape=out_shape,
        scratch_shapes=[pltpu.SemaphoreType.DMA((n_all, N_DEV - 1)), pltpu.SemaphoreType.DMA((n_all, N_DEV - 1)),
                        pltpu.SemaphoreType.DMA((n_all,))],
    )(*shards)


def _reduce_scatter(row_full, stack_full, bcast):
    n_row, n_stack = len(row_full), len(stack_full)
    n_all = n_row + n_stack + len(bcast)
    fulls = list(row_full) + list(stack_full) + list(bcast)

    def body(*refs):
        ins, outs = refs[:n_all], refs[n_all:2 * n_all]
        send_sems, recv_sems, local_sems = refs[2 * n_all:]
        x, y, c, me = _mesh_pos()

        def part(t, slot):
            if t < n_row:
                rows = fulls[t].shape[1] // N_DEV
                return ins[t].at[:, pl.ds(slot * rows, rows), :]
            if t < n_row + n_stack:
                return ins[t].at[slot]
            return ins[t]

        local = [pltpu.make_async_copy(part(t, me), outs[t].at[me], local_sems.at[t]) for t in range(n_all)]
        for cp in local:
            cp.start()
        sends = []
        for mask in range(1, N_DEV):
            for t in range(n_all):
                cp = pltpu.make_async_remote_copy(
                    src_ref=part(t, me ^ mask), dst_ref=outs[t].at[me], send_sem=send_sems.at[t, mask - 1],
                    recv_sem=recv_sems.at[t, mask - 1], device_id=_peer(x, y, c, mask),
                    device_id_type=pl.DeviceIdType.MESH)
                cp.start()
                sends.append(cp)
        for mask in range(1, N_DEV):
            for t in range(n_all):
                pltpu.make_async_remote_copy(
                    src_ref=part(t, me), dst_ref=outs[t].at[me ^ mask], send_sem=send_sems.at[t, mask - 1],
                    recv_sem=recv_sems.at[t, mask - 1], device_id=_peer(x, y, c, mask),
                    device_id_type=pl.DeviceIdType.MESH).wait_recv()
        for cp in sends:
            cp.wait_send()
        for cp in local:
            cp.wait()

    out_shape = [jax.ShapeDtypeStruct((N_DEV, a.shape[0], a.shape[1] // N_DEV, a.shape[2]), a.dtype) for a in row_full]
    out_shape += [jax.ShapeDtypeStruct(a.shape, a.dtype) for a in stack_full]
    out_shape += [jax.ShapeDtypeStruct((N_DEV,) + a.shape, a.dtype) for a in bcast]
    return pl.pallas_call(
        body, name="reduce_scatter_grads", in_specs=[_ANY] * n_all, out_specs=[_ANY] * n_all, out_shape=out_shape,
        scratch_shapes=[pltpu.SemaphoreType.DMA((n_all, N_DEV - 1)), pltpu.SemaphoreType.DMA((n_all, N_DEV - 1)),
                        pltpu.SemaphoreType.DMA((n_all,))],
    )(*fulls)


def _adamw(slots, w, m, v, name):
    shape = w.shape
    cols = shape[-1]
    rows = math.prod(shape[:-1])
    tr = _pick(rows, (64, 32, 16, 8))
    c1 = 1.0 - ADAM_B1 ** ADAM_STEP
    c2 = 1.0 - ADAM_B2 ** ADAM_STEP

    def body(s_ref, w_ref, m_ref, v_ref, g_ref, d_ref, nm_ref, nv_ref):
        g = s_ref[0].astype(F32)
        for k in range(1, N_DEV):
            g = g + s_ref[k].astype(F32)
        nm = ADAM_B1 * m_ref[...] + (1.0 - ADAM_B1) * g
        nv = ADAM_B2 * v_ref[...] + (1.0 - ADAM_B2) * (g * g)
        g_ref[...] = g
        nm_ref[...] = nm
        nv_ref[...] = nv
        d_ref[...] = -ADAM_LR * ((nm / c1) / (jnp.sqrt(nv / c2) + ADAM_EPS) + ADAM_WD * w_ref[...])

    row = pl.BlockSpec((tr, cols), lambda i: (i, 0))
    out = jax.ShapeDtypeStruct((rows, cols), F32)
    outs = pl.pallas_call(
        body, name=name, grid=(rows // tr,),
        in_specs=[pl.BlockSpec((N_DEV, tr, cols), lambda i: (0, i, 0)), row, row, row],
        out_specs=[row] * 4, out_shape=[out] * 4, compiler_params=_cp(("parallel",)),
    )(slots.reshape(N_DEV, rows, cols), w.reshape(rows, cols), m.reshape(rows, cols), v.reshape(rows, cols))
    return [o.reshape(shape) for o in outs]


_SMALL = ("ln_in_g", "ln_in_b", "mem_ln_g", "mem_ln_b", "b_forget", "mla_q_norm_g", "mla_kv_norm_g", "ln_g", "ln_b")
_ORDER = ("ln_in_g", "ln_in_b", "mem_ln_g", "mem_ln_b", "w_in", "b_forget", "mla_q_norm_g", "w_mla_q_up",
          "mla_kv_norm_g", "w_mla_kv_up", "w_mem_kv", "w_out", "ln_g", "ln_b")


def _pack_small(d):
    flat = jnp.concatenate([d[n].reshape(-1) for n in _SMALL])
    n = flat.shape[0]
    padded = ((n + 8 * LANE - 1) // (8 * LANE)) * (8 * LANE)
    return jnp.pad(flat, (0, padded - n)).reshape(-1, LANE)


def _unpack_small(packed, like):
    flat, out, off = packed.reshape(-1), {}, 0
    for n in _SMALL:
        size = math.prod(like[n].shape)
        out[n] = flat[off:off + size].reshape(like[n].shape)
        off += size
    return out


def _unstack_cols(g):
    n, l, r, c = g.shape
    return g.transpose(1, 2, 0, 3).reshape(l, r, n * c)


def _stack_cols(g):
    l, r, nc = g.shape
    return g.reshape(l, r, N_DEV, nc // N_DEV).transpose(2, 0, 1, 3)


def kernel(x, mem, ln_in_g, ln_in_b, mem_ln_g, mem_ln_b, w_in, b_forget, mla_q_norm_g, w_mla_q_up, mla_kv_norm_g, w_mla_kv_up, w_mem_kv, w_out, ln_g, ln_b, loss_target, m_ln_in_g, m_ln_in_b, m_mem_ln_g, m_mem_ln_b, m_w_in, m_b_forget, m_mla_q_norm_g, m_w_mla_q_up, m_mla_kv_norm_g, m_w_mla_kv_up, m_w_mem_kv, m_w_out, m_ln_g, m_ln_b, v_ln_in_g, v_ln_in_b, v_mem_ln_g, v_mem_ln_b, v_w_in, v_b_forget, v_mla_q_norm_g, v_w_mla_q_up, v_mla_kv_norm_g, v_w_mla_kv_up, v_w_mem_kv, v_w_out, v_ln_g, v_ln_b):
    w_shard = dict(ln_in_g=ln_in_g, ln_in_b=ln_in_b, mem_ln_g=mem_ln_g, mem_ln_b=mem_ln_b, w_in=w_in,
                   b_forget=b_forget, mla_q_norm_g=mla_q_norm_g, w_mla_q_up=w_mla_q_up,
                   mla_kv_norm_g=mla_kv_norm_g, w_mla_kv_up=w_mla_kv_up, w_mem_kv=w_mem_kv, w_out=w_out,
                   ln_g=ln_g, ln_b=ln_b)
    m_shard = dict(ln_in_g=m_ln_in_g, ln_in_b=m_ln_in_b, mem_ln_g=m_mem_ln_g, mem_ln_b=m_mem_ln_b, w_in=m_w_in,
                   b_forget=m_b_forget, mla_q_norm_g=m_mla_q_norm_g, w_mla_q_up=m_w_mla_q_up,
                   mla_kv_norm_g=m_mla_kv_norm_g, w_mla_kv_up=m_w_mla_kv_up, w_mem_kv=m_w_mem_kv, w_out=m_w_out,
                   ln_g=m_ln_g, ln_b=m_ln_b)
    v_shard = dict(ln_in_g=v_ln_in_g, ln_in_b=v_ln_in_b, mem_ln_g=v_mem_ln_g, mem_ln_b=v_mem_ln_b, w_in=v_w_in,
                   b_forget=v_b_forget, mla_q_norm_g=v_mla_q_norm_g, w_mla_q_up=v_w_mla_q_up,
                   mla_kv_norm_g=v_mla_kv_norm_g, w_mla_kv_up=v_w_mla_kv_up, w_mem_kv=v_w_mem_kv, w_out=v_w_out,
                   ln_g=v_ln_g, ln_b=v_ln_b)

    to16 = lambda ws: [a.astype(BF16) for a in ws]
    gathered = _all_gather(to16([_permute_cols(w_in), w_mem_kv, w_out]), to16([w_mla_q_up, w_mla_kv_up]))
    g_in, g_mem, g_out, g_qup, g_kvup = [a.astype(F32) for a in gathered]
    full = dict(w_shard)
    full.update(w_in=g_in, w_mem_kv=g_mem, w_out=g_out, w_mla_q_up=_unstack_cols(g_qup),
                w_mla_kv_up=_unstack_cols(g_kvup))

    loss_local, (grad_w, grad_x) = jax.value_and_grad(_trunk_loss, argnums=(0, 1))(
        full, x[0], mem[0], loss_target[0])

    s_in, s_mem, s_out, s_qup, s_kvup, s_small = _reduce_scatter(
        to16([grad_w["w_in"], grad_w["w_mem_kv"], grad_w["w_out"]]),
        to16([_stack_cols(grad_w["w_mla_q_up"]), _stack_cols(grad_w["w_mla_kv_up"])]),
        [_pack_small(grad_w)])

    res = {}
    for name, slots in (("w_mem_kv", s_mem), ("w_out", s_out), ("w_mla_q_up", s_qup), ("w_mla_kv_up", s_kvup)):
        res[name] = _adamw(slots, w_shard[name], m_shard[name], v_shard[name], "adamw_" + name)
    res["w_in"] = [_unpermute_cols(a) for a in _adamw(
        s_in, _permute_cols(w_in), _permute_cols(m_w_in), _permute_cols(v_w_in), "adamw_w_in")]
    small = _adamw(s_small, _pack_small(w_shard), _pack_small(m_shard), _pack_small(v_shard), "adamw_small")
    small = [_unpack_small(a, w_shard) for a in small]
    for name in _SMALL:
        res[name] = [a[name] for a in small]

    loss = lax.psum(loss_local, MESH_AXES)
    outs = [loss, grad_x[None]]
    for k in range(4):
        outs += [res[name][k] for name in _ORDER]
    return tuple(outs)
```

```python
import functools
import math

import jax
import jax.numpy as jnp
from jax import lax
from jax.experimental import pallas as pl
from jax.experimental.pallas import tpu as pltpu

F32 = jnp.float32
BF16 = jnp.bfloat16

D_MODEL = 1024
DEPTH = 2
GROUP_W = 256
N_HEADS = 4
HEAD_DIM = 64
MLA_Q_RANK = 256
MLA_KV_RANK = 128
MLA_NOPE = 64
MLA_ROPE = 32
MLA_V = 64
ROPE_THETA = 10000.0
LN_EPS = 1e-5
RMS_EPS = 1e-6
DEEPNORM_ALPHA = (2 * DEPTH) ** 0.25
SPLIT_SIZES = (256, 256, 256, 4, 256, 256, 256, 256, 128, 32, 256, 1024)
IN_COLS = sum(SPLIT_SIZES)
_ORIG_OFF = [sum(SPLIT_SIZES[:i]) for i in range(len(SPLIT_SIZES))]
_PERM = (("fq", 0), ("fk", 1), ("fv", 2), ("sq", 4), ("sk", 5), ("sv", 6), ("c_q", 7), ("c_kv", 8),
         ("mq", 10), ("gate", 11), ("k_rot", 9), ("f_logit", 3))
LANE = 128
PROJ_COLS = ((IN_COLS + LANE - 1) // LANE) * LANE

ADAM_LR = 0.001
ADAM_B1 = 0.9
ADAM_B2 = 0.999
ADAM_EPS = 1e-08
ADAM_WD = 0.01
ADAM_STEP = 10

N_DEV = 8
MESH_AXES = ("x", "y", "c")
VMEM_LIMIT = 48 * 1024 * 1024
ATTN_VMEM_LIMIT = 56 * 1024 * 1024
ATTN_BQ = 512
ATTN_BK = 512
CUMSUM_CHUNK = 256
NEG_BIG = -1e30
LOG2E = math.log2(math.e)
MM_TM, MM_TN, MM_TK, MM_TK_NT = 1024, 1664, 1024, 3328
CONCAT_MM_TM = 512

_NT = (((1,), (1,)), ((), ()))
_NN = (((1,), (0,)), ((), ()))


def _cp(sem, vmem=VMEM_LIMIT):
    return pltpu.CompilerParams(dimension_semantics=sem, vmem_limit_bytes=vmem)


def _dot(a, b, dims=_NN):
    return lax.dot_general(a, b, dims, preferred_element_type=F32)


def _pick(n, cands):
    for c in cands:
        if c <= n and n % c == 0:
            return c
    return n


def _tile(n, cap):
    if n <= cap:
        return n
    best = None
    for d in range(LANE, cap + 1, LANE):
        if n % d == 0:
            best = d
    assert best is not None, (n, cap)
    return best


def _matmul(a, b, mode, name):
    if mode == "nn":
        (M, K), (K2, N) = a.shape, b.shape
    else:
        (M, K), (N, K2) = a.shape, b.shape
    assert K == K2 and a.dtype == BF16 and b.dtype == BF16, (a.shape, b.shape, mode)
    tm, tn = _tile(M, MM_TM), _tile(N, MM_TN)
    tk = _tile(K, MM_TK if mode == "nn" else MM_TK_NT)
    nk = K // tk
    dims = _NN if mode == "nn" else _NT

    def body(a_ref, b_ref, o_ref, acc_ref):
        part = _dot(a_ref[...], b_ref[...], dims)
        if nk == 1:
            o_ref[...] = part
        else:
            k = pl.program_id(2)

            @pl.when(k == 0)
            def _():
                acc_ref[...] = part

            @pl.when(k > 0)
            def _():
                acc_ref[...] += part

            @pl.when(k == nk - 1)
            def _():
                o_ref[...] = acc_ref[...]

    a_spec = pl.BlockSpec((tm, tk), lambda j, i, k: (i, k))
    if mode == "nn":
        b_spec = pl.BlockSpec((tk, tn), lambda j, i, k: (k, j))
    else:
        b_spec = pl.BlockSpec((tn, tk), lambda j, i, k: (j, k))
    acc_shape = (tm, tn) if nk > 1 else (8, LANE)
    o_spec = pl.BlockSpec((tm, tn), lambda j, i, k: (i, j))
    return pl.pallas_call(
        body, name=name, grid=(N // tn, M // tm, nk),
        in_specs=[a_spec, b_spec], out_specs=o_spec, out_shape=jax.ShapeDtypeStruct((M, N), F32),
        scratch_shapes=[pltpu.VMEM(acc_shape, F32)],
        compiler_params=_cp(("parallel", "parallel", "arbitrary")),
    )(a, b)


def _concat_matmul_nt(pieces, b, name):
    M, (N, K) = pieces[0].shape[0], b.shape
    widths = [p.shape[1] for p in pieces]
    assert sum(widths) == K and all(w % LANE == 0 for w in widths) and b.dtype == BF16, (widths, b.shape)
    tm = _tile(M, CONCAT_MM_TM)
    n = len(pieces)

    def body(*refs):
        b_ref, o_ref, a_ref = refs[n:]
        a_ref[...] = jnp.concatenate([r[...].astype(BF16) for r in refs[:n]], axis=1)
        o_ref[...] = _dot(a_ref[...], b_ref[...], _NT)

    rows = lambda w: pl.BlockSpec((tm, w), lambda i: (i, 0))
    return pl.pallas_call(
        body, name=name, grid=(M // tm,),
        in_specs=[rows(w) for w in widths] + [pl.BlockSpec((N, K), lambda i: (0, 0))],
        out_specs=[rows(N), rows(K)],
        out_shape=[jax.ShapeDtypeStruct((M, N), F32), jax.ShapeDtypeStruct((M, K), BF16)],
        compiler_params=_cp(("parallel",)),
    )(*pieces, b)


def _matmul_split(a, b, widths, dtypes, name):
    (M, K), (K2, N) = a.shape, b.shape
    assert K == K2 and sum(widths) == N and all(w % LANE == 0 for w in widths), (a.shape, b.shape, widths)
    assert a.dtype == BF16 and b.dtype == BF16
    tm = _tile(M, CONCAT_MM_TM)
    offs = [sum(widths[:r]) for r in range(len(widths))]

    def body(a_ref, b_ref, *o_refs):
        av = a_ref[...]
        for o_ref, off, w in zip(o_refs, offs, widths):
            o_ref[...] = _dot(av, b_ref[:, off:off + w]).astype(o_ref.dtype)

    rows = lambda w: pl.BlockSpec((tm, w), lambda i: (i, 0))
    return pl.pallas_call(
        body, name=name, grid=(M // tm,),
        in_specs=[rows(K), pl.BlockSpec((K, N), lambda i: (0, 0))],
        out_specs=[rows(w) for w in widths],
        out_shape=[jax.ShapeDtypeStruct((M, w), d) for w, d in zip(widths, dtypes)],
        compiler_params=_cp(("parallel",)),
    )(a, b)


def _make_mm(name):
    @jax.custom_vjp
    def mm(a, w):
        return _matmul(a.astype(BF16), w.astype(BF16), "nn", name + "_fwd")

    def fwd(a, w):
        a16, w16 = a.astype(BF16), w.astype(BF16)
        return _matmul(a16, w16, "nn", name + "_fwd"), (a16, w16)

    def bwd(res, dy):
        a16, w16 = res
        dy16 = dy.astype(BF16)
        da = _matmul(dy16, w16, "nt", name + "_dx")
        dw = _matmul(a16.T, dy16, "nn", name + "_dw")
        return da, dw

    mm.defvjp(fwd, bwd)
    return mm


def _row_tile(rows):
    return _pick(rows, (512, 256, 128, 64, 32, 16, 8))


def _ln_stats(u):
    mu = jnp.mean(u, axis=-1, keepdims=True)
    d = u - mu
    var = jnp.mean(d * d, axis=-1, keepdims=True)
    return d, lax.rsqrt(var + LN_EPS)


def _ln_fwd_call(x, res, g, b, name, also16=False):
    rows, dm = x.shape
    tr = _row_tile(rows)
    has_res = res is not None
    n_in = 2 if has_res else 1

    def body(*refs):
        if has_res:
            u = DEEPNORM_ALPHA * refs[1][...] + refs[0][...]
        else:
            u = refs[0][...]
        g_ref, b_ref = refs[n_in], refs[n_in + 1]
        d, rstd = _ln_stats(u)
        y = d * rstd * g_ref[...] + b_ref[...]
        refs[n_in + 2][...] = y
        if also16:
            y16 = y.astype(BF16)
            refs[n_in + 3][...] = y16
            refs[n_in + 4][...] = y16.T

    row = pl.BlockSpec((tr, dm), lambda i: (i, 0))
    vec = pl.BlockSpec((1, dm), lambda i: (0, 0))
    args = (x, res) if has_res else (x,)
    out_specs, out_shape = [row], [jax.ShapeDtypeStruct((rows, dm), F32)]
    if also16:
        out_specs += [row, pl.BlockSpec((dm, tr), lambda i: (0, i))]
        out_shape += [jax.ShapeDtypeStruct((rows, dm), BF16), jax.ShapeDtypeStruct((dm, rows), BF16)]
    outs = pl.pallas_call(
        body, name=name, grid=(rows // tr,),
        in_specs=[row] * n_in + [vec, vec], out_specs=out_specs, out_shape=out_shape,
        compiler_params=_cp(("parallel",)),
    )(*args, g.reshape(1, dm), b.reshape(1, dm))
    return outs if also16 else outs[0]


def _ln_bwd_call(dy, x, res, g, name, dy2=None):
    rows, dm = x.shape
    tr = _row_tile(rows)
    has_res = res is not None
    two = dy2 is not None

    def body(*refs):
        dy_ref, refs = refs[0], refs[1:]
        if two:
            dy2_ref, refs = refs[0], refs[1:]
        if has_res:
            x_ref, r_ref, g_ref, dx_ref, dr_ref, dg_ref, db_ref = refs
            u = DEEPNORM_ALPHA * r_ref[...] + x_ref[...]
        else:
            x_ref, g_ref, dx_ref, dg_ref, db_ref = refs
            u = x_ref[...]
        i = pl.program_id(0)
        d, rstd = _ln_stats(u)
        xhat = d * rstd
        dyv = dy_ref[...] + dy2_ref[...] if two else dy_ref[...]
        dxh = dyv * g_ref[...]
        m1 = jnp.mean(dxh, axis=-1, keepdims=True)
        m2 = jnp.mean(dxh * xhat, axis=-1, keepdims=True)
        du = rstd * (dxh - m1 - xhat * m2)
        dx_ref[...] = du
        if has_res:
            dr_ref[...] = DEEPNORM_ALPHA * du
        pg = jnp.sum(dyv * xhat, axis=0, keepdims=True)
        pb = jnp.sum(dyv, axis=0, keepdims=True)

        @pl.when(i == 0)
        def _():
            dg_ref[...] = pg
            db_ref[...] = pb

        @pl.when(i > 0)
        def _():
            dg_ref[...] += pg
            db_ref[...] += pb

    row = pl.BlockSpec((tr, dm), lambda i: (i, 0))
    vec = pl.BlockSpec((1, dm), lambda i: (0, 0))
    big = jax.ShapeDtypeStruct((rows, dm), F32)
    small = jax.ShapeDtypeStruct((1, dm), F32)
    args = ((dy, dy2) if two else (dy,)) + ((x, res) if has_res else (x,))
    n_big = 2 if has_res else 1
    outs = pl.pallas_call(
        body, name=name, grid=(rows // tr,),
        in_specs=[row] * len(args) + [vec],
        out_specs=[row] * n_big + [vec, vec],
        out_shape=[big] * n_big + [small, small],
        compiler_params=_cp(("arbitrary",)),
    )(*args, g.reshape(1, dm))
    return outs


def _make_ln(name, has_res):
    if has_res:
        @jax.custom_vjp
        def ln(x, res, g, b):
            return _ln_fwd_call(x, res, g, b, name + "_fwd")

        def fwd(x, res, g, b):
            return ln(x, res, g, b), (x, res, g)

        def bwd(saved, dy):
            x, res, g = saved
            dx, dr, dg, db = _ln_bwd_call(dy, x, res, g, name + "_bwd")
            return dx, dr, dg.reshape(-1), db.reshape(-1)
    else:
        @jax.custom_vjp
        def ln(x, g, b):
            return _ln_fwd_call(x, None, g, b, name + "_fwd")

        def fwd(x, g, b):
            return ln(x, g, b), (x, g)

        def bwd(saved, dy):
            x, g = saved
            dx, dg, db = _ln_bwd_call(dy, x, None, g, name + "_bwd")
            return dx, dg.reshape(-1), db.reshape(-1)

    ln.defvjp(fwd, bwd)
    return ln


def _rms_fwd_call(x, g, name):
    rows, dm = x.shape
    tr = _row_tile(rows)

    def body(x_ref, g_ref, o_ref):
        xv = x_ref[...]
        rstd = lax.rsqrt(jnp.mean(xv * xv, axis=-1, keepdims=True) + RMS_EPS)
        o_ref[...] = xv * rstd * g_ref[...]

    row = pl.BlockSpec((tr, dm), lambda i: (i, 0))
    vec = pl.BlockSpec((1, dm), lambda i: (0, 0))
    return pl.pallas_call(
        body, name=name, grid=(rows // tr,), in_specs=[row, vec], out_specs=row,
        out_shape=jax.ShapeDtypeStruct((rows, dm), F32), compiler_params=_cp(("parallel",)),
    )(x, g.reshape(1, dm))


def _rms_bwd_call(dy, x, g, name):
    rows, dm = x.shape
    tr = _row_tile(rows)

    def body(dy_ref, x_ref, g_ref, dx_ref, dg_ref):
        i = pl.program_id(0)
        xv = x_ref[...]
        dyv = dy_ref[...]
        rstd = lax.rsqrt(jnp.mean(xv * xv, axis=-1, keepdims=True) + RMS_EPS)
        xhat = xv * rstd
        dxh = dyv * g_ref[...]
        m2 = jnp.mean(dxh * xhat, axis=-1, keepdims=True)
        dx_ref[...] = rstd * (dxh - xhat * m2)
        pg = jnp.sum(dyv * xhat, axis=0, keepdims=True)

        @pl.when(i == 0)
        def _():
            dg_ref[...] = pg

        @pl.when(i > 0)
        def _():
            dg_ref[...] += pg

    row = pl.BlockSpec((tr, dm), lambda i: (i, 0))
    vec = pl.BlockSpec((1, dm), lambda i: (0, 0))
    return pl.pallas_call(
        body, name=name, grid=(rows // tr,), in_specs=[row, row, vec], out_specs=[row, vec],
        out_shape=[jax.ShapeDtypeStruct((rows, dm), F32), jax.ShapeDtypeStruct((1, dm), F32)],
        compiler_params=_cp(("arbitrary",)),
    )(dy, x, g.reshape(1, dm))


def _make_rms(name):
    @jax.custom_vjp
    def rms(x, g):
        return _rms_fwd_call(x, g, name + "_fwd")

    def fwd(x, g):
        return rms(x, g), (x, g)

    def bwd(saved, dy):
        x, g = saved
        dx, dg = _rms_bwd_call(dy, x, g, name + "_bwd")
        return dx, dg.reshape(-1)

    rms.defvjp(fwd, bwd)
    return rms


def _sigmoid(x):
    return 1.0 / (1.0 + jnp.exp(-x))


def _gate_fwd_call(parts, gate, name):
    rows, dm = gate.shape
    tr = _row_tile(rows)
    n = len(parts)

    def body(*refs):
        g_ref, o_ref, oT_ref = refs[n:]
        gv = g_ref[...]
        mixed = jnp.concatenate([r[...] for r in refs[:n]], axis=1)
        y16 = (mixed * (gv * _sigmoid(gv))).astype(BF16)
        o_ref[...] = y16
        oT_ref[...] = y16.T

    row = pl.BlockSpec((tr, dm), lambda i: (i, 0))
    part_specs = [pl.BlockSpec((tr, p.shape[1]), lambda i: (i, 0)) for p in parts]
    return pl.pallas_call(
        body, name=name, grid=(rows // tr,), in_specs=part_specs + [row],
        out_specs=[row, pl.BlockSpec((dm, tr), lambda i: (0, i))],
        out_shape=[jax.ShapeDtypeStruct((rows, dm), BF16), jax.ShapeDtypeStruct((dm, rows), BF16)],
        compiler_params=_cp(("parallel",)),
    )(*parts, gate)


def _gate_bwd_call(dy, parts, gate, name):
    rows, dm = gate.shape
    tr = _row_tile(rows)
    n = len(parts)
    widths = [p.shape[1] for p in parts]

    def body(*refs):
        dy_ref, g_ref = refs[0], refs[n + 1]
        dm_refs, dg_ref = refs[n + 2:2 * n + 2], refs[2 * n + 2]
        gv = g_ref[...]
        dyv = dy_ref[...]
        sg = _sigmoid(gv)
        mixed = jnp.concatenate([r[...] for r in refs[1:n + 1]], axis=1)
        dmixed = dyv * (gv * sg)
        off = 0
        for r, w in zip(dm_refs, widths):
            r[...] = dmixed[:, off:off + w]
            off += w
        dg_ref[...] = dyv * mixed * (sg * (1.0 + gv * (1.0 - sg)))

    row = pl.BlockSpec((tr, dm), lambda i: (i, 0))
    part_specs = [pl.BlockSpec((tr, w), lambda i: (i, 0)) for w in widths]
    return pl.pallas_call(
        body, name=name, grid=(rows // tr,), in_specs=[row] + part_specs + [row], out_specs=part_specs + [row],
        out_shape=[jax.ShapeDtypeStruct((rows, w), F32) for w in widths] + [jax.ShapeDtypeStruct((rows, dm), F32)],
        compiler_params=_cp(("parallel",)),
    )(dy, *parts, gate)


def _make_gate_out(name):
    def run_fwd(parts, gate, w):
        g16, gT16 = _gate_fwd_call(parts, gate, name + "_gate_fwd")
        w16 = w.astype(BF16)
        return _matmul(g16, w16, "nn", name + "_fwd"), (parts, gate, gT16, w16)

    def run_bwd(saved, dy):
        parts, gate, gT16, w16 = saved
        dy16 = dy.astype(BF16)
        dgated = _matmul(dy16, w16, "nt", name + "_dx")
        *dparts, dgate = _gate_bwd_call(dgated, parts, gate, name + "_gate_bwd")
        return tuple(dparts), dgate, _matmul(gT16, dy16, "nn", name + "_dw")

    @jax.custom_vjp
    def gate_out(parts, gate, w):
        return run_fwd(parts, gate, w)[0]

    gate_out.defvjp(run_fwd, run_bwd)
    return gate_out


def _loss_call(y, t, name):
    rows, dm = y.shape
    tr = _row_tile(rows)

    def body(y_ref, t_ref, l_ref, d_ref):
        i = pl.program_id(0)
        e = y_ref[...] - t_ref[...]
        d_ref[...] = e * (1.0 / dm)
        part = 0.5 * jnp.sum(jnp.mean(e * e, axis=-1, keepdims=True), axis=0, keepdims=True)

        @pl.when(i == 0)
        def _():
            l_ref[...] = part

        @pl.when(i > 0)
        def _():
            l_ref[...] += part

    row = pl.BlockSpec((tr, dm), lambda i: (i, 0))
    one = pl.BlockSpec((1, 1), lambda i: (0, 0))
    return pl.pallas_call(
        body, name=name, grid=(rows // tr,), in_specs=[row, row], out_specs=[one, row],
        out_shape=[jax.ShapeDtypeStruct((1, 1), F32), jax.ShapeDtypeStruct((rows, dm), F32)],
        compiler_params=_cp(("arbitrary",)),
    )(y, t)


@jax.custom_vjp
def _loss_op(y, t):
    return _loss_call(y, t, "loss_head")[0][0, 0]


def _loss_fwd(y, t):
    l, d = _loss_call(y, t, "loss_head")
    return l[0, 0], d


def _loss_bwd(d, ct):
    return ct * d, jnp.zeros_like(d)


_loss_op.defvjp(_loss_fwd, _loss_bwd)


def _attn_blocks(S, Sk, cap=None):
    bq, bk = min(cap or ATTN_BQ, S), min(cap or ATTN_BK, Sk)
    assert S % bq == 0 and Sk % bk == 0
    return bq, bk


def _valid_t(i, j, bq, bk, strict):
    key = j * bk + lax.broadcasted_iota(jnp.int32, (bk, bq), 0)
    qry = i * bq + lax.broadcasted_iota(jnp.int32, (bk, bq), 1)
    return (key < qry) if strict else (key <= qry)


def _sm_fwd_t(qn, k, vT, cmul, causal, name):
    H, S, DK = qn.shape
    Sk, dv = k.shape[1], vT.shape[1]
    bq, bk = _attn_blocks(S, Sk)
    nq, nkb = S // bq, Sk // bk
    hb = PAIR * FWD_PAIRS if H % (PAIR * FWD_PAIRS) == 0 else 1
    heads = range(hb)
    if causal:
        assert S == Sk and bq == bk

    def body(qn_ref, k_ref, vT_ref, oT_ref, lse_ref):
        i = pl.program_id(1)
        qTs = [qn_ref[w].T for w in heads]

        def blk(j, carry, masked):
            off = pl.multiple_of(j * bk, bk)
            sT = [_dot(k_ref[w, pl.ds(off, bk), :], qTs[w]) * cmul for w in heads]
            if masked:
                valid = _valid_t(i, j, bq, bk, False)
                sT = [jnp.where(valid, s, NEG_BIG) for s in sT]
            m_new = [jnp.maximum(carry[w][0], jnp.max(sT[w], axis=0, keepdims=True)) for w in heads]
            p = [jnp.exp2(sT[w] - m_new[w]) for w in heads]
            a = [jnp.exp2(carry[w][0] - m_new[w]) for w in heads]
            l = [a[w] * carry[w][1] + jnp.sum(p[w], axis=0, keepdims=True) for w in heads]
            acc = [a[w] * carry[w][2] + _dot(vT_ref[w, :, pl.ds(off, bk)], p[w].astype(BF16)) for w in heads]
            return tuple((m_new[w], l[w], acc[w]) for w in heads)

        carry = tuple((jnp.full((1, bq), NEG_BIG, F32), jnp.zeros((1, bq), F32), jnp.zeros((dv, bq), F32))
                      for _ in heads)
        if causal:
            carry = lax.fori_loop(0, i, lambda j, c: blk(j, c, False), carry)
            carry = blk(i, carry, True)
        else:
            carry = lax.fori_loop(0, nkb, lambda j, c: blk(j, c, False), carry)
        for w in heads:
            oT_ref[w] = carry[w][2] / carry[w][1]
            lse_ref[w] = carry[w][0] + jnp.log2(carry[w][1])

    qcol = lambda d: pl.BlockSpec((hb, d, bq), lambda h, i: (h, 0, i))
    return pl.pallas_call(
        body, name=name, grid=(H // hb, nq),
        in_specs=[pl.BlockSpec((hb, bq, DK), lambda h, i: (h, i, 0)), pl.BlockSpec((hb, Sk, DK), lambda h, i: (h, 0, 0)),
                  pl.BlockSpec((hb, dv, Sk), lambda h, i: (h, 0, 0))],
        out_specs=[qcol(dv), qcol(1)],
        out_shape=[jax.ShapeDtypeStruct((H, dv, S), F32), jax.ShapeDtypeStruct((H, 1, S), F32)],
        compiler_params=_cp(("parallel", "arbitrary"), ATTN_VMEM_LIMIT),
    )(qn, k, vT)


def _sm_bwd_t(qn, k, v, oT, lse, doT, do, cmul, gscale, causal, name):
    H, S, DK = qn.shape
    Sk, dv = k.shape[1], v.shape[2]
    bq, bk = _attn_blocks(S, Sk)
    nq, nkb = S // bq, Sk // bk

    def body(qn_ref, k_ref, v_ref, oT_ref, lse_ref, doT_ref, do_ref, dq_ref, dk_ref, dv_ref):
        i = pl.program_id(1)

        @pl.when(i == 0)
        def _():
            dk_ref[...] = jnp.zeros_like(dk_ref)
            dv_ref[...] = jnp.zeros_like(dv_ref)

        qnb = qn_ref[...]
        qTb = qnb.T
        dob = do_ref[...]
        doTf = doT_ref[...]
        doTb = doTf.astype(BF16)
        delta = jnp.sum(doTf * oT_ref[...], axis=0, keepdims=True)
        lse = lse_ref[...]

        def blk(j, dq, masked):
            off = pl.multiple_of(j * bk, bk)
            kb = k_ref[pl.ds(off, bk), :]
            sT = _dot(kb, qTb) * cmul
            if masked:
                sT = jnp.where(_valid_t(i, j, bq, bk, False), sT, NEG_BIG)
            p = jnp.exp2(sT - lse)
            dp = _dot(v_ref[pl.ds(off, bk), :], doTb)
            ds = p * (dp - delta)
            dsb = (ds * gscale).astype(BF16) if gscale != 1.0 else ds.astype(BF16)
            dv_ref[pl.ds(off, bk), :] += _dot(p.astype(BF16), dob)
            dk_ref[pl.ds(off, bk), :] += _dot(dsb, qnb)
            return dq + _dot(kb.T, dsb)

        dq = jnp.zeros((DK, bq), F32)
        if causal:
            dq = lax.fori_loop(0, i, lambda j, c: blk(j, c, False), dq)
            dq = blk(i, dq, True)
        else:
            dq = lax.fori_loop(0, nkb, lambda j, c: blk(j, c, False), dq)
        dq_ref[...] = dq.T

    qcol = lambda d: pl.BlockSpec((None, d, bq), lambda h, i: (h, 0, i))
    qrow = lambda d: pl.BlockSpec((None, bq, d), lambda h, i: (h, i, 0))
    krow = lambda d: pl.BlockSpec((None, Sk, d), lambda h, i: (h, 0, 0))
    return pl.pallas_call(
        body, name=name, grid=(H, nq),
        in_specs=[qrow(DK), krow(DK), krow(dv), qcol(dv), qcol(1), qcol(dv), qrow(dv)],
        out_specs=[qrow(DK), krow(DK), krow(dv)],
        out_shape=[jax.ShapeDtypeStruct((H, S, DK), F32), jax.ShapeDtypeStruct((H, Sk, DK), F32),
                   jax.ShapeDtypeStruct((H, Sk, dv), F32)],
        compiler_params=_cp(("parallel", "arbitrary"), ATTN_VMEM_LIMIT),
    )(qn, k, v, oT, lse, doT, do)


def _tri(n, fn):
    r = lax.broadcasted_iota(jnp.int32, (n, n), 0)
    c = lax.broadcasted_iota(jnp.int32, (n, n), 1)
    return jnp.where(fn(r, c), 1.0, 0.0).astype(BF16)


def _key_cumsum(x, tri2, suffix, base):
    bk = x.shape[0]
    c = min(CUMSUM_CHUNK, bk)
    n = bk // c
    hi32 = lax.bitcast_convert_type(lax.bitcast_convert_type(x, jnp.int32) & jnp.int32(-65536), F32)
    hi = hi32.astype(BF16)
    lo = (x - hi32).astype(BF16)
    tot = [jnp.sum(x[a * c:(a + 1) * c], axis=0, keepdims=True) for a in range(n)]
    outs = []
    for a in range(n):
        row = base
        for t in (tot[a + 1:] if suffix else tot[:a]):
            row = row + t
        stacked = jnp.concatenate([hi[a * c:(a + 1) * c], lo[a * c:(a + 1) * c]], axis=0)
        outs.append(_dot(tri2, stacked) + row)
    total = tot[0]
    for t in tot[1:]:
        total = total + t
    return (outs[0] if n == 1 else jnp.concatenate(outs, axis=0)), total


def _tri2(n, fn):
    t = _tri(n, fn)
    return jnp.concatenate([t, t], axis=1)


def _sb_logs(z):
    neg_abs = lax.bitcast_convert_type(lax.bitcast_convert_type(z, jnp.int32) | jnp.int32(-2 ** 31), F32)
    ls = jnp.minimum(z, 0.0) - jnp.log(1.0 + jnp.exp(neg_abs))
    return ls, ls - z


PAIR = LANE // HEAD_DIM
FWD_PAIRS = 2
SB_DEAD = -110.0
SB_BLOCK = 256
FOX_DEAD = -160.0
FOX_BLOCK = 512


def _head_lanes(shape, w, axis):
    idx = lax.broadcasted_iota(jnp.int32, shape, axis)
    return (idx >= HEAD_DIM * w) & (idx < HEAD_DIM * (w + 1))


def _bias_rows(w, bq):
    row = lax.broadcasted_iota(jnp.int32, (LANE, bq), 0)
    return jnp.where((row >= 3 * w) & (row < 3 * w + 3), -1.0, 0.0).astype(BF16)


def _merge_pair(parts):
    return jnp.where(_head_lanes(parts[0].shape, 0, 0), parts[0], parts[1]).T


def _smp_fwd(q2, k2, v2, bias, r, causal, name, kstat=None):
    S, C = q2.shape
    Sk = k2.shape[0]
    bq, bk = _attn_blocks(S, Sk, FOX_BLOCK if bias is not None else None)
    nq, nkb, P = S // bq, Sk // bk, C // LANE
    gp = FWD_PAIRS if P % FWD_PAIRS == 0 else 1
    use_f = bias is not None
    if causal:
        assert S == Sk and bq == bk and use_f

    def body(*refs):
        if use_f:
            ks_ref, q_ref, k_ref, v_ref, b_ref, r_ref, o_ref, lse_ref, js_ref = refs
        else:
            q_ref, k_ref, v_ref, o_ref, lse_ref = refs
        i = pl.program_id(1)
        heads = range(PAIR * gp)
        lanes = [slice(LANE * (h // PAIR), LANE * (h // PAIR + 1)) for h in heads]
        qps = [q_ref[:, lanes[h]] for h in heads]
        qTs = [jnp.where(_head_lanes(qps[h].shape, h % PAIR, 1), qps[h], jnp.zeros_like(qps[h])).T for h in heads]
        if use_f:
            qf = [t.astype(F32) for t in qTs]
            qnorm = [jnp.sqrt(jnp.sum(t * t, axis=0, keepdims=True)) for t in qf]
            qTs = [jnp.concatenate([qTs[h], _bias_rows(h % PAIR, bq)], axis=0) for h in heads]

        def blk(j, carry, masked):
            off = pl.multiple_of(j * bk, bk)
            kbs = [k_ref[pl.ds(off, bk), LANE * g:LANE * (g + 1)] for g in range(gp)]
            if use_f:
                kbs = [jnp.concatenate([kbs[g], b_ref[pl.ds(off, bk), LANE * g:LANE * (g + 1)]], axis=1)
                       for g in range(gp)]
            vTbs = [v_ref[pl.ds(off, bk), LANE * g:LANE * (g + 1)].T for g in range(gp)]
            sT = [_dot(kbs[h // PAIR], qTs[h]) * LOG2E for h in heads]
            if masked:
                valid = _valid_t(i, j, bq, bk, False)
                sT = [jnp.where(valid, s, NEG_BIG) for s in sT]
            cm = [jnp.max(s, axis=0, keepdims=True) for s in sT]
            if use_f:
                cm = [cm[h] + r_ref[h] for h in heads]
            m_new = [jnp.maximum(carry[h][0], cm[h]) for h in heads]
            shift = [(m_new[h] - r_ref[h]) if use_f else m_new[h] for h in heads]
            p = [jnp.exp2(sT[h] - shift[h]) for h in heads]
            a = [jnp.exp2(carry[h][0] - m_new[h]) for h in heads]
            l = [a[h] * carry[h][1] + jnp.sum(p[h], axis=0, keepdims=True) for h in heads]
            acc = [a[h] * carry[h][2] + _dot(vTbs[h // PAIR], p[h].astype(BF16)) for h in heads]
            return tuple((m_new[h], l[h], acc[h]) for h in heads)

        def step(jj, state):
            carry, first = state
            j = i - jj
            h0 = pl.program_id(0) * (PAIR * gp)
            bound = [LOG2E * (qnorm[h] * ks_ref[(h0 + h) * nkb + j] - ks_ref[(PAIR * P + h0 + h) * nkb + j])
                     + r_ref[h] - carry[h][0] for h in heads]
            live = jnp.max(functools.reduce(jnp.maximum, bound)) >= FOX_DEAD
            carry = lax.cond(live, lambda cr: blk(j, cr, False), lambda cr: cr, carry)
            return carry, jnp.where(live, j, first)

        carry = tuple((jnp.full((1, bq), NEG_BIG, F32), jnp.zeros((1, bq), F32), jnp.zeros((LANE, bq), F32))
                      for _ in heads)
        if causal:
            carry = blk(i, carry, True)
            carry, first = lax.fori_loop(1, i + 1, step, (carry, i))
            js_ref[0] = jnp.full((1, bq), first, jnp.int32)
        else:
            carry = lax.fori_loop(0, nkb, lambda j, c: blk(j, c, False), carry)
            if use_f:
                js_ref[0] = jnp.zeros((1, bq), jnp.int32)
        for h in heads:
            lse_ref[h] = carry[h][0] + jnp.log2(carry[h][1])
        for g in range(gp):
            o_ref[:, LANE * g:LANE * (g + 1)] = _merge_pair(
                [carry[h][2] / carry[h][1] for h in range(PAIR * g, PAIR * (g + 1))])

    qblk = pl.BlockSpec((bq, LANE * gp), lambda p, i: (i, p))
    kres = pl.BlockSpec((Sk, LANE * gp), lambda p, i: (0, p))
    stat = pl.BlockSpec((PAIR * gp, 1, bq), lambda p, i: (p, 0, i))
    in_specs = [qblk, kres, kres]
    args = [q2, k2, v2]
    out_specs = [qblk, stat]
    out_shape = [jax.ShapeDtypeStruct((S, C), F32), jax.ShapeDtypeStruct((PAIR * P, 1, S), F32)]
    if use_f:
        in_specs = [pl.BlockSpec(memory_space=pltpu.SMEM)] + in_specs + [kres, stat]
        args = [kstat] + args + [bias, r]
        out_specs.append(pl.BlockSpec((1, 1, bq), lambda p, i: (p, 0, i)))
        out_shape.append(jax.ShapeDtypeStruct((P // gp, 1, S), jnp.int32))
    return pl.pallas_call(
        body, name=name, grid=(P // gp, nq), in_specs=in_specs, out_specs=out_specs, out_shape=out_shape,
        compiler_params=_cp(("parallel", "arbitrary"), ATTN_VMEM_LIMIT),
    )(*args)


def _smp_bwd(q2, k2, v2, o2, lse, do2, bias, r, scale, causal, name, first=None):
    S, C = q2.shape
    Sk = k2.shape[0]
    bq, bk = _attn_blocks(S, Sk, FOX_BLOCK if bias is not None else None)
    nq, nkb, P = S // bq, Sk // bk, C // LANE
    use_f = bias is not None

    def body(*refs):
        if use_f:
            (first_ref, q_ref, k_ref, v_ref, o_ref, lse_ref, do_ref, b_ref, r_ref,
             dq_ref, dk_ref, dv_ref, dr_ref, dkey_ref, dk_acc, dv_acc, db_ref) = refs
        else:
            q_ref, k_ref, v_ref, o_ref, lse_ref, do_ref, dq_ref, dk_ref, dv_ref, dk_acc, dv_acc = refs
        i = pl.program_id(1)

        @pl.when(i == 0)
        def _():
            dk_acc[...] = jnp.zeros_like(dk_acc)
            dv_acc[...] = jnp.zeros_like(dv_acc)
            if use_f:
                db_ref[...] = jnp.zeros_like(db_ref)

        qp = q_ref[...]
        dof = do_ref[...]
        prod = dof * o_ref[...]
        heads = range(PAIR)
        mine = [_head_lanes(qp.shape, w, 1) for w in heads]
        qz = [jnp.where(mine[w], qp, jnp.zeros_like(qp)) for w in heads]
        qTs = [qz[w].T for w in heads]
        if use_f:
            qTs = [jnp.concatenate([qTs[w], _bias_rows(w, bq)], axis=0) for w in heads]
        doz = [jnp.where(mine[w], dof, 0.0).astype(BF16) for w in heads]
        doT = [doz[w].T for w in heads]
        delta = [jnp.sum(jnp.where(mine[w], prod, 0.0).T, axis=0, keepdims=True) for w in heads]
        shift = [(lse_ref[w] - r_ref[w]) if use_f else lse_ref[w] for w in heads]

        def blk(j, carry, masked):
            off = pl.multiple_of(j * bk, bk)
            kb = k_ref[pl.ds(off, bk), :]
            kTb = kb.T
            if use_f:
                kb = jnp.concatenate([kb, b_ref[pl.ds(off, bk), :]], axis=1)
            vb = v_ref[pl.ds(off, bk), :]
            sT = [_dot(kb, qTs[w]) * LOG2E for w in heads]
            if masked:
                valid = _valid_t(i, j, bq, bk, False)
                sT = [jnp.where(valid, s, NEG_BIG) for s in sT]
            p = [jnp.exp2(sT[w] - shift[w]) for w in heads]
            dp = [_dot(vb, doT[w]) for w in heads]
            ds = [p[w] * (dp[w] - delta[w]) for w in heads]
            dsb = [d.astype(BF16) for d in ds]
            dvs = [_dot(p[w].astype(BF16), doz[w]) for w in heads]
            dks = [_dot(dsb[w], qz[w]) for w in heads]
            dv_acc[pl.ds(off, bk), :] += dvs[0] + dvs[1]
            dk_acc[pl.ds(off, bk), :] += dks[0] + dks[1]
            dr = [carry[w][1] for w in heads]
            if use_f:
                dr = [dr[w] + jnp.sum(ds[w], axis=0, keepdims=True) for w in heads]
                lane = lax.broadcasted_iota(jnp.int32, (bk, LANE), 1)
                cols = [jnp.where(lane == w, jnp.sum(ds[w], axis=1, keepdims=True), 0.0) for w in heads]
                db_ref[pl.ds(off, bk), :] += cols[0] + cols[1]
            dq = [carry[w][0] + _dot(kTb, dsb[w]) for w in heads]
            return tuple((dq[w], dr[w]) for w in heads)

        carry = tuple((jnp.zeros((LANE, bq), F32), jnp.zeros((1, bq), F32)) for _ in heads)
        if causal:
            start = first_ref[pl.program_id(0) // (P // first.shape[0]), i]
            carry = lax.fori_loop(start, i, lambda j, c: blk(j, c, False), carry)
            carry = blk(i, carry, True)
        else:
            carry = lax.fori_loop(0, nkb, lambda j, c: blk(j, c, False), carry)
        if use_f:
            for w in heads:
                dr_ref[w] = carry[w][1]
        dq_ref[...] = (_merge_pair([carry[w][0] for w in heads]) * scale).astype(BF16)

        @pl.when(i == nq - 1)
        def _():
            dk_ref[...] = dk_acc[...].astype(BF16)
            dv_ref[...] = dv_acc[...].astype(BF16)

        if use_f:
            @pl.when(i == nq - 1)
            def _():
                def chunk(cidx, carry):
                    off = pl.multiple_of(cidx * LANE, LANE)
                    t = db_ref[pl.ds(off, LANE), :].T
                    for w in range(PAIR):
                        dkey_ref[w, :, pl.ds(off, LANE)] = t[w:w + 1, :]
                    return carry

                lax.fori_loop(0, Sk // LANE, chunk, 0)

    qblk = pl.BlockSpec((bq, LANE), lambda p, i: (i, p))
    kres = pl.BlockSpec((Sk, LANE), lambda p, i: (0, p))
    stat = pl.BlockSpec((PAIR, 1, bq), lambda p, i: (p, 0, i))
    in_specs = [qblk, kres, kres, qblk, stat, qblk]
    args = [q2, k2, v2, o2, lse, do2]
    out_specs = [qblk, kres, kres]
    out_shape = [jax.ShapeDtypeStruct((S, C), BF16), jax.ShapeDtypeStruct((Sk, C), BF16),
                 jax.ShapeDtypeStruct((Sk, C), BF16)]
    scratch = [pltpu.VMEM((Sk, LANE), F32), pltpu.VMEM((Sk, LANE), F32)]
    if use_f:
        in_specs = [pl.BlockSpec(memory_space=pltpu.SMEM)] + in_specs + [kres, stat]
        args = [first] + args + [bias, r]
        out_specs += [stat, pl.BlockSpec((PAIR, 1, Sk), lambda p, i: (p, 0, 0))]
        out_shape += [jax.ShapeDtypeStruct((PAIR * P, 1, S), F32), jax.ShapeDtypeStruct((PAIR * P, 1, Sk), F32)]
        scratch.append(pltpu.VMEM((Sk, LANE), F32))
    return pl.pallas_call(
        body, name=name, grid=(P, nq), in_specs=in_specs, out_specs=out_specs, out_shape=out_shape,
        scratch_shapes=scratch, compiler_params=_cp(("parallel", "arbitrary"), ATTN_VMEM_LIMIT),
    )(*args)


def _sbp_fwd(q2, k2, v2, name):
    S, C = q2.shape
    bq, bk = _attn_blocks(S, S, SB_BLOCK)
    assert bq == bk
    nq, P = S // bq, C // LANE
    gp = FWD_PAIRS if P % FWD_PAIRS == 0 else 1
    c = min(CUMSUM_CHUNK, bk)

    def body(q_ref, k_ref, v_ref, o_ref, lt_ref, js_ref):
        i = pl.program_id(1)
        after = _tri2(c, lambda s, j: j > s)
        heads = range(PAIR * gp)
        qps = [q_ref[:, LANE * (h // PAIR):LANE * (h // PAIR + 1)] for h in heads]
        qTs = [jnp.where(_head_lanes(qps[h].shape, h % PAIR, 1), qps[h], jnp.zeros_like(qps[h])).T for h in heads]

        def blk(jj, carry, masked):
            j = i - jj
            off = pl.multiple_of(j * bk, bk)
            kbs = [k_ref[pl.ds(off, bk), LANE * g:LANE * (g + 1)] for g in range(gp)]
            vTbs = [v_ref[pl.ds(off, bk), LANE * g:LANE * (g + 1)].T for g in range(gp)]
            logs = [_sb_logs(_dot(kbs[h // PAIR], qTs[h])) for h in heads]
            ls, lk = [t[0] for t in logs], [t[1] for t in logs]
            if masked:
                valid = _valid_t(i, j, bq, bk, True)
                lk = [jnp.where(valid, t, 0.0) for t in lk]
            cs = [_key_cumsum(lk[h], after, True, carry[h][0]) for h in heads]
            wgt = [jnp.exp(ls[h] + cs[h][0]) for h in heads]
            if masked:
                wgt = [jnp.where(valid, t, 0.0) for t in wgt]
            acc = [carry[h][1] + _dot(vTbs[h // PAIR], wgt[h].astype(BF16)) for h in heads]
            return tuple((carry[h][0] + cs[h][1], acc[h]) for h in heads)

        def step(jj, state):
            carry, first = state
            live = jnp.max(functools.reduce(jnp.maximum, [carry[h][0] for h in heads])) >= SB_DEAD
            carry = lax.cond(live, lambda cr: blk(jj, cr, False), lambda cr: cr, carry)
            return carry, jnp.where(live, i - jj, first)

        carry = tuple((jnp.zeros((1, bq), F32), jnp.zeros((LANE, bq), F32)) for _ in heads)
        carry = blk(0, carry, True)
        carry, first = lax.fori_loop(1, i + 1, step, (carry, i))
        js_ref[0] = jnp.full((1, bq), first, jnp.int32)
        for h in heads:
            lt_ref[h] = carry[h][0]
        for g in range(gp):
            o_ref[:, LANE * g:LANE * (g + 1)] = _merge_pair([carry[h][1] for h in range(PAIR * g, PAIR * (g + 1))])

    qblk = pl.BlockSpec((bq, LANE * gp), lambda p, i: (i, p))
    kres = pl.BlockSpec((S, LANE * gp), lambda p, i: (0, p))
    stat = pl.BlockSpec((PAIR * gp, 1, bq), lambda p, i: (p, 0, i))
    return pl.pallas_call(
        body, name=name, grid=(P // gp, nq),
        in_specs=[qblk, kres, kres],
        out_specs=[qblk, stat, pl.BlockSpec((1, 1, bq), lambda p, i: (p, 0, i))],
        out_shape=[jax.ShapeDtypeStruct((S, C), F32), jax.ShapeDtypeStruct((PAIR * P, 1, S), F32),
                   jax.ShapeDtypeStruct((P // gp, 1, S), jnp.int32)],
        compiler_params=_cp(("parallel", "arbitrary"), ATTN_VMEM_LIMIT),
    )(q2, k2, v2)


def _sbp_bwd(q2, k2, v2, lt, first, do2, scale, name):
    S, C = q2.shape
    bq, bk = _attn_blocks(S, S, SB_BLOCK)
    nq, P = S // bq, C // LANE
    c = min(CUMSUM_CHUNK, bk)

    per_group = P // first.shape[0]

    def body(first_ref, q_ref, k_ref, v_ref, lt_ref, do_ref, dq_ref, dk_ref, dv_ref, dk_acc, dv_acc):
        i = pl.program_id(1)

        @pl.when(i == 0)
        def _():
            dk_acc[...] = jnp.zeros_like(dk_acc)
            dv_acc[...] = jnp.zeros_like(dv_acc)

        qp = q_ref[...]
        dof = do_ref[...]
        upto = _tri2(c, lambda s, j: j <= s)
        before = _tri2(c, lambda s, j: j < s)
        heads = range(PAIR)
        mine = [_head_lanes(qp.shape, w, 1) for w in heads]
        qz = [jnp.where(mine[w], qp, jnp.zeros_like(qp)) for w in heads]
        qTs = [qz[w].T for w in heads]
        doz = [jnp.where(mine[w], dof, 0.0).astype(BF16) for w in heads]
        doT = [doz[w].T for w in heads]
        ltot = [lt_ref[w] for w in heads]

        def blk(j, carry, masked):
            off = pl.multiple_of(j * bk, bk)
            kb = k_ref[pl.ds(off, bk), :]
            vb = v_ref[pl.ds(off, bk), :]
            kTb = kb.T
            logs = [_sb_logs(_dot(kb, qTs[w])) for w in heads]
            ls, lk = [t[0] for t in logs], [t[1] for t in logs]
            if masked:
                valid = _valid_t(i, j, bq, bk, True)
                lk = [jnp.where(valid, t, 0.0) for t in lk]
            pin = [_key_cumsum(lk[w], upto, False, carry[w][1] - ltot[w]) for w in heads]
            wgt = [jnp.exp(ls[w] - pin[w][0]) for w in heads]
            if masked:
                wgt = [jnp.where(valid, t, 0.0) for t in wgt]
            g = [_dot(vb, doT[w]) * wgt[w] for w in heads]
            cin = [_key_cumsum(g[w], before, False, carry[w][2]) for w in heads]
            sig = [jnp.exp(t) for t in ls]
            dz = [g[w] * (1.0 - sig[w]) - cin[w][0] * sig[w] for w in heads]
            if masked:
                dz = [jnp.where(valid, t, 0.0) for t in dz]
            dzb = [t.astype(BF16) for t in dz]
            dvs = [_dot(wgt[w].astype(BF16), doz[w]) for w in heads]
            dks = [_dot(dzb[w], qz[w]) for w in heads]
            dv_acc[pl.ds(off, bk), :] += dvs[0] + dvs[1]
            dk_acc[pl.ds(off, bk), :] += dks[0] + dks[1]
            return tuple((carry[w][0] + _dot(kTb, dzb[w]), carry[w][1] + pin[w][1], carry[w][2] + cin[w][1])
                         for w in heads)

        carry = tuple((jnp.zeros((LANE, bq), F32), jnp.zeros((1, bq), F32), jnp.zeros((1, bq), F32)) for _ in heads)
        start = first_ref[pl.program_id(0) // per_group, i]
        carry = lax.fori_loop(start, i, lambda j, cr: blk(j, cr, False), carry)
        carry = blk(i, carry, True)
        dq_ref[...] = (_merge_pair([carry[w][0] for w in heads]) * scale).astype(BF16)

        @pl.when(i == nq - 1)
        def _():
            dk_ref[...] = dk_acc[...].astype(BF16)
            dv_ref[...] = dv_acc[...].astype(BF16)

    qblk = pl.BlockSpec((bq, LANE), lambda p, i: (i, p))
    kres = pl.BlockSpec((S, LANE), lambda p, i: (0, p))
    stat = pl.BlockSpec((PAIR, 1, bq), lambda p, i: (p, 0, i))
    return pl.pallas_call(
        body, name=name, grid=(P, nq),
        in_specs=[pl.BlockSpec(memory_space=pltpu.SMEM), qblk, kres, kres, stat, qblk],
        out_specs=[qblk, kres, kres],
        out_shape=[jax.ShapeDtypeStruct((S, C), BF16)] * 3,
        scratch_shapes=[pltpu.VMEM((S, LANE), F32), pltpu.VMEM((S, LANE), F32)],
        compiler_params=_cp(("parallel", "arbitrary"), ATTN_VMEM_LIMIT),
    )(first, q2, k2, v2, lt, do2)


def _bias_cols(f_cum):
    H, Sk = f_cum.shape
    terms = jnp.stack(_split3(f_cum), axis=-1)
    packed = terms.reshape(H // PAIR, PAIR, Sk, 3).transpose(2, 0, 1, 3).reshape(Sk, H // PAIR, PAIR * 3)
    return jnp.pad(packed, ((0, 0), (0, 0), (0, LANE - PAIR * 3))).reshape(Sk, -1)


def _make_packed_softmax(name, scale, causal, use_f):
    assert _pow2(scale)

    def run_fwd(q16, k16, v16, f_cum):
        q16 = q16 * scale
        if not use_f:
            o, lse = _smp_fwd(q16, k16, v16, None, None, causal, name + "_fwd")
            return o, (q16, k16, v16, o, lse, None, None, None)
        bq, bk = _attn_blocks(q16.shape[0], k16.shape[0], FOX_BLOCK)
        n_heads = f_cum.shape[0]
        knorm = jnp.sqrt(jnp.sum(jnp.square(k16.astype(F32)).reshape(-1, bk, n_heads, HEAD_DIM), axis=3))
        kstat = jnp.concatenate([jnp.max(knorm, axis=1).T.reshape(-1), f_cum[:, bk - 1::bk].reshape(-1)])
        bias, r = _bias_cols(f_cum), (f_cum * LOG2E)[:, None, :]
        o, lse, first = _smp_fwd(q16, k16, v16, bias, r, causal, name + "_fwd", lax.stop_gradient(kstat))
        return o, (q16, k16, v16, o, lse, bias, r, first[:, 0, ::bq])

    def run_bwd(saved, do):
        q16, k16, v16, o, lse, bias, r, first = saved
        outs = _smp_bwd(q16, k16, v16, o, lse, do, bias, r, scale, causal, name + "_bwd", first)
        if use_f:
            return outs[0], outs[1], outs[2], outs[3][:, 0, :] - outs[4][:, 0, :]
        return tuple(outs)

    if use_f:
        @jax.custom_vjp
        def attn(q, k, v, f_cum):
            return run_fwd(q, k, v, f_cum)[0]

        attn.defvjp(run_fwd, run_bwd)
    else:
        @jax.custom_vjp
        def attn(q, k, v):
            return run_fwd(q, k, v, None)[0]

        attn.defvjp(lambda q, k, v: run_fwd(q, k, v, None), run_bwd)
    return attn


def _make_packed_sb(name, scale):
    assert _pow2(scale)

    def run_fwd(q16, k16, v16):
        q16 = q16 * scale
        o, lt, first = _sbp_fwd(q16, k16, v16, name + "_fwd")
        bq, _ = _attn_blocks(q16.shape[0], q16.shape[0], SB_BLOCK)
        return o, (q16, k16, v16, lt, first[:, 0, ::bq])

    def run_bwd(saved, do):
        q16, k16, v16, lt, first = saved
        return tuple(_sbp_bwd(q16, k16, v16, lt, first, do, scale, name + "_bwd"))

    @jax.custom_vjp
    def attn(q, k, v):
        return run_fwd(q, k, v)[0]

    attn.defvjp(run_fwd, run_bwd)
    return attn


def _round_bf16(x):
    return lax.reduce_precision(x, exponent_bits=8, mantissa_bits=7)


def _split3(x):
    hi = _round_bf16(x)
    mid = _round_bf16(x - hi)
    lo = _round_bf16(x - hi - mid)
    return hi.astype(BF16), mid.astype(BF16), lo.astype(BF16)


def _pow2(x):
    m, _ = math.frexp(x)
    return m == 0.5


def _pad_last(x, n):
    return jnp.pad(x, [(0, 0)] * (x.ndim - 1) + [(0, n - x.shape[-1])])


def _layouts(q, k, scale):
    qh = _pad_last(jnp.transpose(q * scale if _pow2(scale) else q, (1, 0, 2)).astype(BF16), LANE)
    return qh, _pad_last(jnp.transpose(k, (1, 0, 2)).astype(BF16), LANE)


def _make_softmax_attn(name, scale, causal, d):
    pre = _pow2(scale)
    cmul = LOG2E if pre else scale * LOG2E
    gscale = 1.0 if pre else scale

    def run_fwd(q, k, v):
        qn, kn = _layouts(q, k, scale)
        vn = jnp.transpose(v, (1, 0, 2)).astype(BF16)
        oT, lse = _sm_fwd_t(qn, kn, jnp.transpose(vn, (0, 2, 1)), cmul, causal, name + "_fwd")
        return jnp.transpose(oT, (2, 0, 1)), (qn, kn, vn, oT, lse)

    def run_bwd(saved, dout):
        qn, kn, vn, oT, lse = saved
        doT = jnp.transpose(dout, (1, 2, 0))
        do = jnp.transpose(dout, (1, 0, 2)).astype(BF16)
        dq, dk, dv = _sm_bwd_t(qn, kn, vn, oT, lse, doT, do, cmul, gscale, causal, name + "_bwd")
        dq = jnp.transpose(dq[:, :, :d], (1, 0, 2))
        if pre:
            dq = dq * scale
        return dq, jnp.transpose(dk[:, :, :d], (1, 0, 2)), jnp.transpose(dv, (1, 0, 2))

    @jax.custom_vjp
    def attn(q, k, v):
        return run_fwd(q, k, v)[0]

    attn.defvjp(run_fwd, run_bwd)
    return attn


def _rope(x, positions):
    half = x.shape[-1] // 2
    inv_freq = ROPE_THETA ** (-jnp.arange(half, dtype=F32) / half)
    ang = positions.astype(F32)[:, None] * inv_freq[None, :]
    ang = ang.reshape((ang.shape[0],) + (1,) * (x.ndim - 2) + (half,))
    cos, sin = jnp.cos(ang), jnp.sin(ang)
    x1, x2 = x[..., :half], x[..., half:]
    return jnp.concatenate([x1 * cos - x2 * sin, x1 * sin + x2 * cos], axis=-1)


def _permute_cols(w):
    parts = [w[..., _ORIG_OFF[idx]:_ORIG_OFF[idx] + SPLIT_SIZES[idx]] for _, idx in _PERM]
    pad = jnp.zeros(w.shape[:-1] + (PROJ_COLS - IN_COLS,), w.dtype)
    return jnp.concatenate(parts + [pad], axis=-1)


def _unpermute_cols(w):
    start, parts = 0, [None] * len(SPLIT_SIZES)
    for _, idx in _PERM:
        parts[idx] = w[..., start:start + SPLIT_SIZES[idx]]
        start += SPLIT_SIZES[idx]
    return jnp.concatenate(parts, axis=-1)


_BF16_PIECES = ("fq", "fk", "fv", "sq", "sk", "sv", "mq")


def _make_ln_proj(name, has_res):
    wide = [(n, SPLIT_SIZES[idx]) for n, idx in _PERM if SPLIT_SIZES[idx] % LANE == 0]
    narrow = [(n, SPLIT_SIZES[idx]) for n, idx in _PERM if SPLIT_SIZES[idx] % LANE]
    assert [n for n, _ in wide + narrow] == [n for n, _ in _PERM]
    tail = PROJ_COLS - sum(w for _, w in wide)

    def run_fwd(x, res, g, b, w):
        h, h16, hT16 = _ln_fwd_call(x, res, g, b, name + "_ln_fwd", also16=True)
        w16 = w.astype(BF16)
        outs = _matmul_split(h16, w16, [w for _, w in wide] + [tail],
                             [BF16 if n in _BF16_PIECES else F32 for n, _ in wide] + [F32], name + "_fwd")
        pieces, off = list(outs[:-1]), 0
        for _, width in narrow:
            pieces.append(outs[-1][:, off:off + width])
            off += width
        return (h, tuple(pieces)), (x, res, g, hT16, w16)

    def run_bwd(saved, cts):
        x, res, g, hT16, w16 = saved
        dh, dpieces = cts
        wide = [c for c in dpieces if c.shape[1] % LANE == 0]
        narrow = [c.astype(BF16) for c in dpieces if c.shape[1] % LANE]
        pad = jnp.zeros((x.shape[0], PROJ_COLS - IN_COLS), BF16)
        da, dy16 = _concat_matmul_nt(wide + [jnp.concatenate(narrow + [pad], axis=1)], w16, name + "_dx")
        dw = _matmul(hT16, dy16, "nn", name + "_dw")
        outs = _ln_bwd_call(dh, x, res, g, name + "_ln_bwd", dy2=da)
        if has_res:
            dx, dr, dg, db = outs
            return dx, dr, dg.reshape(-1), db.reshape(-1), dw
        dx, dg, db = outs
        return dx, dg.reshape(-1), db.reshape(-1), dw

    if has_res:
        @jax.custom_vjp
        def op(x, res, g, b, w):
            return run_fwd(x, res, g, b, w)[0]

        op.defvjp(run_fwd, run_bwd)
    else:
        @jax.custom_vjp
        def op(x, g, b, w):
            return run_fwd(x, None, g, b, w)[0]

        op.defvjp(lambda x, g, b, w: run_fwd(x, None, g, b, w), run_bwd)

    def call(*args):
        h, pieces = op(*args)
        return h, {n: part for (n, _), part in zip(_PERM, pieces)}

    return call


def _trunk_loss(wts, x2d, mem2d, target2d):
    s = x2d.shape[0]
    positions = jnp.arange(s)
    head_scale = HEAD_DIM ** -0.5
    mla_scale = (MLA_NOPE + MLA_ROPE) ** -0.5

    mem_n = _make_ln("ln_mem", False)(mem2d, wts["mem_ln_g"], wts["mem_ln_b"])
    h, y = None, x2d
    for l in range(DEPTH):
        tag = f"l{l}_"
        w_p = wts["w_in"][l]
        if l == 0:
            h, p = _make_ln_proj(tag + "proj", False)(y, wts["ln_in_g"], wts["ln_in_b"], w_p)
        else:
            h, p = _make_ln_proj(tag + "proj", True)(y, h, wts["ln_g"][l - 1], wts["ln_b"][l - 1], w_p)

        log_f = jax.nn.log_sigmoid(p["f_logit"] + wts["b_forget"][l])
        f_cum = jnp.cumsum(log_f, axis=0).T
        out_fox = _make_packed_softmax(tag + "fox", head_scale, True, True)(p["fq"], p["fk"], p["fv"], f_cum)

        out_sb = _make_packed_sb(tag + "sb", head_scale)(p["sq"], p["sk"], p["sv"])

        cqn = _make_rms(tag + "rms_q")(p["c_q"], wts["mla_q_norm_g"][l])
        q_mla = _make_mm(tag + "q_up")(cqn, wts["w_mla_q_up"][l]).reshape(s, N_HEADS, MLA_NOPE + MLA_ROPE)
        ckvn = _make_rms(tag + "rms_kv")(p["c_kv"], wts["mla_kv_norm_g"][l])
        kv_mla = _make_mm(tag + "kv_up")(ckvn, wts["w_mla_kv_up"][l]).reshape(s, N_HEADS, MLA_NOPE + MLA_V)
        q_full = jnp.concatenate([q_mla[..., :MLA_NOPE], _rope(q_mla[..., MLA_NOPE:], positions)], axis=-1)
        k_rope = jnp.broadcast_to(_rope(p["k_rot"], positions)[:, None, :], (s, N_HEADS, MLA_ROPE))
        k_full = jnp.concatenate([kv_mla[..., :MLA_NOPE], k_rope], axis=-1)
        out_mla = _make_softmax_attn(tag + "mla", mla_scale, True, MLA_NOPE + MLA_ROPE)(
            q_full, k_full, kv_mla[..., MLA_NOPE:]).reshape(s, GROUP_W)

        mkv = _make_mm(tag + "mem_kv")(mem_n, wts["w_mem_kv"][l])
        out_mem = _make_packed_softmax(tag + "mem", head_scale, False, False)(
            p["mq"], mkv[:, :GROUP_W].astype(BF16), mkv[:, GROUP_W:].astype(BF16))

        y = _make_gate_out(tag + "out")((out_fox, out_sb, out_mla, out_mem), p["gate"], wts["w_out"][l])

    h = _make_ln(f"l{DEPTH - 1}_ln", True)(y, h, wts["ln_g"][DEPTH - 1], wts["ln_b"][DEPTH - 1])
    return _loss_op(h, target2d)


def _mesh_pos():
    x, y, c = (lax.axis_index(a) for a in MESH_AXES)
    return x, y, c, 4 * x + 2 * y + c


def _peer(x, y, c, mask):
    return (x ^ ((mask >> 2) & 1), y ^ ((mask >> 1) & 1), c ^ (mask & 1))


_ANY = pl.BlockSpec(memory_space=pl.ANY)


def _all_gather(row_shards, stack_shards):
    n_row, n_all = len(row_shards), len(row_shards) + len(stack_shards)
    shards = list(row_shards) + list(stack_shards)
    chip_masks = (4, 2, 6)
    tensors = range(n_all)

    def body(*refs):
        ins, outs = refs[:n_all], refs[n_all:2 * n_all]
        send_sems, recv_sems, local_sems = refs[2 * n_all:]
        x, y, c, me = _mesh_pos()
        sibling = _peer(x, y, c, 1)

        def window(t, slot):
            if t < n_row:
                rows = shards[t].shape[1]
                return outs[t].at[:, pl.ds(slot * rows, rows), :]
            return outs[t].at[slot]

        def copy(t, k, slot, to, src=None):
            return pltpu.make_async_remote_copy(
                src_ref=window(t, slot) if src is None else src, dst_ref=window(t, slot),
                send_sem=send_sems.at[t, k], recv_sem=recv_sems.at[t, k], device_id=to,
                device_id_type=pl.DeviceIdType.MESH)

        local = [pltpu.make_async_copy(ins[t], window(t, me), local_sems.at[t]) for t in tensors]
        for cp in local:
            cp.start()
        first = [copy(t, 0, me, sibling, src=ins[t]) for t in tensors]
        first += [copy(t, 1 + j, me, _peer(x, y, c, m), src=ins[t]) for j, m in enumerate(chip_masks) for t in tensors]
        for cp in first:
            cp.start()
        passed = []
        for j, m in enumerate(chip_masks):
            for t in tensors:
                copy(t, 1 + j, me ^ m, sibling).wait_recv()
            for t in tensors:
                cp = copy(t, 4 + j, me ^ m, sibling)
                cp.start()
                passed.append(cp)
        for t in tensors:
            copy(t, 0, me ^ 1, sibling).wait_recv()
        for j, m in enumerate(chip_masks):
            for t in tensors:
                copy(t, 4 + j, me ^ m ^ 1, sibling).wait_recv()
        for cp in first + passed:
            cp.wait_send()
        for cp in local:
            cp.wait()

    out_shape = [jax.ShapeDtypeStruct((a.shape[0], N_DEV * a.shape[1], a.shape[2]), a.dtype) for a in row_shards]
    out_shape += [jax.ShapeDtypeStruct((N_DEV,) + a.shape, a.dtype) for a in stack_shards]
    return pl.pallas_call(
        body, name="all_gather_weights", in_specs=[_ANY] * n_all, out_specs=[_ANY] * n_all, out_shape=out_shape,
        scratch_shapes=[pltpu.SemaphoreType.DMA((n_all, N_DEV - 1)), pltpu.SemaphoreType.DMA((n_all, N_DEV - 1)),
                        pltpu.SemaphoreType.DMA((n_all,))],
    )(*shards)


def _reduce_scatter(row_full, stack_full, bcast):
    n_row, n_stack = len(row_full), len(stack_full)
    n_all = n_row + n_stack + len(bcast)
    fulls = list(row_full) + list(stack_full) + list(bcast)

    def body(*refs):
        ins, outs = refs[:n_all], refs[n_all:2 * n_all]
        send_sems, recv_sems, local_sems = refs[2 * n_all:]
        x, y, c, me = _mesh_pos()

        def part(t, slot):
            if t < n_row:
                rows = fulls[t].shape[1] // N_DEV
                return ins[t].at[:, pl.ds(slot * rows, rows), :]
            if t < n_row + n_stack:
                return ins[t].at[slot]
            return ins[t]

        local = [pltpu.make_async_copy(part(t, me), outs[t].at[me], local_sems.at[t]) for t in range(n_all)]
        for cp in local:
            cp.start()
        sends = []
        for mask in range(1, N_DEV):
            for t in range(n_all):
                cp = pltpu.make_async_remote_copy(
                    src_ref=part(t, me ^ mask), dst_ref=outs[t].at[me], send_sem=send_sems.at[t, mask - 1],
                    recv_sem=recv_sems.at[t, mask - 1], device_id=_peer(x, y, c, mask),
                    device_id_type=pl.DeviceIdType.MESH)
                cp.start()
                sends.append(cp)
        for mask in range(1, N_DEV):
            for t in range(n_all):
                pltpu.make_async_remote_copy(
                    src_ref=part(t, me), dst_ref=outs[t].at[me ^ mask], send_sem=send_sems.at[t, mask - 1],
                    recv_sem=recv_sems.at[t, mask - 1], device_id=_peer(x, y, c, mask),
                    device_id_type=pl.DeviceIdType.MESH).wait_recv()
        for cp in sends:
            cp.wait_send()
        for cp in local:
            cp.wait()

    out_shape = [jax.ShapeDtypeStruct((N_DEV, a.shape[0], a.shape[1] // N_DEV, a.shape[2]), a.dtype) for a in row_full]
    out_shape += [jax.ShapeDtypeStruct(a.shape, a.dtype) for a in stack_full]
    out_shape += [jax.ShapeDtypeStruct((N_DEV,) + a.shape, a.dtype) for a in bcast]
    return pl.pallas_call(
        body, name="reduce_scatter_grads", in_specs=[_ANY] * n_all, out_specs=[_ANY] * n_all, out_shape=out_shape,
        scratch_shapes=[pltpu.SemaphoreType.DMA((n_all, N_DEV - 1)), pltpu.SemaphoreType.DMA((n_all, N_DEV - 1)),
                        pltpu.SemaphoreType.DMA((n_all,))],
    )(*fulls)


def _adamw(slots, w, m, v, name):
    shape = w.shape
    cols = shape[-1]
    rows = math.prod(shape[:-1])
    tr = _pick(rows, (64, 32, 16, 8))
    c1 = 1.0 - ADAM_B1 ** ADAM_STEP
    c2 = 1.0 - ADAM_B2 ** ADAM_STEP

    def body(s_ref, w_ref, m_ref, v_ref, g_ref, d_ref, nm_ref, nv_ref):
        g = s_ref[0].astype(F32)
        for k in range(1, N_DEV):
            g = g + s_ref[k].astype(F32)
        nm = ADAM_B1 * m_ref[...] + (1.0 - ADAM_B1) * g
        nv = ADAM_B2 * v_ref[...] + (1.0 - ADAM_B2) * (g * g)
        g_ref[...] = g
        nm_ref[...] = nm
        nv_ref[...] = nv
        d_ref[...] = -ADAM_LR * ((nm / c1) / (jnp.sqrt(nv / c2) + ADAM_EPS) + ADAM_WD * w_ref[...])

    row = pl.BlockSpec((tr, cols), lambda i: (i, 0))
    out = jax.ShapeDtypeStruct((rows, cols), F32)
    outs = pl.pallas_call(
        body, name=name, grid=(rows // tr,),
        in_specs=[pl.BlockSpec((N_DEV, tr, cols), lambda i: (0, i, 0)), row, row, row],
        out_specs=[row] * 4, out_shape=[out] * 4, compiler_params=_cp(("parallel",)),
    )(slots.reshape(N_DEV, rows, cols), w.reshape(rows, cols), m.reshape(rows, cols), v.reshape(rows, cols))
    return [o.reshape(shape) for o in outs]


_SMALL = ("ln_in_g", "ln_in_b", "mem_ln_g", "mem_ln_b", "b_forget", "mla_q_norm_g", "mla_kv_norm_g", "ln_g", "ln_b")
_ORDER = ("ln_in_g", "ln_in_b", "mem_ln_g", "mem_ln_b", "w_in", "b_forget", "mla_q_norm_g", "w_mla_q_up",
          "mla_kv_norm_g", "w_mla_kv_up", "w_mem_kv", "w_out", "ln_g", "ln_b")


def _pack_small(d):
    flat = jnp.concatenate([d[n].reshape(-1) for n in _SMALL])
    n = flat.shape[0]
    padded = ((n + 8 * LANE - 1) // (8 * LANE)) * (8 * LANE)
    return jnp.pad(flat, (0, padded - n)).reshape(-1, LANE)


def _unpack_small(packed, like):
    flat, out, off = packed.reshape(-1), {}, 0
    for n in _SMALL:
        size = math.prod(like[n].shape)
        out[n] = flat[off:off + size].reshape(like[n].shape)
        off += size
    return out


def _unstack_cols(g):
    n, l, r, c = g.shape
    return g.transpose(1, 2, 0, 3).reshape(l, r, n * c)


def _stack_cols(g):
    l, r, nc = g.shape
    return g.reshape(l, r, N_DEV, nc // N_DEV).transpose(2, 0, 1, 3)


def kernel(x, mem, ln_in_g, ln_in_b, mem_ln_g, mem_ln_b, w_in, b_forget, mla_q_norm_g, w_mla_q_up, mla_kv_norm_g, w_mla_kv_up, w_mem_kv, w_out, ln_g, ln_b, loss_target, m_ln_in_g, m_ln_in_b, m_mem_ln_g, m_mem_ln_b, m_w_in, m_b_forget, m_mla_q_norm_g, m_w_mla_q_up, m_mla_kv_norm_g, m_w_mla_kv_up, m_w_mem_kv, m_w_out, m_ln_g, m_ln_b, v_ln_in_g, v_ln_in_b, v_mem_ln_g, v_mem_ln_b, v_w_in, v_b_forget, v_mla_q_norm_g, v_w_mla_q_up, v_mla_kv_norm_g, v_w_mla_kv_up, v_w_mem_kv, v_w_out, v_ln_g, v_ln_b):
    w_shard = dict(ln_in_g=ln_in_g, ln_in_b=ln_in_b, mem_ln_g=mem_ln_g, mem_ln_b=mem_ln_b, w_in=w_in,
                   b_forget=b_forget, mla_q_norm_g=mla_q_norm_g, w_mla_q_up=w_mla_q_up,
                   mla_kv_norm_g=mla_kv_norm_g, w_mla_kv_up=w_mla_kv_up, w_mem_kv=w_mem_kv, w_out=w_out,
                   ln_g=ln_g, ln_b=ln_b)
    m_shard = dict(ln_in_g=m_ln_in_g, ln_in_b=m_ln_in_b, mem_ln_g=m_mem_ln_g, mem_ln_b=m_mem_ln_b, w_in=m_w_in,
                   b_forget=m_b_forget, mla_q_norm_g=m_mla_q_norm_g, w_mla_q_up=m_w_mla_q_up,
                   mla_kv_norm_g=m_mla_kv_norm_g, w_mla_kv_up=m_w_mla_kv_up, w_mem_kv=m_w_mem_kv, w_out=m_w_out,
                   ln_g=m_ln_g, ln_b=m_ln_b)
    v_shard = dict(ln_in_g=v_ln_in_g, ln_in_b=v_ln_in_b, mem_ln_g=v_mem_ln_g, mem_ln_b=v_mem_ln_b, w_in=v_w_in,
                   b_forget=v_b_forget, mla_q_norm_g=v_mla_q_norm_g, w_mla_q_up=v_w_mla_q_up,
                   mla_kv_norm_g=v_mla_kv_norm_g, w_mla_kv_up=v_w_mla_kv_up, w_mem_kv=v_w_mem_kv, w_out=v_w_out,
                   ln_g=v_ln_g, ln_b=v_ln_b)

    to16 = lambda ws: [a.astype(BF16) for a in ws]
    gathered = _all_gather(to16([_permute_cols(w_in), w_mem_kv, w_out]), to16([w_mla_q_up, w_mla_kv_up]))
    g_in, g_mem, g_out, g_qup, g_kvup = [a.astype(F32) for a in gathered]
    full = dict(w_shard)
    full.update(w_in=g_in, w_mem_kv=g_mem, w_out=g_out, w_mla_q_up=_unstack_cols(g_qup),
                w_mla_kv_up=_unstack_cols(g_kvup))

    loss_local, (grad_w, grad_x) = jax.value_and_grad(_trunk_loss, argnums=(0, 1))(
        full, x[0], mem[0], loss_target[0])

    s_in, s_mem, s_out, s_qup, s_kvup, s_small = _reduce_scatter(
        to16([grad_w["w_in"], grad_w["w_mem_kv"], grad_w["w_out"]]),
        to16([_stack_cols(grad_w["w_mla_q_up"]), _stack_cols(grad_w["w_mla_kv_up"])]),
        [_pack_small(grad_w)])

    res = {}
    for name, slots in (("w_mem_kv", s_mem), ("w_out", s_out), ("w_mla_q_up", s_qup), ("w_mla_kv_up", s_kvup)):
        res[name] = _adamw(slots, w_shard[name], m_shard[name], v_shard[name], "adamw_" + name)
    res["w_in"] = [_unpermute_cols(a) for a in _adamw(
        s_in, _permute_cols(w_in), _permute_cols(m_w_in), _permute_cols(v_w_in), "adamw_w_in")]
    small = _adamw(s_small, _pack_small(w_shard), _pack_small(m_shard), _pack_small(v_shard), "adamw_small")
    small = [_unpack_small(a, w_shard) for a in small]
    for name in _SMALL:
        res[name] = [a[name] for a in small]

    loss = lax.psum(loss_local, MESH_AXES)
    outs = [loss, grad_x[None]]
    for k in range(4):
        outs += [res[name][k] for name in _ORDER]
    return tuple(outs)
```

```python
import functools
import math

import jax
import jax.numpy as jnp
from jax import lax
from jax.experimental import pallas as pl
from jax.experimental.pallas import tpu as pltpu

F32 = jnp.float32
BF16 = jnp.bfloat16

D_MODEL = 1024
DEPTH = 2
GROUP_W = 256
N_HEADS = 4
HEAD_DIM = 64
MLA_Q_RANK = 256
MLA_KV_RANK = 128
MLA_NOPE = 64
MLA_ROPE = 32
MLA_V = 64
ROPE_THETA = 10000.0
LN_EPS = 1e-5
RMS_EPS = 1e-6
DEEPNORM_ALPHA = (2 * DEPTH) ** 0.25
SPLIT_SIZES = (256, 256, 256, 4, 256, 256, 256, 256, 128, 32, 256, 1024)
IN_COLS = sum(SPLIT_SIZES)
_ORIG_OFF = [sum(SPLIT_SIZES[:i]) for i in range(len(SPLIT_SIZES))]
_PERM = (("fq", 0), ("fk", 1), ("fv", 2), ("sq", 4), ("sk", 5), ("sv", 6), ("c_q", 7), ("c_kv", 8),
         ("mq", 10), ("gate", 11), ("k_rot", 9), ("f_logit", 3))
LANE = 128
PROJ_COLS = ((IN_COLS + LANE - 1) // LANE) * LANE

ADAM_LR = 0.001
ADAM_B1 = 0.9
ADAM_B2 = 0.999
ADAM_EPS = 1e-08
ADAM_WD = 0.01
ADAM_STEP = 10

N_DEV = 8
MESH_AXES = ("x", "y", "c")
VMEM_LIMIT = 48 * 1024 * 1024
ATTN_VMEM_LIMIT = 56 * 1024 * 1024
ATTN_BQ = 512
ATTN_BK = 512
CUMSUM_CHUNK = 256
NEG_BIG = -1e30
LOG2E = math.log2(math.e)
MM_TM, MM_TN, MM_TK, MM_TK_NT = 1024, 1664, 1024, 3328
CONCAT_MM_TM = 512

_NT = (((1,), (1,)), ((), ()))
_NN = (((1,), (0,)), ((), ()))


def _cp(sem, vmem=VMEM_LIMIT):
    return pltpu.CompilerParams(dimension_semantics=sem, vmem_limit_bytes=vmem)


def _dot(a, b, dims=_NN):
    return lax.dot_general(a, b, dims, preferred_element_type=F32)


def _pick(n, cands):
    for c in cands:
        if c <= n and n % c == 0:
            return c
    return n


def _tile(n, cap):
    if n <= cap:
        return n
    best = None
    for d in range(LANE, cap + 1, LANE):
        if n % d == 0:
            best = d
    assert best is not None, (n, cap)
    return best


def _matmul(a, b, mode, name):
    if mode == "nn":
        (M, K), (K2, N) = a.shape, b.shape
    else:
        (M, K), (N, K2) = a.shape, b.shape
    assert K == K2 and a.dtype == BF16 and b.dtype == BF16, (a.shape, b.shape, mode)
    tm, tn = _tile(M, MM_TM), _tile(N, MM_TN)
    tk = _tile(K, MM_TK if mode == "nn" else MM_TK_NT)
    nk = K // tk
    dims = _NN if mode == "nn" else _NT

    def body(a_ref, b_ref, o_ref, acc_ref):
        part = _dot(a_ref[...], b_ref[...], dims)
        if nk == 1:
            o_ref[...] = part
        else:
            k = pl.program_id(2)

            @pl.when(k == 0)
            def _():
                acc_ref[...] = part

            @pl.when(k > 0)
            def _():
                acc_ref[...] += part

            @pl.when(k == nk - 1)
            def _():
                o_ref[...] = acc_ref[...]

    a_spec = pl.BlockSpec((tm, tk), lambda j, i, k: (i, k))
    if mode == "nn":
        b_spec = pl.BlockSpec((tk, tn), lambda j, i, k: (k, j))
    else:
        b_spec = pl.BlockSpec((tn, tk), lambda j, i, k: (j, k))
    acc_shape = (tm, tn) if nk > 1 else (8, LANE)
    o_spec = pl.BlockSpec((tm, tn), lambda j, i, k: (i, j))
    return pl.pallas_call(
        body, name=name, grid=(N // tn, M // tm, nk),
        in_specs=[a_spec, b_spec], out_specs=o_spec, out_shape=jax.ShapeDtypeStruct((M, N), F32),
        scratch_shapes=[pltpu.VMEM(acc_shape, F32)],
        compiler_params=_cp(("parallel", "parallel", "arbitrary")),
    )(a, b)


def _concat_matmul_nt(pieces, b, name):
    M, (N, K) = pieces[0].shape[0], b.shape
    widths = [p.shape[1] for p in pieces]
    assert sum(widths) == K and all(w % LANE == 0 for w in widths) and b.dtype == BF16, (widths, b.shape)
    tm = _tile(M, CONCAT_MM_TM)
    n = len(pieces)

    def body(*refs):
        b_ref, o_ref, a_ref = refs[n:]
        a_ref[...] = jnp.concatenate([r[...].astype(BF16) for r in refs[:n]], axis=1)
        o_ref[...] = _dot(a_ref[...], b_ref[...], _NT)

    rows = lambda w: pl.BlockSpec((tm, w), lambda i: (i, 0))
    return pl.pallas_call(
        body, name=name, grid=(M // tm,),
        in_specs=[rows(w) for w in widths] + [pl.BlockSpec((N, K), lambda i: (0, 0))],
        out_specs=[rows(N), rows(K)],
        out_shape=[jax.ShapeDtypeStruct((M, N), F32), jax.ShapeDtypeStruct((M, K), BF16)],
        compiler_params=_cp(("parallel",)),
    )(*pieces, b)


def _matmul_split(a, b, widths, dtypes, name):
    (M, K), (K2, N) = a.shape, b.shape
    assert K == K2 and sum(widths) == N and all(w % LANE == 0 for w in widths), (a.shape, b.shape, widths)
    assert a.dtype == BF16 and b.dtype == BF16
    tm = _tile(M, CONCAT_MM_TM)
    offs = [sum(widths[:r]) for r in range(len(widths))]

    def body(a_ref, b_ref, *o_refs):
        av = a_ref[...]
        for o_ref, off, w in zip(o_refs, offs, widths):
            o_ref[...] = _dot(av, b_ref[:, off:off + w]).astype(o_ref.dtype)

    rows = lambda w: pl.BlockSpec((tm, w), lambda i: (i, 0))
    return pl.pallas_call(
        body, name=name, grid=(M // tm,),
        in_specs=[rows(K), pl.BlockSpec((K, N), lambda i: (0, 0))],
        out_specs=[rows(w) for w in widths],
        out_shape=[jax.ShapeDtypeStruct((M, w), d) for w, d in zip(widths, dtypes)],
        compiler_params=_cp(("parallel",)),
    )(a, b)


def _make_mm(name):
    @jax.custom_vjp
    def mm(a, w):
        return _matmul(a.astype(BF16), w.astype(BF16), "nn", name + "_fwd")

    def fwd(a, w):
        a16, w16 = a.astype(BF16), w.astype(BF16)
        return _matmul(a16, w16, "nn", name + "_fwd"), (a16, w16)

    def bwd(res, dy):
        a16, w16 = res
        dy16 = dy.astype(BF16)
        da = _matmul(dy16, w16, "nt", name + "_dx")
        dw = _matmul(a16.T, dy16, "nn", name + "_dw")
        return da, dw

    mm.defvjp(fwd, bwd)
    return mm


def _row_tile(rows):
    return _pick(rows, (512, 256, 128, 64, 32, 16, 8))


def _ln_stats(u):
    mu = jnp.mean(u, axis=-1, keepdims=True)
    d = u - mu
    var = jnp.mean(d * d, axis=-1, keepdims=True)
    return d, lax.rsqrt(var + LN_EPS)


def _ln_fwd_call(x, res, g, b, name, also16=False):
    rows, dm = x.shape
    tr = _row_tile(rows)
    has_res = res is not None
    n_in = 2 if has_res else 1

    def body(*refs):
        if has_res:
            u = DEEPNORM_ALPHA * refs[1][...] + refs[0][...]
        else:
            u = refs[0][...]
        g_ref, b_ref = refs[n_in], refs[n_in + 1]
        d, rstd = _ln_stats(u)
        y = d * rstd * g_ref[...] + b_ref[...]
        refs[n_in + 2][...] = y
        if also16:
            y16 = y.astype(BF16)
            refs[n_in + 3][...] = y16
            refs[n_in + 4][...] = y16.T

    row = pl.BlockSpec((tr, dm), lambda i: (i, 0))
    vec = pl.BlockSpec((1, dm), lambda i: (0, 0))
    args = (x, res) if has_res else (x,)
    out_specs, out_shape = [row], [jax.ShapeDtypeStruct((rows, dm), F32)]
    if also16:
        out_specs += [row, pl.BlockSpec((dm, tr), lambda i: (0, i))]
        out_shape += [jax.ShapeDtypeStruct((rows, dm), BF16), jax.ShapeDtypeStruct((dm, rows), BF16)]
    outs = pl.pallas_call(
        body, name=name, grid=(rows // tr,),
        in_specs=[row] * n_in + [vec, vec], out_specs=out_specs, out_shape=out_shape,
        compiler_params=_cp(("parallel",)),
    )(*args, g.reshape(1, dm), b.reshape(1, dm))
    return outs if also16 else outs[0]


def _ln_bwd_call(dy, x, res, g, name, dy2=None):
    rows, dm = x.shape
    tr = _row_tile(rows)
    has_res = res is not None
    two = dy2 is not None

    def body(*refs):
        dy_ref, refs = refs[0], refs[1:]
        if two:
            dy2_ref, refs = refs[0], refs[1:]
        if has_res:
            x_ref, r_ref, g_ref, dx_ref, dr_ref, dg_ref, db_ref = refs
            u = DEEPNORM_ALPHA * r_ref[...] + x_ref[...]
        else:
            x_ref, g_ref, dx_ref, dg_ref, db_ref = refs
            u = x_ref[...]
        i = pl.program_id(0)
        d, rstd = _ln_stats(u)
        xhat = d * rstd
        dyv = dy_ref[...] + dy2_ref[...] if two else dy_ref[...]
        dxh = dyv * g_ref[...]
        m1 = jnp.mean(dxh, axis=-1, keepdims=True)
        m2 = jnp.mean(dxh * xhat, axis=-1, keepdims=True)
        du = rstd * (dxh - m1 - xhat * m2)
        dx_ref[...] = du
        if has_res:
            dr_ref[...] = DEEPNORM_ALPHA * du
        pg = jnp.sum(dyv * xhat, axis=0, keepdims=True)
        pb = jnp.sum(dyv, axis=0, keepdims=True)

        @pl.when(i == 0)
        def _():
            dg_ref[...] = pg
            db_ref[...] = pb

        @pl.when(i > 0)
        def _():
            dg_ref[...] += pg
            db_ref[...] += pb

    row = pl.BlockSpec((tr, dm), lambda i: (i, 0))
    vec = pl.BlockSpec((1, dm), lambda i: (0, 0))
    big = jax.ShapeDtypeStruct((rows, dm), F32)
    small = jax.ShapeDtypeStruct((1, dm), F32)
    args = ((dy, dy2) if two else (dy,)) + ((x, res) if has_res else (x,))
    n_big = 2 if has_res else 1
    outs = pl.pallas_call(
        body, name=name, grid=(rows // tr,),
        in_specs=[row] * len(args) + [vec],
        out_specs=[row] * n_big + [vec, vec],
        out_shape=[big] * n_big + [small, small],
        compiler_params=_cp(("arbitrary",)),
    )(*args, g.reshape(1, dm))
    return outs


def _make_ln(name, has_res):
    if has_res:
        @jax.custom_vjp
        def ln(x, res, g, b):
            return _ln_fwd_call(x, res, g, b, name + "_fwd")

        def fwd(x, res, g, b):
            return ln(x, res, g, b), (x, res, g)

        def bwd(saved, dy):
            x, res, g = saved
            dx, dr, dg, db = _ln_bwd_call(dy, x, res, g, name + "_bwd")
            return dx, dr, dg.reshape(-1), db.reshape(-1)
    else:
        @jax.custom_vjp
        def ln(x, g, b):
            return _ln_fwd_call(x, None, g, b, name + "_fwd")

        def fwd(x, g, b):
            return ln(x, g, b), (x, g)

        def bwd(saved, dy):
            x, g = saved
            dx, dg, db = _ln_bwd_call(dy, x, None, g, name + "_bwd")
            return dx, dg.reshape(-1), db.reshape(-1)

    ln.defvjp(fwd, bwd)
    return ln


def _rms_fwd_call(x, g, name):
    rows, dm = x.shape
    tr = _row_tile(rows)

    def body(x_ref, g_ref, o_ref):
        xv = x_ref[...]
        rstd = lax.rsqrt(jnp.mean(xv * xv, axis=-1, keepdims=True) + RMS_EPS)
        o_ref[...] = xv * rstd * g_ref[...]

    row = pl.BlockSpec((tr, dm), lambda i: (i, 0))
    vec = pl.BlockSpec((1, dm), lambda i: (0, 0))
    return pl.pallas_call(
        body, name=name, grid=(rows // tr,), in_specs=[row, vec], out_specs=row,
        out_shape=jax.ShapeDtypeStruct((rows, dm), F32), compiler_params=_cp(("parallel",)),
    )(x, g.reshape(1, dm))


def _rms_bwd_call(dy, x, g, name):
    rows, dm = x.shape
    tr = _row_tile(rows)

    def body(dy_ref, x_ref, g_ref, dx_ref, dg_ref):
        i = pl.program_id(0)
        xv = x_ref[...]
        dyv = dy_ref[...]
        rstd = lax.rsqrt(jnp.mean(xv * xv, axis=-1, keepdims=True) + RMS_EPS)
        xhat = xv * rstd
        dxh = dyv * g_ref[...]
        m2 = jnp.mean(dxh * xhat, axis=-1, keepdims=True)
        dx_ref[...] = rstd * (dxh - xhat * m2)
        pg = jnp.sum(dyv * xhat, axis=0, keepdims=True)

        @pl.when(i == 0)
        def _():
            dg_ref[...] = pg

        @pl.when(i > 0)
        def _():
            dg_ref[...] += pg

    row = pl.BlockSpec((tr, dm), lambda i: (i, 0))
    vec = pl.BlockSpec((1, dm), lambda i: (0, 0))
    return pl.pallas_call(
        body, name=name, grid=(rows // tr,), in_specs=[row, row, vec], out_specs=[row, vec],
        out_shape=[jax.ShapeDtypeStruct((rows, dm), F32), jax.ShapeDtypeStruct((1, dm), F32)],
        compiler_params=_cp(("arbitrary",)),
    )(dy, x, g.reshape(1, dm))


def _make_rms(name):
    @jax.custom_vjp
    def rms(x, g):
        return _rms_fwd_call(x, g, name + "_fwd")

    def fwd(x, g):
        return rms(x, g), (x, g)

    def bwd(saved, dy):
        x, g = saved
        dx, dg = _rms_bwd_call(dy, x, g, name + "_bwd")
        return dx, dg.reshape(-1)

    rms.defvjp(fwd, bwd)
    return rms


def _sigmoid(x):
    return 1.0 / (1.0 + jnp.exp(-x))


def _gate_fwd_call(parts, gate, name):
    rows, dm = gate.shape
    tr = _row_tile(rows)
    n = len(parts)

    def body(*refs):
        g_ref, o_ref, oT_ref = refs[n:]
        gv = g_ref[...]
        mixed = jnp.concatenate([r[...] for r in refs[:n]], axis=1)
        y16 = (mixed * (gv * _sigmoid(gv))).astype(BF16)
        o_ref[...] = y16
        oT_ref[...] = y16.T

    row = pl.BlockSpec((tr, dm), lambda i: (i, 0))
    part_specs = [pl.BlockSpec((tr, p.shape[1]), lambda i: (i, 0)) for p in parts]
    return pl.pallas_call(
        body, name=name, grid=(rows // tr,), in_specs=part_specs + [row],
        out_specs=[row, pl.BlockSpec((dm, tr), lambda i: (0, i))],
        out_shape=[jax.ShapeDtypeStruct((rows, dm), BF16), jax.ShapeDtypeStruct((dm, rows), BF16)],
        compiler_params=_cp(("parallel",)),
    )(*parts, gate)


def _gate_bwd_call(dy, parts, gate, name):
    rows, dm = gate.shape
    tr = _row_tile(rows)
    n = len(parts)
    widths = [p.shape[1] for p in parts]

    def body(*refs):
        dy_ref, g_ref = refs[0], refs[n + 1]
        dm_refs, dg_ref = refs[n + 2:2 * n + 2], refs[2 * n + 2]
        gv = g_ref[...]
        dyv = dy_ref[...]
        sg = _sigmoid(gv)
        mixed = jnp.concatenate([r[...] for r in refs[1:n + 1]], axis=1)
        dmixed = dyv * (gv * sg)
        off = 0
        for r, w in zip(dm_refs, widths):
            r[...] = dmixed[:, off:off + w]
            off += w
        dg_ref[...] = dyv * mixed * (sg * (1.0 + gv * (1.0 - sg)))

    row = pl.BlockSpec((tr, dm), lambda i: (i, 0))
    part_specs = [pl.BlockSpec((tr, w), lambda i: (i, 0)) for w in widths]
    return pl.pallas_call(
        body, name=name, grid=(rows // tr,), in_specs=[row] + part_specs + [row], out_specs=part_specs + [row],
        out_shape=[jax.ShapeDtypeStruct((rows, w), F32) for w in widths] + [jax.ShapeDtypeStruct((rows, dm), F32)],
        compiler_params=_cp(("parallel",)),
    )(dy, *parts, gate)


def _make_gate_out(name):
    def run_fwd(parts, gate, w):
        g16, gT16 = _gate_fwd_call(parts, gate, name + "_gate_fwd")
        w16 = w.astype(BF16)
        return _matmul(g16, w16, "nn", name + "_fwd"), (parts, gate, gT16, w16)

    def run_bwd(saved, dy):
        parts, gate, gT16, w16 = saved
        dy16 = dy.astype(BF16)
        dgated = _matmul(dy16, w16, "nt", name + "_dx")
        *dparts, dgate = _gate_bwd_call(dgated, parts, gate, name + "_gate_bwd")
        return tuple(dparts), dgate, _matmul(gT16, dy16, "nn", name + "_dw")

    @jax.custom_vjp
    def gate_out(parts, gate, w):
        return run_fwd(parts, gate, w)[0]

    gate_out.defvjp(run_fwd, run_bwd)
    return gate_out


def _loss_call(y, t, name):
    rows, dm = y.shape
    tr = _row_tile(rows)

    def body(y_ref, t_ref, l_ref, d_ref):
        i = pl.program_id(0)
        e = y_ref[...] - t_ref[...]
        d_ref[...] = e * (1.0 / dm)
        part = 0.5 * jnp.sum(jnp.mean(e * e, axis=-1, keepdims=True), axis=0, keepdims=True)

        @pl.when(i == 0)
        def _():
            l_ref[...] = part

        @pl.when(i > 0)
        def _():
            l_ref[...] += part

    row = pl.BlockSpec((tr, dm), lambda i: (i, 0))
    one = pl.BlockSpec((1, 1), lambda i: (0, 0))
    return pl.pallas_call(
        body, name=name, grid=(rows // tr,), in_specs=[row, row], out_specs=[one, row],
        out_shape=[jax.ShapeDtypeStruct((1, 1), F32), jax.ShapeDtypeStruct((rows, dm), F32)],
        compiler_params=_cp(("arbitrary",)),
    )(y, t)


@jax.custom_vjp
def _loss_op(y, t):
    return _loss_call(y, t, "loss_head")[0][0, 0]


def _loss_fwd(y, t):
    l, d = _loss_call(y, t, "loss_head")
    return l[0, 0], d


def _loss_bwd(d, ct):
    return ct * d, jnp.zeros_like(d)


_loss_op.defvjp(_loss_fwd, _loss_bwd)


def _attn_blocks(S, Sk, cap=None):
    bq, bk = min(cap or ATTN_BQ, S), min(cap or ATTN_BK, Sk)
    assert S % bq == 0 and Sk % bk == 0
    return bq, bk


def _valid_t(i, j, bq, bk, strict):
    key = j * bk + lax.broadcasted_iota(jnp.int32, (bk, bq), 0)
    qry = i * bq + lax.broadcasted_iota(jnp.int32, (bk, bq), 1)
    return (key < qry) if strict else (key <= qry)


def _sm_fwd_t(qn, k, vT, cmul, causal, name):
    H, S, DK = qn.shape
    Sk, dv = k.shape[1], vT.shape[1]
    bq, bk = _attn_blocks(S, Sk)
    nq, nkb = S // bq, Sk // bk
    hb = PAIR * FWD_PAIRS if H % (PAIR * FWD_PAIRS) == 0 else 1
    heads = range(hb)
    if causal:
        assert S == Sk and bq == bk

    def body(qn_ref, k_ref, vT_ref, oT_ref, lse_ref):
        i = pl.program_id(1)
        qTs = [qn_ref[w].T for w in heads]

        def blk(j, carry, masked):
            off = pl.multiple_of(j * bk, bk)
            sT = [_dot(k_ref[w, pl.ds(off, bk), :], qTs[w]) * cmul for w in heads]
            if masked:
                valid = _valid_t(i, j, bq, bk, False)
                sT = [jnp.where(valid, s, NEG_BIG) for s in sT]
            m_new = [jnp.maximum(carry[w][0], jnp.max(sT[w], axis=0, keepdims=True)) for w in heads]
            p = [jnp.exp2(sT[w] - m_new[w]) for w in heads]
            a = [jnp.exp2(carry[w][0] - m_new[w]) for w in heads]
            l = [a[w] * carry[w][1] + jnp.sum(p[w], axis=0, keepdims=True) for w in heads]
            acc = [a[w] * carry[w][2] + _dot(vT_ref[w, :, pl.ds(off, bk)], p[w].astype(BF16)) for w in heads]
            return tuple((m_new[w], l[w], acc[w]) for w in heads)

        carry = tuple((jnp.full((1, bq), NEG_BIG, F32), jnp.zeros((1, bq), F32), jnp.zeros((dv, bq), F32))
                      for _ in heads)
        if causal:
            carry = lax.fori_loop(0, i, lambda j, c: blk(j, c, False), carry)
            carry = blk(i, carry, True)
        else:
            carry = lax.fori_loop(0, nkb, lambda j, c: blk(j, c, False), carry)
        for w in heads:
            oT_ref[w] = carry[w][2] / carry[w][1]
            lse_ref[w] = carry[w][0] + jnp.log2(carry[w][1])

    qcol = lambda d: pl.BlockSpec((hb, d, bq), lambda h, i: (h, 0, i))
    return pl.pallas_call(
        body, name=name, grid=(H // hb, nq),
        in_specs=[pl.BlockSpec((hb, bq, DK), lambda h, i: (h, i, 0)), pl.BlockSpec((hb, Sk, DK), lambda h, i: (h, 0, 0)),
                  pl.BlockSpec((hb, dv, Sk), lambda h, i: (h, 0, 0))],
        out_specs=[qcol(dv), qcol(1)],
        out_shape=[jax.ShapeDtypeStruct((H, dv, S), F32), jax.ShapeDtypeStruct((H, 1, S), F32)],
        compiler_params=_cp(("parallel", "arbitrary"), ATTN_VMEM_LIMIT),
    )(qn, k, vT)


def _sm_bwd_t(qn, k, v, oT, lse, doT, do, cmul, gscale, causal, name):
    H, S, DK = qn.shape
    Sk, dv = k.shape[1], v.shape[2]
    bq, bk = _attn_blocks(S, Sk)
    nq, nkb = S // bq, Sk // bk

    def body(qn_ref, k_ref, v_ref, oT_ref, lse_ref, doT_ref, do_ref, dq_ref, dk_ref, dv_ref):
        i = pl.program_id(1)

        @pl.when(i == 0)
        def _():
            dk_ref[...] = jnp.zeros_like(dk_ref)
            dv_ref[...] = jnp.zeros_like(dv_ref)

        qnb = qn_ref[...]
        qTb = qnb.T
        dob = do_ref[...]
        doTf = doT_ref[...]
        doTb = doTf.astype(BF16)
        delta = jnp.sum(doTf * oT_ref[...], axis=0, keepdims=True)
        lse = lse_ref[...]

        def blk(j, dq, masked):
            off = pl.multiple_of(j * bk, bk)
            kb = k_ref[pl.ds(off, bk), :]
            sT = _dot(kb, qTb) * cmul
            if masked:
                sT = jnp.where(_valid_t(i, j, bq, bk, False), sT, NEG_BIG)
            p = jnp.exp2(sT - lse)
            dp = _dot(v_ref[pl.ds(off, bk), :], doTb)
            ds = p * (dp - delta)
            dsb = (ds * gscale).astype(BF16) if gscale != 1.0 else ds.astype(BF16)
            dv_ref[pl.ds(off, bk), :] += _dot(p.astype(BF16), dob)
            dk_ref[pl.ds(off, bk), :] += _dot(dsb, qnb)
            return dq + _dot(kb.T, dsb)

        dq = jnp.zeros((DK, bq), F32)
        if causal:
            dq = lax.fori_loop(0, i, lambda j, c: blk(j, c, False), dq)
            dq = blk(i, dq, True)
        else:
            dq = lax.fori_loop(0, nkb, lambda j, c: blk(j, c, False), dq)
        dq_ref[...] = dq.T

    qcol = lambda d: pl.BlockSpec((None, d, bq), lambda h, i: (h, 0, i))
    qrow = lambda d: pl.BlockSpec((None, bq, d), lambda h, i: (h, i, 0))
    krow = lambda d: pl.BlockSpec((None, Sk, d), lambda h, i: (h, 0, 0))
    return pl.pallas_call(
        body, name=name, grid=(H, nq),
        in_specs=[qrow(DK), krow(DK), krow(dv), qcol(dv), qcol(1), qcol(dv), qrow(dv)],
        out_specs=[qrow(DK), krow(DK), krow(dv)],
        out_shape=[jax.ShapeDtypeStruct((H, S, DK), F32), jax.ShapeDtypeStruct((H, Sk, DK), F32),
                   jax.ShapeDtypeStruct((H, Sk, dv), F32)],
        compiler_params=_cp(("parallel", "arbitrary"), ATTN_VMEM_LIMIT),
    )(qn, k, v, oT, lse, doT, do)


def _tri(n, fn):
    r = lax.broadcasted_iota(jnp.int32, (n, n), 0)
    c = lax.broadcasted_iota(jnp.int32, (n, n), 1)
    return jnp.where(fn(r, c), 1.0, 0.0).astype(BF16)


def _key_cumsum(x, tri2, suffix, base):
    bk = x.shape[0]
    c = min(CUMSUM_CHUNK, bk)
    n = bk // c
    hi32 = lax.bitcast_convert_type(lax.bitcast_convert_type(x, jnp.int32) & jnp.int32(-65536), F32)
    hi = hi32.astype(BF16)
    lo = (x - hi32).astype(BF16)
    tot = [jnp.sum(x[a * c:(a + 1) * c], axis=0, keepdims=True) for a in range(n)]
    outs = []
    for a in range(n):
        row = base
        for t in (tot[a + 1:] if suffix else tot[:a]):
            row = row + t
        stacked = jnp.concatenate([hi[a * c:(a + 1) * c], lo[a * c:(a + 1) * c]], axis=0)
        outs.append(_dot(tri2, stacked) + row)
    total = tot[0]
    for t in tot[1:]:
        total = total + t
    return (outs[0] if n == 1 else jnp.concatenate(outs, axis=0)), total


def _tri2(n, fn):
    t = _tri(n, fn)
    return jnp.concatenate([t, t], axis=1)


def _sb_logs(z):
    neg_abs = lax.bitcast_convert_type(lax.bitcast_convert_type(z, jnp.int32) | jnp.int32(-2 ** 31), F32)
    ls = jnp.minimum(z, 0.0) - jnp.log(1.0 + jnp.exp(neg_abs))
    return ls, ls - z


PAIR = LANE // HEAD_DIM
FWD_PAIRS = 2
SB_DEAD = -110.0
SB_BLOCK = 256
FOX_DEAD = -160.0
FOX_BLOCK = 512


def _head_lanes(shape, w, axis):
    idx = lax.broadcasted_iota(jnp.int32, shape, axis)
    return (idx >= HEAD_DIM * w) & (idx < HEAD_DIM * (w + 1))


def _bias_rows(w, bq):
    row = lax.broadcasted_iota(jnp.int32, (LANE, bq), 0)
    return jnp.where((row >= 3 * w) & (row < 3 * w + 3), -1.0, 0.0).astype(BF16)


def _merge_pair(parts):
    return jnp.where(_head_lanes(parts[0].shape, 0, 0), parts[0], parts[1]).T


def _smp_fwd(q2, k2, v2, bias, r, causal, name, kstat=None):
    S, C = q2.shape
    Sk = k2.shape[0]
    bq, bk = _attn_blocks(S, Sk, FOX_BLOCK if bias is not None else None)
    nq, nkb, P = S // bq, Sk // bk, C // LANE
    gp = FWD_PAIRS if P % FWD_PAIRS == 0 else 1
    use_f = bias is not None
    if causal:
        assert S == Sk and bq == bk and use_f

    def body(*refs):
        if use_f:
            ks_ref, q_ref, k_ref, v_ref, b_ref, r_ref, o_ref, lse_ref, js_ref = refs
        else:
            q_ref, k_ref, v_ref, o_ref, lse_ref = refs
        i = pl.program_id(1)
        heads = range(PAIR * gp)
        lanes = [slice(LANE * (h // PAIR), LANE * (h // PAIR + 1)) for h in heads]
        qps = [q_ref[:, lanes[h]] for h in heads]
        qTs = [jnp.where(_head_lanes(qps[h].shape, h % PAIR, 1), qps[h], jnp.zeros_like(qps[h])).T for h in heads]
        if use_f:
            qf = [t.astype(F32) for t in qTs]
            qnorm = [jnp.sqrt(jnp.sum(t * t, axis=0, keepdims=True)) for t in qf]
            qTs = [jnp.concatenate([qTs[h], _bias_rows(h % PAIR, bq)], axis=0) for h in heads]

        def blk(j, carry, masked):
            off = pl.multiple_of(j * bk, bk)
            kbs = [k_ref[pl.ds(off, bk), LANE * g:LANE * (g + 1)] for g in range(gp)]
            if use_f:
                kbs = [jnp.concatenate([kbs[g], b_ref[pl.ds(off, bk), LANE * g:LANE * (g + 1)]], axis=1)
                       for g in range(gp)]
            vTbs = [v_ref[pl.ds(off, bk), LANE * g:LANE * (g + 1)].T for g in range(gp)]
            sT = [_dot(kbs[h // PAIR], qTs[h]) * LOG2E for h in heads]
            if masked:
                valid = _valid_t(i, j, bq, bk, False)
                sT = [jnp.where(valid, s, NEG_BIG) for s in sT]
            cm = [jnp.max(s, axis=0, keepdims=True) for s in sT]
            if use_f:
                cm = [cm[h] + r_ref[h] for h in heads]
            m_new = [jnp.maximum(carry[h][0], cm[h]) for h in heads]
            shift = [(m_new[h] - r_ref[h]) if use_f else m_new[h] for h in heads]
            p = [jnp.exp2(sT[h] - shift[h]) for h in heads]
            a = [jnp.exp2(carry[h][0] - m_new[h]) for h in heads]
            l = [a[h] * carry[h][1] + jnp.sum(p[h], axis=0, keepdims=True) for h in heads]
            acc = [a[h] * carry[h][2] + _dot(vTbs[h // PAIR], p[h].astype(BF16)) for h in heads]
            return tuple((m_new[h], l[h], acc[h]) for h in heads)

        def step(jj, state):
            carry, first = state
            j = i - jj
            h0 = pl.program_id(0) * (PAIR * gp)
            bound = [LOG2E * (qnorm[h] * ks_ref[(h0 + h) * nkb + j] - ks_ref[(PAIR * P + h0 + h) * nkb + j])
                     + r_ref[h] - carry[h][0] for h in heads]
            live = jnp.max(functools.reduce(jnp.maximum, bound)) >= FOX_DEAD
            carry = lax.cond(live, lambda cr: blk(j, cr, False), lambda cr: cr, carry)
            return carry, jnp.where(live, j, first)

        carry = tuple((jnp.full((1, bq), NEG_BIG, F32), jnp.zeros((1, bq), F32), jnp.zeros((LANE, bq), F32))
                      for _ in heads)
        if causal:
            carry = blk(i, carry, True)
            carry, first = lax.fori_loop(1, i + 1, step, (carry, i))
            js_ref[0] = jnp.full((1, bq), first, jnp.int32)
        else:
            carry = lax.fori_loop(0, nkb, lambda j, c: blk(j, c, False), carry)
            if use_f:
                js_ref[0] = jnp.zeros((1, bq), jnp.int32)
        for h in heads:
            lse_ref[h] = carry[h][0] + jnp.log2(carry[h][1])
        for g in range(gp):
            o_ref[:, LANE * g:LANE * (g + 1)] = _merge_pair(
                [carry[h][2] / carry[h][1] for h in range(PAIR * g, PAIR * (g + 1))])

    qblk = pl.BlockSpec((bq, LANE * gp), lambda p, i: (i, p))
    kres = pl.BlockSpec((Sk, LANE * gp), lambda p, i: (0, p))
    stat = pl.BlockSpec((PAIR * gp, 1, bq), lambda p, i: (p, 0, i))
    in_specs = [qblk, kres, kres]
    args = [q2, k2, v2]
    out_specs = [qblk, stat]
    out_shape = [jax.ShapeDtypeStruct((S, C), F32), jax.ShapeDtypeStruct((PAIR * P, 1, S), F32)]
    if use_f:
        in_specs = [pl.BlockSpec(memory_space=pltpu.SMEM)] + in_specs + [kres, stat]
        args = [kstat] + args + [bias, r]
        out_specs.append(pl.BlockSpec((1, 1, bq), lambda p, i: (p, 0, i)))
        out_shape.append(jax.ShapeDtypeStruct((P // gp, 1, S), jnp.int32))
    return pl.pallas_call(
        body, name=name, grid=(P // gp, nq), in_specs=in_specs, out_specs=out_specs, out_shape=out_shape,
        compiler_params=_cp(("parallel", "arbitrary"), ATTN_VMEM_LIMIT),
    )(*args)


def _smp_bwd(q2, k2, v2, o2, lse, do2, bias, r, scale, causal, name, first=None):
    S, C = q2.shape
    Sk = k2.shape[0]
    bq, bk = _attn_blocks(S, Sk, FOX_BLOCK if bias is not None else None)
    nq, nkb, P = S // bq, Sk // bk, C // LANE
    use_f = bias is not None

    def body(*refs):
        if use_f:
            (first_ref, q_ref, k_ref, v_ref, o_ref, lse_ref, do_ref, b_ref, r_ref,
             dq_ref, dk_ref, dv_ref, dr_ref, dkey_ref, dk_acc, dv_acc, db_ref) = refs
        else:
            q_ref, k_ref, v_ref, o_ref, lse_ref, do_ref, dq_ref, dk_ref, dv_ref, dk_acc, dv_acc = refs
        i = pl.program_id(1)

        @pl.when(i == 0)
        def _():
            dk_acc[...] = jnp.zeros_like(dk_acc)
            dv_acc[...] = jnp.zeros_like(dv_acc)
            if use_f:
                db_ref[...] = jnp.zeros_like(db_ref)

        qp = q_ref[...]
        dof = do_ref[...]
        prod = dof * o_ref[...]
        heads = range(PAIR)
        mine = [_head_lanes(qp.shape, w, 1) for w in heads]
        qz = [jnp.where(mine[w], qp, jnp.zeros_like(qp)) for w in heads]
        qTs = [qz[w].T for w in heads]
        if use_f:
            qTs = [jnp.concatenate([qTs[w], _bias_rows(w, bq)], axis=0) for w in heads]
        doz = [jnp.where(mine[w], dof, 0.0).astype(BF16) for w in heads]
        doT = [doz[w].T for w in heads]
        delta = [jnp.sum(jnp.where(mine[w], prod, 0.0).T, axis=0, keepdims=True) for w in heads]
        shift = [(lse_ref[w] - r_ref[w]) if use_f else lse_ref[w] for w in heads]

        def blk(j, carry, masked):
            off = pl.multiple_of(j * bk, bk)
            kb = k_ref[pl.ds(off, bk), :]
            kTb = kb.T
            if use_f:
                kb = jnp.concatenate([kb, b_ref[pl.ds(off, bk), :]], axis=1)
            vb = v_ref[pl.ds(off, bk), :]
            sT = [_dot(kb, qTs[w]) * LOG2E for w in heads]
            if masked:
                valid = _valid_t(i, j, bq, bk, False)
                sT = [jnp.where(valid, s, NEG_BIG) for s in sT]
            p = [jnp.exp2(sT[w] - shift[w]) for w in heads]
            dp = [_dot(vb, doT[w]) for w in heads]
            ds = [p[w] * (dp[w] - delta[w]) for w in heads]
            dsb = [d.astype(BF16) for d in ds]
            dvs = [_dot(p[w].astype(BF16), doz[w]) for w in heads]
            dks = [_dot(dsb[w], qz[w]) for w in heads]
            dv_acc[pl.ds(off, bk), :] += dvs[0] + dvs[1]
            dk_acc[pl.ds(off, bk), :] += dks[0] + dks[1]
            dr = [carry[w][1] for w in heads]
            if use_f:
                dr = [dr[w] + jnp.sum(ds[w], axis=0, keepdims=True) for w in heads]
                lane = lax.broadcasted_iota(jnp.int32, (bk, LANE), 1)
                cols = [jnp.where(lane == w, jnp.sum(ds[w], axis=1, keepdims=True), 0.0) for w in heads]
                db_ref[pl.ds(off, bk), :] += cols[0] + cols[1]
            dq = [carry[w][0] + _dot(kTb, dsb[w]) for w in heads]
            return tuple((dq[w], dr[w]) for w in heads)

        carry = tuple((jnp.zeros((LANE, bq), F32), jnp.zeros((1, bq), F32)) for _ in heads)
        if causal:
            start = first_ref[pl.program_id(0) // (P // first.shape[0]), i]
            carry = lax.fori_loop(start, i, lambda j, c: blk(j, c, False), carry)
            carry = blk(i, carry, True)
        else:
            carry = lax.fori_loop(0, nkb, lambda j, c: blk(j, c, False), carry)
        if use_f:
            for w in heads:
                dr_ref[w] = carry[w][1]
        dq_ref[...] = (_merge_pair([carry[w][0] for w in heads]) * scale).astype(BF16)

        @pl.when(i == nq - 1)
        def _():
            dk_ref[...] = dk_acc[...].astype(BF16)
            dv_ref[...] = dv_acc[...].astype(BF16)

        if use_f:
            @pl.when(i == nq - 1)
            def _():
                def chunk(cidx, carry):
                    off = pl.multiple_of(cidx * LANE, LANE)
                    t = db_ref[pl.ds(off, LANE), :].T
                    for w in range(PAIR):
                        dkey_ref[w, :, pl.ds(off, LANE)] = t[w:w + 1, :]
                    return carry

                lax.fori_loop(0, Sk // LANE, chunk, 0)

    qblk = pl.BlockSpec((bq, LANE), lambda p, i: (i, p))
    kres = pl.BlockSpec((Sk, LANE), lambda p, i: (0, p))
    stat = pl.BlockSpec((PAIR, 1, bq), lambda p, i: (p, 0, i))
    in_specs = [qblk, kres, kres, qblk, stat, qblk]
    args = [q2, k2, v2, o2, lse, do2]
    out_specs = [qblk, kres, kres]
    out_shape = [jax.ShapeDtypeStruct((S, C), BF16), jax.ShapeDtypeStruct((Sk, C), BF16),
                 jax.ShapeDtypeStruct((Sk, C), BF16)]
    scratch = [pltpu.VMEM((Sk, LANE), F32), pltpu.VMEM((Sk, LANE), F32)]
    if use_f:
        in_specs = [pl.BlockSpec(memory_space=pltpu.SMEM)] + in_specs + [kres, stat]
        args = [first] + args + [bias, r]
        out_specs += [stat, pl.BlockSpec((PAIR, 1, Sk), lambda p, i: (p, 0, 0))]
        out_shape += [jax.ShapeDtypeStruct((PAIR * P, 1, S), F32), jax.ShapeDtypeStruct((PAIR * P, 1, Sk), F32)]
        scratch.append(pltpu.VMEM((Sk, LANE), F32))
    return pl.pallas_call(
        body, name=name, grid=(P, nq), in_specs=in_specs, out_specs=out_specs, out_shape=out_shape,
        scratch_shapes=scratch, compiler_params=_cp(("parallel", "arbitrary"), ATTN_VMEM_LIMIT),
    )(*args)


def _sbp_fwd(q2, k2, v2, name):
    S, C = q2.shape
    bq, bk = _attn_blocks(S, S, SB_BLOCK)
    assert bq == bk
    nq, P = S // bq, C // LANE
    gp = FWD_PAIRS if P % FWD_PAIRS == 0 else 1
    c = min(CUMSUM_CHUNK, bk)

    def body(q_ref, k_ref, v_ref, o_ref, lt_ref, js_ref):
        i = pl.program_id(1)
        after = _tri2(c, lambda s, j: j > s)
        heads = range(PAIR * gp)
        qps = [q_ref[:, LANE * (h // PAIR):LANE * (h // PAIR + 1)] for h in heads]
        qTs = [jnp.where(_head_lanes(qps[h].shape, h % PAIR, 1), qps[h], jnp.zeros_like(qps[h])).T for h in heads]

        def blk(jj, carry, masked):
            j = i - jj
            off = pl.multiple_of(j * bk, bk)
            kbs = [k_ref[pl.ds(off, bk), LANE * g:LANE * (g + 1)] for g in range(gp)]
            vTbs = [v_ref[pl.ds(off, bk), LANE * g:LANE * (g + 1)].T for g in range(gp)]
            logs = [_sb_logs(_dot(kbs[h // PAIR], qTs[h])) for h in heads]
            ls, lk = [t[0] for t in logs], [t[1] for t in logs]
            if masked:
                valid = _valid_t(i, j, bq, bk, True)
                lk = [jnp.where(valid, t, 0.0) for t in lk]
            cs = [_key_cumsum(lk[h], after, True, carry[h][0]) for h in heads]
            wgt = [jnp.exp(ls[h] + cs[h][0]) for h in heads]
            if masked:
                wgt = [jnp.where(valid, t, 0.0) for t in wgt]
            acc = [carry[h][1] + _dot(vTbs[h // PAIR], wgt[h].astype(BF16)) for h in heads]
            return tuple((carry[h][0] + cs[h][1], acc[h]) for h in heads)

        def step(jj, state):
            carry, first = state
            live = jnp.max(functools.reduce(jnp.maximum, [carry[h][0] for h in heads])) >= SB_DEAD
            carry = lax.cond(live, lambda cr: blk(jj, cr, False), lambda cr: cr, carry)
            return carry, jnp.where(live, i - jj, first)

        carry = tuple((jnp.zeros((1, bq), F32), jnp.zeros((LANE, bq), F32)) for _ in heads)
        carry = blk(0, carry, True)
        carry, first = lax.fori_loop(1, i + 1, step, (carry, i))
        js_ref[0] = jnp.full((1, bq), first, jnp.int32)
        for h in heads:
            lt_ref[h] = carry[h][0]
        for g in range(gp):
            o_ref[:, LANE * g:LANE * (g + 1)] = _merge_pair([carry[h][1] for h in range(PAIR * g, PAIR * (g + 1))])

    qblk = pl.BlockSpec((bq, LANE * gp), lambda p, i: (i, p))
    kres = pl.BlockSpec((S, LANE * gp), lambda p, i: (0, p))
    stat = pl.BlockSpec((PAIR * gp, 1, bq), lambda p, i: (p, 0, i))
    return pl.pallas_call(
        body, name=name, grid=(P // gp, nq),
        in_specs=[qblk, kres, kres],
        out_specs=[qblk, stat, pl.BlockSpec((1, 1, bq), lambda p, i: (p, 0, i))],
        out_shape=[jax.ShapeDtypeStruct((S, C), F32), jax.ShapeDtypeStruct((PAIR * P, 1, S), F32),
                   jax.ShapeDtypeStruct((P // gp, 1, S), jnp.int32)],
        compiler_params=_cp(("parallel", "arbitrary"), ATTN_VMEM_LIMIT),
    )(q2, k2, v2)


def _sbp_bwd(q2, k2, v2, lt, first, do2, scale, name):
    S, C = q2.shape
    bq, bk = _attn_blocks(S, S, SB_BLOCK)
    nq, P = S // bq, C // LANE
    c = min(CUMSUM_CHUNK, bk)

    per_group = P // first.shape[0]

    def body(first_ref, q_ref, k_ref, v_ref, lt_ref, do_ref, dq_ref, dk_ref, dv_ref, dk_acc, dv_acc):
        i = pl.program_id(1)

        @pl.when(i == 0)
        def _():
            dk_acc[...] = jnp.zeros_like(dk_acc)
            dv_acc[...] = jnp.zeros_like(dv_acc)

        qp = q_ref[...]
        dof = do_ref[...]
        upto = _tri2(c, lambda s, j: j <= s)
        before = _tri2(c, lambda s, j: j < s)
        heads = range(PAIR)
        mine = [_head_lanes(qp.shape, w, 1) for w in heads]
        qz = [jnp.where(mine[w], qp, jnp.zeros_like(qp)) for w in heads]
        qTs = [qz[w].T for w in heads]
        doz = [jnp.where(mine[w], dof, 0.0).astype(BF16) for w in heads]
        doT = [doz[w].T for w in heads]
        ltot = [lt_ref[w] for w in heads]

        def blk(j, carry, masked):
            off = pl.multiple_of(j * bk, bk)
            kb = k_ref[pl.ds(off, bk), :]
            vb = v_ref[pl.ds(off, bk), :]
            kTb = kb.T
            logs = [_sb_logs(_dot(kb, qTs[w])) for w in heads]
            ls, lk = [t[0] for t in logs], [t[1] for t in logs]
            if masked:
                valid = _valid_t(i, j, bq, bk, True)
                lk = [jnp.where(valid, t, 0.0) for t in lk]
            pin = [_key_cumsum(lk[w], upto, False, carry[w][1] - ltot[w]) for w in heads]
            wgt = [jnp.exp(ls[w] - pin[w][0]) for w in heads]
            if masked:
                wgt = [jnp.where(valid, t, 0.0) for t in wgt]
            g = [_dot(vb, doT[w]) * wgt[w] for w in heads]
            cin = [_key_cumsum(g[w], before, False, carry[w][2]) for w in heads]
            sig = [jnp.exp(t) for t in ls]
            dz = [g[w] * (1.0 - sig[w]) - cin[w][0] * sig[w] for w in heads]
            if masked:
                dz = [jnp.where(valid, t, 0.0) for t in dz]
            dzb = [t.astype(BF16) for t in dz]
            dvs = [_dot(wgt[w].astype(BF16), doz[w]) for w in heads]
            dks = [_dot(dzb[w], qz[w]) for w in heads]
            dv_acc[pl.ds(off, bk), :] += dvs[0] + dvs[1]
            dk_acc[pl.ds(off, bk), :] += dks[0] + dks[1]
            return tuple((carry[w][0] + _dot(kTb, dzb[w]), carry[w][1] + pin[w][1], carry[w][2] + cin[w][1])
                         for w in heads)

        carry = tuple((jnp.zeros((LANE, bq), F32), jnp.zeros((1, bq), F32), jnp.zeros((1, bq), F32)) for _ in heads)
        start = first_ref[pl.program_id(0) // per_group, i]
        carry = lax.fori_loop(start, i, lambda j, cr: blk(j, cr, False), carry)
        carry = blk(i, carry, True)
        dq_ref[...] = (_merge_pair([carry[w][0] for w in heads]) * scale).astype(BF16)

        @pl.when(i == nq - 1)
        def _():
            dk_ref[...] = dk_acc[...].astype(BF16)
            dv_ref[...] = dv_acc[...].astype(BF16)

    qblk = pl.BlockSpec((bq, LANE), lambda p, i: (i, p))
    kres = pl.BlockSpec((S, LANE), lambda p, i: (0, p))
    stat = pl.BlockSpec((PAIR, 1, bq), lambda p, i: (p, 0, i))
    return pl.pallas_call(
        body, name=name, grid=(P, nq),
        in_specs=[pl.BlockSpec(memory_space=pltpu.SMEM), qblk, kres, kres, stat, qblk],
        out_specs=[qblk, kres, kres],
        out_shape=[jax.ShapeDtypeStruct((S, C), BF16)] * 3,
        scratch_shapes=[pltpu.VMEM((S, LANE), F32), pltpu.VMEM((S, LANE), F32)],
        compiler_params=_cp(("parallel", "arbitrary"), ATTN_VMEM_LIMIT),
    )(first, q2, k2, v2, lt, do2)


def _bias_cols(f_cum):
    H, Sk = f_cum.shape
    terms = jnp.stack(_split3(f_cum), axis=-1)
    packed = terms.reshape(H // PAIR, PAIR, Sk, 3).transpose(2, 0, 1, 3).reshape(Sk, H // PAIR, PAIR * 3)
    return jnp.pad(packed, ((0, 0), (0, 0), (0, LANE - PAIR * 3))).reshape(Sk, -1)


def _make_packed_softmax(name, scale, causal, use_f):
    assert _pow2(scale)

    def run_fwd(q16, k16, v16, f_cum):
        q16 = q16 * scale
        if not use_f:
            o, lse = _smp_fwd(q16, k16, v16, None, None, causal, name + "_fwd")
            return o, (q16, k16, v16, o, lse, None, None, None)
        bq, bk = _attn_blocks(q16.shape[0], k16.shape[0], FOX_BLOCK)
        n_heads = f_cum.shape[0]
        knorm = jnp.sqrt(jnp.sum(jnp.square(k16.astype(F32)).reshape(-1, bk, n_heads, HEAD_DIM), axis=3))
        kstat = jnp.concatenate([jnp.max(knorm, axis=1).T.reshape(-1), f_cum[:, bk - 1::bk].reshape(-1)])
        bias, r = _bias_cols(f_cum), (f_cum * LOG2E)[:, None, :]
        o, lse, first = _smp_fwd(q16, k16, v16, bias, r, causal, name + "_fwd", lax.stop_gradient(kstat))
        return o, (q16, k16, v16, o, lse, bias, r, first[:, 0, ::bq])

    def run_bwd(saved, do):
        q16, k16, v16, o, lse, bias, r, first = saved
        outs = _smp_bwd(q16, k16, v16, o, lse, do, bias, r, scale, causal, name + "_bwd", first)
        if use_f:
            return outs[0], outs[1], outs[2], outs[3][:, 0, :] - outs[4][:, 0, :]
        return tuple(outs)

    if use_f:
        @jax.custom_vjp
        def attn(q, k, v, f_cum):
            return run_fwd(q, k, v, f_cum)[0]

        attn.defvjp(run_fwd, run_bwd)
    else:
        @jax.custom_vjp
        def attn(q, k, v):
            return run_fwd(q, k, v, None)[0]

        attn.defvjp(lambda q, k, v: run_fwd(q, k, v, None), run_bwd)
    return attn


def _make_packed_sb(name, scale):
    assert _pow2(scale)

    def run_fwd(q16, k16, v16):
        q16 = q16 * scale
        o, lt, first = _sbp_fwd(q16, k16, v16, name + "_fwd")
        bq, _ = _attn_blocks(q16.shape[0], q16.shape[0], SB_BLOCK)
        return o, (q16, k16, v16, lt, first[:, 0, ::bq])

    def run_bwd(saved, do):
        q16, k16, v16, lt, first = saved
        return tuple(_sbp_bwd(q16, k16, v16, lt, first, do, scale, name + "_bwd"))

    @jax.custom_vjp
    def attn(q, k, v):
        return run_fwd(q, k, v)[0]

    attn.defvjp(run_fwd, run_bwd)
    return attn


def _round_bf16(x):
    return lax.reduce_precision(x, exponent_bits=8, mantissa_bits=7)


def _split3(x):
    hi = _round_bf16(x)
    mid = _round_bf16(x - hi)
    lo = _round_bf16(x - hi - mid)
    return hi.astype(BF16), mid.astype(BF16), lo.astype(BF16)


def _pow2(x):
    m, _ = math.frexp(x)
    return m == 0.5


def _pad_last(x, n):
    return jnp.pad(x, [(0, 0)] * (x.ndim - 1) + [(0, n - x.shape[-1])])


def _layouts(q, k, scale):
    qh = _pad_last(jnp.transpose(q * scale if _pow2(scale) else q, (1, 0, 2)).astype(BF16), LANE)
    return qh, _pad_last(jnp.transpose(k, (1, 0, 2)).astype(BF16), LANE)


def _make_softmax_attn(name, scale, causal, d):
    pre = _pow2(scale)
    cmul = LOG2E if pre else scale * LOG2E
    gscale = 1.0 if pre else scale

    def run_fwd(q, k, v):
        qn, kn = _layouts(q, k, scale)
        vn = jnp.transpose(v, (1, 0, 2)).astype(BF16)
        oT, lse = _sm_fwd_t(qn, kn, jnp.transpose(vn, (0, 2, 1)), cmul, causal, name + "_fwd")
        return jnp.transpose(oT, (2, 0, 1)), (qn, kn, vn, oT, lse)

    def run_bwd(saved, dout):
        qn, kn, vn, oT, lse = saved
        doT = jnp.transpose(dout, (1, 2, 0))
        do = jnp.transpose(dout, (1, 0, 2)).astype(BF16)
        dq, dk, dv = _sm_bwd_t(qn, kn, vn, oT, lse, doT, do, cmul, gscale, causal, name + "_bwd")
        dq = jnp.transpose(dq[:, :, :d], (1, 0, 2))
        if pre:
            dq = dq * scale
        return dq, jnp.transpose(dk[:, :, :d], (1, 0, 2)), jnp.transpose(dv, (1, 0, 2))

    @jax.custom_vjp
    def attn(q, k, v):
        return run_fwd(q, k, v)[0]

    attn.defvjp(run_fwd, run_bwd)
    return attn


def _rope(x, positions):
    half = x.shape[-1] // 2
    inv_freq = ROPE_THETA ** (-jnp.arange(half, dtype=F32) / half)
    ang = positions.astype(F32)[:, None] * inv_freq[None, :]
    ang = ang.reshape((ang.shape[0],) + (1,) * (x.ndim - 2) + (half,))
    cos, sin = jnp.cos(ang), jnp.sin(ang)
    x1, x2 = x[..., :half], x[..., half:]
    return jnp.concatenate([x1 * cos - x2 * sin, x1 * sin + x2 * cos], axis=-1)


def _permute_cols(w):
    parts = [w[..., _ORIG_OFF[idx]:_ORIG_OFF[idx] + SPLIT_SIZES[idx]] for _, idx in _PERM]
    pad = jnp.zeros(w.shape[:-1] + (PROJ_COLS - IN_COLS,), w.dtype)
    return jnp.concatenate(parts + [pad], axis=-1)


def _unpermute_cols(w):
    start, parts = 0, [None] * len(SPLIT_SIZES)
    for _, idx in _PERM:
        parts[idx] = w[..., start:start + SPLIT_SIZES[idx]]
        start += SPLIT_SIZES[idx]
    return jnp.concatenate(parts, axis=-1)


_BF16_PIECES = ("fq", "fk", "fv", "sq", "sk", "sv", "mq")


def _make_ln_proj(name, has_res):
    wide = [(n, SPLIT_SIZES[idx]) for n, idx in _PERM if SPLIT_SIZES[idx] % LANE == 0]
    narrow = [(n, SPLIT_SIZES[idx]) for n, idx in _PERM if SPLIT_SIZES[idx] % LANE]
    assert [n for n, _ in wide + narrow] == [n for n, _ in _PERM]
    tail = PROJ_COLS - sum(w for _, w in wide)

    def run_fwd(x, res, g, b, w):
        h, h16, hT16 = _ln_fwd_call(x, res, g, b, name + "_ln_fwd", also16=True)
        w16 = w.astype(BF16)
        outs = _matmul_split(h16, w16, [w for _, w in wide] + [tail],
                             [BF16 if n in _BF16_PIECES else F32 for n, _ in wide] + [F32], name + "_fwd")
        pieces, off = list(outs[:-1]), 0
        for _, width in narrow:
            pieces.append(outs[-1][:, off:off + width])
            off += width
        return (h, tuple(pieces)), (x, res, g, hT16, w16)

    def run_bwd(saved, cts):
        x, res, g, hT16, w16 = saved
        dh, dpieces = cts
        wide = [c for c in dpieces if c.shape[1] % LANE == 0]
        narrow = [c.astype(BF16) for c in dpieces if c.shape[1] % LANE]
        pad = jnp.zeros((x.shape[0], PROJ_COLS - IN_COLS), BF16)
        da, dy16 = _concat_matmul_nt(wide + [jnp.concatenate(narrow + [pad], axis=1)], w16, name + "_dx")
        dw = _matmul(hT16, dy16, "nn", name + "_dw")
        outs = _ln_bwd_call(dh, x, res, g, name + "_ln_bwd", dy2=da)
        if has_res:
            dx, dr, dg, db = outs
            return dx, dr, dg.reshape(-1), db.reshape(-1), dw
        dx, dg, db = outs
        return dx, dg.reshape(-1), db.reshape(-1), dw

    if has_res:
        @jax.custom_vjp
        def op(x, res, g, b, w):
            return run_fwd(x, res, g, b, w)[0]

        op.defvjp(run_fwd, run_bwd)
    else:
        @jax.custom_vjp
        def op(x, g, b, w):
            return run_fwd(x, None, g, b, w)[0]

        op.defvjp(lambda x, g, b, w: run_fwd(x, None, g, b, w), run_bwd)

    def call(*args):
        h, pieces = op(*args)
        return h, {n: part for (n, _), part in zip(_PERM, pieces)}

    return call


def _trunk_loss(wts, x2d, mem2d, target2d):
    s = x2d.shape[0]
    positions = jnp.arange(s)
    head_scale = HEAD_DIM ** -0.5
    mla_scale = (MLA_NOPE + MLA_ROPE) ** -0.5

    mem_n = _make_ln("ln_mem", False)(mem2d, wts["mem_ln_g"], wts["mem_ln_b"])
    h, y = None, x2d
    for l in range(DEPTH):
        tag = f"l{l}_"
        w_p = wts["w_in"][l]
        if l == 0:
            h, p = _make_ln_proj(tag + "proj", False)(y, wts["ln_in_g"], wts["ln_in_b"], w_p)
        else:
            h, p = _make_ln_proj(tag + "proj", True)(y, h, wts["ln_g"][l - 1], wts["ln_b"][l - 1], w_p)

        log_f = jax.nn.log_sigmoid(p["f_logit"].T + wts["b_forget"][l][:, None])
        f_cum = jnp.cumsum(log_f, axis=1)
        out_fox = _make_packed_softmax(tag + "fox", head_scale, True, True)(p["fq"], p["fk"], p["fv"], f_cum)

        out_sb = _make_packed_sb(tag + "sb", head_scale)(p["sq"], p["sk"], p["sv"])

        cqn = _make_rms(tag + "rms_q")(p["c_q"], wts["mla_q_norm_g"][l])
        q_mla = _make_mm(tag + "q_up")(cqn, wts["w_mla_q_up"][l]).reshape(s, N_HEADS, MLA_NOPE + MLA_ROPE)
        ckvn = _make_rms(tag + "rms_kv")(p["c_kv"], wts["mla_kv_norm_g"][l])
        kv_mla = _make_mm(tag + "kv_up")(ckvn, wts["w_mla_kv_up"][l]).reshape(s, N_HEADS, MLA_NOPE + MLA_V)
        q_full = jnp.concatenate([q_mla[..., :MLA_NOPE], _rope(q_mla[..., MLA_NOPE:], positions)], axis=-1)
        k_rope = jnp.broadcast_to(_rope(p["k_rot"], positions)[:, None, :], (s, N_HEADS, MLA_ROPE))
        k_full = jnp.concatenate([kv_mla[..., :MLA_NOPE], k_rope], axis=-1)
        out_mla = _make_softmax_attn(tag + "mla", mla_scale, True, MLA_NOPE + MLA_ROPE)(
            q_full, k_full, kv_mla[..., MLA_NOPE:]).reshape(s, GROUP_W)

        mkv = _make_mm(tag + "mem_kv")(mem_n, wts["w_mem_kv"][l])
        out_mem = _make_packed_softmax(tag + "mem", head_scale, False, False)(
            p["mq"], mkv[:, :GROUP_W].astype(BF16), mkv[:, GROUP_W:].astype(BF16))

        y = _make_gate_out(tag + "out")((out_fox, out_sb, out_mla, out_mem), p["gate"], wts["w_out"][l])

    h = _make_ln(f"l{DEPTH - 1}_ln", True)(y, h, wts["ln_g"][DEPTH - 1], wts["ln_b"][DEPTH - 1])
    return _loss_op(h, target2d)


def _mesh_pos():
    x, y, c = (lax.axis_index(a) for a in MESH_AXES)
    return x, y, c, 4 * x + 2 * y + c


def _peer(x, y, c, mask):
    return (x ^ ((mask >> 2) & 1), y ^ ((mask >> 1) & 1), c ^ (mask & 1))


_ANY = pl.BlockSpec(memory_space=pl.ANY)


def _all_gather(row_shards, stack_shards):
    n_row, n_all = len(row_shards), len(row_shards) + len(stack_shards)
    shards = list(row_shards) + list(stack_shards)
    chip_masks = (4, 2, 6)
    tensors = range(n_all)

    def body(*refs):
        ins, outs = refs[:n_all], refs[n_all:2 * n_all]
        send_sems, recv_sems, local_sems = refs[2 * n_all:]
        x, y, c, me = _mesh_pos()
        sibling = _peer(x, y, c, 1)

        def window(t, slot):
            if t < n_row:
                rows = shards[t].shape[1]
                return outs[t].at[:, pl.ds(slot * rows, rows), :]
            return outs[t].at[slot]

        def copy(t, k, slot, to, src=None):
            return pltpu.make_async_remote_copy(
                src_ref=window(t, slot) if src is None else src, dst_ref=window(t, slot),
                send_sem=send_sems.at[t, k], recv_sem=recv_sems.at[t, k], device_id=to,
                device_id_type=pl.DeviceIdType.MESH)

        local = [pltpu.make_async_copy(ins[t], window(t, me), local_sems.at[t]) for t in tensors]
        for cp in local:
            cp.start()
        first = [copy(t, 0, me, sibling, src=ins[t]) for t in tensors]
        first += [copy(t, 1 + j, me, _peer(x, y, c, m), src=ins[t]) for j, m in enumerate(chip_masks) for t in tensors]
        for cp in first:
            cp.start()
        passed = []
        for j, m in enumerate(chip_masks):
            for t in tensors:
                copy(t, 1 + j, me ^ m, sibling).wait_recv()
            for t in tensors:
                cp = copy(t, 4 + j, me ^ m, sibling)
                cp.start()
                passed.append(cp)
        for t in tensors:
            copy(t, 0, me ^ 1, sibling).wait_recv()
        for j, m in enumerate(chip_masks):
            for t in tensors:
                copy(t, 4 + j, me ^ m ^ 1, sibling).wait_recv()
        for cp in first + passed:
            cp.wait_send()
        for cp in local:
            cp.wait()

    out_shape = [jax.ShapeDtypeStruct((a.shape[0], N_DEV * a.shape[1], a.shape[2]), a.dtype) for a in row_shards]
    out_shape += [jax.ShapeDtypeStruct((N_DEV,) + a.shape, a.dtype) for a in stack_shards]
    return pl.pallas_call(
        body, name="all_gather_weights", in_specs=[_ANY] * n_all, out_specs=[_ANY] * n_all, out_shape=out_shape,
        scratch_shapes=[pltpu.SemaphoreType.DMA((n_all, N_DEV - 1)), pltpu.SemaphoreType.DMA((n_all, N_DEV - 1)),
                        pltpu.SemaphoreType.DMA((n_all,))],
    )(*shards)


def _reduce_scatter(row_full, stack_full, bcast):
    n_row, n_stack = len(row_full), len(stack_full)
    n_all = n_row + n_stack + len(bcast)
    fulls = list(row_full) + list(stack_full) + list(bcast)

    def body(*refs):
        ins, outs = refs[:n_all], refs[n_all:2 * n_all]
        send_sems, recv_sems, local_sems = refs[2 * n_all:]
        x, y, c, me = _mesh_pos()

        def part(t, slot):
            if t < n_row:
                rows = fulls[t].shape[1] // N_DEV
                return ins[t].at[:, pl.ds(slot * rows, rows), :]
            if t < n_row + n_stack:
                return ins[t].at[slot]
            return ins[t]

        local = [pltpu.make_async_copy(part(t, me), outs[t].at[me], local_sems.at[t]) for t in range(n_all)]
        for cp in local:
            cp.start()
        sends = []
        for mask in range(1, N_DEV):
            for t in range(n_all):
                cp = pltpu.make_async_remote_copy(
                    src_ref=part(t, me ^ mask), dst_ref=outs[t].at[me], send_sem=send_sems.at[t, mask - 1],
                    recv_sem=recv_sems.at[t, mask - 1], device_id=_peer(x, y, c, mask),
                    device_id_type=pl.DeviceIdType.MESH)
                cp.start()
                sends.append(cp)
        for mask in range(1, N_DEV):
            for t in range(n_all):
                pltpu.make_async_remote_copy(
                    src_ref=part(t, me), dst_ref=outs[t].at[me ^ mask], send_sem=send_sems.at[t, mask - 1],
                    recv_sem=recv_sems.at[t, mask - 1], device_id=_peer(x, y, c, mask),
                    device_id_type=pl.DeviceIdType.MESH).wait_recv()
        for cp in sends:
            cp.wait_send()
        for cp in local:
            cp.wait()

    out_shape = [jax.ShapeDtypeStruct((N_DEV, a.shape[0], a.shape[1] // N_DEV, a.shape[2]), a.dtype) for a in row_full]
    out_shape += [jax.ShapeDtypeStruct(a.shape, a.dtype) for a in stack_full]
    out_shape += [jax.ShapeDtypeStruct((N_DEV,) + a.shape, a.dtype) for a in bcast]
    return pl.pallas_call(
        body, name="reduce_scatter_grads", in_specs=[_ANY] * n_all, out_specs=[_ANY] * n_all, out_shape=out_shape,
        scratch_shapes=[pltpu.SemaphoreType.DMA((n_all, N_DEV - 1)), pltpu.SemaphoreType.DMA((n_all, N_DEV - 1)),
                        pltpu.SemaphoreType.DMA((n_all,))],
    )(*fulls)


def _adamw(slots, w, m, v, name):
    shape = w.shape
    cols = shape[-1]
    rows = math.prod(shape[:-1])
    tr = _pick(rows, (64, 32, 16, 8))
    c1 = 1.0 - ADAM_B1 ** ADAM_STEP
    c2 = 1.0 - ADAM_B2 ** ADAM_STEP

    def body(s_ref, w_ref, m_ref, v_ref, g_ref, d_ref, nm_ref, nv_ref):
        g = s_ref[0].astype(F32)
        for k in range(1, N_DEV):
            g = g + s_ref[k].astype(F32)
        nm = ADAM_B1 * m_ref[...] + (1.0 - ADAM_B1) * g
        nv = ADAM_B2 * v_ref[...] + (1.0 - ADAM_B2) * (g * g)
        g_ref[...] = g
        nm_ref[...] = nm
        nv_ref[...] = nv
        d_ref[...] = -ADAM_LR * ((nm / c1) / (jnp.sqrt(nv / c2) + ADAM_EPS) + ADAM_WD * w_ref[...])

    row = pl.BlockSpec((tr, cols), lambda i: (i, 0))
    out = jax.ShapeDtypeStruct((rows, cols), F32)
    outs = pl.pallas_call(
        body, name=name, grid=(rows // tr,),
        in_specs=[pl.BlockSpec((N_DEV, tr, cols), lambda i: (0, i, 0)), row, row, row],
        out_specs=[row] * 4, out_shape=[out] * 4, compiler_params=_cp(("parallel",)),
    )(slots.reshape(N_DEV, rows, cols), w.reshape(rows, cols), m.reshape(rows, cols), v.reshape(rows, cols))
    return [o.reshape(shape) for o in outs]


_SMALL = ("ln_in_g", "ln_in_b", "mem_ln_g", "mem_ln_b", "b_forget", "mla_q_norm_g", "mla_kv_norm_g", "ln_g", "ln_b")
_ORDER = ("ln_in_g", "ln_in_b", "mem_ln_g", "mem_ln_b", "w_in", "b_forget", "mla_q_norm_g", "w_mla_q_up",
          "mla_kv_norm_g", "w_mla_kv_up", "w_mem_kv", "w_out", "ln_g", "ln_b")


def _pack_small(d):
    flat = jnp.concatenate([d[n].reshape(-1) for n in _SMALL])
    n = flat.shape[0]
    padded = ((n + 8 * LANE - 1) // (8 * LANE)) * (8 * LANE)
    return jnp.pad(flat, (0, padded - n)).reshape(-1, LANE)


def _unpack_small(packed, like):
    flat, out, off = packed.reshape(-1), {}, 0
    for n in _SMALL:
        size = math.prod(like[n].shape)
        out[n] = flat[off:off + size].reshape(like[n].shape)
        off += size
    return out


def _unstack_cols(g):
    n, l, r, c = g.shape
    return g.transpose(1, 2, 0, 3).reshape(l, r, n * c)


def _stack_cols(g):
    l, r, nc = g.shape
    return g.reshape(l, r, N_DEV, nc // N_DEV).transpose(2, 0, 1, 3)


def kernel(x, mem, ln_in_g, ln_in_b, mem_ln_g, mem_ln_b, w_in, b_forget, mla_q_norm_g, w_mla_q_up, mla_kv_norm_g, w_mla_kv_up, w_mem_kv, w_out, ln_g, ln_b, loss_target, m_ln_in_g, m_ln_in_b, m_mem_ln_g, m_mem_ln_b, m_w_in, m_b_forget, m_mla_q_norm_g, m_w_mla_q_up, m_mla_kv_norm_g, m_w_mla_kv_up, m_w_mem_kv, m_w_out, m_ln_g, m_ln_b, v_ln_in_g, v_ln_in_b, v_mem_ln_g, v_mem_ln_b, v_w_in, v_b_forget, v_mla_q_norm_g, v_w_mla_q_up, v_mla_kv_norm_g, v_w_mla_kv_up, v_w_mem_kv, v_w_out, v_ln_g, v_ln_b):
    w_shard = dict(ln_in_g=ln_in_g, ln_in_b=ln_in_b, mem_ln_g=mem_ln_g, mem_ln_b=mem_ln_b, w_in=w_in,
                   b_forget=b_forget, mla_q_norm_g=mla_q_norm_g, w_mla_q_up=w_mla_q_up,
                   mla_kv_norm_g=mla_kv_norm_g, w_mla_kv_up=w_mla_kv_up, w_mem_kv=w_mem_kv, w_out=w_out,
                   ln_g=ln_g, ln_b=ln_b)
    m_shard = dict(ln_in_g=m_ln_in_g, ln_in_b=m_ln_in_b, mem_ln_g=m_mem_ln_g, mem_ln_b=m_mem_ln_b, w_in=m_w_in,
                   b_forget=m_b_forget, mla_q_norm_g=m_mla_q_norm_g, w_mla_q_up=m_w_mla_q_up,
                   mla_kv_norm_g=m_mla_kv_norm_g, w_mla_kv_up=m_w_mla_kv_up, w_mem_kv=m_w_mem_kv, w_out=m_w_out,
                   ln_g=m_ln_g, ln_b=m_ln_b)
    v_shard = dict(ln_in_g=v_ln_in_g, ln_in_b=v_ln_in_b, mem_ln_g=v_mem_ln_g, mem_ln_b=v_mem_ln_b, w_in=v_w_in,
                   b_forget=v_b_forget, mla_q_norm_g=v_mla_q_norm_g, w_mla_q_up=v_w_mla_q_up,
                   mla_kv_norm_g=v_mla_kv_norm_g, w_mla_kv_up=v_w_mla_kv_up, w_mem_kv=v_w_mem_kv, w_out=v_w_out,
                   ln_g=v_ln_g, ln_b=v_ln_b)

    to16 = lambda ws: [a.astype(BF16) for a in ws]
    gathered = _all_gather(to16([_permute_cols(w_in), w_mem_kv, w_out]), to16([w_mla_q_up, w_mla_kv_up]))
    g_in, g_mem, g_out, g_qup, g_kvup = [a.astype(F32) for a in gathered]
    full = dict(w_shard)
    full.update(w_in=g_in, w_mem_kv=g_mem, w_out=g_out, w_mla_q_up=_unstack_cols(g_qup),
                w_mla_kv_up=_unstack_cols(g_kvup))

    loss_local, (grad_w, grad_x) = jax.value_and_grad(_trunk_loss, argnums=(0, 1))(
        full, x[0], mem[0], loss_target[0])

    s_in, s_mem, s_out, s_qup, s_kvup, s_small = _reduce_scatter(
        to16([grad_w["w_in"], grad_w["w_mem_kv"], grad_w["w_out"]]),
        to16([_stack_cols(grad_w["w_mla_q_up"]), _stack_cols(grad_w["w_mla_kv_up"])]),
        [_pack_small(grad_w)])

    res = {}
    for name, slots in (("w_mem_kv", s_mem), ("w_out", s_out), ("w_mla_q_up", s_qup), ("w_mla_kv_up", s_kvup)):
        res[name] = _adamw(slots, w_shard[name], m_shard[name], v_shard[name], "adamw_" + name)
    res["w_in"] = [_unpermute_cols(a) for a in _adamw(
        s_in, _permute_cols(w_in), _permute_cols(m_w_in), _permute_cols(v_w_in), "adamw_w_in")]
    small = _adamw(s_small, _pack_small(w_shard), _pack_small(m_shard), _pack_small(v_shard), "adamw_small")
    small = [_unpack_small(a, w_shard) for a in small]
    for name in _SMALL:
        res[name] = [a[name] for a in small]

    loss = lax.psum(loss_local, MESH_AXES)
    outs = [loss, grad_x[None]]
    for k in range(4):
        outs += [res[name][k] for name in _ORDER]
    return tuple(outs)
```

```python
import functools
import math

import jax
import jax.numpy as jnp
from jax import lax
from jax.experimental import pallas as pl
from jax.experimental.pallas import tpu as pltpu

F32 = jnp.float32
BF16 = jnp.bfloat16

D_MODEL = 1024
DEPTH = 2
GROUP_W = 256
N_HEADS = 4
HEAD_DIM = 64
MLA_Q_RANK = 256
MLA_KV_RANK = 128
MLA_NOPE = 64
MLA_ROPE = 32
MLA_V = 64
ROPE_THETA = 10000.0
LN_EPS = 1e-5
RMS_EPS = 1e-6
DEEPNORM_ALPHA = (2 * DEPTH) ** 0.25
SPLIT_SIZES = (256, 256, 256, 4, 256, 256, 256, 256, 128, 32, 256, 1024)
IN_COLS = sum(SPLIT_SIZES)
_ORIG_OFF = [sum(SPLIT_SIZES[:i]) for i in range(len(SPLIT_SIZES))]
_PERM = (("fq", 0), ("fk", 1), ("fv", 2), ("sq", 4), ("sk", 5), ("sv", 6), ("c_q", 7), ("c_kv", 8),
         ("mq", 10), ("gate", 11), ("k_rot", 9), ("f_logit", 3))
LANE = 128
PROJ_COLS = ((IN_COLS + LANE - 1) // LANE) * LANE

ADAM_LR = 0.001
ADAM_B1 = 0.9
ADAM_B2 = 0.999
ADAM_EPS = 1e-08
ADAM_WD = 0.01
ADAM_STEP = 10

N_DEV = 8
MESH_AXES = ("x", "y", "c")
VMEM_LIMIT = 48 * 1024 * 1024
ATTN_VMEM_LIMIT = 56 * 1024 * 1024
ATTN_BQ = 512
ATTN_BK = 512
CUMSUM_CHUNK = 256
NEG_BIG = -1e30
LOG2E = math.log2(math.e)
MM_TM, MM_TN, MM_TK, MM_TK_NT = 1024, 1664, 1024, 3328
CONCAT_MM_TM = 512

_NT = (((1,), (1,)), ((), ()))
_NN = (((1,), (0,)), ((), ()))


def _cp(sem, vmem=VMEM_LIMIT):
    return pltpu.CompilerParams(dimension_semantics=sem, vmem_limit_bytes=vmem)


def _dot(a, b, dims=_NN):
    return lax.dot_general(a, b, dims, preferred_element_type=F32)


def _pick(n, cands):
    for c in cands:
        if c <= n and n % c == 0:
            return c
    return n


def _tile(n, cap):
    if n <= cap:
        return n
    best = None
    for d in range(LANE, cap + 1, LANE):
        if n % d == 0:
            best = d
    assert best is not None, (n, cap)
    return best


def _matmul(a, b, mode, name):
    if mode == "nn":
        (M, K), (K2, N) = a.shape, b.shape
    else:
        (M, K), (N, K2) = a.shape, b.shape
    assert K == K2 and a.dtype == BF16 and b.dtype == BF16, (a.shape, b.shape, mode)
    tm, tn = _tile(M, MM_TM), _tile(N, MM_TN)
    tk = _tile(K, MM_TK if mode == "nn" else MM_TK_NT)
    nk = K // tk
    dims = _NN if mode == "nn" else _NT

    def body(a_ref, b_ref, o_ref, acc_ref):
        part = _dot(a_ref[...], b_ref[...], dims)
        if nk == 1:
            o_ref[...] = part
        else:
            k = pl.program_id(2)

            @pl.when(k == 0)
            def _():
                acc_ref[...] = part

            @pl.when(k > 0)
            def _():
                acc_ref[...] += part

            @pl.when(k == nk - 1)
            def _():
                o_ref[...] = acc_ref[...]

    a_spec = pl.BlockSpec((tm, tk), lambda j, i, k: (i, k))
    if mode == "nn":
        b_spec = pl.BlockSpec((tk, tn), lambda j, i, k: (k, j))
    else:
        b_spec = pl.BlockSpec((tn, tk), lambda j, i, k: (j, k))
    acc_shape = (tm, tn) if nk > 1 else (8, LANE)
    o_spec = pl.BlockSpec((tm, tn), lambda j, i, k: (i, j))
    return pl.pallas_call(
        body, name=name, grid=(N // tn, M // tm, nk),
        in_specs=[a_spec, b_spec], out_specs=o_spec, out_shape=jax.ShapeDtypeStruct((M, N), F32),
        scratch_shapes=[pltpu.VMEM(acc_shape, F32)],
        compiler_params=_cp(("parallel", "parallel", "arbitrary")),
    )(a, b)


def _concat_matmul_nt(pieces, b, name):
    M, (N, K) = pieces[0].shape[0], b.shape
    widths = [p.shape[1] for p in pieces]
    assert sum(widths) == K and all(w % LANE == 0 for w in widths) and b.dtype == BF16, (widths, b.shape)
    tm = _tile(M, CONCAT_MM_TM)
    n = len(pieces)

    def body(*refs):
        b_ref, o_ref, a_ref = refs[n:]
        a_ref[...] = jnp.concatenate([r[...].astype(BF16) for r in refs[:n]], axis=1)
        o_ref[...] = _dot(a_ref[...], b_ref[...], _NT)

    rows = lambda w: pl.BlockSpec((tm, w), lambda i: (i, 0))
    return pl.pallas_call(
        body, name=name, grid=(M // tm,),
        in_specs=[rows(w) for w in widths] + [pl.BlockSpec((N, K), lambda i: (0, 0))],
        out_specs=[rows(N), rows(K)],
        out_shape=[jax.ShapeDtypeStruct((M, N), F32), jax.ShapeDtypeStruct((M, K), BF16)],
        compiler_params=_cp(("parallel",)),
    )(*pieces, b)


def _matmul_split(a, b, widths, dtypes, name):
    (M, K), (K2, N) = a.shape, b.shape
    assert K == K2 and sum(widths) == N and all(w % LANE == 0 for w in widths), (a.shape, b.shape, widths)
    assert a.dtype == BF16 and b.dtype == BF16
    tm = _tile(M, CONCAT_MM_TM)
    offs = [sum(widths[:r]) for r in range(len(widths))]

    def body(a_ref, b_ref, *o_refs):
        av = a_ref[...]
        for o_ref, off, w in zip(o_refs, offs, widths):
            o_ref[...] = _dot(av, b_ref[:, off:off + w]).astype(o_ref.dtype)

    rows = lambda w: pl.BlockSpec((tm, w), lambda i: (i, 0))
    return pl.pallas_call(
        body, name=name, grid=(M // tm,),
        in_specs=[rows(K), pl.BlockSpec((K, N), lambda i: (0, 0))],
        out_specs=[rows(w) for w in widths],
        out_shape=[jax.ShapeDtypeStruct((M, w), d) for w, d in zip(widths, dtypes)],
        compiler_params=_cp(("parallel",)),
    )(a, b)


def _make_mm(name):
    @jax.custom_vjp
    def mm(a, w):
        return _matmul(a.astype(BF16), w.astype(BF16), "nn", name + "_fwd")

    def fwd(a, w):
        a16, w16 = a.astype(BF16), w.astype(BF16)
        return _matmul(a16, w16, "nn", name + "_fwd"), (a16, w16)

    def bwd(res, dy):
        a16, w16 = res
        dy16 = dy.astype(BF16)
        da = _matmul(dy16, w16, "nt", name + "_dx")
        dw = _matmul(a16.T, dy16, "nn", name + "_dw")
        return da, dw

    mm.defvjp(fwd, bwd)
    return mm


def _row_tile(rows):
    return _pick(rows, (512, 256, 128, 64, 32, 16, 8))


def _ln_stats(u):
    mu = jnp.mean(u, axis=-1, keepdims=True)
    d = u - mu
    var = jnp.mean(d * d, axis=-1, keepdims=True)
    return d, lax.rsqrt(var + LN_EPS)


def _ln_fwd_call(x, res, g, b, name, also16=False):
    rows, dm = x.shape
    tr = _row_tile(rows)
    has_res = res is not None
    n_in = 2 if has_res else 1

    def body(*refs):
        if has_res:
            u = DEEPNORM_ALPHA * refs[1][...] + refs[0][...]
        else:
            u = refs[0][...]
        g_ref, b_ref = refs[n_in], refs[n_in + 1]
        d, rstd = _ln_stats(u)
        y = d * rstd * g_ref[...] + b_ref[...]
        refs[n_in + 2][...] = y
        if also16:
            y16 = y.astype(BF16)
            refs[n_in + 3][...] = y16
            refs[n_in + 4][...] = y16.T

    row = pl.BlockSpec((tr, dm), lambda i: (i, 0))
    vec = pl.BlockSpec((1, dm), lambda i: (0, 0))
    args = (x, res) if has_res else (x,)
    out_specs, out_shape = [row], [jax.ShapeDtypeStruct((rows, dm), F32)]
    if also16:
        out_specs += [row, pl.BlockSpec((dm, tr), lambda i: (0, i))]
        out_shape += [jax.ShapeDtypeStruct((rows, dm), BF16), jax.ShapeDtypeStruct((dm, rows), BF16)]
    outs = pl.pallas_call(
        body, name=name, grid=(rows // tr,),
        in_specs=[row] * n_in + [vec, vec], out_specs=out_specs, out_shape=out_shape,
        compiler_params=_cp(("parallel",)),
    )(*args, g.reshape(1, dm), b.reshape(1, dm))
    return outs if also16 else outs[0]


def _ln_bwd_call(dy, x, res, g, name, dy2=None):
    rows, dm = x.shape
    tr = _row_tile(rows)
    has_res = res is not None
    two = dy2 is not None

    def body(*refs):
        dy_ref, refs = refs[0], refs[1:]
        if two:
            dy2_ref, refs = refs[0], refs[1:]
        if has_res:
            x_ref, r_ref, g_ref, dx_ref, dr_ref, dg_ref, db_ref = refs
            u = DEEPNORM_ALPHA * r_ref[...] + x_ref[...]
        else:
            x_ref, g_ref, dx_ref, dg_ref, db_ref = refs
            u = x_ref[...]
        i = pl.program_id(0)
        d, rstd = _ln_stats(u)
        xhat = d * rstd
        dyv = dy_ref[...] + dy2_ref[...] if two else dy_ref[...]
        dxh = dyv * g_ref[...]
        m1 = jnp.mean(dxh, axis=-1, keepdims=True)
        m2 = jnp.mean(dxh * xhat, axis=-1, keepdims=True)
        du = rstd * (dxh - m1 - xhat * m2)
        dx_ref[...] = du
        if has_res:
            dr_ref[...] = DEEPNORM_ALPHA * du
        pg = jnp.sum(dyv * xhat, axis=0, keepdims=True)
        pb = jnp.sum(dyv, axis=0, keepdims=True)

        @pl.when(i == 0)
        def _():
            dg_ref[...] = pg
            db_ref[...] = pb

        @pl.when(i > 0)
        def _():
            dg_ref[...] += pg
            db_ref[...] += pb

    row = pl.BlockSpec((tr, dm), lambda i: (i, 0))
    vec = pl.BlockSpec((1, dm), lambda i: (0, 0))
    big = jax.ShapeDtypeStruct((rows, dm), F32)
    small = jax.ShapeDtypeStruct((1, dm), F32)
    args = ((dy, dy2) if two else (dy,)) + ((x, res) if has_res else (x,))
    n_big = 2 if has_res else 1
    outs = pl.pallas_call(
        body, name=name, grid=(rows // tr,),
        in_specs=[row] * len(args) + [vec],
        out_specs=[row] * n_big + [vec, vec],
        out_shape=[big] * n_big + [small, small],
        compiler_params=_cp(("arbitrary",)),
    )(*args, g.reshape(1, dm))
    return outs


def _make_ln(name, has_res):
    if has_res:
        @jax.custom_vjp
        def ln(x, res, g, b):
            return _ln_fwd_call(x, res, g, b, name + "_fwd")

        def fwd(x, res, g, b):
            return ln(x, res, g, b), (x, res, g)

        def bwd(saved, dy):
            x, res, g = saved
            dx, dr, dg, db = _ln_bwd_call(dy, x, res, g, name + "_bwd")
            return dx, dr, dg.reshape(-1), db.reshape(-1)
    else:
        @jax.custom_vjp
        def ln(x, g, b):
            return _ln_fwd_call(x, None, g, b, name + "_fwd")

        def fwd(x, g, b):
            return ln(x, g, b), (x, g)

        def bwd(saved, dy):
            x, g = saved
            dx, dg, db = _ln_bwd_call(dy, x, None, g, name + "_bwd")
            return dx, dg.reshape(-1), db.reshape(-1)

    ln.defvjp(fwd, bwd)
    return ln


def _rms_fwd_call(x, g, name):
    rows, dm = x.shape
    tr = _row_tile(rows)

    def body(x_ref, g_ref, o_ref):
        xv = x_ref[...]
        rstd = lax.rsqrt(jnp.mean(xv * xv, axis=-1, keepdims=True) + RMS_EPS)
        o_ref[...] = xv * rstd * g_ref[...]

    row = pl.BlockSpec((tr, dm), lambda i: (i, 0))
    vec = pl.BlockSpec((1, dm), lambda i: (0, 0))
    return pl.pallas_call(
        body, name=name, grid=(rows // tr,), in_specs=[row, vec], out_specs=row,
        out_shape=jax.ShapeDtypeStruct((rows, dm), F32), compiler_params=_cp(("parallel",)),
    )(x, g.reshape(1, dm))


def _rms_bwd_call(dy, x, g, name):
    rows, dm = x.shape
    tr = _row_tile(rows)

    def body(dy_ref, x_ref, g_ref, dx_ref, dg_ref):
        i = pl.program_id(0)
        xv = x_ref[...]
        dyv = dy_ref[...]
        rstd = lax.rsqrt(jnp.mean(xv * xv, axis=-1, keepdims=True) + RMS_EPS)
        xhat = xv * rstd
        dxh = dyv * g_ref[...]
        m2 = jnp.mean(dxh * xhat, axis=-1, keepdims=True)
        dx_ref[...] = rstd * (dxh - xhat * m2)
        pg = jnp.sum(dyv * xhat, axis=0, keepdims=True)

        @pl.when(i == 0)
        def _():
            dg_ref[...] = pg

        @pl.when(i > 0)
        def _():
            dg_ref[...] += pg

    row = pl.BlockSpec((tr, dm), lambda i: (i, 0))
    vec = pl.BlockSpec((1, dm), lambda i: (0, 0))
    return pl.pallas_call(
        body, name=name, grid=(rows // tr,), in_specs=[row, row, vec], out_specs=[row, vec],
        out_shape=[jax.ShapeDtypeStruct((rows, dm), F32), jax.ShapeDtypeStruct((1, dm), F32)],
        compiler_params=_cp(("arbitrary",)),
    )(dy, x, g.reshape(1, dm))


def _make_rms(name):
    @jax.custom_vjp
    def rms(x, g):
        return _rms_fwd_call(x, g, name + "_fwd")

    def fwd(x, g):
        return rms(x, g), (x, g)

    def bwd(saved, dy):
        x, g = saved
        dx, dg = _rms_bwd_call(dy, x, g, name + "_bwd")
        return dx, dg.reshape(-1)

    rms.defvjp(fwd, bwd)
    return rms


def _sigmoid(x):
    return 1.0 / (1.0 + jnp.exp(-x))


def _gate_fwd_call(parts, gate, name):
    rows, dm = gate.shape
    tr = _row_tile(rows)
    n = len(parts)

    def body(*refs):
        g_ref, o_ref, oT_ref = refs[n:]
        gv = g_ref[...]
        mixed = jnp.concatenate([r[...] for r in refs[:n]], axis=1)
        y16 = (mixed * (gv * _sigmoid(gv))).astype(BF16)
        o_ref[...] = y16
        oT_ref[...] = y16.T

    row = pl.BlockSpec((tr, dm), lambda i: (i, 0))
    part_specs = [pl.BlockSpec((tr, p.shape[1]), lambda i: (i, 0)) for p in parts]
    return pl.pallas_call(
        body, name=name, grid=(rows // tr,), in_specs=part_specs + [row],
        out_specs=[row, pl.BlockSpec((dm, tr), lambda i: (0, i))],
        out_shape=[jax.ShapeDtypeStruct((rows, dm), BF16), jax.ShapeDtypeStruct((dm, rows), BF16)],
        compiler_params=_cp(("parallel",)),
    )(*parts, gate)


def _gate_bwd_call(dy16, w16, parts, gate, name):
    rows, dm = gate.shape
    tr = _row_tile(rows)
    n = len(parts)
    widths = [p.shape[1] for p in parts]
    assert w16.shape == (dm, dy16.shape[1]) and dy16.dtype == BF16 and w16.dtype == BF16

    def body(*refs):
        dy_ref, w_ref, g_ref = refs[0], refs[1], refs[n + 2]
        dm_refs, dg_ref = refs[n + 3:2 * n + 3], refs[2 * n + 3]
        gv = g_ref[...]
        dyv = _dot(dy_ref[...], w_ref[...], _NT)
        sg = _sigmoid(gv)
        mixed = jnp.concatenate([r[...] for r in refs[2:n + 2]], axis=1)
        dmixed = dyv * (gv * sg)
        off = 0
        for r, w in zip(dm_refs, widths):
            r[...] = dmixed[:, off:off + w]
            off += w
        dg_ref[...] = dyv * mixed * (sg * (1.0 + gv * (1.0 - sg)))

    row = pl.BlockSpec((tr, dm), lambda i: (i, 0))
    part_specs = [pl.BlockSpec((tr, w), lambda i: (i, 0)) for w in widths]
    dy_spec = pl.BlockSpec((tr, dy16.shape[1]), lambda i: (i, 0))
    w_spec = pl.BlockSpec(w16.shape, lambda i: (0, 0))
    return pl.pallas_call(
        body, name=name, grid=(rows // tr,), in_specs=[dy_spec, w_spec] + part_specs + [row],
        out_specs=part_specs + [row],
        out_shape=[jax.ShapeDtypeStruct((rows, w), F32) for w in widths] + [jax.ShapeDtypeStruct((rows, dm), F32)],
        compiler_params=_cp(("parallel",)),
    )(dy16, w16, *parts, gate)


def _make_gate_out(name):
    def run_fwd(parts, gate, w):
        g16, gT16 = _gate_fwd_call(parts, gate, name + "_gate_fwd")
        w16 = w.astype(BF16)
        return _matmul(g16, w16, "nn", name + "_fwd"), (parts, gate, gT16, w16)

    def run_bwd(saved, dy):
        parts, gate, gT16, w16 = saved
        dy16 = dy.astype(BF16)
        *dparts, dgate = _gate_bwd_call(dy16, w16, parts, gate, name + "_gate_bwd")
        return tuple(dparts), dgate, _matmul(gT16, dy16, "nn", name + "_dw")

    @jax.custom_vjp
    def gate_out(parts, gate, w):
        return run_fwd(parts, gate, w)[0]

    gate_out.defvjp(run_fwd, run_bwd)
    return gate_out


def _loss_call(y, t, name):
    rows, dm = y.shape
    tr = _row_tile(rows)

    def body(y_ref, t_ref, l_ref, d_ref):
        i = pl.program_id(0)
        e = y_ref[...] - t_ref[...]
        d_ref[...] = e * (1.0 / dm)
        part = 0.5 * jnp.sum(jnp.mean(e * e, axis=-1, keepdims=True), axis=0, keepdims=True)

        @pl.when(i == 0)
        def _():
            l_ref[...] = part

        @pl.when(i > 0)
        def _():
            l_ref[...] += part

    row = pl.BlockSpec((tr, dm), lambda i: (i, 0))
    one = pl.BlockSpec((1, 1), lambda i: (0, 0))
    return pl.pallas_call(
        body, name=name, grid=(rows // tr,), in_specs=[row, row], out_specs=[one, row],
        out_shape=[jax.ShapeDtypeStruct((1, 1), F32), jax.ShapeDtypeStruct((rows, dm), F32)],
        compiler_params=_cp(("arbitrary",)),
    )(y, t)


@jax.custom_vjp
def _loss_op(y, t):
    return _loss_call(y, t, "loss_head")[0][0, 0]


def _loss_fwd(y, t):
    l, d = _loss_call(y, t, "loss_head")
    return l[0, 0], d


def _loss_bwd(d, ct):
    return ct * d, jnp.zeros_like(d)


_loss_op.defvjp(_loss_fwd, _loss_bwd)


def _attn_blocks(S, Sk, cap=None):
    bq, bk = min(cap or ATTN_BQ, S), min(cap or ATTN_BK, Sk)
    assert S % bq == 0 and Sk % bk == 0
    return bq, bk


def _valid_t(i, j, bq, bk, strict):
    key = j * bk + lax.broadcasted_iota(jnp.int32, (bk, bq), 0)
    qry = i * bq + lax.broadcasted_iota(jnp.int32, (bk, bq), 1)
    return (key < qry) if strict else (key <= qry)


def _sm_fwd_t(qn, k, vT, cmul, causal, name):
    H, S, DK = qn.shape
    Sk, dv = k.shape[1], vT.shape[1]
    bq, bk = _attn_blocks(S, Sk)
    nq, nkb = S // bq, Sk // bk
    hb = PAIR * FWD_PAIRS if H % (PAIR * FWD_PAIRS) == 0 else 1
    heads = range(hb)
    if causal:
        assert S == Sk and bq == bk

    def body(qn_ref, k_ref, vT_ref, oT_ref, lse_ref):
        i = pl.program_id(1)
        qTs = [qn_ref[w].T for w in heads]

        def blk(j, carry, masked):
            off = pl.multiple_of(j * bk, bk)
            sT = [_dot(k_ref[w, pl.ds(off, bk), :], qTs[w]) * cmul for w in heads]
            if masked:
                valid = _valid_t(i, j, bq, bk, False)
                sT = [jnp.where(valid, s, NEG_BIG) for s in sT]
            m_new = [jnp.maximum(carry[w][0], jnp.max(sT[w], axis=0, keepdims=True)) for w in heads]
            p = [jnp.exp2(sT[w] - m_new[w]) for w in heads]
            a = [jnp.exp2(carry[w][0] - m_new[w]) for w in heads]
            l = [a[w] * carry[w][1] + jnp.sum(p[w], axis=0, keepdims=True) for w in heads]
            acc = [a[w] * carry[w][2] + _dot(vT_ref[w, :, pl.ds(off, bk)], p[w].astype(BF16)) for w in heads]
            return tuple((m_new[w], l[w], acc[w]) for w in heads)

        carry = tuple((jnp.full((1, bq), NEG_BIG, F32), jnp.zeros((1, bq), F32), jnp.zeros((dv, bq), F32))
                      for _ in heads)
        if causal:
            carry = lax.fori_loop(0, i, lambda j, c: blk(j, c, False), carry)
            carry = blk(i, carry, True)
        else:
            carry = lax.fori_loop(0, nkb, lambda j, c: blk(j, c, False), carry)
        for w in heads:
            oT_ref[w] = carry[w][2] / carry[w][1]
            lse_ref[w] = carry[w][0] + jnp.log2(carry[w][1])

    qcol = lambda d: pl.BlockSpec((hb, d, bq), lambda h, i: (h, 0, i))
    return pl.pallas_call(
        body, name=name, grid=(H // hb, nq),
        in_specs=[pl.BlockSpec((hb, bq, DK), lambda h, i: (h, i, 0)), pl.BlockSpec((hb, Sk, DK), lambda h, i: (h, 0, 0)),
                  pl.BlockSpec((hb, dv, Sk), lambda h, i: (h, 0, 0))],
        out_specs=[qcol(dv), qcol(1)],
        out_shape=[jax.ShapeDtypeStruct((H, dv, S), F32), jax.ShapeDtypeStruct((H, 1, S), F32)],
        compiler_params=_cp(("parallel", "arbitrary"), ATTN_VMEM_LIMIT),
    )(qn, k, vT)


def _sm_bwd_t(qn, k, v, oT, lse, doT, do, cmul, gscale, causal, name):
    H, S, DK = qn.shape
    Sk, dv = k.shape[1], v.shape[2]
    bq, bk = _attn_blocks(S, Sk)
    nq, nkb = S // bq, Sk // bk

    def body(qn_ref, k_ref, v_ref, oT_ref, lse_ref, doT_ref, do_ref, dq_ref, dk_ref, dv_ref):
        i = pl.program_id(1)

        @pl.when(i == 0)
        def _():
            dk_ref[...] = jnp.zeros_like(dk_ref)
            dv_ref[...] = jnp.zeros_like(dv_ref)

        qnb = qn_ref[...]
        qTb = qnb.T
        dob = do_ref[...]
        doTf = doT_ref[...]
        doTb = doTf.astype(BF16)
        delta = jnp.sum(doTf * oT_ref[...], axis=0, keepdims=True)
        lse = lse_ref[...]

        def blk(j, dq, masked):
            off = pl.multiple_of(j * bk, bk)
            kb = k_ref[pl.ds(off, bk), :]
            sT = _dot(kb, qTb) * cmul
            if masked:
                sT = jnp.where(_valid_t(i, j, bq, bk, False), sT, NEG_BIG)
            p = jnp.exp2(sT - lse)
            dp = _dot(v_ref[pl.ds(off, bk), :], doTb)
            ds = p * (dp - delta)
            dsb = (ds * gscale).astype(BF16) if gscale != 1.0 else ds.astype(BF16)
            dv_ref[pl.ds(off, bk), :] += _dot(p.astype(BF16), dob)
            dk_ref[pl.ds(off, bk), :] += _dot(dsb, qnb)
            return dq + _dot(kb.T, dsb)

        dq = jnp.zeros((DK, bq), F32)
        if causal:
            dq = lax.fori_loop(0, i, lambda j, c: blk(j, c, False), dq)
            dq = blk(i, dq, True)
        else:
            dq = lax.fori_loop(0, nkb, lambda j, c: blk(j, c, False), dq)
        dq_ref[...] = dq.T

    qcol = lambda d: pl.BlockSpec((None, d, bq), lambda h, i: (h, 0, i))
    qrow = lambda d: pl.BlockSpec((None, bq, d), lambda h, i: (h, i, 0))
    krow = lambda d: pl.BlockSpec((None, Sk, d), lambda h, i: (h, 0, 0))
    return pl.pallas_call(
        body, name=name, grid=(H, nq),
        in_specs=[qrow(DK), krow(DK), krow(dv), qcol(dv), qcol(1), qcol(dv), qrow(dv)],
        out_specs=[qrow(DK), krow(DK), krow(dv)],
        out_shape=[jax.ShapeDtypeStruct((H, S, DK), F32), jax.ShapeDtypeStruct((H, Sk, DK), F32),
                   jax.ShapeDtypeStruct((H, Sk, dv), F32)],
        compiler_params=_cp(("parallel", "arbitrary"), ATTN_VMEM_LIMIT),
    )(qn, k, v, oT, lse, doT, do)


def _tri(n, fn):
    r = lax.broadcasted_iota(jnp.int32, (n, n), 0)
    c = lax.broadcasted_iota(jnp.int32, (n, n), 1)
    return jnp.where(fn(r, c), 1.0, 0.0).astype(BF16)


def _key_cumsum(x, tri2, suffix, base):
    bk = x.shape[0]
    c = min(CUMSUM_CHUNK, bk)
    n = bk // c
    hi32 = lax.bitcast_convert_type(lax.bitcast_convert_type(x, jnp.int32) & jnp.int32(-65536), F32)
    hi = hi32.astype(BF16)
    lo = (x - hi32).astype(BF16)
    tot = [jnp.sum(x[a * c:(a + 1) * c], axis=0, keepdims=True) for a in range(n)]
    outs = []
    for a in range(n):
        row = base
        for t in (tot[a + 1:] if suffix else tot[:a]):
            row = row + t
        stacked = jnp.concatenate([hi[a * c:(a + 1) * c], lo[a * c:(a + 1) * c]], axis=0)
        outs.append(_dot(tri2, stacked) + row)
    total = tot[0]
    for t in tot[1:]:
        total = total + t
    return (outs[0] if n == 1 else jnp.concatenate(outs, axis=0)), total


def _tri2(n, fn):
    t = _tri(n, fn)
    return jnp.concatenate([t, t], axis=1)


def _sb_logs(z):
    neg_abs = lax.bitcast_convert_type(lax.bitcast_convert_type(z, jnp.int32) | jnp.int32(-2 ** 31), F32)
    ls = jnp.minimum(z, 0.0) - jnp.log(1.0 + jnp.exp(neg_abs))
    return ls, ls - z


PAIR = LANE // HEAD_DIM
FWD_PAIRS = 2
SB_DEAD = -110.0
SB_BLOCK = 256
FOX_DEAD = -160.0
FOX_BLOCK = 512


def _head_lanes(shape, w, axis):
    idx = lax.broadcasted_iota(jnp.int32, shape, axis)
    return (idx >= HEAD_DIM * w) & (idx < HEAD_DIM * (w + 1))


def _bias_rows(w, bq):
    row = lax.broadcasted_iota(jnp.int32, (LANE, bq), 0)
    return jnp.where((row >= 3 * w) & (row < 3 * w + 3), -1.0, 0.0).astype(BF16)


def _merge_pair(parts):
    return jnp.where(_head_lanes(parts[0].shape, 0, 0), parts[0], parts[1]).T


def _smp_fwd(q2, k2, v2, bias, r, causal, name, kstat=None):
    S, C = q2.shape
    Sk = k2.shape[0]
    bq, bk = _attn_blocks(S, Sk, FOX_BLOCK if bias is not None else None)
    nq, nkb, P = S // bq, Sk // bk, C // LANE
    gp = FWD_PAIRS if P % FWD_PAIRS == 0 else 1
    use_f = bias is not None
    if causal:
        assert S == Sk and bq == bk and use_f

    def body(*refs):
        if use_f:
            ks_ref, q_ref, k_ref, v_ref, b_ref, r_ref, o_ref, lse_ref, js_ref = refs
        else:
            q_ref, k_ref, v_ref, o_ref, lse_ref = refs
        i = pl.program_id(1)
        heads = range(PAIR * gp)
        lanes = [slice(LANE * (h // PAIR), LANE * (h // PAIR + 1)) for h in heads]
        qps = [q_ref[:, lanes[h]] for h in heads]
        qTs = [jnp.where(_head_lanes(qps[h].shape, h % PAIR, 1), qps[h], jnp.zeros_like(qps[h])).T for h in heads]
        if use_f:
            qf = [t.astype(F32) for t in qTs]
            qnorm = [jnp.sqrt(jnp.sum(t * t, axis=0, keepdims=True)) for t in qf]
            qTs = [jnp.concatenate([qTs[h], _bias_rows(h % PAIR, bq)], axis=0) for h in heads]

        def blk(j, carry, masked):
            off = pl.multiple_of(j * bk, bk)
            kbs = [k_ref[pl.ds(off, bk), LANE * g:LANE * (g + 1)] for g in range(gp)]
            if use_f:
                kbs = [jnp.concatenate([kbs[g], b_ref[pl.ds(off, bk), LANE * g:LANE * (g + 1)]], axis=1)
                       for g in range(gp)]
            vTbs = [v_ref[pl.ds(off, bk), LANE * g:LANE * (g + 1)].T for g in range(gp)]
            sT = [_dot(kbs[h // PAIR], qTs[h]) * LOG2E for h in heads]
            if masked:
                valid = _valid_t(i, j, bq, bk, False)
                sT = [jnp.where(valid, s, NEG_BIG) for s in sT]
            cm = [jnp.max(s, axis=0, keepdims=True) for s in sT]
            if use_f:
                cm = [cm[h] + r_ref[h] for h in heads]
            m_new = [jnp.maximum(carry[h][0], cm[h]) for h in heads]
            shift = [(m_new[h] - r_ref[h]) if use_f else m_new[h] for h in heads]
            p = [jnp.exp2(sT[h] - shift[h]) for h in heads]
            a = [jnp.exp2(carry[h][0] - m_new[h]) for h in heads]
            l = [a[h] * carry[h][1] + jnp.sum(p[h], axis=0, keepdims=True) for h in heads]
            acc = [a[h] * carry[h][2] + _dot(vTbs[h // PAIR], p[h].astype(BF16)) for h in heads]
            return tuple((m_new[h], l[h], acc[h]) for h in heads)

        def step(jj, state):
            carry, first = state
            j = i - jj
            h0 = pl.program_id(0) * (PAIR * gp)
            bound = [LOG2E * (qnorm[h] * ks_ref[(h0 + h) * nkb + j] - ks_ref[(PAIR * P + h0 + h) * nkb + j])
                     + r_ref[h] - carry[h][0] for h in heads]
            live = jnp.max(functools.reduce(jnp.maximum, bound)) >= FOX_DEAD
            carry = lax.cond(live, lambda cr: blk(j, cr, False), lambda cr: cr, carry)
            return carry, jnp.where(live, j, first)

        carry = tuple((jnp.full((1, bq), NEG_BIG, F32), jnp.zeros((1, bq), F32), jnp.zeros((LANE, bq), F32))
                      for _ in heads)
        if causal:
            carry = blk(i, carry, True)
            carry, first = lax.fori_loop(1, i + 1, step, (carry, i))
            js_ref[0] = jnp.full((1, bq), first, jnp.int32)
        else:
            carry = lax.fori_loop(0, nkb, lambda j, c: blk(j, c, False), carry)
            if use_f:
                js_ref[0] = jnp.zeros((1, bq), jnp.int32)
        for h in heads:
            lse_ref[h] = carry[h][0] + jnp.log2(carry[h][1])
        for g in range(gp):
            o_ref[:, LANE * g:LANE * (g + 1)] = _merge_pair(
                [carry[h][2] / carry[h][1] for h in range(PAIR * g, PAIR * (g + 1))])

    qblk = pl.BlockSpec((bq, LANE * gp), lambda p, i: (i, p))
    kres = pl.BlockSpec((Sk, LANE * gp), lambda p, i: (0, p))
    stat = pl.BlockSpec((PAIR * gp, 1, bq), lambda p, i: (p, 0, i))
    in_specs = [qblk, kres, kres]
    args = [q2, k2, v2]
    out_specs = [qblk, stat]
    out_shape = [jax.ShapeDtypeStruct((S, C), F32), jax.ShapeDtypeStruct((PAIR * P, 1, S), F32)]
    if use_f:
        in_specs = [pl.BlockSpec(memory_space=pltpu.SMEM)] + in_specs + [kres, stat]
        args = [kstat] + args + [bias, r]
        out_specs.append(pl.BlockSpec((1, 1, bq), lambda p, i: (p, 0, i)))
        out_shape.append(jax.ShapeDtypeStruct((P // gp, 1, S), jnp.int32))
    return pl.pallas_call(
        body, name=name, grid=(P // gp, nq), in_specs=in_specs, out_specs=out_specs, out_shape=out_shape,
        compiler_params=_cp(("parallel", "arbitrary"), ATTN_VMEM_LIMIT),
    )(*args)


def _smp_bwd(q2, k2, v2, o2, lse, do2, bias, r, scale, causal, name, first=None):
    S, C = q2.shape
    Sk = k2.shape[0]
    bq, bk = _attn_blocks(S, Sk, FOX_BLOCK if bias is not None else None)
    nq, nkb, P = S // bq, Sk // bk, C // LANE
    use_f = bias is not None

    def body(*refs):
        if use_f:
            (first_ref, q_ref, k_ref, v_ref, o_ref, lse_ref, do_ref, b_ref, r_ref,
             dq_ref, dk_ref, dv_ref, dr_ref, dkey_ref, dk_acc, dv_acc, db_ref) = refs
        else:
            q_ref, k_ref, v_ref, o_ref, lse_ref, do_ref, dq_ref, dk_ref, dv_ref, dk_acc, dv_acc = refs
        i = pl.program_id(1)

        @pl.when(i == 0)
        def _():
            dk_acc[...] = jnp.zeros_like(dk_acc)
            dv_acc[...] = jnp.zeros_like(dv_acc)
            if use_f:
                db_ref[...] = jnp.zeros_like(db_ref)

        qp = q_ref[...]
        dof = do_ref[...]
        prod = dof * o_ref[...]
        heads = range(PAIR)
        mine = [_head_lanes(qp.shape, w, 1) for w in heads]
        qz = [jnp.where(mine[w], qp, jnp.zeros_like(qp)) for w in heads]
        qTs = [qz[w].T for w in heads]
        if use_f:
            qTs = [jnp.concatenate([qTs[w], _bias_rows(w, bq)], axis=0) for w in heads]
        doz = [jnp.where(mine[w], dof, 0.0).astype(BF16) for w in heads]
        doT = [doz[w].T for w in heads]
        delta = [jnp.sum(jnp.where(mine[w], prod, 0.0).T, axis=0, keepdims=True) for w in heads]
        shift = [(lse_ref[w] - r_ref[w]) if use_f else lse_ref[w] for w in heads]

        def blk(j, carry, masked):
            off = pl.multiple_of(j * bk, bk)
            kb = k_ref[pl.ds(off, bk), :]
            kTb = kb.T
            if use_f:
                kb = jnp.concatenate([kb, b_ref[pl.ds(off, bk), :]], axis=1)
            vb = v_ref[pl.ds(off, bk), :]
            sT = [_dot(kb, qTs[w]) * LOG2E for w in heads]
            if masked:
                valid = _valid_t(i, j, bq, bk, False)
                sT = [jnp.where(valid, s, NEG_BIG) for s in sT]
            p = [jnp.exp2(sT[w] - shift[w]) for w in heads]
            dp = [_dot(vb, doT[w]) for w in heads]
            ds = [p[w] * (dp[w] - delta[w]) for w in heads]
            dsb = [d.astype(BF16) for d in ds]
            dvs = [_dot(p[w].astype(BF16), doz[w]) for w in heads]
            dks = [_dot(dsb[w], qz[w]) for w in heads]
            dv_acc[pl.ds(off, bk), :] += dvs[0] + dvs[1]
            dk_acc[pl.ds(off, bk), :] += dks[0] + dks[1]
            dr = [carry[w][1] for w in heads]
            if use_f:
                dr = [dr[w] + jnp.sum(ds[w], axis=0, keepdims=True) for w in heads]
                lane = lax.broadcasted_iota(jnp.int32, (bk, LANE), 1)
                cols = [jnp.where(lane == w, jnp.sum(ds[w], axis=1, keepdims=True), 0.0) for w in heads]
                db_ref[pl.ds(off, bk), :] += cols[0] + cols[1]
            dq = [carry[w][0] + _dot(kTb, dsb[w]) for w in heads]
            return tuple((dq[w], dr[w]) for w in heads)

        carry = tuple((jnp.zeros((LANE, bq), F32), jnp.zeros((1, bq), F32)) for _ in heads)
        if causal:
            start = first_ref[pl.program_id(0) // (P // first.shape[0]), i]
            carry = lax.fori_loop(start, i, lambda j, c: blk(j, c, False), carry)
            carry = blk(i, carry, True)
        else:
            carry = lax.fori_loop(0, nkb, lambda j, c: blk(j, c, False), carry)
        if use_f:
            for w in heads:
                dr_ref[w] = carry[w][1]
        dq_ref[...] = (_merge_pair([carry[w][0] for w in heads]) * scale).astype(BF16)

        @pl.when(i == nq - 1)
        def _():
            dk_ref[...] = dk_acc[...].astype(BF16)
            dv_ref[...] = dv_acc[...].astype(BF16)

        if use_f:
            @pl.when(i == nq - 1)
            def _():
                def chunk(cidx, carry):
                    off = pl.multiple_of(cidx * LANE, LANE)
                    t = db_ref[pl.ds(off, LANE), :].T
                    for w in range(PAIR):
                        dkey_ref[w, :, pl.ds(off, LANE)] = t[w:w + 1, :]
                    return carry

                lax.fori_loop(0, Sk // LANE, chunk, 0)

    qblk = pl.BlockSpec((bq, LANE), lambda p, i: (i, p))
    kres = pl.BlockSpec((Sk, LANE), lambda p, i: (0, p))
    stat = pl.BlockSpec((PAIR, 1, bq), lambda p, i: (p, 0, i))
    in_specs = [qblk, kres, kres, qblk, stat, qblk]
    args = [q2, k2, v2, o2, lse, do2]
    out_specs = [qblk, kres, kres]
    out_shape = [jax.ShapeDtypeStruct((S, C), BF16), jax.ShapeDtypeStruct((Sk, C), BF16),
                 jax.ShapeDtypeStruct((Sk, C), BF16)]
    scratch = [pltpu.VMEM((Sk, LANE), F32), pltpu.VMEM((Sk, LANE), F32)]
    if use_f:
        in_specs = [pl.BlockSpec(memory_space=pltpu.SMEM)] + in_specs + [kres, stat]
        args = [first] + args + [bias, r]
        out_specs += [stat, pl.BlockSpec((PAIR, 1, Sk), lambda p, i: (p, 0, 0))]
        out_shape += [jax.ShapeDtypeStruct((PAIR * P, 1, S), F32), jax.ShapeDtypeStruct((PAIR * P, 1, Sk), F32)]
        scratch.append(pltpu.VMEM((Sk, LANE), F32))
    return pl.pallas_call(
        body, name=name, grid=(P, nq), in_specs=in_specs, out_specs=out_specs, out_shape=out_shape,
        scratch_shapes=scratch, compiler_params=_cp(("parallel", "arbitrary"), ATTN_VMEM_LIMIT),
    )(*args)


def _sbp_fwd(q2, k2, v2, name):
    S, C = q2.shape
    bq, bk = _attn_blocks(S, S, SB_BLOCK)
    assert bq == bk
    nq, P = S // bq, C // LANE
    gp = FWD_PAIRS if P % FWD_PAIRS == 0 else 1
    c = min(CUMSUM_CHUNK, bk)

    def body(q_ref, k_ref, v_ref, o_ref, lt_ref, js_ref):
        i = pl.program_id(1)
        after = _tri2(c, lambda s, j: j > s)
        heads = range(PAIR * gp)
        qps = [q_ref[:, LANE * (h // PAIR):LANE * (h // PAIR + 1)] for h in heads]
        qTs = [jnp.where(_head_lanes(qps[h].shape, h % PAIR, 1), qps[h], jnp.zeros_like(qps[h])).T for h in heads]

        def blk(jj, carry, masked):
            j = i - jj
            off = pl.multiple_of(j * bk, bk)
            kbs = [k_ref[pl.ds(off, bk), LANE * g:LANE * (g + 1)] for g in range(gp)]
            vTbs = [v_ref[pl.ds(off, bk), LANE * g:LANE * (g + 1)].T for g in range(gp)]
            logs = [_sb_logs(_dot(kbs[h // PAIR], qTs[h])) for h in heads]
            ls, lk = [t[0] for t in logs], [t[1] for t in logs]
            if masked:
                valid = _valid_t(i, j, bq, bk, True)
                lk = [jnp.where(valid, t, 0.0) for t in lk]
            cs = [_key_cumsum(lk[h], after, True, carry[h][0]) for h in heads]
            wgt = [jnp.exp(ls[h] + cs[h][0]) for h in heads]
            if masked:
                wgt = [jnp.where(valid, t, 0.0) for t in wgt]
            acc = [carry[h][1] + _dot(vTbs[h // PAIR], wgt[h].astype(BF16)) for h in heads]
            return tuple((carry[h][0] + cs[h][1], acc[h]) for h in heads)

        def step(jj, state):
            carry, first = state
            live = jnp.max(functools.reduce(jnp.maximum, [carry[h][0] for h in heads])) >= SB_DEAD
            carry = lax.cond(live, lambda cr: blk(jj, cr, False), lambda cr: cr, carry)
            return carry, jnp.where(live, i - jj, first)

        carry = tuple((jnp.zeros((1, bq), F32), jnp.zeros((LANE, bq), F32)) for _ in heads)
        carry = blk(0, carry, True)
        carry, first = lax.fori_loop(1, i + 1, step, (carry, i))
        js_ref[0] = jnp.full((1, bq), first, jnp.int32)
        for h in heads:
            lt_ref[h] = carry[h][0]
        for g in range(gp):
            o_ref[:, LANE * g:LANE * (g + 1)] = _merge_pair([carry[h][1] for h in range(PAIR * g, PAIR * (g + 1))])

    qblk = pl.BlockSpec((bq, LANE * gp), lambda p, i: (i, p))
    kres = pl.BlockSpec((S, LANE * gp), lambda p, i: (0, p))
    stat = pl.BlockSpec((PAIR * gp, 1, bq), lambda p, i: (p, 0, i))
    return pl.pallas_call(
        body, name=name, grid=(P // gp, nq),
        in_specs=[qblk, kres, kres],
        out_specs=[qblk, stat, pl.BlockSpec((1, 1, bq), lambda p, i: (p, 0, i))],
        out_shape=[jax.ShapeDtypeStruct((S, C), F32), jax.ShapeDtypeStruct((PAIR * P, 1, S), F32),
                   jax.ShapeDtypeStruct((P // gp, 1, S), jnp.int32)],
        compiler_params=_cp(("parallel", "arbitrary"), ATTN_VMEM_LIMIT),
    )(q2, k2, v2)


def _sbp_bwd(q2, k2, v2, lt, first, do2, scale, name):
    S, C = q2.shape
    bq, bk = _attn_blocks(S, S, SB_BLOCK)
    nq, P = S // bq, C // LANE
    c = min(CUMSUM_CHUNK, bk)

    per_group = P // first.shape[0]

    def body(first_ref, q_ref, k_ref, v_ref, lt_ref, do_ref, dq_ref, dk_ref, dv_ref, dk_acc, dv_acc):
        i = pl.program_id(1)

        @pl.when(i == 0)
        def _():
            dk_acc[...] = jnp.zeros_like(dk_acc)
            dv_acc[...] = jnp.zeros_like(dv_acc)

        qp = q_ref[...]
        dof = do_ref[...]
        upto = _tri2(c, lambda s, j: j <= s)
        before = _tri2(c, lambda s, j: j < s)
        heads = range(PAIR)
        mine = [_head_lanes(qp.shape, w, 1) for w in heads]
        qz = [jnp.where(mine[w], qp, jnp.zeros_like(qp)) for w in heads]
        qTs = [qz[w].T for w in heads]
        doz = [jnp.where(mine[w], dof, 0.0).astype(BF16) for w in heads]
        doT = [doz[w].T for w in heads]
        ltot = [lt_ref[w] for w in heads]

        def blk(j, carry, masked):
            off = pl.multiple_of(j * bk, bk)
            kb = k_ref[pl.ds(off, bk), :]
            vb = v_ref[pl.ds(off, bk), :]
            kTb = kb.T
            logs = [_sb_logs(_dot(kb, qTs[w])) for w in heads]
            ls, lk = [t[0] for t in logs], [t[1] for t in logs]
            if masked:
                valid = _valid_t(i, j, bq, bk, True)
                lk = [jnp.where(valid, t, 0.0) for t in lk]
            pin = [_key_cumsum(lk[w], upto, False, carry[w][1] - ltot[w]) for w in heads]
            wgt = [jnp.exp(ls[w] - pin[w][0]) for w in heads]
            if masked:
                wgt = [jnp.where(valid, t, 0.0) for t in wgt]
            g = [_dot(vb, doT[w]) * wgt[w] for w in heads]
            cin = [_key_cumsum(g[w], before, False, carry[w][2]) for w in heads]
            sig = [jnp.exp(t) for t in ls]
            dz = [g[w] * (1.0 - sig[w]) - cin[w][0] * sig[w] for w in heads]
            if masked:
                dz = [jnp.where(valid, t, 0.0) for t in dz]
            dzb = [t.astype(BF16) for t in dz]
            dvs = [_dot(wgt[w].astype(BF16), doz[w]) for w in heads]
            dks = [_dot(dzb[w], qz[w]) for w in heads]
            dv_acc[pl.ds(off, bk), :] += dvs[0] + dvs[1]
            dk_acc[pl.ds(off, bk), :] += dks[0] + dks[1]
            return tuple((carry[w][0] + _dot(kTb, dzb[w]), carry[w][1] + pin[w][1], carry[w][2] + cin[w][1])
                         for w in heads)

        carry = tuple((jnp.zeros((LANE, bq), F32), jnp.zeros((1, bq), F32), jnp.zeros((1, bq), F32)) for _ in heads)
        start = first_ref[pl.program_id(0) // per_group, i]
        carry = lax.fori_loop(start, i, lambda j, cr: blk(j, cr, False), carry)
        carry = blk(i, carry, True)
        dq_ref[...] = (_merge_pair([carry[w][0] for w in heads]) * scale).astype(BF16)

        @pl.when(i == nq - 1)
        def _():
            dk_ref[...] = dk_acc[...].astype(BF16)
            dv_ref[...] = dv_acc[...].astype(BF16)

    qblk = pl.BlockSpec((bq, LANE), lambda p, i: (i, p))
    kres = pl.BlockSpec((S, LANE), lambda p, i: (0, p))
    stat = pl.BlockSpec((PAIR, 1, bq), lambda p, i: (p, 0, i))
    return pl.pallas_call(
        body, name=name, grid=(P, nq),
        in_specs=[pl.BlockSpec(memory_space=pltpu.SMEM), qblk, kres, kres, stat, qblk],
        out_specs=[qblk, kres, kres],
        out_shape=[jax.ShapeDtypeStruct((S, C), BF16)] * 3,
        scratch_shapes=[pltpu.VMEM((S, LANE), F32), pltpu.VMEM((S, LANE), F32)],
        compiler_params=_cp(("parallel", "arbitrary"), ATTN_VMEM_LIMIT),
    )(first, q2, k2, v2, lt, do2)


def _bias_cols(f_cum):
    H, Sk = f_cum.shape
    terms = jnp.stack(_split3(f_cum), axis=-1)
    packed = terms.reshape(H // PAIR, PAIR, Sk, 3).transpose(2, 0, 1, 3).reshape(Sk, H // PAIR, PAIR * 3)
    return jnp.pad(packed, ((0, 0), (0, 0), (0, LANE - PAIR * 3))).reshape(Sk, -1)


def _make_packed_softmax(name, scale, causal, use_f):
    assert _pow2(scale)

    def run_fwd(q16, k16, v16, f_cum):
        q16 = q16 * scale
        if not use_f:
            o, lse = _smp_fwd(q16, k16, v16, None, None, causal, name + "_fwd")
            return o, (q16, k16, v16, o, lse, None, None, None)
        bq, bk = _attn_blocks(q16.shape[0], k16.shape[0], FOX_BLOCK)
        n_heads = f_cum.shape[0]
        knorm = jnp.sqrt(jnp.sum(jnp.square(k16.astype(F32)).reshape(-1, bk, n_heads, HEAD_DIM), axis=3))
        kstat = jnp.concatenate([jnp.max(knorm, axis=1).T.reshape(-1), f_cum[:, bk - 1::bk].reshape(-1)])
        bias, r = _bias_cols(f_cum), (f_cum * LOG2E)[:, None, :]
        o, lse, first = _smp_fwd(q16, k16, v16, bias, r, causal, name + "_fwd", lax.stop_gradient(kstat))
        return o, (q16, k16, v16, o, lse, bias, r, first[:, 0, ::bq])

    def run_bwd(saved, do):
        q16, k16, v16, o, lse, bias, r, first = saved
        outs = _smp_bwd(q16, k16, v16, o, lse, do, bias, r, scale, causal, name + "_bwd", first)
        if use_f:
            return outs[0], outs[1], outs[2], outs[3][:, 0, :] - outs[4][:, 0, :]
        return tuple(outs)

    if use_f:
        @jax.custom_vjp
        def attn(q, k, v, f_cum):
            return run_fwd(q, k, v, f_cum)[0]

        attn.defvjp(run_fwd, run_bwd)
    else:
        @jax.custom_vjp
        def attn(q, k, v):
            return run_fwd(q, k, v, None)[0]

        attn.defvjp(lambda q, k, v: run_fwd(q, k, v, None), run_bwd)
    return attn


def _make_packed_sb(name, scale):
    assert _pow2(scale)

    def run_fwd(q16, k16, v16):
        q16 = q16 * scale
        o, lt, first = _sbp_fwd(q16, k16, v16, name + "_fwd")
        bq, _ = _attn_blocks(q16.shape[0], q16.shape[0], SB_BLOCK)
        return o, (q16, k16, v16, lt, first[:, 0, ::bq])

    def run_bwd(saved, do):
        q16, k16, v16, lt, first = saved
        return tuple(_sbp_bwd(q16, k16, v16, lt, first, do, scale, name + "_bwd"))

    @jax.custom_vjp
    def attn(q, k, v):
        return run_fwd(q, k, v)[0]

    attn.defvjp(run_fwd, run_bwd)
    return attn


def _round_bf16(x):
    return lax.reduce_precision(x, exponent_bits=8, mantissa_bits=7)


def _split3(x):
    hi = _round_bf16(x)
    mid = _round_bf16(x - hi)
    lo = _round_bf16(x - hi - mid)
    return hi.astype(BF16), mid.astype(BF16), lo.astype(BF16)


def _pow2(x):
    m, _ = math.frexp(x)
    return m == 0.5


def _pad_last(x, n):
    return jnp.pad(x, [(0, 0)] * (x.ndim - 1) + [(0, n - x.shape[-1])])


def _layouts(q, k, scale):
    qh = _pad_last(jnp.transpose(q * scale if _pow2(scale) else q, (1, 0, 2)).astype(BF16), LANE)
    return qh, _pad_last(jnp.transpose(k, (1, 0, 2)).astype(BF16), LANE)


def _make_softmax_attn(name, scale, causal, d):
    pre = _pow2(scale)
    cmul = LOG2E if pre else scale * LOG2E
    gscale = 1.0 if pre else scale

    def run_fwd(q, k, v):
        qn, kn = _layouts(q, k, scale)
        vn = jnp.transpose(v, (1, 0, 2)).astype(BF16)
        oT, lse = _sm_fwd_t(qn, kn, jnp.transpose(vn, (0, 2, 1)), cmul, causal, name + "_fwd")
        return jnp.transpose(oT, (2, 0, 1)), (qn, kn, vn, oT, lse)

    def run_bwd(saved, dout):
        qn, kn, vn, oT, lse = saved
        doT = jnp.transpose(dout, (1, 2, 0))
        do = jnp.transpose(dout, (1, 0, 2)).astype(BF16)
        dq, dk, dv = _sm_bwd_t(qn, kn, vn, oT, lse, doT, do, cmul, gscale, causal, name + "_bwd")
        dq = jnp.transpose(dq[:, :, :d], (1, 0, 2))
        if pre:
            dq = dq * scale
        return dq, jnp.transpose(dk[:, :, :d], (1, 0, 2)), jnp.transpose(dv, (1, 0, 2))

    @jax.custom_vjp
    def attn(q, k, v):
        return run_fwd(q, k, v)[0]

    attn.defvjp(run_fwd, run_bwd)
    return attn


def _rope(x, positions):
    half = x.shape[-1] // 2
    inv_freq = ROPE_THETA ** (-jnp.arange(half, dtype=F32) / half)
    ang = positions.astype(F32)[:, None] * inv_freq[None, :]
    ang = ang.reshape((ang.shape[0],) + (1,) * (x.ndim - 2) + (half,))
    cos, sin = jnp.cos(ang), jnp.sin(ang)
    x1, x2 = x[..., :half], x[..., half:]
    return jnp.concatenate([x1 * cos - x2 * sin, x1 * sin + x2 * cos], axis=-1)


def _permute_cols(w):
    parts = [w[..., _ORIG_OFF[idx]:_ORIG_OFF[idx] + SPLIT_SIZES[idx]] for _, idx in _PERM]
    pad = jnp.zeros(w.shape[:-1] + (PROJ_COLS - IN_COLS,), w.dtype)
    return jnp.concatenate(parts + [pad], axis=-1)


def _unpermute_cols(w):
    start, parts = 0, [None] * len(SPLIT_SIZES)
    for _, idx in _PERM:
        parts[idx] = w[..., start:start + SPLIT_SIZES[idx]]
        start += SPLIT_SIZES[idx]
    return jnp.concatenate(parts, axis=-1)


_BF16_PIECES = ("fq", "fk", "fv", "sq", "sk", "sv", "mq")


def _make_ln_proj(name, has_res):
    wide = [(n, SPLIT_SIZES[idx]) for n, idx in _PERM if SPLIT_SIZES[idx] % LANE == 0]
    narrow = [(n, SPLIT_SIZES[idx]) for n, idx in _PERM if SPLIT_SIZES[idx] % LANE]
    assert [n for n, _ in wide + narrow] == [n for n, _ in _PERM]
    tail = PROJ_COLS - sum(w for _, w in wide)

    def run_fwd(x, res, g, b, w):
        h, h16, hT16 = _ln_fwd_call(x, res, g, b, name + "_ln_fwd", also16=True)
        w16 = w.astype(BF16)
        outs = _matmul_split(h16, w16, [w for _, w in wide] + [tail],
                             [BF16 if n in _BF16_PIECES else F32 for n, _ in wide] + [F32], name + "_fwd")
        pieces, off = list(outs[:-1]), 0
        for _, width in narrow:
            pieces.append(outs[-1][:, off:off + width])
            off += width
        return (h, tuple(pieces)), (x, res, g, hT16, w16)

    def run_bwd(saved, cts):
        x, res, g, hT16, w16 = saved
        dh, dpieces = cts
        wide = [c for c in dpieces if c.shape[1] % LANE == 0]
        narrow = [c.astype(BF16) for c in dpieces if c.shape[1] % LANE]
        pad = jnp.zeros((x.shape[0], PROJ_COLS - IN_COLS), BF16)
        da, dy16 = _concat_matmul_nt(wide + [jnp.concatenate(narrow + [pad], axis=1)], w16, name + "_dx")
        dw = _matmul(hT16, dy16, "nn", name + "_dw")
        outs = _ln_bwd_call(dh, x, res, g, name + "_ln_bwd", dy2=da)
        if has_res:
            dx, dr, dg, db = outs
            return dx, dr, dg.reshape(-1), db.reshape(-1), dw
        dx, dg, db = outs
        return dx, dg.reshape(-1), db.reshape(-1), dw

    if has_res:
        @jax.custom_vjp
        def op(x, res, g, b, w):
            return run_fwd(x, res, g, b, w)[0]

        op.defvjp(run_fwd, run_bwd)
    else:
        @jax.custom_vjp
        def op(x, g, b, w):
            return run_fwd(x, None, g, b, w)[0]

        op.defvjp(lambda x, g, b, w: run_fwd(x, None, g, b, w), run_bwd)

    def call(*args):
        h, pieces = op(*args)
        return h, {n: part for (n, _), part in zip(_PERM, pieces)}

    return call


def _trunk_loss(wts, x2d, mem2d, target2d):
    s = x2d.shape[0]
    positions = jnp.arange(s)
    head_scale = HEAD_DIM ** -0.5
    mla_scale = (MLA_NOPE + MLA_ROPE) ** -0.5

    mem_n = _make_ln("ln_mem", False)(mem2d, wts["mem_ln_g"], wts["mem_ln_b"])
    h, y = None, x2d
    for l in range(DEPTH):
        tag = f"l{l}_"
        w_p = wts["w_in"][l]
        if l == 0:
            h, p = _make_ln_proj(tag + "proj", False)(y, wts["ln_in_g"], wts["ln_in_b"], w_p)
        else:
            h, p = _make_ln_proj(tag + "proj", True)(y, h, wts["ln_g"][l - 1], wts["ln_b"][l - 1], w_p)

        log_f = jax.nn.log_sigmoid(p["f_logit"].T + wts["b_forget"][l][:, None])
        f_cum = jnp.cumsum(log_f, axis=1)
        out_fox = _make_packed_softmax(tag + "fox", head_scale, True, True)(p["fq"], p["fk"], p["fv"], f_cum)

        out_sb = _make_packed_sb(tag + "sb", head_scale)(p["sq"], p["sk"], p["sv"])

        cqn = _make_rms(tag + "rms_q")(p["c_q"], wts["mla_q_norm_g"][l])
        q_mla = _make_mm(tag + "q_up")(cqn, wts["w_mla_q_up"][l]).reshape(s, N_HEADS, MLA_NOPE + MLA_ROPE)
        ckvn = _make_rms(tag + "rms_kv")(p["c_kv"], wts["mla_kv_norm_g"][l])
        kv_mla = _make_mm(tag + "kv_up")(ckvn, wts["w_mla_kv_up"][l]).reshape(s, N_HEADS, MLA_NOPE + MLA_V)
        q_full = jnp.concatenate([q_mla[..., :MLA_NOPE], _rope(q_mla[..., MLA_NOPE:], positions)], axis=-1)
        k_rope = jnp.broadcast_to(_rope(p["k_rot"], positions)[:, None, :], (s, N_HEADS, MLA_ROPE))
        k_full = jnp.concatenate([kv_mla[..., :MLA_NOPE], k_rope], axis=-1)
        out_mla = _make_softmax_attn(tag + "mla", mla_scale, True, MLA_NOPE + MLA_ROPE)(
            q_full, k_full, kv_mla[..., MLA_NOPE:]).reshape(s, GROUP_W)

        mkv = _make_mm(tag + "mem_kv")(mem_n, wts["w_mem_kv"][l])
        out_mem = _make_packed_softmax(tag + "mem", head_scale, False, False)(
            p["mq"], mkv[:, :GROUP_W].astype(BF16), mkv[:, GROUP_W:].astype(BF16))

        y = _make_gate_out(tag + "out")((out_fox, out_sb, out_mla, out_mem), p["gate"], wts["w_out"][l])

    h = _make_ln(f"l{DEPTH - 1}_ln", True)(y, h, wts["ln_g"][DEPTH - 1], wts["ln_b"][DEPTH - 1])
    return _loss_op(h, target2d)


def _mesh_pos():
    x, y, c = (lax.axis_index(a) for a in MESH_AXES)
    return x, y, c, 4 * x + 2 * y + c


def _peer(x, y, c, mask):
    return (x ^ ((mask >> 2) & 1), y ^ ((mask >> 1) & 1), c ^ (mask & 1))


_ANY = pl.BlockSpec(memory_space=pl.ANY)


def _all_gather(row_shards, stack_shards):
    n_row, n_all = len(row_shards), len(row_shards) + len(stack_shards)
    shards = list(row_shards) + list(stack_shards)
    chip_masks = (4, 2, 6)
    tensors = range(n_all)

    def body(*refs):
        ins, outs = refs[:n_all], refs[n_all:2 * n_all]
        send_sems, recv_sems, local_sems = refs[2 * n_all:]
        x, y, c, me = _mesh_pos()
        sibling = _peer(x, y, c, 1)

        def window(t, slot):
            if t < n_row:
                rows = shards[t].shape[1]
                return outs[t].at[:, pl.ds(slot * rows, rows), :]
            return outs[t].at[slot]

        def copy(t, k, slot, to, src=None):
            return pltpu.make_async_remote_copy(
                src_ref=window(t, slot) if src is None else src, dst_ref=window(t, slot),
                send_sem=send_sems.at[t, k], recv_sem=recv_sems.at[t, k], device_id=to,
                device_id_type=pl.DeviceIdType.MESH)

        local = [pltpu.make_async_copy(ins[t], window(t, me), local_sems.at[t]) for t in tensors]
        for cp in local:
            cp.start()
        first = [copy(t, 0, me, sibling, src=ins[t]) for t in tensors]
        first += [copy(t, 1 + j, me, _peer(x, y, c, m), src=ins[t]) for j, m in enumerate(chip_masks) for t in tensors]
        for cp in first:
            cp.start()
        passed = []
        for j, m in enumerate(chip_masks):
            for t in tensors:
                copy(t, 1 + j, me ^ m, sibling).wait_recv()
            for t in tensors:
                cp = copy(t, 4 + j, me ^ m, sibling)
                cp.start()
                passed.append(cp)
        for t in tensors:
            copy(t, 0, me ^ 1, sibling).wait_recv()
        for j, m in enumerate(chip_masks):
            for t in tensors:
                copy(t, 4 + j, me ^ m ^ 1, sibling).wait_recv()
        for cp in first + passed:
            cp.wait_send()
        for cp in local:
            cp.wait()

    out_shape = [jax.ShapeDtypeStruct((a.shape[0], N_DEV * a.shape[1], a.shape[2]), a.dtype) for a in row_shards]
    out_shape += [jax.ShapeDtypeStruct((N_DEV,) + a.shape, a.dtype) for a in stack_shards]
    return pl.pallas_call(
        body, name="all_gather_weights", in_specs=[_ANY] * n_all, out_specs=[_ANY] * n_all, out_shape=out_shape,
        scratch_shapes=[pltpu.SemaphoreType.DMA((n_all, N_DEV - 1)), pltpu.SemaphoreType.DMA((n_all, N_DEV - 1)),
                        pltpu.SemaphoreType.DMA((n_all,))],
    )(*shards)


def _reduce_scatter(row_full, stack_full, bcast):
    n_row, n_stack = len(row_full), len(stack_full)
    n_all = n_row + n_stack + len(bcast)
    fulls = list(row_full) + list(stack_full) + list(bcast)

    def body(*refs):
        ins, outs = refs[:n_all], refs[n_all:2 * n_all]
        send_sems, recv_sems, local_sems = refs[2 * n_all:]
        x, y, c, me = _mesh_pos()

        def part(t, slot):
            if t < n_row:
                rows = fulls[t].shape[1] // N_DEV
                return ins[t].at[:, pl.ds(slot * rows, rows), :]
            if t < n_row + n_stack:
                return ins[t].at[slot]
            return ins[t]

        local = [pltpu.make_async_copy(part(t, me), outs[t].at[me], local_sems.at[t]) for t in range(n_all)]
        for cp in local:
            cp.start()
        sends = []
        for mask in range(1, N_DEV):
            for t in range(n_all):
                cp = pltpu.make_async_remote_copy(
                    src_ref=part(t, me ^ mask), dst_ref=outs[t].at[me], send_sem=send_sems.at[t, mask - 1],
                    recv_sem=recv_sems.at[t, mask - 1], device_id=_peer(x, y, c, mask),
                    device_id_type=pl.DeviceIdType.MESH)
                cp.start()
                sends.append(cp)
        for mask in range(1, N_DEV):
            for t in range(n_all):
                pltpu.make_async_remote_copy(
                    src_ref=part(t, me), dst_ref=outs[t].at[me ^ mask], send_sem=send_sems.at[t, mask - 1],
                    recv_sem=recv_sems.at[t, mask - 1], device_id=_peer(x, y, c, mask),
                    device_id_type=pl.DeviceIdType.MESH).wait_recv()
        for cp in sends:
            cp.wait_send()
        for cp in local:
            cp.wait()

    out_shape = [jax.ShapeDtypeStruct((N_DEV, a.shape[0], a.shape[1] // N_DEV, a.shape[2]), a.dtype) for a in row_full]
    out_shape += [jax.ShapeDtypeStruct(a.shape, a.dtype) for a in stack_full]
    out_shape += [jax.ShapeDtypeStruct((N_DEV,) + a.shape, a.dtype) for a in bcast]
    return pl.pallas_call(
        body, name="reduce_scatter_grads", in_specs=[_ANY] * n_all, out_specs=[_ANY] * n_all, out_shape=out_shape,
        scratch_shapes=[pltpu.SemaphoreType.DMA((n_all, N_DEV - 1)), pltpu.SemaphoreType.DMA((n_all, N_DEV - 1)),
                        pltpu.SemaphoreType.DMA((n_all,))],
    )(*fulls)


def _adamw(slots, w, m, v, name):
    shape = w.shape
    cols = shape[-1]
    rows = math.prod(shape[:-1])
    tr = _pick(rows, (64, 32, 16, 8))
    c1 = 1.0 - ADAM_B1 ** ADAM_STEP
    c2 = 1.0 - ADAM_B2 ** ADAM_STEP

    def body(s_ref, w_ref, m_ref, v_ref, g_ref, d_ref, nm_ref, nv_ref):
        g = s_ref[0].astype(F32)
        for k in range(1, N_DEV):
            g = g + s_ref[k].astype(F32)
        nm = ADAM_B1 * m_ref[...] + (1.0 - ADAM_B1) * g
        nv = ADAM_B2 * v_ref[...] + (1.0 - ADAM_B2) * (g * g)
        g_ref[...] = g
        nm_ref[...] = nm
        nv_ref[...] = nv
        d_ref[...] = -ADAM_LR * ((nm / c1) / (jnp.sqrt(nv / c2) + ADAM_EPS) + ADAM_WD * w_ref[...])

    row = pl.BlockSpec((tr, cols), lambda i: (i, 0))
    out = jax.ShapeDtypeStruct((rows, cols), F32)
    outs = pl.pallas_call(
        body, name=name, grid=(rows // tr,),
        in_specs=[pl.BlockSpec((N_DEV, tr, cols), lambda i: (0, i, 0)), row, row, row],
        out_specs=[row] * 4, out_shape=[out] * 4, compiler_params=_cp(("parallel",)),
    )(slots.reshape(N_DEV, rows, cols), w.reshape(rows, cols), m.reshape(rows, cols), v.reshape(rows, cols))
    return [o.reshape(shape) for o in outs]


_SMALL = ("ln_in_g", "ln_in_b", "mem_ln_g", "mem_ln_b", "b_forget", "mla_q_norm_g", "mla_kv_norm_g", "ln_g", "ln_b")
_ORDER = ("ln_in_g", "ln_in_b", "mem_ln_g", "mem_ln_b", "w_in", "b_forget", "mla_q_norm_g", "w_mla_q_up",
          "mla_kv_norm_g", "w_mla_kv_up", "w_mem_kv", "w_out", "ln_g", "ln_b")


def _pack_small(d):
    flat = jnp.concatenate([d[n].reshape(-1) for n in _SMALL])
    n = flat.shape[0]
    padded = ((n + 8 * LANE - 1) // (8 * LANE)) * (8 * LANE)
    return jnp.pad(flat, (0, padded - n)).reshape(-1, LANE)


def _unpack_small(packed, like):
    flat, out, off = packed.reshape(-1), {}, 0
    for n in _SMALL:
        size = math.prod(like[n].shape)
        out[n] = flat[off:off + size].reshape(like[n].shape)
        off += size
    return out


def _unstack_cols(g):
    n, l, r, c = g.shape
    return g.transpose(1, 2, 0, 3).reshape(l, r, n * c)


def _stack_cols(g):
    l, r, nc = g.shape
    return g.reshape(l, r, N_DEV, nc // N_DEV).transpose(2, 0, 1, 3)


def kernel(x, mem, ln_in_g, ln_in_b, mem_ln_g, mem_ln_b, w_in, b_forget, mla_q_norm_g, w_mla_q_up, mla_kv_norm_g, w_mla_kv_up, w_mem_kv, w_out, ln_g, ln_b, loss_target, m_ln_in_g, m_ln_in_b, m_mem_ln_g, m_mem_ln_b, m_w_in, m_b_forget, m_mla_q_norm_g, m_w_mla_q_up, m_mla_kv_norm_g, m_w_mla_kv_up, m_w_mem_kv, m_w_out, m_ln_g, m_ln_b, v_ln_in_g, v_ln_in_b, v_mem_ln_g, v_mem_ln_b, v_w_in, v_b_forget, v_mla_q_norm_g, v_w_mla_q_up, v_mla_kv_norm_g, v_w_mla_kv_up, v_w_mem_kv, v_w_out, v_ln_g, v_ln_b):
    w_shard = dict(ln_in_g=ln_in_g, ln_in_b=ln_in_b, mem_ln_g=mem_ln_g, mem_ln_b=mem_ln_b, w_in=w_in,
                   b_forget=b_forget, mla_q_norm_g=mla_q_norm_g, w_mla_q_up=w_mla_q_up,
                   mla_kv_norm_g=mla_kv_norm_g, w_mla_kv_up=w_mla_kv_up, w_mem_kv=w_mem_kv, w_out=w_out,
                   ln_g=ln_g, ln_b=ln_b)
    m_shard = dict(ln_in_g=m_ln_in_g, ln_in_b=m_ln_in_b, mem_ln_g=m_mem_ln_g, mem_ln_b=m_mem_ln_b, w_in=m_w_in,
                   b_forget=m_b_forget, mla_q_norm_g=m_mla_q_norm_g, w_mla_q_up=m_w_mla_q_up,
                   mla_kv_norm_g=m_mla_kv_norm_g, w_mla_kv_up=m_w_mla_kv_up, w_mem_kv=m_w_mem_kv, w_out=m_w_out,
                   ln_g=m_ln_g, ln_b=m_ln_b)
    v_shard = dict(ln_in_g=v_ln_in_g, ln_in_b=v_ln_in_b, mem_ln_g=v_mem_ln_g, mem_ln_b=v_mem_ln_b, w_in=v_w_in,
                   b_forget=v_b_forget, mla_q_norm_g=v_mla_q_norm_g, w_mla_q_up=v_w_mla_q_up,
                   mla_kv_norm_g=v_mla_kv_norm_g, w_mla_kv_up=v_w_mla_kv_up, w_mem_kv=v_w_mem_kv, w_out=v_w_out,
                   ln_g=v_ln_g, ln_b=v_ln_b)

    to16 = lambda ws: [a.astype(BF16) for a in ws]
    gathered = _all_gather(to16([_permute_cols(w_in), w_mem_kv, w_out]), to16([w_mla_q_up, w_mla_kv_up]))
    g_in, g_mem, g_out, g_qup, g_kvup = [a.astype(F32) for a in gathered]
    full = dict(w_shard)
    full.update(w_in=g_in, w_mem_kv=g_mem, w_out=g_out, w_mla_q_up=_unstack_cols(g_qup),
                w_mla_kv_up=_unstack_cols(g_kvup))

    loss_local, (grad_w, grad_x) = jax.value_and_grad(_trunk_loss, argnums=(0, 1))(
        full, x[0], mem[0], loss_target[0])

    s_in, s_mem, s_out, s_qup, s_kvup, s_small = _reduce_scatter(
        to16([grad_w["w_in"], grad_w["w_mem_kv"], grad_w["w_out"]]),
        to16([_stack_cols(grad_w["w_mla_q_up"]), _stack_cols(grad_w["w_mla_kv_up"])]),
        [_pack_small(grad_w)])

    res = {}
    for name, slots in (("w_mem_kv", s_mem), ("w_out", s_out), ("w_mla_q_up", s_qup), ("w_mla_kv_up", s_kvup)):
        res[name] = _adamw(slots, w_shard[name], m_shard[name], v_shard[name], "adamw_" + name)
    res["w_in"] = [_unpermute_cols(a) for a in _adamw(
        s_in, _permute_cols(w_in), _permute_cols(m_w_in), _permute_cols(v_w_in), "adamw_w_in")]
    small = _adamw(s_small, _pack_small(w_shard), _pack_small(m_shard), _pack_small(v_shard), "adamw_small")
    small = [_unpack_small(a, w_shard) for a in small]
    for name in _SMALL:
        res[name] = [a[name] for a in small]

    loss = lax.psum(loss_local, MESH_AXES)
    outs = [loss, grad_x[None]]
    for k in range(4):
        outs += [res[name][k] for name in _ORDER]
    return tuple(outs)
```

```python
import functools
import math

import jax
import jax.numpy as jnp
from jax import lax
from jax.experimental import pallas as pl
from jax.experimental.pallas import tpu as pltpu

F32 = jnp.float32
BF16 = jnp.bfloat16

D_MODEL = 1024
DEPTH = 2
GROUP_W = 256
N_HEADS = 4
HEAD_DIM = 64
MLA_Q_RANK = 256
MLA_KV_RANK = 128
MLA_NOPE = 64
MLA_ROPE = 32
MLA_V = 64
ROPE_THETA = 10000.0
LN_EPS = 1e-5
RMS_EPS = 1e-6
DEEPNORM_ALPHA = (2 * DEPTH) ** 0.25
SPLIT_SIZES = (256, 256, 256, 4, 256, 256, 256, 256, 128, 32, 256, 1024)
IN_COLS = sum(SPLIT_SIZES)
_ORIG_OFF = [sum(SPLIT_SIZES[:i]) for i in range(len(SPLIT_SIZES))]
_PERM = (("fq", 0), ("fk", 1), ("fv", 2), ("sq", 4), ("sk", 5), ("sv", 6), ("c_q", 7), ("c_kv", 8),
         ("mq", 10), ("gate", 11), ("k_rot", 9), ("f_logit", 3))
LANE = 128
PROJ_COLS = ((IN_COLS + LANE - 1) // LANE) * LANE

ADAM_LR = 0.001
ADAM_B1 = 0.9
ADAM_B2 = 0.999
ADAM_EPS = 1e-08
ADAM_WD = 0.01
ADAM_STEP = 10

N_DEV = 8
MESH_AXES = ("x", "y", "c")
VMEM_LIMIT = 48 * 1024 * 1024
ATTN_VMEM_LIMIT = 56 * 1024 * 1024
ATTN_BQ = 512
ATTN_BK = 512
CUMSUM_CHUNK = 256
NEG_BIG = -1e30
LOG2E = math.log2(math.e)
MM_TM, MM_TN, MM_TK, MM_TK_NT = 1024, 1664, 1024, 3328
CONCAT_MM_TM = 512

_NT = (((1,), (1,)), ((), ()))
_NN = (((1,), (0,)), ((), ()))


def _cp(sem, vmem=VMEM_LIMIT):
    return pltpu.CompilerParams(dimension_semantics=sem, vmem_limit_bytes=vmem)


def _dot(a, b, dims=_NN):
    return lax.dot_general(a, b, dims, preferred_element_type=F32)


def _pick(n, cands):
    for c in cands:
        if c <= n and n % c == 0:
            return c
    return n


def _tile(n, cap):
    if n <= cap:
        return n
    best = None
    for d in range(LANE, cap + 1, LANE):
        if n % d == 0:
            best = d
    assert best is not None, (n, cap)
    return best


def _matmul(a, b, mode, name):
    if mode == "nn":
        (M, K), (K2, N) = a.shape, b.shape
    else:
        (M, K), (N, K2) = a.shape, b.shape
    assert K == K2 and a.dtype == BF16 and b.dtype == BF16, (a.shape, b.shape, mode)
    tm, tn = _tile(M, MM_TM), _tile(N, MM_TN)
    tk = _tile(K, MM_TK if mode == "nn" else MM_TK_NT)
    nk = K // tk
    dims = _NN if mode == "nn" else _NT

    def body(a_ref, b_ref, o_ref, acc_ref):
        part = _dot(a_ref[...], b_ref[...], dims)
        if nk == 1:
            o_ref[...] = part
        else:
            k = pl.program_id(2)

            @pl.when(k == 0)
            def _():
                acc_ref[...] = part

            @pl.when(k > 0)
            def _():
                acc_ref[...] += part

            @pl.when(k == nk - 1)
            def _():
                o_ref[...] = acc_ref[...]

    a_spec = pl.BlockSpec((tm, tk), lambda j, i, k: (i, k))
    if mode == "nn":
        b_spec = pl.BlockSpec((tk, tn), lambda j, i, k: (k, j))
    else:
        b_spec = pl.BlockSpec((tn, tk), lambda j, i, k: (j, k))
    acc_shape = (tm, tn) if nk > 1 else (8, LANE)
    o_spec = pl.BlockSpec((tm, tn), lambda j, i, k: (i, j))
    return pl.pallas_call(
        body, name=name, grid=(N // tn, M // tm, nk),
        in_specs=[a_spec, b_spec], out_specs=o_spec, out_shape=jax.ShapeDtypeStruct((M, N), F32),
        scratch_shapes=[pltpu.VMEM(acc_shape, F32)],
        compiler_params=_cp(("parallel", "parallel", "arbitrary")),
    )(a, b)


def _concat_matmul_nt(pieces, b, name):
    M, (N, K) = pieces[0].shape[0], b.shape
    widths = [p.shape[1] for p in pieces]
    assert sum(widths) == K and all(w % LANE == 0 for w in widths) and b.dtype == BF16, (widths, b.shape)
    tm = _tile(M, CONCAT_MM_TM)
    n = len(pieces)

    def body(*refs):
        b_ref, o_ref, a_ref = refs[n:]
        a_ref[...] = jnp.concatenate([r[...].astype(BF16) for r in refs[:n]], axis=1)
        o_ref[...] = _dot(a_ref[...], b_ref[...], _NT)

    rows = lambda w: pl.BlockSpec((tm, w), lambda i: (i, 0))
    return pl.pallas_call(
        body, name=name, grid=(M // tm,),
        in_specs=[rows(w) for w in widths] + [pl.BlockSpec((N, K), lambda i: (0, 0))],
        out_specs=[rows(N), rows(K)],
        out_shape=[jax.ShapeDtypeStruct((M, N), F32), jax.ShapeDtypeStruct((M, K), BF16)],
        compiler_params=_cp(("parallel",)),
    )(*pieces, b)


def _matmul_split(a, b, widths, dtypes, name):
    (M, K), (K2, N) = a.shape, b.shape
    assert K == K2 and sum(widths) == N and all(w % LANE == 0 for w in widths), (a.shape, b.shape, widths)
    assert a.dtype == BF16 and b.dtype == BF16
    tm = _tile(M, CONCAT_MM_TM)
    offs = [sum(widths[:r]) for r in range(len(widths))]

    def body(a_ref, b_ref, *o_refs):
        av = a_ref[...]
        for o_ref, off, w in zip(o_refs, offs, widths):
            o_ref[...] = _dot(av, b_ref[:, off:off + w]).astype(o_ref.dtype)

    rows = lambda w: pl.BlockSpec((tm, w), lambda i: (i, 0))
    return pl.pallas_call(
        body, name=name, grid=(M // tm,),
        in_specs=[rows(K), pl.BlockSpec((K, N), lambda i: (0, 0))],
        out_specs=[rows(w) for w in widths],
        out_shape=[jax.ShapeDtypeStruct((M, w), d) for w, d in zip(widths, dtypes)],
        compiler_params=_cp(("parallel",)),
    )(a, b)


def _make_mm(name):
    @jax.custom_vjp
    def mm(a, w):
        return _matmul(a.astype(BF16), w.astype(BF16), "nn", name + "_fwd")

    def fwd(a, w):
        a16, w16 = a.astype(BF16), w.astype(BF16)
        return _matmul(a16, w16, "nn", name + "_fwd"), (a16, w16)

    def bwd(res, dy):
        a16, w16 = res
        dy16 = dy.astype(BF16)
        da = _matmul(dy16, w16, "nt", name + "_dx")
        dw = _matmul(a16.T, dy16, "nn", name + "_dw")
        return da, dw

    mm.defvjp(fwd, bwd)
    return mm


def _row_tile(rows):
    return _pick(rows, (512, 256, 128, 64, 32, 16, 8))


def _ln_stats(u):
    mu = jnp.mean(u, axis=-1, keepdims=True)
    d = u - mu
    var = jnp.mean(d * d, axis=-1, keepdims=True)
    return d, lax.rsqrt(var + LN_EPS)


def _ln_fwd_call(x, res, g, b, name, also16=False):
    rows, dm = x.shape
    tr = _row_tile(rows)
    has_res = res is not None
    n_in = 2 if has_res else 1

    def body(*refs):
        if has_res:
            u = DEEPNORM_ALPHA * refs[1][...] + refs[0][...]
        else:
            u = refs[0][...]
        g_ref, b_ref = refs[n_in], refs[n_in + 1]
        d, rstd = _ln_stats(u)
        y = d * rstd * g_ref[...] + b_ref[...]
        refs[n_in + 2][...] = y
        if also16:
            y16 = y.astype(BF16)
            refs[n_in + 3][...] = y16
            refs[n_in + 4][...] = y16.T

    row = pl.BlockSpec((tr, dm), lambda i: (i, 0))
    vec = pl.BlockSpec((1, dm), lambda i: (0, 0))
    args = (x, res) if has_res else (x,)
    out_specs, out_shape = [row], [jax.ShapeDtypeStruct((rows, dm), F32)]
    if also16:
        out_specs += [row, pl.BlockSpec((dm, tr), lambda i: (0, i))]
        out_shape += [jax.ShapeDtypeStruct((rows, dm), BF16), jax.ShapeDtypeStruct((dm, rows), BF16)]
    outs = pl.pallas_call(
        body, name=name, grid=(rows // tr,),
        in_specs=[row] * n_in + [vec, vec], out_specs=out_specs, out_shape=out_shape,
        compiler_params=_cp(("parallel",)),
    )(*args, g.reshape(1, dm), b.reshape(1, dm))
    return outs if also16 else outs[0]


def _ln_bwd_call(dy, x, res, g, name, dy2=None):
    rows, dm = x.shape
    tr = _row_tile(rows)
    has_res = res is not None
    two = dy2 is not None

    def body(*refs):
        dy_ref, refs = refs[0], refs[1:]
        if two:
            dy2_ref, refs = refs[0], refs[1:]
        if has_res:
            x_ref, r_ref, g_ref, dx_ref, dr_ref, dg_ref, db_ref = refs
            u = DEEPNORM_ALPHA * r_ref[...] + x_ref[...]
        else:
            x_ref, g_ref, dx_ref, dg_ref, db_ref = refs
            u = x_ref[...]
        i = pl.program_id(0)
        d, rstd = _ln_stats(u)
        xhat = d * rstd
        dyv = dy_ref[...] + dy2_ref[...] if two else dy_ref[...]
        dxh = dyv * g_ref[...]
        m1 = jnp.mean(dxh, axis=-1, keepdims=True)
        m2 = jnp.mean(dxh * xhat, axis=-1, keepdims=True)
        du = rstd * (dxh - m1 - xhat * m2)
        dx_ref[...] = du
        if has_res:
            dr_ref[...] = DEEPNORM_ALPHA * du
        pg = jnp.sum(dyv * xhat, axis=0, keepdims=True)
        pb = jnp.sum(dyv, axis=0, keepdims=True)

        @pl.when(i == 0)
        def _():
            dg_ref[...] = pg
            db_ref[...] = pb

        @pl.when(i > 0)
        def _():
            dg_ref[...] += pg
            db_ref[...] += pb

    row = pl.BlockSpec((tr, dm), lambda i: (i, 0))
    vec = pl.BlockSpec((1, dm), lambda i: (0, 0))
    big = jax.ShapeDtypeStruct((rows, dm), F32)
    small = jax.ShapeDtypeStruct((1, dm), F32)
    args = ((dy, dy2) if two else (dy,)) + ((x, res) if has_res else (x,))
    n_big = 2 if has_res else 1
    outs = pl.pallas_call(
        body, name=name, grid=(rows // tr,),
        in_specs=[row] * len(args) + [vec],
        out_specs=[row] * n_big + [vec, vec],
        out_shape=[big] * n_big + [small, small],
        compiler_params=_cp(("arbitrary",)),
    )(*args, g.reshape(1, dm))
    return outs


def _make_ln(name, has_res):
    if has_res:
        @jax.custom_vjp
        def ln(x, res, g, b):
            return _ln_fwd_call(x, res, g, b, name + "_fwd")

        def fwd(x, res, g, b):
            return ln(x, res, g, b), (x, res, g)

        def bwd(saved, dy):
            x, res, g = saved
            dx, dr, dg, db = _ln_bwd_call(dy, x, res, g, name + "_bwd")
            return dx, dr, dg.reshape(-1), db.reshape(-1)
    else:
        @jax.custom_vjp
        def ln(x, g, b):
            return _ln_fwd_call(x, None, g, b, name + "_fwd")

        def fwd(x, g, b):
            return ln(x, g, b), (x, g)

        def bwd(saved, dy):
            x, g = saved
            dx, dg, db = _ln_bwd_call(dy, x, None, g, name + "_bwd")
            return dx, dg.reshape(-1), db.reshape(-1)

    ln.defvjp(fwd, bwd)
    return ln


def _rms_fwd_call(x, g, name):
    rows, dm = x.shape
    tr = _row_tile(rows)

    def body(x_ref, g_ref, o_ref):
        xv = x_ref[...]
        rstd = lax.rsqrt(jnp.mean(xv * xv, axis=-1, keepdims=True) + RMS_EPS)
        o_ref[...] = xv * rstd * g_ref[...]

    row = pl.BlockSpec((tr, dm), lambda i: (i, 0))
    vec = pl.BlockSpec((1, dm), lambda i: (0, 0))
    return pl.pallas_call(
        body, name=name, grid=(rows // tr,), in_specs=[row, vec], out_specs=row,
        out_shape=jax.ShapeDtypeStruct((rows, dm), F32), compiler_params=_cp(("parallel",)),
    )(x, g.reshape(1, dm))


def _rms_bwd_call(dy, x, g, name):
    rows, dm = x.shape
    tr = _row_tile(rows)

    def body(dy_ref, x_ref, g_ref, dx_ref, dg_ref):
        i = pl.program_id(0)
        xv = x_ref[...]
        dyv = dy_ref[...]
        rstd = lax.rsqrt(jnp.mean(xv * xv, axis=-1, keepdims=True) + RMS_EPS)
        xhat = xv * rstd
        dxh = dyv * g_ref[...]
        m2 = jnp.mean(dxh * xhat, axis=-1, keepdims=True)
        dx_ref[...] = rstd * (dxh - xhat * m2)
        pg = jnp.sum(dyv * xhat, axis=0, keepdims=True)

        @pl.when(i == 0)
        def _():
            dg_ref[...] = pg

        @pl.when(i > 0)
        def _():
            dg_ref[...] += pg

    row = pl.BlockSpec((tr, dm), lambda i: (i, 0))
    vec = pl.BlockSpec((1, dm), lambda i: (0, 0))
    return pl.pallas_call(
        body, name=name, grid=(rows // tr,), in_specs=[row, row, vec], out_specs=[row, vec],
        out_shape=[jax.ShapeDtypeStruct((rows, dm), F32), jax.ShapeDtypeStruct((1, dm), F32)],
        compiler_params=_cp(("arbitrary",)),
    )(dy, x, g.reshape(1, dm))


def _make_rms(name):
    @jax.custom_vjp
    def rms(x, g):
        return _rms_fwd_call(x, g, name + "_fwd")

    def fwd(x, g):
        return rms(x, g), (x, g)

    def bwd(saved, dy):
        x, g = saved
        dx, dg = _rms_bwd_call(dy, x, g, name + "_bwd")
        return dx, dg.reshape(-1)

    rms.defvjp(fwd, bwd)
    return rms


def _sigmoid(x):
    return 1.0 / (1.0 + jnp.exp(-x))


def _gate_fwd_call(parts, gate, w16, name):
    rows, dm = gate.shape
    tr = _row_tile(rows)
    n = len(parts)
    n_out = w16.shape[1]
    assert w16.shape[0] == dm and w16.dtype == BF16

    def body(*refs):
        g_ref, w_ref, o_ref, gT_ref = refs[n:]
        gv = g_ref[...]
        mixed = jnp.concatenate([r[...] for r in refs[:n]], axis=1)
        y16 = (mixed * (gv * _sigmoid(gv))).astype(BF16)
        gT_ref[...] = y16.T
        o_ref[...] = _dot(y16, w_ref[...])

    row = pl.BlockSpec((tr, dm), lambda i: (i, 0))
    part_specs = [pl.BlockSpec((tr, p.shape[1]), lambda i: (i, 0)) for p in parts]
    return pl.pallas_call(
        body, name=name, grid=(rows // tr,), in_specs=part_specs + [row, pl.BlockSpec(w16.shape, lambda i: (0, 0))],
        out_specs=[pl.BlockSpec((tr, n_out), lambda i: (i, 0)), pl.BlockSpec((dm, tr), lambda i: (0, i))],
        out_shape=[jax.ShapeDtypeStruct((rows, n_out), F32), jax.ShapeDtypeStruct((dm, rows), BF16)],
        compiler_params=_cp(("parallel",)),
    )(*parts, gate, w16)


def _gate_bwd_call(dy16, w16, parts, gate, name):
    rows, dm = gate.shape
    tr = _row_tile(rows)
    n = len(parts)
    widths = [p.shape[1] for p in parts]
    assert w16.shape == (dm, dy16.shape[1]) and dy16.dtype == BF16 and w16.dtype == BF16

    def body(*refs):
        dy_ref, w_ref, g_ref = refs[0], refs[1], refs[n + 2]
        dm_refs, dg_ref = refs[n + 3:2 * n + 3], refs[2 * n + 3]
        gv = g_ref[...]
        dyv = _dot(dy_ref[...], w_ref[...], _NT)
        sg = _sigmoid(gv)
        mixed = jnp.concatenate([r[...] for r in refs[2:n + 2]], axis=1)
        dmixed = dyv * (gv * sg)
        off = 0
        for r, w in zip(dm_refs, widths):
            r[...] = dmixed[:, off:off + w]
            off += w
        dg_ref[...] = dyv * mixed * (sg * (1.0 + gv * (1.0 - sg)))

    row = pl.BlockSpec((tr, dm), lambda i: (i, 0))
    part_specs = [pl.BlockSpec((tr, w), lambda i: (i, 0)) for w in widths]
    dy_spec = pl.BlockSpec((tr, dy16.shape[1]), lambda i: (i, 0))
    w_spec = pl.BlockSpec(w16.shape, lambda i: (0, 0))
    return pl.pallas_call(
        body, name=name, grid=(rows // tr,), in_specs=[dy_spec, w_spec] + part_specs + [row],
        out_specs=part_specs + [row],
        out_shape=[jax.ShapeDtypeStruct((rows, w), F32) for w in widths] + [jax.ShapeDtypeStruct((rows, dm), F32)],
        compiler_params=_cp(("parallel",)),
    )(dy16, w16, *parts, gate)


def _make_gate_out(name):
    def run_fwd(parts, gate, w):
        w16 = w.astype(BF16)
        y, gT16 = _gate_fwd_call(parts, gate, w16, name + "_gate_fwd")
        return y, (parts, gate, gT16, w16)

    def run_bwd(saved, dy):
        parts, gate, gT16, w16 = saved
        dy16 = dy.astype(BF16)
        *dparts, dgate = _gate_bwd_call(dy16, w16, parts, gate, name + "_gate_bwd")
        return tuple(dparts), dgate, _matmul(gT16, dy16, "nn", name + "_dw")

    @jax.custom_vjp
    def gate_out(parts, gate, w):
        return run_fwd(parts, gate, w)[0]

    gate_out.defvjp(run_fwd, run_bwd)
    return gate_out


def _loss_call(y, t, name):
    rows, dm = y.shape
    tr = _row_tile(rows)

    def body(y_ref, t_ref, l_ref, d_ref):
        i = pl.program_id(0)
        e = y_ref[...] - t_ref[...]
        d_ref[...] = e * (1.0 / dm)
        part = 0.5 * jnp.sum(jnp.mean(e * e, axis=-1, keepdims=True), axis=0, keepdims=True)

        @pl.when(i == 0)
        def _():
            l_ref[...] = part

        @pl.when(i > 0)
        def _():
            l_ref[...] += part

    row = pl.BlockSpec((tr, dm), lambda i: (i, 0))
    one = pl.BlockSpec((1, 1), lambda i: (0, 0))
    return pl.pallas_call(
        body, name=name, grid=(rows // tr,), in_specs=[row, row], out_specs=[one, row],
        out_shape=[jax.ShapeDtypeStruct((1, 1), F32), jax.ShapeDtypeStruct((rows, dm), F32)],
        compiler_params=_cp(("arbitrary",)),
    )(y, t)


@jax.custom_vjp
def _loss_op(y, t):
    return _loss_call(y, t, "loss_head")[0][0, 0]


def _loss_fwd(y, t):
    l, d = _loss_call(y, t, "loss_head")
    return l[0, 0], d


def _loss_bwd(d, ct):
    return ct * d, jnp.zeros_like(d)


_loss_op.defvjp(_loss_fwd, _loss_bwd)


def _attn_blocks(S, Sk, cap=None):
    bq, bk = min(cap or ATTN_BQ, S), min(cap or ATTN_BK, Sk)
    assert S % bq == 0 and Sk % bk == 0
    return bq, bk


def _valid_t(i, j, bq, bk, strict):
    key = j * bk + lax.broadcasted_iota(jnp.int32, (bk, bq), 0)
    qry = i * bq + lax.broadcasted_iota(jnp.int32, (bk, bq), 1)
    return (key < qry) if strict else (key <= qry)


def _sm_fwd_t(qn, k, vT, cmul, causal, name):
    H, S, DK = qn.shape
    Sk, dv = k.shape[1], vT.shape[1]
    bq, bk = _attn_blocks(S, Sk)
    nq, nkb = S // bq, Sk // bk
    hb = PAIR * FWD_PAIRS if H % (PAIR * FWD_PAIRS) == 0 else 1
    heads = range(hb)
    if causal:
        assert S == Sk and bq == bk

    def body(qn_ref, k_ref, vT_ref, oT_ref, lse_ref):
        i = pl.program_id(1)
        qTs = [qn_ref[w].T for w in heads]

        def blk(j, carry, masked):
            off = pl.multiple_of(j * bk, bk)
            sT = [_dot(k_ref[w, pl.ds(off, bk), :], qTs[w]) * cmul for w in heads]
            if masked:
                valid = _valid_t(i, j, bq, bk, False)
                sT = [jnp.where(valid, s, NEG_BIG) for s in sT]
            m_new = [jnp.maximum(carry[w][0], jnp.max(sT[w], axis=0, keepdims=True)) for w in heads]
            p = [jnp.exp2(sT[w] - m_new[w]) for w in heads]
            a = [jnp.exp2(carry[w][0] - m_new[w]) for w in heads]
            l = [a[w] * carry[w][1] + jnp.sum(p[w], axis=0, keepdims=True) for w in heads]
            acc = [a[w] * carry[w][2] + _dot(vT_ref[w, :, pl.ds(off, bk)], p[w].astype(BF16)) for w in heads]
            return tuple((m_new[w], l[w], acc[w]) for w in heads)

        carry = tuple((jnp.full((1, bq), NEG_BIG, F32), jnp.zeros((1, bq), F32), jnp.zeros((dv, bq), F32))
                      for _ in heads)
        if causal:
            carry = lax.fori_loop(0, i, lambda j, c: blk(j, c, False), carry)
            carry = blk(i, carry, True)
        else:
            carry = lax.fori_loop(0, nkb, lambda j, c: blk(j, c, False), carry)
        for w in heads:
            oT_ref[w] = carry[w][2] / carry[w][1]
            lse_ref[w] = carry[w][0] + jnp.log2(carry[w][1])

    qcol = lambda d: pl.BlockSpec((hb, d, bq), lambda h, i: (h, 0, i))
    return pl.pallas_call(
        body, name=name, grid=(H // hb, nq),
        in_specs=[pl.BlockSpec((hb, bq, DK), lambda h, i: (h, i, 0)), pl.BlockSpec((hb, Sk, DK), lambda h, i: (h, 0, 0)),
                  pl.BlockSpec((hb, dv, Sk), lambda h, i: (h, 0, 0))],
        out_specs=[qcol(dv), qcol(1)],
        out_shape=[jax.ShapeDtypeStruct((H, dv, S), F32), jax.ShapeDtypeStruct((H, 1, S), F32)],
        compiler_params=_cp(("parallel", "arbitrary"), ATTN_VMEM_LIMIT),
    )(qn, k, vT)


def _sm_bwd_t(qn, k, v, oT, lse, doT, do, cmul, gscale, causal, name):
    H, S, DK = qn.shape
    Sk, dv = k.shape[1], v.shape[2]
    bq, bk = _attn_blocks(S, Sk)
    nq, nkb = S // bq, Sk // bk

    def body(qn_ref, k_ref, v_ref, oT_ref, lse_ref, doT_ref, do_ref, dq_ref, dk_ref, dv_ref):
        i = pl.program_id(1)

        @pl.when(i == 0)
        def _():
            dk_ref[...] = jnp.zeros_like(dk_ref)
            dv_ref[...] = jnp.zeros_like(dv_ref)

        qnb = qn_ref[...]
        qTb = qnb.T
        dob = do_ref[...]
        doTf = doT_ref[...]
        doTb = doTf.astype(BF16)
        delta = jnp.sum(doTf * oT_ref[...], axis=0, keepdims=True)
        lse = lse_ref[...]

        def blk(j, dq, masked):
            off = pl.multiple_of(j * bk, bk)
            kb = k_ref[pl.ds(off, bk), :]
            sT = _dot(kb, qTb) * cmul
            if masked:
                sT = jnp.where(_valid_t(i, j, bq, bk, False), sT, NEG_BIG)
            p = jnp.exp2(sT - lse)
            dp = _dot(v_ref[pl.ds(off, bk), :], doTb)
            ds = p * (dp - delta)
            dsb = (ds * gscale).astype(BF16) if gscale != 1.0 else ds.astype(BF16)
            dv_ref[pl.ds(off, bk), :] += _dot(p.astype(BF16), dob)
            dk_ref[pl.ds(off, bk), :] += _dot(dsb, qnb)
            return dq + _dot(kb.T, dsb)

        dq = jnp.zeros((DK, bq), F32)
        if causal:
            dq = lax.fori_loop(0, i, lambda j, c: blk(j, c, False), dq)
            dq = blk(i, dq, True)
        else:
            dq = lax.fori_loop(0, nkb, lambda j, c: blk(j, c, False), dq)
        dq_ref[...] = dq.T

    qcol = lambda d: pl.BlockSpec((None, d, bq), lambda h, i: (h, 0, i))
    qrow = lambda d: pl.BlockSpec((None, bq, d), lambda h, i: (h, i, 0))
    krow = lambda d: pl.BlockSpec((None, Sk, d), lambda h, i: (h, 0, 0))
    return pl.pallas_call(
        body, name=name, grid=(H, nq),
        in_specs=[qrow(DK), krow(DK), krow(dv), qcol(dv), qcol(1), qcol(dv), qrow(dv)],
        out_specs=[qrow(DK), krow(DK), krow(dv)],
        out_shape=[jax.ShapeDtypeStruct((H, S, DK), F32), jax.ShapeDtypeStruct((H, Sk, DK), F32),
                   jax.ShapeDtypeStruct((H, Sk, dv), F32)],
        compiler_params=_cp(("parallel", "arbitrary"), ATTN_VMEM_LIMIT),
    )(qn, k, v, oT, lse, doT, do)


def _tri(n, fn):
    r = lax.broadcasted_iota(jnp.int32, (n, n), 0)
    c = lax.broadcasted_iota(jnp.int32, (n, n), 1)
    return jnp.where(fn(r, c), 1.0, 0.0).astype(BF16)


def _key_cumsum(x, tri2, suffix, base):
    bk = x.shape[0]
    c = min(CUMSUM_CHUNK, bk)
    n = bk // c
    hi32 = lax.bitcast_convert_type(lax.bitcast_convert_type(x, jnp.int32) & jnp.int32(-65536), F32)
    hi = hi32.astype(BF16)
    lo = (x - hi32).astype(BF16)
    tot = [jnp.sum(x[a * c:(a + 1) * c], axis=0, keepdims=True) for a in range(n)]
    outs = []
    for a in range(n):
        row = base
        for t in (tot[a + 1:] if suffix else tot[:a]):
            row = row + t
        stacked = jnp.concatenate([hi[a * c:(a + 1) * c], lo[a * c:(a + 1) * c]], axis=0)
        outs.append(_dot(tri2, stacked) + row)
    total = tot[0]
    for t in tot[1:]:
        total = total + t
    return (outs[0] if n == 1 else jnp.concatenate(outs, axis=0)), total


def _tri2(n, fn):
    t = _tri(n, fn)
    return jnp.concatenate([t, t], axis=1)


def _sb_logs(z):
    neg_abs = lax.bitcast_convert_type(lax.bitcast_convert_type(z, jnp.int32) | jnp.int32(-2 ** 31), F32)
    ls = jnp.minimum(z, 0.0) - jnp.log(1.0 + jnp.exp(neg_abs))
    return ls, ls - z


PAIR = LANE // HEAD_DIM
FWD_PAIRS = 2
SB_DEAD = -110.0
SB_BLOCK = 256
FOX_DEAD = -160.0
FOX_BLOCK = 512


def _head_lanes(shape, w, axis):
    idx = lax.broadcasted_iota(jnp.int32, shape, axis)
    return (idx >= HEAD_DIM * w) & (idx < HEAD_DIM * (w + 1))


def _bias_rows(w, bq):
    row = lax.broadcasted_iota(jnp.int32, (LANE, bq), 0)
    return jnp.where((row >= 3 * w) & (row < 3 * w + 3), -1.0, 0.0).astype(BF16)


def _merge_pair(parts):
    return jnp.where(_head_lanes(parts[0].shape, 0, 0), parts[0], parts[1]).T


def _smp_fwd(q2, k2, v2, bias, r, causal, name, kstat=None):
    S, C = q2.shape
    Sk = k2.shape[0]
    bq, bk = _attn_blocks(S, Sk, FOX_BLOCK if bias is not None else None)
    nq, nkb, P = S // bq, Sk // bk, C // LANE
    gp = FWD_PAIRS if P % FWD_PAIRS == 0 else 1
    use_f = bias is not None
    if causal:
        assert S == Sk and bq == bk and use_f

    def body(*refs):
        if use_f:
            ks_ref, q_ref, k_ref, v_ref, b_ref, r_ref, o_ref, lse_ref, js_ref = refs
        else:
            q_ref, k_ref, v_ref, o_ref, lse_ref = refs
        i = pl.program_id(1)
        heads = range(PAIR * gp)
        lanes = [slice(LANE * (h // PAIR), LANE * (h // PAIR + 1)) for h in heads]
        qps = [q_ref[:, lanes[h]] for h in heads]
        qTs = [jnp.where(_head_lanes(qps[h].shape, h % PAIR, 1), qps[h], jnp.zeros_like(qps[h])).T for h in heads]
        if use_f:
            qf = [t.astype(F32) for t in qTs]
            qnorm = [jnp.sqrt(jnp.sum(t * t, axis=0, keepdims=True)) for t in qf]
            qTs = [jnp.concatenate([qTs[h], _bias_rows(h % PAIR, bq)], axis=0) for h in heads]

        def blk(j, carry, masked):
            off = pl.multiple_of(j * bk, bk)
            kbs = [k_ref[pl.ds(off, bk), LANE * g:LANE * (g + 1)] for g in range(gp)]
            if use_f:
                kbs = [jnp.concatenate([kbs[g], b_ref[pl.ds(off, bk), LANE * g:LANE * (g + 1)]], axis=1)
                       for g in range(gp)]
            vTbs = [v_ref[pl.ds(off, bk), LANE * g:LANE * (g + 1)].T for g in range(gp)]
            sT = [_dot(kbs[h // PAIR], qTs[h]) * LOG2E for h in heads]
            if masked:
                valid = _valid_t(i, j, bq, bk, False)
                sT = [jnp.where(valid, s, NEG_BIG) for s in sT]
            cm = [jnp.max(s, axis=0, keepdims=True) for s in sT]
            if use_f:
                cm = [cm[h] + r_ref[h] for h in heads]
            m_new = [jnp.maximum(carry[h][0], cm[h]) for h in heads]
            shift = [(m_new[h] - r_ref[h]) if use_f else m_new[h] for h in heads]
            p = [jnp.exp2(sT[h] - shift[h]) for h in heads]
            a = [jnp.exp2(carry[h][0] - m_new[h]) for h in heads]
            l = [a[h] * carry[h][1] + jnp.sum(p[h], axis=0, keepdims=True) for h in heads]
            acc = [a[h] * carry[h][2] + _dot(vTbs[h // PAIR], p[h].astype(BF16)) for h in heads]
            return tuple((m_new[h], l[h], acc[h]) for h in heads)

        def step(jj, state):
            carry, first = state
            j = i - jj
            h0 = pl.program_id(0) * (PAIR * gp)
            bound = [LOG2E * (qnorm[h] * ks_ref[(h0 + h) * nkb + j] - ks_ref[(PAIR * P + h0 + h) * nkb + j])
                     + r_ref[h] - carry[h][0] for h in heads]
            live = jnp.max(functools.reduce(jnp.maximum, bound)) >= FOX_DEAD
            carry = lax.cond(live, lambda cr: blk(j, cr, False), lambda cr: cr, carry)
            return carry, jnp.where(live, j, first)

        carry = tuple((jnp.full((1, bq), NEG_BIG, F32), jnp.zeros((1, bq), F32), jnp.zeros((LANE, bq), F32))
                      for _ in heads)
        if causal:
            carry = blk(i, carry, True)
            carry, first = lax.fori_loop(1, i + 1, step, (carry, i))
            js_ref[0] = jnp.full((1, bq), first, jnp.int32)
        else:
            carry = lax.fori_loop(0, nkb, lambda j, c: blk(j, c, False), carry)
            if use_f:
                js_ref[0] = jnp.zeros((1, bq), jnp.int32)
        for h in heads:
            lse_ref[h] = carry[h][0] + jnp.log2(carry[h][1])
        for g in range(gp):
            o_ref[:, LANE * g:LANE * (g + 1)] = _merge_pair(
                [carry[h][2] / carry[h][1] for h in range(PAIR * g, PAIR * (g + 1))])

    qblk = pl.BlockSpec((bq, LANE * gp), lambda p, i: (i, p))
    kres = pl.BlockSpec((Sk, LANE * gp), lambda p, i: (0, p))
    stat = pl.BlockSpec((PAIR * gp, 1, bq), lambda p, i: (p, 0, i))
    in_specs = [qblk, kres, kres]
    args = [q2, k2, v2]
    out_specs = [qblk, stat]
    out_shape = [jax.ShapeDtypeStruct((S, C), F32), jax.ShapeDtypeStruct((PAIR * P, 1, S), F32)]
    if use_f:
        in_specs = [pl.BlockSpec(memory_space=pltpu.SMEM)] + in_specs + [kres, stat]
        args = [kstat] + args + [bias, r]
        out_specs.append(pl.BlockSpec((1, 1, bq), lambda p, i: (p, 0, i)))
        out_shape.append(jax.ShapeDtypeStruct((P // gp, 1, S), jnp.int32))
    return pl.pallas_call(
        body, name=name, grid=(P // gp, nq), in_specs=in_specs, out_specs=out_specs, out_shape=out_shape,
        compiler_params=_cp(("parallel", "arbitrary"), ATTN_VMEM_LIMIT),
    )(*args)


def _smp_bwd(q2, k2, v2, o2, lse, do2, bias, r, scale, causal, name, first=None):
    S, C = q2.shape
    Sk = k2.shape[0]
    bq, bk = _attn_blocks(S, Sk, FOX_BLOCK if bias is not None else None)
    nq, nkb, P = S // bq, Sk // bk, C // LANE
    use_f = bias is not None

    def body(*refs):
        if use_f:
            (first_ref, q_ref, k_ref, v_ref, o_ref, lse_ref, do_ref, b_ref, r_ref,
             dq_ref, dk_ref, dv_ref, dr_ref, dkey_ref, dk_acc, dv_acc, db_ref) = refs
        else:
            q_ref, k_ref, v_ref, o_ref, lse_ref, do_ref, dq_ref, dk_ref, dv_ref, dk_acc, dv_acc = refs
        i = pl.program_id(1)

        @pl.when(i == 0)
        def _():
            dk_acc[...] = jnp.zeros_like(dk_acc)
            dv_acc[...] = jnp.zeros_like(dv_acc)
            if use_f:
                db_ref[...] = jnp.zeros_like(db_ref)

        qp = q_ref[...]
        dof = do_ref[...]
        prod = dof * o_ref[...]
        heads = range(PAIR)
        mine = [_head_lanes(qp.shape, w, 1) for w in heads]
        qz = [jnp.where(mine[w], qp, jnp.zeros_like(qp)) for w in heads]
        qTs = [qz[w].T for w in heads]
        if use_f:
            qTs = [jnp.concatenate([qTs[w], _bias_rows(w, bq)], axis=0) for w in heads]
        doz = [jnp.where(mine[w], dof, 0.0).astype(BF16) for w in heads]
        doT = [doz[w].T for w in heads]
        delta = [jnp.sum(jnp.where(mine[w], prod, 0.0).T, axis=0, keepdims=True) for w in heads]
        shift = [(lse_ref[w] - r_ref[w]) if use_f else lse_ref[w] for w in heads]

        def blk(j, carry, masked):
            off = pl.multiple_of(j * bk, bk)
            kb = k_ref[pl.ds(off, bk), :]
            kTb = kb.T
            if use_f:
                kb = jnp.concatenate([kb, b_ref[pl.ds(off, bk), :]], axis=1)
            vb = v_ref[pl.ds(off, bk), :]
            sT = [_dot(kb, qTs[w]) * LOG2E for w in heads]
            if masked:
                valid = _valid_t(i, j, bq, bk, False)
                sT = [jnp.where(valid, s, NEG_BIG) for s in sT]
            p = [jnp.exp2(sT[w] - shift[w]) for w in heads]
            dp = [_dot(vb, doT[w]) for w in heads]
            ds = [p[w] * (dp[w] - delta[w]) for w in heads]
            dsb = [d.astype(BF16) for d in ds]
            dvs = [_dot(p[w].astype(BF16), doz[w]) for w in heads]
            dks = [_dot(dsb[w], qz[w]) for w in heads]
            dv_acc[pl.ds(off, bk), :] += dvs[0] + dvs[1]
            dk_acc[pl.ds(off, bk), :] += dks[0] + dks[1]
            dr = [carry[w][1] for w in heads]
            if use_f:
                dr = [dr[w] + jnp.sum(ds[w], axis=0, keepdims=True) for w in heads]
                lane = lax.broadcasted_iota(jnp.int32, (bk, LANE), 1)
                cols = [jnp.where(lane == w, jnp.sum(ds[w], axis=1, keepdims=True), 0.0) for w in heads]
                db_ref[pl.ds(off, bk), :] += cols[0] + cols[1]
            dq = [carry[w][0] + _dot(kTb, dsb[w]) for w in heads]
            return tuple((dq[w], dr[w]) for w in heads)

        carry = tuple((jnp.zeros((LANE, bq), F32), jnp.zeros((1, bq), F32)) for _ in heads)
        if causal:
            start = first_ref[pl.program_id(0) // (P // first.shape[0]), i]
            carry = lax.fori_loop(start, i, lambda j, c: blk(j, c, False), carry)
            carry = blk(i, carry, True)
        else:
            carry = lax.fori_loop(0, nkb, lambda j, c: blk(j, c, False), carry)
        if use_f:
            for w in heads:
                dr_ref[w] = carry[w][1]
        dq_ref[...] = (_merge_pair([carry[w][0] for w in heads]) * scale).astype(BF16)

        @pl.when(i == nq - 1)
        def _():
            dk_ref[...] = dk_acc[...].astype(BF16)
            dv_ref[...] = dv_acc[...].astype(BF16)

        if use_f:
            @pl.when(i == nq - 1)
            def _():
                def chunk(cidx, carry):
                    off = pl.multiple_of(cidx * LANE, LANE)
                    t = db_ref[pl.ds(off, LANE), :].T
                    for w in range(PAIR):
                        dkey_ref[w, :, pl.ds(off, LANE)] = t[w:w + 1, :]
                    return carry

                lax.fori_loop(0, Sk // LANE, chunk, 0)

    qblk = pl.BlockSpec((bq, LANE), lambda p, i: (i, p))
    kres = pl.BlockSpec((Sk, LANE), lambda p, i: (0, p))
    stat = pl.BlockSpec((PAIR, 1, bq), lambda p, i: (p, 0, i))
    in_specs = [qblk, kres, kres, qblk, stat, qblk]
    args = [q2, k2, v2, o2, lse, do2]
    out_specs = [qblk, kres, kres]
    out_shape = [jax.ShapeDtypeStruct((S, C), BF16), jax.ShapeDtypeStruct((Sk, C), BF16),
                 jax.ShapeDtypeStruct((Sk, C), BF16)]
    scratch = [pltpu.VMEM((Sk, LANE), F32), pltpu.VMEM((Sk, LANE), F32)]
    if use_f:
        in_specs = [pl.BlockSpec(memory_space=pltpu.SMEM)] + in_specs + [kres, stat]
        args = [first] + args + [bias, r]
        out_specs += [stat, pl.BlockSpec((PAIR, 1, Sk), lambda p, i: (p, 0, 0))]
        out_shape += [jax.ShapeDtypeStruct((PAIR * P, 1, S), F32), jax.ShapeDtypeStruct((PAIR * P, 1, Sk), F32)]
        scratch.append(pltpu.VMEM((Sk, LANE), F32))
    return pl.pallas_call(
        body, name=name, grid=(P, nq), in_specs=in_specs, out_specs=out_specs, out_shape=out_shape,
        scratch_shapes=scratch, compiler_params=_cp(("parallel", "arbitrary"), ATTN_VMEM_LIMIT),
    )(*args)


def _sbp_fwd(q2, k2, v2, name):
    S, C = q2.shape
    bq, bk = _attn_blocks(S, S, SB_BLOCK)
    assert bq == bk
    nq, P = S // bq, C // LANE
    gp = FWD_PAIRS if P % FWD_PAIRS == 0 else 1
    c = min(CUMSUM_CHUNK, bk)

    def body(q_ref, k_ref, v_ref, o_ref, lt_ref, js_ref):
        i = pl.program_id(1)
        after = _tri2(c, lambda s, j: j > s)
        heads = range(PAIR * gp)
        qps = [q_ref[:, LANE * (h // PAIR):LANE * (h // PAIR + 1)] for h in heads]
        qTs = [jnp.where(_head_lanes(qps[h].shape, h % PAIR, 1), qps[h], jnp.zeros_like(qps[h])).T for h in heads]

        def blk(jj, carry, masked):
            j = i - jj
            off = pl.multiple_of(j * bk, bk)
            kbs = [k_ref[pl.ds(off, bk), LANE * g:LANE * (g + 1)] for g in range(gp)]
            vTbs = [v_ref[pl.ds(off, bk), LANE * g:LANE * (g + 1)].T for g in range(gp)]
            logs = [_sb_logs(_dot(kbs[h // PAIR], qTs[h])) for h in heads]
            ls, lk = [t[0] for t in logs], [t[1] for t in logs]
            if masked:
                valid = _valid_t(i, j, bq, bk, True)
                lk = [jnp.where(valid, t, 0.0) for t in lk]
            cs = [_key_cumsum(lk[h], after, True, carry[h][0]) for h in heads]
            wgt = [jnp.exp(ls[h] + cs[h][0]) for h in heads]
            if masked:
                wgt = [jnp.where(valid, t, 0.0) for t in wgt]
            acc = [carry[h][1] + _dot(vTbs[h // PAIR], wgt[h].astype(BF16)) for h in heads]
            return tuple((carry[h][0] + cs[h][1], acc[h]) for h in heads)

        def step(jj, state):
            carry, first = state
            live = jnp.max(functools.reduce(jnp.maximum, [carry[h][0] for h in heads])) >= SB_DEAD
            carry = lax.cond(live, lambda cr: blk(jj, cr, False), lambda cr: cr, carry)
            return carry, jnp.where(live, i - jj, first)

        carry = tuple((jnp.zeros((1, bq), F32), jnp.zeros((LANE, bq), F32)) for _ in heads)
        carry = blk(0, carry, True)
        carry, first = lax.fori_loop(1, i + 1, step, (carry, i))
        js_ref[0] = jnp.full((1, bq), first, jnp.int32)
        for h in heads:
            lt_ref[h] = carry[h][0]
        for g in range(gp):
            o_ref[:, LANE * g:LANE * (g + 1)] = _merge_pair([carry[h][1] for h in range(PAIR * g, PAIR * (g + 1))])

    qblk = pl.BlockSpec((bq, LANE * gp), lambda p, i: (i, p))
    kres = pl.BlockSpec((S, LANE * gp), lambda p, i: (0, p))
    stat = pl.BlockSpec((PAIR * gp, 1, bq), lambda p, i: (p, 0, i))
    return pl.pallas_call(
        body, name=name, grid=(P // gp, nq),
        in_specs=[qblk, kres, kres],
        out_specs=[qblk, stat, pl.BlockSpec((1, 1, bq), lambda p, i: (p, 0, i))],
        out_shape=[jax.ShapeDtypeStruct((S, C), F32), jax.ShapeDtypeStruct((PAIR * P, 1, S), F32),
                   jax.ShapeDtypeStruct((P // gp, 1, S), jnp.int32)],
        compiler_params=_cp(("parallel", "arbitrary"), ATTN_VMEM_LIMIT),
    )(q2, k2, v2)


def _sbp_bwd(q2, k2, v2, lt, first, do2, scale, name):
    S, C = q2.shape
    bq, bk = _attn_blocks(S, S, SB_BLOCK)
    nq, P = S // bq, C // LANE
    c = min(CUMSUM_CHUNK, bk)

    per_group = P // first.shape[0]

    def body(first_ref, q_ref, k_ref, v_ref, lt_ref, do_ref, dq_ref, dk_ref, dv_ref, dk_acc, dv_acc):
        i = pl.program_id(1)

        @pl.when(i == 0)
        def _():
            dk_acc[...] = jnp.zeros_like(dk_acc)
            dv_acc[...] = jnp.zeros_like(dv_acc)

        qp = q_ref[...]
        dof = do_ref[...]
        upto = _tri2(c, lambda s, j: j <= s)
        before = _tri2(c, lambda s, j: j < s)
        heads = range(PAIR)
        mine = [_head_lanes(qp.shape, w, 1) for w in heads]
        qz = [jnp.where(mine[w], qp, jnp.zeros_like(qp)) for w in heads]
        qTs = [qz[w].T for w in heads]
        doz = [jnp.where(mine[w], dof, 0.0).astype(BF16) for w in heads]
        doT = [doz[w].T for w in heads]
        ltot = [lt_ref[w] for w in heads]

        def blk(j, carry, masked):
            off = pl.multiple_of(j * bk, bk)
            kb = k_ref[pl.ds(off, bk), :]
            vb = v_ref[pl.ds(off, bk), :]
            kTb = kb.T
            logs = [_sb_logs(_dot(kb, qTs[w])) for w in heads]
            ls, lk = [t[0] for t in logs], [t[1] for t in logs]
            if masked:
                valid = _valid_t(i, j, bq, bk, True)
                lk = [jnp.where(valid, t, 0.0) for t in lk]
            pin = [_key_cumsum(lk[w], upto, False, carry[w][1] - ltot[w]) for w in heads]
            wgt = [jnp.exp(ls[w] - pin[w][0]) for w in heads]
            if masked:
                wgt = [jnp.where(valid, t, 0.0) for t in wgt]
            g = [_dot(vb, doT[w]) * wgt[w] for w in heads]
            cin = [_key_cumsum(g[w], before, False, carry[w][2]) for w in heads]
            sig = [jnp.exp(t) for t in ls]
            dz = [g[w] * (1.0 - sig[w]) - cin[w][0] * sig[w] for w in heads]
            if masked:
                dz = [jnp.where(valid, t, 0.0) for t in dz]
            dzb = [t.astype(BF16) for t in dz]
            dvs = [_dot(wgt[w].astype(BF16), doz[w]) for w in heads]
            dks = [_dot(dzb[w], qz[w]) for w in heads]
            dv_acc[pl.ds(off, bk), :] += dvs[0] + dvs[1]
            dk_acc[pl.ds(off, bk), :] += dks[0] + dks[1]
            return tuple((carry[w][0] + _dot(kTb, dzb[w]), carry[w][1] + pin[w][1], carry[w][2] + cin[w][1])
                         for w in heads)

        carry = tuple((jnp.zeros((LANE, bq), F32), jnp.zeros((1, bq), F32), jnp.zeros((1, bq), F32)) for _ in heads)
        start = first_ref[pl.program_id(0) // per_group, i]
        carry = lax.fori_loop(start, i, lambda j, cr: blk(j, cr, False), carry)
        carry = blk(i, carry, True)
        dq_ref[...] = (_merge_pair([carry[w][0] for w in heads]) * scale).astype(BF16)

        @pl.when(i == nq - 1)
        def _():
            dk_ref[...] = dk_acc[...].astype(BF16)
            dv_ref[...] = dv_acc[...].astype(BF16)

    qblk = pl.BlockSpec((bq, LANE), lambda p, i: (i, p))
    kres = pl.BlockSpec((S, LANE), lambda p, i: (0, p))
    stat = pl.BlockSpec((PAIR, 1, bq), lambda p, i: (p, 0, i))
    return pl.pallas_call(
        body, name=name, grid=(P, nq),
        in_specs=[pl.BlockSpec(memory_space=pltpu.SMEM), qblk, kres, kres, stat, qblk],
        out_specs=[qblk, kres, kres],
        out_shape=[jax.ShapeDtypeStruct((S, C), BF16)] * 3,
        scratch_shapes=[pltpu.VMEM((S, LANE), F32), pltpu.VMEM((S, LANE), F32)],
        compiler_params=_cp(("parallel", "arbitrary"), ATTN_VMEM_LIMIT),
    )(first, q2, k2, v2, lt, do2)


def _bias_cols(f_cum):
    H, Sk = f_cum.shape
    terms = jnp.stack(_split3(f_cum), axis=-1)
    packed = terms.reshape(H // PAIR, PAIR, Sk, 3).transpose(2, 0, 1, 3).reshape(Sk, H // PAIR, PAIR * 3)
    return jnp.pad(packed, ((0, 0), (0, 0), (0, LANE - PAIR * 3))).reshape(Sk, -1)


def _make_packed_softmax(name, scale, causal, use_f):
    assert _pow2(scale)

    def run_fwd(q16, k16, v16, f_cum):
        q16 = q16 * scale
        if not use_f:
            o, lse = _smp_fwd(q16, k16, v16, None, None, causal, name + "_fwd")
            return o, (q16, k16, v16, o, lse, None, None, None)
        bq, bk = _attn_blocks(q16.shape[0], k16.shape[0], FOX_BLOCK)
        n_heads = f_cum.shape[0]
        knorm = jnp.sqrt(jnp.sum(jnp.square(k16.astype(F32)).reshape(-1, bk, n_heads, HEAD_DIM), axis=3))
        kstat = jnp.concatenate([jnp.max(knorm, axis=1).T.reshape(-1), f_cum[:, bk - 1::bk].reshape(-1)])
        bias, r = _bias_cols(f_cum), (f_cum * LOG2E)[:, None, :]
        o, lse, first = _smp_fwd(q16, k16, v16, bias, r, causal, name + "_fwd", lax.stop_gradient(kstat))
        return o, (q16, k16, v16, o, lse, bias, r, first[:, 0, ::bq])

    def run_bwd(saved, do):
        q16, k16, v16, o, lse, bias, r, first = saved
        outs = _smp_bwd(q16, k16, v16, o, lse, do, bias, r, scale, causal, name + "_bwd", first)
        if use_f:
            return outs[0], outs[1], outs[2], outs[3][:, 0, :] - outs[4][:, 0, :]
        return tuple(outs)

    if use_f:
        @jax.custom_vjp
        def attn(q, k, v, f_cum):
            return run_fwd(q, k, v, f_cum)[0]

        attn.defvjp(run_fwd, run_bwd)
    else:
        @jax.custom_vjp
        def attn(q, k, v):
            return run_fwd(q, k, v, None)[0]

        attn.defvjp(lambda q, k, v: run_fwd(q, k, v, None), run_bwd)
    return attn


def _make_packed_sb(name, scale):
    assert _pow2(scale)

    def run_fwd(q16, k16, v16):
        q16 = q16 * scale
        o, lt, first = _sbp_fwd(q16, k16, v16, name + "_fwd")
        bq, _ = _attn_blocks(q16.shape[0], q16.shape[0], SB_BLOCK)
        return o, (q16, k16, v16, lt, first[:, 0, ::bq])

    def run_bwd(saved, do):
        q16, k16, v16, lt, first = saved
        return tuple(_sbp_bwd(q16, k16, v16, lt, first, do, scale, name + "_bwd"))

    @jax.custom_vjp
    def attn(q, k, v):
        return run_fwd(q, k, v)[0]

    attn.defvjp(run_fwd, run_bwd)
    return attn


def _round_bf16(x):
    return lax.reduce_precision(x, exponent_bits=8, mantissa_bits=7)


def _split3(x):
    hi = _round_bf16(x)
    mid = _round_bf16(x - hi)
    lo = _round_bf16(x - hi - mid)
    return hi.astype(BF16), mid.astype(BF16), lo.astype(BF16)


def _pow2(x):
    m, _ = math.frexp(x)
    return m == 0.5


def _pad_last(x, n):
    return jnp.pad(x, [(0, 0)] * (x.ndim - 1) + [(0, n - x.shape[-1])])


def _layouts(q, k, scale):
    qh = _pad_last(jnp.transpose(q * scale if _pow2(scale) else q, (1, 0, 2)).astype(BF16), LANE)
    return qh, _pad_last(jnp.transpose(k, (1, 0, 2)).astype(BF16), LANE)


def _make_softmax_attn(name, scale, causal, d):
    pre = _pow2(scale)
    cmul = LOG2E if pre else scale * LOG2E
    gscale = 1.0 if pre else scale

    def run_fwd(q, k, v):
        qn, kn = _layouts(q, k, scale)
        vn = jnp.transpose(v, (1, 0, 2)).astype(BF16)
        oT, lse = _sm_fwd_t(qn, kn, jnp.transpose(vn, (0, 2, 1)), cmul, causal, name + "_fwd")
        return jnp.transpose(oT, (2, 0, 1)), (qn, kn, vn, oT, lse)

    def run_bwd(saved, dout):
        qn, kn, vn, oT, lse = saved
        doT = jnp.transpose(dout, (1, 2, 0))
        do = jnp.transpose(dout, (1, 0, 2)).astype(BF16)
        dq, dk, dv = _sm_bwd_t(qn, kn, vn, oT, lse, doT, do, cmul, gscale, causal, name + "_bwd")
        dq = jnp.transpose(dq[:, :, :d], (1, 0, 2))
        if pre:
            dq = dq * scale
        return dq, jnp.transpose(dk[:, :, :d], (1, 0, 2)), jnp.transpose(dv, (1, 0, 2))

    @jax.custom_vjp
    def attn(q, k, v):
        return run_fwd(q, k, v)[0]

    attn.defvjp(run_fwd, run_bwd)
    return attn


def _rope(x, positions):
    half = x.shape[-1] // 2
    inv_freq = ROPE_THETA ** (-jnp.arange(half, dtype=F32) / half)
    ang = positions.astype(F32)[:, None] * inv_freq[None, :]
    ang = ang.reshape((ang.shape[0],) + (1,) * (x.ndim - 2) + (half,))
    cos, sin = jnp.cos(ang), jnp.sin(ang)
    x1, x2 = x[..., :half], x[..., half:]
    return jnp.concatenate([x1 * cos - x2 * sin, x1 * sin + x2 * cos], axis=-1)


def _permute_cols(w):
    parts = [w[..., _ORIG_OFF[idx]:_ORIG_OFF[idx] + SPLIT_SIZES[idx]] for _, idx in _PERM]
    pad = jnp.zeros(w.shape[:-1] + (PROJ_COLS - IN_COLS,), w.dtype)
    return jnp.concatenate(parts + [pad], axis=-1)


def _unpermute_cols(w):
    start, parts = 0, [None] * len(SPLIT_SIZES)
    for _, idx in _PERM:
        parts[idx] = w[..., start:start + SPLIT_SIZES[idx]]
        start += SPLIT_SIZES[idx]
    return jnp.concatenate(parts, axis=-1)


_BF16_PIECES = ("fq", "fk", "fv", "sq", "sk", "sv", "mq")


def _make_ln_proj(name, has_res):
    wide = [(n, SPLIT_SIZES[idx]) for n, idx in _PERM if SPLIT_SIZES[idx] % LANE == 0]
    narrow = [(n, SPLIT_SIZES[idx]) for n, idx in _PERM if SPLIT_SIZES[idx] % LANE]
    assert [n for n, _ in wide + narrow] == [n for n, _ in _PERM]
    tail = PROJ_COLS - sum(w for _, w in wide)

    def run_fwd(x, res, g, b, w):
        h, h16, hT16 = _ln_fwd_call(x, res, g, b, name + "_ln_fwd", also16=True)
        w16 = w.astype(BF16)
        outs = _matmul_split(h16, w16, [w for _, w in wide] + [tail],
                             [BF16 if n in _BF16_PIECES else F32 for n, _ in wide] + [F32], name + "_fwd")
        pieces, off = list(outs[:-1]), 0
        for _, width in narrow:
            pieces.append(outs[-1][:, off:off + width])
            off += width
        return (h, tuple(pieces)), (x, res, g, hT16, w16)

    def run_bwd(saved, cts):
        x, res, g, hT16, w16 = saved
        dh, dpieces = cts
        wide = [c for c in dpieces if c.shape[1] % LANE == 0]
        narrow = [c.astype(BF16) for c in dpieces if c.shape[1] % LANE]
        pad = jnp.zeros((x.shape[0], PROJ_COLS - IN_COLS), BF16)
        da, dy16 = _concat_matmul_nt(wide + [jnp.concatenate(narrow + [pad], axis=1)], w16, name + "_dx")
        dw = _matmul(hT16, dy16, "nn", name + "_dw")
        outs = _ln_bwd_call(dh, x, res, g, name + "_ln_bwd", dy2=da)
        if has_res:
            dx, dr, dg, db = outs
            return dx, dr, dg.reshape(-1), db.reshape(-1), dw
        dx, dg, db = outs
        return dx, dg.reshape(-1), db.reshape(-1), dw

    if has_res:
        @jax.custom_vjp
        def op(x, res, g, b, w):
            return run_fwd(x, res, g, b, w)[0]

        op.defvjp(run_fwd, run_bwd)
    else:
        @jax.custom_vjp
        def op(x, g, b, w):
            return run_fwd(x, None, g, b, w)[0]

        op.defvjp(lambda x, g, b, w: run_fwd(x, None, g, b, w), run_bwd)

    def call(*args):
        h, pieces = op(*args)
        return h, {n: part for (n, _), part in zip(_PERM, pieces)}

    return call


def _trunk_loss(wts, x2d, mem2d, target2d):
    s = x2d.shape[0]
    positions = jnp.arange(s)
    head_scale = HEAD_DIM ** -0.5
    mla_scale = (MLA_NOPE + MLA_ROPE) ** -0.5

    mem_n = _make_ln("ln_mem", False)(mem2d, wts["mem_ln_g"], wts["mem_ln_b"])
    h, y = None, x2d
    for l in range(DEPTH):
        tag = f"l{l}_"
        w_p = wts["w_in"][l]
        if l == 0:
            h, p = _make_ln_proj(tag + "proj", False)(y, wts["ln_in_g"], wts["ln_in_b"], w_p)
        else:
            h, p = _make_ln_proj(tag + "proj", True)(y, h, wts["ln_g"][l - 1], wts["ln_b"][l - 1], w_p)

        log_f = jax.nn.log_sigmoid(p["f_logit"].T + wts["b_forget"][l][:, None])
        f_cum = jnp.cumsum(log_f, axis=1)
        out_fox = _make_packed_softmax(tag + "fox", head_scale, True, True)(p["fq"], p["fk"], p["fv"], f_cum)

        out_sb = _make_packed_sb(tag + "sb", head_scale)(p["sq"], p["sk"], p["sv"])

        cqn = _make_rms(tag + "rms_q")(p["c_q"], wts["mla_q_norm_g"][l])
        q_mla = _make_mm(tag + "q_up")(cqn, wts["w_mla_q_up"][l]).reshape(s, N_HEADS, MLA_NOPE + MLA_ROPE)
        ckvn = _make_rms(tag + "rms_kv")(p["c_kv"], wts["mla_kv_norm_g"][l])
        kv_mla = _make_mm(tag + "kv_up")(ckvn, wts["w_mla_kv_up"][l]).reshape(s, N_HEADS, MLA_NOPE + MLA_V)
        q_full = jnp.concatenate([q_mla[..., :MLA_NOPE], _rope(q_mla[..., MLA_NOPE:], positions)], axis=-1)
        k_rope = jnp.broadcast_to(_rope(p["k_rot"], positions)[:, None, :], (s, N_HEADS, MLA_ROPE))
        k_full = jnp.concatenate([kv_mla[..., :MLA_NOPE], k_rope], axis=-1)
        out_mla = _make_softmax_attn(tag + "mla", mla_scale, True, MLA_NOPE + MLA_ROPE)(
            q_full, k_full, kv_mla[..., MLA_NOPE:]).reshape(s, GROUP_W)

        mkv = _make_mm(tag + "mem_kv")(mem_n, wts["w_mem_kv"][l])
        out_mem = _make_packed_softmax(tag + "mem", head_scale, False, False)(
            p["mq"], mkv[:, :GROUP_W].astype(BF16), mkv[:, GROUP_W:].astype(BF16))

        y = _make_gate_out(tag + "out")((out_fox, out_sb, out_mla, out_mem), p["gate"], wts["w_out"][l])

    h = _make_ln(f"l{DEPTH - 1}_ln", True)(y, h, wts["ln_g"][DEPTH - 1], wts["ln_b"][DEPTH - 1])
    return _loss_op(h, target2d)


def _mesh_pos():
    x, y, c = (lax.axis_index(a) for a in MESH_AXES)
    return x, y, c, 4 * x + 2 * y + c


def _peer(x, y, c, mask):
    return (x ^ ((mask >> 2) & 1), y ^ ((mask >> 1) & 1), c ^ (mask & 1))


_ANY = pl.BlockSpec(memory_space=pl.ANY)


def _all_gather(row_shards, stack_shards):
    n_row, n_all = len(row_shards), len(row_shards) + len(stack_shards)
    shards = list(row_shards) + list(stack_shards)
    chip_masks = (4, 2, 6)
    tensors = range(n_all)

    def body(*refs):
        ins, outs = refs[:n_all], refs[n_all:2 * n_all]
        send_sems, recv_sems, local_sems = refs[2 * n_all:]
        x, y, c, me = _mesh_pos()
        sibling = _peer(x, y, c, 1)

        def window(t, slot):
            if t < n_row:
                rows = shards[t].shape[1]
                return outs[t].at[:, pl.ds(slot * rows, rows), :]
            return outs[t].at[slot]

        def copy(t, k, slot, to, src=None):
            return pltpu.make_async_remote_copy(
                src_ref=window(t, slot) if src is None else src, dst_ref=window(t, slot),
                send_sem=send_sems.at[t, k], recv_sem=recv_sems.at[t, k], device_id=to,
                device_id_type=pl.DeviceIdType.MESH)

        local = [pltpu.make_async_copy(ins[t], window(t, me), local_sems.at[t]) for t in tensors]
        for cp in local:
            cp.start()
        first = [copy(t, 0, me, sibling, src=ins[t]) for t in tensors]
        first += [copy(t, 1 + j, me, _peer(x, y, c, m), src=ins[t]) for j, m in enumerate(chip_masks) for t in tensors]
        for cp in first:
            cp.start()
        passed = []
        for j, m in enumerate(chip_masks):
            for t in tensors:
                copy(t, 1 + j, me ^ m, sibling).wait_recv()
            for t in tensors:
                cp = copy(t, 4 + j, me ^ m, sibling)
                cp.start()
                passed.append(cp)
        for t in tensors:
            copy(t, 0, me ^ 1, sibling).wait_recv()
        for j, m in enumerate(chip_masks):
            for t in tensors:
                copy(t, 4 + j, me ^ m ^ 1, sibling).wait_recv()
        for cp in first + passed:
            cp.wait_send()
        for cp in local:
            cp.wait()

    out_shape = [jax.ShapeDtypeStruct((a.shape[0], N_DEV * a.shape[1], a.shape[2]), a.dtype) for a in row_shards]
    out_shape += [jax.ShapeDtypeStruct((N_DEV,) + a.shape, a.dtype) for a in stack_shards]
    return pl.pallas_call(
        body, name="all_gather_weights", in_specs=[_ANY] * n_all, out_specs=[_ANY] * n_all, out_shape=out_shape,
        scratch_shapes=[pltpu.SemaphoreType.DMA((n_all, N_DEV - 1)), pltpu.SemaphoreType.DMA((n_all, N_DEV - 1)),
                        pltpu.SemaphoreType.DMA((n_all,))],
    )(*shards)


def _reduce_scatter(row_full, stack_full, bcast):
    n_row, n_stack = len(row_full), len(stack_full)
    n_all = n_row + n_stack + len(bcast)
    fulls = list(row_full) + list(stack_full) + list(bcast)

    def body(*refs):
        ins, outs = refs[:n_all], refs[n_all:2 * n_all]
        send_sems, recv_sems, local_sems = refs[2 * n_all:]
        x, y, c, me = _mesh_pos()

        def part(t, slot):
            if t < n_row:
                rows = fulls[t].shape[1] // N_DEV
                return ins[t].at[:, pl.ds(slot * rows, rows), :]
            if t < n_row + n_stack:
                return ins[t].at[slot]
            return ins[t]

        local = [pltpu.make_async_copy(part(t, me), outs[t].at[me], local_sems.at[t]) for t in range(n_all)]
        for cp in local:
            cp.start()
        sends = []
        for mask in range(1, N_DEV):
            for t in range(n_all):
                cp = pltpu.make_async_remote_copy(
                    src_ref=part(t, me ^ mask), dst_ref=outs[t].at[me], send_sem=send_sems.at[t, mask - 1],
                    recv_sem=recv_sems.at[t, mask - 1], device_id=_peer(x, y, c, mask),
                    device_id_type=pl.DeviceIdType.MESH)
                cp.start()
                sends.append(cp)
        for mask in range(1, N_DEV):
            for t in range(n_all):
                pltpu.make_async_remote_copy(
                    src_ref=part(t, me), dst_ref=outs[t].at[me ^ mask], send_sem=send_sems.at[t, mask - 1],
                    recv_sem=recv_sems.at[t, mask - 1], device_id=_peer(x, y, c, mask),
                    device_id_type=pl.DeviceIdType.MESH).wait_recv()
        for cp in sends:
            cp.wait_send()
        for cp in local:
            cp.wait()

    out_shape = [jax.ShapeDtypeStruct((N_DEV, a.shape[0], a.shape[1] // N_DEV, a.shape[2]), a.dtype) for a in row_full]
    out_shape += [jax.ShapeDtypeStruct(a.shape, a.dtype) for a in stack_full]
    out_shape += [jax.ShapeDtypeStruct((N_DEV,) + a.shape, a.dtype) for a in bcast]
    return pl.pallas_call(
        body, name="reduce_scatter_grads", in_specs=[_ANY] * n_all, out_specs=[_ANY] * n_all, out_shape=out_shape,
        scratch_shapes=[pltpu.SemaphoreType.DMA((n_all, N_DEV - 1)), pltpu.SemaphoreType.DMA((n_all, N_DEV - 1)),
                        pltpu.SemaphoreType.DMA((n_all,))],
    )(*fulls)


def _adamw(slots, w, m, v, name):
    shape = w.shape
    cols = shape[-1]
    rows = math.prod(shape[:-1])
    tr = _pick(rows, (64, 32, 16, 8))
    c1 = 1.0 - ADAM_B1 ** ADAM_STEP
    c2 = 1.0 - ADAM_B2 ** ADAM_STEP

    def body(s_ref, w_ref, m_ref, v_ref, g_ref, d_ref, nm_ref, nv_ref):
        g = s_ref[0].astype(F32)
        for k in range(1, N_DEV):
            g = g + s_ref[k].astype(F32)
        nm = ADAM_B1 * m_ref[...] + (1.0 - ADAM_B1) * g
        nv = ADAM_B2 * v_ref[...] + (1.0 - ADAM_B2) * (g * g)
        g_ref[...] = g
        nm_ref[...] = nm
        nv_ref[...] = nv
        d_ref[...] = -ADAM_LR * ((nm / c1) / (jnp.sqrt(nv / c2) + ADAM_EPS) + ADAM_WD * w_ref[...])

    row = pl.BlockSpec((tr, cols), lambda i: (i, 0))
    out = jax.ShapeDtypeStruct((rows, cols), F32)
    outs = pl.pallas_call(
        body, name=name, grid=(rows // tr,),
        in_specs=[pl.BlockSpec((N_DEV, tr, cols), lambda i: (0, i, 0)), row, row, row],
        out_specs=[row] * 4, out_shape=[out] * 4, compiler_params=_cp(("parallel",)),
    )(slots.reshape(N_DEV, rows, cols), w.reshape(rows, cols), m.reshape(rows, cols), v.reshape(rows, cols))
    return [o.reshape(shape) for o in outs]


_SMALL = ("ln_in_g", "ln_in_b", "mem_ln_g", "mem_ln_b", "b_forget", "mla_q_norm_g", "mla_kv_norm_g", "ln_g", "ln_b")
_ORDER = ("ln_in_g", "ln_in_b", "mem_ln_g", "mem_ln_b", "w_in", "b_forget", "mla_q_norm_g", "w_mla_q_up",
          "mla_kv_norm_g", "w_mla_kv_up", "w_mem_kv", "w_out", "ln_g", "ln_b")


def _pack_small(d):
    flat = jnp.concatenate([d[n].reshape(-1) for n in _SMALL])
    n = flat.shape[0]
    padded = ((n + 8 * LANE - 1) // (8 * LANE)) * (8 * LANE)
    return jnp.pad(flat, (0, padded - n)).reshape(-1, LANE)


def _unpack_small(packed, like):
    flat, out, off = packed.reshape(-1), {}, 0
    for n in _SMALL:
        size = math.prod(like[n].shape)
        out[n] = flat[off:off + size].reshape(like[n].shape)
        off += size
    return out


def _unstack_cols(g):
    n, l, r, c = g.shape
    return g.transpose(1, 2, 0, 3).reshape(l, r, n * c)


def _stack_cols(g):
    l, r, nc = g.shape
    return g.reshape(l, r, N_DEV, nc // N_DEV).transpose(2, 0, 1, 3)


def kernel(x, mem, ln_in_g, ln_in_b, mem_ln_g, mem_ln_b, w_in, b_forget, mla_q_norm_g, w_mla_q_up, mla_kv_norm_g, w_mla_kv_up, w_mem_kv, w_out, ln_g, ln_b, loss_target, m_ln_in_g, m_ln_in_b, m_mem_ln_g, m_mem_ln_b, m_w_in, m_b_forget, m_mla_q_norm_g, m_w_mla_q_up, m_mla_kv_norm_g, m_w_mla_kv_up, m_w_mem_kv, m_w_out, m_ln_g, m_ln_b, v_ln_in_g, v_ln_in_b, v_mem_ln_g, v_mem_ln_b, v_w_in, v_b_forget, v_mla_q_norm_g, v_w_mla_q_up, v_mla_kv_norm_g, v_w_mla_kv_up, v_w_mem_kv, v_w_out, v_ln_g, v_ln_b):
    w_shard = dict(ln_in_g=ln_in_g, ln_in_b=ln_in_b, mem_ln_g=mem_ln_g, mem_ln_b=mem_ln_b, w_in=w_in,
                   b_forget=b_forget, mla_q_norm_g=mla_q_norm_g, w_mla_q_up=w_mla_q_up,
                   mla_kv_norm_g=mla_kv_norm_g, w_mla_kv_up=w_mla_kv_up, w_mem_kv=w_mem_kv, w_out=w_out,
                   ln_g=ln_g, ln_b=ln_b)
    m_shard = dict(ln_in_g=m_ln_in_g, ln_in_b=m_ln_in_b, mem_ln_g=m_mem_ln_g, mem_ln_b=m_mem_ln_b, w_in=m_w_in,
                   b_forget=m_b_forget, mla_q_norm_g=m_mla_q_norm_g, w_mla_q_up=m_w_mla_q_up,
                   mla_kv_norm_g=m_mla_kv_norm_g, w_mla_kv_up=m_w_mla_kv_up, w_mem_kv=m_w_mem_kv, w_out=m_w_out,
                   ln_g=m_ln_g, ln_b=m_ln_b)
    v_shard = dict(ln_in_g=v_ln_in_g, ln_in_b=v_ln_in_b, mem_ln_g=v_mem_ln_g, mem_ln_b=v_mem_ln_b, w_in=v_w_in,
                   b_forget=v_b_forget, mla_q_norm_g=v_mla_q_norm_g, w_mla_q_up=v_w_mla_q_up,
                   mla_kv_norm_g=v_mla_kv_norm_g, w_mla_kv_up=v_w_mla_kv_up, w_mem_kv=v_w_mem_kv, w_out=v_w_out,
                   ln_g=v_ln_g, ln_b=v_ln_b)

    to16 = lambda ws: [a.astype(BF16) for a in ws]
    gathered = _all_gather(to16([_permute_cols(w_in), w_mem_kv, w_out]), to16([w_mla_q_up, w_mla_kv_up]))
    g_in, g_mem, g_out, g_qup, g_kvup = [a.astype(F32) for a in gathered]
    full = dict(w_shard)
    full.update(w_in=g_in, w_mem_kv=g_mem, w_out=g_out, w_mla_q_up=_unstack_cols(g_qup),
                w_mla_kv_up=_unstack_cols(g_kvup))

    loss_local, (grad_w, grad_x) = jax.value_and_grad(_trunk_loss, argnums=(0, 1))(
        full, x[0], mem[0], loss_target[0])

    s_in, s_mem, s_out, s_qup, s_kvup, s_small = _reduce_scatter(
        to16([grad_w["w_in"], grad_w["w_mem_kv"], grad_w["w_out"]]),
        to16([_stack_cols(grad_w["w_mla_q_up"]), _stack_cols(grad_w["w_mla_kv_up"])]),
        [_pack_small(grad_w)])

    res = {}
    for name, slots in (("w_mem_kv", s_mem), ("w_out", s_out), ("w_mla_q_up", s_qup), ("w_mla_kv_up", s_kvup)):
        res[name] = _adamw(slots, w_shard[name], m_shard[name], v_shard[name], "adamw_" + name)
    res["w_in"] = [_unpermute_cols(a) for a in _adamw(
        s_in, _permute_cols(w_in), _permute_cols(m_w_in), _permute_cols(v_w_in), "adamw_w_in")]
    small = _adamw(s_small, _pack_small(w_shard), _pack_small(m_shard), _pack_small(v_shard), "adamw_small")
    small = [_unpack_small(a, w_shard) for a in small]
    for name in _SMALL:
        res[name] = [a[name] for a in small]

    loss = lax.psum(loss_local, MESH_AXES)
    outs = [loss, grad_x[None]]
    for k in range(4):
        outs += [res[name][k] for name in _ORDER]
    return tuple(outs)
```

```python
import functools
import math

import jax
import jax.numpy as jnp
from jax import lax
from jax.experimental import pallas as pl
from jax.experimental.pallas import tpu as pltpu

F32 = jnp.float32
BF16 = jnp.bfloat16

D_MODEL = 1024
DEPTH = 2
GROUP_W = 256
N_HEADS = 4
HEAD_DIM = 64
MLA_Q_RANK = 256
MLA_KV_RANK = 128
MLA_NOPE = 64
MLA_ROPE = 32
MLA_V = 64
ROPE_THETA = 10000.0
LN_EPS = 1e-5
RMS_EPS = 1e-6
DEEPNORM_ALPHA = (2 * DEPTH) ** 0.25
SPLIT_SIZES = (256, 256, 256, 4, 256, 256, 256, 256, 128, 32, 256, 1024)
IN_COLS = sum(SPLIT_SIZES)
_ORIG_OFF = [sum(SPLIT_SIZES[:i]) for i in range(len(SPLIT_SIZES))]
_PERM = (("fq", 0), ("fk", 1), ("fv", 2), ("sq", 4), ("sk", 5), ("sv", 6), ("c_q", 7), ("c_kv", 8),
         ("mq", 10), ("gate", 11), ("k_rot", 9), ("f_logit", 3))
LANE = 128
PROJ_COLS = ((IN_COLS + LANE - 1) // LANE) * LANE

ADAM_LR = 0.001
ADAM_B1 = 0.9
ADAM_B2 = 0.999
ADAM_EPS = 1e-08
ADAM_WD = 0.01
ADAM_STEP = 10

N_DEV = 8
MESH_AXES = ("x", "y", "c")
VMEM_LIMIT = 48 * 1024 * 1024
ATTN_VMEM_LIMIT = 56 * 1024 * 1024
ATTN_BQ = 512
ATTN_BK = 512
CUMSUM_CHUNK = 256
NEG_BIG = -1e30
LOG2E = math.log2(math.e)
MM_TM, MM_TN, MM_TK, MM_TK_NT = 1024, 1664, 1024, 3328
CONCAT_MM_TM = 512
SPLIT_MM_TM = 1024

_NT = (((1,), (1,)), ((), ()))
_NN = (((1,), (0,)), ((), ()))


def _cp(sem, vmem=VMEM_LIMIT):
    return pltpu.CompilerParams(dimension_semantics=sem, vmem_limit_bytes=vmem)


def _dot(a, b, dims=_NN):
    return lax.dot_general(a, b, dims, preferred_element_type=F32)


def _pick(n, cands):
    for c in cands:
        if c <= n and n % c == 0:
            return c
    return n


def _tile(n, cap):
    if n <= cap:
        return n
    best = None
    for d in range(LANE, cap + 1, LANE):
        if n % d == 0:
            best = d
    assert best is not None, (n, cap)
    return best


def _matmul(a, b, mode, name):
    if mode == "nn":
        (M, K), (K2, N) = a.shape, b.shape
    else:
        (M, K), (N, K2) = a.shape, b.shape
    assert K == K2 and a.dtype == BF16 and b.dtype == BF16, (a.shape, b.shape, mode)
    tm, tn = _tile(M, MM_TM), _tile(N, MM_TN)
    tk = _tile(K, MM_TK if mode == "nn" else MM_TK_NT)
    nk = K // tk
    dims = _NN if mode == "nn" else _NT

    def body(a_ref, b_ref, o_ref, acc_ref):
        part = _dot(a_ref[...], b_ref[...], dims)
        if nk == 1:
            o_ref[...] = part
        else:
            k = pl.program_id(2)

            @pl.when(k == 0)
            def _():
                acc_ref[...] = part

            @pl.when(k > 0)
            def _():
                acc_ref[...] += part

            @pl.when(k == nk - 1)
            def _():
                o_ref[...] = acc_ref[...]

    a_spec = pl.BlockSpec((tm, tk), lambda j, i, k: (i, k))
    if mode == "nn":
        b_spec = pl.BlockSpec((tk, tn), lambda j, i, k: (k, j))
    else:
        b_spec = pl.BlockSpec((tn, tk), lambda j, i, k: (j, k))
    acc_shape = (tm, tn) if nk > 1 else (8, LANE)
    o_spec = pl.BlockSpec((tm, tn), lambda j, i, k: (i, j))
    return pl.pallas_call(
        body, name=name, grid=(N // tn, M // tm, nk),
        in_specs=[a_spec, b_spec], out_specs=o_spec, out_shape=jax.ShapeDtypeStruct((M, N), F32),
        scratch_shapes=[pltpu.VMEM(acc_shape, F32)],
        compiler_params=_cp(("parallel", "parallel", "arbitrary")),
    )(a, b)


def _concat_matmul_nt(pieces, b, name):
    M, (N, K) = pieces[0].shape[0], b.shape
    widths = [p.shape[1] for p in pieces]
    assert sum(widths) == K and all(w % LANE == 0 for w in widths) and b.dtype == BF16, (widths, b.shape)
    tm = _tile(M, CONCAT_MM_TM)
    n = len(pieces)

    def body(*refs):
        b_ref, o_ref, a_ref = refs[n:]
        a_ref[...] = jnp.concatenate([r[...].astype(BF16) for r in refs[:n]], axis=1)
        o_ref[...] = _dot(a_ref[...], b_ref[...], _NT)

    rows = lambda w: pl.BlockSpec((tm, w), lambda i: (i, 0))
    return pl.pallas_call(
        body, name=name, grid=(M // tm,),
        in_specs=[rows(w) for w in widths] + [pl.BlockSpec((N, K), lambda i: (0, 0))],
        out_specs=[rows(N), rows(K)],
        out_shape=[jax.ShapeDtypeStruct((M, N), F32), jax.ShapeDtypeStruct((M, K), BF16)],
        compiler_params=_cp(("parallel",)),
    )(*pieces, b)


def _matmul_split(a, b, widths, dtypes, name):
    (M, K), (K2, N) = a.shape, b.shape
    assert K == K2 and sum(widths) == N and all(w % LANE == 0 for w in widths), (a.shape, b.shape, widths)
    assert a.dtype == BF16 and b.dtype == BF16
    tm = _tile(M, SPLIT_MM_TM)
    offs = [sum(widths[:r]) for r in range(len(widths))]

    def body(a_ref, b_ref, *o_refs):
        av = a_ref[...]
        for o_ref, off, w in zip(o_refs, offs, widths):
            o_ref[...] = _dot(av, b_ref[:, off:off + w]).astype(o_ref.dtype)

    rows = lambda w: pl.BlockSpec((tm, w), lambda i: (i, 0))
    return pl.pallas_call(
        body, name=name, grid=(M // tm,),
        in_specs=[rows(K), pl.BlockSpec((K, N), lambda i: (0, 0), pipeline_mode=pl.Buffered(1))],
        out_specs=[rows(w) for w in widths],
        out_shape=[jax.ShapeDtypeStruct((M, w), d) for w, d in zip(widths, dtypes)],
        compiler_params=_cp(("parallel",)),
    )(a, b)


def _make_mm(name):
    @jax.custom_vjp
    def mm(a, w):
        return _matmul(a.astype(BF16), w.astype(BF16), "nn", name + "_fwd")

    def fwd(a, w):
        a16, w16 = a.astype(BF16), w.astype(BF16)
        return _matmul(a16, w16, "nn", name + "_fwd"), (a16, w16)

    def bwd(res, dy):
        a16, w16 = res
        dy16 = dy.astype(BF16)
        da = _matmul(dy16, w16, "nt", name + "_dx")
        dw = _matmul(a16.T, dy16, "nn", name + "_dw")
        return da, dw

    mm.defvjp(fwd, bwd)
    return mm


def _row_tile(rows):
    return _pick(rows, (512, 256, 128, 64, 32, 16, 8))


def _ln_stats(u):
    mu = jnp.mean(u, axis=-1, keepdims=True)
    d = u - mu
    var = jnp.mean(d * d, axis=-1, keepdims=True)
    return d, lax.rsqrt(var + LN_EPS)


def _ln_fwd_call(x, res, g, b, name, also16=False):
    rows, dm = x.shape
    tr = _row_tile(rows)
    has_res = res is not None
    n_in = 2 if has_res else 1

    def body(*refs):
        if has_res:
            u = DEEPNORM_ALPHA * refs[1][...] + refs[0][...]
        else:
            u = refs[0][...]
        g_ref, b_ref = refs[n_in], refs[n_in + 1]
        d, rstd = _ln_stats(u)
        y = d * rstd * g_ref[...] + b_ref[...]
        refs[n_in + 2][...] = y
        if also16:
            y16 = y.astype(BF16)
            refs[n_in + 3][...] = y16
            refs[n_in + 4][...] = y16.T

    row = pl.BlockSpec((tr, dm), lambda i: (i, 0))
    vec = pl.BlockSpec((1, dm), lambda i: (0, 0))
    args = (x, res) if has_res else (x,)
    out_specs, out_shape = [row], [jax.ShapeDtypeStruct((rows, dm), F32)]
    if also16:
        out_specs += [row, pl.BlockSpec((dm, tr), lambda i: (0, i))]
        out_shape += [jax.ShapeDtypeStruct((rows, dm), BF16), jax.ShapeDtypeStruct((dm, rows), BF16)]
    outs = pl.pallas_call(
        body, name=name, grid=(rows // tr,),
        in_specs=[row] * n_in + [vec, vec], out_specs=out_specs, out_shape=out_shape,
        compiler_params=_cp(("parallel",)),
    )(*args, g.reshape(1, dm), b.reshape(1, dm))
    return outs if also16 else outs[0]


def _ln_bwd_call(dy, x, res, g, name, dy2=None):
    rows, dm = x.shape
    tr = _row_tile(rows)
    has_res = res is not None
    two = dy2 is not None

    def body(*refs):
        dy_ref, refs = refs[0], refs[1:]
        if two:
            dy2_ref, refs = refs[0], refs[1:]
        if has_res:
            x_ref, r_ref, g_ref, dx_ref, dr_ref, dg_ref, db_ref = refs
            u = DEEPNORM_ALPHA * r_ref[...] + x_ref[...]
        else:
            x_ref, g_ref, dx_ref, dg_ref, db_ref = refs
            u = x_ref[...]
        i = pl.program_id(0)
        d, rstd = _ln_stats(u)
        xhat = d * rstd
        dyv = dy_ref[...] + dy2_ref[...] if two else dy_ref[...]
        dxh = dyv * g_ref[...]
        m1 = jnp.mean(dxh, axis=-1, keepdims=True)
        m2 = jnp.mean(dxh * xhat, axis=-1, keepdims=True)
        du = rstd * (dxh - m1 - xhat * m2)
        dx_ref[...] = du
        if has_res:
            dr_ref[...] = DEEPNORM_ALPHA * du
        pg = jnp.sum(dyv * xhat, axis=0, keepdims=True)
        pb = jnp.sum(dyv, axis=0, keepdims=True)

        @pl.when(i == 0)
        def _():
            dg_ref[...] = pg
            db_ref[...] = pb

        @pl.when(i > 0)
        def _():
            dg_ref[...] += pg
            db_ref[...] += pb

    row = pl.BlockSpec((tr, dm), lambda i: (i, 0))
    vec = pl.BlockSpec((1, dm), lambda i: (0, 0))
    big = jax.ShapeDtypeStruct((rows, dm), F32)
    small = jax.ShapeDtypeStruct((1, dm), F32)
    args = ((dy, dy2) if two else (dy,)) + ((x, res) if has_res else (x,))
    n_big = 2 if has_res else 1
    outs = pl.pallas_call(
        body, name=name, grid=(rows // tr,),
        in_specs=[row] * len(args) + [vec],
        out_specs=[row] * n_big + [vec, vec],
        out_shape=[big] * n_big + [small, small],
        compiler_params=_cp(("arbitrary",)),
    )(*args, g.reshape(1, dm))
    return outs


def _make_ln(name, has_res):
    if has_res:
        @jax.custom_vjp
        def ln(x, res, g, b):
            return _ln_fwd_call(x, res, g, b, name + "_fwd")

        def fwd(x, res, g, b):
            return ln(x, res, g, b), (x, res, g)

        def bwd(saved, dy):
            x, res, g = saved
            dx, dr, dg, db = _ln_bwd_call(dy, x, res, g, name + "_bwd")
            return dx, dr, dg.reshape(-1), db.reshape(-1)
    else:
        @jax.custom_vjp
        def ln(x, g, b):
            return _ln_fwd_call(x, None, g, b, name + "_fwd")

        def fwd(x, g, b):
            return ln(x, g, b), (x, g)

        def bwd(saved, dy):
            x, g = saved
            dx, dg, db = _ln_bwd_call(dy, x, None, g, name + "_bwd")
            return dx, dg.reshape(-1), db.reshape(-1)

    ln.defvjp(fwd, bwd)
    return ln


def _rms_fwd_call(x, g, name):
    rows, dm = x.shape
    tr = _row_tile(rows)

    def body(x_ref, g_ref, o_ref):
        xv = x_ref[...]
        rstd = lax.rsqrt(jnp.mean(xv * xv, axis=-1, keepdims=True) + RMS_EPS)
        o_ref[...] = xv * rstd * g_ref[...]

    row = pl.BlockSpec((tr, dm), lambda i: (i, 0))
    vec = pl.BlockSpec((1, dm), lambda i: (0, 0))
    return pl.pallas_call(
        body, name=name, grid=(rows // tr,), in_specs=[row, vec], out_specs=row,
        out_shape=jax.ShapeDtypeStruct((rows, dm), F32), compiler_params=_cp(("parallel",)),
    )(x, g.reshape(1, dm))


def _rms_bwd_call(dy, x, g, name):
    rows, dm = x.shape
    tr = _row_tile(rows)

    def body(dy_ref, x_ref, g_ref, dx_ref, dg_ref):
        i = pl.program_id(0)
        xv = x_ref[...]
        dyv = dy_ref[...]
        rstd = lax.rsqrt(jnp.mean(xv * xv, axis=-1, keepdims=True) + RMS_EPS)
        xhat = xv * rstd
        dxh = dyv * g_ref[...]
        m2 = jnp.mean(dxh * xhat, axis=-1, keepdims=True)
        dx_ref[...] = rstd * (dxh - xhat * m2)
        pg = jnp.sum(dyv * xhat, axis=0, keepdims=True)

        @pl.when(i == 0)
        def _():
            dg_ref[...] = pg

        @pl.when(i > 0)
        def _():
            dg_ref[...] += pg

    row = pl.BlockSpec((tr, dm), lambda i: (i, 0))
    vec = pl.BlockSpec((1, dm), lambda i: (0, 0))
    return pl.pallas_call(
        body, name=name, grid=(rows // tr,), in_specs=[row, row, vec], out_specs=[row, vec],
        out_shape=[jax.ShapeDtypeStruct((rows, dm), F32), jax.ShapeDtypeStruct((1, dm), F32)],
        compiler_params=_cp(("arbitrary",)),
    )(dy, x, g.reshape(1, dm))


def _make_rms(name):
    @jax.custom_vjp
    def rms(x, g):
        return _rms_fwd_call(x, g, name + "_fwd")

    def fwd(x, g):
        return rms(x, g), (x, g)

    def bwd(saved, dy):
        x, g = saved
        dx, dg = _rms_bwd_call(dy, x, g, name + "_bwd")
        return dx, dg.reshape(-1)

    rms.defvjp(fwd, bwd)
    return rms


def _sigmoid(x):
    return 1.0 / (1.0 + jnp.exp(-x))


def _gate_fwd_call(parts, gate, w16, name):
    rows, dm = gate.shape
    tr = _row_tile(rows)
    n = len(parts)
    n_out = w16.shape[1]
    assert w16.shape[0] == dm and w16.dtype == BF16

    def body(*refs):
        g_ref, w_ref, o_ref, gT_ref = refs[n:]
        gv = g_ref[...]
        mixed = jnp.concatenate([r[...] for r in refs[:n]], axis=1)
        y16 = (mixed * (gv * _sigmoid(gv))).astype(BF16)
        gT_ref[...] = y16.T
        o_ref[...] = _dot(y16, w_ref[...])

    row = pl.BlockSpec((tr, dm), lambda i: (i, 0))
    part_specs = [pl.BlockSpec((tr, p.shape[1]), lambda i: (i, 0)) for p in parts]
    return pl.pallas_call(
        body, name=name, grid=(rows // tr,), in_specs=part_specs + [row, pl.BlockSpec(w16.shape, lambda i: (0, 0))],
        out_specs=[pl.BlockSpec((tr, n_out), lambda i: (i, 0)), pl.BlockSpec((dm, tr), lambda i: (0, i))],
        out_shape=[jax.ShapeDtypeStruct((rows, n_out), F32), jax.ShapeDtypeStruct((dm, rows), BF16)],
        compiler_params=_cp(("parallel",)),
    )(*parts, gate, w16)


def _gate_bwd_call(dy16, w16, parts, gate, name):
    rows, dm = gate.shape
    tr = _row_tile(rows)
    n = len(parts)
    widths = [p.shape[1] for p in parts]
    assert w16.shape == (dm, dy16.shape[1]) and dy16.dtype == BF16 and w16.dtype == BF16

    def body(*refs):
        dy_ref, w_ref, g_ref = refs[0], refs[1], refs[n + 2]
        dm_refs, dg_ref = refs[n + 3:2 * n + 3], refs[2 * n + 3]
        gv = g_ref[...]
        dyv = _dot(dy_ref[...], w_ref[...], _NT)
        sg = _sigmoid(gv)
        mixed = jnp.concatenate([r[...] for r in refs[2:n + 2]], axis=1)
        dmixed = dyv * (gv * sg)
        off = 0
        for r, w in zip(dm_refs, widths):
            r[...] = dmixed[:, off:off + w]
            off += w
        dg_ref[...] = dyv * mixed * (sg * (1.0 + gv * (1.0 - sg)))

    row = pl.BlockSpec((tr, dm), lambda i: (i, 0))
    part_specs = [pl.BlockSpec((tr, w), lambda i: (i, 0)) for w in widths]
    dy_spec = pl.BlockSpec((tr, dy16.shape[1]), lambda i: (i, 0))
    w_spec = pl.BlockSpec(w16.shape, lambda i: (0, 0))
    return pl.pallas_call(
        body, name=name, grid=(rows // tr,), in_specs=[dy_spec, w_spec] + part_specs + [row],
        out_specs=part_specs + [row],
        out_shape=[jax.ShapeDtypeStruct((rows, w), F32) for w in widths] + [jax.ShapeDtypeStruct((rows, dm), F32)],
        compiler_params=_cp(("parallel",)),
    )(dy16, w16, *parts, gate)


def _make_gate_out(name):
    def run_fwd(parts, gate, w):
        w16 = w.astype(BF16)
        y, gT16 = _gate_fwd_call(parts, gate, w16, name + "_gate_fwd")
        return y, (parts, gate, gT16, w16)

    def run_bwd(saved, dy):
        parts, gate, gT16, w16 = saved
        dy16 = dy.astype(BF16)
        *dparts, dgate = _gate_bwd_call(dy16, w16, parts, gate, name + "_gate_bwd")
        return tuple(dparts), dgate, _matmul(gT16, dy16, "nn", name + "_dw")

    @jax.custom_vjp
    def gate_out(parts, gate, w):
        return run_fwd(parts, gate, w)[0]

    gate_out.defvjp(run_fwd, run_bwd)
    return gate_out


def _loss_call(y, t, name):
    rows, dm = y.shape
    tr = _row_tile(rows)

    def body(y_ref, t_ref, l_ref, d_ref):
        i = pl.program_id(0)
        e = y_ref[...] - t_ref[...]
        d_ref[...] = e * (1.0 / dm)
        part = 0.5 * jnp.sum(jnp.mean(e * e, axis=-1, keepdims=True), axis=0, keepdims=True)

        @pl.when(i == 0)
        def _():
            l_ref[...] = part

        @pl.when(i > 0)
        def _():
            l_ref[...] += part

    row = pl.BlockSpec((tr, dm), lambda i: (i, 0))
    one = pl.BlockSpec((1, 1), lambda i: (0, 0))
    return pl.pallas_call(
        body, name=name, grid=(rows // tr,), in_specs=[row, row], out_specs=[one, row],
        out_shape=[jax.ShapeDtypeStruct((1, 1), F32), jax.ShapeDtypeStruct((rows, dm), F32)],
        compiler_params=_cp(("arbitrary",)),
    )(y, t)


@jax.custom_vjp
def _loss_op(y, t):
    return _loss_call(y, t, "loss_head")[0][0, 0]


def _loss_fwd(y, t):
    l, d = _loss_call(y, t, "loss_head")
    return l[0, 0], d


def _loss_bwd(d, ct):
    return ct * d, jnp.zeros_like(d)


_loss_op.defvjp(_loss_fwd, _loss_bwd)


def _attn_blocks(S, Sk, cap=None):
    bq, bk = min(cap or ATTN_BQ, S), min(cap or ATTN_BK, Sk)
    assert S % bq == 0 and Sk % bk == 0
    return bq, bk


def _valid_t(i, j, bq, bk, strict):
    key = j * bk + lax.broadcasted_iota(jnp.int32, (bk, bq), 0)
    qry = i * bq + lax.broadcasted_iota(jnp.int32, (bk, bq), 1)
    return (key < qry) if strict else (key <= qry)


def _sm_fwd_t(qn, k, vT, cmul, causal, name):
    H, S, DK = qn.shape
    Sk, dv = k.shape[1], vT.shape[1]
    bq, bk = _attn_blocks(S, Sk)
    nq, nkb = S // bq, Sk // bk
    hb = PAIR * FWD_PAIRS if H % (PAIR * FWD_PAIRS) == 0 else 1
    heads = range(hb)
    if causal:
        assert S == Sk and bq == bk

    def body(qn_ref, k_ref, vT_ref, oT_ref, lse_ref):
        i = pl.program_id(1)
        qTs = [qn_ref[w].T for w in heads]

        def blk(j, carry, masked):
            off = pl.multiple_of(j * bk, bk)
            sT = [_dot(k_ref[w, pl.ds(off, bk), :], qTs[w]) * cmul for w in heads]
            if masked:
                valid = _valid_t(i, j, bq, bk, False)
                sT = [jnp.where(valid, s, NEG_BIG) for s in sT]
            m_new = [jnp.maximum(carry[w][0], jnp.max(sT[w], axis=0, keepdims=True)) for w in heads]
            p = [jnp.exp2(sT[w] - m_new[w]) for w in heads]
            a = [jnp.exp2(carry[w][0] - m_new[w]) for w in heads]
            l = [a[w] * carry[w][1] + jnp.sum(p[w], axis=0, keepdims=True) for w in heads]
            acc = [a[w] * carry[w][2] + _dot(vT_ref[w, :, pl.ds(off, bk)], p[w].astype(BF16)) for w in heads]
            return tuple((m_new[w], l[w], acc[w]) for w in heads)

        carry = tuple((jnp.full((1, bq), NEG_BIG, F32), jnp.zeros((1, bq), F32), jnp.zeros((dv, bq), F32))
                      for _ in heads)
        if causal:
            carry = lax.fori_loop(0, i, lambda j, c: blk(j, c, False), carry)
            carry = blk(i, carry, True)
        else:
            carry = lax.fori_loop(0, nkb, lambda j, c: blk(j, c, False), carry)
        for w in heads:
            oT_ref[w] = carry[w][2] / carry[w][1]
            lse_ref[w] = carry[w][0] + jnp.log2(carry[w][1])

    qcol = lambda d: pl.BlockSpec((hb, d, bq), lambda h, i: (h, 0, i))
    return pl.pallas_call(
        body, name=name, grid=(H // hb, nq),
        in_specs=[pl.BlockSpec((hb, bq, DK), lambda h, i: (h, i, 0)), pl.BlockSpec((hb, Sk, DK), lambda h, i: (h, 0, 0)),
                  pl.BlockSpec((hb, dv, Sk), lambda h, i: (h, 0, 0))],
        out_specs=[qcol(dv), qcol(1)],
        out_shape=[jax.ShapeDtypeStruct((H, dv, S), F32), jax.ShapeDtypeStruct((H, 1, S), F32)],
        compiler_params=_cp(("parallel", "arbitrary"), ATTN_VMEM_LIMIT),
    )(qn, k, vT)


def _sm_bwd_t(qn, k, v, oT, lse, doT, do, cmul, gscale, causal, name):
    H, S, DK = qn.shape
    Sk, dv = k.shape[1], v.shape[2]
    bq, bk = _attn_blocks(S, Sk)
    nq, nkb = S // bq, Sk // bk

    def body(qn_ref, k_ref, v_ref, oT_ref, lse_ref, doT_ref, do_ref, dq_ref, dk_ref, dv_ref):
        i = pl.program_id(1)

        @pl.when(i == 0)
        def _():
            dk_ref[...] = jnp.zeros_like(dk_ref)
            dv_ref[...] = jnp.zeros_like(dv_ref)

        qnb = qn_ref[...]
        qTb = qnb.T
        dob = do_ref[...]
        doTf = doT_ref[...]
        doTb = doTf.astype(BF16)
        delta = jnp.sum(doTf * oT_ref[...], axis=0, keepdims=True)
        lse = lse_ref[...]

        def blk(j, dq, masked):
            off = pl.multiple_of(j * bk, bk)
            kb = k_ref[pl.ds(off, bk), :]
            sT = _dot(kb, qTb) * cmul
            if masked:
                sT = jnp.where(_valid_t(i, j, bq, bk, False), sT, NEG_BIG)
            p = jnp.exp2(sT - lse)
            dp = _dot(v_ref[pl.ds(off, bk), :], doTb)
            ds = p * (dp - delta)
            dsb = (ds * gscale).astype(BF16) if gscale != 1.0 else ds.astype(BF16)
            dv_ref[pl.ds(off, bk), :] += _dot(p.astype(BF16), dob)
            dk_ref[pl.ds(off, bk), :] += _dot(dsb, qnb)
            return dq + _dot(kb.T, dsb)

        dq = jnp.zeros((DK, bq), F32)
        if causal:
            dq = lax.fori_loop(0, i, lambda j, c: blk(j, c, False), dq)
            dq = blk(i, dq, True)
        else:
            dq = lax.fori_loop(0, nkb, lambda j, c: blk(j, c, False), dq)
        dq_ref[...] = dq.T

    qcol = lambda d: pl.BlockSpec((None, d, bq), lambda h, i: (h, 0, i))
    qrow = lambda d: pl.BlockSpec((None, bq, d), lambda h, i: (h, i, 0))
    krow = lambda d: pl.BlockSpec((None, Sk, d), lambda h, i: (h, 0, 0))
    return pl.pallas_call(
        body, name=name, grid=(H, nq),
        in_specs=[qrow(DK), krow(DK), krow(dv), qcol(dv), qcol(1), qcol(dv), qrow(dv)],
        out_specs=[qrow(DK), krow(DK), krow(dv)],
        out_shape=[jax.ShapeDtypeStruct((H, S, DK), F32), jax.ShapeDtypeStruct((H, Sk, DK), F32),
                   jax.ShapeDtypeStruct((H, Sk, dv), F32)],
        compiler_params=_cp(("parallel", "arbitrary"), ATTN_VMEM_LIMIT),
    )(qn, k, v, oT, lse, doT, do)


def _tri(n, fn):
    r = lax.broadcasted_iota(jnp.int32, (n, n), 0)
    c = lax.broadcasted_iota(jnp.int32, (n, n), 1)
    return jnp.where(fn(r, c), 1.0, 0.0).astype(BF16)


def _key_cumsum(x, tri2, suffix, base):
    bk = x.shape[0]
    c = min(CUMSUM_CHUNK, bk)
    n = bk // c
    hi32 = lax.bitcast_convert_type(lax.bitcast_convert_type(x, jnp.int32) & jnp.int32(-65536), F32)
    hi = hi32.astype(BF16)
    lo = (x - hi32).astype(BF16)
    tot = [jnp.sum(x[a * c:(a + 1) * c], axis=0, keepdims=True) for a in range(n)]
    outs = []
    for a in range(n):
        row = base
        for t in (tot[a + 1:] if suffix else tot[:a]):
            row = row + t
        stacked = jnp.concatenate([hi[a * c:(a + 1) * c], lo[a * c:(a + 1) * c]], axis=0)
        outs.append(_dot(tri2, stacked) + row)
    total = tot[0]
    for t in tot[1:]:
        total = total + t
    return (outs[0] if n == 1 else jnp.concatenate(outs, axis=0)), total


def _tri2(n, fn):
    t = _tri(n, fn)
    return jnp.concatenate([t, t], axis=1)


def _sb_logs(z):
    neg_abs = lax.bitcast_convert_type(lax.bitcast_convert_type(z, jnp.int32) | jnp.int32(-2 ** 31), F32)
    ls = jnp.minimum(z, 0.0) - jnp.log(1.0 + jnp.exp(neg_abs))
    return ls, ls - z


PAIR = LANE // HEAD_DIM
FWD_PAIRS = 2
SB_DEAD = -110.0
SB_BLOCK = 256
FOX_DEAD = -160.0
FOX_BLOCK = 512


def _head_lanes(shape, w, axis):
    idx = lax.broadcasted_iota(jnp.int32, shape, axis)
    return (idx >= HEAD_DIM * w) & (idx < HEAD_DIM * (w + 1))


def _bias_rows(w, bq):
    row = lax.broadcasted_iota(jnp.int32, (LANE, bq), 0)
    return jnp.where((row >= 3 * w) & (row < 3 * w + 3), -1.0, 0.0).astype(BF16)


def _merge_pair(parts):
    return jnp.where(_head_lanes(parts[0].shape, 0, 0), parts[0], parts[1]).T


def _smp_fwd(q2, k2, v2, bias, r, causal, name, kstat=None):
    S, C = q2.shape
    Sk = k2.shape[0]
    bq, bk = _attn_blocks(S, Sk, FOX_BLOCK if bias is not None else None)
    nq, nkb, P = S // bq, Sk // bk, C // LANE
    gp = FWD_PAIRS if P % FWD_PAIRS == 0 else 1
    use_f = bias is not None
    if causal:
        assert S == Sk and bq == bk and use_f

    def body(*refs):
        if use_f:
            ks_ref, q_ref, k_ref, v_ref, b_ref, r_ref, o_ref, lse_ref, js_ref = refs
        else:
            q_ref, k_ref, v_ref, o_ref, lse_ref = refs
        i = pl.program_id(1)
        heads = range(PAIR * gp)
        lanes = [slice(LANE * (h // PAIR), LANE * (h // PAIR + 1)) for h in heads]
        qps = [q_ref[:, lanes[h]] for h in heads]
        qTs = [jnp.where(_head_lanes(qps[h].shape, h % PAIR, 1), qps[h], jnp.zeros_like(qps[h])).T for h in heads]
        if use_f:
            qf = [t.astype(F32) for t in qTs]
            qnorm = [jnp.sqrt(jnp.sum(t * t, axis=0, keepdims=True)) for t in qf]
            qTs = [jnp.concatenate([qTs[h], _bias_rows(h % PAIR, bq)], axis=0) for h in heads]

        def blk(j, carry, masked):
            off = pl.multiple_of(j * bk, bk)
            kbs = [k_ref[pl.ds(off, bk), LANE * g:LANE * (g + 1)] for g in range(gp)]
            if use_f:
                kbs = [jnp.concatenate([kbs[g], b_ref[pl.ds(off, bk), LANE * g:LANE * (g + 1)]], axis=1)
                       for g in range(gp)]
            vTbs = [v_ref[pl.ds(off, bk), LANE * g:LANE * (g + 1)].T for g in range(gp)]
            sT = [_dot(kbs[h // PAIR], qTs[h]) * LOG2E for h in heads]
            if masked:
                valid = _valid_t(i, j, bq, bk, False)
                sT = [jnp.where(valid, s, NEG_BIG) for s in sT]
            cm = [jnp.max(s, axis=0, keepdims=True) for s in sT]
            if use_f:
                cm = [cm[h] + r_ref[h] for h in heads]
            m_new = [jnp.maximum(carry[h][0], cm[h]) for h in heads]
            shift = [(m_new[h] - r_ref[h]) if use_f else m_new[h] for h in heads]
            p = [jnp.exp2(sT[h] - shift[h]) for h in heads]
            a = [jnp.exp2(carry[h][0] - m_new[h]) for h in heads]
            l = [a[h] * carry[h][1] + jnp.sum(p[h], axis=0, keepdims=True) for h in heads]
            acc = [a[h] * carry[h][2] + _dot(vTbs[h // PAIR], p[h].astype(BF16)) for h in heads]
            return tuple((m_new[h], l[h], acc[h]) for h in heads)

        def step(jj, state):
            carry, first = state
            j = i - jj
            h0 = pl.program_id(0) * (PAIR * gp)
            bound = [LOG2E * (qnorm[h] * ks_ref[(h0 + h) * nkb + j] - ks_ref[(PAIR * P + h0 + h) * nkb + j])
                     + r_ref[h] - carry[h][0] for h in heads]
            live = jnp.max(functools.reduce(jnp.maximum, bound)) >= FOX_DEAD
            carry = lax.cond(live, lambda cr: blk(j, cr, False), lambda cr: cr, carry)
            return carry, jnp.where(live, j, first)

        carry = tuple((jnp.full((1, bq), NEG_BIG, F32), jnp.zeros((1, bq), F32), jnp.zeros((LANE, bq), F32))
                      for _ in heads)
        if causal:
            carry = blk(i, carry, True)
            carry, first = lax.fori_loop(1, i + 1, step, (carry, i))
            js_ref[0] = jnp.full((1, bq), first, jnp.int32)
        else:
            carry = lax.fori_loop(0, nkb, lambda j, c: blk(j, c, False), carry)
            if use_f:
                js_ref[0] = jnp.zeros((1, bq), jnp.int32)
        for h in heads:
            lse_ref[h] = carry[h][0] + jnp.log2(carry[h][1])
        for g in range(gp):
            o_ref[:, LANE * g:LANE * (g + 1)] = _merge_pair(
                [carry[h][2] / carry[h][1] for h in range(PAIR * g, PAIR * (g + 1))])

    qblk = pl.BlockSpec((bq, LANE * gp), lambda p, i: (i, p))
    kres = pl.BlockSpec((Sk, LANE * gp), lambda p, i: (0, p))
    stat = pl.BlockSpec((PAIR * gp, 1, bq), lambda p, i: (p, 0, i))
    in_specs = [qblk, kres, kres]
    args = [q2, k2, v2]
    out_specs = [qblk, stat]
    out_shape = [jax.ShapeDtypeStruct((S, C), F32), jax.ShapeDtypeStruct((PAIR * P, 1, S), F32)]
    if use_f:
        in_specs = [pl.BlockSpec(memory_space=pltpu.SMEM)] + in_specs + [kres, stat]
        args = [kstat] + args + [bias, r]
        out_specs.append(pl.BlockSpec((1, 1, bq), lambda p, i: (p, 0, i)))
        out_shape.append(jax.ShapeDtypeStruct((P // gp, 1, S), jnp.int32))
    return pl.pallas_call(
        body, name=name, grid=(P // gp, nq), in_specs=in_specs, out_specs=out_specs, out_shape=out_shape,
        compiler_params=_cp(("parallel", "arbitrary"), ATTN_VMEM_LIMIT),
    )(*args)


def _smp_bwd(q2, k2, v2, o2, lse, do2, bias, r, scale, causal, name, first=None):
    S, C = q2.shape
    Sk = k2.shape[0]
    bq, bk = _attn_blocks(S, Sk, FOX_BLOCK if bias is not None else None)
    nq, nkb, P = S // bq, Sk // bk, C // LANE
    use_f = bias is not None

    def body(*refs):
        if use_f:
            (first_ref, q_ref, k_ref, v_ref, o_ref, lse_ref, do_ref, b_ref, r_ref,
             dq_ref, dk_ref, dv_ref, dr_ref, dkey_ref, dk_acc, dv_acc, db_ref) = refs
        else:
            q_ref, k_ref, v_ref, o_ref, lse_ref, do_ref, dq_ref, dk_ref, dv_ref, dk_acc, dv_acc = refs
        i = pl.program_id(1)

        @pl.when(i == 0)
        def _():
            dk_acc[...] = jnp.zeros_like(dk_acc)
            dv_acc[...] = jnp.zeros_like(dv_acc)
            if use_f:
                db_ref[...] = jnp.zeros_like(db_ref)

        qp = q_ref[...]
        dof = do_ref[...]
        prod = dof * o_ref[...]
        heads = range(PAIR)
        mine = [_head_lanes(qp.shape, w, 1) for w in heads]
        qz = [jnp.where(mine[w], qp, jnp.zeros_like(qp)) for w in heads]
        qTs = [qz[w].T for w in heads]
        if use_f:
            qTs = [jnp.concatenate([qTs[w], _bias_rows(w, bq)], axis=0) for w in heads]
        doz = [jnp.where(mine[w], dof, 0.0).astype(BF16) for w in heads]
        doT = [doz[w].T for w in heads]
        delta = [jnp.sum(jnp.where(mine[w], prod, 0.0).T, axis=0, keepdims=True) for w in heads]
        shift = [(lse_ref[w] - r_ref[w]) if use_f else lse_ref[w] for w in heads]

        def blk(j, carry, masked):
            off = pl.multiple_of(j * bk, bk)
            kb = k_ref[pl.ds(off, bk), :]
            kTb = kb.T
            if use_f:
                kb = jnp.concatenate([kb, b_ref[pl.ds(off, bk), :]], axis=1)
            vb = v_ref[pl.ds(off, bk), :]
            sT = [_dot(kb, qTs[w]) * LOG2E for w in heads]
            if masked:
                valid = _valid_t(i, j, bq, bk, False)
                sT = [jnp.where(valid, s, NEG_BIG) for s in sT]
            p = [jnp.exp2(sT[w] - shift[w]) for w in heads]
            dp = [_dot(vb, doT[w]) for w in heads]
            ds = [p[w] * (dp[w] - delta[w]) for w in heads]
            dsb = [d.astype(BF16) for d in ds]
            dvs = [_dot(p[w].astype(BF16), doz[w]) for w in heads]
            dks = [_dot(dsb[w], qz[w]) for w in heads]
            dv_acc[pl.ds(off, bk), :] += dvs[0] + dvs[1]
            dk_acc[pl.ds(off, bk), :] += dks[0] + dks[1]
            dr = [carry[w][1] for w in heads]
            if use_f:
                dr = [dr[w] + jnp.sum(ds[w], axis=0, keepdims=True) for w in heads]
                lane = lax.broadcasted_iota(jnp.int32, (bk, LANE), 1)
                cols = [jnp.where(lane == w, jnp.sum(ds[w], axis=1, keepdims=True), 0.0) for w in heads]
                db_ref[pl.ds(off, bk), :] += cols[0] + cols[1]
            dq = [carry[w][0] + _dot(kTb, dsb[w]) for w in heads]
            return tuple((dq[w], dr[w]) for w in heads)

        carry = tuple((jnp.zeros((LANE, bq), F32), jnp.zeros((1, bq), F32)) for _ in heads)
        if causal:
            start = first_ref[pl.program_id(0) // (P // first.shape[0]), i]
            carry = lax.fori_loop(start, i, lambda j, c: blk(j, c, False), carry)
            carry = blk(i, carry, True)
        else:
            carry = lax.fori_loop(0, nkb, lambda j, c: blk(j, c, False), carry)
        if use_f:
            for w in heads:
                dr_ref[w] = carry[w][1]
        dq_ref[...] = (_merge_pair([carry[w][0] for w in heads]) * scale).astype(BF16)

        @pl.when(i == nq - 1)
        def _():
            dk_ref[...] = dk_acc[...].astype(BF16)
            dv_ref[...] = dv_acc[...].astype(BF16)

        if use_f:
            @pl.when(i == nq - 1)
            def _():
                def chunk(cidx, carry):
                    off = pl.multiple_of(cidx * LANE, LANE)
                    t = db_ref[pl.ds(off, LANE), :].T
                    for w in range(PAIR):
                        dkey_ref[w, :, pl.ds(off, LANE)] = t[w:w + 1, :]
                    return carry

                lax.fori_loop(0, Sk // LANE, chunk, 0)

    qblk = pl.BlockSpec((bq, LANE), lambda p, i: (i, p))
    kres = pl.BlockSpec((Sk, LANE), lambda p, i: (0, p))
    stat = pl.BlockSpec((PAIR, 1, bq), lambda p, i: (p, 0, i))
    in_specs = [qblk, kres, kres, qblk, stat, qblk]
    args = [q2, k2, v2, o2, lse, do2]
    out_specs = [qblk, kres, kres]
    out_shape = [jax.ShapeDtypeStruct((S, C), BF16), jax.ShapeDtypeStruct((Sk, C), BF16),
                 jax.ShapeDtypeStruct((Sk, C), BF16)]
    scratch = [pltpu.VMEM((Sk, LANE), F32), pltpu.VMEM((Sk, LANE), F32)]
    if use_f:
        in_specs = [pl.BlockSpec(memory_space=pltpu.SMEM)] + in_specs + [kres, stat]
        args = [first] + args + [bias, r]
        out_specs += [stat, pl.BlockSpec((PAIR, 1, Sk), lambda p, i: (p, 0, 0))]
        out_shape += [jax.ShapeDtypeStruct((PAIR * P, 1, S), F32), jax.ShapeDtypeStruct((PAIR * P, 1, Sk), F32)]
        scratch.append(pltpu.VMEM((Sk, LANE), F32))
    return pl.pallas_call(
        body, name=name, grid=(P, nq), in_specs=in_specs, out_specs=out_specs, out_shape=out_shape,
        scratch_shapes=scratch, compiler_params=_cp(("parallel", "arbitrary"), ATTN_VMEM_LIMIT),
    )(*args)


def _sbp_fwd(q2, k2, v2, name):
    S, C = q2.shape
    bq, bk = _attn_blocks(S, S, SB_BLOCK)
    assert bq == bk
    nq, P = S // bq, C // LANE
    gp = FWD_PAIRS if P % FWD_PAIRS == 0 else 1
    c = min(CUMSUM_CHUNK, bk)

    def body(q_ref, k_ref, v_ref, o_ref, lt_ref, js_ref):
        i = pl.program_id(1)
        after = _tri2(c, lambda s, j: j > s)
        heads = range(PAIR * gp)
        qps = [q_ref[:, LANE * (h // PAIR):LANE * (h // PAIR + 1)] for h in heads]
        qTs = [jnp.where(_head_lanes(qps[h].shape, h % PAIR, 1), qps[h], jnp.zeros_like(qps[h])).T for h in heads]

        def blk(jj, carry, masked):
            j = i - jj
            off = pl.multiple_of(j * bk, bk)
            kbs = [k_ref[pl.ds(off, bk), LANE * g:LANE * (g + 1)] for g in range(gp)]
            vTbs = [v_ref[pl.ds(off, bk), LANE * g:LANE * (g + 1)].T for g in range(gp)]
            logs = [_sb_logs(_dot(kbs[h // PAIR], qTs[h])) for h in heads]
            ls, lk = [t[0] for t in logs], [t[1] for t in logs]
            if masked:
                valid = _valid_t(i, j, bq, bk, True)
                lk = [jnp.where(valid, t, 0.0) for t in lk]
            cs = [_key_cumsum(lk[h], after, True, carry[h][0]) for h in heads]
            wgt = [jnp.exp(ls[h] + cs[h][0]) for h in heads]
            if masked:
                wgt = [jnp.where(valid, t, 0.0) for t in wgt]
            acc = [carry[h][1] + _dot(vTbs[h // PAIR], wgt[h].astype(BF16)) for h in heads]
            return tuple((carry[h][0] + cs[h][1], acc[h]) for h in heads)

        def step(jj, state):
            carry, first = state
            live = jnp.max(functools.reduce(jnp.maximum, [carry[h][0] for h in heads])) >= SB_DEAD
            carry = lax.cond(live, lambda cr: blk(jj, cr, False), lambda cr: cr, carry)
            return carry, jnp.where(live, i - jj, first)

        carry = tuple((jnp.zeros((1, bq), F32), jnp.zeros((LANE, bq), F32)) for _ in heads)
        carry = blk(0, carry, True)
        carry, first = lax.fori_loop(1, i + 1, step, (carry, i))
        js_ref[0] = jnp.full((1, bq), first, jnp.int32)
        for h in heads:
            lt_ref[h] = carry[h][0]
        for g in range(gp):
            o_ref[:, LANE * g:LANE * (g + 1)] = _merge_pair([carry[h][1] for h in range(PAIR * g, PAIR * (g + 1))])

    qblk = pl.BlockSpec((bq, LANE * gp), lambda p, i: (i, p))
    kres = pl.BlockSpec((S, LANE * gp), lambda p, i: (0, p))
    stat = pl.BlockSpec((PAIR * gp, 1, bq), lambda p, i: (p, 0, i))
    return pl.pallas_call(
        body, name=name, grid=(P // gp, nq),
        in_specs=[qblk, kres, kres],
        out_specs=[qblk, stat, pl.BlockSpec((1, 1, bq), lambda p, i: (p, 0, i))],
        out_shape=[jax.ShapeDtypeStruct((S, C), F32), jax.ShapeDtypeStruct((PAIR * P, 1, S), F32),
                   jax.ShapeDtypeStruct((P // gp, 1, S), jnp.int32)],
        compiler_params=_cp(("parallel", "arbitrary"), ATTN_VMEM_LIMIT),
    )(q2, k2, v2)


def _sbp_bwd(q2, k2, v2, lt, first, do2, scale, name):
    S, C = q2.shape
    bq, bk = _attn_blocks(S, S, SB_BLOCK)
    nq, P = S // bq, C // LANE
    c = min(CUMSUM_CHUNK, bk)

    per_group = P // first.shape[0]

    def body(first_ref, q_ref, k_ref, v_ref, lt_ref, do_ref, dq_ref, dk_ref, dv_ref, dk_acc, dv_acc):
        i = pl.program_id(1)

        @pl.when(i == 0)
        def _():
            dk_acc[...] = jnp.zeros_like(dk_acc)
            dv_acc[...] = jnp.zeros_like(dv_acc)

        qp = q_ref[...]
        dof = do_ref[...]
        upto = _tri2(c, lambda s, j: j <= s)
        before = _tri2(c, lambda s, j: j < s)
        heads = range(PAIR)
        mine = [_head_lanes(qp.shape, w, 1) for w in heads]
        qz = [jnp.where(mine[w], qp, jnp.zeros_like(qp)) for w in heads]
        qTs = [qz[w].T for w in heads]
        doz = [jnp.where(mine[w], dof, 0.0).astype(BF16) for w in heads]
        doT = [doz[w].T for w in heads]
        ltot = [lt_ref[w] for w in heads]

        def blk(j, carry, masked):
            off = pl.multiple_of(j * bk, bk)
            kb = k_ref[pl.ds(off, bk), :]
            vb = v_ref[pl.ds(off, bk), :]
            kTb = kb.T
            logs = [_sb_logs(_dot(kb, qTs[w])) for w in heads]
            ls, lk = [t[0] for t in logs], [t[1] for t in logs]
            if masked:
                valid = _valid_t(i, j, bq, bk, True)
                lk = [jnp.where(valid, t, 0.0) for t in lk]
            pin = [_key_cumsum(lk[w], upto, False, carry[w][1] - ltot[w]) for w in heads]
            wgt = [jnp.exp(ls[w] - pin[w][0]) for w in heads]
            if masked:
                wgt = [jnp.where(valid, t, 0.0) for t in wgt]
            g = [_dot(vb, doT[w]) * wgt[w] for w in heads]
            cin = [_key_cumsum(g[w], before, False, carry[w][2]) for w in heads]
            sig = [jnp.exp(t) for t in ls]
            dz = [g[w] * (1.0 - sig[w]) - cin[w][0] * sig[w] for w in heads]
            if masked:
                dz = [jnp.where(valid, t, 0.0) for t in dz]
            dzb = [t.astype(BF16) for t in dz]
            dvs = [_dot(wgt[w].astype(BF16), doz[w]) for w in heads]
            dks = [_dot(dzb[w], qz[w]) for w in heads]
            dv_acc[pl.ds(off, bk), :] += dvs[0] + dvs[1]
            dk_acc[pl.ds(off, bk), :] += dks[0] + dks[1]
            return tuple((carry[w][0] + _dot(kTb, dzb[w]), carry[w][1] + pin[w][1], carry[w][2] + cin[w][1])
                         for w in heads)

        carry = tuple((jnp.zeros((LANE, bq), F32), jnp.zeros((1, bq), F32), jnp.zeros((1, bq), F32)) for _ in heads)
        start = first_ref[pl.program_id(0) // per_group, i]
        carry = lax.fori_loop(start, i, lambda j, cr: blk(j, cr, False), carry)
        carry = blk(i, carry, True)
        dq_ref[...] = (_merge_pair([carry[w][0] for w in heads]) * scale).astype(BF16)

        @pl.when(i == nq - 1)
        def _():
            dk_ref[...] = dk_acc[...].astype(BF16)
            dv_ref[...] = dv_acc[...].astype(BF16)

    qblk = pl.BlockSpec((bq, LANE), lambda p, i: (i, p))
    kres = pl.BlockSpec((S, LANE), lambda p, i: (0, p))
    stat = pl.BlockSpec((PAIR, 1, bq), lambda p, i: (p, 0, i))
    return pl.pallas_call(
        body, name=name, grid=(P, nq),
        in_specs=[pl.BlockSpec(memory_space=pltpu.SMEM), qblk, kres, kres, stat, qblk],
        out_specs=[qblk, kres, kres],
        out_shape=[jax.ShapeDtypeStruct((S, C), BF16)] * 3,
        scratch_shapes=[pltpu.VMEM((S, LANE), F32), pltpu.VMEM((S, LANE), F32)],
        compiler_params=_cp(("parallel", "arbitrary"), ATTN_VMEM_LIMIT),
    )(first, q2, k2, v2, lt, do2)


def _bias_cols(f_cum):
    H, Sk = f_cum.shape
    terms = jnp.stack(_split3(f_cum), axis=-1)
    packed = terms.reshape(H // PAIR, PAIR, Sk, 3).transpose(2, 0, 1, 3).reshape(Sk, H // PAIR, PAIR * 3)
    return jnp.pad(packed, ((0, 0), (0, 0), (0, LANE - PAIR * 3))).reshape(Sk, -1)


def _make_packed_softmax(name, scale, causal, use_f):
    assert _pow2(scale)

    def run_fwd(q16, k16, v16, f_cum):
        q16 = q16 * scale
        if not use_f:
            o, lse = _smp_fwd(q16, k16, v16, None, None, causal, name + "_fwd")
            return o, (q16, k16, v16, o, lse, None, None, None)
        bq, bk = _attn_blocks(q16.shape[0], k16.shape[0], FOX_BLOCK)
        n_heads = f_cum.shape[0]
        knorm = jnp.sqrt(jnp.sum(jnp.square(k16.astype(F32)).reshape(-1, bk, n_heads, HEAD_DIM), axis=3))
        kstat = jnp.concatenate([jnp.max(knorm, axis=1).T.reshape(-1), f_cum[:, bk - 1::bk].reshape(-1)])
        bias, r = _bias_cols(f_cum), (f_cum * LOG2E)[:, None, :]
        o, lse, first = _smp_fwd(q16, k16, v16, bias, r, causal, name + "_fwd", lax.stop_gradient(kstat))
        return o, (q16, k16, v16, o, lse, bias, r, first[:, 0, ::bq])

    def run_bwd(saved, do):
        q16, k16, v16, o, lse, bias, r, first = saved
        outs = _smp_bwd(q16, k16, v16, o, lse, do, bias, r, scale, causal, name + "_bwd", first)
        if use_f:
            return outs[0], outs[1], outs[2], outs[3][:, 0, :] - outs[4][:, 0, :]
        return tuple(outs)

    if use_f:
        @jax.custom_vjp
        def attn(q, k, v, f_cum):
            return run_fwd(q, k, v, f_cum)[0]

        attn.defvjp(run_fwd, run_bwd)
    else:
        @jax.custom_vjp
        def attn(q, k, v):
            return run_fwd(q, k, v, None)[0]

        attn.defvjp(lambda q, k, v: run_fwd(q, k, v, None), run_bwd)
    return attn


def _make_packed_sb(name, scale):
    assert _pow2(scale)

    def run_fwd(q16, k16, v16):
        q16 = q16 * scale
        o, lt, first = _sbp_fwd(q16, k16, v16, name + "_fwd")
        bq, _ = _attn_blocks(q16.shape[0], q16.shape[0], SB_BLOCK)
        return o, (q16, k16, v16, lt, first[:, 0, ::bq])

    def run_bwd(saved, do):
        q16, k16, v16, lt, first = saved
        return tuple(_sbp_bwd(q16, k16, v16, lt, first, do, scale, name + "_bwd"))

    @jax.custom_vjp
    def attn(q, k, v):
        return run_fwd(q, k, v)[0]

    attn.defvjp(run_fwd, run_bwd)
    return attn


def _round_bf16(x):
    return lax.reduce_precision(x, exponent_bits=8, mantissa_bits=7)


def _split3(x):
    hi = _round_bf16(x)
    mid = _round_bf16(x - hi)
    lo = _round_bf16(x - hi - mid)
    return hi.astype(BF16), mid.astype(BF16), lo.astype(BF16)


def _pow2(x):
    m, _ = math.frexp(x)
    return m == 0.5


def _pad_last(x, n):
    return jnp.pad(x, [(0, 0)] * (x.ndim - 1) + [(0, n - x.shape[-1])])


def _layouts(q, k, scale):
    qh = _pad_last(jnp.transpose(q * scale if _pow2(scale) else q, (1, 0, 2)).astype(BF16), LANE)
    return qh, _pad_last(jnp.transpose(k, (1, 0, 2)).astype(BF16), LANE)


def _make_softmax_attn(name, scale, causal, d):
    pre = _pow2(scale)
    cmul = LOG2E if pre else scale * LOG2E
    gscale = 1.0 if pre else scale

    def run_fwd(q, k, v):
        qn, kn = _layouts(q, k, scale)
        vn = jnp.transpose(v, (1, 0, 2)).astype(BF16)
        oT, lse = _sm_fwd_t(qn, kn, jnp.transpose(vn, (0, 2, 1)), cmul, causal, name + "_fwd")
        return jnp.transpose(oT, (2, 0, 1)), (qn, kn, vn, oT, lse)

    def run_bwd(saved, dout):
        qn, kn, vn, oT, lse = saved
        doT = jnp.transpose(dout, (1, 2, 0))
        do = jnp.transpose(dout, (1, 0, 2)).astype(BF16)
        dq, dk, dv = _sm_bwd_t(qn, kn, vn, oT, lse, doT, do, cmul, gscale, causal, name + "_bwd")
        dq = jnp.transpose(dq[:, :, :d], (1, 0, 2))
        if pre:
            dq = dq * scale
        return dq, jnp.transpose(dk[:, :, :d], (1, 0, 2)), jnp.transpose(dv, (1, 0, 2))

    @jax.custom_vjp
    def attn(q, k, v):
        return run_fwd(q, k, v)[0]

    attn.defvjp(run_fwd, run_bwd)
    return attn


def _rope(x, positions):
    half = x.shape[-1] // 2
    inv_freq = ROPE_THETA ** (-jnp.arange(half, dtype=F32) / half)
    ang = positions.astype(F32)[:, None] * inv_freq[None, :]
    ang = ang.reshape((ang.shape[0],) + (1,) * (x.ndim - 2) + (half,))
    cos, sin = jnp.cos(ang), jnp.sin(ang)
    x1, x2 = x[..., :half], x[..., half:]
    return jnp.concatenate([x1 * cos - x2 * sin, x1 * sin + x2 * cos], axis=-1)


def _permute_cols(w):
    parts = [w[..., _ORIG_OFF[idx]:_ORIG_OFF[idx] + SPLIT_SIZES[idx]] for _, idx in _PERM]
    pad = jnp.zeros(w.shape[:-1] + (PROJ_COLS - IN_COLS,), w.dtype)
    return jnp.concatenate(parts + [pad], axis=-1)


def _unpermute_cols(w):
    start, parts = 0, [None] * len(SPLIT_SIZES)
    for _, idx in _PERM:
        parts[idx] = w[..., start:start + SPLIT_SIZES[idx]]
        start += SPLIT_SIZES[idx]
    return jnp.concatenate(parts, axis=-1)


_BF16_PIECES = ("fq", "fk", "fv", "sq", "sk", "sv", "mq")


def _make_ln_proj(name, has_res):
    wide = [(n, SPLIT_SIZES[idx]) for n, idx in _PERM if SPLIT_SIZES[idx] % LANE == 0]
    narrow = [(n, SPLIT_SIZES[idx]) for n, idx in _PERM if SPLIT_SIZES[idx] % LANE]
    assert [n for n, _ in wide + narrow] == [n for n, _ in _PERM]
    tail = PROJ_COLS - sum(w for _, w in wide)

    def run_fwd(x, res, g, b, w):
        h, h16, hT16 = _ln_fwd_call(x, res, g, b, name + "_ln_fwd", also16=True)
        w16 = w.astype(BF16)
        outs = _matmul_split(h16, w16, [w for _, w in wide] + [tail],
                             [BF16 if n in _BF16_PIECES else F32 for n, _ in wide] + [F32], name + "_fwd")
        pieces, off = list(outs[:-1]), 0
        for _, width in narrow:
            pieces.append(outs[-1][:, off:off + width])
            off += width
        return (h, tuple(pieces)), (x, res, g, hT16, w16)

    def run_bwd(saved, cts):
        x, res, g, hT16, w16 = saved
        dh, dpieces = cts
        wide = [c for c in dpieces if c.shape[1] % LANE == 0]
        narrow = [c.astype(BF16) for c in dpieces if c.shape[1] % LANE]
        pad = jnp.zeros((x.shape[0], PROJ_COLS - IN_COLS), BF16)
        da, dy16 = _concat_matmul_nt(wide + [jnp.concatenate(narrow + [pad], axis=1)], w16, name + "_dx")
        dw = _matmul(hT16, dy16, "nn", name + "_dw")
        outs = _ln_bwd_call(dh, x, res, g, name + "_ln_bwd", dy2=da)
        if has_res:
            dx, dr, dg, db = outs
            return dx, dr, dg.reshape(-1), db.reshape(-1), dw
        dx, dg, db = outs
        return dx, dg.reshape(-1), db.reshape(-1), dw

    if has_res:
        @jax.custom_vjp
        def op(x, res, g, b, w):
            return run_fwd(x, res, g, b, w)[0]

        op.defvjp(run_fwd, run_bwd)
    else:
        @jax.custom_vjp
        def op(x, g, b, w):
            return run_fwd(x, None, g, b, w)[0]

        op.defvjp(lambda x, g, b, w: run_fwd(x, None, g, b, w), run_bwd)

    def call(*args):
        h, pieces = op(*args)
        return h, {n: part for (n, _), part in zip(_PERM, pieces)}

    return call


def _trunk_loss(wts, x2d, mem2d, target2d):
    s = x2d.shape[0]
    positions = jnp.arange(s)
    head_scale = HEAD_DIM ** -0.5
    mla_scale = (MLA_NOPE + MLA_ROPE) ** -0.5

    mem_n = _make_ln("ln_mem", False)(mem2d, wts["mem_ln_g"], wts["mem_ln_b"])
    h, y = None, x2d
    for l in range(DEPTH):
        tag = f"l{l}_"
        w_p = wts["w_in"][l]
        if l == 0:
            h, p = _make_ln_proj(tag + "proj", False)(y, wts["ln_in_g"], wts["ln_in_b"], w_p)
        else:
            h, p = _make_ln_proj(tag + "proj", True)(y, h, wts["ln_g"][l - 1], wts["ln_b"][l - 1], w_p)

        log_f = jax.nn.log_sigmoid(p["f_logit"].T + wts["b_forget"][l][:, None])
        f_cum = jnp.cumsum(log_f, axis=1)
        out_fox = _make_packed_softmax(tag + "fox", head_scale, True, True)(p["fq"], p["fk"], p["fv"], f_cum)

        out_sb = _make_packed_sb(tag + "sb", head_scale)(p["sq"], p["sk"], p["sv"])

        cqn = _make_rms(tag + "rms_q")(p["c_q"], wts["mla_q_norm_g"][l])
        q_mla = _make_mm(tag + "q_up")(cqn, wts["w_mla_q_up"][l]).reshape(s, N_HEADS, MLA_NOPE + MLA_ROPE)
        ckvn = _make_rms(tag + "rms_kv")(p["c_kv"], wts["mla_kv_norm_g"][l])
        kv_mla = _make_mm(tag + "kv_up")(ckvn, wts["w_mla_kv_up"][l]).reshape(s, N_HEADS, MLA_NOPE + MLA_V)
        q_full = jnp.concatenate([q_mla[..., :MLA_NOPE], _rope(q_mla[..., MLA_NOPE:], positions)], axis=-1)
        k_rope = jnp.broadcast_to(_rope(p["k_rot"], positions)[:, None, :], (s, N_HEADS, MLA_ROPE))
        k_full = jnp.concatenate([kv_mla[..., :MLA_NOPE], k_rope], axis=-1)
        out_mla = _make_softmax_attn(tag + "mla", mla_scale, True, MLA_NOPE + MLA_ROPE)(
            q_full, k_full, kv_mla[..., MLA_NOPE:]).reshape(s, GROUP_W)

        mkv = _make_mm(tag + "mem_kv")(mem_n, wts["w_mem_kv"][l])
        out_mem = _make_packed_softmax(tag + "mem", head_scale, False, False)(
            p["mq"], mkv[:, :GROUP_W].astype(BF16), mkv[:, GROUP_W:].astype(BF16))

        y = _make_gate_out(tag + "out")((out_fox, out_sb, out_mla, out_mem), p["gate"], wts["w_out"][l])

    h = _make_ln(f"l{DEPTH - 1}_ln", True)(y, h, wts["ln_g"][DEPTH - 1], wts["ln_b"][DEPTH - 1])
    return _loss_op(h, target2d)


def _mesh_pos():
    x, y, c = (lax.axis_index(a) for a in MESH_AXES)
    return x, y, c, 4 * x + 2 * y + c


def _peer(x, y, c, mask):
    return (x ^ ((mask >> 2) & 1), y ^ ((mask >> 1) & 1), c ^ (mask & 1))


_ANY = pl.BlockSpec(memory_space=pl.ANY)


def _all_gather(row_shards, stack_shards):
    n_row, n_all = len(row_shards), len(row_shards) + len(stack_shards)
    shards = list(row_shards) + list(stack_shards)
    chip_masks = (4, 2, 6)
    tensors = range(n_all)

    def body(*refs):
        ins, outs = refs[:n_all], refs[n_all:2 * n_all]
        send_sems, recv_sems, local_sems = refs[2 * n_all:]
        x, y, c, me = _mesh_pos()
        sibling = _peer(x, y, c, 1)

        def window(t, slot):
            if t < n_row:
                rows = shards[t].shape[1]
                return outs[t].at[:, pl.ds(slot * rows, rows), :]
            return outs[t].at[slot]

        def copy(t, k, slot, to, src=None):
            return pltpu.make_async_remote_copy(
                src_ref=window(t, slot) if src is None else src, dst_ref=window(t, slot),
                send_sem=send_sems.at[t, k], recv_sem=recv_sems.at[t, k], device_id=to,
                device_id_type=pl.DeviceIdType.MESH)

        local = [pltpu.make_async_copy(ins[t], window(t, me), local_sems.at[t]) for t in tensors]
        for cp in local:
            cp.start()
        first = [copy(t, 0, me, sibling, src=ins[t]) for t in tensors]
        first += [copy(t, 1 + j, me, _peer(x, y, c, m), src=ins[t]) for j, m in enumerate(chip_masks) for t in tensors]
        for cp in first:
            cp.start()
        passed = []
        for j, m in enumerate(chip_masks):
            for t in tensors:
                copy(t, 1 + j, me ^ m, sibling).wait_recv()
            for t in tensors:
                cp = copy(t, 4 + j, me ^ m, sibling)
                cp.start()
                passed.append(cp)
        for t in tensors:
            copy(t, 0, me ^ 1, sibling).wait_recv()
        for j, m in enumerate(chip_masks):
            for t in tensors:
                copy(t, 4 + j, me ^ m ^ 1, sibling).wait_recv()
        for cp in first + passed:
            cp.wait_send()
        for cp in local:
            cp.wait()

    out_shape = [jax.ShapeDtypeStruct((a.shape[0], N_DEV * a.shape[1], a.shape[2]), a.dtype) for a in row_shards]
    out_shape += [jax.ShapeDtypeStruct((N_DEV,) + a.shape, a.dtype) for a in stack_shards]
    return pl.pallas_call(
        body, name="all_gather_weights", in_specs=[_ANY] * n_all, out_specs=[_ANY] * n_all, out_shape=out_shape,
        scratch_shapes=[pltpu.SemaphoreType.DMA((n_all, N_DEV - 1)), pltpu.SemaphoreType.DMA((n_all, N_DEV - 1)),
                        pltpu.SemaphoreType.DMA((n_all,))],
    )(*shards)


def _reduce_scatter(row_full, stack_full, bcast):
    n_row, n_stack = len(row_full), len(stack_full)
    n_all = n_row + n_stack + len(bcast)
    fulls = list(row_full) + list(stack_full) + list(bcast)

    def body(*refs):
        ins, outs = refs[:n_all], refs[n_all:2 * n_all]
        send_sems, recv_sems, local_sems = refs[2 * n_all:]
        x, y, c, me = _mesh_pos()

        def part(t, slot):
            if t < n_row:
                rows = fulls[t].shape[1] // N_DEV
                return ins[t].at[:, pl.ds(slot * rows, rows), :]
            if t < n_row + n_stack:
                return ins[t].at[slot]
            return ins[t]

        local = [pltpu.make_async_copy(part(t, me), outs[t].at[me], local_sems.at[t]) for t in range(n_all)]
        for cp in local:
            cp.start()
        sends = []
        for mask in range(1, N_DEV):
            for t in range(n_all):
                cp = pltpu.make_async_remote_copy(
                    src_ref=part(t, me ^ mask), dst_ref=outs[t].at[me], send_sem=send_sems.at[t, mask - 1],
                    recv_sem=recv_sems.at[t, mask - 1], device_id=_peer(x, y, c, mask),
                    device_id_type=pl.DeviceIdType.MESH)
                cp.start()
                sends.append(cp)
        for mask in range(1, N_DEV):
            for t in range(n_all):
                pltpu.make_async_remote_copy(
                    src_ref=part(t, me), dst_ref=outs[t].at[me ^ mask], send_sem=send_sems.at[t, mask - 1],
                    recv_sem=recv_sems.at[t, mask - 1], device_id=_peer(x, y, c, mask),
                    device_id_type=pl.DeviceIdType.MESH).wait_recv()
        for cp in sends:
            cp.wait_send()
        for cp in local:
            cp.wait()

    out_shape = [jax.ShapeDtypeStruct((N_DEV, a.shape[0], a.shape[1] // N_DEV, a.shape[2]), a.dtype) for a in row_full]
    out_shape += [jax.ShapeDtypeStruct(a.shape, a.dtype) for a in stack_full]
    out_shape += [jax.ShapeDtypeStruct((N_DEV,) + a.shape, a.dtype) for a in bcast]
    return pl.pallas_call(
        body, name="reduce_scatter_grads", in_specs=[_ANY] * n_all, out_specs=[_ANY] * n_all, out_shape=out_shape,
        scratch_shapes=[pltpu.SemaphoreType.DMA((n_all, N_DEV - 1)), pltpu.SemaphoreType.DMA((n_all, N_DEV - 1)),
                        pltpu.SemaphoreType.DMA((n_all,))],
    )(*fulls)


def _adamw(slots, w, m, v, name):
    shape = w.shape
    cols = shape[-1]
    rows = math.prod(shape[:-1])
    tr = _pick(rows, (64, 32, 16, 8))
    c1 = 1.0 - ADAM_B1 ** ADAM_STEP
    c2 = 1.0 - ADAM_B2 ** ADAM_STEP

    def body(s_ref, w_ref, m_ref, v_ref, g_ref, d_ref, nm_ref, nv_ref):
        g = s_ref[0].astype(F32)
        for k in range(1, N_DEV):
            g = g + s_ref[k].astype(F32)
        nm = ADAM_B1 * m_ref[...] + (1.0 - ADAM_B1) * g
        nv = ADAM_B2 * v_ref[...] + (1.0 - ADAM_B2) * (g * g)
        g_ref[...] = g
        nm_ref[...] = nm
        nv_ref[...] = nv
        d_ref[...] = -ADAM_LR * ((nm / c1) / (jnp.sqrt(nv / c2) + ADAM_EPS) + ADAM_WD * w_ref[...])

    row = pl.BlockSpec((tr, cols), lambda i: (i, 0))
    out = jax.ShapeDtypeStruct((rows, cols), F32)
    outs = pl.pallas_call(
        body, name=name, grid=(rows // tr,),
        in_specs=[pl.BlockSpec((N_DEV, tr, cols), lambda i: (0, i, 0)), row, row, row],
        out_specs=[row] * 4, out_shape=[out] * 4, compiler_params=_cp(("parallel",)),
    )(slots.reshape(N_DEV, rows, cols), w.reshape(rows, cols), m.reshape(rows, cols), v.reshape(rows, cols))
    return [o.reshape(shape) for o in outs]


_SMALL = ("ln_in_g", "ln_in_b", "mem_ln_g", "mem_ln_b", "b_forget", "mla_q_norm_g", "mla_kv_norm_g", "ln_g", "ln_b")
_ORDER = ("ln_in_g", "ln_in_b", "mem_ln_g", "mem_ln_b", "w_in", "b_forget", "mla_q_norm_g", "w_mla_q_up",
          "mla_kv_norm_g", "w_mla_kv_up", "w_mem_kv", "w_out", "ln_g", "ln_b")


def _pack_small(d):
    flat = jnp.concatenate([d[n].reshape(-1) for n in _SMALL])
    n = flat.shape[0]
    padded = ((n + 8 * LANE - 1) // (8 * LANE)) * (8 * LANE)
    return jnp.pad(flat, (0, padded - n)).reshape(-1, LANE)


def _unpack_small(packed, like):
    flat, out, off = packed.reshape(-1), {}, 0
    for n in _SMALL:
        size = math.prod(like[n].shape)
        out[n] = flat[off:off + size].reshape(like[n].shape)
        off += size
    return out


def _unstack_cols(g):
    n, l, r, c = g.shape
    return g.transpose(1, 2, 0, 3).reshape(l, r, n * c)


def _stack_cols(g):
    l, r, nc = g.shape
    return g.reshape(l, r, N_DEV, nc // N_DEV).transpose(2, 0, 1, 3)


def kernel(x, mem, ln_in_g, ln_in_b, mem_ln_g, mem_ln_b, w_in, b_forget, mla_q_norm_g, w_mla_q_up, mla_kv_norm_g, w_mla_kv_up, w_mem_kv, w_out, ln_g, ln_b, loss_target, m_ln_in_g, m_ln_in_b, m_mem_ln_g, m_mem_ln_b, m_w_in, m_b_forget, m_mla_q_norm_g, m_w_mla_q_up, m_mla_kv_norm_g, m_w_mla_kv_up, m_w_mem_kv, m_w_out, m_ln_g, m_ln_b, v_ln_in_g, v_ln_in_b, v_mem_ln_g, v_mem_ln_b, v_w_in, v_b_forget, v_mla_q_norm_g, v_w_mla_q_up, v_mla_kv_norm_g, v_w_mla_kv_up, v_w_mem_kv, v_w_out, v_ln_g, v_ln_b):
    w_shard = dict(ln_in_g=ln_in_g, ln_in_b=ln_in_b, mem_ln_g=mem_ln_g, mem_ln_b=mem_ln_b, w_in=w_in,
                   b_forget=b_forget, mla_q_norm_g=mla_q_norm_g, w_mla_q_up=w_mla_q_up,
                   mla_kv_norm_g=mla_kv_norm_g, w_mla_kv_up=w_mla_kv_up, w_mem_kv=w_mem_kv, w_out=w_out,
                   ln_g=ln_g, ln_b=ln_b)
    m_shard = dict(ln_in_g=m_ln_in_g, ln_in_b=m_ln_in_b, mem_ln_g=m_mem_ln_g, mem_ln_b=m_mem_ln_b, w_in=m_w_in,
                   b_forget=m_b_forget, mla_q_norm_g=m_mla_q_norm_g, w_mla_q_up=m_w_mla_q_up,
                   mla_kv_norm_g=m_mla_kv_norm_g, w_mla_kv_up=m_w_mla_kv_up, w_mem_kv=m_w_mem_kv, w_out=m_w_out,
                   ln_g=m_ln_g, ln_b=m_ln_b)
    v_shard = dict(ln_in_g=v_ln_in_g, ln_in_b=v_ln_in_b, mem_ln_g=v_mem_ln_g, mem_ln_b=v_mem_ln_b, w_in=v_w_in,
                   b_forget=v_b_forget, mla_q_norm_g=v_mla_q_norm_g, w_mla_q_up=v_w_mla_q_up,
                   mla_kv_norm_g=v_mla_kv_norm_g, w_mla_kv_up=v_w_mla_kv_up, w_mem_kv=v_w_mem_kv, w_out=v_w_out,
                   ln_g=v_ln_g, ln_b=v_ln_b)

    to16 = lambda ws: [a.astype(BF16) for a in ws]
    gathered = _all_gather(to16([_permute_cols(w_in), w_mem_kv, w_out]), to16([w_mla_q_up, w_mla_kv_up]))
    g_in, g_mem, g_out, g_qup, g_kvup = [a.astype(F32) for a in gathered]
    full = dict(w_shard)
    full.update(w_in=g_in, w_mem_kv=g_mem, w_out=g_out, w_mla_q_up=_unstack_cols(g_qup),
                w_mla_kv_up=_unstack_cols(g_kvup))

    loss_local, (grad_w, grad_x) = jax.value_and_grad(_trunk_loss, argnums=(0, 1))(
        full, x[0], mem[0], loss_target[0])

    s_in, s_mem, s_out, s_qup, s_kvup, s_small = _reduce_scatter(
        to16([grad_w["w_in"], grad_w["w_mem_kv"], grad_w["w_out"]]),
        to16([_stack_cols(grad_w["w_mla_q_up"]), _stack_cols(grad_w["w_mla_kv_up"])]),
        [_pack_small(grad_w)])

    res = {}
    for name, slots in (("w_mem_kv", s_mem), ("w_out", s_out), ("w_mla_q_up", s_qup), ("w_mla_kv_up", s_kvup)):
        res[name] = _adamw(slots, w_shard[name], m_shard[name], v_shard[name], "adamw_" + name)
    res["w_in"] = [_unpermute_cols(a) for a in _adamw(
        s_in, _permute_cols(w_in), _permute_cols(m_w_in), _permute_cols(v_w_in), "adamw_w_in")]
    small = _adamw(s_small, _pack_small(w_shard), _pack_small(m_shard), _pack_small(v_shard), "adamw_small")
    small = [_unpack_small(a, w_shard) for a in small]
    for name in _SMALL:
        res[name] = [a[name] for a in small]

    loss = lax.psum(loss_local, MESH_AXES)
    outs = [loss, grad_x[None]]
    for k in range(4):
        outs += [res[name][k] for name in _ORDER]
    return tuple(outs)
```
